```python
import math
import jax, jax.numpy as jnp
from jax import lax
import numpy as np

D_MODEL = 2048
BATCH = 8
SEQ = 8192
DEPTH = 1

N_HEADS = 16
QK_NOPE_DIM = 128
ROPE_DIM = 64
V_HEAD_DIM = 128
Q_LORA_RANK = 512
KV_LORA_RANK = 512
MLA_WIDTH = N_HEADS * V_HEAD_DIM
CONV_WIDTH = D_MODEL
CONV_K = 3
ROPE_THETA = 10000.0
RMS_EPS = 1e-6
BLOCK_Q = 128

IN_SPLIT = [
    Q_LORA_RANK,
    KV_LORA_RANK,
    ROPE_DIM,
    MLA_WIDTH,
    CONV_WIDTH,
    CONV_WIDTH,
    CONV_WIDTH,
    CONV_WIDTH,
    D_MODEL,
    D_MODEL,
]
IN_TOTAL = sum(IN_SPLIT)

kernel_name = "hybrid_mla_shortconv_gated_merge"


def _rmsnorm(x, g):
    xf = x.astype(jnp.float32)
    r = lax.rsqrt(jnp.mean(xf * xf, axis=-1, keepdims=True) + RMS_EPS)
    return (xf * r * g.astype(jnp.float32)).astype(x.dtype)


def _rope_tables(positions):
    inv_freq = ROPE_THETA ** (-jnp.arange(0, ROPE_DIM, 2, dtype=jnp.float32) / ROPE_DIM)
    ang = positions.astype(jnp.float32)[..., None] * inv_freq
    return jnp.cos(ang), jnp.sin(ang)


def _apply_rope(x, cos, sin):
    xf = x.astype(jnp.float32)
    x1, x2 = jnp.split(xf, 2, axis=-1)
    out = jnp.concatenate([x1 * cos - x2 * sin, x1 * sin + x2 * cos], axis=-1)
    return out.astype(x.dtype)


def _causal_mla_attention(q_nope, q_rope, k_nope, k_rope, v):
    b, s, h, _ = q_nope.shape
    nb = s // BLOCK_Q
    scale = 1.0 / math.sqrt(QK_NOPE_DIM + ROPE_DIM)
    qn_blocks = q_nope.reshape(b, nb, BLOCK_Q, h, QK_NOPE_DIM).transpose(1, 0, 2, 3, 4)
    qr_blocks = q_rope.reshape(b, nb, BLOCK_Q, h, ROPE_DIM).transpose(1, 0, 2, 3, 4)
    key_pos = jnp.arange(s, dtype=jnp.int32)

    def one_block(args):
        qn, qr, blk = args
        sc = (jnp.einsum('bqhd,bkhd->bhqk', qn, k_nope)
              + jnp.einsum('bqhr,bkr->bhqk', qr, k_rope)).astype(jnp.float32) * scale
        q_pos = blk * BLOCK_Q + jnp.arange(BLOCK_Q, dtype=jnp.int32)
        mask = key_pos[None, :] <= q_pos[:, None]
        sc = jnp.where(mask[None, None], sc, -jnp.inf)
        p = jax.nn.softmax(sc, axis=-1).astype(v.dtype)
        return jnp.einsum('bhqk,bkhd->bqhd', p, v)

    out = lax.map(one_block, (qn_blocks, qr_blocks, jnp.arange(nb, dtype=jnp.int32)))
    return out.transpose(1, 0, 2, 3, 4).reshape(b, s, h, V_HEAD_DIM)


def _fwd_setup_inputs(seed: int = 0) -> dict:
    key = jax.random.key(seed)
    ks = jax.random.split(key, 16)
    f32 = jnp.float32

    def w(k, shape, fan_in):
        return jax.random.normal(k, shape, f32) * (fan_in ** -0.5)

    def gain(k, n):
        return 1.0 + 0.02 * jax.random.normal(k, (n,), f32)

    x = jax.random.normal(ks[0], (BATCH, SEQ, D_MODEL), f32)
    start = jax.random.randint(ks[1], (BATCH, 1), 0, 1024, dtype=jnp.int32)
    positions = start + jnp.arange(SEQ, dtype=jnp.int32)[None, :]
    return {
        "x": x,
        "positions": positions,
        "pre_norm_g": gain(ks[2], D_MODEL),
        "w_in": w(ks[3], (D_MODEL, IN_TOTAL), D_MODEL),
        "q_a_norm_g": gain(ks[4], Q_LORA_RANK),
        "w_q_b": w(ks[5], (Q_LORA_RANK, N_HEADS * (QK_NOPE_DIM + ROPE_DIM)), Q_LORA_RANK),
        "kv_a_norm_g": gain(ks[6], KV_LORA_RANK),
        "w_kv_b": w(ks[7], (KV_LORA_RANK, N_HEADS * (QK_NOPE_DIM + V_HEAD_DIM)), KV_LORA_RANK),
        "conv_w": w(ks[8], (CONV_K, CONV_WIDTH), CONV_K),
        "w_o_mla": w(ks[9], (MLA_WIDTH, D_MODEL), MLA_WIDTH),
        "w_o_conv": w(ks[10], (CONV_WIDTH, D_MODEL), CONV_WIDTH),
        "w_out": w(ks[11], (D_MODEL, D_MODEL), D_MODEL),
        "post_norm_g": gain(ks[12], D_MODEL),
    }


def _fwd_reference(x, positions, pre_norm_g, w_in, q_a_norm_g, w_q_b, kv_a_norm_g, w_kv_b,
              conv_w, w_o_mla, w_o_conv, w_out, post_norm_g):
    b, s, _ = x.shape
    cos, sin = _rope_tables(positions)
    for _layer in range(DEPTH):
        h = _rmsnorm(x, pre_norm_g)
        proj = jnp.einsum('bsd,de->bse', h, w_in)
        cuts = [int(c) for c in np.cumsum(IN_SPLIT)[:-1]]
        (q_a, c_kv, k_rope, z_mla, c_in, b_gate, c_gate, z_conv,
         g_mla, g_conv) = jnp.split(proj, cuts, axis=-1)

        q = jnp.einsum('bsr,re->bse', _rmsnorm(q_a, q_a_norm_g), w_q_b)
        q = q.reshape(b, s, N_HEADS, QK_NOPE_DIM + ROPE_DIM)
        q_nope, q_rope = q[..., :QK_NOPE_DIM], q[..., QK_NOPE_DIM:]
        kv = jnp.einsum('bsr,re->bse', _rmsnorm(c_kv, kv_a_norm_g), w_kv_b)
        kv = kv.reshape(b, s, N_HEADS, QK_NOPE_DIM + V_HEAD_DIM)
        k_nope, v = kv[..., :QK_NOPE_DIM], kv[..., QK_NOPE_DIM:]
        q_rope = _apply_rope(q_rope, cos[:, :, None, :], sin[:, :, None, :])
        k_rope = _apply_rope(k_rope, cos, sin)
        attn = _causal_mla_attention(q_nope, q_rope, k_nope, k_rope, v)
        attn = attn.reshape(b, s, MLA_WIDTH) * jax.nn.silu(z_mla)
        y_mla = jnp.einsum('bse,ed->bsd', attn, w_o_mla)

        u = c_gate * c_in
        u_pad = jnp.pad(u, ((0, 0), (CONV_K - 1, 0), (0, 0)))
        conv = sum(conv_w[k] * u_pad[:, k:k + s] for k in range(CONV_K))
        y_conv = jnp.einsum('bse,ed->bsd', b_gate * conv * jax.nn.silu(z_conv), w_o_conv)

        merged = jax.nn.sigmoid(g_mla) * y_mla + jax.nn.sigmoid(g_conv) * y_conv
        out = jnp.einsum('bsd,de->bse', merged, w_out)
        x = x + _rmsnorm(out, post_norm_g)
    return x


import jax as _jax
import jax.numpy as _jnp

TWIN_FORMAT = 'train_step'
FWD_PARAMS = ['x', 'positions', 'pre_norm_g', 'w_in', 'q_a_norm_g', 'w_q_b', 'kv_a_norm_g', 'w_kv_b', 'conv_w', 'w_o_mla', 'w_o_conv', 'w_out', 'post_norm_g']
TWIN_WEIGHTS = ['pre_norm_g', 'w_in', 'q_a_norm_g', 'w_q_b', 'kv_a_norm_g', 'w_kv_b', 'conv_w', 'w_o_mla', 'w_o_conv', 'w_out', 'post_norm_g']
TWIN_DIFF_INPUT = 'x'
TWIN_INPUTS = ['x', 'positions', 'pre_norm_g', 'w_in', 'q_a_norm_g', 'w_q_b', 'kv_a_norm_g', 'w_kv_b', 'conv_w', 'w_o_mla', 'w_o_conv', 'w_out', 'post_norm_g', 'loss_target', 'm_pre_norm_g', 'm_w_in', 'm_q_a_norm_g', 'm_w_q_b', 'm_kv_a_norm_g', 'm_w_kv_b', 'm_conv_w', 'm_w_o_mla', 'm_w_o_conv', 'm_w_out', 'm_post_norm_g', 'v_pre_norm_g', 'v_w_in', 'v_q_a_norm_g', 'v_w_q_b', 'v_kv_a_norm_g', 'v_w_kv_b', 'v_conv_w', 'v_w_o_mla', 'v_w_o_conv', 'v_w_out', 'v_post_norm_g']
TWIN_OUTPUTS = ['loss', 'grad_x', 'grad_pre_norm_g', 'grad_w_in', 'grad_q_a_norm_g', 'grad_w_q_b', 'grad_kv_a_norm_g', 'grad_w_kv_b', 'grad_conv_w', 'grad_w_o_mla', 'grad_w_o_conv', 'grad_w_out', 'grad_post_norm_g', 'delta_pre_norm_g', 'delta_w_in', 'delta_q_a_norm_g', 'delta_w_q_b', 'delta_kv_a_norm_g', 'delta_w_kv_b', 'delta_conv_w', 'delta_w_o_mla', 'delta_w_o_conv', 'delta_w_out', 'delta_post_norm_g', 'new_m_pre_norm_g', 'new_m_w_in', 'new_m_q_a_norm_g', 'new_m_w_q_b', 'new_m_kv_a_norm_g', 'new_m_w_kv_b', 'new_m_conv_w', 'new_m_w_o_mla', 'new_m_w_o_conv', 'new_m_w_out', 'new_m_post_norm_g', 'new_v_pre_norm_g', 'new_v_w_in', 'new_v_q_a_norm_g', 'new_v_w_q_b', 'new_v_kv_a_norm_g', 'new_v_w_kv_b', 'new_v_conv_w', 'new_v_w_o_mla', 'new_v_w_o_conv', 'new_v_w_out', 'new_v_post_norm_g']
TWIN_LEAF_KINDS = {'loss': 'loss', 'grad_x': 'grad_x', 'grad_pre_norm_g': 'grad_w', 'grad_w_in': 'grad_w', 'grad_q_a_norm_g': 'grad_w', 'grad_w_q_b': 'grad_w', 'grad_kv_a_norm_g': 'grad_w', 'grad_w_kv_b': 'grad_w', 'grad_conv_w': 'grad_w', 'grad_w_o_mla': 'grad_w', 'grad_w_o_conv': 'grad_w', 'grad_w_out': 'grad_w', 'grad_post_norm_g': 'grad_w', 'delta_pre_norm_g': 'delta_w', 'delta_w_in': 'delta_w', 'delta_q_a_norm_g': 'delta_w', 'delta_w_q_b': 'delta_w', 'delta_kv_a_norm_g': 'delta_w', 'delta_w_kv_b': 'delta_w', 'delta_conv_w': 'delta_w', 'delta_w_o_mla': 'delta_w', 'delta_w_o_conv': 'delta_w', 'delta_w_out': 'delta_w', 'delta_post_norm_g': 'delta_w', 'new_m_pre_norm_g': 'new_m', 'new_m_w_in': 'new_m', 'new_m_q_a_norm_g': 'new_m', 'new_m_w_q_b': 'new_m', 'new_m_kv_a_norm_g': 'new_m', 'new_m_w_kv_b': 'new_m', 'new_m_conv_w': 'new_m', 'new_m_w_o_mla': 'new_m', 'new_m_w_o_conv': 'new_m', 'new_m_w_out': 'new_m', 'new_m_post_norm_g': 'new_m', 'new_v_pre_norm_g': 'new_v', 'new_v_w_in': 'new_v', 'new_v_q_a_norm_g': 'new_v', 'new_v_w_q_b': 'new_v', 'new_v_kv_a_norm_g': 'new_v', 'new_v_w_kv_b': 'new_v', 'new_v_conv_w': 'new_v', 'new_v_w_o_mla': 'new_v', 'new_v_w_o_conv': 'new_v', 'new_v_w_out': 'new_v', 'new_v_post_norm_g': 'new_v'}


def _forward(args):
    return _fwd_reference(*[args[k] for k in FWD_PARAMS])


def _output_shape():
    def fwd():
        inp = _fwd_setup_inputs(0)
        return _fwd_reference(*[inp[k] for k in FWD_PARAMS])
    out = _jax.eval_shape(fwd)
    return out.shape, out.dtype

N_MICROBATCH = 1
ADAM_LR = 0.001
ADAM_B1 = 0.9
ADAM_B2 = 0.999
ADAM_EPS = 1e-08
ADAM_WD = 0.01
ADAM_STEP = 10
PER_EXAMPLE_BATCH_AXIS = {'x': 0, 'positions': 0, 'loss_target': 0}
SHARED_INPUTS = []
_WEIGHT_DTYPES = {'pre_norm_g': _jnp.float32, 'w_in': _jnp.float32, 'q_a_norm_g': _jnp.float32, 'w_q_b': _jnp.float32, 'kv_a_norm_g': _jnp.float32, 'w_kv_b': _jnp.float32, 'conv_w': _jnp.float32, 'w_o_mla': _jnp.float32, 'w_o_conv': _jnp.float32, 'w_out': _jnp.float32, 'post_norm_g': _jnp.float32}
MOMENT_SCALE = {'pre_norm_g': 3.672297e-01, 'w_in': 1.345430e-01, 'q_a_norm_g': 6.451472e-02, 'w_q_b': 2.601770e-02, 'kv_a_norm_g': 9.378669e-02, 'w_kv_b': 3.255510e-02, 'conv_w': 1.783501e-01, 'w_o_mla': 3.771904e-02, 'w_o_conv': 1.799075e-01, 'w_out': 1.782227e-01, 'post_norm_g': 3.203590e+01}


def _to_microbatches(a, axis):
    t = _jnp.moveaxis(a, axis, 0)
    t = t.reshape((N_MICROBATCH, t.shape[0] // N_MICROBATCH) + t.shape[1:])
    return _jnp.moveaxis(t, 1, axis + 1)


def setup_inputs(seed: int = 0) -> dict:
    inp = _fwd_setup_inputs(seed)
    key = _jax.random.fold_in(_jax.random.key(seed), 7919)
    shape, _ = _output_shape()
    out = dict(inp)
    out["loss_target"] = _jax.random.normal(_jax.random.fold_in(key, 0), shape, _jnp.float32)
    for i, name in enumerate(TWIN_WEIGHTS):
        w = inp[name].astype(_jnp.float32)
        if MOMENT_SCALE is None:
            s = _jnp.sqrt(_jnp.mean(_jnp.square(w)) + 1e-30)
        else:
            s = MOMENT_SCALE[name]
        km, kv = _jax.random.split(_jax.random.fold_in(key, i + 1))
        out[name] = w
        out["m_" + name] = s * _jax.random.normal(km, w.shape, _jnp.float32)
        out["v_" + name] = (s * s) * _jax.random.uniform(kv, w.shape, _jnp.float32, 0.5, 1.5)
    if N_MICROBATCH > 1:
        for name, axis in PER_EXAMPLE_BATCH_AXIS.items():
            out[name] = _to_microbatches(out[name], axis)
    return {'x': out['x'], 'positions': out['positions'], 'pre_norm_g': out['pre_norm_g'], 'w_in': out['w_in'], 'q_a_norm_g': out['q_a_norm_g'], 'w_q_b': out['w_q_b'], 'kv_a_norm_g': out['kv_a_norm_g'], 'w_kv_b': out['w_kv_b'], 'conv_w': out['conv_w'], 'w_o_mla': out['w_o_mla'], 'w_o_conv': out['w_o_conv'], 'w_out': out['w_out'], 'post_norm_g': out['post_norm_g'], 'loss_target': out['loss_target'], 'm_pre_norm_g': out['m_pre_norm_g'], 'm_w_in': out['m_w_in'], 'm_q_a_norm_g': out['m_q_a_norm_g'], 'm_w_q_b': out['m_w_q_b'], 'm_kv_a_norm_g': out['m_kv_a_norm_g'], 'm_w_kv_b': out['m_w_kv_b'], 'm_conv_w': out['m_conv_w'], 'm_w_o_mla': out['m_w_o_mla'], 'm_w_o_conv': out['m_w_o_conv'], 'm_w_out': out['m_w_out'], 'm_post_norm_g': out['m_post_norm_g'], 'v_pre_norm_g': out['v_pre_norm_g'], 'v_w_in': out['v_w_in'], 'v_q_a_norm_g': out['v_q_a_norm_g'], 'v_w_q_b': out['v_w_q_b'], 'v_kv_a_norm_g': out['v_kv_a_norm_g'], 'v_w_kv_b': out['v_w_kv_b'], 'v_conv_w': out['v_conv_w'], 'v_w_o_mla': out['v_w_o_mla'], 'v_w_o_conv': out['v_w_o_conv'], 'v_w_out': out['v_w_out'], 'v_post_norm_g': out['v_post_norm_g']}


def _loss(weights, diff, rest, loss_target):
    with _jax.named_scope("forward"):
        args = {**rest, TWIN_DIFF_INPUT: diff, **{k: w.astype(_WEIGHT_DTYPES[k]) for k, w in weights.items()}}
        y = _forward(args)
    with _jax.named_scope("loss_head"):
        err = _jnp.square(y.astype(_jnp.float32) - loss_target)
        return 0.5 * _jnp.sum(_jnp.mean(err, axis=-1)) if err.ndim else 0.5 * err


def _adamw(w, g, m, v):
    m = ADAM_B1 * m + (1.0 - ADAM_B1) * g
    v = ADAM_B2 * v + (1.0 - ADAM_B2) * _jnp.square(g)
    m_hat = m / (1.0 - ADAM_B1 ** ADAM_STEP)
    v_hat = v / (1.0 - ADAM_B2 ** ADAM_STEP)
    delta = -ADAM_LR * (m_hat / (_jnp.sqrt(v_hat) + ADAM_EPS) + ADAM_WD * w)
    return delta, m, v


def reference(x, positions, pre_norm_g, w_in, q_a_norm_g, w_q_b, kv_a_norm_g, w_kv_b, conv_w, w_o_mla, w_o_conv, w_out, post_norm_g, loss_target, m_pre_norm_g, m_w_in, m_q_a_norm_g, m_w_q_b, m_kv_a_norm_g, m_w_kv_b, m_conv_w, m_w_o_mla, m_w_o_conv, m_w_out, m_post_norm_g, v_pre_norm_g, v_w_in, v_q_a_norm_g, v_w_q_b, v_kv_a_norm_g, v_w_kv_b, v_conv_w, v_w_o_mla, v_w_o_conv, v_w_out, v_post_norm_g):
    given = dict(x=x, positions=positions, pre_norm_g=pre_norm_g, w_in=w_in, q_a_norm_g=q_a_norm_g, w_q_b=w_q_b, kv_a_norm_g=kv_a_norm_g, w_kv_b=w_kv_b, conv_w=conv_w, w_o_mla=w_o_mla, w_o_conv=w_o_conv, w_out=w_out, post_norm_g=post_norm_g, loss_target=loss_target, m_pre_norm_g=m_pre_norm_g, m_w_in=m_w_in, m_q_a_norm_g=m_q_a_norm_g, m_w_q_b=m_w_q_b, m_kv_a_norm_g=m_kv_a_norm_g, m_w_kv_b=m_w_kv_b, m_conv_w=m_conv_w, m_w_o_mla=m_w_o_mla, m_w_o_conv=m_w_o_conv, m_w_out=m_w_out, m_post_norm_g=m_post_norm_g, v_pre_norm_g=v_pre_norm_g, v_w_in=v_w_in, v_q_a_norm_g=v_q_a_norm_g, v_w_q_b=v_w_q_b, v_kv_a_norm_g=v_kv_a_norm_g, v_w_kv_b=v_w_kv_b, v_conv_w=v_conv_w, v_w_o_mla=v_w_o_mla, v_w_o_conv=v_w_o_conv, v_w_out=v_w_out, v_post_norm_g=v_post_norm_g)
    weights = {n: given[n] for n in TWIN_WEIGHTS}
    shared = {n: given[n] for n in SHARED_INPUTS}
    per_example = {n: given[n] for n in ['x', 'positions']}
    grad_fn = _jax.value_and_grad(_loss, argnums=(0, 1))

    def one_microbatch(ex, loss_target):
        ex = dict(ex)
        diff = ex.pop(TWIN_DIFF_INPUT)
        return grad_fn(weights, diff, {**shared, **ex}, loss_target)

    if N_MICROBATCH == 1:
        loss, (grad_w, grad_x) = one_microbatch(per_example, given["loss_target"])
    else:
        def body(carry, xs):
            loss_sum, grad_sum = carry
            l_k, (gw_k, gx_k) = one_microbatch(xs[0], xs[1])
            with _jax.named_scope("update"):
                return (loss_sum + l_k, _jax.tree.map(_jnp.add, grad_sum, gw_k)), gx_k

        init = (_jnp.zeros((), _jnp.float32), _jax.tree.map(_jnp.zeros_like, weights))
        (loss, grad_w), grad_x = _jax.lax.scan(body, init, (per_example, given["loss_target"]))
    with _jax.named_scope("update"):
        delta_w, new_m, new_v = {}, {}, {}
        for n in TWIN_WEIGHTS:
            delta_w[n], new_m[n], new_v[n] = _adamw(weights[n], grad_w[n], given["m_" + n], given["v_" + n])
    return (loss, grad_x, *[grad_w[n] for n in TWIN_WEIGHTS], *[delta_w[n] for n in TWIN_WEIGHTS],
            *[new_m[n] for n in TWIN_WEIGHTS], *[new_v[n] for n in TWIN_WEIGHTS])
```

```python
import functools
import math

import jax
import jax.numpy as jnp
from jax import lax
from jax.experimental import pallas as pl
from jax.experimental.pallas import tpu as pltpu

F32 = jnp.float32
BF16 = jnp.bfloat16

NDEV = 8
D = 2048
H = 16
DN = 128
DR = 64
DV = 128
RQ = 512
RKV = 512
HW = 256
ROPE_THETA = 10000.0
RMS_EPS = 1e-6
N_IN = 15424
SHARD_IN = N_IN // NDEV
SMALL = RQ + RKV + DR
NP = 7 * D + RQ + RKV + 128
SEG = dict(z_mla=0, c_in=1, b_gate=2, c_gate=3, z_conv=4, g_mla=5, g_conv=6)
OFF_QA = 7 * D
OFF_CKV = OFF_QA + RQ
OFF_KR = OFF_CKV + RKV
EXT = 2176
VMEM_CAP = 56 * 1024 * 1024

ADAM_LR = 0.001
ADAM_B1 = 0.9
ADAM_B2 = 0.999
ADAM_EPS = 1e-08
ADAM_WD = 0.01
ADAM_STEP = 10

NN = (((1,), (0,)), ((), ()))
NT = (((1,), (1,)), ((), ()))
TN = (((0,), (0,)), ((), ()))
MESH = pl.DeviceIdType.MESH


def _cparams(sem, vmem_bytes):
    return pltpu.CompilerParams(dimension_semantics=sem, vmem_limit_bytes=int(min(VMEM_CAP, max(vmem_bytes, 16 << 20))))


def _nbytes(shape, dtype):
    return math.prod(shape) * jnp.dtype(dtype).itemsize


def _matmul(a, b, *, mode, out_dtype, tm, tn, tk, name, m_outer=False):
    if mode == "nn":
        (M, K), (K2, N) = a.shape, b.shape
    elif mode == "nt":
        (M, K), (N, K2) = a.shape, b.shape
    else:
        (K, M), (K2, N) = a.shape, b.shape
    assert K == K2, (a.shape, b.shape, mode)
    tm, tn, tk = min(tm, M), min(tn, N), min(tk, K)
    assert M % tm == 0 and N % tn == 0 and K % tk == 0, (M, N, K, tm, tn, tk)
    ni, nj, nk = M // tm, N // tn, K // tk
    dims = dict(nn=NN, nt=NT, tn=TN)[mode]

    if m_outer:
        grid = (ni, nj, nk)
        ij = lambda g0, g1: (g0, g1)
    else:
        grid = (nj, ni, nk)
        ij = lambda g0, g1: (g1, g0)

    if mode == "tn":
        a_spec = pl.BlockSpec((tk, tm), lambda g0, g1, k: (k, ij(g0, g1)[0]))
        a_tile = (tk, tm)
    else:
        a_spec = pl.BlockSpec((tm, tk), lambda g0, g1, k: (ij(g0, g1)[0], k))
        a_tile = (tm, tk)
    if mode == "nt":
        b_spec = pl.BlockSpec((tn, tk), lambda g0, g1, k: (ij(g0, g1)[1], k))
    else:
        b_spec = pl.BlockSpec((tk, tn), lambda g0, g1, k: (k, ij(g0, g1)[1]))
    o_spec = pl.BlockSpec((tm, tn), lambda g0, g1, k: ij(g0, g1))

    def body(a_ref, b_ref, o_ref, *scratch):
        prod = lax.dot_general(a_ref[...], b_ref[...], dims, preferred_element_type=F32)
        if nk == 1:
            o_ref[...] = prod.astype(o_ref.dtype)
        else:
            acc_ref, = scratch
            k = pl.program_id(2)

            @pl.when(k == 0)
            def _():
                acc_ref[...] = prod

            @pl.when(k > 0)
            def _():
                acc_ref[...] += prod

            @pl.when(k == nk - 1)
            def _():
                o_ref[...] = acc_ref[...].astype(o_ref.dtype)

    vmem = 2 * (_nbytes(a_tile, a.dtype) + _nbytes((tk, tn), b.dtype) + _nbytes((tm, tn), out_dtype)) + 2 * _nbytes((tm, tn), F32)
    return pl.pallas_call(
        body,
        grid=grid,
        in_specs=[a_spec, b_spec],
        out_specs=o_spec,
        out_shape=jax.ShapeDtypeStruct((M, N), out_dtype),
        scratch_shapes=[] if nk == 1 else [pltpu.VMEM((tm, tn), F32)],
        compiler_params=_cparams(("parallel", "parallel", "arbitrary"), vmem + (8 << 20)),
        name=name,
    )(a, b)


def _in_dest(k):
    return SHARD_IN * k - SMALL


def _assemble_w_in(g):
    R = 128

    def body(g_ref, o_ref, ext, acc):
        ext[...] = jnp.zeros_like(ext)
        acc[...] = jnp.zeros_like(acc)
        lane = lax.broadcasted_iota(jnp.int32, (R, EXT), 1)
        ext[:, 0:SHARD_IN] = g_ref[0].astype(F32)
        v = ext[...]
        acc[:, OFF_QA:NP] = jnp.where(lane[:, 0:NP - OFF_QA] < SMALL, v[:, 0:NP - OFF_QA], 0.0)
        w = v[:, 1024:2048]
        w = pltpu.roll(w, 1024 - 64, 1)
        acc[:, 0:1024] = jnp.where(lane[:, 0:1024] < SHARD_IN - SMALL, w, 0.0)
        for k in range(1, NDEV):
            ext[:, 0:SHARD_IN] = g_ref[k].astype(F32)
            dest = _in_dest(k)
            t, o = dest // 128, dest % 128
            width = -(-(o + SHARD_IN) // 128) * 128
            v = pltpu.roll(ext[...], o, 1)[:, 0:width]
            acc[:, 128 * t:128 * t + width] += v
        o_ref[...] = acc[...].astype(BF16)

    return pl.pallas_call(
        body,
        grid=(D // R,),
        in_specs=[pl.BlockSpec((NDEV, R, SHARD_IN), lambda i: (0, i, 0))],
        out_specs=pl.BlockSpec((R, NP), lambda i: (i, 0)),
        out_shape=jax.ShapeDtypeStruct((D, NP), BF16),
        scratch_shapes=[pltpu.VMEM((R, EXT), F32), pltpu.VMEM((R, NP), F32)],
        compiler_params=_cparams(("parallel",), 40 << 20),
        name="assemble_w_in",
    )(g)


def _split_dw_in(dw):
    R = 128

    def body(dw_ref, o_ref):
        lane = lax.broadcasted_iota(jnp.int32, (R, 1024), 1)
        o_ref[0, :, 0:1024] = dw_ref[:, OFF_QA:OFF_QA + 1024]
        tail = dw_ref[:, OFF_QA + 1024:NP]
        tail = jnp.concatenate([tail, jnp.zeros((R, 1024 - 128), F32)], axis=1)
        head = dw_ref[:, 0:1024]
        mixed = jnp.where(lane < 64, tail, pltpu.roll(head, 64, 1))
        o_ref[0, :, 1024:SHARD_IN] = mixed[:, 0:SHARD_IN - 1024]
        for k in range(1, NDEV):
            dest = _in_dest(k)
            t, o = dest // 128, dest % 128
            width = -(-(o + SHARD_IN) // 128) * 128
            v = dw_ref[:, 128 * t:128 * t + width]
            v = pltpu.roll(v, width - o, 1)
            o_ref[k] = v[:, 0:SHARD_IN]

    return pl.pallas_call(
        body,
        grid=(D // R,),
        in_specs=[pl.BlockSpec((R, NP), lambda i: (i, 0))],
        out_specs=pl.BlockSpec((NDEV, R, SHARD_IN), lambda i: (0, i, 0)),
        out_shape=jax.ShapeDtypeStruct((NDEV, D, SHARD_IN), F32),
        compiler_params=_cparams(("parallel",), 40 << 20),
        name="split_dw_in",
    )(dw)


def _assemble_w_q(g):
    def body(g_ref, o_ref):
        lane = lax.broadcasted_iota(jnp.int32, (RQ, 128), 1)
        lo = lane < 64
        for k in range(NDEV):
            t0 = g_ref[k, :, 0:128].astype(F32)
            t1 = g_ref[k, :, 128:256].astype(F32)
            t2 = g_ref[k, :, 256:384].astype(F32)
            base = 2 * k * HW
            o_ref[:, base:base + 128] = t0.astype(BF16)
            o_ref[:, base + 128:base + 256] = jnp.where(lo, t1, 0.0).astype(BF16)
            o_ref[:, base + 256:base + 384] = pltpu.roll(jnp.where(lo, t2, t1), 64, 1).astype(BF16)
            o_ref[:, base + 384:base + 512] = jnp.where(lo, pltpu.roll(t2, 64, 1), 0.0).astype(BF16)

    return pl.pallas_call(
        body,
        out_shape=jax.ShapeDtypeStruct((RQ, H * HW), BF16),
        compiler_params=_cparams(None, 32 << 20),
        name="assemble_w_q",
    )(g)


def _split_dw_q(dw):
    def body(dw_ref, o_ref):
        lane = lax.broadcasted_iota(jnp.int32, (RQ, 128), 1)
        lo = lane < 64
        for k in range(NDEV):
            base = 2 * k * HW
            a = dw_ref[:, base:base + 128]
            b = dw_ref[:, base + 128:base + 256]
            c = pltpu.roll(dw_ref[:, base + 256:base + 384], 64, 1)
            d = pltpu.roll(dw_ref[:, base + 384:base + 512], 64, 1)
            o_ref[k, :, 0:128] = a
            o_ref[k, :, 128:256] = jnp.where(lo, b, c)
            o_ref[k, :, 256:384] = jnp.where(lo, c, d)

    return pl.pallas_call(
        body,
        out_shape=jax.ShapeDtypeStruct((NDEV, RQ, 384), F32),
        compiler_params=_cparams(None, 32 << 20),
        name="split_dw_q",
    )(dw)


def _concat_cols(g, dtype, name):
    n, R, C = g.shape

    def body(g_ref, o_ref):
        for k in range(n):
            o_ref[:, k * C:(k + 1) * C] = g_ref[k].astype(dtype)

    return pl.pallas_call(body, out_shape=jax.ShapeDtypeStruct((R, n * C), dtype), compiler_params=_cparams(None, 32 << 20), name=name)(g)


def _split_cols(x, n, name):
    R, NC = x.shape
    C = NC // n

    def body(x_ref, o_ref):
        for k in range(n):
            o_ref[k] = x_ref[:, k * C:(k + 1) * C]

    return pl.pallas_call(body, out_shape=jax.ShapeDtypeStruct((n, R, C), x.dtype), compiler_params=_cparams(None, 32 << 20), name=name)(x)


def _rms_scale(xf):
    return lax.rsqrt(jnp.mean(xf * xf, axis=-1, keepdims=True) + RMS_EPS)


def _prenorm(x, g, ts):
    S = x.shape[0]

    def body(x_ref, g_ref, h_ref):
        xf = x_ref[...]
        h_ref[...] = (xf * _rms_scale(xf) * g_ref[...]).astype(BF16)

    return pl.pallas_call(
        body,
        grid=(S // ts,),
        in_specs=[pl.BlockSpec((ts, D), lambda i: (i, 0)), pl.BlockSpec((1, D), lambda i: (0, 0))],
        out_specs=pl.BlockSpec((ts, D), lambda i: (i, 0)),
        out_shape=jax.ShapeDtypeStruct((S, D), BF16),
        compiler_params=_cparams(("parallel",), 32 << 20),
        name="prenorm",
    )(x, g)


def _inv_freq_tile():
    inv_freq = ROPE_THETA ** (-jnp.arange(0, DR, 2, dtype=F32) / DR)
    return jnp.tile(inv_freq, 4).reshape(1, 128)


def _rope_tables(pos_ref, freq_ref, ts):
    lane = lax.broadcasted_iota(jnp.int32, (ts, 128), 1)
    ang = pos_ref[...].astype(F32) * freq_ref[...]
    return jnp.cos(ang), jnp.sin(ang), lane


def _rope_swap(t, lane):
    return jnp.where(lane < 32, pltpu.roll(t, 96, 1), pltpu.roll(t, 32, 1))


def _qkv_prep(proj, pos, freq, g_qa, g_kva, w_q, w_kv, ts):
    S = proj.shape[0]

    def body(qa_ref, ckv_ref, kr_ref, pos_ref, freq_ref, gq_ref, gk_ref, wq_ref, wkv_ref, q_ref, k_ref, v_ref, qn_ref, kvn_ref):
        qa = qa_ref[...]
        qn = (qa * _rms_scale(qa) * gq_ref[...]).astype(BF16)
        ckv = ckv_ref[...]
        kvn = (ckv * _rms_scale(ckv) * gk_ref[...]).astype(BF16)
        qn_ref[...] = qn
        kvn_ref[...] = kvn
        cos, sin, lane = _rope_tables(pos_ref, freq_ref, ts)
        sgn_sin = jnp.where(lane < 32, -sin, sin)
        live = lane < DR
        kr = kr_ref[...]
        kr = jnp.where(live, kr * cos + _rope_swap(kr, lane) * sgn_sin, 0.0).astype(BF16)
        qf = jnp.dot(qn, wq_ref[...], preferred_element_type=F32)
        kvf = jnp.dot(kvn, wkv_ref[...], preferred_element_type=F32)
        for h in range(H):
            q_ref[h, :, 0:DN] = qf[:, h * HW:h * HW + DN].astype(BF16)
            t = qf[:, h * HW + DN:(h + 1) * HW]
            q_ref[h, :, DN:HW] = jnp.where(live, t * cos + _rope_swap(t, lane) * sgn_sin, 0.0).astype(BF16)
            k_ref[h, :, 0:DN] = kvf[:, h * HW:h * HW + DN].astype(BF16)
            k_ref[h, :, DN:HW] = kr
            v_ref[h] = kvf[:, h * HW + DN:(h + 1) * HW].astype(BF16)

    row = lambda w, blk: pl.BlockSpec((ts, w), lambda i: (i, blk))
    full = lambda a: pl.BlockSpec(a.shape, lambda i: (0,) * a.ndim)
    return pl.pallas_call(
        body,
        grid=(S // ts,),
        in_specs=[row(RQ, OFF_QA // RQ), row(RKV, OFF_CKV // RKV), row(128, OFF_KR // 128),
                  pl.BlockSpec((ts, 1), lambda i: (i, 0)), full(freq), full(g_qa), full(g_kva), full(w_q), full(w_kv)],
        out_specs=[pl.BlockSpec((H, ts, HW), lambda i: (0, i, 0)), pl.BlockSpec((H, ts, HW), lambda i: (0, i, 0)),
                   pl.BlockSpec((H, ts, DV), lambda i: (0, i, 0)), pl.BlockSpec((ts, RQ), lambda i: (i, 0)),
                   pl.BlockSpec((ts, RKV), lambda i: (i, 0))],
        out_shape=[jax.ShapeDtypeStruct((H, S, HW), BF16), jax.ShapeDtypeStruct((H, S, HW), BF16),
                   jax.ShapeDtypeStruct((H, S, DV), BF16), jax.ShapeDtypeStruct((S, RQ), BF16),
                   jax.ShapeDtypeStruct((S, RKV), BF16)],
        compiler_params=_cparams(("parallel",), 48 << 20),
        name="qkv_prep",
    )(proj, proj, proj, pos, freq, g_qa, g_kva, w_q, w_kv)


def _col_to_row8(col, n):
    return jnp.transpose(jnp.broadcast_to(col, (n, 128)))[0:8, :]


def _flash_fwd(q, k, v, tq):
    S = q.shape[1]
    nq = S // tq
    scale = 1.0 / math.sqrt(DN + DR)

    def body(q_ref, k_ref, v_ref, o_ref, lse_ref, m_sc, l_sc, acc_sc):
        qi, kj = pl.program_id(1), pl.program_id(2)

        @pl.when(kj == 0)
        def _():
            m_sc[...] = jnp.full_like(m_sc, -jnp.inf)
            l_sc[...] = jnp.zeros_like(l_sc)
            acc_sc[...] = jnp.zeros_like(acc_sc)

        def step(diag):
            s = lax.dot_general(q_ref[0], k_ref[0], NT, preferred_element_type=F32) * scale
            if diag:
                r = lax.broadcasted_iota(jnp.int32, (tq, tq), 0)
                c = lax.broadcasted_iota(jnp.int32, (tq, tq), 1)
                s = jnp.where(c <= r, s, -jnp.inf)
            m_prev = m_sc[...]
            m_new = jnp.maximum(m_prev, jnp.max(s, axis=1, keepdims=True))
            alpha = jnp.exp(m_prev - m_new)
            p = jnp.exp(s - m_new)
            l_sc[...] = alpha * l_sc[...] + jnp.sum(p, axis=1, keepdims=True)
            acc_sc[...] = alpha * acc_sc[...] + jnp.dot(p.astype(BF16), v_ref[0], preferred_element_type=F32)
            m_sc[...] = m_new

        @pl.when(kj < qi)
        def _():
            step(False)

        @pl.when(kj == qi)
        def _():
            step(True)
            l = l_sc[...]
            o_ref[...] = acc_sc[...] / l
            lse_ref[0] = _col_to_row8(m_sc[...] + jnp.log(l), tq)

    return pl.pallas_call(
        body,
        grid=(H, nq, nq),
        in_specs=[pl.BlockSpec((1, tq, HW), lambda h, i, j: (h, i, 0)),
                  pl.BlockSpec((1, tq, HW), lambda h, i, j: (h, jnp.minimum(i, j), 0)),
                  pl.BlockSpec((1, tq, DV), lambda h, i, j: (h, jnp.minimum(i, j), 0))],
        out_specs=[pl.BlockSpec((tq, DV), lambda h, i, j: (i, h)), pl.BlockSpec((1, 8, tq), lambda h, i, j: (h, 0, i))],
        out_shape=[jax.ShapeDtypeStruct((S, H * DV), F32), jax.ShapeDtypeStruct((H, 8, S), F32)],
        scratch_shapes=[pltpu.VMEM((tq, 1), F32), pltpu.VMEM((tq, 1), F32), pltpu.VMEM((tq, DV), F32)],
        compiler_params=_cparams(("parallel", "parallel", "arbitrary"), 32 << 20),
        name="flash_fwd",
    )(q, k, v)


def _sigmoid(x):
    return 1.0 / (1.0 + jnp.exp(-x))


def _shift_rows(u, prev8, n, first):
    ts = u.shape[0]
    row = lax.broadcasted_iota(jnp.int32, u.shape, 0)
    out = pltpu.roll(u, n, 0)
    for j in range(n):
        halo = jnp.where(first, 0.0, prev8[8 - n + j:8 - n + j + 1, :])
        out = jnp.where(row == j, halo, out)
    return out


def _gates_fwd(proj, attn, conv_w, ts):
    S = proj.shape[0]

    def body(attn_ref, zm_ref, cin_ref, bg_ref, cg_ref, zc_ref, cin_p, cg_p, w_ref, am_ref, ac_ref):
        first = pl.program_id(0) == 0
        zm = zm_ref[...]
        am_ref[...] = (attn_ref[...] * (zm * _sigmoid(zm))).astype(BF16)
        u = cg_ref[...] * cin_ref[...]
        up = cg_p[...] * cin_p[...]
        w = w_ref[...]
        conv = w[0:1, :] * _shift_rows(u, up, 2, first) + w[1:2, :] * _shift_rows(u, up, 1, first) + w[2:3, :] * u
        zc = zc_ref[...]
        ac_ref[...] = (bg_ref[...] * conv * (zc * _sigmoid(zc))).astype(BF16)

    seg = lambda name: pl.BlockSpec((ts, D), lambda i: (i, SEG[name]))
    prev = lambda name: pl.BlockSpec((8, D), lambda i: (jnp.maximum(i * (ts // 8) - 1, 0), SEG[name]))
    return pl.pallas_call(
        body,
        grid=(S // ts,),
        in_specs=[pl.BlockSpec((ts, D), lambda i: (i, 0)), seg("z_mla"), seg("c_in"), seg("b_gate"), seg("c_gate"), seg("z_conv"),
                  prev("c_in"), prev("c_gate"), pl.BlockSpec((8, D), lambda i: (0, 0))],
        out_specs=[pl.BlockSpec((ts, D), lambda i: (i, 0))] * 2,
        out_shape=[jax.ShapeDtypeStruct((S, D), BF16)] * 2,
        compiler_params=_cparams(("arbitrary",), 48 << 20),
        name="gates_fwd",
    )(attn, proj, proj, proj, proj, proj, proj, proj, conv_w)


def _merge_fwd(proj, y_mla, y_conv, ts):
    S = proj.shape[0]

    def body(gm_ref, gc_ref, ym_ref, yc_ref, o_ref):
        o_ref[...] = (_sigmoid(gm_ref[...]) * ym_ref[...] + _sigmoid(gc_ref[...]) * yc_ref[...]).astype(BF16)

    seg = lambda name: pl.BlockSpec((ts, D), lambda i: (i, SEG[name]))
    row = pl.BlockSpec((ts, D), lambda i: (i, 0))
    return pl.pallas_call(
        body, grid=(S // ts,), in_specs=[seg("g_mla"), seg("g_conv"), row, row], out_specs=row,
        out_shape=jax.ShapeDtypeStruct((S, D), BF16), compiler_params=_cparams(("parallel",), 32 << 20), name="merge_fwd",
    )(proj, proj, y_mla, y_conv)


def _post_loss(out, x, target, g_post, ts):
    S = out.shape[0]

    def body(o_ref, x_ref, t_ref, g_ref, dy_ref, do_ref, dg_ref, loss_ref):
        i = pl.program_id(0)
        o = o_ref[...]
        r = _rms_scale(o)
        n = o * r
        g = g_ref[...]
        err = x_ref[...] + n * g - t_ref[...]
        dy = err * (1.0 / D)
        dy_ref[...] = dy
        dn = dy * g
        do_ref[...] = (r * (dn - n * jnp.mean(dn * n, axis=-1, keepdims=True))).astype(BF16)
        dg = jnp.sum(dy * n, axis=0, keepdims=True)
        part = jnp.sum(jnp.sum(err * err, axis=0, keepdims=True), axis=1, keepdims=True) * (0.5 / D)

        @pl.when(i == 0)
        def _():
            dg_ref[...] = jnp.zeros_like(dg_ref)
            loss_ref[...] = jnp.zeros_like(loss_ref)

        dg_ref[0:1, :] += dg
        loss_ref[...] += jnp.broadcast_to(part, loss_ref.shape)

    row = pl.BlockSpec((ts, D), lambda i: (i, 0))
    return pl.pallas_call(
        body, grid=(S // ts,),
        in_specs=[row, row, row, pl.BlockSpec((1, D), lambda i: (0, 0))],
        out_specs=[row, row, pl.BlockSpec((8, D), lambda i: (0, 0)), pl.BlockSpec((8, 128), lambda i: (0, 0))],
        out_shape=[jax.ShapeDtypeStruct((S, D), F32), jax.ShapeDtypeStruct((S, D), BF16),
                   jax.ShapeDtypeStruct((8, D), F32), jax.ShapeDtypeStruct((8, 128), F32)],
        compiler_params=_cparams(("arbitrary",), 40 << 20), name="post_loss",
    )(out, x, target, g_post)


def _merge_bwd(proj, y_mla, y_conv, dmerged, ts):
    S = proj.shape[0]

    def body(gm_ref, gc_ref, ym_ref, yc_ref, dm_ref, dym_ref, dyc_ref, dgm_ref, dgc_ref):
        dm = dm_ref[...]
        sm = _sigmoid(gm_ref[...])
        sc = _sigmoid(gc_ref[...])
        dym_ref[...] = (dm * sm).astype(BF16)
        dyc_ref[...] = (dm * sc).astype(BF16)
        dgm_ref[...] = (dm * ym_ref[...] * (sm * (1.0 - sm))).astype(BF16)
        dgc_ref[...] = (dm * yc_ref[...] * (sc * (1.0 - sc))).astype(BF16)

    seg = lambda name: pl.BlockSpec((ts, D), lambda i: (i, SEG[name]))
    row = pl.BlockSpec((ts, D), lambda i: (i, 0))
    return pl.pallas_call(
        body, grid=(S // ts,), in_specs=[seg("g_mla"), seg("g_conv"), row, row, row], out_specs=[row] * 4,
        out_shape=[jax.ShapeDtypeStruct((S, D), BF16)] * 4, compiler_params=_cparams(("parallel",), 40 << 20), name="merge_bwd",
    )(proj, proj, y_mla, y_conv, dmerged)


def _gates_bwd(proj, attn, da_mla, da_conv, conv_w, ts):
    S = proj.shape[0]
    nblk = S // ts

    def body(attn_ref, zm_ref, cin_ref, bg_ref, cg_ref, zc_ref, dam_ref, dac_ref, cin_p, cg_p, bg_n, zc_n, dac_n, w_ref,
             dattn_ref, delta_ref, dzm_ref, dcin_ref, dbg_ref, dcg_ref, dzc_ref, dw_ref):
        i = pl.program_id(0)
        first = i == 0
        last = i == nblk - 1
        zm = zm_ref[...]
        sg = _sigmoid(zm)
        silu = zm * sg
        attn = attn_ref[...]
        dam = dam_ref[...]
        dattn = dam * silu
        dattn_ref[...] = dattn.astype(BF16)
        dzm_ref[...] = (dam * attn * (sg * (1.0 + zm * (1.0 - sg)))).astype(BF16)
        prod = dattn * attn
        for h in range(H):
            col = jnp.sum(prod[:, h * DV:(h + 1) * DV], axis=1, keepdims=True)
            delta_ref[h] = _col_to_row8(col, ts)
        w = w_ref[...]
        cin, cg, bg, zc = cin_ref[...], cg_ref[...], bg_ref[...], zc_ref[...]
        u = cg * cin
        up = cg_p[...] * cin_p[...]
        u1 = _shift_rows(u, up, 1, first)
        u2 = _shift_rows(u, up, 2, first)
        conv = w[0:1, :] * u2 + w[1:2, :] * u1 + w[2:3, :] * u
        sgc = _sigmoid(zc)
        siluc = zc * sgc
        dac = dac_ref[...]
        dbg_ref[...] = (dac * conv * siluc).astype(BF16)
        dzc_ref[...] = (dac * bg * conv * (sgc * (1.0 + zc * (1.0 - sgc)))).astype(BF16)
        dconv = dac * bg * siluc
        zn = zc_n[...]
        dconv_n = jnp.where(last, 0.0, dac_n[...] * bg_n[...] * (zn * _sigmoid(zn)))
        row = lax.broadcasted_iota(jnp.int32, dconv.shape, 0)
        d1 = jnp.where(row == ts - 1, dconv_n[0:1, :], pltpu.roll(dconv, ts - 1, 0))
        d2 = jnp.where(row == ts - 1, dconv_n[1:2, :], jnp.where(row == ts - 2, dconv_n[0:1, :], pltpu.roll(dconv, ts - 2, 0)))
        du = w[2:3, :] * dconv + w[1:2, :] * d1 + w[0:1, :] * d2
        dcg_ref[...] = (du * cin).astype(BF16)
        dcin_ref[...] = (du * cg).astype(BF16)

        @pl.when(first)
        def _():
            dw_ref[...] = jnp.zeros_like(dw_ref)

        dw_ref[0:1, :] += jnp.sum(dconv * u2, axis=0, keepdims=True)
        dw_ref[1:2, :] += jnp.sum(dconv * u1, axis=0, keepdims=True)
        dw_ref[2:3, :] += jnp.sum(dconv * u, axis=0, keepdims=True)

    seg = lambda name: pl.BlockSpec((ts, D), lambda i: (i, SEG[name]))
    prev = lambda name: pl.BlockSpec((8, D), lambda i: (jnp.maximum(i * (ts // 8) - 1, 0), SEG[name]))
    nxt = lambda blk: pl.BlockSpec((8, D), lambda i: (jnp.minimum((i + 1) * (ts // 8), S // 8 - 1), blk))
    row = pl.BlockSpec((ts, D), lambda i: (i, 0))
    return pl.pallas_call(
        body, grid=(nblk,),
        in_specs=[row, seg("z_mla"), seg("c_in"), seg("b_gate"), seg("c_gate"), seg("z_conv"), row, row,
                  prev("c_in"), prev("c_gate"), nxt(SEG["b_gate"]), nxt(SEG["z_conv"]), nxt(0), pl.BlockSpec((8, D), lambda i: (0, 0))],
        out_specs=[row, pl.BlockSpec((H, 8, ts), lambda i: (0, 0, i)), row, row, row, row, row, pl.BlockSpec((8, D), lambda i: (0, 0))],
        out_shape=[jax.ShapeDtypeStruct((S, D), BF16), jax.ShapeDtypeStruct((H, 8, S), F32)] + [jax.ShapeDtypeStruct((S, D), BF16)] * 5
        + [jax.ShapeDtypeStruct((8, D), F32)],
        compiler_params=_cparams(("arbitrary",), 56 << 20), name="gates_bwd",
    )(attn, proj, proj, proj, proj, proj, da_mla, da_conv, proj, proj, proj, proj, da_conv, conv_w)


def _flash_bwd(q, k, v, do, lse, delta, tq):
    S = q.shape[1]
    nq = S // tq
    scale = 1.0 / math.sqrt(DN + DR)

    def body(q_ref, k_ref, v_ref, do_ref, lse_ref, dl_ref, dq_ref, dk_ref, dv_ref, dk_sc, dv_sc):
        kj, qi = pl.program_id(1), pl.program_id(2)

        @pl.when((kj == 0) & (qi == 0))
        def _():
            dq_ref[...] = jnp.zeros_like(dq_ref)

        @pl.when(qi == kj)
        def _():
            dk_sc[...] = jnp.zeros_like(dk_sc)
            dv_sc[...] = jnp.zeros_like(dv_sc)

        def step(diag):
            qb, kb, vb, dob = q_ref[0], k_ref[0], v_ref[0], do_ref[...]
            st = lax.dot_general(kb, qb, NT, preferred_element_type=F32) * scale
            pt = jnp.exp(st - lse_ref[0, 0:1, :])
            if diag:
                r = lax.broadcasted_iota(jnp.int32, (tq, tq), 0)
                c = lax.broadcasted_iota(jnp.int32, (tq, tq), 1)
                pt = jnp.where(r <= c, pt, 0.0)
            dv_sc[...] += jnp.dot(pt.astype(BF16), dob, preferred_element_type=F32)
            dpt = lax.dot_general(vb, dob, NT, preferred_element_type=F32)
            dst = (pt * (dpt - dl_ref[0, 0:1, :]) * scale).astype(BF16)
            dk_sc[...] += jnp.dot(dst, qb, preferred_element_type=F32)
            rows = pl.ds(pl.multiple_of(qi * tq, tq), tq)
            dq_ref[0, rows, :] += lax.dot_general(dst, kb, TN, preferred_element_type=F32)

        @pl.when(qi > kj)
        def _():
            step(False)

        @pl.when(qi == kj)
        def _():
            step(True)

        @pl.when(qi == nq - 1)
        def _():
            dk_ref[0] = dk_sc[...]
            dv_ref[0] = dv_sc[...]

    qmap = lambda h, j, i: (h, jnp.maximum(i, j), 0)
    return pl.pallas_call(
        body,
        grid=(H, nq, nq),
        in_specs=[pl.BlockSpec((1, tq, HW), qmap), pl.BlockSpec((1, tq, HW), lambda h, j, i: (h, j, 0)),
                  pl.BlockSpec((1, tq, DV), lambda h, j, i: (h, j, 0)),
                  pl.BlockSpec((tq, DV), lambda h, j, i: (jnp.maximum(i, j), h)),
                  pl.BlockSpec((1, 8, tq), lambda h, j, i: (h, 0, jnp.maximum(i, j))),
                  pl.BlockSpec((1, 8, tq), lambda h, j, i: (h, 0, jnp.maximum(i, j)))],
        out_specs=[pl.BlockSpec((1, S, HW), lambda h, j, i: (h, 0, 0)), pl.BlockSpec((1, tq, HW), lambda h, j, i: (h, j, 0)),
                   pl.BlockSpec((1, tq, DV), lambda h, j, i: (h, j, 0))],
        out_shape=[jax.ShapeDtypeStruct((H, S, HW), F32), jax.ShapeDtypeStruct((H, S, HW), F32), jax.ShapeDtypeStruct((H, S, DV), F32)],
        scratch_shapes=[pltpu.VMEM((tq, HW), F32), pltpu.VMEM((tq, DV), F32)],
        compiler_params=_cparams(("parallel", "arbitrary", "arbitrary"), 48 << 20),
        name="flash_bwd",
    )(q, k, v, do, lse, delta)


def _rms_bwd(xf, g, dn_out):
    r = _rms_scale(xf)
    n = xf * r
    dn = dn_out * g
    dx = r * (dn - n * jnp.mean(dn * n, axis=-1, keepdims=True))
    return dx, jnp.sum(dn_out * n, axis=0, keepdims=True)


def _qkv_bwd(proj, pos, freq, g_qa, g_kva, w_q, w_kv, dq, dk, dv, ts):
    S = proj.shape[0]

    def body(qa_ref, ckv_ref, pos_ref, freq_ref, gq_ref, gk_ref, wq_ref, wkv_ref, dq_ref, dk_ref, dv_ref,
             dqp_ref, dkvp_ref, dqa_ref, dckv_ref, dkr_ref, dgq_ref, dgk_ref):
        i = pl.program_id(0)
        cos, sin, lane = _rope_tables(pos_ref, freq_ref, ts)
        sgn_sin = jnp.where(lane < 32, sin, -sin)
        live = lane < DR
        kr_sum = jnp.zeros((ts, 128), F32)
        for h in range(H):
            dqh = dq_ref[h]
            dqp_ref[:, h * HW:h * HW + DN] = dqh[:, 0:DN].astype(BF16)
            t = dqh[:, DN:HW]
            dqp_ref[:, h * HW + DN:(h + 1) * HW] = jnp.where(live, t * cos + _rope_swap(t, lane) * sgn_sin, 0.0).astype(BF16)
            dkh = dk_ref[h]
            dkvp_ref[:, h * HW:h * HW + DN] = dkh[:, 0:DN].astype(BF16)
            dkvp_ref[:, h * HW + DN:(h + 1) * HW] = dv_ref[h].astype(BF16)
            kr_sum = kr_sum + dkh[:, DN:HW]
        dkr_ref[...] = jnp.where(live, kr_sum * cos + _rope_swap(kr_sum, lane) * sgn_sin, 0.0).astype(BF16)
        dqn = lax.dot_general(dqp_ref[...], wq_ref[...], NT, preferred_element_type=F32)
        dkvn = lax.dot_general(dkvp_ref[...], wkv_ref[...], NT, preferred_element_type=F32)
        dqa, dgq = _rms_bwd(qa_ref[...], gq_ref[...], dqn)
        dckv, dgk = _rms_bwd(ckv_ref[...], gk_ref[...], dkvn)
        dqa_ref[...] = dqa.astype(BF16)
        dckv_ref[...] = dckv.astype(BF16)

        @pl.when(i == 0)
        def _():
            dgq_ref[...] = jnp.zeros_like(dgq_ref)
            dgk_ref[...] = jnp.zeros_like(dgk_ref)

        dgq_ref[0:1, :] += dgq
        dgk_ref[0:1, :] += dgk

    rowb = lambda w, blk: pl.BlockSpec((ts, w), lambda i: (i, blk))
    full = lambda a: pl.BlockSpec(a.shape, lambda i: (0,) * a.ndim)
    heads = lambda w: pl.BlockSpec((H, ts, w), lambda i: (0, i, 0))
    return pl.pallas_call(
        body, grid=(S // ts,),
        in_specs=[rowb(RQ, OFF_QA // RQ), rowb(RKV, OFF_CKV // RKV), pl.BlockSpec((ts, 1), lambda i: (i, 0)), full(freq), full(g_qa), full(g_kva),
                  full(w_q), full(w_kv), heads(HW), heads(HW), heads(DV)],
        out_specs=[rowb(H * HW, 0), rowb(H * HW, 0), rowb(RQ, 0), rowb(RKV, 0), rowb(128, 0),
                   pl.BlockSpec((8, RQ), lambda i: (0, 0)), pl.BlockSpec((8, RKV), lambda i: (0, 0))],
        out_shape=[jax.ShapeDtypeStruct((S, H * HW), BF16), jax.ShapeDtypeStruct((S, H * HW), BF16), jax.ShapeDtypeStruct((S, RQ), BF16),
                   jax.ShapeDtypeStruct((S, RKV), BF16), jax.ShapeDtypeStruct((S, 128), BF16),
                   jax.ShapeDtypeStruct((8, RQ), F32), jax.ShapeDtypeStruct((8, RKV), F32)],
        compiler_params=_cparams(("arbitrary",), 56 << 20), name="qkv_bwd",
    )(proj, proj, pos, freq, g_qa, g_kva, w_q, w_kv, dq, dk, dv)


def _pack_dproj(parts, ts):
    S = parts[0].shape[0]
    widths = [p.shape[1] for p in parts]
    assert sum(widths) == NP

    def body(*refs):
        o_ref = refs[-1]
        off = 0
        for r, w in zip(refs[:-1], widths):
            o_ref[:, off:off + w] = r[...]
            off += w

    return pl.pallas_call(
        body, grid=(S // ts,),
        in_specs=[pl.BlockSpec((ts, w), lambda i: (i, 0)) for w in widths],
        out_specs=pl.BlockSpec((ts, NP), lambda i: (i, 0)),
        out_shape=jax.ShapeDtypeStruct((S, NP), BF16),
        compiler_params=_cparams(("parallel",), 48 << 20), name="pack_dproj",
    )(*parts)


def _prenorm_bwd(x, g, dh, dy, ts):
    S = x.shape[0]

    def body(x_ref, g_ref, dh_ref, dy_ref, gx_ref, dg_ref):
        dx, dg = _rms_bwd(x_ref[...], g_ref[...], dh_ref[...])
        gx_ref[...] = dy_ref[...] + dx

        @pl.when(pl.program_id(0) == 0)
        def _():
            dg_ref[...] = jnp.zeros_like(dg_ref)

        dg_ref[0:1, :] += dg

    row = pl.BlockSpec((ts, D), lambda i: (i, 0))
    return pl.pallas_call(
        body, grid=(S // ts,), in_specs=[row, pl.BlockSpec((1, D), lambda i: (0, 0)), row, row],
        out_specs=[row, pl.BlockSpec((8, D), lambda i: (0, 0))],
        out_shape=[jax.ShapeDtypeStruct((S, D), F32), jax.ShapeDtypeStruct((8, D), F32)],
        compiler_params=_cparams(("arbitrary",), 40 << 20), name="prenorm_bwd",
    )(x, g, dh, dy)


def _local_step(x, pos, target, g_pre, g_qa, g_kva, g_post, w_in, w_q, w_kv, conv_w8, w_o_mla, w_o_conv, w_out):
    S = x.shape[0]
    ts = min(256, S)
    tq = min(512, S)
    tm = min(512, S)
    mm = functools.partial(_matmul, tm=tm)
    freq = _inv_freq_tile()

    h = _prenorm(x, g_pre, ts)
    proj = mm(h, w_in, mode="nn", out_dtype=F32, tn=1408, tk=D, name="mm_proj")
    q, k, v, qn, kvn = _qkv_prep(proj, pos, freq, g_qa, g_kva, w_q, w_kv, ts)
    attn, lse = _flash_fwd(q, k, v, tq)
    a_mla, a_conv = _gates_fwd(proj, attn, conv_w8, ts)
    y_mla = mm(a_mla, w_o_mla, mode="nn", out_dtype=F32, tn=1024, tk=D, name="mm_y_mla")
    y_conv = mm(a_conv, w_o_conv, mode="nn", out_dtype=F32, tn=1024, tk=D, name="mm_y_conv")
    merged = _merge_fwd(proj, y_mla, y_conv, ts)
    out = mm(merged, w_out, mode="nn", out_dtype=F32, tn=1024, tk=D, name="mm_out")
    dy, dout, dg_post, loss = _post_loss(out, x, target, g_post, ts)

    dmerged = mm(dout, w_out, mode="nt", out_dtype=F32, tn=1024, tk=D, name="mm_dmerged")
    dw_out = _matmul(merged, dout, mode="tn", out_dtype=F32, tm=1024, tn=1024, tk=1024, name="mm_dw_out")
    dy_mla, dy_conv, dg_mla, dg_conv = _merge_bwd(proj, y_mla, y_conv, dmerged, ts)
    da_mla = mm(dy_mla, w_o_mla, mode="nt", out_dtype=F32, tn=1024, tk=D, name="mm_da_mla")
    da_conv = mm(dy_conv, w_o_conv, mode="nt", out_dtype=F32, tn=1024, tk=D, name="mm_da_conv")
    dw_o_mla = _matmul(a_mla, dy_mla, mode="tn", out_dtype=F32, tm=1024, tn=1024, tk=1024, name="mm_dw_o_mla")
    dw_o_conv = _matmul(a_conv, dy_conv, mode="tn", out_dtype=F32, tm=1024, tn=1024, tk=1024, name="mm_dw_o_conv")
    dattn, delta, dz_mla, dc_in, db_gate, dc_gate, dz_conv, dconv_w = _gates_bwd(proj, attn, da_mla, da_conv, conv_w8, min(128, S))
    dq, dk, dv = _flash_bwd(q, k, v, dattn, lse, delta, tq)
    dqp, dkvp, dq_a, dc_kv, dk_rope, dg_qa, dg_kva = _qkv_bwd(proj, pos, freq, g_qa, g_kva, w_q, w_kv, dq, dk, dv, min(128, S))
    dw_q = _matmul(qn, dqp, mode="tn", out_dtype=F32, tm=RQ, tn=1024, tk=1024, name="mm_dw_q")
    dw_kv = _matmul(kvn, dkvp, mode="tn", out_dtype=F32, tm=RKV, tn=1024, tk=1024, name="mm_dw_kv")
    dproj = _pack_dproj([dz_mla, dc_in, db_gate, dc_gate, dz_conv, dg_mla, dg_conv, dq_a, dc_kv, dk_rope], ts)
    dh = mm(dproj, w_in, mode="nt", out_dtype=F32, tn=D, tk=1408, name="mm_dh")
    dw_in = _matmul(h, dproj, mode="tn", out_dtype=F32, tm=1024, tn=1408, tk=1024, name="mm_dw_in")
    grad_x, dg_pre = _prenorm_bwd(x, g_pre, dh, dy, ts)
    return dict(loss=loss, grad_x=grad_x, dg_pre=dg_pre, dg_qa=dg_qa, dg_kva=dg_kva, dg_post=dg_post, dw_in=dw_in, dw_q=dw_q,
                dw_kv=dw_kv, dconv_w=dconv_w, dw_o_mla=dw_o_mla, dw_o_conv=dw_o_conv, dw_out=dw_out)


def _my_id():
    return lax.axis_index("x") * 4 + lax.axis_index("y") * 2 + lax.axis_index("c")


def _peer(d):
    p = (_my_id() + d) % NDEV
    return (p // 4, (p // 2) % 2, p % 2), p


def _all_gather(arrays, name):
    n = len(arrays)

    def body(*refs):
        ins, outs = refs[:n], refs[n:2 * n]
        send_sems, recv_sems, local_sems = refs[2 * n:]
        me = _my_id()
        local = [pltpu.make_async_copy(ins[a], outs[a].at[me], local_sems.at[a]) for a in range(n)]
        for cp in local:
            cp.start()
        sends = []
        for d in range(1, NDEV):
            dev, _ = _peer(d)
            for a in range(n):
                cp = pltpu.make_async_remote_copy(src_ref=ins[a], dst_ref=outs[a].at[me], send_sem=send_sems.at[a, d - 1],
                                                  recv_sem=recv_sems.at[a, d - 1], device_id=dev, device_id_type=MESH)
                cp.start()
                sends.append(cp)
        for d in range(1, NDEV):
            _, src = _peer(NDEV - d)
            for a in range(n):
                pltpu.make_async_remote_copy(src_ref=ins[a], dst_ref=outs[a].at[src], send_sem=send_sems.at[a, d - 1],
                                             recv_sem=recv_sems.at[a, d - 1], device_id=_peer(d)[0], device_id_type=MESH).wait_recv()
        for cp in sends:
            cp.wait_send()
        for cp in local:
            cp.wait()

    anyspec = pl.BlockSpec(memory_space=pl.ANY)
    return pl.pallas_call(
        body,
        in_specs=[anyspec] * n,
        out_specs=[anyspec] * n,
        out_shape=[jax.ShapeDtypeStruct((NDEV,) + a.shape, a.dtype) for a in arrays],
        scratch_shapes=[pltpu.SemaphoreType.DMA((n, NDEV - 1)), pltpu.SemaphoreType.DMA((n, NDEV - 1)), pltpu.SemaphoreType.DMA((n,))],
        name=name,
    )(*arrays)


def _scatter_parts(arrays, name):
    n = len(arrays)

    def body(*refs):
        ins, outs = refs[:n], refs[n:2 * n]
        send_sems, recv_sems, local_sems = refs[2 * n:]
        me = _my_id()
        local = [pltpu.make_async_copy(ins[a].at[me], outs[a].at[me], local_sems.at[a]) for a in range(n)]
        for cp in local:
            cp.start()
        sends = []
        for d in range(1, NDEV):
            dev, p = _peer(d)
            for a in range(n):
                cp = pltpu.make_async_remote_copy(src_ref=ins[a].at[p], dst_ref=outs[a].at[me], send_sem=send_sems.at[a, d - 1],
                                                  recv_sem=recv_sems.at[a, d - 1], device_id=dev, device_id_type=MESH)
                cp.start()
                sends.append(cp)
        for d in range(1, NDEV):
            _, src = _peer(NDEV - d)
            for a in range(n):
                pltpu.make_async_remote_copy(src_ref=ins[a].at[src], dst_ref=outs[a].at[src], send_sem=send_sems.at[a, d - 1],
                                             recv_sem=recv_sems.at[a, d - 1], device_id=_peer(d)[0], device_id_type=MESH).wait_recv()
        for cp in sends:
            cp.wait_send()
        for cp in local:
            cp.wait()

    anyspec = pl.BlockSpec(memory_space=pl.ANY)
    return pl.pallas_call(
        body,
        in_specs=[anyspec] * n,
        out_specs=[anyspec] * n,
        out_shape=[jax.ShapeDtypeStruct(a.shape, a.dtype) for a in arrays],
        scratch_shapes=[pltpu.SemaphoreType.DMA((n, NDEV - 1)), pltpu.SemaphoreType.DMA((n, NDEV - 1)), pltpu.SemaphoreType.DMA((n,))],
        name=name,
    )(*arrays)


def _adamw_math(w, g, m, v):
    m = ADAM_B1 * m + (1.0 - ADAM_B1) * g
    v = ADAM_B2 * v + (1.0 - ADAM_B2) * (g * g)
    m_hat = m / (1.0 - ADAM_B1 ** ADAM_STEP)
    v_hat = v / (1.0 - ADAM_B2 ** ADAM_STEP)
    delta = -ADAM_LR * (m_hat / (jnp.sqrt(v_hat) + ADAM_EPS) + ADAM_WD * w)
    return delta, m, v


def _sum_adamw(parts, w, m, v, rows_per_block, name):
    n, R, C = parts.shape
    tr = min(rows_per_block, R)
    assert R % tr == 0

    def body(p_ref, w_ref, m_ref, v_ref, g_ref, d_ref, nm_ref, nv_ref):
        g = p_ref[0]
        for j in range(1, n):
            g = g + p_ref[j]
        g_ref[...] = g
        d, nm, nv = _adamw_math(w_ref[...], g, m_ref[...], v_ref[...])
        d_ref[...] = d
        nm_ref[...] = nm
        nv_ref[...] = nv

    blk = pl.BlockSpec((tr, C), lambda i: (i, 0))
    return pl.pallas_call(
        body, grid=(R // tr,),
        in_specs=[pl.BlockSpec((n, tr, C), lambda i: (0, i, 0)), blk, blk, blk],
        out_specs=[blk] * 4,
        out_shape=[jax.ShapeDtypeStruct((R, C), F32)] * 4,
        compiler_params=_cparams(("parallel",), 48 << 20), name=name,
    )(parts, w, m, v)


def kernel(x, positions, pre_norm_g, w_in, q_a_norm_g, w_q_b, kv_a_norm_g, w_kv_b, conv_w, w_o_mla, w_o_conv, w_out, post_norm_g, loss_target, m_pre_norm_g, m_w_in, m_q_a_norm_g, m_w_q_b, m_kv_a_norm_g, m_w_kv_b, m_conv_w, m_w_o_mla, m_w_o_conv, m_w_out, m_post_norm_g, v_pre_norm_g, v_w_in, v_q_a_norm_g, v_w_q_b, v_kv_a_norm_g, v_w_kv_b, v_conv_w, v_w_o_mla, v_w_o_conv, v_w_out, v_post_norm_g):
    S = x.shape[1]
    conv_pad = jnp.zeros((8, 256), F32).at[0:3, :].set(conv_w)
    g_in, g_q, g_kv, g_om, g_oc, g_out, g_cw = _all_gather(
        [w_in.astype(BF16), w_q_b.astype(BF16), w_kv_b.astype(BF16), w_o_mla.astype(BF16), w_o_conv.astype(BF16), w_out.astype(BF16),
         conv_pad], "all_gather_weights")
    w_in_f = _assemble_w_in(g_in)
    w_q_f = _assemble_w_q(g_q)
    w_kv_f = _concat_cols(g_kv, BF16, "assemble_w_kv")
    conv_w8 = _concat_cols(g_cw, F32, "assemble_conv_w")
    row2 = lambda a: a.reshape(1, -1)
    r = _local_step(x[0], positions.reshape(S, 1), loss_target[0], row2(pre_norm_g), row2(q_a_norm_g), row2(kv_a_norm_g),
                    row2(post_norm_g), w_in_f, w_q_f, w_kv_f, conv_w8, g_om.reshape(D, D), g_oc.reshape(D, D), g_out.reshape(D, D))
    small = jnp.concatenate([r["dg_pre"][0:1], r["dg_post"][0:1], jnp.pad(r["dg_qa"][0:1], ((0, 0), (0, D - RQ))),
                             jnp.pad(r["dg_kva"][0:1], ((0, 0), (0, D - RKV))), jnp.pad(r["loss"][0:1], ((0, 0), (0, D - 128))),
                             jnp.zeros((3, D), F32)], axis=0)
    p_in, p_q, p_kv, p_cw, p_om, p_oc, p_out = _scatter_parts(
        [_split_dw_in(r["dw_in"]), _split_dw_q(r["dw_q"]), _split_cols(r["dw_kv"], NDEV, "split_dw_kv"),
         _split_cols(r["dconv_w"], NDEV, "split_dconv_w"), r["dw_o_mla"].reshape(NDEV, D // NDEV, D),
         r["dw_o_conv"].reshape(NDEV, D // NDEV, D), r["dw_out"].reshape(NDEV, D // NDEV, D)], "scatter_grads")
    p_small, = _all_gather([small], "all_gather_small")
    o_in = _sum_adamw(p_in, w_in, m_w_in, v_w_in, 64, "adamw_w_in")
    o_q = _sum_adamw(p_q, w_q_b, m_w_q_b, v_w_q_b, 512, "adamw_w_q")
    o_kv = _sum_adamw(p_kv, w_kv_b, m_w_kv_b, v_w_kv_b, 512, "adamw_w_kv")
    pad8 = lambda a: jnp.zeros((8, 256), F32).at[0:3, :].set(a)
    o_cw = [a[0:3] for a in _sum_adamw(p_cw, conv_pad, pad8(m_conv_w), pad8(v_conv_w), 8, "adamw_conv_w")]
    o_om = _sum_adamw(p_om, w_o_mla, m_w_o_mla, v_w_o_mla, 64, "adamw_w_o_mla")
    o_oc = _sum_adamw(p_oc, w_o_conv, m_w_o_conv, v_w_o_conv, 64, "adamw_w_o_conv")
    o_out = _sum_adamw(p_out, w_out, m_w_out, v_w_out, 64, "adamw_w_out")
    padv = lambda a: jnp.pad(a.reshape(1, -1), ((0, 0), (0, D - a.shape[0])))
    stack = lambda pre, post, qa, kva: jnp.concatenate([row2(pre), row2(post), padv(qa), padv(kva), jnp.zeros((4, D), F32)], axis=0)
    o_g = _sum_adamw(p_small, stack(pre_norm_g, post_norm_g, q_a_norm_g, kv_a_norm_g),
                     stack(m_pre_norm_g, m_post_norm_g, m_q_a_norm_g, m_kv_a_norm_g),
                     stack(v_pre_norm_g, v_post_norm_g, v_q_a_norm_g, v_kv_a_norm_g), 8, "adamw_gains")
    loss = o_g[0][4, 0]
    outs = {}
    for idx, kind in enumerate(("grad", "delta", "new_m", "new_v")):
        o = o_g[idx]
        outs[kind] = dict(pre_norm_g=o[0], w_in=o_in[idx], q_a_norm_g=o[2, 0:RQ], w_q_b=o_q[idx], kv_a_norm_g=o[3, 0:RKV],
                          w_kv_b=o_kv[idx], conv_w=o_cw[idx], w_o_mla=o_om[idx], w_o_conv=o_oc[idx], w_out=o_out[idx], post_norm_g=o[1])
    names = ["pre_norm_g", "w_in", "q_a_norm_g", "w_q_b", "kv_a_norm_g", "w_kv_b", "conv_w", "w_o_mla", "w_o_conv", "w_out", "post_norm_g"]
    return (loss, r["grad_x"][None], *[outs["grad"][n] for n in names], *[outs["delta"][n] for n in names],
            *[outs["new_m"][n] for n in names], *[outs["new_v"][n] for n in names])
```

```python
import functools
import math

import jax
import jax.numpy as jnp
from jax import lax
from jax.experimental import pallas as pl
from jax.experimental.pallas import tpu as pltpu

F32 = jnp.float32
BF16 = jnp.bfloat16

NDEV = 8
D = 2048
H = 16
DN = 128
DR = 64
DV = 128
RQ = 512
RKV = 512
HW = 256
ROPE_THETA = 10000.0
RMS_EPS = 1e-6
N_IN = 15424
SHARD_IN = N_IN // NDEV
SMALL = RQ + RKV + DR
NP = 7 * D + RQ + RKV + 128
SEG = dict(z_mla=0, c_in=1, b_gate=2, c_gate=3, z_conv=4, g_mla=5, g_conv=6)
OFF_QA = 7 * D
OFF_CKV = OFF_QA + RQ
OFF_KR = OFF_CKV + RKV
EXT = 2176
VMEM_CAP = 56 * 1024 * 1024

ADAM_LR = 0.001
ADAM_B1 = 0.9
ADAM_B2 = 0.999
ADAM_EPS = 1e-08
ADAM_WD = 0.01
ADAM_STEP = 10

LOG2E = math.log2(math.e)
NN = (((1,), (0,)), ((), ()))
NT = (((1,), (1,)), ((), ()))
TN = (((0,), (0,)), ((), ()))
MESH = pl.DeviceIdType.MESH


def _cparams(sem, vmem_bytes):
    return pltpu.CompilerParams(dimension_semantics=sem, vmem_limit_bytes=int(min(VMEM_CAP, max(vmem_bytes, 16 << 20))))


def _nbytes(shape, dtype):
    return math.prod(shape) * jnp.dtype(dtype).itemsize


def _matmul(a, b, *, mode, out_dtype, tm, tn, tk, name, m_outer=False):
    if mode == "nn":
        (M, K), (K2, N) = a.shape, b.shape
    elif mode == "nt":
        (M, K), (N, K2) = a.shape, b.shape
    else:
        (K, M), (K2, N) = a.shape, b.shape
    assert K == K2, (a.shape, b.shape, mode)
    tm, tn, tk = min(tm, M), min(tn, N), min(tk, K)
    assert M % tm == 0 and N % tn == 0 and K % tk == 0, (M, N, K, tm, tn, tk)
    ni, nj, nk = M // tm, N // tn, K // tk
    dims = dict(nn=NN, nt=NT, tn=TN)[mode]

    if m_outer:
        grid = (ni, nj, nk)
        ij = lambda g0, g1: (g0, g1)
    else:
        grid = (nj, ni, nk)
        ij = lambda g0, g1: (g1, g0)

    if mode == "tn":
        a_spec = pl.BlockSpec((tk, tm), lambda g0, g1, k: (k, ij(g0, g1)[0]))
        a_tile = (tk, tm)
    else:
        a_spec = pl.BlockSpec((tm, tk), lambda g0, g1, k: (ij(g0, g1)[0], k))
        a_tile = (tm, tk)
    if mode == "nt":
        b_spec = pl.BlockSpec((tn, tk), lambda g0, g1, k: (ij(g0, g1)[1], k))
    else:
        b_spec = pl.BlockSpec((tk, tn), lambda g0, g1, k: (k, ij(g0, g1)[1]))
    o_spec = pl.BlockSpec((tm, tn), lambda g0, g1, k: ij(g0, g1))

    def body(a_ref, b_ref, o_ref, *scratch):
        prod = lax.dot_general(a_ref[...], b_ref[...], dims, preferred_element_type=F32)
        if nk == 1:
            o_ref[...] = prod.astype(o_ref.dtype)
        else:
            acc_ref, = scratch
            k = pl.program_id(2)

            @pl.when(k == 0)
            def _():
                acc_ref[...] = prod

            @pl.when(k > 0)
            def _():
                acc_ref[...] += prod

            @pl.when(k == nk - 1)
            def _():
                o_ref[...] = acc_ref[...].astype(o_ref.dtype)

    vmem = 2 * (_nbytes(a_tile, a.dtype) + _nbytes((tk, tn), b.dtype) + _nbytes((tm, tn), out_dtype)) + 2 * _nbytes((tm, tn), F32)
    return pl.pallas_call(
        body,
        grid=grid,
        in_specs=[a_spec, b_spec],
        out_specs=o_spec,
        out_shape=jax.ShapeDtypeStruct((M, N), out_dtype),
        scratch_shapes=[] if nk == 1 else [pltpu.VMEM((tm, tn), F32)],
        compiler_params=_cparams(("parallel", "parallel", "arbitrary"), vmem + (8 << 20)),
        name=name,
    )(a, b)


def _in_dest(k):
    return SHARD_IN * k - SMALL


def _assemble_w_in(g):
    R = 128

    def body(g_ref, o_ref, ext, acc):
        ext[...] = jnp.zeros_like(ext)
        acc[...] = jnp.zeros_like(acc)
        lane = lax.broadcasted_iota(jnp.int32, (R, EXT), 1)
        ext[:, 0:SHARD_IN] = g_ref[0].astype(F32)
        v = ext[...]
        acc[:, OFF_QA:NP] = jnp.where(lane[:, 0:NP - OFF_QA] < SMALL, v[:, 0:NP - OFF_QA], 0.0)
        w = v[:, 1024:2048]
        w = pltpu.roll(w, 1024 - 64, 1)
        acc[:, 0:1024] = jnp.where(lane[:, 0:1024] < SHARD_IN - SMALL, w, 0.0)
        for k in range(1, NDEV):
            ext[:, 0:SHARD_IN] = g_ref[k].astype(F32)
            dest = _in_dest(k)
            t, o = dest // 128, dest % 128
            width = -(-(o + SHARD_IN) // 128) * 128
            v = pltpu.roll(ext[...], o, 1)[:, 0:width]
            acc[:, 128 * t:128 * t + width] += v
        o_ref[...] = acc[...].astype(BF16)

    return pl.pallas_call(
        body,
        grid=(D // R,),
        in_specs=[pl.BlockSpec((NDEV, R, SHARD_IN), lambda i: (0, i, 0))],
        out_specs=pl.BlockSpec((R, NP), lambda i: (i, 0)),
        out_shape=jax.ShapeDtypeStruct((D, NP), BF16),
        scratch_shapes=[pltpu.VMEM((R, EXT), F32), pltpu.VMEM((R, NP), F32)],
        compiler_params=_cparams(("parallel",), 40 << 20),
        name="assemble_w_in",
    )(g)


def _split_dw_in(dw):
    R = 128

    def body(dw_ref, o_ref):
        lane = lax.broadcasted_iota(jnp.int32, (R, 1024), 1)
        o_ref[0, :, 0:1024] = dw_ref[:, OFF_QA:OFF_QA + 1024]
        tail = dw_ref[:, OFF_QA + 1024:NP]
        tail = jnp.concatenate([tail, jnp.zeros((R, 1024 - 128), F32)], axis=1)
        head = dw_ref[:, 0:1024]
        mixed = jnp.where(lane < 64, tail, pltpu.roll(head, 64, 1))
        o_ref[0, :, 1024:SHARD_IN] = mixed[:, 0:SHARD_IN - 1024]
        for k in range(1, NDEV):
            dest = _in_dest(k)
            t, o = dest // 128, dest % 128
            width = -(-(o + SHARD_IN) // 128) * 128
            v = dw_ref[:, 128 * t:128 * t + width]
            v = pltpu.roll(v, width - o, 1)
            o_ref[k] = v[:, 0:SHARD_IN]

    return pl.pallas_call(
        body,
        grid=(D // R,),
        in_specs=[pl.BlockSpec((R, NP), lambda i: (i, 0))],
        out_specs=pl.BlockSpec((NDEV, R, SHARD_IN), lambda i: (0, i, 0)),
        out_shape=jax.ShapeDtypeStruct((NDEV, D, SHARD_IN), F32),
        compiler_params=_cparams(("parallel",), 40 << 20),
        name="split_dw_in",
    )(dw)


def _assemble_w_q(g):
    def body(g_ref, o_ref):
        lane = lax.broadcasted_iota(jnp.int32, (RQ, 128), 1)
        lo = lane < 64
        for k in range(NDEV):
            t0 = g_ref[k, :, 0:128].astype(F32)
            t1 = g_ref[k, :, 128:256].astype(F32)
            t2 = g_ref[k, :, 256:384].astype(F32)
            base = 2 * k * HW
            o_ref[:, base:base + 128] = t0.astype(BF16)
            o_ref[:, base + 128:base + 256] = jnp.where(lo, t1, 0.0).astype(BF16)
            o_ref[:, base + 256:base + 384] = pltpu.roll(jnp.where(lo, t2, t1), 64, 1).astype(BF16)
            o_ref[:, base + 384:base + 512] = jnp.where(lo, pltpu.roll(t2, 64, 1), 0.0).astype(BF16)

    return pl.pallas_call(
        body,
        out_shape=jax.ShapeDtypeStruct((RQ, H * HW), BF16),
        compiler_params=_cparams(None, 32 << 20),
        name="assemble_w_q",
    )(g)


def _split_dw_q(dw):
    def body(dw_ref, o_ref):
        lane = lax.broadcasted_iota(jnp.int32, (RQ, 128), 1)
        lo = lane < 64
        for k in range(NDEV):
            base = 2 * k * HW
            a = dw_ref[:, base:base + 128]
            b = dw_ref[:, base + 128:base + 256]
            c = pltpu.roll(dw_ref[:, base + 256:base + 384], 64, 1)
            d = pltpu.roll(dw_ref[:, base + 384:base + 512], 64, 1)
            o_ref[k, :, 0:128] = a
            o_ref[k, :, 128:256] = jnp.where(lo, b, c)
            o_ref[k, :, 256:384] = jnp.where(lo, c, d)

    return pl.pallas_call(
        body,
        out_shape=jax.ShapeDtypeStruct((NDEV, RQ, 384), F32),
        compiler_params=_cparams(None, 32 << 20),
        name="split_dw_q",
    )(dw)


def _concat_cols(g, dtype, name):
    n, R, C = g.shape

    def body(g_ref, o_ref):
        for k in range(n):
            o_ref[:, k * C:(k + 1) * C] = g_ref[k].astype(dtype)

    return pl.pallas_call(body, out_shape=jax.ShapeDtypeStruct((R, n * C), dtype), compiler_params=_cparams(None, 32 << 20), name=name)(g)


def _split_cols(x, n, name):
    R, NC = x.shape
    C = NC // n

    def body(x_ref, o_ref):
        for k in range(n):
            o_ref[k] = x_ref[:, k * C:(k + 1) * C]

    return pl.pallas_call(body, out_shape=jax.ShapeDtypeStruct((n, R, C), x.dtype), compiler_params=_cparams(None, 32 << 20), name=name)(x)


def _rms_scale(xf):
    return lax.rsqrt(jnp.mean(xf * xf, axis=-1, keepdims=True) + RMS_EPS)


def _prenorm(x, g, ts):
    S = x.shape[0]

    def body(x_ref, g_ref, h_ref):
        xf = x_ref[...]
        h_ref[...] = (xf * _rms_scale(xf) * g_ref[...]).astype(BF16)

    return pl.pallas_call(
        body,
        grid=(S // ts,),
        in_specs=[pl.BlockSpec((ts, D), lambda i: (i, 0)), pl.BlockSpec((1, D), lambda i: (0, 0))],
        out_specs=pl.BlockSpec((ts, D), lambda i: (i, 0)),
        out_shape=jax.ShapeDtypeStruct((S, D), BF16),
        compiler_params=_cparams(("parallel",), 32 << 20),
        name="prenorm",
    )(x, g)


def _inv_freq_tile():
    inv_freq = ROPE_THETA ** (-jnp.arange(0, DR, 2, dtype=F32) / DR)
    return jnp.tile(inv_freq, 4).reshape(1, 128)


def _rope_tables(pos_ref, freq_ref, ts):
    lane = lax.broadcasted_iota(jnp.int32, (ts, 128), 1)
    ang = pos_ref[...].astype(F32) * freq_ref[...]
    return jnp.cos(ang), jnp.sin(ang), lane


def _rope_swap(t, lane):
    return jnp.where(lane < 32, pltpu.roll(t, 96, 1), pltpu.roll(t, 32, 1))


def _qkv_prep(proj, pos, freq, g_qa, g_kva, w_q, w_kv, ts):
    S = proj.shape[0]

    def body(qa_ref, ckv_ref, kr_ref, pos_ref, freq_ref, gq_ref, gk_ref, wq_ref, wkv_ref, q_ref, k_ref, v_ref, vt_ref, qn_ref, kvn_ref):
        qa = qa_ref[...]
        qn = (qa * _rms_scale(qa) * gq_ref[...]).astype(BF16)
        ckv = ckv_ref[...]
        kvn = (ckv * _rms_scale(ckv) * gk_ref[...]).astype(BF16)
        qn_ref[...] = qn
        kvn_ref[...] = kvn
        cos, sin, lane = _rope_tables(pos_ref, freq_ref, ts)
        sgn_sin = jnp.where(lane < 32, -sin, sin)
        live = lane < DR
        kr = kr_ref[...]
        kr = jnp.where(live, kr * cos + _rope_swap(kr, lane) * sgn_sin, 0.0).astype(BF16)
        qf = jnp.dot(qn, wq_ref[...], preferred_element_type=F32)
        kvf = jnp.dot(kvn, wkv_ref[...], preferred_element_type=F32)
        for h in range(H):
            q_ref[h, :, 0:DN] = qf[:, h * HW:h * HW + DN].astype(BF16)
            t = qf[:, h * HW + DN:(h + 1) * HW]
            q_ref[h, :, DN:HW] = jnp.where(live, t * cos + _rope_swap(t, lane) * sgn_sin, 0.0).astype(BF16)
            k_ref[h, :, 0:DN] = kvf[:, h * HW:h * HW + DN].astype(BF16)
            k_ref[h, :, DN:HW] = kr
            vh = kvf[:, h * HW + DN:(h + 1) * HW]
            v_ref[h] = vh.astype(BF16)
            vt_ref[h] = jnp.transpose(vh).astype(BF16)

    row = lambda w, blk: pl.BlockSpec((ts, w), lambda i: (i, blk))
    full = lambda a: pl.BlockSpec(a.shape, lambda i: (0,) * a.ndim)
    return pl.pallas_call(
        body,
        grid=(S // ts,),
        in_specs=[row(RQ, OFF_QA // RQ), row(RKV, OFF_CKV // RKV), row(128, OFF_KR // 128),
                  pl.BlockSpec((ts, 1), lambda i: (i, 0)), full(freq), full(g_qa), full(g_kva), full(w_q), full(w_kv)],
        out_specs=[pl.BlockSpec((H, ts, HW), lambda i: (0, i, 0)), pl.BlockSpec((H, ts, HW), lambda i: (0, i, 0)),
                   pl.BlockSpec((H, ts, DV), lambda i: (0, i, 0)), pl.BlockSpec((H, DV, ts), lambda i: (0, 0, i)),
                   pl.BlockSpec((ts, RQ), lambda i: (i, 0)), pl.BlockSpec((ts, RKV), lambda i: (i, 0))],
        out_shape=[jax.ShapeDtypeStruct((H, S, HW), BF16), jax.ShapeDtypeStruct((H, S, HW), BF16),
                   jax.ShapeDtypeStruct((H, S, DV), BF16), jax.ShapeDtypeStruct((H, DV, S), BF16),
                   jax.ShapeDtypeStruct((S, RQ), BF16), jax.ShapeDtypeStruct((S, RKV), BF16)],
        compiler_params=_cparams(("parallel",), 48 << 20),
        name="qkv_prep",
    )(proj, proj, proj, pos, freq, g_qa, g_kva, w_q, w_kv)


def _col_to_row8(col, n):
    return jnp.transpose(jnp.broadcast_to(col, (n, 128)))[0:8, :]


def _flash_fwd(q, k, vt, tq, hb=8):
    S = q.shape[1]
    nq = S // tq
    scale = 1.0 / math.sqrt(DN + DR)
    c2 = scale * math.log2(math.e)
    tsub = tq
    nsub = tq // tsub

    def body(q_ref, k_ref, vt_ref, o_ref, lse_ref, m_sc, l_sc, acc_sc):
        qi, kj = pl.program_id(1), pl.program_id(2)

        @pl.when(kj == 0)
        def _():
            m_sc[...] = jnp.full_like(m_sc, -jnp.inf)
            l_sc[...] = jnp.zeros_like(l_sc)
            acc_sc[...] = jnp.zeros_like(acc_sc)

        def step(diag):
            chains = [(h, u) for h in range(hb) for u in range(nsub)]

            def scores(h, u):
                return lax.dot_general(k_ref[h], q_ref[h, u * tsub:(u + 1) * tsub, :], NT, preferred_element_type=F32)

            st_next = scores(*chains[0])
            for ci, (h, u) in enumerate(chains):
                st = st_next
                if ci + 1 < len(chains):
                    st_next = scores(*chains[ci + 1])
                cols = slice(u * tsub, (u + 1) * tsub)
                if diag:
                    r = lax.broadcasted_iota(jnp.int32, (tq, tsub), 0)
                    c = lax.broadcasted_iota(jnp.int32, (tq, tsub), 1) + u * tsub
                    st = jnp.where(r <= c, st, -jnp.inf)
                m_prev = m_sc[h, 0:1, cols]
                m_new = jnp.maximum(m_prev, jnp.max(st, axis=0, keepdims=True))
                alpha = jnp.exp2((m_prev - m_new) * c2)
                pt = jnp.exp2((st - m_new) * c2)
                l_sc[h, :, cols] = jnp.broadcast_to(alpha * l_sc[h, 0:1, cols] + jnp.sum(pt, axis=0, keepdims=True), (8, tsub))
                m_sc[h, :, cols] = jnp.broadcast_to(m_new, (8, tsub))
                acc_sc[h, :, cols] = alpha * acc_sc[h, :, cols] + jnp.dot(vt_ref[h], pt.astype(BF16), preferred_element_type=F32)

        @pl.when(kj < qi)
        def _():
            step(False)

        @pl.when(kj == qi)
        def _():
            step(True)
            for h in range(hb):
                l = l_sc[h, 0:1, :]
                o_ref[:, h * DV:(h + 1) * DV] = jnp.transpose(acc_sc[h] / l)
                lse_ref[h] = m_sc[h] * scale + jnp.log(l_sc[h])

    kmap = lambda h, i, j: (h, jnp.minimum(i, j), 0)
    return pl.pallas_call(
        body,
        grid=(H // hb, nq, nq),
        in_specs=[pl.BlockSpec((hb, tq, HW), lambda h, i, j: (h, i, 0)), pl.BlockSpec((hb, tq, HW), kmap),
                  pl.BlockSpec((hb, DV, tq), lambda h, i, j: (h, 0, jnp.minimum(i, j)))],
        out_specs=[pl.BlockSpec((tq, hb * DV), lambda h, i, j: (i, h)), pl.BlockSpec((hb, 8, tq), lambda h, i, j: (h, 0, i))],
        out_shape=[jax.ShapeDtypeStruct((S, H * DV), F32), jax.ShapeDtypeStruct((H, 8, S), F32)],
        scratch_shapes=[pltpu.VMEM((hb, 8, tq), F32), pltpu.VMEM((hb, 8, tq), F32), pltpu.VMEM((hb, DV, tq), F32)],
        compiler_params=_cparams(("parallel", "parallel", "arbitrary"), 40 << 20),
        name="flash_fwd",
    )(q, k, vt)


def _sigmoid(x):
    return 1.0 / (1.0 + jnp.exp(-x))


def _shift_rows(u, prev8, n, first):
    ts = u.shape[0]
    row = lax.broadcasted_iota(jnp.int32, u.shape, 0)
    out = pltpu.roll(u, n, 0)
    for j in range(n):
        halo = jnp.where(first, 0.0, prev8[8 - n + j:8 - n + j + 1, :])
        out = jnp.where(row == j, halo, out)
    return out


def _gates_fwd(proj, attn, conv_w, ts):
    S = proj.shape[0]

    def body(attn_ref, zm_ref, cin_ref, bg_ref, cg_ref, zc_ref, cin_p, cg_p, w_ref, am_ref, ac_ref):
        first = pl.program_id(0) == 0
        zm = zm_ref[...]
        am_ref[...] = (attn_ref[...] * (zm * _sigmoid(zm))).astype(BF16)
        u = cg_ref[...] * cin_ref[...]
        up = cg_p[...] * cin_p[...]
        w = w_ref[...]
        conv = w[0:1, :] * _shift_rows(u, up, 2, first) + w[1:2, :] * _shift_rows(u, up, 1, first) + w[2:3, :] * u
        zc = zc_ref[...]
        ac_ref[...] = (bg_ref[...] * conv * (zc * _sigmoid(zc))).astype(BF16)

    seg = lambda name: pl.BlockSpec((ts, D), lambda i: (i, SEG[name]))
    prev = lambda name: pl.BlockSpec((8, D), lambda i: (jnp.maximum(i * (ts // 8) - 1, 0), SEG[name]))
    return pl.pallas_call(
        body,
        grid=(S // ts,),
        in_specs=[pl.BlockSpec((ts, D), lambda i: (i, 0)), seg("z_mla"), seg("c_in"), seg("b_gate"), seg("c_gate"), seg("z_conv"),
                  prev("c_in"), prev("c_gate"), pl.BlockSpec((8, D), lambda i: (0, 0))],
        out_specs=[pl.BlockSpec((ts, D), lambda i: (i, 0))] * 2,
        out_shape=[jax.ShapeDtypeStruct((S, D), BF16)] * 2,
        compiler_params=_cparams(("arbitrary",), 48 << 20),
        name="gates_fwd",
    )(attn, proj, proj, proj, proj, proj, proj, proj, conv_w)


def _merge_fwd(proj, y_mla, y_conv, ts):
    S = proj.shape[0]

    def body(gm_ref, gc_ref, ym_ref, yc_ref, o_ref):
        o_ref[...] = (_sigmoid(gm_ref[...]) * ym_ref[...] + _sigmoid(gc_ref[...]) * yc_ref[...]).astype(BF16)

    seg = lambda name: pl.BlockSpec((ts, D), lambda i: (i, SEG[name]))
    row = pl.BlockSpec((ts, D), lambda i: (i, 0))
    return pl.pallas_call(
        body, grid=(S // ts,), in_specs=[seg("g_mla"), seg("g_conv"), row, row], out_specs=row,
        out_shape=jax.ShapeDtypeStruct((S, D), BF16), compiler_params=_cparams(("parallel",), 32 << 20), name="merge_fwd",
    )(proj, proj, y_mla, y_conv)


def _post_loss(out, x, target, g_post, ts):
    S = out.shape[0]

    def body(o_ref, x_ref, t_ref, g_ref, dy_ref, do_ref, dg_ref, loss_ref):
        i = pl.program_id(0)
        o = o_ref[...]
        r = _rms_scale(o)
        n = o * r
        g = g_ref[...]
        err = x_ref[...] + n * g - t_ref[...]
        dy = err * (1.0 / D)
        dy_ref[...] = dy
        dn = dy * g
        do_ref[...] = (r * (dn - n * jnp.mean(dn * n, axis=-1, keepdims=True))).astype(BF16)
        dg = jnp.sum(dy * n, axis=0, keepdims=True)
        part = jnp.sum(jnp.sum(err * err, axis=0, keepdims=True), axis=1, keepdims=True) * (0.5 / D)

        @pl.when(i == 0)
        def _():
            dg_ref[...] = jnp.zeros_like(dg_ref)
            loss_ref[...] = jnp.zeros_like(loss_ref)

        dg_ref[0:1, :] += dg
        loss_ref[...] += jnp.broadcast_to(part, loss_ref.shape)

    row = pl.BlockSpec((ts, D), lambda i: (i, 0))
    return pl.pallas_call(
        body, grid=(S // ts,),
        in_specs=[row, row, row, pl.BlockSpec((1, D), lambda i: (0, 0))],
        out_specs=[row, row, pl.BlockSpec((8, D), lambda i: (0, 0)), pl.BlockSpec((8, 128), lambda i: (0, 0))],
        out_shape=[jax.ShapeDtypeStruct((S, D), F32), jax.ShapeDtypeStruct((S, D), BF16),
                   jax.ShapeDtypeStruct((8, D), F32), jax.ShapeDtypeStruct((8, 128), F32)],
        compiler_params=_cparams(("arbitrary",), 40 << 20), name="post_loss",
    )(out, x, target, g_post)


def _merge_bwd(proj, y_mla, y_conv, dmerged, ts):
    S = proj.shape[0]

    def body(gm_ref, gc_ref, ym_ref, yc_ref, dm_ref, dym_ref, dyc_ref, dgm_ref, dgc_ref):
        dm = dm_ref[...]
        sm = _sigmoid(gm_ref[...])
        sc = _sigmoid(gc_ref[...])
        dym_ref[...] = (dm * sm).astype(BF16)
        dyc_ref[...] = (dm * sc).astype(BF16)
        dgm_ref[...] = (dm * ym_ref[...] * (sm * (1.0 - sm))).astype(BF16)
        dgc_ref[...] = (dm * yc_ref[...] * (sc * (1.0 - sc))).astype(BF16)

    seg = lambda name: pl.BlockSpec((ts, D), lambda i: (i, SEG[name]))
    row = pl.BlockSpec((ts, D), lambda i: (i, 0))
    return pl.pallas_call(
        body, grid=(S // ts,), in_specs=[seg("g_mla"), seg("g_conv"), row, row, row], out_specs=[row] * 4,
        out_shape=[jax.ShapeDtypeStruct((S, D), BF16)] * 4, compiler_params=_cparams(("parallel",), 40 << 20), name="merge_bwd",
    )(proj, proj, y_mla, y_conv, dmerged)


def _gates_bwd(proj, attn, da_mla, da_conv, conv_w, ts):
    S = proj.shape[0]
    nblk = S // ts

    def body(attn_ref, zm_ref, cin_ref, bg_ref, cg_ref, zc_ref, dam_ref, dac_ref, cin_p, cg_p, bg_n, zc_n, dac_n, w_ref,
             dattn_ref, delta_ref, dzm_ref, dcin_ref, dbg_ref, dcg_ref, dzc_ref, dw_ref):
        i = pl.program_id(0)
        first = i == 0
        last = i == nblk - 1
        zm = zm_ref[...]
        sg = _sigmoid(zm)
        silu = zm * sg
        attn = attn_ref[...]
        dam = dam_ref[...]
        dattn = dam * silu
        dattn_ref[...] = dattn.astype(BF16)
        dzm_ref[...] = (dam * attn * (sg * (1.0 + zm * (1.0 - sg)))).astype(BF16)
        prod = dattn * attn
        for h in range(H):
            col = jnp.sum(prod[:, h * DV:(h + 1) * DV], axis=1, keepdims=True)
            delta_ref[h] = _col_to_row8(col, ts)
        w = w_ref[...]
        cin, cg, bg, zc = cin_ref[...], cg_ref[...], bg_ref[...], zc_ref[...]
        u = cg * cin
        up = cg_p[...] * cin_p[...]
        u1 = _shift_rows(u, up, 1, first)
        u2 = _shift_rows(u, up, 2, first)
        conv = w[0:1, :] * u2 + w[1:2, :] * u1 + w[2:3, :] * u
        sgc = _sigmoid(zc)
        siluc = zc * sgc
        dac = dac_ref[...]
        dbg_ref[...] = (dac * conv * siluc).astype(BF16)
        dzc_ref[...] = (dac * bg * conv * (sgc * (1.0 + zc * (1.0 - sgc)))).astype(BF16)
        dconv = dac * bg * siluc
        zn = zc_n[...]
        dconv_n = jnp.where(last, 0.0, dac_n[...] * bg_n[...] * (zn * _sigmoid(zn)))
        row = lax.broadcasted_iota(jnp.int32, dconv.shape, 0)
        d1 = jnp.where(row == ts - 1, dconv_n[0:1, :], pltpu.roll(dconv, ts - 1, 0))
        d2 = jnp.where(row == ts - 1, dconv_n[1:2, :], jnp.where(row == ts - 2, dconv_n[0:1, :], pltpu.roll(dconv, ts - 2, 0)))
        du = w[2:3, :] * dconv + w[1:2, :] * d1 + w[0:1, :] * d2
        dcg_ref[...] = (du * cin).astype(BF16)
        dcin_ref[...] = (du * cg).astype(BF16)

        @pl.when(first)
        def _():
            dw_ref[...] = jnp.zeros_like(dw_ref)

        dw_ref[0:1, :] += jnp.sum(dconv * u2, axis=0, keepdims=True)
        dw_ref[1:2, :] += jnp.sum(dconv * u1, axis=0, keepdims=True)
        dw_ref[2:3, :] += jnp.sum(dconv * u, axis=0, keepdims=True)

    seg = lambda name: pl.BlockSpec((ts, D), lambda i: (i, SEG[name]))
    prev = lambda name: pl.BlockSpec((8, D), lambda i: (jnp.maximum(i * (ts // 8) - 1, 0), SEG[name]))
    nxt = lambda blk: pl.BlockSpec((8, D), lambda i: (jnp.minimum((i + 1) * (ts // 8), S // 8 - 1), blk))
    row = pl.BlockSpec((ts, D), lambda i: (i, 0))
    return pl.pallas_call(
        body, grid=(nblk,),
        in_specs=[row, seg("z_mla"), seg("c_in"), seg("b_gate"), seg("c_gate"), seg("z_conv"), row, row,
                  prev("c_in"), prev("c_gate"), nxt(SEG["b_gate"]), nxt(SEG["z_conv"]), nxt(0), pl.BlockSpec((8, D), lambda i: (0, 0))],
        out_specs=[row, pl.BlockSpec((H, 8, ts), lambda i: (0, 0, i)), row, row, row, row, row, pl.BlockSpec((8, D), lambda i: (0, 0))],
        out_shape=[jax.ShapeDtypeStruct((S, D), BF16), jax.ShapeDtypeStruct((H, 8, S), F32)] + [jax.ShapeDtypeStruct((S, D), BF16)] * 5
        + [jax.ShapeDtypeStruct((8, D), F32)],
        compiler_params=_cparams(("arbitrary",), 56 << 20), name="gates_bwd",
    )(attn, proj, proj, proj, proj, proj, da_mla, da_conv, proj, proj, proj, proj, da_conv, conv_w)


def _flash_bwd(q, k, v, do, lse, delta, tq):
    S = q.shape[1]
    nq = S // tq
    scale = 1.0 / math.sqrt(DN + DR)
    c2 = scale * LOG2E
    tsub = min(256, tq)
    nsub = tq // tsub

    def body(q_ref, k_ref, v_ref, do_ref, lse_ref, dl_ref, dq_ref, dk_ref, dv_ref, dk_sc, dv_sc):
        kj, qi = pl.program_id(1), pl.program_id(2)

        @pl.when((kj == 0) & (qi == 0))
        def _():
            dq_ref[...] = jnp.zeros_like(dq_ref)

        @pl.when(qi == kj)
        def _():
            dk_sc[...] = jnp.zeros_like(dk_sc)
            dv_sc[...] = jnp.zeros_like(dv_sc)

        def step(diag):
            kb, vb = k_ref[0], v_ref[0]

            def first_matmuls(u):
                sub = slice(u * tsub, (u + 1) * tsub)
                st = lax.dot_general(kb, q_ref[0, sub, :], NT, preferred_element_type=F32)
                dpt = lax.dot_general(vb, do_ref[sub, :], NT, preferred_element_type=F32)
                return st, dpt

            nxt = first_matmuls(0)
            for u in range(nsub):
                st, dpt = nxt
                if u + 1 < nsub:
                    nxt = first_matmuls(u + 1)
                sub = slice(u * tsub, (u + 1) * tsub)
                pt = jnp.exp2(st * c2 - lse_ref[0, 0:1, sub] * LOG2E)
                if diag:
                    r = lax.broadcasted_iota(jnp.int32, (tq, tsub), 0)
                    c = lax.broadcasted_iota(jnp.int32, (tq, tsub), 1) + u * tsub
                    pt = jnp.where(r <= c, pt, 0.0)
                dst = (pt * (dpt - dl_ref[0, 0:1, sub])).astype(BF16)
                dv_sc[...] += jnp.dot(pt.astype(BF16), do_ref[sub, :], preferred_element_type=F32)
                dk_sc[...] += jnp.dot(dst, q_ref[0, sub, :], preferred_element_type=F32)
                rows = pl.ds(pl.multiple_of(qi * tq + u * tsub, tsub), tsub)
                dq_ref[0, rows, :] += lax.dot_general(dst, kb, TN, preferred_element_type=F32)

        @pl.when(qi > kj)
        def _():
            step(False)

        @pl.when(qi == kj)
        def _():
            step(True)

        @pl.when(qi == nq - 1)
        def _():
            dk_ref[0] = dk_sc[...] * scale
            dv_ref[0] = dv_sc[...]

        @pl.when((kj == nq - 1) & (qi == nq - 1))
        def _():
            dq_ref[...] = dq_ref[...] * scale

    qmap = lambda h, j, i: (h, jnp.maximum(i, j), 0)
    return pl.pallas_call(
        body,
        grid=(H, nq, nq),
        in_specs=[pl.BlockSpec((1, tq, HW), qmap), pl.BlockSpec((1, tq, HW), lambda h, j, i: (h, j, 0)),
                  pl.BlockSpec((1, tq, DV), lambda h, j, i: (h, j, 0)),
                  pl.BlockSpec((tq, DV), lambda h, j, i: (jnp.maximum(i, j), h)),
                  pl.BlockSpec((1, 8, tq), lambda h, j, i: (h, 0, jnp.maximum(i, j))),
                  pl.BlockSpec((1, 8, tq), lambda h, j, i: (h, 0, jnp.maximum(i, j)))],
        out_specs=[pl.BlockSpec((1, S, HW), lambda h, j, i: (h, 0, 0)), pl.BlockSpec((1, tq, HW), lambda h, j, i: (h, j, 0)),
                   pl.BlockSpec((1, tq, DV), lambda h, j, i: (h, j, 0))],
        out_shape=[jax.ShapeDtypeStruct((H, S, HW), F32), jax.ShapeDtypeStruct((H, S, HW), F32), jax.ShapeDtypeStruct((H, S, DV), F32)],
        scratch_shapes=[pltpu.VMEM((tq, HW), F32), pltpu.VMEM((tq, DV), F32)],
        compiler_params=_cparams(("parallel", "arbitrary", "arbitrary"), 48 << 20),
        name="flash_bwd",
    )(q, k, v, do, lse, delta)


def _rms_bwd(xf, g, dn_out):
    r = _rms_scale(xf)
    n = xf * r
    dn = dn_out * g
    dx = r * (dn - n * jnp.mean(dn * n, axis=-1, keepdims=True))
    return dx, jnp.sum(dn_out * n, axis=0, keepdims=True)


def _qkv_bwd(proj, pos, freq, g_qa, g_kva, w_q, w_kv, dq, dk, dv, ts):
    S = proj.shape[0]

    def body(qa_ref, ckv_ref, pos_ref, freq_ref, gq_ref, gk_ref, wq_ref, wkv_ref, dq_ref, dk_ref, dv_ref,
             dqp_ref, dkvp_ref, dqa_ref, dckv_ref, dkr_ref, dgq_ref, dgk_ref):
        i = pl.program_id(0)
        cos, sin, lane = _rope_tables(pos_ref, freq_ref, ts)
        sgn_sin = jnp.where(lane < 32, sin, -sin)
        live = lane < DR
        kr_sum = jnp.zeros((ts, 128), F32)
        for h in range(H):
            dqh = dq_ref[h]
            dqp_ref[:, h * HW:h * HW + DN] = dqh[:, 0:DN].astype(BF16)
            t = dqh[:, DN:HW]
            dqp_ref[:, h * HW + DN:(h + 1) * HW] = jnp.where(live, t * cos + _rope_swap(t, lane) * sgn_sin, 0.0).astype(BF16)
            dkh = dk_ref[h]
            dkvp_ref[:, h * HW:h * HW + DN] = dkh[:, 0:DN].astype(BF16)
            dkvp_ref[:, h * HW + DN:(h + 1) * HW] = dv_ref[h].astype(BF16)
            kr_sum = kr_sum + dkh[:, DN:HW]
        dkr_ref[...] = jnp.where(live, kr_sum * cos + _rope_swap(kr_sum, lane) * sgn_sin, 0.0).astype(BF16)
        dqn = lax.dot_general(dqp_ref[...], wq_ref[...], NT, preferred_element_type=F32)
        dkvn = lax.dot_general(dkvp_ref[...], wkv_ref[...], NT, preferred_element_type=F32)
        dqa, dgq = _rms_bwd(qa_ref[...], gq_ref[...], dqn)
        dckv, dgk = _rms_bwd(ckv_ref[...], gk_ref[...], dkvn)
        dqa_ref[...] = dqa.astype(BF16)
        dckv_ref[...] = dckv.astype(BF16)

        @pl.when(i == 0)
        def _():
            dgq_ref[...] = jnp.zeros_like(dgq_ref)
            dgk_ref[...] = jnp.zeros_like(dgk_ref)

        dgq_ref[0:1, :] += dgq
        dgk_ref[0:1, :] += dgk

    rowb = lambda w, blk: pl.BlockSpec((ts, w), lambda i: (i, blk))
    full = lambda a: pl.BlockSpec(a.shape, lambda i: (0,) * a.ndim)
    heads = lambda w: pl.BlockSpec((H, ts, w), lambda i: (0, i, 0))
    return pl.pallas_call(
        body, grid=(S // ts,),
        in_specs=[rowb(RQ, OFF_QA // RQ), rowb(RKV, OFF_CKV // RKV), pl.BlockSpec((ts, 1), lambda i: (i, 0)), full(freq), full(g_qa), full(g_kva),
                  full(w_q), full(w_kv), heads(HW), heads(HW), heads(DV)],
        out_specs=[rowb(H * HW, 0), rowb(H * HW, 0), rowb(RQ, 0), rowb(RKV, 0), rowb(128, 0),
                   pl.BlockSpec((8, RQ), lambda i: (0, 0)), pl.BlockSpec((8, RKV), lambda i: (0, 0))],
        out_shape=[jax.ShapeDtypeStruct((S, H * HW), BF16), jax.ShapeDtypeStruct((S, H * HW), BF16), jax.ShapeDtypeStruct((S, RQ), BF16),
                   jax.ShapeDtypeStruct((S, RKV), BF16), jax.ShapeDtypeStruct((S, 128), BF16),
                   jax.ShapeDtypeStruct((8, RQ), F32), jax.ShapeDtypeStruct((8, RKV), F32)],
        compiler_params=_cparams(("arbitrary",), 56 << 20), name="qkv_bwd",
    )(proj, proj, pos, freq, g_qa, g_kva, w_q, w_kv, dq, dk, dv)


def _pack_dproj(parts, ts):
    S = parts[0].shape[0]
    widths = [p.shape[1] for p in parts]
    assert sum(widths) == NP

    def body(*refs):
        o_ref = refs[-1]
        off = 0
        for r, w in zip(refs[:-1], widths):
            o_ref[:, off:off + w] = r[...]
            off += w

    return pl.pallas_call(
        body, grid=(S // ts,),
        in_specs=[pl.BlockSpec((ts, w), lambda i: (i, 0)) for w in widths],
        out_specs=pl.BlockSpec((ts, NP), lambda i: (i, 0)),
        out_shape=jax.ShapeDtypeStruct((S, NP), BF16),
        compiler_params=_cparams(("parallel",), 48 << 20), name="pack_dproj",
    )(*parts)


def _prenorm_bwd(x, g, dh, dy, ts):
    S = x.shape[0]

    def body(x_ref, g_ref, dh_ref, dy_ref, gx_ref, dg_ref):
        dx, dg = _rms_bwd(x_ref[...], g_ref[...], dh_ref[...])
        gx_ref[...] = dy_ref[...] + dx

        @pl.when(pl.program_id(0) == 0)
        def _():
            dg_ref[...] = jnp.zeros_like(dg_ref)

        dg_ref[0:1, :] += dg

    row = pl.BlockSpec((ts, D), lambda i: (i, 0))
    return pl.pallas_call(
        body, grid=(S // ts,), in_specs=[row, pl.BlockSpec((1, D), lambda i: (0, 0)), row, row],
        out_specs=[row, pl.BlockSpec((8, D), lambda i: (0, 0))],
        out_shape=[jax.ShapeDtypeStruct((S, D), F32), jax.ShapeDtypeStruct((8, D), F32)],
        compiler_params=_cparams(("arbitrary",), 40 << 20), name="prenorm_bwd",
    )(x, g, dh, dy)


def _local_step(x, pos, target, g_pre, g_qa, g_kva, g_post, w_in, w_q, w_kv, conv_w8, w_o_mla, w_o_conv, w_out):
    S = x.shape[0]
    ts = min(256, S)
    tq = min(512, S)
    tm = min(512, S)
    mm = functools.partial(_matmul, tm=tm)
    freq = _inv_freq_tile()

    h = _prenorm(x, g_pre, ts)
    proj = mm(h, w_in, mode="nn", out_dtype=F32, tn=1408, tk=D, name="mm_proj")
    q, k, v, vt, qn, kvn = _qkv_prep(proj, pos, freq, g_qa, g_kva, w_q, w_kv, ts)
    attn, lse = _flash_fwd(q, k, vt, tq)
    a_mla, a_conv = _gates_fwd(proj, attn, conv_w8, ts)
    y_mla = mm(a_mla, w_o_mla, mode="nn", out_dtype=F32, tn=1024, tk=D, name="mm_y_mla")
    y_conv = mm(a_conv, w_o_conv, mode="nn", out_dtype=F32, tn=1024, tk=D, name="mm_y_conv")
    merged = _merge_fwd(proj, y_mla, y_conv, ts)
    out = mm(merged, w_out, mode="nn", out_dtype=F32, tn=1024, tk=D, name="mm_out")
    dy, dout, dg_post, loss = _post_loss(out, x, target, g_post, ts)

    dmerged = mm(dout, w_out, mode="nt", out_dtype=F32, tn=1024, tk=D, name="mm_dmerged")
    dw_out = _matmul(merged, dout, mode="tn", out_dtype=F32, tm=1024, tn=1024, tk=1024, name="mm_dw_out")
    dy_mla, dy_conv, dg_mla, dg_conv = _merge_bwd(proj, y_mla, y_conv, dmerged, ts)
    da_mla = mm(dy_mla, w_o_mla, mode="nt", out_dtype=F32, tn=1024, tk=D, name="mm_da_mla")
    da_conv = mm(dy_conv, w_o_conv, mode="nt", out_dtype=F32, tn=1024, tk=D, name="mm_da_conv")
    dw_o_mla = _matmul(a_mla, dy_mla, mode="tn", out_dtype=F32, tm=1024, tn=1024, tk=1024, name="mm_dw_o_mla")
    dw_o_conv = _matmul(a_conv, dy_conv, mode="tn", out_dtype=F32, tm=1024, tn=1024, tk=1024, name="mm_dw_o_conv")
    dattn, delta, dz_mla, dc_in, db_gate, dc_gate, dz_conv, dconv_w = _gates_bwd(proj, attn, da_mla, da_conv, conv_w8, min(128, S))
    dq, dk, dv = _flash_bwd(q, k, v, dattn, lse, delta, tq)
    dqp, dkvp, dq_a, dc_kv, dk_rope, dg_qa, dg_kva = _qkv_bwd(proj, pos, freq, g_qa, g_kva, w_q, w_kv, dq, dk, dv, min(128, S))
    dw_q = _matmul(qn, dqp, mode="tn", out_dtype=F32, tm=RQ, tn=1024, tk=1024, name="mm_dw_q")
    dw_kv = _matmul(kvn, dkvp, mode="tn", out_dtype=F32, tm=RKV, tn=1024, tk=1024, name="mm_dw_kv")
    dproj = _pack_dproj([dz_mla, dc_in, db_gate, dc_gate, dz_conv, dg_mla, dg_conv, dq_a, dc_kv, dk_rope], ts)
    dh = mm(dproj, w_in, mode="nt", out_dtype=F32, tn=D, tk=1408, name="mm_dh")
    dw_in = _matmul(h, dproj, mode="tn", out_dtype=F32, tm=1024, tn=1408, tk=1024, name="mm_dw_in")
    grad_x, dg_pre = _prenorm_bwd(x, g_pre, dh, dy, ts)
    return dict(loss=loss, grad_x=grad_x, dg_pre=dg_pre, dg_qa=dg_qa, dg_kva=dg_kva, dg_post=dg_post, dw_in=dw_in, dw_q=dw_q,
                dw_kv=dw_kv, dconv_w=dconv_w, dw_o_mla=dw_o_mla, dw_o_conv=dw_o_conv, dw_out=dw_out)


def _my_id():
    return lax.axis_index("x") * 4 + lax.axis_index("y") * 2 + lax.axis_index("c")


def _peer(d):
    p = (_my_id() + d) % NDEV
    return (p // 4, (p // 2) % 2, p % 2), p


def _all_gather(arrays, name):
    n = len(arrays)

    def body(*refs):
        ins, outs = refs[:n], refs[n:2 * n]
        send_sems, recv_sems, local_sems = refs[2 * n:]
        me = _my_id()
        local = [pltpu.make_async_copy(ins[a], outs[a].at[me], local_sems.at[a]) for a in range(n)]
        for cp in local:
            cp.start()
        sends = []
        for d in range(1, NDEV):
            dev, _ = _peer(d)
            for a in range(n):
                cp = pltpu.make_async_remote_copy(src_ref=ins[a], dst_ref=outs[a].at[me], send_sem=send_sems.at[a, d - 1],
                                                  recv_sem=recv_sems.at[a, d - 1], device_id=dev, device_id_type=MESH)
                cp.start()
                sends.append(cp)
        for d in range(1, NDEV):
            _, src = _peer(NDEV - d)
            for a in range(n):
                pltpu.make_async_remote_copy(src_ref=ins[a], dst_ref=outs[a].at[src], send_sem=send_sems.at[a, d - 1],
                                             recv_sem=recv_sems.at[a, d - 1], device_id=_peer(d)[0], device_id_type=MESH).wait_recv()
        for cp in sends:
            cp.wait_send()
        for cp in local:
            cp.wait()

    anyspec = pl.BlockSpec(memory_space=pl.ANY)
    return pl.pallas_call(
        body,
        in_specs=[anyspec] * n,
        out_specs=[anyspec] * n,
        out_shape=[jax.ShapeDtypeStruct((NDEV,) + a.shape, a.dtype) for a in arrays],
        scratch_shapes=[pltpu.SemaphoreType.DMA((n, NDEV - 1)), pltpu.SemaphoreType.DMA((n, NDEV - 1)), pltpu.SemaphoreType.DMA((n,))],
        name=name,
    )(*arrays)


def _scatter_parts(arrays, name):
    n = len(arrays)

    def body(*refs):
        ins, outs = refs[:n], refs[n:2 * n]
        send_sems, recv_sems, local_sems = refs[2 * n:]
        me = _my_id()
        local = [pltpu.make_async_copy(ins[a].at[me], outs[a].at[me], local_sems.at[a]) for a in range(n)]
        for cp in local:
            cp.start()
        sends = []
        for d in range(1, NDEV):
            dev, p = _peer(d)
            for a in range(n):
                cp = pltpu.make_async_remote_copy(src_ref=ins[a].at[p], dst_ref=outs[a].at[me], send_sem=send_sems.at[a, d - 1],
                                                  recv_sem=recv_sems.at[a, d - 1], device_id=dev, device_id_type=MESH)
                cp.start()
                sends.append(cp)
        for d in range(1, NDEV):
            _, src = _peer(NDEV - d)
            for a in range(n):
                pltpu.make_async_remote_copy(src_ref=ins[a].at[src], dst_ref=outs[a].at[src], send_sem=send_sems.at[a, d - 1],
                                             recv_sem=recv_sems.at[a, d - 1], device_id=_peer(d)[0], device_id_type=MESH).wait_recv()
        for cp in sends:
            cp.wait_send()
        for cp in local:
            cp.wait()

    anyspec = pl.BlockSpec(memory_space=pl.ANY)
    return pl.pallas_call(
        body,
        in_specs=[anyspec] * n,
        out_specs=[anyspec] * n,
        out_shape=[jax.ShapeDtypeStruct(a.shape, a.dtype) for a in arrays],
        scratch_shapes=[pltpu.SemaphoreType.DMA((n, NDEV - 1)), pltpu.SemaphoreType.DMA((n, NDEV - 1)), pltpu.SemaphoreType.DMA((n,))],
        name=name,
    )(*arrays)


def _adamw_math(w, g, m, v):
    m = ADAM_B1 * m + (1.0 - ADAM_B1) * g
    v = ADAM_B2 * v + (1.0 - ADAM_B2) * (g * g)
    m_hat = m / (1.0 - ADAM_B1 ** ADAM_STEP)
    v_hat = v / (1.0 - ADAM_B2 ** ADAM_STEP)
    delta = -ADAM_LR * (m_hat / (jnp.sqrt(v_hat) + ADAM_EPS) + ADAM_WD * w)
    return delta, m, v


def _sum_adamw(parts, w, m, v, rows_per_block, name):
    n, R, C = parts.shape
    tr = min(rows_per_block, R)
    assert R % tr == 0

    def body(p_ref, w_ref, m_ref, v_ref, g_ref, d_ref, nm_ref, nv_ref):
        g = p_ref[0]
        for j in range(1, n):
            g = g + p_ref[j]
        g_ref[...] = g
        d, nm, nv = _adamw_math(w_ref[...], g, m_ref[...], v_ref[...])
        d_ref[...] = d
        nm_ref[...] = nm
        nv_ref[...] = nv

    blk = pl.BlockSpec((tr, C), lambda i: (i, 0))
    return pl.pallas_call(
        body, grid=(R // tr,),
        in_specs=[pl.BlockSpec((n, tr, C), lambda i: (0, i, 0)), blk, blk, blk],
        out_specs=[blk] * 4,
        out_shape=[jax.ShapeDtypeStruct((R, C), F32)] * 4,
        compiler_params=_cparams(("parallel",), 48 << 20), name=name,
    )(parts, w, m, v)


def kernel(x, positions, pre_norm_g, w_in, q_a_norm_g, w_q_b, kv_a_norm_g, w_kv_b, conv_w, w_o_mla, w_o_conv, w_out, post_norm_g, loss_target, m_pre_norm_g, m_w_in, m_q_a_norm_g, m_w_q_b, m_kv_a_norm_g, m_w_kv_b, m_conv_w, m_w_o_mla, m_w_o_conv, m_w_out, m_post_norm_g, v_pre_norm_g, v_w_in, v_q_a_norm_g, v_w_q_b, v_kv_a_norm_g, v_w_kv_b, v_conv_w, v_w_o_mla, v_w_o_conv, v_w_out, v_post_norm_g):
    S = x.shape[1]
    conv_pad = jnp.zeros((8, 256), F32).at[0:3, :].set(conv_w)
    g_in, g_q, g_kv, g_om, g_oc, g_out, g_cw = _all_gather(
        [w_in.astype(BF16), w_q_b.astype(BF16), w_kv_b.astype(BF16), w_o_mla.astype(BF16), w_o_conv.astype(BF16), w_out.astype(BF16),
         conv_pad], "all_gather_weights")
    w_in_f = _assemble_w_in(g_in)
    w_q_f = _assemble_w_q(g_q)
    w_kv_f = _concat_cols(g_kv, BF16, "assemble_w_kv")
    conv_w8 = _concat_cols(g_cw, F32, "assemble_conv_w")
    row2 = lambda a: a.reshape(1, -1)
    r = _local_step(x[0], positions.reshape(S, 1), loss_target[0], row2(pre_norm_g), row2(q_a_norm_g), row2(kv_a_norm_g),
                    row2(post_norm_g), w_in_f, w_q_f, w_kv_f, conv_w8, g_om.reshape(D, D), g_oc.reshape(D, D), g_out.reshape(D, D))
    small = jnp.concatenate([r["dg_pre"][0:1], r["dg_post"][0:1], jnp.pad(r["dg_qa"][0:1], ((0, 0), (0, D - RQ))),
                             jnp.pad(r["dg_kva"][0:1], ((0, 0), (0, D - RKV))), jnp.pad(r["loss"][0:1], ((0, 0), (0, D - 128))),
                             jnp.zeros((3, D), F32)], axis=0)
    p_in, p_q, p_kv, p_cw, p_om, p_oc, p_out = _scatter_parts(
        [_split_dw_in(r["dw_in"]), _split_dw_q(r["dw_q"]), _split_cols(r["dw_kv"], NDEV, "split_dw_kv"),
         _split_cols(r["dconv_w"], NDEV, "split_dconv_w"), r["dw_o_mla"].reshape(NDEV, D // NDEV, D),
         r["dw_o_conv"].reshape(NDEV, D // NDEV, D), r["dw_out"].reshape(NDEV, D // NDEV, D)], "scatter_grads")
    p_small, = _all_gather([small], "all_gather_small")
    o_in = _sum_adamw(p_in, w_in, m_w_in, v_w_in, 64, "adamw_w_in")
    o_q = _sum_adamw(p_q, w_q_b, m_w_q_b, v_w_q_b, 512, "adamw_w_q")
    o_kv = _sum_adamw(p_kv, w_kv_b, m_w_kv_b, v_w_kv_b, 512, "adamw_w_kv")
    pad8 = lambda a: jnp.zeros((8, 256), F32).at[0:3, :].set(a)
    o_cw = [a[0:3] for a in _sum_adamw(p_cw, conv_pad, pad8(m_conv_w), pad8(v_conv_w), 8, "adamw_conv_w")]
    o_om = _sum_adamw(p_om, w_o_mla, m_w_o_mla, v_w_o_mla, 64, "adamw_w_o_mla")
    o_oc = _sum_adamw(p_oc, w_o_conv, m_w_o_conv, v_w_o_conv, 64, "adamw_w_o_conv")
    o_out = _sum_adamw(p_out, w_out, m_w_out, v_w_out, 64, "adamw_w_out")
    padv = lambda a: jnp.pad(a.reshape(1, -1), ((0, 0), (0, D - a.shape[0])))
    stack = lambda pre, post, qa, kva: jnp.concatenate([row2(pre), row2(post), padv(qa), padv(kva), jnp.zeros((4, D), F32)], axis=0)
    o_g = _sum_adamw(p_small, stack(pre_norm_g, post_norm_g, q_a_norm_g, kv_a_norm_g),
                     stack(m_pre_norm_g, m_post_norm_g, m_q_a_norm_g, m_kv_a_norm_g),
                     stack(v_pre_norm_g, v_post_norm_g, v_q_a_norm_g, v_kv_a_norm_g), 8, "adamw_gains")
    loss = o_g[0][4, 0]
    outs = {}
    for idx, kind in enumerate(("grad", "delta", "new_m", "new_v")):
        o = o_g[idx]
        outs[kind] = dict(pre_norm_g=o[0], w_in=o_in[idx], q_a_norm_g=o[2, 0:RQ], w_q_b=o_q[idx], kv_a_norm_g=o[3, 0:RKV],
                          w_kv_b=o_kv[idx], conv_w=o_cw[idx], w_o_mla=o_om[idx], w_o_conv=o_oc[idx], w_out=o_out[idx], post_norm_g=o[1])
    names = ["pre_norm_g", "w_in", "q_a_norm_g", "w_q_b", "kv_a_norm_g", "w_kv_b", "conv_w", "w_o_mla", "w_o_conv", "w_out", "post_norm_g"]
    return (loss, r["grad_x"][None], *[outs["grad"][n] for n in names], *[outs["delta"][n] for n in names],
            *[outs["new_m"][n] for n in names], *[outs["new_v"][n] for n in names])
```

```python
import functools
import math

import jax
import jax.numpy as jnp
from jax import lax
from jax.experimental import pallas as pl
from jax.experimental.pallas import tpu as pltpu

F32 = jnp.float32
BF16 = jnp.bfloat16

NDEV = 8
D = 2048
H = 16
DN = 128
DR = 64
DV = 128
RQ = 512
RKV = 512
HW = 256
ROPE_THETA = 10000.0
RMS_EPS = 1e-6
N_IN = 15424
SHARD_IN = N_IN // NDEV
SMALL = RQ + RKV + DR
NP = 7 * D + RQ + RKV + 128
SEG = dict(z_mla=0, c_in=1, b_gate=2, c_gate=3, z_conv=4, g_mla=5, g_conv=6)
OFF_QA = 7 * D
OFF_CKV = OFF_QA + RQ
OFF_KR = OFF_CKV + RKV
EXT = 2176
VMEM_CAP = 56 * 1024 * 1024

ADAM_LR = 0.001
ADAM_B1 = 0.9
ADAM_B2 = 0.999
ADAM_EPS = 1e-08
ADAM_WD = 0.01
ADAM_STEP = 10

LOG2E = math.log2(math.e)
NN = (((1,), (0,)), ((), ()))
NT = (((1,), (1,)), ((), ()))
TN = (((0,), (0,)), ((), ()))
MESH = pl.DeviceIdType.MESH


def _cparams(sem, vmem_bytes):
    return pltpu.CompilerParams(dimension_semantics=sem, vmem_limit_bytes=int(min(VMEM_CAP, max(vmem_bytes, 16 << 20))))


def _nbytes(shape, dtype):
    return math.prod(shape) * jnp.dtype(dtype).itemsize


def _matmul(a, b, *, mode, out_dtype, tm, tn, tk, name, m_outer=False):
    if mode == "nn":
        (M, K), (K2, N) = a.shape, b.shape
    elif mode == "nt":
        (M, K), (N, K2) = a.shape, b.shape
    else:
        (K, M), (K2, N) = a.shape, b.shape
    assert K == K2, (a.shape, b.shape, mode)
    tm, tn, tk = min(tm, M), min(tn, N), min(tk, K)
    assert M % tm == 0 and N % tn == 0 and K % tk == 0, (M, N, K, tm, tn, tk)
    ni, nj, nk = M // tm, N // tn, K // tk
    dims = dict(nn=NN, nt=NT, tn=TN)[mode]

    if m_outer:
        grid = (ni, nj, nk)
        ij = lambda g0, g1: (g0, g1)
    else:
        grid = (nj, ni, nk)
        ij = lambda g0, g1: (g1, g0)

    if mode == "tn":
        a_spec = pl.BlockSpec((tk, tm), lambda g0, g1, k: (k, ij(g0, g1)[0]))
        a_tile = (tk, tm)
    else:
        a_spec = pl.BlockSpec((tm, tk), lambda g0, g1, k: (ij(g0, g1)[0], k))
        a_tile = (tm, tk)
    if mode == "nt":
        b_spec = pl.BlockSpec((tn, tk), lambda g0, g1, k: (ij(g0, g1)[1], k))
    else:
        b_spec = pl.BlockSpec((tk, tn), lambda g0, g1, k: (k, ij(g0, g1)[1]))
    o_spec = pl.BlockSpec((tm, tn), lambda g0, g1, k: ij(g0, g1))

    def body(a_ref, b_ref, o_ref, *scratch):
        prod = lax.dot_general(a_ref[...], b_ref[...], dims, preferred_element_type=F32)
        if nk == 1:
            o_ref[...] = prod.astype(o_ref.dtype)
        else:
            acc_ref, = scratch
            k = pl.program_id(2)

            @pl.when(k == 0)
            def _():
                acc_ref[...] = prod

            @pl.when(k > 0)
            def _():
                acc_ref[...] += prod

            @pl.when(k == nk - 1)
            def _():
                o_ref[...] = acc_ref[...].astype(o_ref.dtype)

    vmem = 2 * (_nbytes(a_tile, a.dtype) + _nbytes((tk, tn), b.dtype) + _nbytes((tm, tn), out_dtype)) + 2 * _nbytes((tm, tn), F32)
    return pl.pallas_call(
        body,
        grid=grid,
        in_specs=[a_spec, b_spec],
        out_specs=o_spec,
        out_shape=jax.ShapeDtypeStruct((M, N), out_dtype),
        scratch_shapes=[] if nk == 1 else [pltpu.VMEM((tm, tn), F32)],
        compiler_params=_cparams(("parallel", "parallel", "arbitrary"), vmem + (8 << 20)),
        name=name,
    )(a, b)


def _in_dest(k):
    return SHARD_IN * k - SMALL


def _assemble_w_in(g):
    R = 128

    def body(g_ref, o_ref, ext, acc):
        ext[...] = jnp.zeros_like(ext)
        acc[...] = jnp.zeros_like(acc)
        lane = lax.broadcasted_iota(jnp.int32, (R, EXT), 1)
        ext[:, 0:SHARD_IN] = g_ref[0].astype(F32)
        v = ext[...]
        acc[:, OFF_QA:NP] = jnp.where(lane[:, 0:NP - OFF_QA] < SMALL, v[:, 0:NP - OFF_QA], 0.0)
        w = v[:, 1024:2048]
        w = pltpu.roll(w, 1024 - 64, 1)
        acc[:, 0:1024] = jnp.where(lane[:, 0:1024] < SHARD_IN - SMALL, w, 0.0)
        for k in range(1, NDEV):
            ext[:, 0:SHARD_IN] = g_ref[k].astype(F32)
            dest = _in_dest(k)
            t, o = dest // 128, dest % 128
            width = -(-(o + SHARD_IN) // 128) * 128
            v = pltpu.roll(ext[...], o, 1)[:, 0:width]
            acc[:, 128 * t:128 * t + width] += v
        o_ref[...] = acc[...].astype(BF16)

    return pl.pallas_call(
        body,
        grid=(D // R,),
        in_specs=[pl.BlockSpec((NDEV, R, SHARD_IN), lambda i: (0, i, 0))],
        out_specs=pl.BlockSpec((R, NP), lambda i: (i, 0)),
        out_shape=jax.ShapeDtypeStruct((D, NP), BF16),
        scratch_shapes=[pltpu.VMEM((R, EXT), F32), pltpu.VMEM((R, NP), F32)],
        compiler_params=_cparams(("parallel",), 40 << 20),
        name="assemble_w_in",
    )(g)


def _split_dw_in(dw):
    R = 128

    def body(dw_ref, o_ref):
        lane = lax.broadcasted_iota(jnp.int32, (R, 1024), 1)
        o_ref[0, :, 0:1024] = dw_ref[:, OFF_QA:OFF_QA + 1024]
        tail = dw_ref[:, OFF_QA + 1024:NP]
        tail = jnp.concatenate([tail, jnp.zeros((R, 1024 - 128), F32)], axis=1)
        head = dw_ref[:, 0:1024]
        mixed = jnp.where(lane < 64, tail, pltpu.roll(head, 64, 1))
        o_ref[0, :, 1024:SHARD_IN] = mixed[:, 0:SHARD_IN - 1024]
        for k in range(1, NDEV):
            dest = _in_dest(k)
            t, o = dest // 128, dest % 128
            width = -(-(o + SHARD_IN) // 128) * 128
            v = dw_ref[:, 128 * t:128 * t + width]
            v = pltpu.roll(v, width - o, 1)
            o_ref[k] = v[:, 0:SHARD_IN]

    return pl.pallas_call(
        body,
        grid=(D // R,),
        in_specs=[pl.BlockSpec((R, NP), lambda i: (i, 0))],
        out_specs=pl.BlockSpec((NDEV, R, SHARD_IN), lambda i: (0, i, 0)),
        out_shape=jax.ShapeDtypeStruct((NDEV, D, SHARD_IN), F32),
        compiler_params=_cparams(("parallel",), 40 << 20),
        name="split_dw_in",
    )(dw)


def _assemble_w_q(g):
    def body(g_ref, o_ref):
        lane = lax.broadcasted_iota(jnp.int32, (RQ, 128), 1)
        lo = lane < 64
        for k in range(NDEV):
            t0 = g_ref[k, :, 0:128].astype(F32)
            t1 = g_ref[k, :, 128:256].astype(F32)
            t2 = g_ref[k, :, 256:384].astype(F32)
            base = 2 * k * HW
            o_ref[:, base:base + 128] = t0.astype(BF16)
            o_ref[:, base + 128:base + 256] = jnp.where(lo, t1, 0.0).astype(BF16)
            o_ref[:, base + 256:base + 384] = pltpu.roll(jnp.where(lo, t2, t1), 64, 1).astype(BF16)
            o_ref[:, base + 384:base + 512] = jnp.where(lo, pltpu.roll(t2, 64, 1), 0.0).astype(BF16)

    return pl.pallas_call(
        body,
        out_shape=jax.ShapeDtypeStruct((RQ, H * HW), BF16),
        compiler_params=_cparams(None, 32 << 20),
        name="assemble_w_q",
    )(g)


def _split_dw_q(dw):
    def body(dw_ref, o_ref):
        lane = lax.broadcasted_iota(jnp.int32, (RQ, 128), 1)
        lo = lane < 64
        for k in range(NDEV):
            base = 2 * k * HW
            a = dw_ref[:, base:base + 128]
            b = dw_ref[:, base + 128:base + 256]
            c = pltpu.roll(dw_ref[:, base + 256:base + 384], 64, 1)
            d = pltpu.roll(dw_ref[:, base + 384:base + 512], 64, 1)
            o_ref[k, :, 0:128] = a
            o_ref[k, :, 128:256] = jnp.where(lo, b, c)
            o_ref[k, :, 256:384] = jnp.where(lo, c, d)

    return pl.pallas_call(
        body,
        out_shape=jax.ShapeDtypeStruct((NDEV, RQ, 384), F32),
        compiler_params=_cparams(None, 32 << 20),
        name="split_dw_q",
    )(dw)


def _concat_cols(g, dtype, name):
    n, R, C = g.shape

    def body(g_ref, o_ref):
        for k in range(n):
            o_ref[:, k * C:(k + 1) * C] = g_ref[k].astype(dtype)

    return pl.pallas_call(body, out_shape=jax.ShapeDtypeStruct((R, n * C), dtype), compiler_params=_cparams(None, 32 << 20), name=name)(g)


def _split_cols(x, n, name):
    R, NC = x.shape
    C = NC // n

    def body(x_ref, o_ref):
        for k in range(n):
            o_ref[k] = x_ref[:, k * C:(k + 1) * C]

    return pl.pallas_call(body, out_shape=jax.ShapeDtypeStruct((n, R, C), x.dtype), compiler_params=_cparams(None, 32 << 20), name=name)(x)


def _rms_scale(xf):
    return lax.rsqrt(jnp.mean(xf * xf, axis=-1, keepdims=True) + RMS_EPS)


def _prenorm(x, g, ts):
    S = x.shape[0]

    def body(x_ref, g_ref, h_ref):
        xf = x_ref[...]
        h_ref[...] = (xf * _rms_scale(xf) * g_ref[...]).astype(BF16)

    return pl.pallas_call(
        body,
        grid=(S // ts,),
        in_specs=[pl.BlockSpec((ts, D), lambda i: (i, 0)), pl.BlockSpec((1, D), lambda i: (0, 0))],
        out_specs=pl.BlockSpec((ts, D), lambda i: (i, 0)),
        out_shape=jax.ShapeDtypeStruct((S, D), BF16),
        compiler_params=_cparams(("parallel",), 32 << 20),
        name="prenorm",
    )(x, g)


def _inv_freq_tile():
    inv_freq = ROPE_THETA ** (-jnp.arange(0, DR, 2, dtype=F32) / DR)
    return jnp.tile(inv_freq, 4).reshape(1, 128)


def _rope_tables(pos_ref, freq_ref, ts):
    lane = lax.broadcasted_iota(jnp.int32, (ts, 128), 1)
    ang = pos_ref[...].astype(F32) * freq_ref[...]
    return jnp.cos(ang), jnp.sin(ang), lane


def _rope_swap(t, lane):
    return jnp.where(lane < 32, pltpu.roll(t, 96, 1), pltpu.roll(t, 32, 1))


def _qkv_prep(proj, pos, freq, g_qa, g_kva, w_q, w_kv, ts):
    S = proj.shape[0]

    def body(qa_ref, ckv_ref, kr_ref, pos_ref, freq_ref, gq_ref, gk_ref, wq_ref, wkv_ref, q_ref, k_ref, v_ref, vt_ref, qn_ref, kvn_ref):
        qa = qa_ref[...]
        qn = (qa * _rms_scale(qa) * gq_ref[...]).astype(BF16)
        ckv = ckv_ref[...]
        kvn = (ckv * _rms_scale(ckv) * gk_ref[...]).astype(BF16)
        qn_ref[...] = qn
        kvn_ref[...] = kvn
        cos, sin, lane = _rope_tables(pos_ref, freq_ref, ts)
        sgn_sin = jnp.where(lane < 32, -sin, sin)
        live = lane < DR
        kr = kr_ref[...]
        kr = jnp.where(live, kr * cos + _rope_swap(kr, lane) * sgn_sin, 0.0).astype(BF16)
        qf = jnp.dot(qn, wq_ref[...], preferred_element_type=F32)
        kvf = jnp.dot(kvn, wkv_ref[...], preferred_element_type=F32)
        for h in range(H):
            q_ref[h, :, 0:DN] = qf[:, h * HW:h * HW + DN].astype(BF16)
            t = qf[:, h * HW + DN:(h + 1) * HW]
            q_ref[h, :, DN:HW] = jnp.where(live, t * cos + _rope_swap(t, lane) * sgn_sin, 0.0).astype(BF16)
            k_ref[h, :, 0:DN] = kvf[:, h * HW:h * HW + DN].astype(BF16)
            k_ref[h, :, DN:HW] = kr
            vh = kvf[:, h * HW + DN:(h + 1) * HW]
            v_ref[h] = vh.astype(BF16)
            vt_ref[h] = jnp.transpose(vh).astype(BF16)

    row = lambda w, blk: pl.BlockSpec((ts, w), lambda i: (i, blk))
    full = lambda a: pl.BlockSpec(a.shape, lambda i: (0,) * a.ndim)
    return pl.pallas_call(
        body,
        grid=(S // ts,),
        in_specs=[row(RQ, OFF_QA // RQ), row(RKV, OFF_CKV // RKV), row(128, OFF_KR // 128),
                  pl.BlockSpec((ts, 1), lambda i: (i, 0)), full(freq), full(g_qa), full(g_kva), full(w_q), full(w_kv)],
        out_specs=[pl.BlockSpec((H, ts, HW), lambda i: (0, i, 0)), pl.BlockSpec((H, ts, HW), lambda i: (0, i, 0)),
                   pl.BlockSpec((H, ts, DV), lambda i: (0, i, 0)), pl.BlockSpec((H, DV, ts), lambda i: (0, 0, i)),
                   pl.BlockSpec((ts, RQ), lambda i: (i, 0)), pl.BlockSpec((ts, RKV), lambda i: (i, 0))],
        out_shape=[jax.ShapeDtypeStruct((H, S, HW), BF16), jax.ShapeDtypeStruct((H, S, HW), BF16),
                   jax.ShapeDtypeStruct((H, S, DV), BF16), jax.ShapeDtypeStruct((H, DV, S), BF16),
                   jax.ShapeDtypeStruct((S, RQ), BF16), jax.ShapeDtypeStruct((S, RKV), BF16)],
        compiler_params=_cparams(("parallel",), 48 << 20),
        name="qkv_prep",
    )(proj, proj, proj, pos, freq, g_qa, g_kva, w_q, w_kv)


def _col_to_row8(col, n):
    return jnp.transpose(jnp.broadcast_to(col, (n, 128)))[0:8, :]


def _causal_pairs(n, key_major):
    if key_major:
        pairs = [(i, j) for j in range(n) for i in range(j, n)]
    else:
        pairs = [(i, j) for i in range(n) for j in range(i + 1)]
    return jnp.array([p[0] for p in pairs], jnp.int32), jnp.array([p[1] for p in pairs], jnp.int32)


def _flash_fwd(q, k, vt, tq, hb=8):
    S = q.shape[1]
    nq = S // tq
    scale = 1.0 / math.sqrt(DN + DR)
    c2 = scale * math.log2(math.e)
    tsub = tq
    nsub = tq // tsub

    qi_tab, kj_tab = _causal_pairs(nq, key_major=False)

    def body(qi_ref, kj_ref, q_ref, k_ref, vt_ref, o_ref, lse_ref, m_sc, l_sc, acc_sc):
        p = pl.program_id(1)
        qi, kj = qi_ref[p], kj_ref[p]

        @pl.when(kj == 0)
        def _():
            m_sc[...] = jnp.full_like(m_sc, -jnp.inf)
            l_sc[...] = jnp.zeros_like(l_sc)
            acc_sc[...] = jnp.zeros_like(acc_sc)

        def step(diag):
            chains = [(h, u) for h in range(hb) for u in range(nsub)]

            def scores(h, u):
                return lax.dot_general(k_ref[h], q_ref[h, u * tsub:(u + 1) * tsub, :], NT, preferred_element_type=F32)

            st_next = scores(*chains[0])
            for ci, (h, u) in enumerate(chains):
                st = st_next
                if ci + 1 < len(chains):
                    st_next = scores(*chains[ci + 1])
                cols = slice(u * tsub, (u + 1) * tsub)
                if diag:
                    r = lax.broadcasted_iota(jnp.int32, (tq, tsub), 0)
                    c = lax.broadcasted_iota(jnp.int32, (tq, tsub), 1) + u * tsub
                    st = jnp.where(r <= c, st, -jnp.inf)
                m_prev = m_sc[h, 0:1, cols]
                m_new = jnp.maximum(m_prev, jnp.max(st, axis=0, keepdims=True))
                alpha = jnp.exp2((m_prev - m_new) * c2)
                pt = jnp.exp2((st - m_new) * c2)
                l_sc[h, :, cols] = jnp.broadcast_to(alpha * l_sc[h, 0:1, cols] + jnp.sum(pt, axis=0, keepdims=True), (8, tsub))
                m_sc[h, :, cols] = jnp.broadcast_to(m_new, (8, tsub))
                acc_sc[h, :, cols] = alpha * acc_sc[h, :, cols] + jnp.dot(vt_ref[h], pt.astype(BF16), preferred_element_type=F32)

        @pl.when(kj < qi)
        def _():
            step(False)

        @pl.when(kj == qi)
        def _():
            step(True)
            for h in range(hb):
                l = l_sc[h, 0:1, :]
                o_ref[:, h * DV:(h + 1) * DV] = jnp.transpose(acc_sc[h] / l)
                lse_ref[h] = m_sc[h] * scale + jnp.log(l_sc[h])

    return pl.pallas_call(
        body,
        grid_spec=pltpu.PrefetchScalarGridSpec(
            num_scalar_prefetch=2,
            grid=(H // hb, len(qi_tab)),
            in_specs=[pl.BlockSpec((hb, tq, HW), lambda h, p, qi, kj: (h, qi[p], 0)),
                      pl.BlockSpec((hb, tq, HW), lambda h, p, qi, kj: (h, kj[p], 0)),
                      pl.BlockSpec((hb, DV, tq), lambda h, p, qi, kj: (h, 0, kj[p]))],
            out_specs=[pl.BlockSpec((tq, hb * DV), lambda h, p, qi, kj: (qi[p], h)),
                       pl.BlockSpec((hb, 8, tq), lambda h, p, qi, kj: (h, 0, qi[p]))],
            scratch_shapes=[pltpu.VMEM((hb, 8, tq), F32), pltpu.VMEM((hb, 8, tq), F32), pltpu.VMEM((hb, DV, tq), F32)],
        ),
        out_shape=[jax.ShapeDtypeStruct((S, H * DV), F32), jax.ShapeDtypeStruct((H, 8, S), F32)],
        compiler_params=_cparams(("parallel", "arbitrary"), 40 << 20),
        name="flash_fwd",
    )(qi_tab, kj_tab, q, k, vt)


def _sigmoid(x):
    return 1.0 / (1.0 + jnp.exp(-x))


def _shift_rows(u, prev8, n, first):
    ts = u.shape[0]
    row = lax.broadcasted_iota(jnp.int32, u.shape, 0)
    out = pltpu.roll(u, n, 0)
    for j in range(n):
        halo = jnp.where(first, 0.0, prev8[8 - n + j:8 - n + j + 1, :])
        out = jnp.where(row == j, halo, out)
    return out


def _gates_fwd(proj, attn, conv_w, ts):
    S = proj.shape[0]

    def body(attn_ref, zm_ref, cin_ref, bg_ref, cg_ref, zc_ref, cin_p, cg_p, w_ref, am_ref, ac_ref):
        first = pl.program_id(0) == 0
        zm = zm_ref[...]
        am_ref[...] = (attn_ref[...] * (zm * _sigmoid(zm))).astype(BF16)
        u = cg_ref[...] * cin_ref[...]
        up = cg_p[...] * cin_p[...]
        w = w_ref[...]
        conv = w[0:1, :] * _shift_rows(u, up, 2, first) + w[1:2, :] * _shift_rows(u, up, 1, first) + w[2:3, :] * u
        zc = zc_ref[...]
        ac_ref[...] = (bg_ref[...] * conv * (zc * _sigmoid(zc))).astype(BF16)

    seg = lambda name: pl.BlockSpec((ts, D), lambda i: (i, SEG[name]))
    prev = lambda name: pl.BlockSpec((8, D), lambda i: (jnp.maximum(i * (ts // 8) - 1, 0), SEG[name]))
    return pl.pallas_call(
        body,
        grid=(S // ts,),
        in_specs=[pl.BlockSpec((ts, D), lambda i: (i, 0)), seg("z_mla"), seg("c_in"), seg("b_gate"), seg("c_gate"), seg("z_conv"),
                  prev("c_in"), prev("c_gate"), pl.BlockSpec((8, D), lambda i: (0, 0))],
        out_specs=[pl.BlockSpec((ts, D), lambda i: (i, 0))] * 2,
        out_shape=[jax.ShapeDtypeStruct((S, D), BF16)] * 2,
        compiler_params=_cparams(("arbitrary",), 48 << 20),
        name="gates_fwd",
    )(attn, proj, proj, proj, proj, proj, proj, proj, conv_w)


def _merge_fwd(proj, y_mla, y_conv, ts):
    S = proj.shape[0]

    def body(gm_ref, gc_ref, ym_ref, yc_ref, o_ref):
        o_ref[...] = (_sigmoid(gm_ref[...]) * ym_ref[...] + _sigmoid(gc_ref[...]) * yc_ref[...]).astype(BF16)

    seg = lambda name: pl.BlockSpec((ts, D), lambda i: (i, SEG[name]))
    row = pl.BlockSpec((ts, D), lambda i: (i, 0))
    return pl.pallas_call(
        body, grid=(S // ts,), in_specs=[seg("g_mla"), seg("g_conv"), row, row], out_specs=row,
        out_shape=jax.ShapeDtypeStruct((S, D), BF16), compiler_params=_cparams(("parallel",), 32 << 20), name="merge_fwd",
    )(proj, proj, y_mla, y_conv)


def _post_loss(out, x, target, g_post, ts):
    S = out.shape[0]

    def body(o_ref, x_ref, t_ref, g_ref, dy_ref, do_ref, dg_ref, loss_ref):
        i = pl.program_id(0)
        o = o_ref[...]
        r = _rms_scale(o)
        n = o * r
        g = g_ref[...]
        err = x_ref[...] + n * g - t_ref[...]
        dy = err * (1.0 / D)
        dy_ref[...] = dy
        dn = dy * g
        do_ref[...] = (r * (dn - n * jnp.mean(dn * n, axis=-1, keepdims=True))).astype(BF16)
        dg = jnp.sum(dy * n, axis=0, keepdims=True)
        part = jnp.sum(jnp.sum(err * err, axis=0, keepdims=True), axis=1, keepdims=True) * (0.5 / D)

        @pl.when(i == 0)
        def _():
            dg_ref[...] = jnp.zeros_like(dg_ref)
            loss_ref[...] = jnp.zeros_like(loss_ref)

        dg_ref[0:1, :] += dg
        loss_ref[...] += jnp.broadcast_to(part, loss_ref.shape)

    row = pl.BlockSpec((ts, D), lambda i: (i, 0))
    return pl.pallas_call(
        body, grid=(S // ts,),
        in_specs=[row, row, row, pl.BlockSpec((1, D), lambda i: (0, 0))],
        out_specs=[row, row, pl.BlockSpec((8, D), lambda i: (0, 0)), pl.BlockSpec((8, 128), lambda i: (0, 0))],
        out_shape=[jax.ShapeDtypeStruct((S, D), F32), jax.ShapeDtypeStruct((S, D), BF16),
                   jax.ShapeDtypeStruct((8, D), F32), jax.ShapeDtypeStruct((8, 128), F32)],
        compiler_params=_cparams(("arbitrary",), 40 << 20), name="post_loss",
    )(out, x, target, g_post)


def _merge_bwd(proj, y_mla, y_conv, dmerged, ts):
    S = proj.shape[0]

    def body(gm_ref, gc_ref, ym_ref, yc_ref, dm_ref, dym_ref, dyc_ref, dgm_ref, dgc_ref):
        dm = dm_ref[...]
        sm = _sigmoid(gm_ref[...])
        sc = _sigmoid(gc_ref[...])
        dym_ref[...] = (dm * sm).astype(BF16)
        dyc_ref[...] = (dm * sc).astype(BF16)
        dgm_ref[...] = (dm * ym_ref[...] * (sm * (1.0 - sm))).astype(BF16)
        dgc_ref[...] = (dm * yc_ref[...] * (sc * (1.0 - sc))).astype(BF16)

    seg = lambda name: pl.BlockSpec((ts, D), lambda i: (i, SEG[name]))
    row = pl.BlockSpec((ts, D), lambda i: (i, 0))
    return pl.pallas_call(
        body, grid=(S // ts,), in_specs=[seg("g_mla"), seg("g_conv"), row, row, row], out_specs=[row] * 4,
        out_shape=[jax.ShapeDtypeStruct((S, D), BF16)] * 4, compiler_params=_cparams(("parallel",), 40 << 20), name="merge_bwd",
    )(proj, proj, y_mla, y_conv, dmerged)


def _gates_bwd(proj, attn, da_mla, da_conv, conv_w, ts):
    S = proj.shape[0]
    nblk = S // ts

    def body(attn_ref, zm_ref, cin_ref, bg_ref, cg_ref, zc_ref, dam_ref, dac_ref, cin_p, cg_p, bg_n, zc_n, dac_n, w_ref,
             dattn_ref, delta_ref, dzm_ref, dcin_ref, dbg_ref, dcg_ref, dzc_ref, dw_ref):
        i = pl.program_id(0)
        first = i == 0
        last = i == nblk - 1
        zm = zm_ref[...]
        sg = _sigmoid(zm)
        silu = zm * sg
        attn = attn_ref[...]
        dam = dam_ref[...]
        dattn = dam * silu
        dattn_ref[...] = dattn.astype(BF16)
        dzm_ref[...] = (dam * attn * (sg * (1.0 + zm * (1.0 - sg)))).astype(BF16)
        prod = dattn * attn
        for h in range(H):
            col = jnp.sum(prod[:, h * DV:(h + 1) * DV], axis=1, keepdims=True)
            delta_ref[h] = _col_to_row8(col, ts)
        w = w_ref[...]
        cin, cg, bg, zc = cin_ref[...], cg_ref[...], bg_ref[...], zc_ref[...]
        u = cg * cin
        up = cg_p[...] * cin_p[...]
        u1 = _shift_rows(u, up, 1, first)
        u2 = _shift_rows(u, up, 2, first)
        conv = w[0:1, :] * u2 + w[1:2, :] * u1 + w[2:3, :] * u
        sgc = _sigmoid(zc)
        siluc = zc * sgc
        dac = dac_ref[...]
        dbg_ref[...] = (dac * conv * siluc).astype(BF16)
        dzc_ref[...] = (dac * bg * conv * (sgc * (1.0 + zc * (1.0 - sgc)))).astype(BF16)
        dconv = dac * bg * siluc
        zn = zc_n[...]
        dconv_n = jnp.where(last, 0.0, dac_n[...] * bg_n[...] * (zn * _sigmoid(zn)))
        row = lax.broadcasted_iota(jnp.int32, dconv.shape, 0)
        d1 = jnp.where(row == ts - 1, dconv_n[0:1, :], pltpu.roll(dconv, ts - 1, 0))
        d2 = jnp.where(row == ts - 1, dconv_n[1:2, :], jnp.where(row == ts - 2, dconv_n[0:1, :], pltpu.roll(dconv, ts - 2, 0)))
        du = w[2:3, :] * dconv + w[1:2, :] * d1 + w[0:1, :] * d2
        dcg_ref[...] = (du * cin).astype(BF16)
        dcin_ref[...] = (du * cg).astype(BF16)

        @pl.when(first)
        def _():
            dw_ref[...] = jnp.zeros_like(dw_ref)

        dw_ref[0:1, :] += jnp.sum(dconv * u2, axis=0, keepdims=True)
        dw_ref[1:2, :] += jnp.sum(dconv * u1, axis=0, keepdims=True)
        dw_ref[2:3, :] += jnp.sum(dconv * u, axis=0, keepdims=True)

    seg = lambda name: pl.BlockSpec((ts, D), lambda i: (i, SEG[name]))
    prev = lambda name: pl.BlockSpec((8, D), lambda i: (jnp.maximum(i * (ts // 8) - 1, 0), SEG[name]))
    nxt = lambda blk: pl.BlockSpec((8, D), lambda i: (jnp.minimum((i + 1) * (ts // 8), S // 8 - 1), blk))
    row = pl.BlockSpec((ts, D), lambda i: (i, 0))
    return pl.pallas_call(
        body, grid=(nblk,),
        in_specs=[row, seg("z_mla"), seg("c_in"), seg("b_gate"), seg("c_gate"), seg("z_conv"), row, row,
                  prev("c_in"), prev("c_gate"), nxt(SEG["b_gate"]), nxt(SEG["z_conv"]), nxt(0), pl.BlockSpec((8, D), lambda i: (0, 0))],
        out_specs=[row, pl.BlockSpec((H, 8, ts), lambda i: (0, 0, i)), row, row, row, row, row, pl.BlockSpec((8, D), lambda i: (0, 0))],
        out_shape=[jax.ShapeDtypeStruct((S, D), BF16), jax.ShapeDtypeStruct((H, 8, S), F32)] + [jax.ShapeDtypeStruct((S, D), BF16)] * 5
        + [jax.ShapeDtypeStruct((8, D), F32)],
        compiler_params=_cparams(("arbitrary",), 56 << 20), name="gates_bwd",
    )(attn, proj, proj, proj, proj, proj, da_mla, da_conv, proj, proj, proj, proj, da_conv, conv_w)


def _flash_bwd(q, k, v, do, lse, delta, tq):
    S = q.shape[1]
    nq = S // tq
    scale = 1.0 / math.sqrt(DN + DR)
    c2 = scale * LOG2E
    tsub = min(256, tq)
    nsub = tq // tsub

    qi_tab, kj_tab = _causal_pairs(nq, key_major=True)

    def body(qi_ref, kj_ref, q_ref, k_ref, v_ref, do_ref, lse_ref, dl_ref, dq_ref, dk_ref, dv_ref, dk_sc, dv_sc):
        p = pl.program_id(1)
        qi, kj = qi_ref[p], kj_ref[p]

        @pl.when((kj == 0) & (qi == 0))
        def _():
            dq_ref[...] = jnp.zeros_like(dq_ref)

        @pl.when(qi == kj)
        def _():
            dk_sc[...] = jnp.zeros_like(dk_sc)
            dv_sc[...] = jnp.zeros_like(dv_sc)

        def step(diag):
            kb, vb = k_ref[0], v_ref[0]

            def first_matmuls(u):
                sub = slice(u * tsub, (u + 1) * tsub)
                st = lax.dot_general(kb, q_ref[0, sub, :], NT, preferred_element_type=F32)
                dpt = lax.dot_general(vb, do_ref[sub, :], NT, preferred_element_type=F32)
                return st, dpt

            nxt = first_matmuls(0)
            for u in range(nsub):
                st, dpt = nxt
                if u + 1 < nsub:
                    nxt = first_matmuls(u + 1)
                sub = slice(u * tsub, (u + 1) * tsub)
                pt = jnp.exp2(st * c2 - lse_ref[0, 0:1, sub] * LOG2E)
                if diag:
                    r = lax.broadcasted_iota(jnp.int32, (tq, tsub), 0)
                    c = lax.broadcasted_iota(jnp.int32, (tq, tsub), 1) + u * tsub
                    pt = jnp.where(r <= c, pt, 0.0)
                dst = (pt * (dpt - dl_ref[0, 0:1, sub])).astype(BF16)
                dv_sc[...] += jnp.dot(pt.astype(BF16), do_ref[sub, :], preferred_element_type=F32)
                dk_sc[...] += jnp.dot(dst, q_ref[0, sub, :], preferred_element_type=F32)
                rows = pl.ds(pl.multiple_of(qi * tq + u * tsub, tsub), tsub)
                dq_ref[0, rows, :] += lax.dot_general(dst, kb, TN, preferred_element_type=F32)

        @pl.when(qi > kj)
        def _():
            step(False)

        @pl.when(qi == kj)
        def _():
            step(True)

        @pl.when(qi == nq - 1)
        def _():
            dk_ref[0] = dk_sc[...] * scale
            dv_ref[0] = dv_sc[...]

        @pl.when((kj == nq - 1) & (qi == nq - 1))
        def _():
            dq_ref[...] = dq_ref[...] * scale

    return pl.pallas_call(
        body,
        grid_spec=pltpu.PrefetchScalarGridSpec(
            num_scalar_prefetch=2,
            grid=(H, len(qi_tab)),
            in_specs=[pl.BlockSpec((1, tq, HW), lambda h, p, qi, kj: (h, qi[p], 0)),
                      pl.BlockSpec((1, tq, HW), lambda h, p, qi, kj: (h, kj[p], 0)),
                      pl.BlockSpec((1, tq, DV), lambda h, p, qi, kj: (h, kj[p], 0)),
                      pl.BlockSpec((tq, DV), lambda h, p, qi, kj: (qi[p], h)),
                      pl.BlockSpec((1, 8, tq), lambda h, p, qi, kj: (h, 0, qi[p])),
                      pl.BlockSpec((1, 8, tq), lambda h, p, qi, kj: (h, 0, qi[p]))],
            out_specs=[pl.BlockSpec((1, S, HW), lambda h, p, qi, kj: (h, 0, 0)),
                       pl.BlockSpec((1, tq, HW), lambda h, p, qi, kj: (h, kj[p], 0)),
                       pl.BlockSpec((1, tq, DV), lambda h, p, qi, kj: (h, kj[p], 0))],
            scratch_shapes=[pltpu.VMEM((tq, HW), F32), pltpu.VMEM((tq, DV), F32)],
        ),
        out_shape=[jax.ShapeDtypeStruct((H, S, HW), F32), jax.ShapeDtypeStruct((H, S, HW), F32), jax.ShapeDtypeStruct((H, S, DV), F32)],
        compiler_params=_cparams(("parallel", "arbitrary"), 48 << 20),
        name="flash_bwd",
    )(qi_tab, kj_tab, q, k, v, do, lse, delta)


def _rms_bwd(xf, g, dn_out):
    r = _rms_scale(xf)
    n = xf * r
    dn = dn_out * g
    dx = r * (dn - n * jnp.mean(dn * n, axis=-1, keepdims=True))
    return dx, jnp.sum(dn_out * n, axis=0, keepdims=True)


def _qkv_bwd(proj, pos, freq, g_qa, g_kva, w_q, w_kv, dq, dk, dv, ts):
    S = proj.shape[0]

    def body(qa_ref, ckv_ref, pos_ref, freq_ref, gq_ref, gk_ref, wq_ref, wkv_ref, dq_ref, dk_ref, dv_ref,
             dqp_ref, dkvp_ref, dqa_ref, dckv_ref, dkr_ref, dgq_ref, dgk_ref):
        i = pl.program_id(0)
        cos, sin, lane = _rope_tables(pos_ref, freq_ref, ts)
        sgn_sin = jnp.where(lane < 32, sin, -sin)
        live = lane < DR
        kr_sum = jnp.zeros((ts, 128), F32)
        for h in range(H):
            dqh = dq_ref[h]
            dqp_ref[:, h * HW:h * HW + DN] = dqh[:, 0:DN].astype(BF16)
            t = dqh[:, DN:HW]
            dqp_ref[:, h * HW + DN:(h + 1) * HW] = jnp.where(live, t * cos + _rope_swap(t, lane) * sgn_sin, 0.0).astype(BF16)
            dkh = dk_ref[h]
            dkvp_ref[:, h * HW:h * HW + DN] = dkh[:, 0:DN].astype(BF16)
            dkvp_ref[:, h * HW + DN:(h + 1) * HW] = dv_ref[h].astype(BF16)
            kr_sum = kr_sum + dkh[:, DN:HW]
        dkr_ref[...] = jnp.where(live, kr_sum * cos + _rope_swap(kr_sum, lane) * sgn_sin, 0.0).astype(BF16)
        dqn = lax.dot_general(dqp_ref[...], wq_ref[...], NT, preferred_element_type=F32)
        dkvn = lax.dot_general(dkvp_ref[...], wkv_ref[...], NT, preferred_element_type=F32)
        dqa, dgq = _rms_bwd(qa_ref[...], gq_ref[...], dqn)
        dckv, dgk = _rms_bwd(ckv_ref[...], gk_ref[...], dkvn)
        dqa_ref[...] = dqa.astype(BF16)
        dckv_ref[...] = dckv.astype(BF16)

        @pl.when(i == 0)
        def _():
            dgq_ref[...] = jnp.zeros_like(dgq_ref)
            dgk_ref[...] = jnp.zeros_like(dgk_ref)

        dgq_ref[0:1, :] += dgq
        dgk_ref[0:1, :] += dgk

    rowb = lambda w, blk: pl.BlockSpec((ts, w), lambda i: (i, blk))
    full = lambda a: pl.BlockSpec(a.shape, lambda i: (0,) * a.ndim)
    heads = lambda w: pl.BlockSpec((H, ts, w), lambda i: (0, i, 0))
    return pl.pallas_call(
        body, grid=(S // ts,),
        in_specs=[rowb(RQ, OFF_QA // RQ), rowb(RKV, OFF_CKV // RKV), pl.BlockSpec((ts, 1), lambda i: (i, 0)), full(freq), full(g_qa), full(g_kva),
                  full(w_q), full(w_kv), heads(HW), heads(HW), heads(DV)],
        out_specs=[rowb(H * HW, 0), rowb(H * HW, 0), rowb(RQ, 0), rowb(RKV, 0), rowb(128, 0),
                   pl.BlockSpec((8, RQ), lambda i: (0, 0)), pl.BlockSpec((8, RKV), lambda i: (0, 0))],
        out_shape=[jax.ShapeDtypeStruct((S, H * HW), BF16), jax.ShapeDtypeStruct((S, H * HW), BF16), jax.ShapeDtypeStruct((S, RQ), BF16),
                   jax.ShapeDtypeStruct((S, RKV), BF16), jax.ShapeDtypeStruct((S, 128), BF16),
                   jax.ShapeDtypeStruct((8, RQ), F32), jax.ShapeDtypeStruct((8, RKV), F32)],
        compiler_params=_cparams(("arbitrary",), 56 << 20), name="qkv_bwd",
    )(proj, proj, pos, freq, g_qa, g_kva, w_q, w_kv, dq, dk, dv)


def _pack_dproj(parts, ts):
    S = parts[0].shape[0]
    widths = [p.shape[1] for p in parts]
    assert sum(widths) == NP

    def body(*refs):
        o_ref = refs[-1]
        off = 0
        for r, w in zip(refs[:-1], widths):
            o_ref[:, off:off + w] = r[...]
            off += w

    return pl.pallas_call(
        body, grid=(S // ts,),
        in_specs=[pl.BlockSpec((ts, w), lambda i: (i, 0)) for w in widths],
        out_specs=pl.BlockSpec((ts, NP), lambda i: (i, 0)),
        out_shape=jax.ShapeDtypeStruct((S, NP), BF16),
        compiler_params=_cparams(("parallel",), 48 << 20), name="pack_dproj",
    )(*parts)


def _prenorm_bwd(x, g, dh, dy, ts):
    S = x.shape[0]

    def body(x_ref, g_ref, dh_ref, dy_ref, gx_ref, dg_ref):
        dx, dg = _rms_bwd(x_ref[...], g_ref[...], dh_ref[...])
        gx_ref[...] = dy_ref[...] + dx

        @pl.when(pl.program_id(0) == 0)
        def _():
            dg_ref[...] = jnp.zeros_like(dg_ref)

        dg_ref[0:1, :] += dg

    row = pl.BlockSpec((ts, D), lambda i: (i, 0))
    return pl.pallas_call(
        body, grid=(S // ts,), in_specs=[row, pl.BlockSpec((1, D), lambda i: (0, 0)), row, row],
        out_specs=[row, pl.BlockSpec((8, D), lambda i: (0, 0))],
        out_shape=[jax.ShapeDtypeStruct((S, D), F32), jax.ShapeDtypeStruct((8, D), F32)],
        compiler_params=_cparams(("arbitrary",), 40 << 20), name="prenorm_bwd",
    )(x, g, dh, dy)


def _local_step(x, pos, target, g_pre, g_qa, g_kva, g_post, w_in, w_q, w_kv, conv_w8, w_o_mla, w_o_conv, w_out):
    S = x.shape[0]
    ts = min(256, S)
    tq = min(512, S)
    tm = min(512, S)
    mm = functools.partial(_matmul, tm=tm)
    freq = _inv_freq_tile()

    h = _prenorm(x, g_pre, ts)
    proj = mm(h, w_in, mode="nn", out_dtype=F32, tn=1408, tk=D, name="mm_proj")
    q, k, v, vt, qn, kvn = _qkv_prep(proj, pos, freq, g_qa, g_kva, w_q, w_kv, ts)
    attn, lse = _flash_fwd(q, k, vt, tq)
    a_mla, a_conv = _gates_fwd(proj, attn, conv_w8, ts)
    y_mla = mm(a_mla, w_o_mla, mode="nn", out_dtype=F32, tn=1024, tk=D, name="mm_y_mla")
    y_conv = mm(a_conv, w_o_conv, mode="nn", out_dtype=F32, tn=1024, tk=D, name="mm_y_conv")
    merged = _merge_fwd(proj, y_mla, y_conv, ts)
    out = mm(merged, w_out, mode="nn", out_dtype=F32, tn=1024, tk=D, name="mm_out")
    dy, dout, dg_post, loss = _post_loss(out, x, target, g_post, ts)

    dmerged = mm(dout, w_out, mode="nt", out_dtype=F32, tn=1024, tk=D, name="mm_dmerged")
    dw_out = _matmul(merged, dout, mode="tn", out_dtype=F32, tm=1024, tn=1024, tk=1024, name="mm_dw_out")
    dy_mla, dy_conv, dg_mla, dg_conv = _merge_bwd(proj, y_mla, y_conv, dmerged, ts)
    da_mla = mm(dy_mla, w_o_mla, mode="nt", out_dtype=F32, tn=1024, tk=D, name="mm_da_mla")
    da_conv = mm(dy_conv, w_o_conv, mode="nt", out_dtype=F32, tn=1024, tk=D, name="mm_da_conv")
    dw_o_mla = _matmul(a_mla, dy_mla, mode="tn", out_dtype=F32, tm=1024, tn=1024, tk=1024, name="mm_dw_o_mla")
    dw_o_conv = _matmul(a_conv, dy_conv, mode="tn", out_dtype=F32, tm=1024, tn=1024, tk=1024, name="mm_dw_o_conv")
    dattn, delta, dz_mla, dc_in, db_gate, dc_gate, dz_conv, dconv_w = _gates_bwd(proj, attn, da_mla, da_conv, conv_w8, min(128, S))
    dq, dk, dv = _flash_bwd(q, k, v, dattn, lse, delta, tq)
    dqp, dkvp, dq_a, dc_kv, dk_rope, dg_qa, dg_kva = _qkv_bwd(proj, pos, freq, g_qa, g_kva, w_q, w_kv, dq, dk, dv, min(128, S))
    dw_q = _matmul(qn, dqp, mode="tn", out_dtype=F32, tm=RQ, tn=1024, tk=1024, name="mm_dw_q")
    dw_kv = _matmul(kvn, dkvp, mode="tn", out_dtype=F32, tm=RKV, tn=1024, tk=1024, name="mm_dw_kv")
    dproj = _pack_dproj([dz_mla, dc_in, db_gate, dc_gate, dz_conv, dg_mla, dg_conv, dq_a, dc_kv, dk_rope], ts)
    dh = mm(dproj, w_in, mode="nt", out_dtype=F32, tn=D, tk=1408, name="mm_dh")
    dw_in = _matmul(h, dproj, mode="tn", out_dtype=F32, tm=1024, tn=1408, tk=1024, name="mm_dw_in")
    grad_x, dg_pre = _prenorm_bwd(x, g_pre, dh, dy, ts)
    return dict(loss=loss, grad_x=grad_x, dg_pre=dg_pre, dg_qa=dg_qa, dg_kva=dg_kva, dg_post=dg_post, dw_in=dw_in, dw_q=dw_q,
                dw_kv=dw_kv, dconv_w=dconv_w, dw_o_mla=dw_o_mla, dw_o_conv=dw_o_conv, dw_out=dw_out)


def _my_id():
    return lax.axis_index("x") * 4 + lax.axis_index("y") * 2 + lax.axis_index("c")


def _place():
    x, y, c = lax.axis_index("x"), lax.axis_index("y"), lax.axis_index("c")
    return (x, y, c), (x, y, 1 - c), [(1 - x, y), (x, 1 - y), (1 - x, 1 - y)]


def _slot(px, py, pc):
    return 4 * px + 2 * py + pc


def _all_gather(arrays, name):
    n = len(arrays)

    def body(*refs):
        ins, outs = refs[:n], refs[n:2 * n]
        send_sems, recv_sems, local_sems = refs[2 * n:]
        me, sib, chips = _place()
        c = me[2]

        def copy(a, k, block, to, src=None):
            rows = outs[a].at[_slot(*block)]
            return pltpu.make_async_remote_copy(src_ref=rows if src is None else src, dst_ref=rows, send_sem=send_sems.at[a, k],
                                                recv_sem=recv_sems.at[a, k], device_id=to, device_id_type=MESH)

        local = [pltpu.make_async_copy(ins[a], outs[a].at[_slot(*me)], local_sems.at[a]) for a in range(n)]
        for cp in local:
            cp.start()
        sends = []
        for a in range(n):
            sends.append(copy(a, 0, me, sib, src=ins[a]))
            sends += [copy(a, 1 + j, me, (*chip, c), src=ins[a]) for j, chip in enumerate(chips)]
        for cp in sends:
            cp.start()
        for a in range(n):
            for j, chip in enumerate(chips):
                copy(a, 1 + j, (*chip, c), me).wait_recv()
                fwd = copy(a, 4 + j, (*chip, c), sib)
                fwd.start()
                sends.append(fwd)
        for a in range(n):
            copy(a, 0, sib, me).wait_recv()
            for j, chip in enumerate(chips):
                copy(a, 4 + j, (*chip, 1 - c), me).wait_recv()
        for cp in sends:
            cp.wait_send()
        for cp in local:
            cp.wait()

    anyspec = pl.BlockSpec(memory_space=pl.ANY)
    return pl.pallas_call(
        body,
        in_specs=[anyspec] * n,
        out_specs=[anyspec] * n,
        out_shape=[jax.ShapeDtypeStruct((NDEV,) + a.shape, a.dtype) for a in arrays],
        scratch_shapes=[pltpu.SemaphoreType.DMA((n, NDEV - 1)), pltpu.SemaphoreType.DMA((n, NDEV - 1)), pltpu.SemaphoreType.DMA((n,))],
        name=name,
    )(*arrays)


def _exchange(arrays, nslots, pick, name):
    n = len(arrays)

    def body(*refs):
        ins, outs = refs[:n], refs[n:2 * n]
        send_sems, recv_sems = refs[2 * n:]
        sends = []
        for a in range(n):
            for k in range(nslots):
                slab, to = pick(k)
                cp = pltpu.make_async_remote_copy(src_ref=ins[a].at[slab], dst_ref=outs[a].at[k], send_sem=send_sems.at[a, k],
                                                  recv_sem=recv_sems.at[a, k], device_id=to, device_id_type=MESH)
                cp.start()
                sends.append(cp)
        for cp in sends:
            cp.wait_recv()
        for cp in sends:
            cp.wait_send()

    anyspec = pl.BlockSpec(memory_space=pl.ANY)
    return pl.pallas_call(
        body,
        in_specs=[anyspec] * n,
        out_specs=[anyspec] * n,
        out_shape=[jax.ShapeDtypeStruct((nslots,) + a.shape[1:], a.dtype) for a in arrays],
        scratch_shapes=[pltpu.SemaphoreType.DMA((n, nslots)), pltpu.SemaphoreType.DMA((n, nslots))],
        name=name,
    )(*arrays)


def _slabs_bf16(x, rows_per_block, name):
    n, R, C = x.shape
    tr = min(rows_per_block, R)

    def body(x_ref, o_ref):
        o_ref[...] = x_ref[...].astype(BF16)

    blk = pl.BlockSpec((1, tr, C), lambda k, i: (k, i, 0))
    return pl.pallas_call(body, grid=(n, R // tr), in_specs=[blk], out_specs=blk, out_shape=jax.ShapeDtypeStruct(x.shape, BF16),
                          compiler_params=_cparams(("parallel", "parallel"), 32 << 20), name=name)(x)


def _to_sibling(k):
    me, sib, chips = _place()
    dest = sib if k == 0 else (*chips[k - 1], sib[2])
    return _slot(*dest), sib


def _to_chips(k):
    me, sib, chips = _place()
    return k, (*chips[k], me[2])


def _chip_sums(own, got, slots, rows_per_block, name):
    _, R, C = own.shape
    tr = min(rows_per_block, R)

    def body(slots_ref, own_ref, got_ref, o_ref):
        o_ref[0] = (own_ref[0] + got_ref[0].astype(F32)).astype(BF16)

    return pl.pallas_call(
        body,
        grid_spec=pltpu.PrefetchScalarGridSpec(
            num_scalar_prefetch=1, grid=(3, R // tr),
            in_specs=[pl.BlockSpec((1, tr, C), lambda j, i, s: (s[1 + j], i, 0)), pl.BlockSpec((1, tr, C), lambda j, i, s: (1 + j, i, 0))],
            out_specs=pl.BlockSpec((1, tr, C), lambda j, i, s: (j, i, 0))),
        out_shape=jax.ShapeDtypeStruct((3, R, C), BF16),
        compiler_params=_cparams(("parallel", "parallel"), 32 << 20), name=name,
    )(slots, own, got)


def _adamw_math(w, g, m, v):
    m = ADAM_B1 * m + (1.0 - ADAM_B1) * g
    v = ADAM_B2 * v + (1.0 - ADAM_B2) * (g * g)
    m_hat = m / (1.0 - ADAM_B1 ** ADAM_STEP)
    v_hat = v / (1.0 - ADAM_B2 ** ADAM_STEP)
    delta = -ADAM_LR * (m_hat / (jnp.sqrt(v_hat) + ADAM_EPS) + ADAM_WD * w)
    return delta, m, v


def _reduce_adamw(own, got1, got2, slots, w, m, v, rows_per_block, name):
    _, R, C = own.shape
    tr = min(rows_per_block, R)
    assert R % tr == 0

    def body(slots_ref, own_ref, g1_ref, g2_ref, w_ref, m_ref, v_ref, g_ref, d_ref, nm_ref, nv_ref):
        g = own_ref[0] + g1_ref[0].astype(F32)
        for j in range(3):
            g = g + g2_ref[j].astype(F32)
        g_ref[...] = g
        d, nm, nv = _adamw_math(w_ref[...], g, m_ref[...], v_ref[...])
        d_ref[...] = d
        nm_ref[...] = nm
        nv_ref[...] = nv

    blk = pl.BlockSpec((tr, C), lambda i, s: (i, 0))
    return pl.pallas_call(
        body,
        grid_spec=pltpu.PrefetchScalarGridSpec(
            num_scalar_prefetch=1, grid=(R // tr,),
            in_specs=[pl.BlockSpec((1, tr, C), lambda i, s: (s[0], i, 0)), pl.BlockSpec((1, tr, C), lambda i, s: (0, i, 0)),
                      pl.BlockSpec((3, tr, C), lambda i, s: (0, i, 0)), blk, blk, blk],
            out_specs=[blk] * 4),
        out_shape=[jax.ShapeDtypeStruct((R, C), F32)] * 4,
        compiler_params=_cparams(("parallel",), 48 << 20), name=name,
    )(slots, own, got1, got2, w, m, v)


def _sum_adamw(parts, w, m, v, rows_per_block, name):
    n, R, C = parts.shape
    tr = min(rows_per_block, R)
    assert R % tr == 0

    def body(p_ref, w_ref, m_ref, v_ref, g_ref, d_ref, nm_ref, nv_ref):
        g = p_ref[0]
        for j in range(1, n):
            g = g + p_ref[j]
        g_ref[...] = g
        d, nm, nv = _adamw_math(w_ref[...], g, m_ref[...], v_ref[...])
        d_ref[...] = d
        nm_ref[...] = nm
        nv_ref[...] = nv

    blk = pl.BlockSpec((tr, C), lambda i: (i, 0))
    return pl.pallas_call(
        body, grid=(R // tr,),
        in_specs=[pl.BlockSpec((n, tr, C), lambda i: (0, i, 0)), blk, blk, blk],
        out_specs=[blk] * 4,
        out_shape=[jax.ShapeDtypeStruct((R, C), F32)] * 4,
        compiler_params=_cparams(("parallel",), 48 << 20), name=name,
    )(parts, w, m, v)


def kernel(x, positions, pre_norm_g, w_in, q_a_norm_g, w_q_b, kv_a_norm_g, w_kv_b, conv_w, w_o_mla, w_o_conv, w_out, post_norm_g, loss_target, m_pre_norm_g, m_w_in, m_q_a_norm_g, m_w_q_b, m_kv_a_norm_g, m_w_kv_b, m_conv_w, m_w_o_mla, m_w_o_conv, m_w_out, m_post_norm_g, v_pre_norm_g, v_w_in, v_q_a_norm_g, v_w_q_b, v_kv_a_norm_g, v_w_kv_b, v_conv_w, v_w_o_mla, v_w_o_conv, v_w_out, v_post_norm_g):
    S = x.shape[1]
    conv_pad = jnp.zeros((8, 256), F32).at[0:3, :].set(conv_w)
    g_in, g_q, g_kv, g_om, g_oc, g_out, g_cw = _all_gather(
        [w_in.astype(BF16), w_q_b.astype(BF16), w_kv_b.astype(BF16), w_o_mla.astype(BF16), w_o_conv.astype(BF16), w_out.astype(BF16),
         conv_pad], "all_gather_weights")
    w_in_f = _assemble_w_in(g_in)
    w_q_f = _assemble_w_q(g_q)
    w_kv_f = _concat_cols(g_kv, BF16, "assemble_w_kv")
    conv_w8 = _concat_cols(g_cw, F32, "assemble_conv_w")
    row2 = lambda a: a.reshape(1, -1)
    r = _local_step(x[0], positions.reshape(S, 1), loss_target[0], row2(pre_norm_g), row2(q_a_norm_g), row2(kv_a_norm_g),
                    row2(post_norm_g), w_in_f, w_q_f, w_kv_f, conv_w8, g_om.reshape(D, D), g_oc.reshape(D, D), g_out.reshape(D, D))
    small = jnp.concatenate([r["dg_pre"][0:1], r["dg_post"][0:1], jnp.pad(r["dg_qa"][0:1], ((0, 0), (0, D - RQ))),
                             jnp.pad(r["dg_kva"][0:1], ((0, 0), (0, D - RKV))), jnp.pad(r["loss"][0:1], ((0, 0), (0, D - 128))),
                             jnp.zeros((3, D), F32)], axis=0)
    gnames = ["w_in", "w_q", "w_kv", "conv_w", "w_o_mla", "w_o_conv", "w_out"]
    own = [_split_dw_in(r["dw_in"]), _split_dw_q(r["dw_q"]), _split_cols(r["dw_kv"], NDEV, "split_dw_kv"),
           _split_cols(r["dconv_w"], NDEV, "split_dconv_w"), r["dw_o_mla"].reshape(NDEV, D // NDEV, D),
           r["dw_o_conv"].reshape(NDEV, D // NDEV, D), r["dw_out"].reshape(NDEV, D // NDEV, D)]
    (mx, my, mc), _, chips = _place()
    slots = jnp.stack([_slot(mx, my, mc)] + [_slot(cx, cy, mc) for cx, cy in chips]).astype(jnp.int32)
    got1 = _exchange([_slabs_bf16(g, 128, "bf16_" + nm) for g, nm in zip(own, gnames)], 4, _to_sibling, "grads_to_sibling")
    sums = [_chip_sums(o, g1, slots, 128, "chip_sum_" + nm) for o, g1, nm in zip(own, got1, gnames)]
    got2 = _exchange(sums, 3, _to_chips, "grads_to_chips")
    p_small, = _all_gather([small], "all_gather_small")
    pad8 = lambda a: jnp.zeros((8, 256), F32).at[0:3, :].set(a)
    params = [(w_in, m_w_in, v_w_in), (w_q_b, m_w_q_b, v_w_q_b), (w_kv_b, m_w_kv_b, v_w_kv_b), (conv_pad, pad8(m_conv_w), pad8(v_conv_w)),
              (w_o_mla, m_w_o_mla, v_w_o_mla), (w_o_conv, m_w_o_conv, v_w_o_conv), (w_out, m_w_out, v_w_out)]
    o_in, o_q, o_kv, o_cw, o_om, o_oc, o_out = [
        _reduce_adamw(o, g1, g2, slots, w, m, v, 128, "adamw_" + nm)
        for o, g1, g2, (w, m, v), nm in zip(own, got1, got2, params, gnames)]
    o_cw = [a[0:3] for a in o_cw]
    padv = lambda a: jnp.pad(a.reshape(1, -1), ((0, 0), (0, D - a.shape[0])))
    stack = lambda pre, post, qa, kva: jnp.concatenate([row2(pre), row2(post), padv(qa), padv(kva), jnp.zeros((4, D), F32)], axis=0)
    o_g = _sum_adamw(p_small, stack(pre_norm_g, post_norm_g, q_a_norm_g, kv_a_norm_g),
                     stack(m_pre_norm_g, m_post_norm_g, m_q_a_norm_g, m_kv_a_norm_g),
                     stack(v_pre_norm_g, v_post_norm_g, v_q_a_norm_g, v_kv_a_norm_g), 8, "adamw_gains")
    loss = o_g[0][4, 0]
    outs = {}
    for idx, kind in enumerate(("grad", "delta", "new_m", "new_v")):
        o = o_g[idx]
        outs[kind] = dict(pre_norm_g=o[0], w_in=o_in[idx], q_a_norm_g=o[2, 0:RQ], w_q_b=o_q[idx], kv_a_norm_g=o[3, 0:RKV],
                          w_kv_b=o_kv[idx], conv_w=o_cw[idx], w_o_mla=o_om[idx], w_o_conv=o_oc[idx], w_out=o_out[idx], post_norm_g=o[1])
    names = ["pre_norm_g", "w_in", "q_a_norm_g", "w_q_b", "kv_a_norm_g", "w_kv_b", "conv_w", "w_o_mla", "w_o_conv", "w_out", "post_norm_g"]
    return (loss, r["grad_x"][None], *[outs["grad"][n] for n in names], *[outs["delta"][n] for n in names],
            *[outs["new_m"][n] for n in names], *[outs["new_v"][n] for n in names])
```

```python
import functools
import math

import jax
import jax.numpy as jnp
from jax import lax
from jax.experimental import pallas as pl
from jax.experimental.pallas import tpu as pltpu

F32 = jnp.float32
BF16 = jnp.bfloat16

NDEV = 8
D = 2048
H = 16
DN = 128
DR = 64
DV = 128
RQ = 512
RKV = 512
HW = 256
ROPE_THETA = 10000.0
RMS_EPS = 1e-6
N_IN = 15424
SHARD_IN = N_IN // NDEV
SMALL = RQ + RKV + DR
NP = 7 * D + RQ + RKV + 128
SEG = dict(z_mla=0, c_in=1, b_gate=2, c_gate=3, z_conv=4, g_mla=5, g_conv=6)
OFF_QA = 7 * D
OFF_CKV = OFF_QA + RQ
OFF_KR = OFF_CKV + RKV
EXT = 2176
VMEM_CAP = 56 * 1024 * 1024

ADAM_LR = 0.001
ADAM_B1 = 0.9
ADAM_B2 = 0.999
ADAM_EPS = 1e-08
ADAM_WD = 0.01
ADAM_STEP = 10

LOG2E = math.log2(math.e)
NN = (((1,), (0,)), ((), ()))
NT = (((1,), (1,)), ((), ()))
TN = (((0,), (0,)), ((), ()))
MESH = pl.DeviceIdType.MESH


def _cparams(sem, vmem_bytes):
    return pltpu.CompilerParams(dimension_semantics=sem, vmem_limit_bytes=int(min(VMEM_CAP, max(vmem_bytes, 16 << 20))))


def _nbytes(shape, dtype):
    return math.prod(shape) * jnp.dtype(dtype).itemsize


def _matmul(a, b, *, mode, out_dtype, tm, tn, tk, name, m_outer=False):
    if mode == "nn":
        (M, K), (K2, N) = a.shape, b.shape
    elif mode == "nt":
        (M, K), (N, K2) = a.shape, b.shape
    else:
        (K, M), (K2, N) = a.shape, b.shape
    assert K == K2, (a.shape, b.shape, mode)
    tm, tn, tk = min(tm, M), min(tn, N), min(tk, K)
    assert M % tm == 0 and N % tn == 0 and K % tk == 0, (M, N, K, tm, tn, tk)
    ni, nj, nk = M // tm, N // tn, K // tk
    dims = dict(nn=NN, nt=NT, tn=TN)[mode]

    if m_outer:
        grid = (ni, nj, nk)
        ij = lambda g0, g1: (g0, g1)
    else:
        grid = (nj, ni, nk)
        ij = lambda g0, g1: (g1, g0)

    if mode == "tn":
        a_spec = pl.BlockSpec((tk, tm), lambda g0, g1, k: (k, ij(g0, g1)[0]))
        a_tile = (tk, tm)
    else:
        a_spec = pl.BlockSpec((tm, tk), lambda g0, g1, k: (ij(g0, g1)[0], k))
        a_tile = (tm, tk)
    if mode == "nt":
        b_spec = pl.BlockSpec((tn, tk), lambda g0, g1, k: (ij(g0, g1)[1], k))
    else:
        b_spec = pl.BlockSpec((tk, tn), lambda g0, g1, k: (k, ij(g0, g1)[1]))
    o_spec = pl.BlockSpec((tm, tn), lambda g0, g1, k: ij(g0, g1))

    def body(a_ref, b_ref, o_ref, *scratch):
        prod = lax.dot_general(a_ref[...], b_ref[...], dims, preferred_element_type=F32)
        if nk == 1:
            o_ref[...] = prod.astype(o_ref.dtype)
        else:
            acc_ref, = scratch
            k = pl.program_id(2)

            @pl.when(k == 0)
            def _():
                acc_ref[...] = prod

            @pl.when(k > 0)
            def _():
                acc_ref[...] += prod

            @pl.when(k == nk - 1)
            def _():
                o_ref[...] = acc_ref[...].astype(o_ref.dtype)

    vmem = 2 * (_nbytes(a_tile, a.dtype) + _nbytes((tk, tn), b.dtype) + _nbytes((tm, tn), out_dtype)) + 2 * _nbytes((tm, tn), F32)
    return pl.pallas_call(
        body,
        grid=grid,
        in_specs=[a_spec, b_spec],
        out_specs=o_spec,
        out_shape=jax.ShapeDtypeStruct((M, N), out_dtype),
        scratch_shapes=[] if nk == 1 else [pltpu.VMEM((tm, tn), F32)],
        compiler_params=_cparams(("parallel", "parallel", "arbitrary"), vmem + (8 << 20)),
        name=name,
    )(a, b)


def _in_dest(k):
    return SHARD_IN * k - SMALL


def _assemble_w_in(g):
    R = 128

    def body(g_ref, o_ref, ext, acc):
        ext[...] = jnp.zeros_like(ext)
        acc[...] = jnp.zeros_like(acc)
        lane = lax.broadcasted_iota(jnp.int32, (R, EXT), 1)
        ext[:, 0:SHARD_IN] = g_ref[0].astype(F32)
        v = ext[...]
        acc[:, OFF_QA:NP] = jnp.where(lane[:, 0:NP - OFF_QA] < SMALL, v[:, 0:NP - OFF_QA], 0.0)
        w = v[:, 1024:2048]
        w = pltpu.roll(w, 1024 - 64, 1)
        acc[:, 0:1024] = jnp.where(lane[:, 0:1024] < SHARD_IN - SMALL, w, 0.0)
        for k in range(1, NDEV):
            ext[:, 0:SHARD_IN] = g_ref[k].astype(F32)
            dest = _in_dest(k)
            t, o = dest // 128, dest % 128
            width = -(-(o + SHARD_IN) // 128) * 128
            v = pltpu.roll(ext[...], o, 1)[:, 0:width]
            acc[:, 128 * t:128 * t + width] += v
        o_ref[...] = acc[...].astype(BF16)

    return pl.pallas_call(
        body,
        grid=(D // R,),
        in_specs=[pl.BlockSpec((NDEV, R, SHARD_IN), lambda i: (0, i, 0))],
        out_specs=pl.BlockSpec((R, NP), lambda i: (i, 0)),
        out_shape=jax.ShapeDtypeStruct((D, NP), BF16),
        scratch_shapes=[pltpu.VMEM((R, EXT), F32), pltpu.VMEM((R, NP), F32)],
        compiler_params=_cparams(("parallel",), 40 << 20),
        name="assemble_w_in",
    )(g)


def _split_dw_in(dw):
    R = 128

    def body(dw_ref, o_ref):
        lane = lax.broadcasted_iota(jnp.int32, (R, 1024), 1)
        o_ref[0, :, 0:1024] = dw_ref[:, OFF_QA:OFF_QA + 1024]
        tail = dw_ref[:, OFF_QA + 1024:NP]
        tail = jnp.concatenate([tail, jnp.zeros((R, 1024 - 128), F32)], axis=1)
        head = dw_ref[:, 0:1024]
        mixed = jnp.where(lane < 64, tail, pltpu.roll(head, 64, 1))
        o_ref[0, :, 1024:SHARD_IN] = mixed[:, 0:SHARD_IN - 1024]
        for k in range(1, NDEV):
            dest = _in_dest(k)
            t, o = dest // 128, dest % 128
            width = -(-(o + SHARD_IN) // 128) * 128
            v = dw_ref[:, 128 * t:128 * t + width]
            v = pltpu.roll(v, width - o, 1)
            o_ref[k] = v[:, 0:SHARD_IN]

    return pl.pallas_call(
        body,
        grid=(D // R,),
        in_specs=[pl.BlockSpec((R, NP), lambda i: (i, 0))],
        out_specs=pl.BlockSpec((NDEV, R, SHARD_IN), lambda i: (0, i, 0)),
        out_shape=jax.ShapeDtypeStruct((NDEV, D, SHARD_IN), F32),
        compiler_params=_cparams(("parallel",), 40 << 20),
        name="split_dw_in",
    )(dw)


def _assemble_w_q(g):
    def body(g_ref, o_ref):
        lane = lax.broadcasted_iota(jnp.int32, (RQ, 128), 1)
        lo = lane < 64
        for k in range(NDEV):
            t0 = g_ref[k, :, 0:128].astype(F32)
            t1 = g_ref[k, :, 128:256].astype(F32)
            t2 = g_ref[k, :, 256:384].astype(F32)
            base = 2 * k * HW
            o_ref[:, base:base + 128] = t0.astype(BF16)
            o_ref[:, base + 128:base + 256] = jnp.where(lo, t1, 0.0).astype(BF16)
            o_ref[:, base + 256:base + 384] = pltpu.roll(jnp.where(lo, t2, t1), 64, 1).astype(BF16)
            o_ref[:, base + 384:base + 512] = jnp.where(lo, pltpu.roll(t2, 64, 1), 0.0).astype(BF16)

    return pl.pallas_call(
        body,
        out_shape=jax.ShapeDtypeStruct((RQ, H * HW), BF16),
        compiler_params=_cparams(None, 32 << 20),
        name="assemble_w_q",
    )(g)


def _split_dw_q(dw):
    def body(dw_ref, o_ref):
        lane = lax.broadcasted_iota(jnp.int32, (RQ, 128), 1)
        lo = lane < 64
        for k in range(NDEV):
            base = 2 * k * HW
            a = dw_ref[:, base:base + 128]
            b = dw_ref[:, base + 128:base + 256]
            c = pltpu.roll(dw_ref[:, base + 256:base + 384], 64, 1)
            d = pltpu.roll(dw_ref[:, base + 384:base + 512], 64, 1)
            o_ref[k, :, 0:128] = a
            o_ref[k, :, 128:256] = jnp.where(lo, b, c)
            o_ref[k, :, 256:384] = jnp.where(lo, c, d)

    return pl.pallas_call(
        body,
        out_shape=jax.ShapeDtypeStruct((NDEV, RQ, 384), F32),
        compiler_params=_cparams(None, 32 << 20),
        name="split_dw_q",
    )(dw)


def _concat_cols(g, dtype, name):
    n, R, C = g.shape

    def body(g_ref, o_ref):
        for k in range(n):
            o_ref[:, k * C:(k + 1) * C] = g_ref[k].astype(dtype)

    return pl.pallas_call(body, out_shape=jax.ShapeDtypeStruct((R, n * C), dtype), compiler_params=_cparams(None, 32 << 20), name=name)(g)


def _split_cols(x, n, name):
    R, NC = x.shape
    C = NC // n

    def body(x_ref, o_ref):
        for k in range(n):
            o_ref[k] = x_ref[:, k * C:(k + 1) * C]

    return pl.pallas_call(body, out_shape=jax.ShapeDtypeStruct((n, R, C), x.dtype), compiler_params=_cparams(None, 32 << 20), name=name)(x)


def _rms_scale(xf):
    return lax.rsqrt(jnp.mean(xf * xf, axis=-1, keepdims=True) + RMS_EPS)


def _prenorm(x, g, ts):
    S = x.shape[0]

    def body(x_ref, g_ref, h_ref):
        xf = x_ref[...]
        h_ref[...] = (xf * _rms_scale(xf) * g_ref[...]).astype(BF16)

    return pl.pallas_call(
        body,
        grid=(S // ts,),
        in_specs=[pl.BlockSpec((ts, D), lambda i: (i, 0)), pl.BlockSpec((1, D), lambda i: (0, 0))],
        out_specs=pl.BlockSpec((ts, D), lambda i: (i, 0)),
        out_shape=jax.ShapeDtypeStruct((S, D), BF16),
        compiler_params=_cparams(("parallel",), 32 << 20),
        name="prenorm",
    )(x, g)


def _inv_freq_tile():
    inv_freq = ROPE_THETA ** (-jnp.arange(0, DR, 2, dtype=F32) / DR)
    return jnp.tile(inv_freq, 4).reshape(1, 128)


def _rope_tables(pos_ref, freq_ref, ts):
    lane = lax.broadcasted_iota(jnp.int32, (ts, 128), 1)
    ang = pos_ref[...].astype(F32) * freq_ref[...]
    return jnp.cos(ang), jnp.sin(ang), lane


def _rope_swap(t, lane):
    return jnp.where(lane < 32, pltpu.roll(t, 96, 1), pltpu.roll(t, 32, 1))


def _qkv_prep(proj, pos, freq, g_qa, g_kva, w_q, w_kv, ts):
    S = proj.shape[0]

    def body(qa_ref, ckv_ref, kr_ref, pos_ref, freq_ref, gq_ref, gk_ref, wq_ref, wkv_ref, q_ref, k_ref, v_ref, vt_ref, qn_ref, kvn_ref):
        qa = qa_ref[...]
        qn = (qa * _rms_scale(qa) * gq_ref[...]).astype(BF16)
        ckv = ckv_ref[...]
        kvn = (ckv * _rms_scale(ckv) * gk_ref[...]).astype(BF16)
        qn_ref[...] = qn
        kvn_ref[...] = kvn
        cos, sin, lane = _rope_tables(pos_ref, freq_ref, ts)
        sgn_sin = jnp.where(lane < 32, -sin, sin)
        live = lane < DR
        kr = kr_ref[...]
        kr = jnp.where(live, kr * cos + _rope_swap(kr, lane) * sgn_sin, 0.0).astype(BF16)
        qf = jnp.dot(qn, wq_ref[...], preferred_element_type=F32)
        kvf = jnp.dot(kvn, wkv_ref[...], preferred_element_type=F32)
        for h in range(H):
            q_ref[h, :, 0:DN] = qf[:, h * HW:h * HW + DN].astype(BF16)
            t = qf[:, h * HW + DN:(h + 1) * HW]
            q_ref[h, :, DN:HW] = jnp.where(live, t * cos + _rope_swap(t, lane) * sgn_sin, 0.0).astype(BF16)
            k_ref[h, :, 0:DN] = kvf[:, h * HW:h * HW + DN].astype(BF16)
            k_ref[h, :, DN:HW] = kr
            vh = kvf[:, h * HW + DN:(h + 1) * HW]
            v_ref[h] = vh.astype(BF16)
            vt_ref[h] = jnp.transpose(vh).astype(BF16)

    row = lambda w, blk: pl.BlockSpec((ts, w), lambda i: (i, blk))
    full = lambda a: pl.BlockSpec(a.shape, lambda i: (0,) * a.ndim)
    return pl.pallas_call(
        body,
        grid=(S // ts,),
        in_specs=[row(RQ, OFF_QA // RQ), row(RKV, OFF_CKV // RKV), row(128, OFF_KR // 128),
                  pl.BlockSpec((ts, 1), lambda i: (i, 0)), full(freq), full(g_qa), full(g_kva), full(w_q), full(w_kv)],
        out_specs=[pl.BlockSpec((H, ts, HW), lambda i: (0, i, 0)), pl.BlockSpec((H, ts, HW), lambda i: (0, i, 0)),
                   pl.BlockSpec((H, ts, DV), lambda i: (0, i, 0)), pl.BlockSpec((H, DV, ts), lambda i: (0, 0, i)),
                   pl.BlockSpec((ts, RQ), lambda i: (i, 0)), pl.BlockSpec((ts, RKV), lambda i: (i, 0))],
        out_shape=[jax.ShapeDtypeStruct((H, S, HW), BF16), jax.ShapeDtypeStruct((H, S, HW), BF16),
                   jax.ShapeDtypeStruct((H, S, DV), BF16), jax.ShapeDtypeStruct((H, DV, S), BF16),
                   jax.ShapeDtypeStruct((S, RQ), BF16), jax.ShapeDtypeStruct((S, RKV), BF16)],
        compiler_params=_cparams(("parallel",), 48 << 20),
        name="qkv_prep",
    )(proj, proj, proj, pos, freq, g_qa, g_kva, w_q, w_kv)


def _col_to_row8(col, n):
    return jnp.transpose(jnp.broadcast_to(col, (n, 128)))[0:8, :]


def _causal_pairs(n, key_major):
    if key_major:
        pairs = [(i, j) for j in range(n) for i in range(j, n)]
    else:
        pairs = [(i, j) for i in range(n) for j in range(i + 1)]
    return jnp.array([p[0] for p in pairs], jnp.int32), jnp.array([p[1] for p in pairs], jnp.int32)


def _flash_fwd(q, k, vt, tq, hb=8):
    S = q.shape[1]
    nq = S // tq
    scale = 1.0 / math.sqrt(DN + DR)
    c2 = scale * math.log2(math.e)
    tsub = tq
    nsub = tq // tsub

    qi_tab, kj_tab = _causal_pairs(nq, key_major=False)

    def body(qi_ref, kj_ref, q_ref, k_ref, vt_ref, o_ref, lse_ref, m_sc, l_sc, acc_sc):
        p = pl.program_id(1)
        qi, kj = qi_ref[p], kj_ref[p]

        @pl.when(kj == 0)
        def _():
            m_sc[...] = jnp.full_like(m_sc, -jnp.inf)
            l_sc[...] = jnp.zeros_like(l_sc)
            acc_sc[...] = jnp.zeros_like(acc_sc)

        def step(diag):
            chains = [(h, u) for h in range(hb) for u in range(nsub)]

            def scores(h, u):
                return lax.dot_general(k_ref[h], q_ref[h, u * tsub:(u + 1) * tsub, :], NT, preferred_element_type=F32)

            st_next = scores(*chains[0])
            for ci, (h, u) in enumerate(chains):
                st = st_next
                if ci + 1 < len(chains):
                    st_next = scores(*chains[ci + 1])
                cols = slice(u * tsub, (u + 1) * tsub)
                if diag:
                    r = lax.broadcasted_iota(jnp.int32, (tq, tsub), 0)
                    c = lax.broadcasted_iota(jnp.int32, (tq, tsub), 1) + u * tsub
                    st = jnp.where(r <= c, st, -jnp.inf)
                m_prev = m_sc[h, 0:1, cols]
                m_new = jnp.maximum(m_prev, jnp.max(st, axis=0, keepdims=True))
                alpha = jnp.exp2((m_prev - m_new) * c2)
                pt = jnp.exp2((st - m_new) * c2)
                l_sc[h, :, cols] = jnp.broadcast_to(alpha * l_sc[h, 0:1, cols] + jnp.sum(pt, axis=0, keepdims=True), (8, tsub))
                m_sc[h, :, cols] = jnp.broadcast_to(m_new, (8, tsub))
                acc_sc[h, :, cols] = alpha * acc_sc[h, :, cols] + jnp.dot(vt_ref[h], pt.astype(BF16), preferred_element_type=F32)

        @pl.when(kj < qi)
        def _():
            step(False)

        @pl.when(kj == qi)
        def _():
            step(True)
            for h in range(hb):
                l = l_sc[h, 0:1, :]
                o_ref[:, h * DV:(h + 1) * DV] = jnp.transpose(acc_sc[h] / l)
                lse_ref[h] = m_sc[h] * scale + jnp.log(l_sc[h])

    return pl.pallas_call(
        body,
        grid_spec=pltpu.PrefetchScalarGridSpec(
            num_scalar_prefetch=2,
            grid=(H // hb, len(qi_tab)),
            in_specs=[pl.BlockSpec((hb, tq, HW), lambda h, p, qi, kj: (h, qi[p], 0)),
                      pl.BlockSpec((hb, tq, HW), lambda h, p, qi, kj: (h, kj[p], 0)),
                      pl.BlockSpec((hb, DV, tq), lambda h, p, qi, kj: (h, 0, kj[p]))],
            out_specs=[pl.BlockSpec((tq, hb * DV), lambda h, p, qi, kj: (qi[p], h)),
                       pl.BlockSpec((hb, 8, tq), lambda h, p, qi, kj: (h, 0, qi[p]))],
            scratch_shapes=[pltpu.VMEM((hb, 8, tq), F32), pltpu.VMEM((hb, 8, tq), F32), pltpu.VMEM((hb, DV, tq), F32)],
        ),
        out_shape=[jax.ShapeDtypeStruct((S, H * DV), F32), jax.ShapeDtypeStruct((H, 8, S), F32)],
        compiler_params=_cparams(("parallel", "arbitrary"), 40 << 20),
        name="flash_fwd",
    )(qi_tab, kj_tab, q, k, vt)


def _sigmoid(x):
    return 1.0 / (1.0 + jnp.exp(-x))


def _shift_rows(u, prev8, n, first):
    ts = u.shape[0]
    row = lax.broadcasted_iota(jnp.int32, u.shape, 0)
    out = pltpu.roll(u, n, 0)
    for j in range(n):
        halo = jnp.where(first, 0.0, prev8[8 - n + j:8 - n + j + 1, :])
        out = jnp.where(row == j, halo, out)
    return out


def _gates_fwd(proj, attn, conv_w, ts):
    S = proj.shape[0]

    def body(attn_ref, zm_ref, cin_ref, bg_ref, cg_ref, zc_ref, cin_p, cg_p, w_ref, am_ref, ac_ref):
        first = pl.program_id(0) == 0
        zm = zm_ref[...]
        am_ref[...] = (attn_ref[...] * (zm * _sigmoid(zm))).astype(BF16)
        u = cg_ref[...] * cin_ref[...]
        up = cg_p[...] * cin_p[...]
        w = w_ref[...]
        conv = w[0:1, :] * _shift_rows(u, up, 2, first) + w[1:2, :] * _shift_rows(u, up, 1, first) + w[2:3, :] * u
        zc = zc_ref[...]
        ac_ref[...] = (bg_ref[...] * conv * (zc * _sigmoid(zc))).astype(BF16)

    seg = lambda name: pl.BlockSpec((ts, D), lambda i: (i, SEG[name]))
    prev = lambda name: pl.BlockSpec((8, D), lambda i: (jnp.maximum(i * (ts // 8) - 1, 0), SEG[name]))
    return pl.pallas_call(
        body,
        grid=(S // ts,),
        in_specs=[pl.BlockSpec((ts, D), lambda i: (i, 0)), seg("z_mla"), seg("c_in"), seg("b_gate"), seg("c_gate"), seg("z_conv"),
                  prev("c_in"), prev("c_gate"), pl.BlockSpec((8, D), lambda i: (0, 0))],
        out_specs=[pl.BlockSpec((ts, D), lambda i: (i, 0))] * 2,
        out_shape=[jax.ShapeDtypeStruct((S, D), BF16)] * 2,
        compiler_params=_cparams(("arbitrary",), 48 << 20),
        name="gates_fwd",
    )(attn, proj, proj, proj, proj, proj, proj, proj, conv_w)


def _merge_fwd(proj, y_mla, y_conv, ts):
    S = proj.shape[0]

    def body(gm_ref, gc_ref, ym_ref, yc_ref, o_ref):
        o_ref[...] = (_sigmoid(gm_ref[...]) * ym_ref[...] + _sigmoid(gc_ref[...]) * yc_ref[...]).astype(BF16)

    seg = lambda name: pl.BlockSpec((ts, D), lambda i: (i, SEG[name]))
    row = pl.BlockSpec((ts, D), lambda i: (i, 0))
    return pl.pallas_call(
        body, grid=(S // ts,), in_specs=[seg("g_mla"), seg("g_conv"), row, row], out_specs=row,
        out_shape=jax.ShapeDtypeStruct((S, D), BF16), compiler_params=_cparams(("parallel",), 32 << 20), name="merge_fwd",
    )(proj, proj, y_mla, y_conv)


def _post_loss(out, x, target, g_post, ts):
    S = out.shape[0]

    def body(o_ref, x_ref, t_ref, g_ref, dy_ref, do_ref, dg_ref, loss_ref):
        i = pl.program_id(0)
        o = o_ref[...]
        r = _rms_scale(o)
        n = o * r
        g = g_ref[...]
        err = x_ref[...] + n * g - t_ref[...]
        dy = err * (1.0 / D)
        dy_ref[...] = dy
        dn = dy * g
        do_ref[...] = (r * (dn - n * jnp.mean(dn * n, axis=-1, keepdims=True))).astype(BF16)
        dg = jnp.sum(dy * n, axis=0, keepdims=True)
        part = jnp.sum(jnp.sum(err * err, axis=0, keepdims=True), axis=1, keepdims=True) * (0.5 / D)

        @pl.when(i == 0)
        def _():
            dg_ref[...] = jnp.zeros_like(dg_ref)
            loss_ref[...] = jnp.zeros_like(loss_ref)

        dg_ref[0:1, :] += dg
        loss_ref[...] += jnp.broadcast_to(part, loss_ref.shape)

    row = pl.BlockSpec((ts, D), lambda i: (i, 0))
    return pl.pallas_call(
        body, grid=(S // ts,),
        in_specs=[row, row, row, pl.BlockSpec((1, D), lambda i: (0, 0))],
        out_specs=[row, row, pl.BlockSpec((8, D), lambda i: (0, 0)), pl.BlockSpec((8, 128), lambda i: (0, 0))],
        out_shape=[jax.ShapeDtypeStruct((S, D), F32), jax.ShapeDtypeStruct((S, D), BF16),
                   jax.ShapeDtypeStruct((8, D), F32), jax.ShapeDtypeStruct((8, 128), F32)],
        compiler_params=_cparams(("arbitrary",), 40 << 20), name="post_loss",
    )(out, x, target, g_post)


def _merge_bwd(proj, y_mla, y_conv, dmerged, ts):
    S = proj.shape[0]

    def body(gm_ref, gc_ref, ym_ref, yc_ref, dm_ref, dym_ref, dyc_ref, dgm_ref, dgc_ref):
        dm = dm_ref[...]
        sm = _sigmoid(gm_ref[...])
        sc = _sigmoid(gc_ref[...])
        dym_ref[...] = (dm * sm).astype(BF16)
        dyc_ref[...] = (dm * sc).astype(BF16)
        dgm_ref[...] = (dm * ym_ref[...] * (sm * (1.0 - sm))).astype(BF16)
        dgc_ref[...] = (dm * yc_ref[...] * (sc * (1.0 - sc))).astype(BF16)

    seg = lambda name: pl.BlockSpec((ts, D), lambda i: (i, SEG[name]))
    row = pl.BlockSpec((ts, D), lambda i: (i, 0))
    return pl.pallas_call(
        body, grid=(S // ts,), in_specs=[seg("g_mla"), seg("g_conv"), row, row, row], out_specs=[row] * 4,
        out_shape=[jax.ShapeDtypeStruct((S, D), BF16)] * 4, compiler_params=_cparams(("parallel",), 40 << 20), name="merge_bwd",
    )(proj, proj, y_mla, y_conv, dmerged)


def _gates_bwd(proj, attn, da_mla, da_conv, conv_w, ts):
    S = proj.shape[0]
    nblk = S // ts

    def body(attn_ref, zm_ref, cin_ref, bg_ref, cg_ref, zc_ref, dam_ref, dac_ref, cin_p, cg_p, bg_n, zc_n, dac_n, w_ref,
             dattn_ref, delta_ref, dzm_ref, dcin_ref, dbg_ref, dcg_ref, dzc_ref, dw_ref):
        i = pl.program_id(0)
        first = i == 0
        last = i == nblk - 1
        zm = zm_ref[...]
        sg = _sigmoid(zm)
        silu = zm * sg
        attn = attn_ref[...]
        dam = dam_ref[...]
        dattn = dam * silu
        dattn_ref[...] = dattn.astype(BF16)
        dzm_ref[...] = (dam * attn * (sg * (1.0 + zm * (1.0 - sg)))).astype(BF16)
        prod = dattn * attn
        for h in range(H):
            col = jnp.sum(prod[:, h * DV:(h + 1) * DV], axis=1, keepdims=True)
            delta_ref[h] = _col_to_row8(col, ts)
        w = w_ref[...]
        cin, cg, bg, zc = cin_ref[...], cg_ref[...], bg_ref[...], zc_ref[...]
        u = cg * cin
        up = cg_p[...] * cin_p[...]
        u1 = _shift_rows(u, up, 1, first)
        u2 = _shift_rows(u, up, 2, first)
        conv = w[0:1, :] * u2 + w[1:2, :] * u1 + w[2:3, :] * u
        sgc = _sigmoid(zc)
        siluc = zc * sgc
        dac = dac_ref[...]
        dbg_ref[...] = (dac * conv * siluc).astype(BF16)
        dzc_ref[...] = (dac * bg * conv * (sgc * (1.0 + zc * (1.0 - sgc)))).astype(BF16)
        dconv = dac * bg * siluc
        zn = zc_n[...]
        dconv_n = jnp.where(last, 0.0, dac_n[...] * bg_n[...] * (zn * _sigmoid(zn)))
        row = lax.broadcasted_iota(jnp.int32, dconv.shape, 0)
        d1 = jnp.where(row == ts - 1, dconv_n[0:1, :], pltpu.roll(dconv, ts - 1, 0))
        d2 = jnp.where(row == ts - 1, dconv_n[1:2, :], jnp.where(row == ts - 2, dconv_n[0:1, :], pltpu.roll(dconv, ts - 2, 0)))
        du = w[2:3, :] * dconv + w[1:2, :] * d1 + w[0:1, :] * d2
        dcg_ref[...] = (du * cin).astype(BF16)
        dcin_ref[...] = (du * cg).astype(BF16)

        @pl.when(first)
        def _():
            dw_ref[...] = jnp.zeros_like(dw_ref)

        dw_ref[0:1, :] += jnp.sum(dconv * u2, axis=0, keepdims=True)
        dw_ref[1:2, :] += jnp.sum(dconv * u1, axis=0, keepdims=True)
        dw_ref[2:3, :] += jnp.sum(dconv * u, axis=0, keepdims=True)

    seg = lambda name: pl.BlockSpec((ts, D), lambda i: (i, SEG[name]))
    prev = lambda name: pl.BlockSpec((8, D), lambda i: (jnp.maximum(i * (ts // 8) - 1, 0), SEG[name]))
    nxt = lambda blk: pl.BlockSpec((8, D), lambda i: (jnp.minimum((i + 1) * (ts // 8), S // 8 - 1), blk))
    row = pl.BlockSpec((ts, D), lambda i: (i, 0))
    return pl.pallas_call(
        body, grid=(nblk,),
        in_specs=[row, seg("z_mla"), seg("c_in"), seg("b_gate"), seg("c_gate"), seg("z_conv"), row, row,
                  prev("c_in"), prev("c_gate"), nxt(SEG["b_gate"]), nxt(SEG["z_conv"]), nxt(0), pl.BlockSpec((8, D), lambda i: (0, 0))],
        out_specs=[row, pl.BlockSpec((H, 8, ts), lambda i: (0, 0, i)), row, row, row, row, row, pl.BlockSpec((8, D), lambda i: (0, 0))],
        out_shape=[jax.ShapeDtypeStruct((S, D), BF16), jax.ShapeDtypeStruct((H, 8, S), F32)] + [jax.ShapeDtypeStruct((S, D), BF16)] * 5
        + [jax.ShapeDtypeStruct((8, D), F32)],
        compiler_params=_cparams(("arbitrary",), 56 << 20), name="gates_bwd",
    )(attn, proj, proj, proj, proj, proj, da_mla, da_conv, proj, proj, proj, proj, da_conv, conv_w)


def _flash_bwd(q, k, v, do, lse, delta, tq, hb=2):
    S = q.shape[1]
    nq = S // tq
    scale = 1.0 / math.sqrt(DN + DR)
    c2 = scale * LOG2E
    tsub = min(256, tq)
    nsub = tq // tsub

    qi_tab, kj_tab = _causal_pairs(nq, key_major=True)
    npairs = nq * (nq + 1) // 2

    def body(qi_ref, kj_ref, q_ref, k_ref, v_ref, do_ref, lse_ref, dl_ref, dq_ref, dk_ref, dv_ref, dk_sc, dv_sc):
        p = pl.program_id(1)
        qi, kj = qi_ref[p], kj_ref[p]

        @pl.when(p == 0)
        def _():
            dq_ref[...] = jnp.zeros_like(dq_ref)

        @pl.when(qi == kj)
        def _():
            dk_sc[...] = jnp.zeros_like(dk_sc)
            dv_sc[...] = jnp.zeros_like(dv_sc)

        def step(diag):
            chains = [(h, u) for h in range(hb) for u in range(nsub)]

            def first_matmuls(h, u):
                sub = slice(u * tsub, (u + 1) * tsub)
                st = lax.dot_general(k_ref[h], q_ref[h, sub, :], NT, preferred_element_type=F32)
                dpt = lax.dot_general(v_ref[h], do_ref[sub, h * DV:(h + 1) * DV], NT, preferred_element_type=F32)
                return st, dpt

            nxt = first_matmuls(*chains[0])
            for ci, (h, u) in enumerate(chains):
                st, dpt = nxt
                if ci + 1 < len(chains):
                    nxt = first_matmuls(*chains[ci + 1])
                sub = slice(u * tsub, (u + 1) * tsub)
                pt = jnp.exp2(st * c2 - lse_ref[h, 0:1, sub] * LOG2E)
                if diag:
                    r = lax.broadcasted_iota(jnp.int32, (tq, tsub), 0)
                    c = lax.broadcasted_iota(jnp.int32, (tq, tsub), 1) + u * tsub
                    pt = jnp.where(r <= c, pt, 0.0)
                dst = (pt * (dpt - dl_ref[h, 0:1, sub])).astype(BF16)
                dv_sc[h] += jnp.dot(pt.astype(BF16), do_ref[sub, h * DV:(h + 1) * DV], preferred_element_type=F32)
                dk_sc[h] += jnp.dot(dst, q_ref[h, sub, :], preferred_element_type=F32)
                rows = pl.ds(pl.multiple_of(qi * tq + u * tsub, tsub), tsub)
                dq_ref[h, rows, :] += lax.dot_general(dst, k_ref[h], TN, preferred_element_type=F32)

        @pl.when(qi > kj)
        def _():
            step(False)

        @pl.when(qi == kj)
        def _():
            step(True)

        @pl.when(qi == nq - 1)
        def _():
            dk_ref[...] = dk_sc[...] * scale
            dv_ref[...] = dv_sc[...]

        @pl.when(p == npairs - 1)
        def _():
            dq_ref[...] = dq_ref[...] * scale

    return pl.pallas_call(
        body,
        grid_spec=pltpu.PrefetchScalarGridSpec(
            num_scalar_prefetch=2,
            grid=(H // hb, npairs),
            in_specs=[pl.BlockSpec((hb, tq, HW), lambda h, p, qi, kj: (h, qi[p], 0)),
                      pl.BlockSpec((hb, tq, HW), lambda h, p, qi, kj: (h, kj[p], 0)),
                      pl.BlockSpec((hb, tq, DV), lambda h, p, qi, kj: (h, kj[p], 0)),
                      pl.BlockSpec((tq, hb * DV), lambda h, p, qi, kj: (qi[p], h)),
                      pl.BlockSpec((hb, 8, tq), lambda h, p, qi, kj: (h, 0, qi[p])),
                      pl.BlockSpec((hb, 8, tq), lambda h, p, qi, kj: (h, 0, qi[p]))],
            out_specs=[pl.BlockSpec((hb, S, HW), lambda h, p, qi, kj: (h, 0, 0)),
                       pl.BlockSpec((hb, tq, HW), lambda h, p, qi, kj: (h, kj[p], 0)),
                       pl.BlockSpec((hb, tq, DV), lambda h, p, qi, kj: (h, kj[p], 0))],
            scratch_shapes=[pltpu.VMEM((hb, tq, HW), F32), pltpu.VMEM((hb, tq, DV), F32)],
        ),
        out_shape=[jax.ShapeDtypeStruct((H, S, HW), F32), jax.ShapeDtypeStruct((H, S, HW), F32), jax.ShapeDtypeStruct((H, S, DV), F32)],
        compiler_params=_cparams(("parallel", "arbitrary"), VMEM_CAP),
        name="flash_bwd",
    )(qi_tab, kj_tab, q, k, v, do, lse, delta)


def _rms_bwd(xf, g, dn_out):
    r = _rms_scale(xf)
    n = xf * r
    dn = dn_out * g
    dx = r * (dn - n * jnp.mean(dn * n, axis=-1, keepdims=True))
    return dx, jnp.sum(dn_out * n, axis=0, keepdims=True)


def _qkv_bwd(proj, pos, freq, g_qa, g_kva, w_q, w_kv, dq, dk, dv, ts):
    S = proj.shape[0]

    def body(qa_ref, ckv_ref, pos_ref, freq_ref, gq_ref, gk_ref, wq_ref, wkv_ref, dq_ref, dk_ref, dv_ref,
             dqp_ref, dkvp_ref, dqa_ref, dckv_ref, dkr_ref, dgq_ref, dgk_ref):
        i = pl.program_id(0)
        cos, sin, lane = _rope_tables(pos_ref, freq_ref, ts)
        sgn_sin = jnp.where(lane < 32, sin, -sin)
        live = lane < DR
        kr_sum = jnp.zeros((ts, 128), F32)
        for h in range(H):
            dqh = dq_ref[h]
            dqp_ref[:, h * HW:h * HW + DN] = dqh[:, 0:DN].astype(BF16)
            t = dqh[:, DN:HW]
            dqp_ref[:, h * HW + DN:(h + 1) * HW] = jnp.where(live, t * cos + _rope_swap(t, lane) * sgn_sin, 0.0).astype(BF16)
            dkh = dk_ref[h]
            dkvp_ref[:, h * HW:h * HW + DN] = dkh[:, 0:DN].astype(BF16)
            dkvp_ref[:, h * HW + DN:(h + 1) * HW] = dv_ref[h].astype(BF16)
            kr_sum = kr_sum + dkh[:, DN:HW]
        dkr_ref[...] = jnp.where(live, kr_sum * cos + _rope_swap(kr_sum, lane) * sgn_sin, 0.0).astype(BF16)
        dqn = lax.dot_general(dqp_ref[...], wq_ref[...], NT, preferred_element_type=F32)
        dkvn = lax.dot_general(dkvp_ref[...], wkv_ref[...], NT, preferred_element_type=F32)
        dqa, dgq = _rms_bwd(qa_ref[...], gq_ref[...], dqn)
        dckv, dgk = _rms_bwd(ckv_ref[...], gk_ref[...], dkvn)
        dqa_ref[...] = dqa.astype(BF16)
        dckv_ref[...] = dckv.astype(BF16)

        @pl.when(i == 0)
        def _():
            dgq_ref[...] = jnp.zeros_like(dgq_ref)
            dgk_ref[...] = jnp.zeros_like(dgk_ref)

        dgq_ref[0:1, :] += dgq
        dgk_ref[0:1, :] += dgk

    rowb = lambda w, blk: pl.BlockSpec((ts, w), lambda i: (i, blk))
    full = lambda a: pl.BlockSpec(a.shape, lambda i: (0,) * a.ndim)
    heads = lambda w: pl.BlockSpec((H, ts, w), lambda i: (0, i, 0))
    return pl.pallas_call(
        body, grid=(S // ts,),
        in_specs=[rowb(RQ, OFF_QA // RQ), rowb(RKV, OFF_CKV // RKV), pl.BlockSpec((ts, 1), lambda i: (i, 0)), full(freq), full(g_qa), full(g_kva),
                  full(w_q), full(w_kv), heads(HW), heads(HW), heads(DV)],
        out_specs=[rowb(H * HW, 0), rowb(H * HW, 0), rowb(RQ, 0), rowb(RKV, 0), rowb(128, 0),
                   pl.BlockSpec((8, RQ), lambda i: (0, 0)), pl.BlockSpec((8, RKV), lambda i: (0, 0))],
        out_shape=[jax.ShapeDtypeStruct((S, H * HW), BF16), jax.ShapeDtypeStruct((S, H * HW), BF16), jax.ShapeDtypeStruct((S, RQ), BF16),
                   jax.ShapeDtypeStruct((S, RKV), BF16), jax.ShapeDtypeStruct((S, 128), BF16),
                   jax.ShapeDtypeStruct((8, RQ), F32), jax.ShapeDtypeStruct((8, RKV), F32)],
        compiler_params=_cparams(("arbitrary",), 56 << 20), name="qkv_bwd",
    )(proj, proj, pos, freq, g_qa, g_kva, w_q, w_kv, dq, dk, dv)


def _pack_dproj(parts, ts):
    S = parts[0].shape[0]
    widths = [p.shape[1] for p in parts]
    assert sum(widths) == NP

    def body(*refs):
        o_ref = refs[-1]
        off = 0
        for r, w in zip(refs[:-1], widths):
            o_ref[:, off:off + w] = r[...]
            off += w

    return pl.pallas_call(
        body, grid=(S // ts,),
        in_specs=[pl.BlockSpec((ts, w), lambda i: (i, 0)) for w in widths],
        out_specs=pl.BlockSpec((ts, NP), lambda i: (i, 0)),
        out_shape=jax.ShapeDtypeStruct((S, NP), BF16),
        compiler_params=_cparams(("parallel",), 48 << 20), name="pack_dproj",
    )(*parts)


def _prenorm_bwd(x, g, dh, dy, ts):
    S = x.shape[0]

    def body(x_ref, g_ref, dh_ref, dy_ref, gx_ref, dg_ref):
        dx, dg = _rms_bwd(x_ref[...], g_ref[...], dh_ref[...])
        gx_ref[...] = dy_ref[...] + dx

        @pl.when(pl.program_id(0) == 0)
        def _():
            dg_ref[...] = jnp.zeros_like(dg_ref)

        dg_ref[0:1, :] += dg

    row = pl.BlockSpec((ts, D), lambda i: (i, 0))
    return pl.pallas_call(
        body, grid=(S // ts,), in_specs=[row, pl.BlockSpec((1, D), lambda i: (0, 0)), row, row],
        out_specs=[row, pl.BlockSpec((8, D), lambda i: (0, 0))],
        out_shape=[jax.ShapeDtypeStruct((S, D), F32), jax.ShapeDtypeStruct((8, D), F32)],
        compiler_params=_cparams(("arbitrary",), 40 << 20), name="prenorm_bwd",
    )(x, g, dh, dy)


def _local_step(x, pos, target, g_pre, g_qa, g_kva, g_post, w_in, w_q, w_kv, conv_w8, w_o_mla, w_o_conv, w_out):
    S = x.shape[0]
    ts = min(256, S)
    tq = min(512, S)
    tm = min(512, S)
    mm = functools.partial(_matmul, tm=tm)
    freq = _inv_freq_tile()

    h = _prenorm(x, g_pre, ts)
    proj = _matmul(h, w_in, mode="nn", out_dtype=F32, tm=1024, tn=1408, tk=D, name="mm_proj")
    q, k, v, vt, qn, kvn = _qkv_prep(proj, pos, freq, g_qa, g_kva, w_q, w_kv, ts)
    attn, lse = _flash_fwd(q, k, vt, tq)
    a_mla, a_conv = _gates_fwd(proj, attn, conv_w8, ts)
    y_mla = mm(a_mla, w_o_mla, mode="nn", out_dtype=F32, tn=1024, tk=D, name="mm_y_mla")
    y_conv = mm(a_conv, w_o_conv, mode="nn", out_dtype=F32, tn=1024, tk=D, name="mm_y_conv")
    merged = _merge_fwd(proj, y_mla, y_conv, ts)
    out = mm(merged, w_out, mode="nn", out_dtype=F32, tn=1024, tk=D, name="mm_out")
    dy, dout, dg_post, loss = _post_loss(out, x, target, g_post, ts)

    dmerged = mm(dout, w_out, mode="nt", out_dtype=F32, tn=1024, tk=D, name="mm_dmerged")
    dw_out = _matmul(merged, dout, mode="tn", out_dtype=F32, tm=1024, tn=1024, tk=1024, name="mm_dw_out")
    dy_mla, dy_conv, dg_mla, dg_conv = _merge_bwd(proj, y_mla, y_conv, dmerged, ts)
    da_mla = mm(dy_mla, w_o_mla, mode="nt", out_dtype=F32, tn=1024, tk=D, name="mm_da_mla")
    da_conv = mm(dy_conv, w_o_conv, mode="nt", out_dtype=F32, tn=1024, tk=D, name="mm_da_conv")
    dw_o_mla = _matmul(a_mla, dy_mla, mode="tn", out_dtype=F32, tm=1024, tn=1024, tk=1024, name="mm_dw_o_mla")
    dw_o_conv = _matmul(a_conv, dy_conv, mode="tn", out_dtype=F32, tm=1024, tn=1024, tk=1024, name="mm_dw_o_conv")
    dattn, delta, dz_mla, dc_in, db_gate, dc_gate, dz_conv, dconv_w = _gates_bwd(proj, attn, da_mla, da_conv, conv_w8, min(128, S))
    dq, dk, dv = _flash_bwd(q, k, v, dattn, lse, delta, tq)
    dqp, dkvp, dq_a, dc_kv, dk_rope, dg_qa, dg_kva = _qkv_bwd(proj, pos, freq, g_qa, g_kva, w_q, w_kv, dq, dk, dv, min(128, S))
    dw_q = _matmul(qn, dqp, mode="tn", out_dtype=F32, tm=RQ, tn=1024, tk=1024, name="mm_dw_q")
    dw_kv = _matmul(kvn, dkvp, mode="tn", out_dtype=F32, tm=RKV, tn=1024, tk=1024, name="mm_dw_kv")
    dproj = _pack_dproj([dz_mla, dc_in, db_gate, dc_gate, dz_conv, dg_mla, dg_conv, dq_a, dc_kv, dk_rope], ts)
    dh = _matmul(dproj, w_in, mode="nt", out_dtype=F32, tm=1024, tn=D, tk=1408, name="mm_dh")
    dw_in = _matmul(h, dproj, mode="tn", out_dtype=F32, tm=D, tn=1408, tk=512, name="mm_dw_in")
    grad_x, dg_pre = _prenorm_bwd(x, g_pre, dh, dy, ts)
    return dict(loss=loss, grad_x=grad_x, dg_pre=dg_pre, dg_qa=dg_qa, dg_kva=dg_kva, dg_post=dg_post, dw_in=dw_in, dw_q=dw_q,
                dw_kv=dw_kv, dconv_w=dconv_w, dw_o_mla=dw_o_mla, dw_o_conv=dw_o_conv, dw_out=dw_out)


def _my_id():
    return lax.axis_index("x") * 4 + lax.axis_index("y") * 2 + lax.axis_index("c")


def _place():
    x, y, c = lax.axis_index("x"), lax.axis_index("y"), lax.axis_index("c")
    return (x, y, c), (x, y, 1 - c), [(1 - x, y), (x, 1 - y), (1 - x, 1 - y)]


def _slot(px, py, pc):
    return 4 * px + 2 * py + pc


def _all_gather(arrays, name):
    n = len(arrays)

    def body(*refs):
        ins, outs = refs[:n], refs[n:2 * n]
        send_sems, recv_sems, local_sems = refs[2 * n:]
        me, sib, chips = _place()
        c = me[2]

        def copy(a, k, block, to, src=None):
            rows = outs[a].at[_slot(*block)]
            return pltpu.make_async_remote_copy(src_ref=rows if src is None else src, dst_ref=rows, send_sem=send_sems.at[a, k],
                                                recv_sem=recv_sems.at[a, k], device_id=to, device_id_type=MESH)

        local = [pltpu.make_async_copy(ins[a], outs[a].at[_slot(*me)], local_sems.at[a]) for a in range(n)]
        for cp in local:
            cp.start()
        sends = []
        for a in range(n):
            sends.append(copy(a, 0, me, sib, src=ins[a]))
            sends += [copy(a, 1 + j, me, (*chip, c), src=ins[a]) for j, chip in enumerate(chips)]
        for cp in sends:
            cp.start()
        for a in range(n):
            for j, chip in enumerate(chips):
                copy(a, 1 + j, (*chip, c), me).wait_recv()
                fwd = copy(a, 4 + j, (*chip, c), sib)
                fwd.start()
                sends.append(fwd)
        for a in range(n):
            copy(a, 0, sib, me).wait_recv()
            for j, chip in enumerate(chips):
                copy(a, 4 + j, (*chip, 1 - c), me).wait_recv()
        for cp in sends:
            cp.wait_send()
        for cp in local:
            cp.wait()

    anyspec = pl.BlockSpec(memory_space=pl.ANY)
    return pl.pallas_call(
        body,
        in_specs=[anyspec] * n,
        out_specs=[anyspec] * n,
        out_shape=[jax.ShapeDtypeStruct((NDEV,) + a.shape, a.dtype) for a in arrays],
        scratch_shapes=[pltpu.SemaphoreType.DMA((n, NDEV - 1)), pltpu.SemaphoreType.DMA((n, NDEV - 1)), pltpu.SemaphoreType.DMA((n,))],
        name=name,
    )(*arrays)


def _exchange(arrays, nslots, pick, name):
    n = len(arrays)

    def body(*refs):
        ins, outs = refs[:n], refs[n:2 * n]
        send_sems, recv_sems = refs[2 * n:]
        sends = []
        for a in range(n):
            for k in range(nslots):
                slab, to = pick(k)
                cp = pltpu.make_async_remote_copy(src_ref=ins[a].at[slab], dst_ref=outs[a].at[k], send_sem=send_sems.at[a, k],
                                                  recv_sem=recv_sems.at[a, k], device_id=to, device_id_type=MESH)
                cp.start()
                sends.append(cp)
        for cp in sends:
            cp.wait_recv()
        for cp in sends:
            cp.wait_send()

    anyspec = pl.BlockSpec(memory_space=pl.ANY)
    return pl.pallas_call(
        body,
        in_specs=[anyspec] * n,
        out_specs=[anyspec] * n,
        out_shape=[jax.ShapeDtypeStruct((nslots,) + a.shape[1:], a.dtype) for a in arrays],
        scratch_shapes=[pltpu.SemaphoreType.DMA((n, nslots)), pltpu.SemaphoreType.DMA((n, nslots))],
        name=name,
    )(*arrays)


def _slabs_bf16(x, rows_per_block, name):
    n, R, C = x.shape
    tr = min(rows_per_block, R)

    def body(x_ref, o_ref):
        o_ref[...] = x_ref[...].astype(BF16)

    blk = pl.BlockSpec((1, tr, C), lambda k, i: (k, i, 0))
    return pl.pallas_call(body, grid=(n, R // tr), in_specs=[blk], out_specs=blk, out_shape=jax.ShapeDtypeStruct(x.shape, BF16),
                          compiler_params=_cparams(("parallel", "parallel"), 32 << 20), name=name)(x)


def _to_sibling(k):
    me, sib, chips = _place()
    dest = sib if k == 0 else (*chips[k - 1], sib[2])
    return _slot(*dest), sib


def _to_chips(k):
    me, sib, chips = _place()
    return k, (*chips[k], me[2])


def _chip_sums(own, got, slots, rows_per_block, name):
    _, R, C = own.shape
    tr = min(rows_per_block, R)

    def body(slots_ref, own_ref, got_ref, o_ref):
        o_ref[0] = (own_ref[0] + got_ref[0].astype(F32)).astype(BF16)

    return pl.pallas_call(
        body,
        grid_spec=pltpu.PrefetchScalarGridSpec(
            num_scalar_prefetch=1, grid=(3, R // tr),
            in_specs=[pl.BlockSpec((1, tr, C), lambda j, i, s: (s[1 + j], i, 0)), pl.BlockSpec((1, tr, C), lambda j, i, s: (1 + j, i, 0))],
            out_specs=pl.BlockSpec((1, tr, C), lambda j, i, s: (j, i, 0))),
        out_shape=jax.ShapeDtypeStruct((3, R, C), BF16),
        compiler_params=_cparams(("parallel", "parallel"), 32 << 20), name=name,
    )(slots, own, got)


def _adamw_math(w, g, m, v):
    m = ADAM_B1 * m + (1.0 - ADAM_B1) * g
    v = ADAM_B2 * v + (1.0 - ADAM_B2) * (g * g)
    m_hat = m / (1.0 - ADAM_B1 ** ADAM_STEP)
    v_hat = v / (1.0 - ADAM_B2 ** ADAM_STEP)
    delta = -ADAM_LR * (m_hat / (jnp.sqrt(v_hat) + ADAM_EPS) + ADAM_WD * w)
    return delta, m, v


def _reduce_adamw(own, got1, got2, slots, w, m, v, rows_per_block, name):
    _, R, C = own.shape
    tr = min(rows_per_block, R)
    assert R % tr == 0

    def body(slots_ref, own_ref, g1_ref, g2_ref, w_ref, m_ref, v_ref, g_ref, d_ref, nm_ref, nv_ref):
        g = own_ref[0] + g1_ref[0].astype(F32)
        for j in range(3):
            g = g + g2_ref[j].astype(F32)
        g_ref[...] = g
        d, nm, nv = _adamw_math(w_ref[...], g, m_ref[...], v_ref[...])
        d_ref[...] = d
        nm_ref[...] = nm
        nv_ref[...] = nv

    blk = pl.BlockSpec((tr, C), lambda i, s: (i, 0))
    return pl.pallas_call(
        body,
        grid_spec=pltpu.PrefetchScalarGridSpec(
            num_scalar_prefetch=1, grid=(R // tr,),
            in_specs=[pl.BlockSpec((1, tr, C), lambda i, s: (s[0], i, 0)), pl.BlockSpec((1, tr, C), lambda i, s: (0, i, 0)),
                      pl.BlockSpec((3, tr, C), lambda i, s: (0, i, 0)), blk, blk, blk],
            out_specs=[blk] * 4),
        out_shape=[jax.ShapeDtypeStruct((R, C), F32)] * 4,
        compiler_params=_cparams(("parallel",), 48 << 20), name=name,
    )(slots, own, got1, got2, w, m, v)


def _sum_adamw(parts, w, m, v, rows_per_block, name):
    n, R, C = parts.shape
    tr = min(rows_per_block, R)
    assert R % tr == 0

    def body(p_ref, w_ref, m_ref, v_ref, g_ref, d_ref, nm_ref, nv_ref):
        g = p_ref[0]
        for j in range(1, n):
            g = g + p_ref[j]
        g_ref[...] = g
        d, nm, nv = _adamw_math(w_ref[...], g, m_ref[...], v_ref[...])
        d_ref[...] = d
        nm_ref[...] = nm
        nv_ref[...] = nv

    blk = pl.BlockSpec((tr, C), lambda i: (i, 0))
    return pl.pallas_call(
        body, grid=(R // tr,),
        in_specs=[pl.BlockSpec((n, tr, C), lambda i: (0, i, 0)), blk, blk, blk],
        out_specs=[blk] * 4,
        out_shape=[jax.ShapeDtypeStruct((R, C), F32)] * 4,
        compiler_params=_cparams(("parallel",), 48 << 20), name=name,
    )(parts, w, m, v)


def kernel(x, positions, pre_norm_g, w_in, q_a_norm_g, w_q_b, kv_a_norm_g, w_kv_b, conv_w, w_o_mla, w_o_conv, w_out, post_norm_g, loss_target, m_pre_norm_g, m_w_in, m_q_a_norm_g, m_w_q_b, m_kv_a_norm_g, m_w_kv_b, m_conv_w, m_w_o_mla, m_w_o_conv, m_w_out, m_post_norm_g, v_pre_norm_g, v_w_in, v_q_a_norm_g, v_w_q_b, v_kv_a_norm_g, v_w_kv_b, v_conv_w, v_w_o_mla, v_w_o_conv, v_w_out, v_post_norm_g):
    S = x.shape[1]
    conv_pad = jnp.zeros((8, 256), F32).at[0:3, :].set(conv_w)
    g_in, g_q, g_kv, g_om, g_oc, g_out, g_cw = _all_gather(
        [w_in.astype(BF16), w_q_b.astype(BF16), w_kv_b.astype(BF16), w_o_mla.astype(BF16), w_o_conv.astype(BF16), w_out.astype(BF16),
         conv_pad], "all_gather_weights")
    w_in_f = _assemble_w_in(g_in)
    w_q_f = _assemble_w_q(g_q)
    w_kv_f = _concat_cols(g_kv, BF16, "assemble_w_kv")
    conv_w8 = _concat_cols(g_cw, F32, "assemble_conv_w")
    row2 = lambda a: a.reshape(1, -1)
    r = _local_step(x[0], positions.reshape(S, 1), loss_target[0], row2(pre_norm_g), row2(q_a_norm_g), row2(kv_a_norm_g),
                    row2(post_norm_g), w_in_f, w_q_f, w_kv_f, conv_w8, g_om.reshape(D, D), g_oc.reshape(D, D), g_out.reshape(D, D))
    small = jnp.concatenate([r["dg_pre"][0:1], r["dg_post"][0:1], jnp.pad(r["dg_qa"][0:1], ((0, 0), (0, D - RQ))),
                             jnp.pad(r["dg_kva"][0:1], ((0, 0), (0, D - RKV))), jnp.pad(r["loss"][0:1], ((0, 0), (0, D - 128))),
                             jnp.zeros((3, D), F32)], axis=0)
    gnames = ["w_in", "w_q", "w_kv", "conv_w", "w_o_mla", "w_o_conv", "w_out"]
    own = [_split_dw_in(r["dw_in"]), _split_dw_q(r["dw_q"]), _split_cols(r["dw_kv"], NDEV, "split_dw_kv"),
           _split_cols(r["dconv_w"], NDEV, "split_dconv_w"), r["dw_o_mla"].reshape(NDEV, D // NDEV, D),
           r["dw_o_conv"].reshape(NDEV, D // NDEV, D), r["dw_out"].reshape(NDEV, D // NDEV, D)]
    (mx, my, mc), _, chips = _place()
    slots = jnp.stack([_slot(mx, my, mc)] + [_slot(cx, cy, mc) for cx, cy in chips]).astype(jnp.int32)
    got1 = _exchange([_slabs_bf16(g, 128, "bf16_" + nm) for g, nm in zip(own, gnames)], 4, _to_sibling, "grads_to_sibling")
    sums = [_chip_sums(o, g1, slots, 128, "chip_sum_" + nm) for o, g1, nm in zip(own, got1, gnames)]
    got2 = _exchange(sums, 3, _to_chips, "grads_to_chips")
    p_small, = _all_gather([small], "all_gather_small")
    pad8 = lambda a: jnp.zeros((8, 256), F32).at[0:3, :].set(a)
    params = [(w_in, m_w_in, v_w_in), (w_q_b, m_w_q_b, v_w_q_b), (w_kv_b, m_w_kv_b, v_w_kv_b), (conv_pad, pad8(m_conv_w), pad8(v_conv_w)),
              (w_o_mla, m_w_o_mla, v_w_o_mla), (w_o_conv, m_w_o_conv, v_w_o_conv), (w_out, m_w_out, v_w_out)]
    o_in, o_q, o_kv, o_cw, o_om, o_oc, o_out = [
        _reduce_adamw(o, g1, g2, slots, w, m, v, 128, "adamw_" + nm)
        for o, g1, g2, (w, m, v), nm in zip(own, got1, got2, params, gnames)]
    o_cw = [a[0:3] for a in o_cw]
    padv = lambda a: jnp.pad(a.reshape(1, -1), ((0, 0), (0, D - a.shape[0])))
    stack = lambda pre, post, qa, kva: jnp.concatenate([row2(pre), row2(post), padv(qa), padv(kva), jnp.zeros((4, D), F32)], axis=0)
    o_g = _sum_adamw(p_small, stack(pre_norm_g, post_norm_g, q_a_norm_g, kv_a_norm_g),
                     stack(m_pre_norm_g, m_post_norm_g, m_q_a_norm_g, m_kv_a_norm_g),
                     stack(v_pre_norm_g, v_post_norm_g, v_q_a_norm_g, v_kv_a_norm_g), 8, "adamw_gains")
    loss = o_g[0][4, 0]
    outs = {}
    for idx, kind in enumerate(("grad", "delta", "new_m", "new_v")):
        o = o_g[idx]
        outs[kind] = dict(pre_norm_g=o[0], w_in=o_in[idx], q_a_norm_g=o[2, 0:RQ], w_q_b=o_q[idx], kv_a_norm_g=o[3, 0:RKV],
                          w_kv_b=o_kv[idx], conv_w=o_cw[idx], w_o_mla=o_om[idx], w_o_conv=o_oc[idx], w_out=o_out[idx], post_norm_g=o[1])
    names = ["pre_norm_g", "w_in", "q_a_norm_g", "w_q_b", "kv_a_norm_g", "w_kv_b", "conv_w", "w_o_mla", "w_o_conv", "w_out", "post_norm_g"]
    return (loss, r["grad_x"][None], *[outs["grad"][n] for n in names], *[outs["delta"][n] for n in names],
            *[outs["new_m"][n] for n in names], *[outs["new_v"][n] for n in names])
```

```python
import functools
import math

import jax
import jax.numpy as jnp
from jax import lax
from jax.experimental import pallas as pl
from jax.experimental.pallas import tpu as pltpu

F32 = jnp.float32
BF16 = jnp.bfloat16

NDEV = 8
D = 2048
H = 16
DN = 128
DR = 64
DV = 128
RQ = 512
RKV = 512
HW = 256
ROPE_THETA = 10000.0
RMS_EPS = 1e-6
N_IN = 15424
SHARD_IN = N_IN // NDEV
SMALL = RQ + RKV + DR
NP = 7 * D + RQ + RKV + 128
SEG = dict(z_mla=0, c_in=1, b_gate=2, c_gate=3, z_conv=4, g_mla=5, g_conv=6)
OFF_QA = 7 * D
OFF_CKV = OFF_QA + RQ
OFF_KR = OFF_CKV + RKV
EXT = 2176
VMEM_CAP = 56 * 1024 * 1024

ADAM_LR = 0.001
ADAM_B1 = 0.9
ADAM_B2 = 0.999
ADAM_EPS = 1e-08
ADAM_WD = 0.01
ADAM_STEP = 10

LOG2E = math.log2(math.e)
NN = (((1,), (0,)), ((), ()))
NT = (((1,), (1,)), ((), ()))
TN = (((0,), (0,)), ((), ()))
MESH = pl.DeviceIdType.MESH


def _cparams(sem, vmem_bytes):
    return pltpu.CompilerParams(dimension_semantics=sem, vmem_limit_bytes=int(min(VMEM_CAP, max(vmem_bytes, 16 << 20))))


def _nbytes(shape, dtype):
    return math.prod(shape) * jnp.dtype(dtype).itemsize


class _Exchange:
    def __init__(self, arrays, out_shapes, sem_shapes, start, finish):
        self.arrays, self.out_shapes, self.sem_shapes, self.start, self.finish = arrays, out_shapes, sem_shapes, start, finish
        self.n_sems = len(sem_shapes)


def _matmul(a, b, *, mode, out_dtype, tm, tn, tk, name, m_outer=False, exchange=None):
    if mode == "nn":
        (M, K), (K2, N) = a.shape, b.shape
    elif mode == "nt":
        (M, K), (N, K2) = a.shape, b.shape
    else:
        (K, M), (K2, N) = a.shape, b.shape
    assert K == K2, (a.shape, b.shape, mode)
    tm, tn, tk = min(tm, M), min(tn, N), min(tk, K)
    assert M % tm == 0 and N % tn == 0 and K % tk == 0, (M, N, K, tm, tn, tk)
    ni, nj, nk = M // tm, N // tn, K // tk
    dims = dict(nn=NN, nt=NT, tn=TN)[mode]

    if m_outer:
        grid = (ni, nj, nk)
        ij = lambda g0, g1: (g0, g1)
    else:
        grid = (nj, ni, nk)
        ij = lambda g0, g1: (g1, g0)

    if mode == "tn":
        a_spec = pl.BlockSpec((tk, tm), lambda g0, g1, k: (k, ij(g0, g1)[0]))
        a_tile = (tk, tm)
    else:
        a_spec = pl.BlockSpec((tm, tk), lambda g0, g1, k: (ij(g0, g1)[0], k))
        a_tile = (tm, tk)
    if mode == "nt":
        b_spec = pl.BlockSpec((tn, tk), lambda g0, g1, k: (ij(g0, g1)[1], k))
    else:
        b_spec = pl.BlockSpec((tk, tn), lambda g0, g1, k: (k, ij(g0, g1)[1]))
    o_spec = pl.BlockSpec((tm, tn), lambda g0, g1, k: ij(g0, g1))

    n_in = len(exchange.arrays) if exchange else 0

    def body(a_ref, b_ref, *refs):
        x_in, o_ref, x_out, scratch = refs[:n_in], refs[n_in], refs[n_in + 1:2 * n_in + 1], refs[2 * n_in + 1:]
        if exchange:
            sems = scratch[len(scratch) - exchange.n_sems:]
            step = (pl.program_id(0) * grid[1] + pl.program_id(1)) * grid[2] + pl.program_id(2)

            @pl.when(step == 0)
            def _():
                exchange.start(x_in, x_out, sems)

        prod = lax.dot_general(a_ref[...], b_ref[...], dims, preferred_element_type=F32)
        if nk == 1:
            o_ref[...] = prod.astype(o_ref.dtype)
        else:
            acc_ref = scratch[0]
            k = pl.program_id(2)

            @pl.when(k == 0)
            def _():
                acc_ref[...] = prod

            @pl.when(k > 0)
            def _():
                acc_ref[...] += prod

            @pl.when(k == nk - 1)
            def _():
                o_ref[...] = acc_ref[...].astype(o_ref.dtype)

        if exchange:
            @pl.when(step == grid[0] * grid[1] * grid[2] - 1)
            def _():
                exchange.finish(x_in, x_out, sems)

    vmem = 2 * (_nbytes(a_tile, a.dtype) + _nbytes((tk, tn), b.dtype) + _nbytes((tm, tn), out_dtype)) + 2 * _nbytes((tm, tn), F32)
    anyspec = pl.BlockSpec(memory_space=pl.ANY)
    outs = pl.pallas_call(
        body,
        grid=grid,
        in_specs=[a_spec, b_spec] + [anyspec] * n_in,
        out_specs=[o_spec] + [anyspec] * n_in,
        out_shape=[jax.ShapeDtypeStruct((M, N), out_dtype)] + (exchange.out_shapes if exchange else []),
        scratch_shapes=([] if nk == 1 else [pltpu.VMEM((tm, tn), F32)]) + (exchange.sem_shapes if exchange else []),
        compiler_params=_cparams(("arbitrary",) * 3 if exchange else ("parallel", "parallel", "arbitrary"), vmem + (8 << 20)),
        name=name,
    )(a, b, *(exchange.arrays if exchange else []))
    return (outs[0], outs[1:]) if exchange else outs[0]


def _in_dest(k):
    return SHARD_IN * k - SMALL


def _assemble_w_in(g):
    R = 128

    def body(g_ref, o_ref, ext, acc):
        ext[...] = jnp.zeros_like(ext)
        acc[...] = jnp.zeros_like(acc)
        lane = lax.broadcasted_iota(jnp.int32, (R, EXT), 1)
        ext[:, 0:SHARD_IN] = g_ref[0].astype(F32)
        v = ext[...]
        acc[:, OFF_QA:NP] = jnp.where(lane[:, 0:NP - OFF_QA] < SMALL, v[:, 0:NP - OFF_QA], 0.0)
        w = v[:, 1024:2048]
        w = pltpu.roll(w, 1024 - 64, 1)
        acc[:, 0:1024] = jnp.where(lane[:, 0:1024] < SHARD_IN - SMALL, w, 0.0)
        for k in range(1, NDEV):
            ext[:, 0:SHARD_IN] = g_ref[k].astype(F32)
            dest = _in_dest(k)
            t, o = dest // 128, dest % 128
            width = -(-(o + SHARD_IN) // 128) * 128
            v = pltpu.roll(ext[...], o, 1)[:, 0:width]
            acc[:, 128 * t:128 * t + width] += v
        o_ref[...] = acc[...].astype(BF16)

    return pl.pallas_call(
        body,
        grid=(D // R,),
        in_specs=[pl.BlockSpec((NDEV, R, SHARD_IN), lambda i: (0, i, 0))],
        out_specs=pl.BlockSpec((R, NP), lambda i: (i, 0)),
        out_shape=jax.ShapeDtypeStruct((D, NP), BF16),
        scratch_shapes=[pltpu.VMEM((R, EXT), F32), pltpu.VMEM((R, NP), F32)],
        compiler_params=_cparams(("parallel",), 40 << 20),
        name="assemble_w_in",
    )(g)


def _split_dw_in(dw):
    R = 128

    def body(dw_ref, o_ref):
        lane = lax.broadcasted_iota(jnp.int32, (R, 1024), 1)
        o_ref[0, :, 0:1024] = dw_ref[:, OFF_QA:OFF_QA + 1024]
        tail = dw_ref[:, OFF_QA + 1024:NP]
        tail = jnp.concatenate([tail, jnp.zeros((R, 1024 - 128), F32)], axis=1)
        head = dw_ref[:, 0:1024]
        mixed = jnp.where(lane < 64, tail, pltpu.roll(head, 64, 1))
        o_ref[0, :, 1024:SHARD_IN] = mixed[:, 0:SHARD_IN - 1024]
        for k in range(1, NDEV):
            dest = _in_dest(k)
            t, o = dest // 128, dest % 128
            width = -(-(o + SHARD_IN) // 128) * 128
            v = dw_ref[:, 128 * t:128 * t + width]
            v = pltpu.roll(v, width - o, 1)
            o_ref[k] = v[:, 0:SHARD_IN]

    return pl.pallas_call(
        body,
        grid=(D // R,),
        in_specs=[pl.BlockSpec((R, NP), lambda i: (i, 0))],
        out_specs=pl.BlockSpec((NDEV, R, SHARD_IN), lambda i: (0, i, 0)),
        out_shape=jax.ShapeDtypeStruct((NDEV, D, SHARD_IN), F32),
        compiler_params=_cparams(("parallel",), 40 << 20),
        name="split_dw_in",
    )(dw)


def _assemble_w_q(g):
    def body(g_ref, o_ref):
        lane = lax.broadcasted_iota(jnp.int32, (RQ, 128), 1)
        lo = lane < 64
        for k in range(NDEV):
            t0 = g_ref[k, :, 0:128].astype(F32)
            t1 = g_ref[k, :, 128:256].astype(F32)
            t2 = g_ref[k, :, 256:384].astype(F32)
            base = 2 * k * HW
            o_ref[:, base:base + 128] = t0.astype(BF16)
            o_ref[:, base + 128:base + 256] = jnp.where(lo, t1, 0.0).astype(BF16)
            o_ref[:, base + 256:base + 384] = pltpu.roll(jnp.where(lo, t2, t1), 64, 1).astype(BF16)
            o_ref[:, base + 384:base + 512] = jnp.where(lo, pltpu.roll(t2, 64, 1), 0.0).astype(BF16)

    return pl.pallas_call(
        body,
        out_shape=jax.ShapeDtypeStruct((RQ, H * HW), BF16),
        compiler_params=_cparams(None, 32 << 20),
        name="assemble_w_q",
    )(g)


def _split_dw_q(dw):
    def body(dw_ref, o_ref):
        lane = lax.broadcasted_iota(jnp.int32, (RQ, 128), 1)
        lo = lane < 64
        for k in range(NDEV):
            base = 2 * k * HW
            a = dw_ref[:, base:base + 128]
            b = dw_ref[:, base + 128:base + 256]
            c = pltpu.roll(dw_ref[:, base + 256:base + 384], 64, 1)
            d = pltpu.roll(dw_ref[:, base + 384:base + 512], 64, 1)
            o_ref[k, :, 0:128] = a
            o_ref[k, :, 128:256] = jnp.where(lo, b, c)
            o_ref[k, :, 256:384] = jnp.where(lo, c, d)

    return pl.pallas_call(
        body,
        out_shape=jax.ShapeDtypeStruct((NDEV, RQ, 384), F32),
        compiler_params=_cparams(None, 32 << 20),
        name="split_dw_q",
    )(dw)


def _concat_cols(g, dtype, name):
    n, R, C = g.shape

    def body(g_ref, o_ref):
        for k in range(n):
            o_ref[:, k * C:(k + 1) * C] = g_ref[k].astype(dtype)

    return pl.pallas_call(body, out_shape=jax.ShapeDtypeStruct((R, n * C), dtype), compiler_params=_cparams(None, 32 << 20), name=name)(g)


def _split_cols(x, n, name):
    R, NC = x.shape
    C = NC // n

    def body(x_ref, o_ref):
        for k in range(n):
            o_ref[k] = x_ref[:, k * C:(k + 1) * C]

    return pl.pallas_call(body, out_shape=jax.ShapeDtypeStruct((n, R, C), x.dtype), compiler_params=_cparams(None, 32 << 20), name=name)(x)


def _rms_scale(xf):
    return lax.rsqrt(jnp.mean(xf * xf, axis=-1, keepdims=True) + RMS_EPS)


def _prenorm(x, g, ts):
    S = x.shape[0]

    def body(x_ref, g_ref, h_ref, ht_ref):
        xf = x_ref[...]
        h = xf * _rms_scale(xf) * g_ref[...]
        h_ref[...] = h.astype(BF16)
        ht_ref[...] = jnp.transpose(h).astype(BF16)

    return pl.pallas_call(
        body,
        grid=(S // ts,),
        in_specs=[pl.BlockSpec((ts, D), lambda i: (i, 0)), pl.BlockSpec((1, D), lambda i: (0, 0))],
        out_specs=[pl.BlockSpec((ts, D), lambda i: (i, 0)), pl.BlockSpec((D, ts), lambda i: (0, i))],
        out_shape=[jax.ShapeDtypeStruct((S, D), BF16), jax.ShapeDtypeStruct((D, S), BF16)],
        compiler_params=_cparams(("parallel",), 32 << 20),
        name="prenorm",
    )(x, g)


def _inv_freq_tile():
    inv_freq = ROPE_THETA ** (-jnp.arange(0, DR, 2, dtype=F32) / DR)
    return jnp.tile(inv_freq, 4).reshape(1, 128)


def _rope_tables(pos_ref, freq_ref, ts):
    lane = lax.broadcasted_iota(jnp.int32, (ts, 128), 1)
    ang = pos_ref[...].astype(F32) * freq_ref[...]
    return jnp.cos(ang), jnp.sin(ang), lane


def _rope_swap(t, lane):
    return jnp.where(lane < 32, pltpu.roll(t, 96, 1), pltpu.roll(t, 32, 1))


def _qkv_prep(proj, pos, freq, g_qa, g_kva, w_q, w_kv, ts):
    S = proj.shape[0]

    def body(qa_ref, ckv_ref, kr_ref, pos_ref, freq_ref, gq_ref, gk_ref, wq_ref, wkv_ref, q_ref, k_ref, v_ref, vt_ref, qn_ref, kvn_ref):
        qa = qa_ref[...]
        qn = (qa * _rms_scale(qa) * gq_ref[...]).astype(BF16)
        ckv = ckv_ref[...]
        kvn = (ckv * _rms_scale(ckv) * gk_ref[...]).astype(BF16)
        qn_ref[...] = qn
        kvn_ref[...] = kvn
        cos, sin, lane = _rope_tables(pos_ref, freq_ref, ts)
        sgn_sin = jnp.where(lane < 32, -sin, sin)
        live = lane < DR
        kr = kr_ref[...]
        kr = jnp.where(live, kr * cos + _rope_swap(kr, lane) * sgn_sin, 0.0).astype(BF16)
        qf = jnp.dot(qn, wq_ref[...], preferred_element_type=F32)
        kvf = jnp.dot(kvn, wkv_ref[...], preferred_element_type=F32)
        for h in range(H):
            q_ref[h, :, 0:DN] = qf[:, h * HW:h * HW + DN].astype(BF16)
            t = qf[:, h * HW + DN:(h + 1) * HW]
            q_ref[h, :, DN:HW] = jnp.where(live, t * cos + _rope_swap(t, lane) * sgn_sin, 0.0).astype(BF16)
            k_ref[h, :, 0:DN] = kvf[:, h * HW:h * HW + DN].astype(BF16)
            k_ref[h, :, DN:HW] = kr
            vh = kvf[:, h * HW + DN:(h + 1) * HW]
            v_ref[h] = vh.astype(BF16)
            vt_ref[h] = jnp.transpose(vh).astype(BF16)

    row = lambda w, blk: pl.BlockSpec((ts, w), lambda i: (i, blk))
    full = lambda a: pl.BlockSpec(a.shape, lambda i: (0,) * a.ndim)
    return pl.pallas_call(
        body,
        grid=(S // ts,),
        in_specs=[row(RQ, OFF_QA // RQ), row(RKV, OFF_CKV // RKV), row(128, OFF_KR // 128),
                  pl.BlockSpec((ts, 1), lambda i: (i, 0)), full(freq), full(g_qa), full(g_kva), full(w_q), full(w_kv)],
        out_specs=[pl.BlockSpec((H, ts, HW), lambda i: (0, i, 0)), pl.BlockSpec((H, ts, HW), lambda i: (0, i, 0)),
                   pl.BlockSpec((H, ts, DV), lambda i: (0, i, 0)), pl.BlockSpec((H, DV, ts), lambda i: (0, 0, i)),
                   pl.BlockSpec((ts, RQ), lambda i: (i, 0)), pl.BlockSpec((ts, RKV), lambda i: (i, 0))],
        out_shape=[jax.ShapeDtypeStruct((H, S, HW), BF16), jax.ShapeDtypeStruct((H, S, HW), BF16),
                   jax.ShapeDtypeStruct((H, S, DV), BF16), jax.ShapeDtypeStruct((H, DV, S), BF16),
                   jax.ShapeDtypeStruct((S, RQ), BF16), jax.ShapeDtypeStruct((S, RKV), BF16)],
        compiler_params=_cparams(("parallel",), 48 << 20),
        name="qkv_prep",
    )(proj, proj, proj, pos, freq, g_qa, g_kva, w_q, w_kv)


def _col_to_row8(col, n):
    return jnp.transpose(jnp.broadcast_to(col, (n, 128)))[0:8, :]


def _causal_pairs(n, key_major):
    if key_major:
        pairs = [(i, j) for j in range(n) for i in range(j, n)]
    else:
        pairs = [(i, j) for i in range(n) for j in range(i + 1)]
    return jnp.array([p[0] for p in pairs], jnp.int32), jnp.array([p[1] for p in pairs], jnp.int32)


def _flash_fwd(q, k, vt, tq, hb=8):
    S = q.shape[1]
    nq = S // tq
    scale = 1.0 / math.sqrt(DN + DR)
    c2 = scale * math.log2(math.e)
    tsub = tq
    nsub = tq // tsub

    qi_tab, kj_tab = _causal_pairs(nq, key_major=False)

    def body(qi_ref, kj_ref, q_ref, k_ref, vt_ref, o_ref, lse_ref, m_sc, l_sc, acc_sc):
        p = pl.program_id(1)
        qi, kj = qi_ref[p], kj_ref[p]

        @pl.when(kj == 0)
        def _():
            m_sc[...] = jnp.full_like(m_sc, -jnp.inf)
            l_sc[...] = jnp.zeros_like(l_sc)
            acc_sc[...] = jnp.zeros_like(acc_sc)

        def step(diag):
            chains = [(h, u) for h in range(hb) for u in range(nsub)]

            def scores(h, u):
                return lax.dot_general(k_ref[h], q_ref[h, u * tsub:(u + 1) * tsub, :], NT, preferred_element_type=F32)

            st_next = scores(*chains[0])
            for ci, (h, u) in enumerate(chains):
                st = st_next
                if ci + 1 < len(chains):
                    st_next = scores(*chains[ci + 1])
                cols = slice(u * tsub, (u + 1) * tsub)
                if diag:
                    r = lax.broadcasted_iota(jnp.int32, (tq, tsub), 0)
                    c = lax.broadcasted_iota(jnp.int32, (tq, tsub), 1) + u * tsub
                    st = jnp.where(r <= c, st, -jnp.inf)
                m_prev = m_sc[h, 0:1, cols]
                m_new = jnp.maximum(m_prev, jnp.max(st, axis=0, keepdims=True))
                alpha = jnp.exp2((m_prev - m_new) * c2)
                pt = jnp.exp2((st - m_new) * c2)
                l_sc[h, :, cols] = jnp.broadcast_to(alpha * l_sc[h, 0:1, cols] + jnp.sum(pt, axis=0, keepdims=True), (8, tsub))
                m_sc[h, :, cols] = jnp.broadcast_to(m_new, (8, tsub))
                acc_sc[h, :, cols] = alpha * acc_sc[h, :, cols] + jnp.dot(vt_ref[h], pt.astype(BF16), preferred_element_type=F32)

        @pl.when(kj < qi)
        def _():
            step(False)

        @pl.when(kj == qi)
        def _():
            step(True)
            for h in range(hb):
                l = l_sc[h, 0:1, :]
                o_ref[:, h * DV:(h + 1) * DV] = jnp.transpose(acc_sc[h] / l)
                lse_ref[h] = m_sc[h] * scale + jnp.log(l_sc[h])

    return pl.pallas_call(
        body,
        grid_spec=pltpu.PrefetchScalarGridSpec(
            num_scalar_prefetch=2,
            grid=(H // hb, len(qi_tab)),
            in_specs=[pl.BlockSpec((hb, tq, HW), lambda h, p, qi, kj: (h, qi[p], 0)),
                      pl.BlockSpec((hb, tq, HW), lambda h, p, qi, kj: (h, kj[p], 0)),
                      pl.BlockSpec((hb, DV, tq), lambda h, p, qi, kj: (h, 0, kj[p]))],
            out_specs=[pl.BlockSpec((tq, hb * DV), lambda h, p, qi, kj: (qi[p], h)),
                       pl.BlockSpec((hb, 8, tq), lambda h, p, qi, kj: (h, 0, qi[p]))],
            scratch_shapes=[pltpu.VMEM((hb, 8, tq), F32), pltpu.VMEM((hb, 8, tq), F32), pltpu.VMEM((hb, DV, tq), F32)],
        ),
        out_shape=[jax.ShapeDtypeStruct((S, H * DV), F32), jax.ShapeDtypeStruct((H, 8, S), F32)],
        compiler_params=_cparams(("parallel", "arbitrary"), 40 << 20),
        name="flash_fwd",
    )(qi_tab, kj_tab, q, k, vt)


def _sigmoid(x):
    return 1.0 / (1.0 + jnp.exp(-x))


def _shift_rows(u, prev8, n, first):
    ts = u.shape[0]
    row = lax.broadcasted_iota(jnp.int32, u.shape, 0)
    out = pltpu.roll(u, n, 0)
    for j in range(n):
        halo = jnp.where(first, 0.0, prev8[8 - n + j:8 - n + j + 1, :])
        out = jnp.where(row == j, halo, out)
    return out


def _gates_fwd(proj, attn, conv_w, ts):
    S = proj.shape[0]

    def body(attn_ref, zm_ref, cin_ref, bg_ref, cg_ref, zc_ref, cin_p, cg_p, w_ref, am_ref, ac_ref):
        first = pl.program_id(0) == 0
        zm = zm_ref[...]
        am_ref[...] = (attn_ref[...] * (zm * _sigmoid(zm))).astype(BF16)
        u = cg_ref[...] * cin_ref[...]
        up = cg_p[...] * cin_p[...]
        w = w_ref[...]
        conv = w[0:1, :] * _shift_rows(u, up, 2, first) + w[1:2, :] * _shift_rows(u, up, 1, first) + w[2:3, :] * u
        zc = zc_ref[...]
        ac_ref[...] = (bg_ref[...] * conv * (zc * _sigmoid(zc))).astype(BF16)

    seg = lambda name: pl.BlockSpec((ts, D), lambda i: (i, SEG[name]))
    prev = lambda name: pl.BlockSpec((8, D), lambda i: (jnp.maximum(i * (ts // 8) - 1, 0), SEG[name]))
    return pl.pallas_call(
        body,
        grid=(S // ts,),
        in_specs=[pl.BlockSpec((ts, D), lambda i: (i, 0)), seg("z_mla"), seg("c_in"), seg("b_gate"), seg("c_gate"), seg("z_conv"),
                  prev("c_in"), prev("c_gate"), pl.BlockSpec((8, D), lambda i: (0, 0))],
        out_specs=[pl.BlockSpec((ts, D), lambda i: (i, 0))] * 2,
        out_shape=[jax.ShapeDtypeStruct((S, D), BF16)] * 2,
        compiler_params=_cparams(("arbitrary",), 48 << 20),
        name="gates_fwd",
    )(attn, proj, proj, proj, proj, proj, proj, proj, conv_w)


def _merge_fwd(proj, y_mla, y_conv, ts):
    S = proj.shape[0]

    def body(gm_ref, gc_ref, ym_ref, yc_ref, o_ref):
        o_ref[...] = (_sigmoid(gm_ref[...]) * ym_ref[...] + _sigmoid(gc_ref[...]) * yc_ref[...]).astype(BF16)

    seg = lambda name: pl.BlockSpec((ts, D), lambda i: (i, SEG[name]))
    row = pl.BlockSpec((ts, D), lambda i: (i, 0))
    return pl.pallas_call(
        body, grid=(S // ts,), in_specs=[seg("g_mla"), seg("g_conv"), row, row], out_specs=row,
        out_shape=jax.ShapeDtypeStruct((S, D), BF16), compiler_params=_cparams(("parallel",), 32 << 20), name="merge_fwd",
    )(proj, proj, y_mla, y_conv)


def _post_loss(out, x, target, g_post, ts):
    S = out.shape[0]

    def body(o_ref, x_ref, t_ref, g_ref, dy_ref, do_ref, dg_ref, loss_ref):
        i = pl.program_id(0)
        o = o_ref[...]
        r = _rms_scale(o)
        n = o * r
        g = g_ref[...]
        err = x_ref[...] + n * g - t_ref[...]
        dy = err * (1.0 / D)
        dy_ref[...] = dy
        dn = dy * g
        do_ref[...] = (r * (dn - n * jnp.mean(dn * n, axis=-1, keepdims=True))).astype(BF16)
        dg = jnp.sum(dy * n, axis=0, keepdims=True)
        part = jnp.sum(jnp.sum(err * err, axis=0, keepdims=True), axis=1, keepdims=True) * (0.5 / D)

        @pl.when(i == 0)
        def _():
            dg_ref[...] = jnp.zeros_like(dg_ref)
            loss_ref[...] = jnp.zeros_like(loss_ref)

        dg_ref[0:1, :] += dg
        loss_ref[...] += jnp.broadcast_to(part, loss_ref.shape)

    row = pl.BlockSpec((ts, D), lambda i: (i, 0))
    return pl.pallas_call(
        body, grid=(S // ts,),
        in_specs=[row, row, row, pl.BlockSpec((1, D), lambda i: (0, 0))],
        out_specs=[row, row, pl.BlockSpec((8, D), lambda i: (0, 0)), pl.BlockSpec((8, 128), lambda i: (0, 0))],
        out_shape=[jax.ShapeDtypeStruct((S, D), F32), jax.ShapeDtypeStruct((S, D), BF16),
                   jax.ShapeDtypeStruct((8, D), F32), jax.ShapeDtypeStruct((8, 128), F32)],
        compiler_params=_cparams(("arbitrary",), 40 << 20), name="post_loss",
    )(out, x, target, g_post)


def _merge_bwd(proj, y_mla, y_conv, dmerged, ts):
    S = proj.shape[0]

    def body(gm_ref, gc_ref, ym_ref, yc_ref, dm_ref, dym_ref, dyc_ref, dgm_ref, dgc_ref):
        dm = dm_ref[...]
        sm = _sigmoid(gm_ref[...])
        sc = _sigmoid(gc_ref[...])
        dym_ref[...] = (dm * sm).astype(BF16)
        dyc_ref[...] = (dm * sc).astype(BF16)
        dgm_ref[...] = (dm * ym_ref[...] * (sm * (1.0 - sm))).astype(BF16)
        dgc_ref[...] = (dm * yc_ref[...] * (sc * (1.0 - sc))).astype(BF16)

    seg = lambda name: pl.BlockSpec((ts, D), lambda i: (i, SEG[name]))
    row = pl.BlockSpec((ts, D), lambda i: (i, 0))
    return pl.pallas_call(
        body, grid=(S // ts,), in_specs=[seg("g_mla"), seg("g_conv"), row, row, row], out_specs=[row] * 4,
        out_shape=[jax.ShapeDtypeStruct((S, D), BF16)] * 4, compiler_params=_cparams(("parallel",), 40 << 20), name="merge_bwd",
    )(proj, proj, y_mla, y_conv, dmerged)


def _gates_bwd(proj, attn, da_mla, da_conv, conv_w, ts):
    S = proj.shape[0]
    nblk = S // ts

    def body(attn_ref, zm_ref, cin_ref, bg_ref, cg_ref, zc_ref, dam_ref, dac_ref, cin_p, cg_p, bg_n, zc_n, dac_n, w_ref,
             dattn_ref, delta_ref, dzm_ref, dcin_ref, dbg_ref, dcg_ref, dzc_ref, dw_ref):
        i = pl.program_id(0)
        first = i == 0
        last = i == nblk - 1
        zm = zm_ref[...]
        sg = _sigmoid(zm)
        silu = zm * sg
        attn = attn_ref[...]
        dam = dam_ref[...]
        dattn = dam * silu
        dattn_ref[...] = dattn.astype(BF16)
        dzm_ref[...] = (dam * attn * (sg * (1.0 + zm * (1.0 - sg)))).astype(BF16)
        prod = dattn * attn
        for h in range(H):
            col = jnp.sum(prod[:, h * DV:(h + 1) * DV], axis=1, keepdims=True)
            delta_ref[h] = _col_to_row8(col, ts)
        w = w_ref[...]
        cin, cg, bg, zc = cin_ref[...], cg_ref[...], bg_ref[...], zc_ref[...]
        u = cg * cin
        up = cg_p[...] * cin_p[...]
        u1 = _shift_rows(u, up, 1, first)
        u2 = _shift_rows(u, up, 2, first)
        conv = w[0:1, :] * u2 + w[1:2, :] * u1 + w[2:3, :] * u
        sgc = _sigmoid(zc)
        siluc = zc * sgc
        dac = dac_ref[...]
        dbg_ref[...] = (dac * conv * siluc).astype(BF16)
        dzc_ref[...] = (dac * bg * conv * (sgc * (1.0 + zc * (1.0 - sgc)))).astype(BF16)
        dconv = dac * bg * siluc
        zn = zc_n[...]
        dconv_n = jnp.where(last, 0.0, dac_n[...] * bg_n[...] * (zn * _sigmoid(zn)))
        row = lax.broadcasted_iota(jnp.int32, dconv.shape, 0)
        d1 = jnp.where(row == ts - 1, dconv_n[0:1, :], pltpu.roll(dconv, ts - 1, 0))
        d2 = jnp.where(row == ts - 1, dconv_n[1:2, :], jnp.where(row == ts - 2, dconv_n[0:1, :], pltpu.roll(dconv, ts - 2, 0)))
        du = w[2:3, :] * dconv + w[1:2, :] * d1 + w[0:1, :] * d2
        dcg_ref[...] = (du * cin).astype(BF16)
        dcin_ref[...] = (du * cg).astype(BF16)

        @pl.when(first)
        def _():
            dw_ref[...] = jnp.zeros_like(dw_ref)

        dw_ref[0:1, :] += jnp.sum(dconv * u2, axis=0, keepdims=True)
        dw_ref[1:2, :] += jnp.sum(dconv * u1, axis=0, keepdims=True)
        dw_ref[2:3, :] += jnp.sum(dconv * u, axis=0, keepdims=True)

    seg = lambda name: pl.BlockSpec((ts, D), lambda i: (i, SEG[name]))
    prev = lambda name: pl.BlockSpec((8, D), lambda i: (jnp.maximum(i * (ts // 8) - 1, 0), SEG[name]))
    nxt = lambda blk: pl.BlockSpec((8, D), lambda i: (jnp.minimum((i + 1) * (ts // 8), S // 8 - 1), blk))
    row = pl.BlockSpec((ts, D), lambda i: (i, 0))
    return pl.pallas_call(
        body, grid=(nblk,),
        in_specs=[row, seg("z_mla"), seg("c_in"), seg("b_gate"), seg("c_gate"), seg("z_conv"), row, row,
                  prev("c_in"), prev("c_gate"), nxt(SEG["b_gate"]), nxt(SEG["z_conv"]), nxt(0), pl.BlockSpec((8, D), lambda i: (0, 0))],
        out_specs=[row, pl.BlockSpec((H, 8, ts), lambda i: (0, 0, i)), row, row, row, row, row, pl.BlockSpec((8, D), lambda i: (0, 0))],
        out_shape=[jax.ShapeDtypeStruct((S, D), BF16), jax.ShapeDtypeStruct((H, 8, S), F32)] + [jax.ShapeDtypeStruct((S, D), BF16)] * 5
        + [jax.ShapeDtypeStruct((8, D), F32)],
        compiler_params=_cparams(("arbitrary",), 56 << 20), name="gates_bwd",
    )(attn, proj, proj, proj, proj, proj, da_mla, da_conv, proj, proj, proj, proj, da_conv, conv_w)


def _flash_bwd(q, k, v, do, lse, delta, tq, hb=2):
    S = q.shape[1]
    nq = S // tq
    scale = 1.0 / math.sqrt(DN + DR)
    c2 = scale * LOG2E
    tsub = min(256, tq)
    nsub = tq // tsub

    qi_tab, kj_tab = _causal_pairs(nq, key_major=True)
    npairs = nq * (nq + 1) // 2

    def body(qi_ref, kj_ref, q_ref, k_ref, v_ref, do_ref, lse_ref, dl_ref, dq_ref, dk_ref, dv_ref, dk_sc, dv_sc):
        p = pl.program_id(1)
        qi, kj = qi_ref[p], kj_ref[p]

        @pl.when(p == 0)
        def _():
            dq_ref[...] = jnp.zeros_like(dq_ref)

        @pl.when(qi == kj)
        def _():
            dk_sc[...] = jnp.zeros_like(dk_sc)
            dv_sc[...] = jnp.zeros_like(dv_sc)

        def step(diag):
            chains = [(h, u) for h in range(hb) for u in range(nsub)]

            def first_matmuls(h, u):
                sub = slice(u * tsub, (u + 1) * tsub)
                st = lax.dot_general(k_ref[h], q_ref[h, sub, :], NT, preferred_element_type=F32)
                dpt = lax.dot_general(v_ref[h], do_ref[sub, h * DV:(h + 1) * DV], NT, preferred_element_type=F32)
                return st, dpt

            nxt = first_matmuls(*chains[0])
            for ci, (h, u) in enumerate(chains):
                st, dpt = nxt
                if ci + 1 < len(chains):
                    nxt = first_matmuls(*chains[ci + 1])
                sub = slice(u * tsub, (u + 1) * tsub)
                pt = jnp.exp2(st * c2 - lse_ref[h, 0:1, sub] * LOG2E)
                if diag:
                    r = lax.broadcasted_iota(jnp.int32, (tq, tsub), 0)
                    c = lax.broadcasted_iota(jnp.int32, (tq, tsub), 1) + u * tsub
                    pt = jnp.where(r <= c, pt, 0.0)
                dst = (pt * (dpt - dl_ref[h, 0:1, sub])).astype(BF16)
                dv_sc[h] += jnp.dot(pt.astype(BF16), do_ref[sub, h * DV:(h + 1) * DV], preferred_element_type=F32)
                dk_sc[h] += jnp.dot(dst, q_ref[h, sub, :], preferred_element_type=F32)
                rows = pl.ds(pl.multiple_of(qi * tq + u * tsub, tsub), tsub)
                dq_ref[h, rows, :] += lax.dot_general(dst, k_ref[h], TN, preferred_element_type=F32)

        @pl.when(qi > kj)
        def _():
            step(False)

        @pl.when(qi == kj)
        def _():
            step(True)

        @pl.when(qi == nq - 1)
        def _():
            dk_ref[...] = dk_sc[...] * scale
            dv_ref[...] = dv_sc[...]

        @pl.when(p == npairs - 1)
        def _():
            dq_ref[...] = dq_ref[...] * scale

    return pl.pallas_call(
        body,
        grid_spec=pltpu.PrefetchScalarGridSpec(
            num_scalar_prefetch=2,
            grid=(H // hb, npairs),
            in_specs=[pl.BlockSpec((hb, tq, HW), lambda h, p, qi, kj: (h, qi[p], 0)),
                      pl.BlockSpec((hb, tq, HW), lambda h, p, qi, kj: (h, kj[p], 0)),
                      pl.BlockSpec((hb, tq, DV), lambda h, p, qi, kj: (h, kj[p], 0)),
                      pl.BlockSpec((tq, hb * DV), lambda h, p, qi, kj: (qi[p], h)),
                      pl.BlockSpec((hb, 8, tq), lambda h, p, qi, kj: (h, 0, qi[p])),
                      pl.BlockSpec((hb, 8, tq), lambda h, p, qi, kj: (h, 0, qi[p]))],
            out_specs=[pl.BlockSpec((hb, S, HW), lambda h, p, qi, kj: (h, 0, 0)),
                       pl.BlockSpec((hb, tq, HW), lambda h, p, qi, kj: (h, kj[p], 0)),
                       pl.BlockSpec((hb, tq, DV), lambda h, p, qi, kj: (h, kj[p], 0))],
            scratch_shapes=[pltpu.VMEM((hb, tq, HW), F32), pltpu.VMEM((hb, tq, DV), F32)],
        ),
        out_shape=[jax.ShapeDtypeStruct((H, S, HW), F32), jax.ShapeDtypeStruct((H, S, HW), F32), jax.ShapeDtypeStruct((H, S, DV), F32)],
        compiler_params=_cparams(("parallel", "arbitrary"), VMEM_CAP),
        name="flash_bwd",
    )(qi_tab, kj_tab, q, k, v, do, lse, delta)


def _rms_bwd(xf, g, dn_out):
    r = _rms_scale(xf)
    n = xf * r
    dn = dn_out * g
    dx = r * (dn - n * jnp.mean(dn * n, axis=-1, keepdims=True))
    return dx, jnp.sum(dn_out * n, axis=0, keepdims=True)


def _qkv_bwd(proj, pos, freq, g_qa, g_kva, w_q, w_kv, dq, dk, dv, ts):
    S = proj.shape[0]

    def body(qa_ref, ckv_ref, pos_ref, freq_ref, gq_ref, gk_ref, wq_ref, wkv_ref, dq_ref, dk_ref, dv_ref,
             dqp_ref, dkvp_ref, dqa_ref, dckv_ref, dkr_ref, dgq_ref, dgk_ref):
        i = pl.program_id(0)
        cos, sin, lane = _rope_tables(pos_ref, freq_ref, ts)
        sgn_sin = jnp.where(lane < 32, sin, -sin)
        live = lane < DR
        kr_sum = jnp.zeros((ts, 128), F32)
        for h in range(H):
            dqh = dq_ref[h]
            dqp_ref[:, h * HW:h * HW + DN] = dqh[:, 0:DN].astype(BF16)
            t = dqh[:, DN:HW]
            dqp_ref[:, h * HW + DN:(h + 1) * HW] = jnp.where(live, t * cos + _rope_swap(t, lane) * sgn_sin, 0.0).astype(BF16)
            dkh = dk_ref[h]
            dkvp_ref[:, h * HW:h * HW + DN] = dkh[:, 0:DN].astype(BF16)
            dkvp_ref[:, h * HW + DN:(h + 1) * HW] = dv_ref[h].astype(BF16)
            kr_sum = kr_sum + dkh[:, DN:HW]
        dkr_ref[...] = jnp.where(live, kr_sum * cos + _rope_swap(kr_sum, lane) * sgn_sin, 0.0).astype(BF16)
        dqn = lax.dot_general(dqp_ref[...], wq_ref[...], NT, preferred_element_type=F32)
        dkvn = lax.dot_general(dkvp_ref[...], wkv_ref[...], NT, preferred_element_type=F32)
        dqa, dgq = _rms_bwd(qa_ref[...], gq_ref[...], dqn)
        dckv, dgk = _rms_bwd(ckv_ref[...], gk_ref[...], dkvn)
        dqa_ref[...] = dqa.astype(BF16)
        dckv_ref[...] = dckv.astype(BF16)

        @pl.when(i == 0)
        def _():
            dgq_ref[...] = jnp.zeros_like(dgq_ref)
            dgk_ref[...] = jnp.zeros_like(dgk_ref)

        dgq_ref[0:1, :] += dgq
        dgk_ref[0:1, :] += dgk

    rowb = lambda w, blk: pl.BlockSpec((ts, w), lambda i: (i, blk))
    full = lambda a: pl.BlockSpec(a.shape, lambda i: (0,) * a.ndim)
    heads = lambda w: pl.BlockSpec((H, ts, w), lambda i: (0, i, 0))
    return pl.pallas_call(
        body, grid=(S // ts,),
        in_specs=[rowb(RQ, OFF_QA // RQ), rowb(RKV, OFF_CKV // RKV), pl.BlockSpec((ts, 1), lambda i: (i, 0)), full(freq), full(g_qa), full(g_kva),
                  full(w_q), full(w_kv), heads(HW), heads(HW), heads(DV)],
        out_specs=[rowb(H * HW, 0), rowb(H * HW, 0), rowb(RQ, 0), rowb(RKV, 0), rowb(128, 0),
                   pl.BlockSpec((8, RQ), lambda i: (0, 0)), pl.BlockSpec((8, RKV), lambda i: (0, 0))],
        out_shape=[jax.ShapeDtypeStruct((S, H * HW), BF16), jax.ShapeDtypeStruct((S, H * HW), BF16), jax.ShapeDtypeStruct((S, RQ), BF16),
                   jax.ShapeDtypeStruct((S, RKV), BF16), jax.ShapeDtypeStruct((S, 128), BF16),
                   jax.ShapeDtypeStruct((8, RQ), F32), jax.ShapeDtypeStruct((8, RKV), F32)],
        compiler_params=_cparams(("arbitrary",), 56 << 20), name="qkv_bwd",
    )(proj, proj, pos, freq, g_qa, g_kva, w_q, w_kv, dq, dk, dv)


def _pack_dproj(parts, ts):
    S = parts[0].shape[0]
    widths = [p.shape[1] for p in parts]
    assert sum(widths) == NP

    def body(*refs):
        o_ref = refs[-1]
        off = 0
        for r, w in zip(refs[:-1], widths):
            o_ref[:, off:off + w] = r[...]
            off += w

    return pl.pallas_call(
        body, grid=(S // ts,),
        in_specs=[pl.BlockSpec((ts, w), lambda i: (i, 0)) for w in widths],
        out_specs=pl.BlockSpec((ts, NP), lambda i: (i, 0)),
        out_shape=jax.ShapeDtypeStruct((S, NP), BF16),
        compiler_params=_cparams(("parallel",), 48 << 20), name="pack_dproj",
    )(*parts)


def _prenorm_bwd(x, g, dh, dy, ts):
    S = x.shape[0]

    def body(x_ref, g_ref, dh_ref, dy_ref, gx_ref, dg_ref):
        dx, dg = _rms_bwd(x_ref[...], g_ref[...], dh_ref[...])
        gx_ref[...] = dy_ref[...] + dx

        @pl.when(pl.program_id(0) == 0)
        def _():
            dg_ref[...] = jnp.zeros_like(dg_ref)

        dg_ref[0:1, :] += dg

    row = pl.BlockSpec((ts, D), lambda i: (i, 0))
    return pl.pallas_call(
        body, grid=(S // ts,), in_specs=[row, pl.BlockSpec((1, D), lambda i: (0, 0)), row, row],
        out_specs=[row, pl.BlockSpec((8, D), lambda i: (0, 0))],
        out_shape=[jax.ShapeDtypeStruct((S, D), F32), jax.ShapeDtypeStruct((8, D), F32)],
        compiler_params=_cparams(("arbitrary",), 40 << 20), name="prenorm_bwd",
    )(x, g, dh, dy)


def _local_step(x, pos, target, g_pre, g_qa, g_kva, g_post, w_in, other_weights, reduce_grads=None):
    S = x.shape[0]
    ts = min(256, S)
    tq = min(512, S)
    tm = min(512, S)
    mm = functools.partial(_matmul, tm=tm)
    freq = _inv_freq_tile()

    h, ht = _prenorm(x, g_pre, ts)
    if len(other_weights) == 2:
        gather, assemble = other_weights
        proj, gathered = _matmul(h, w_in, mode="nn", out_dtype=F32, tm=1024, tn=1408, tk=D, name="mm_proj", exchange=gather)
        other_weights = assemble(*gathered)
    else:
        proj = _matmul(h, w_in, mode="nn", out_dtype=F32, tm=1024, tn=1408, tk=D, name="mm_proj")
    w_q, w_kv, conv_w8, w_o_mla, w_o_conv, w_out = other_weights
    q, k, v, vt, qn, kvn = _qkv_prep(proj, pos, freq, g_qa, g_kva, w_q, w_kv, ts)
    attn, lse = _flash_fwd(q, k, vt, tq)
    a_mla, a_conv = _gates_fwd(proj, attn, conv_w8, ts)
    y_mla = mm(a_mla, w_o_mla, mode="nn", out_dtype=F32, tn=1024, tk=D, name="mm_y_mla")
    y_conv = mm(a_conv, w_o_conv, mode="nn", out_dtype=F32, tn=1024, tk=D, name="mm_y_conv")
    merged = _merge_fwd(proj, y_mla, y_conv, ts)
    out = mm(merged, w_out, mode="nn", out_dtype=F32, tn=1024, tk=D, name="mm_out")
    dy, dout, dg_post, loss = _post_loss(out, x, target, g_post, ts)

    dmerged = mm(dout, w_out, mode="nt", out_dtype=F32, tn=1024, tk=D, name="mm_dmerged")
    dw_out = _matmul(merged, dout, mode="tn", out_dtype=F32, tm=1024, tn=1024, tk=1024, name="mm_dw_out")
    dy_mla, dy_conv, dg_mla, dg_conv = _merge_bwd(proj, y_mla, y_conv, dmerged, ts)
    da_mla = mm(dy_mla, w_o_mla, mode="nt", out_dtype=F32, tn=1024, tk=D, name="mm_da_mla")
    da_conv = mm(dy_conv, w_o_conv, mode="nt", out_dtype=F32, tn=1024, tk=D, name="mm_da_conv")
    dw_o_mla = _matmul(a_mla, dy_mla, mode="tn", out_dtype=F32, tm=1024, tn=1024, tk=1024, name="mm_dw_o_mla")
    dw_o_conv = _matmul(a_conv, dy_conv, mode="tn", out_dtype=F32, tm=1024, tn=1024, tk=1024, name="mm_dw_o_conv")
    dattn, delta, dz_mla, dc_in, db_gate, dc_gate, dz_conv, dconv_w = _gates_bwd(proj, attn, da_mla, da_conv, conv_w8, min(128, S))
    dq, dk, dv = _flash_bwd(q, k, v, dattn, lse, delta, tq)
    dqp, dkvp, dq_a, dc_kv, dk_rope, dg_qa, dg_kva = _qkv_bwd(proj, pos, freq, g_qa, g_kva, w_q, w_kv, dq, dk, dv, min(128, S))
    dw_q = _matmul(qn, dqp, mode="tn", out_dtype=F32, tm=RQ, tn=1024, tk=1024, name="mm_dw_q")
    dw_kv = _matmul(kvn, dkvp, mode="tn", out_dtype=F32, tm=RKV, tn=1024, tk=1024, name="mm_dw_kv")
    dproj = _pack_dproj([dz_mla, dc_in, db_gate, dc_gate, dz_conv, dg_mla, dg_conv, dq_a, dc_kv, dk_rope], ts)
    dw_in = _matmul(ht, dproj, mode="nn", out_dtype=F32, tm=1024, tn=1408, tk=1024, name="mm_dw_in")
    res = dict(dw_in=dw_in, dw_q=dw_q, dw_kv=dw_kv, dconv_w=dconv_w, dw_o_mla=dw_o_mla, dw_o_conv=dw_o_conv, dw_out=dw_out)
    if reduce_grads is None:
        dh = _matmul(dproj, w_in, mode="nt", out_dtype=F32, tm=1024, tn=D, tk=1408, name="mm_dh")
    else:
        exchange, finish = reduce_grads(res)
        dh, got = _matmul(dproj, w_in, mode="nt", out_dtype=F32, tm=1024, tn=D, tk=1408, name="mm_dh", exchange=exchange)
        res["reduced"] = finish(got)
    grad_x, dg_pre = _prenorm_bwd(x, g_pre, dh, dy, ts)
    res.update(loss=loss, grad_x=grad_x, dg_pre=dg_pre, dg_qa=dg_qa, dg_kva=dg_kva, dg_post=dg_post)
    return res


def _my_id():
    return lax.axis_index("x") * 4 + lax.axis_index("y") * 2 + lax.axis_index("c")


def _place():
    x, y, c = lax.axis_index("x"), lax.axis_index("y"), lax.axis_index("c")
    return (x, y, c), (x, y, 1 - c), [(1 - x, y), (x, 1 - y), (1 - x, 1 - y)]


def _slot(px, py, pc):
    return 4 * px + 2 * py + pc


def _all_gather(arrays, name):
    n = len(arrays)

    def body(*refs):
        ins, outs = refs[:n], refs[n:2 * n]
        send_sems, recv_sems, local_sems = refs[2 * n:]
        me, sib, chips = _place()
        c = me[2]

        def copy(a, k, block, to, src=None):
            rows = outs[a].at[_slot(*block)]
            return pltpu.make_async_remote_copy(src_ref=rows if src is None else src, dst_ref=rows, send_sem=send_sems.at[a, k],
                                                recv_sem=recv_sems.at[a, k], device_id=to, device_id_type=MESH)

        local = [pltpu.make_async_copy(ins[a], outs[a].at[_slot(*me)], local_sems.at[a]) for a in range(n)]
        for cp in local:
            cp.start()
        sends = []
        for a in range(n):
            sends.append(copy(a, 0, me, sib, src=ins[a]))
            sends += [copy(a, 1 + j, me, (*chip, c), src=ins[a]) for j, chip in enumerate(chips)]
        for cp in sends:
            cp.start()
        for a in range(n):
            for j, chip in enumerate(chips):
                copy(a, 1 + j, (*chip, c), me).wait_recv()
                fwd = copy(a, 4 + j, (*chip, c), sib)
                fwd.start()
                sends.append(fwd)
        for a in range(n):
            copy(a, 0, sib, me).wait_recv()
            for j, chip in enumerate(chips):
                copy(a, 4 + j, (*chip, 1 - c), me).wait_recv()
        for cp in sends:
            cp.wait_send()
        for cp in local:
            cp.wait()

    anyspec = pl.BlockSpec(memory_space=pl.ANY)
    return pl.pallas_call(
        body,
        in_specs=[anyspec] * n,
        out_specs=[anyspec] * n,
        out_shape=[jax.ShapeDtypeStruct((NDEV,) + a.shape, a.dtype) for a in arrays],
        scratch_shapes=[pltpu.SemaphoreType.DMA((n, NDEV - 1)), pltpu.SemaphoreType.DMA((n, NDEV - 1)), pltpu.SemaphoreType.DMA((n,))],
        name=name,
    )(*arrays)


def _slab_exchange(arrays, nslots, pick):
    n = len(arrays)

    def copies(ins, outs, sems):
        send_sems, recv_sems = sems
        return [pltpu.make_async_remote_copy(src_ref=ins[a].at[pick(k)[0]], dst_ref=outs[a].at[k], send_sem=send_sems.at[a, k],
                                             recv_sem=recv_sems.at[a, k], device_id=pick(k)[1], device_id_type=MESH)
                for a in range(n) for k in range(nslots)]

    def start(ins, outs, sems):
        for cp in copies(ins, outs, sems):
            cp.start()

    def finish(ins, outs, sems):
        cps = copies(ins, outs, sems)
        for cp in cps:
            cp.wait_recv()
        for cp in cps:
            cp.wait_send()

    return _Exchange(arrays, [jax.ShapeDtypeStruct((nslots,) + a.shape[1:], a.dtype) for a in arrays],
                     [pltpu.SemaphoreType.DMA((n, nslots)), pltpu.SemaphoreType.DMA((n, nslots))], start, finish)


def _direct_gather(arrays):
    n = len(arrays)

    def peer(d):
        p = (_my_id() + d) % NDEV
        return (p // 4, (p // 2) % 2, p % 2), p

    def copies(ins, outs, sems, receiving):
        send_sems, recv_sems, _ = sems
        slot = lambda d: peer(NDEV - d)[1] if receiving else _my_id()
        return [pltpu.make_async_remote_copy(src_ref=ins[a], dst_ref=outs[a].at[slot(d)], send_sem=send_sems.at[a, d - 1],
                                             recv_sem=recv_sems.at[a, d - 1], device_id=peer(d)[0], device_id_type=MESH)
                for d in range(1, NDEV) for a in range(n)]

    def local(ins, outs, sems):
        return [pltpu.make_async_copy(ins[a], outs[a].at[_my_id()], sems[2].at[a]) for a in range(n)]

    def start(ins, outs, sems):
        for cp in local(ins, outs, sems) + copies(ins, outs, sems, False):
            cp.start()

    def finish(ins, outs, sems):
        for cp in copies(ins, outs, sems, True):
            cp.wait_recv()
        for cp in copies(ins, outs, sems, False):
            cp.wait_send()
        for cp in local(ins, outs, sems):
            cp.wait()

    return _Exchange(arrays, [jax.ShapeDtypeStruct((NDEV,) + a.shape, a.dtype) for a in arrays],
                     [pltpu.SemaphoreType.DMA((n, NDEV - 1)), pltpu.SemaphoreType.DMA((n, NDEV - 1)), pltpu.SemaphoreType.DMA((n,))],
                     start, finish)


def _run_exchange(exchange, name):
    n = len(exchange.arrays)

    def body(*refs):
        ins, outs, sems = refs[:n], refs[n:2 * n], refs[2 * n:]
        exchange.start(ins, outs, sems)
        exchange.finish(ins, outs, sems)

    anyspec = pl.BlockSpec(memory_space=pl.ANY)
    return pl.pallas_call(body, in_specs=[anyspec] * n, out_specs=[anyspec] * n, out_shape=exchange.out_shapes,
                          scratch_shapes=exchange.sem_shapes, name=name)(*exchange.arrays)


def _slabs_bf16(x, rows_per_block, name):
    n, R, C = x.shape
    tr = min(rows_per_block, R)

    def body(x_ref, o_ref):
        o_ref[...] = x_ref[...].astype(BF16)

    blk = pl.BlockSpec((1, tr, C), lambda k, i: (k, i, 0))
    return pl.pallas_call(body, grid=(n, R // tr), in_specs=[blk], out_specs=blk, out_shape=jax.ShapeDtypeStruct(x.shape, BF16),
                          compiler_params=_cparams(("parallel", "parallel"), 32 << 20), name=name)(x)


def _to_sibling(k):
    me, sib, chips = _place()
    dest = sib if k == 0 else (*chips[k - 1], sib[2])
    return _slot(*dest), sib


def _to_chips(k):
    me, sib, chips = _place()
    return k, (*chips[k], me[2])


def _chip_sums(own, got, slots, rows_per_block, name):
    _, R, C = own.shape
    tr = min(rows_per_block, R)

    def body(slots_ref, own_ref, got_ref, o_ref):
        o_ref[0] = (own_ref[0] + got_ref[0].astype(F32)).astype(BF16)

    return pl.pallas_call(
        body,
        grid_spec=pltpu.PrefetchScalarGridSpec(
            num_scalar_prefetch=1, grid=(3, R // tr),
            in_specs=[pl.BlockSpec((1, tr, C), lambda j, i, s: (s[1 + j], i, 0)), pl.BlockSpec((1, tr, C), lambda j, i, s: (1 + j, i, 0))],
            out_specs=pl.BlockSpec((1, tr, C), lambda j, i, s: (j, i, 0))),
        out_shape=jax.ShapeDtypeStruct((3, R, C), BF16),
        compiler_params=_cparams(("parallel", "parallel"), 32 << 20), name=name,
    )(slots, own, got)


def _adamw_math(w, g, m, v):
    m = ADAM_B1 * m + (1.0 - ADAM_B1) * g
    v = ADAM_B2 * v + (1.0 - ADAM_B2) * (g * g)
    m_hat = m / (1.0 - ADAM_B1 ** ADAM_STEP)
    v_hat = v / (1.0 - ADAM_B2 ** ADAM_STEP)
    delta = -ADAM_LR * (m_hat / (jnp.sqrt(v_hat) + ADAM_EPS) + ADAM_WD * w)
    return delta, m, v


def _reduce_adamw(own, got1, got2, slots, w, m, v, rows_per_block, name):
    _, R, C = own.shape
    tr = min(rows_per_block, R)
    assert R % tr == 0

    def body(slots_ref, own_ref, g1_ref, g2_ref, w_ref, m_ref, v_ref, g_ref, d_ref, nm_ref, nv_ref):
        g = own_ref[0] + g1_ref[0].astype(F32)
        for j in range(3):
            g = g + g2_ref[j].astype(F32)
        g_ref[...] = g
        d, nm, nv = _adamw_math(w_ref[...], g, m_ref[...], v_ref[...])
        d_ref[...] = d
        nm_ref[...] = nm
        nv_ref[...] = nv

    blk = pl.BlockSpec((tr, C), lambda i, s: (i, 0))
    return pl.pallas_call(
        body,
        grid_spec=pltpu.PrefetchScalarGridSpec(
            num_scalar_prefetch=1, grid=(R // tr,),
            in_specs=[pl.BlockSpec((1, tr, C), lambda i, s: (s[0], i, 0)), pl.BlockSpec((1, tr, C), lambda i, s: (0, i, 0)),
                      pl.BlockSpec((3, tr, C), lambda i, s: (0, i, 0)), blk, blk, blk],
            out_specs=[blk] * 4),
        out_shape=[jax.ShapeDtypeStruct((R, C), F32)] * 4,
        compiler_params=_cparams(("parallel",), 48 << 20), name=name,
    )(slots, own, got1, got2, w, m, v)


def _sum_adamw(parts, w, m, v, rows_per_block, name):
    n, R, C = parts.shape
    tr = min(rows_per_block, R)
    assert R % tr == 0

    def body(p_ref, w_ref, m_ref, v_ref, g_ref, d_ref, nm_ref, nv_ref):
        g = p_ref[0]
        for j in range(1, n):
            g = g + p_ref[j]
        g_ref[...] = g
        d, nm, nv = _adamw_math(w_ref[...], g, m_ref[...], v_ref[...])
        d_ref[...] = d
        nm_ref[...] = nm
        nv_ref[...] = nv

    blk = pl.BlockSpec((tr, C), lambda i: (i, 0))
    return pl.pallas_call(
        body, grid=(R // tr,),
        in_specs=[pl.BlockSpec((n, tr, C), lambda i: (0, i, 0)), blk, blk, blk],
        out_specs=[blk] * 4,
        out_shape=[jax.ShapeDtypeStruct((R, C), F32)] * 4,
        compiler_params=_cparams(("parallel",), 48 << 20), name=name,
    )(parts, w, m, v)


def kernel(x, positions, pre_norm_g, w_in, q_a_norm_g, w_q_b, kv_a_norm_g, w_kv_b, conv_w, w_o_mla, w_o_conv, w_out, post_norm_g, loss_target, m_pre_norm_g, m_w_in, m_q_a_norm_g, m_w_q_b, m_kv_a_norm_g, m_w_kv_b, m_conv_w, m_w_o_mla, m_w_o_conv, m_w_out, m_post_norm_g, v_pre_norm_g, v_w_in, v_q_a_norm_g, v_w_q_b, v_kv_a_norm_g, v_w_kv_b, v_conv_w, v_w_o_mla, v_w_o_conv, v_w_out, v_post_norm_g):
    S = x.shape[1]
    conv_pad = jnp.zeros((8, 256), F32).at[0:3, :].set(conv_w)
    g_in, = _all_gather([w_in.astype(BF16)], "all_gather_w_in")
    w_in_f = _assemble_w_in(g_in)
    gather_rest = _direct_gather([w_q_b.astype(BF16), w_kv_b.astype(BF16), conv_pad, w_o_mla.astype(BF16), w_o_conv.astype(BF16),
                                  w_out.astype(BF16)])

    def assemble_rest(g_q, g_kv, g_cw, g_om, g_oc, g_out):
        return (_assemble_w_q(g_q), _concat_cols(g_kv, BF16, "assemble_w_kv"), _concat_cols(g_cw, F32, "assemble_conv_w"),
                g_om.reshape(D, D), g_oc.reshape(D, D), g_out.reshape(D, D))

    gnames = ["w_in", "w_q", "w_kv", "conv_w", "w_o_mla", "w_o_conv", "w_out"]
    (mx, my, mc), _, chips = _place()
    slots = jnp.stack([_slot(mx, my, mc)] + [_slot(cx, cy, mc) for cx, cy in chips]).astype(jnp.int32)

    def reduce_grads(r):
        own = [_split_dw_in(r["dw_in"]), _split_dw_q(r["dw_q"]), _split_cols(r["dw_kv"], NDEV, "split_dw_kv"),
               _split_cols(r["dconv_w"], NDEV, "split_dconv_w"), r["dw_o_mla"].reshape(NDEV, D // NDEV, D),
               r["dw_o_conv"].reshape(NDEV, D // NDEV, D), r["dw_out"].reshape(NDEV, D // NDEV, D)]
        to_sibling = _slab_exchange([_slabs_bf16(g, 128, "bf16_" + nm) for g, nm in zip(own, gnames)], 4, _to_sibling)
        got1 = _run_exchange(to_sibling, "grads_to_sibling")
        sums = [_chip_sums(o, g1, slots, 128, "chip_sum_" + nm) for o, g1, nm in zip(own, got1, gnames)]
        return _slab_exchange(sums, 3, _to_chips), lambda got2: (own, got1, got2)

    row2 = lambda a: a.reshape(1, -1)
    r = _local_step(x[0], positions.reshape(S, 1), loss_target[0], row2(pre_norm_g), row2(q_a_norm_g), row2(kv_a_norm_g),
                    row2(post_norm_g), w_in_f, (gather_rest, assemble_rest), reduce_grads)
    own, got1, got2 = r["reduced"]
    small = jnp.concatenate([r["dg_pre"][0:1], r["dg_post"][0:1], jnp.pad(r["dg_qa"][0:1], ((0, 0), (0, D - RQ))),
                             jnp.pad(r["dg_kva"][0:1], ((0, 0), (0, D - RKV))), jnp.pad(r["loss"][0:1], ((0, 0), (0, D - 128))),
                             jnp.zeros((3, D), F32)], axis=0)
    p_small, = _all_gather([small], "all_gather_small")
    pad8 = lambda a: jnp.zeros((8, 256), F32).at[0:3, :].set(a)
    params = [(w_in, m_w_in, v_w_in), (w_q_b, m_w_q_b, v_w_q_b), (w_kv_b, m_w_kv_b, v_w_kv_b), (conv_pad, pad8(m_conv_w), pad8(v_conv_w)),
              (w_o_mla, m_w_o_mla, v_w_o_mla), (w_o_conv, m_w_o_conv, v_w_o_conv), (w_out, m_w_out, v_w_out)]
    o_in, o_q, o_kv, o_cw, o_om, o_oc, o_out = [
        _reduce_adamw(o, g1, g2, slots, w, m, v, 128, "adamw_" + nm)
        for o, g1, g2, (w, m, v), nm in zip(own, got1, got2, params, gnames)]
    o_cw = [a[0:3] for a in o_cw]
    padv = lambda a: jnp.pad(a.reshape(1, -1), ((0, 0), (0, D - a.shape[0])))
    stack = lambda pre, post, qa, kva: jnp.concatenate([row2(pre), row2(post), padv(qa), padv(kva), jnp.zeros((4, D), F32)], axis=0)
    o_g = _sum_adamw(p_small, stack(pre_norm_g, post_norm_g, q_a_norm_g, kv_a_norm_g),
                     stack(m_pre_norm_g, m_post_norm_g, m_q_a_norm_g, m_kv_a_norm_g),
                     stack(v_pre_norm_g, v_post_norm_g, v_q_a_norm_g, v_kv_a_norm_g), 8, "adamw_gains")
    loss = o_g[0][4, 0]
    outs = {}
    for idx, kind in enumerate(("grad", "delta", "new_m", "new_v")):
        o = o_g[idx]
        outs[kind] = dict(pre_norm_g=o[0], w_in=o_in[idx], q_a_norm_g=o[2, 0:RQ], w_q_b=o_q[idx], kv_a_norm_g=o[3, 0:RKV],
                          w_kv_b=o_kv[idx], conv_w=o_cw[idx], w_o_mla=o_om[idx], w_o_conv=o_oc[idx], w_out=o_out[idx], post_norm_g=o[1])
    names = ["pre_norm_g", "w_in", "q_a_norm_g", "w_q_b", "kv_a_norm_g", "w_kv_b", "conv_w", "w_o_mla", "w_o_conv", "w_out", "post_norm_g"]
    return (loss, r["grad_x"][None], *[outs["grad"][n] for n in names], *[outs["delta"][n] for n in names],
            *[outs["new_m"][n] for n in names], *[outs["new_v"][n] for n in names])
```

```python
import functools
import math

import jax
import jax.numpy as jnp
from jax import lax
from jax.experimental import pallas as pl
from jax.experimental.pallas import tpu as pltpu

F32 = jnp.float32
BF16 = jnp.bfloat16

NDEV = 8
D = 2048
H = 16
DN = 128
DR = 64
DV = 128
RQ = 512
RKV = 512
HW = 256
ROPE_THETA = 10000.0
RMS_EPS = 1e-6
N_IN = 15424
SHARD_IN = N_IN // NDEV
SMALL = RQ + RKV + DR
NP = 7 * D + RQ + RKV + 128
SEG = dict(z_mla=0, c_in=1, b_gate=2, c_gate=3, z_conv=4, g_mla=5, g_conv=6)
OFF_QA = 7 * D
OFF_CKV = OFF_QA + RQ
OFF_KR = OFF_CKV + RKV
EXT = 2176
VMEM_CAP = 56 * 1024 * 1024

ADAM_LR = 0.001
ADAM_B1 = 0.9
ADAM_B2 = 0.999
ADAM_EPS = 1e-08
ADAM_WD = 0.01
ADAM_STEP = 10

LOG2E = math.log2(math.e)
NN = (((1,), (0,)), ((), ()))
NT = (((1,), (1,)), ((), ()))
TN = (((0,), (0,)), ((), ()))
MESH = pl.DeviceIdType.MESH


def _cparams(sem, vmem_bytes):
    return pltpu.CompilerParams(dimension_semantics=sem, vmem_limit_bytes=int(min(VMEM_CAP, max(vmem_bytes, 16 << 20))))


def _nbytes(shape, dtype):
    return math.prod(shape) * jnp.dtype(dtype).itemsize


class _Exchange:
    def __init__(self, arrays, out_shapes, sem_shapes, start, finish):
        self.arrays, self.out_shapes, self.sem_shapes, self.start, self.finish = arrays, out_shapes, sem_shapes, start, finish
        self.n_sems = len(sem_shapes)


def _matmul(a, b, *, mode, out_dtype, tm, tn, tk, name, m_outer=False, exchange=None):
    if mode == "nn":
        (M, K), (K2, N) = a.shape, b.shape
    elif mode == "nt":
        (M, K), (N, K2) = a.shape, b.shape
    else:
        (K, M), (K2, N) = a.shape, b.shape
    assert K == K2, (a.shape, b.shape, mode)
    tm, tn, tk = min(tm, M), min(tn, N), min(tk, K)
    assert M % tm == 0 and N % tn == 0 and K % tk == 0, (M, N, K, tm, tn, tk)
    ni, nj, nk = M // tm, N // tn, K // tk
    dims = dict(nn=NN, nt=NT, tn=TN)[mode]

    if m_outer:
        grid = (ni, nj, nk)
        ij = lambda g0, g1: (g0, g1)
    else:
        grid = (nj, ni, nk)
        ij = lambda g0, g1: (g1, g0)

    if mode == "tn":
        a_spec = pl.BlockSpec((tk, tm), lambda g0, g1, k: (k, ij(g0, g1)[0]))
        a_tile = (tk, tm)
    else:
        a_spec = pl.BlockSpec((tm, tk), lambda g0, g1, k: (ij(g0, g1)[0], k))
        a_tile = (tm, tk)
    if mode == "nt":
        b_spec = pl.BlockSpec((tn, tk), lambda g0, g1, k: (ij(g0, g1)[1], k))
    else:
        b_spec = pl.BlockSpec((tk, tn), lambda g0, g1, k: (k, ij(g0, g1)[1]))
    o_spec = pl.BlockSpec((tm, tn), lambda g0, g1, k: ij(g0, g1))

    n_in = len(exchange.arrays) if exchange else 0

    def body(a_ref, b_ref, *refs):
        x_in, o_ref, x_out, scratch = refs[:n_in], refs[n_in], refs[n_in + 1:2 * n_in + 1], refs[2 * n_in + 1:]
        if exchange:
            sems = scratch[len(scratch) - exchange.n_sems:]
            step = (pl.program_id(0) * grid[1] + pl.program_id(1)) * grid[2] + pl.program_id(2)

            @pl.when(step == 0)
            def _():
                exchange.start(x_in, x_out, sems)

        prod = lax.dot_general(a_ref[...], b_ref[...], dims, preferred_element_type=F32)
        if nk == 1:
            o_ref[...] = prod.astype(o_ref.dtype)
        else:
            acc_ref = scratch[0]
            k = pl.program_id(2)

            @pl.when(k == 0)
            def _():
                acc_ref[...] = prod

            @pl.when(k > 0)
            def _():
                acc_ref[...] += prod

            @pl.when(k == nk - 1)
            def _():
                o_ref[...] = acc_ref[...].astype(o_ref.dtype)

        if exchange:
            @pl.when(step == grid[0] * grid[1] * grid[2] - 1)
            def _():
                exchange.finish(x_in, x_out, sems)

    vmem = 2 * (_nbytes(a_tile, a.dtype) + _nbytes((tk, tn), b.dtype) + _nbytes((tm, tn), out_dtype)) + 2 * _nbytes((tm, tn), F32)
    anyspec = pl.BlockSpec(memory_space=pl.ANY)
    outs = pl.pallas_call(
        body,
        grid=grid,
        in_specs=[a_spec, b_spec] + [anyspec] * n_in,
        out_specs=[o_spec] + [anyspec] * n_in,
        out_shape=[jax.ShapeDtypeStruct((M, N), out_dtype)] + (exchange.out_shapes if exchange else []),
        scratch_shapes=([] if nk == 1 else [pltpu.VMEM((tm, tn), F32)]) + (exchange.sem_shapes if exchange else []),
        compiler_params=_cparams(("arbitrary",) * 3 if exchange else ("parallel", "parallel", "arbitrary"), vmem + (8 << 20)),
        name=name,
    )(a, b, *(exchange.arrays if exchange else []))
    return (outs[0], outs[1:]) if exchange else outs[0]


def _in_dest(k):
    return SHARD_IN * k - SMALL


def _assemble_w_in(g):
    R = 128

    def body(g_ref, o_ref, ext, acc):
        ext[...] = jnp.zeros_like(ext)
        acc[...] = jnp.zeros_like(acc)
        lane = lax.broadcasted_iota(jnp.int32, (R, EXT), 1)
        ext[:, 0:SHARD_IN] = g_ref[0].astype(F32)
        v = ext[...]
        acc[:, OFF_QA:NP] = jnp.where(lane[:, 0:NP - OFF_QA] < SMALL, v[:, 0:NP - OFF_QA], 0.0)
        w = v[:, 1024:2048]
        w = pltpu.roll(w, 1024 - 64, 1)
        acc[:, 0:1024] = jnp.where(lane[:, 0:1024] < SHARD_IN - SMALL, w, 0.0)
        for k in range(1, NDEV):
            ext[:, 0:SHARD_IN] = g_ref[k].astype(F32)
            dest = _in_dest(k)
            t, o = dest // 128, dest % 128
            width = -(-(o + SHARD_IN) // 128) * 128
            v = pltpu.roll(ext[...], o, 1)[:, 0:width]
            acc[:, 128 * t:128 * t + width] += v
        o_ref[...] = acc[...].astype(BF16)

    return pl.pallas_call(
        body,
        grid=(D // R,),
        in_specs=[pl.BlockSpec((NDEV, R, SHARD_IN), lambda i: (0, i, 0))],
        out_specs=pl.BlockSpec((R, NP), lambda i: (i, 0)),
        out_shape=jax.ShapeDtypeStruct((D, NP), BF16),
        scratch_shapes=[pltpu.VMEM((R, EXT), F32), pltpu.VMEM((R, NP), F32)],
        compiler_params=_cparams(("parallel",), 40 << 20),
        name="assemble_w_in",
    )(g)


def _split_dw_in(dw):
    R = 128

    def body(dw_ref, o_ref, ob_ref):
        def put(k, cols, v):
            o_ref[k, :, cols] = v
            ob_ref[k, :, cols] = v.astype(BF16)

        lane = lax.broadcasted_iota(jnp.int32, (R, 1024), 1)
        put(0, slice(0, 1024), dw_ref[:, OFF_QA:OFF_QA + 1024])
        tail = dw_ref[:, OFF_QA + 1024:NP]
        tail = jnp.concatenate([tail, jnp.zeros((R, 1024 - 128), F32)], axis=1)
        head = dw_ref[:, 0:1024]
        mixed = jnp.where(lane < 64, tail, pltpu.roll(head, 64, 1))
        put(0, slice(1024, SHARD_IN), mixed[:, 0:SHARD_IN - 1024])
        for k in range(1, NDEV):
            dest = _in_dest(k)
            t, o = dest // 128, dest % 128
            width = -(-(o + SHARD_IN) // 128) * 128
            v = dw_ref[:, 128 * t:128 * t + width]
            v = pltpu.roll(v, width - o, 1)
            put(k, slice(0, SHARD_IN), v[:, 0:SHARD_IN])

    blk = pl.BlockSpec((NDEV, R, SHARD_IN), lambda i: (0, i, 0))
    return pl.pallas_call(
        body,
        grid=(D // R,),
        in_specs=[pl.BlockSpec((R, NP), lambda i: (i, 0))],
        out_specs=[blk, blk],
        out_shape=[jax.ShapeDtypeStruct((NDEV, D, SHARD_IN), F32), jax.ShapeDtypeStruct((NDEV, D, SHARD_IN), BF16)],
        compiler_params=_cparams(("parallel",), 48 << 20),
        name="split_dw_in",
    )(dw)


def _assemble_w_q(g):
    def body(g_ref, o_ref):
        lane = lax.broadcasted_iota(jnp.int32, (RQ, 128), 1)
        lo = lane < 64
        for k in range(NDEV):
            t0 = g_ref[k, :, 0:128].astype(F32)
            t1 = g_ref[k, :, 128:256].astype(F32)
            t2 = g_ref[k, :, 256:384].astype(F32)
            base = 2 * k * HW
            o_ref[:, base:base + 128] = t0.astype(BF16)
            o_ref[:, base + 128:base + 256] = jnp.where(lo, t1, 0.0).astype(BF16)
            o_ref[:, base + 256:base + 384] = pltpu.roll(jnp.where(lo, t2, t1), 64, 1).astype(BF16)
            o_ref[:, base + 384:base + 512] = jnp.where(lo, pltpu.roll(t2, 64, 1), 0.0).astype(BF16)

    return pl.pallas_call(
        body,
        out_shape=jax.ShapeDtypeStruct((RQ, H * HW), BF16),
        compiler_params=_cparams(None, 32 << 20),
        name="assemble_w_q",
    )(g)


def _split_dw_q(dw):
    def body(dw_ref, o_ref):
        lane = lax.broadcasted_iota(jnp.int32, (RQ, 128), 1)
        lo = lane < 64
        for k in range(NDEV):
            base = 2 * k * HW
            a = dw_ref[:, base:base + 128]
            b = dw_ref[:, base + 128:base + 256]
            c = pltpu.roll(dw_ref[:, base + 256:base + 384], 64, 1)
            d = pltpu.roll(dw_ref[:, base + 384:base + 512], 64, 1)
            o_ref[k, :, 0:128] = a
            o_ref[k, :, 128:256] = jnp.where(lo, b, c)
            o_ref[k, :, 256:384] = jnp.where(lo, c, d)

    return pl.pallas_call(
        body,
        out_shape=jax.ShapeDtypeStruct((NDEV, RQ, 384), F32),
        compiler_params=_cparams(None, 32 << 20),
        name="split_dw_q",
    )(dw)


def _concat_cols(g, dtype, name):
    n, R, C = g.shape

    def body(g_ref, o_ref):
        for k in range(n):
            o_ref[:, k * C:(k + 1) * C] = g_ref[k].astype(dtype)

    return pl.pallas_call(body, out_shape=jax.ShapeDtypeStruct((R, n * C), dtype), compiler_params=_cparams(None, 32 << 20), name=name)(g)


def _split_cols(x, n, name):
    R, NC = x.shape
    C = NC // n

    def body(x_ref, o_ref):
        for k in range(n):
            o_ref[k] = x_ref[:, k * C:(k + 1) * C]

    return pl.pallas_call(body, out_shape=jax.ShapeDtypeStruct((n, R, C), x.dtype), compiler_params=_cparams(None, 32 << 20), name=name)(x)


def _rms_scale(xf):
    return lax.rsqrt(jnp.mean(xf * xf, axis=-1, keepdims=True) + RMS_EPS)


def _prenorm(x, g, ts):
    S = x.shape[0]

    def body(x_ref, g_ref, h_ref, ht_ref):
        xf = x_ref[...]
        h = xf * _rms_scale(xf) * g_ref[...]
        h_ref[...] = h.astype(BF16)
        ht_ref[...] = jnp.transpose(h).astype(BF16)

    return pl.pallas_call(
        body,
        grid=(S // ts,),
        in_specs=[pl.BlockSpec((ts, D), lambda i: (i, 0)), pl.BlockSpec((1, D), lambda i: (0, 0))],
        out_specs=[pl.BlockSpec((ts, D), lambda i: (i, 0)), pl.BlockSpec((D, ts), lambda i: (0, i))],
        out_shape=[jax.ShapeDtypeStruct((S, D), BF16), jax.ShapeDtypeStruct((D, S), BF16)],
        compiler_params=_cparams(("parallel",), 32 << 20),
        name="prenorm",
    )(x, g)


def _inv_freq_tile():
    inv_freq = ROPE_THETA ** (-jnp.arange(0, DR, 2, dtype=F32) / DR)
    return jnp.tile(inv_freq, 4).reshape(1, 128)


def _rope_tables(pos_ref, freq_ref, ts):
    lane = lax.broadcasted_iota(jnp.int32, (ts, 128), 1)
    ang = pos_ref[...].astype(F32) * freq_ref[...]
    return jnp.cos(ang), jnp.sin(ang), lane


def _rope_swap(t, lane):
    return jnp.where(lane < 32, pltpu.roll(t, 96, 1), pltpu.roll(t, 32, 1))


def _qkv_prep(proj, pos, freq, g_qa, g_kva, w_q, w_kv, ts):
    S = proj.shape[0]

    def body(qa_ref, ckv_ref, kr_ref, pos_ref, freq_ref, gq_ref, gk_ref, wq_ref, wkv_ref, q_ref, k_ref, v_ref, vt_ref, qn_ref, kvn_ref):
        qa = _f32(qa_ref)
        qn = (qa * _rms_scale(qa) * gq_ref[...]).astype(BF16)
        ckv = _f32(ckv_ref)
        kvn = (ckv * _rms_scale(ckv) * gk_ref[...]).astype(BF16)
        qn_ref[...] = qn
        kvn_ref[...] = kvn
        cos, sin, lane = _rope_tables(pos_ref, freq_ref, ts)
        sgn_sin = jnp.where(lane < 32, -sin, sin)
        live = lane < DR
        kr = _f32(kr_ref)
        kr = jnp.where(live, kr * cos + _rope_swap(kr, lane) * sgn_sin, 0.0).astype(BF16)
        qf = jnp.dot(qn, wq_ref[...], preferred_element_type=F32)
        kvf = jnp.dot(kvn, wkv_ref[...], preferred_element_type=F32)
        for h in range(H):
            q_ref[h, :, 0:DN] = qf[:, h * HW:h * HW + DN].astype(BF16)
            t = qf[:, h * HW + DN:(h + 1) * HW]
            q_ref[h, :, DN:HW] = jnp.where(live, t * cos + _rope_swap(t, lane) * sgn_sin, 0.0).astype(BF16)
            k_ref[h, :, 0:DN] = kvf[:, h * HW:h * HW + DN].astype(BF16)
            k_ref[h, :, DN:HW] = kr
            vh = kvf[:, h * HW + DN:(h + 1) * HW]
            v_ref[h] = vh.astype(BF16)
            vt_ref[h] = jnp.transpose(vh).astype(BF16)

    row = lambda w, blk: pl.BlockSpec((ts, w), lambda i: (i, blk))
    full = lambda a: pl.BlockSpec(a.shape, lambda i: (0,) * a.ndim)
    return pl.pallas_call(
        body,
        grid=(S // ts,),
        in_specs=[row(RQ, OFF_QA // RQ), row(RKV, OFF_CKV // RKV), row(128, OFF_KR // 128),
                  pl.BlockSpec((ts, 1), lambda i: (i, 0)), full(freq), full(g_qa), full(g_kva), full(w_q), full(w_kv)],
        out_specs=[pl.BlockSpec((H, ts, HW), lambda i: (0, i, 0)), pl.BlockSpec((H, ts, HW), lambda i: (0, i, 0)),
                   pl.BlockSpec((H, ts, DV), lambda i: (0, i, 0)), pl.BlockSpec((H, DV, ts), lambda i: (0, 0, i)),
                   pl.BlockSpec((ts, RQ), lambda i: (i, 0)), pl.BlockSpec((ts, RKV), lambda i: (i, 0))],
        out_shape=[jax.ShapeDtypeStruct((H, S, HW), BF16), jax.ShapeDtypeStruct((H, S, HW), BF16),
                   jax.ShapeDtypeStruct((H, S, DV), BF16), jax.ShapeDtypeStruct((H, DV, S), BF16),
                   jax.ShapeDtypeStruct((S, RQ), BF16), jax.ShapeDtypeStruct((S, RKV), BF16)],
        compiler_params=_cparams(("parallel",), 48 << 20),
        name="qkv_prep",
    )(proj, proj, proj, pos, freq, g_qa, g_kva, w_q, w_kv)


def _col_to_row8(col, n):
    return jnp.transpose(jnp.broadcast_to(col, (n, 128)))[0:8, :]


def _causal_pairs(n, key_major):
    if key_major:
        pairs = [(i, j) for j in range(n) for i in range(j, n)]
    else:
        pairs = [(i, j) for i in range(n) for j in range(i + 1)]
    return jnp.array([p[0] for p in pairs], jnp.int32), jnp.array([p[1] for p in pairs], jnp.int32)


def _flash_fwd(q, k, vt, tq, hb=8):
    S = q.shape[1]
    nq = S // tq
    scale = 1.0 / math.sqrt(DN + DR)
    c2 = scale * math.log2(math.e)
    tsub = tq
    nsub = tq // tsub

    qi_tab, kj_tab = _causal_pairs(nq, key_major=False)

    def body(qi_ref, kj_ref, q_ref, k_ref, vt_ref, o_ref, lse_ref, m_sc, l_sc, acc_sc):
        p = pl.program_id(1)
        qi, kj = qi_ref[p], kj_ref[p]

        @pl.when(kj == 0)
        def _():
            m_sc[...] = jnp.full_like(m_sc, -jnp.inf)
            l_sc[...] = jnp.zeros_like(l_sc)
            acc_sc[...] = jnp.zeros_like(acc_sc)

        def step(diag):
            chains = [(h, u) for h in range(hb) for u in range(nsub)]

            def scores(h, u):
                return lax.dot_general(k_ref[h], q_ref[h, u * tsub:(u + 1) * tsub, :], NT, preferred_element_type=F32)

            st_next = scores(*chains[0])
            for ci, (h, u) in enumerate(chains):
                st = st_next
                if ci + 1 < len(chains):
                    st_next = scores(*chains[ci + 1])
                cols = slice(u * tsub, (u + 1) * tsub)
                if diag:
                    r = lax.broadcasted_iota(jnp.int32, (tq, tsub), 0)
                    c = lax.broadcasted_iota(jnp.int32, (tq, tsub), 1) + u * tsub
                    st = jnp.where(r <= c, st, -jnp.inf)
                m_prev = m_sc[h, 0:1, cols]
                m_new = jnp.maximum(m_prev, jnp.max(st, axis=0, keepdims=True))
                alpha = jnp.exp2((m_prev - m_new) * c2)
                pt = jnp.exp2((st - m_new) * c2)
                l_sc[h, :, cols] = jnp.broadcast_to(alpha * l_sc[h, 0:1, cols] + jnp.sum(pt, axis=0, keepdims=True), (8, tsub))
                m_sc[h, :, cols] = jnp.broadcast_to(m_new, (8, tsub))
                acc_sc[h, :, cols] = alpha * acc_sc[h, :, cols] + jnp.dot(vt_ref[h], pt.astype(BF16), preferred_element_type=F32)

        @pl.when(kj < qi)
        def _():
            step(False)

        @pl.when(kj == qi)
        def _():
            step(True)
            for h in range(hb):
                l = l_sc[h, 0:1, :]
                o_ref[:, h * DV:(h + 1) * DV] = jnp.transpose(acc_sc[h] / l).astype(BF16)
                lse_ref[h] = m_sc[h] * scale + jnp.log(l_sc[h])

    return pl.pallas_call(
        body,
        grid_spec=pltpu.PrefetchScalarGridSpec(
            num_scalar_prefetch=2,
            grid=(H // hb, len(qi_tab)),
            in_specs=[pl.BlockSpec((hb, tq, HW), lambda h, p, qi, kj: (h, qi[p], 0)),
                      pl.BlockSpec((hb, tq, HW), lambda h, p, qi, kj: (h, kj[p], 0)),
                      pl.BlockSpec((hb, DV, tq), lambda h, p, qi, kj: (h, 0, kj[p]))],
            out_specs=[pl.BlockSpec((tq, hb * DV), lambda h, p, qi, kj: (qi[p], h)),
                       pl.BlockSpec((hb, 8, tq), lambda h, p, qi, kj: (h, 0, qi[p]))],
            scratch_shapes=[pltpu.VMEM((hb, 8, tq), F32), pltpu.VMEM((hb, 8, tq), F32), pltpu.VMEM((hb, DV, tq), F32)],
        ),
        out_shape=[jax.ShapeDtypeStruct((S, H * DV), BF16), jax.ShapeDtypeStruct((H, 8, S), F32)],
        compiler_params=_cparams(("parallel", "arbitrary"), 40 << 20),
        name="flash_fwd",
    )(qi_tab, kj_tab, q, k, vt)


def _sigmoid(x):
    return 1.0 / (1.0 + jnp.exp(-x))


HALO = 16


def _f32(ref):
    return ref[...].astype(F32)


def _shift_rows(u, prev, n, first):
    ts = u.shape[0]
    row = lax.broadcasted_iota(jnp.int32, u.shape, 0)
    out = pltpu.roll(u, n, 0)
    for j in range(n):
        halo = jnp.where(first, 0.0, prev[HALO - n + j:HALO - n + j + 1, :])
        out = jnp.where(row == j, halo, out)
    return out


def _gates_fwd(proj, attn, conv_w, ts):
    S = proj.shape[0]

    def body(attn_ref, zm_ref, cin_ref, bg_ref, cg_ref, zc_ref, cin_p, cg_p, w_ref, am_ref, ac_ref):
        first = pl.program_id(0) == 0
        zm = _f32(zm_ref)
        am_ref[...] = (_f32(attn_ref) * (zm * _sigmoid(zm))).astype(BF16)
        u = _f32(cg_ref) * _f32(cin_ref)
        up = _f32(cg_p) * _f32(cin_p)
        w = w_ref[...]
        conv = w[0:1, :] * _shift_rows(u, up, 2, first) + w[1:2, :] * _shift_rows(u, up, 1, first) + w[2:3, :] * u
        zc = _f32(zc_ref)
        ac_ref[...] = (_f32(bg_ref) * conv * (zc * _sigmoid(zc))).astype(BF16)

    seg = lambda name: pl.BlockSpec((ts, D), lambda i: (i, SEG[name]))
    prev = lambda name: pl.BlockSpec((HALO, D), lambda i: (jnp.maximum(i * (ts // HALO) - 1, 0), SEG[name]))
    return pl.pallas_call(
        body,
        grid=(S // ts,),
        in_specs=[pl.BlockSpec((ts, D), lambda i: (i, 0)), seg("z_mla"), seg("c_in"), seg("b_gate"), seg("c_gate"), seg("z_conv"),
                  prev("c_in"), prev("c_gate"), pl.BlockSpec((8, D), lambda i: (0, 0))],
        out_specs=[pl.BlockSpec((ts, D), lambda i: (i, 0))] * 2,
        out_shape=[jax.ShapeDtypeStruct((S, D), BF16)] * 2,
        compiler_params=_cparams(("arbitrary",), 48 << 20),
        name="gates_fwd",
    )(attn, proj, proj, proj, proj, proj, proj, proj, conv_w)


def _merge_fwd(proj, y_mla, y_conv, ts):
    S = proj.shape[0]

    def body(gm_ref, gc_ref, ym_ref, yc_ref, o_ref):
        o_ref[...] = (_sigmoid(_f32(gm_ref)) * _f32(ym_ref) + _sigmoid(_f32(gc_ref)) * _f32(yc_ref)).astype(BF16)

    seg = lambda name: pl.BlockSpec((ts, D), lambda i: (i, SEG[name]))
    row = pl.BlockSpec((ts, D), lambda i: (i, 0))
    return pl.pallas_call(
        body, grid=(S // ts,), in_specs=[seg("g_mla"), seg("g_conv"), row, row], out_specs=row,
        out_shape=jax.ShapeDtypeStruct((S, D), BF16), compiler_params=_cparams(("parallel",), 32 << 20), name="merge_fwd",
    )(proj, proj, y_mla, y_conv)


def _post_loss(out, x, target, g_post, ts):
    S = out.shape[0]

    def body(o_ref, x_ref, t_ref, g_ref, dy_ref, do_ref, dg_ref, loss_ref):
        i = pl.program_id(0)
        o = o_ref[...]
        r = _rms_scale(o)
        n = o * r
        g = g_ref[...]
        err = x_ref[...] + n * g - t_ref[...]
        dy = err * (1.0 / D)
        dy_ref[...] = dy
        dn = dy * g
        do_ref[...] = (r * (dn - n * jnp.mean(dn * n, axis=-1, keepdims=True))).astype(BF16)
        dg = jnp.sum(dy * n, axis=0, keepdims=True)
        part = jnp.sum(jnp.sum(err * err, axis=0, keepdims=True), axis=1, keepdims=True) * (0.5 / D)

        @pl.when(i == 0)
        def _():
            dg_ref[...] = jnp.zeros_like(dg_ref)
            loss_ref[...] = jnp.zeros_like(loss_ref)

        dg_ref[0:1, :] += dg
        loss_ref[...] += jnp.broadcast_to(part, loss_ref.shape)

    row = pl.BlockSpec((ts, D), lambda i: (i, 0))
    return pl.pallas_call(
        body, grid=(S // ts,),
        in_specs=[row, row, row, pl.BlockSpec((1, D), lambda i: (0, 0))],
        out_specs=[row, row, pl.BlockSpec((8, D), lambda i: (0, 0)), pl.BlockSpec((8, 128), lambda i: (0, 0))],
        out_shape=[jax.ShapeDtypeStruct((S, D), F32), jax.ShapeDtypeStruct((S, D), BF16),
                   jax.ShapeDtypeStruct((8, D), F32), jax.ShapeDtypeStruct((8, 128), F32)],
        compiler_params=_cparams(("arbitrary",), 40 << 20), name="post_loss",
    )(out, x, target, g_post)


def _merge_bwd(proj, y_mla, y_conv, dmerged, ts):
    S = proj.shape[0]

    def body(gm_ref, gc_ref, ym_ref, yc_ref, dm_ref, dym_ref, dyc_ref, dgm_ref, dgc_ref):
        dm = _f32(dm_ref)
        sm = _sigmoid(_f32(gm_ref))
        sc = _sigmoid(_f32(gc_ref))
        dym_ref[...] = (dm * sm).astype(BF16)
        dyc_ref[...] = (dm * sc).astype(BF16)
        dgm_ref[...] = (dm * _f32(ym_ref) * (sm * (1.0 - sm))).astype(BF16)
        dgc_ref[...] = (dm * _f32(yc_ref) * (sc * (1.0 - sc))).astype(BF16)

    seg = lambda name: pl.BlockSpec((ts, D), lambda i: (i, SEG[name]))
    row = pl.BlockSpec((ts, D), lambda i: (i, 0))
    return pl.pallas_call(
        body, grid=(S // ts,), in_specs=[seg("g_mla"), seg("g_conv"), row, row, row], out_specs=[row] * 4,
        out_shape=[jax.ShapeDtypeStruct((S, D), BF16)] * 4, compiler_params=_cparams(("parallel",), 40 << 20), name="merge_bwd",
    )(proj, proj, y_mla, y_conv, dmerged)


def _gates_bwd(proj, attn, da_mla, da_conv, conv_w, ts):
    S = proj.shape[0]
    nblk = S // ts

    def body(attn_ref, zm_ref, cin_ref, bg_ref, cg_ref, zc_ref, dam_ref, dac_ref, cin_p, cg_p, bg_n, zc_n, dac_n, w_ref,
             dattn_ref, delta_ref, dzm_ref, dcin_ref, dbg_ref, dcg_ref, dzc_ref, dw_ref):
        i = pl.program_id(0)
        first = i == 0
        last = i == nblk - 1
        zm = _f32(zm_ref)
        sg = _sigmoid(zm)
        silu = zm * sg
        attn = _f32(attn_ref)
        dam = _f32(dam_ref)
        dattn = dam * silu
        dattn_ref[...] = dattn.astype(BF16)
        dzm_ref[...] = (dam * attn * (sg * (1.0 + zm * (1.0 - sg)))).astype(BF16)
        prod = dattn * attn
        for h in range(H):
            col = jnp.sum(prod[:, h * DV:(h + 1) * DV], axis=1, keepdims=True)
            delta_ref[h] = _col_to_row8(col, ts)
        w = w_ref[...]
        cin, cg, bg, zc = _f32(cin_ref), _f32(cg_ref), _f32(bg_ref), _f32(zc_ref)
        u = cg * cin
        up = _f32(cg_p) * _f32(cin_p)
        u1 = _shift_rows(u, up, 1, first)
        u2 = _shift_rows(u, up, 2, first)
        conv = w[0:1, :] * u2 + w[1:2, :] * u1 + w[2:3, :] * u
        sgc = _sigmoid(zc)
        siluc = zc * sgc
        dac = _f32(dac_ref)
        dbg_ref[...] = (dac * conv * siluc).astype(BF16)
        dzc_ref[...] = (dac * bg * conv * (sgc * (1.0 + zc * (1.0 - sgc)))).astype(BF16)
        dconv = dac * bg * siluc
        zn = _f32(zc_n)
        dconv_n = jnp.where(last, 0.0, _f32(dac_n) * _f32(bg_n) * (zn * _sigmoid(zn)))
        row = lax.broadcasted_iota(jnp.int32, dconv.shape, 0)
        d1 = jnp.where(row == ts - 1, dconv_n[0:1, :], pltpu.roll(dconv, ts - 1, 0))
        d2 = jnp.where(row == ts - 1, dconv_n[1:2, :], jnp.where(row == ts - 2, dconv_n[0:1, :], pltpu.roll(dconv, ts - 2, 0)))
        du = w[2:3, :] * dconv + w[1:2, :] * d1 + w[0:1, :] * d2
        dcg_ref[...] = (du * cin).astype(BF16)
        dcin_ref[...] = (du * cg).astype(BF16)

        @pl.when(first)
        def _():
            dw_ref[...] = jnp.zeros_like(dw_ref)

        dw_ref[0:1, :] += jnp.sum(dconv * u2, axis=0, keepdims=True)
        dw_ref[1:2, :] += jnp.sum(dconv * u1, axis=0, keepdims=True)
        dw_ref[2:3, :] += jnp.sum(dconv * u, axis=0, keepdims=True)

    seg = lambda name: pl.BlockSpec((ts, D), lambda i: (i, SEG[name]))
    prev = lambda name: pl.BlockSpec((HALO, D), lambda i: (jnp.maximum(i * (ts // HALO) - 1, 0), SEG[name]))
    nxt = lambda blk: pl.BlockSpec((HALO, D), lambda i: (jnp.minimum((i + 1) * (ts // HALO), S // HALO - 1), blk))
    row = pl.BlockSpec((ts, D), lambda i: (i, 0))
    return pl.pallas_call(
        body, grid=(nblk,),
        in_specs=[row, seg("z_mla"), seg("c_in"), seg("b_gate"), seg("c_gate"), seg("z_conv"), row, row,
                  prev("c_in"), prev("c_gate"), nxt(SEG["b_gate"]), nxt(SEG["z_conv"]), nxt(0), pl.BlockSpec((8, D), lambda i: (0, 0))],
        out_specs=[row, pl.BlockSpec((H, 8, ts), lambda i: (0, 0, i)), row, row, row, row, row, pl.BlockSpec((8, D), lambda i: (0, 0))],
        out_shape=[jax.ShapeDtypeStruct((S, D), BF16), jax.ShapeDtypeStruct((H, 8, S), F32)] + [jax.ShapeDtypeStruct((S, D), BF16)] * 5
        + [jax.ShapeDtypeStruct((8, D), F32)],
        compiler_params=_cparams(("arbitrary",), 56 << 20), name="gates_bwd",
    )(attn, proj, proj, proj, proj, proj, da_mla, da_conv, proj, proj, proj, proj, da_conv, conv_w)


def _flash_bwd(q, k, v, do, lse, delta, tq, hb=2):
    S = q.shape[1]
    nq = S // tq
    scale = 1.0 / math.sqrt(DN + DR)
    c2 = scale * LOG2E
    tsub = min(256, tq)
    nsub = tq // tsub

    qi_tab, kj_tab = _causal_pairs(nq, key_major=True)
    npairs = nq * (nq + 1) // 2

    def body(qi_ref, kj_ref, q_ref, k_ref, v_ref, do_ref, lse_ref, dl_ref, dq_ref, dk_ref, dv_ref, dk_sc, dv_sc):
        p = pl.program_id(1)
        qi, kj = qi_ref[p], kj_ref[p]

        @pl.when(p == 0)
        def _():
            dq_ref[...] = jnp.zeros_like(dq_ref)

        @pl.when(qi == kj)
        def _():
            dk_sc[...] = jnp.zeros_like(dk_sc)
            dv_sc[...] = jnp.zeros_like(dv_sc)

        def step(diag):
            chains = [(h, u) for h in range(hb) for u in range(nsub)]

            def first_matmuls(h, u):
                sub = slice(u * tsub, (u + 1) * tsub)
                st = lax.dot_general(k_ref[h], q_ref[h, sub, :], NT, preferred_element_type=F32)
                dpt = lax.dot_general(v_ref[h], do_ref[sub, h * DV:(h + 1) * DV], NT, preferred_element_type=F32)
                return st, dpt

            nxt = first_matmuls(*chains[0])
            for ci, (h, u) in enumerate(chains):
                st, dpt = nxt
                if ci + 1 < len(chains):
                    nxt = first_matmuls(*chains[ci + 1])
                sub = slice(u * tsub, (u + 1) * tsub)
                pt = jnp.exp2(st * c2 - lse_ref[h, 0:1, sub] * LOG2E)
                if diag:
                    r = lax.broadcasted_iota(jnp.int32, (tq, tsub), 0)
                    c = lax.broadcasted_iota(jnp.int32, (tq, tsub), 1) + u * tsub
                    pt = jnp.where(r <= c, pt, 0.0)
                dst = (pt * (dpt - dl_ref[h, 0:1, sub])).astype(BF16)
                dv_sc[h] += jnp.dot(pt.astype(BF16), do_ref[sub, h * DV:(h + 1) * DV], preferred_element_type=F32)
                dk_sc[h] += jnp.dot(dst, q_ref[h, sub, :], preferred_element_type=F32)
                rows = pl.ds(pl.multiple_of(qi * tq + u * tsub, tsub), tsub)
                dq_ref[h, rows, :] += lax.dot_general(dst, k_ref[h], TN, preferred_element_type=F32)

        @pl.when(qi > kj)
        def _():
            step(False)

        @pl.when(qi == kj)
        def _():
            step(True)

        @pl.when(qi == nq - 1)
        def _():
            dk_ref[...] = dk_sc[...] * scale
            dv_ref[...] = dv_sc[...]

        @pl.when(p == npairs - 1)
        def _():
            dq_ref[...] = dq_ref[...] * scale

    return pl.pallas_call(
        body,
        grid_spec=pltpu.PrefetchScalarGridSpec(
            num_scalar_prefetch=2,
            grid=(H // hb, npairs),
            in_specs=[pl.BlockSpec((hb, tq, HW), lambda h, p, qi, kj: (h, qi[p], 0)),
                      pl.BlockSpec((hb, tq, HW), lambda h, p, qi, kj: (h, kj[p], 0)),
                      pl.BlockSpec((hb, tq, DV), lambda h, p, qi, kj: (h, kj[p], 0)),
                      pl.BlockSpec((tq, hb * DV), lambda h, p, qi, kj: (qi[p], h)),
                      pl.BlockSpec((hb, 8, tq), lambda h, p, qi, kj: (h, 0, qi[p])),
                      pl.BlockSpec((hb, 8, tq), lambda h, p, qi, kj: (h, 0, qi[p]))],
            out_specs=[pl.BlockSpec((hb, S, HW), lambda h, p, qi, kj: (h, 0, 0)),
                       pl.BlockSpec((hb, tq, HW), lambda h, p, qi, kj: (h, kj[p], 0)),
                       pl.BlockSpec((hb, tq, DV), lambda h, p, qi, kj: (h, kj[p], 0))],
            scratch_shapes=[pltpu.VMEM((hb, tq, HW), F32), pltpu.VMEM((hb, tq, DV), F32)],
        ),
        out_shape=[jax.ShapeDtypeStruct((H, S, HW), F32), jax.ShapeDtypeStruct((H, S, HW), F32), jax.ShapeDtypeStruct((H, S, DV), F32)],
        compiler_params=_cparams(("parallel", "arbitrary"), VMEM_CAP),
        name="flash_bwd",
    )(qi_tab, kj_tab, q, k, v, do, lse, delta)


def _rms_bwd(xf, g, dn_out):
    r = _rms_scale(xf)
    n = xf * r
    dn = dn_out * g
    dx = r * (dn - n * jnp.mean(dn * n, axis=-1, keepdims=True))
    return dx, jnp.sum(dn_out * n, axis=0, keepdims=True)


def _qkv_bwd(proj, pos, freq, g_qa, g_kva, w_q, w_kv, dq, dk, dv, ts):
    S = proj.shape[0]

    def body(qa_ref, ckv_ref, pos_ref, freq_ref, gq_ref, gk_ref, wq_ref, wkv_ref, dq_ref, dk_ref, dv_ref,
             dqp_ref, dkvp_ref, dqa_ref, dckv_ref, dkr_ref, dgq_ref, dgk_ref):
        i = pl.program_id(0)
        cos, sin, lane = _rope_tables(pos_ref, freq_ref, ts)
        sgn_sin = jnp.where(lane < 32, sin, -sin)
        live = lane < DR
        kr_sum = jnp.zeros((ts, 128), F32)
        for h in range(H):
            dqh = dq_ref[h]
            dqp_ref[:, h * HW:h * HW + DN] = dqh[:, 0:DN].astype(BF16)
            t = dqh[:, DN:HW]
            dqp_ref[:, h * HW + DN:(h + 1) * HW] = jnp.where(live, t * cos + _rope_swap(t, lane) * sgn_sin, 0.0).astype(BF16)
            dkh = dk_ref[h]
            dkvp_ref[:, h * HW:h * HW + DN] = dkh[:, 0:DN].astype(BF16)
            dkvp_ref[:, h * HW + DN:(h + 1) * HW] = dv_ref[h].astype(BF16)
            kr_sum = kr_sum + dkh[:, DN:HW]
        dkr_ref[...] = jnp.where(live, kr_sum * cos + _rope_swap(kr_sum, lane) * sgn_sin, 0.0).astype(BF16)
        dqn = lax.dot_general(dqp_ref[...], wq_ref[...], NT, preferred_element_type=F32)
        dkvn = lax.dot_general(dkvp_ref[...], wkv_ref[...], NT, preferred_element_type=F32)
        dqa, dgq = _rms_bwd(_f32(qa_ref), gq_ref[...], dqn)
        dckv, dgk = _rms_bwd(_f32(ckv_ref), gk_ref[...], dkvn)
        dqa_ref[...] = dqa.astype(BF16)
        dckv_ref[...] = dckv.astype(BF16)

        @pl.when(i == 0)
        def _():
            dgq_ref[...] = jnp.zeros_like(dgq_ref)
            dgk_ref[...] = jnp.zeros_like(dgk_ref)

        dgq_ref[0:1, :] += dgq
        dgk_ref[0:1, :] += dgk

    rowb = lambda w, blk: pl.BlockSpec((ts, w), lambda i: (i, blk))
    full = lambda a: pl.BlockSpec(a.shape, lambda i: (0,) * a.ndim)
    heads = lambda w: pl.BlockSpec((H, ts, w), lambda i: (0, i, 0))
    return pl.pallas_call(
        body, grid=(S // ts,),
        in_specs=[rowb(RQ, OFF_QA // RQ), rowb(RKV, OFF_CKV // RKV), pl.BlockSpec((ts, 1), lambda i: (i, 0)), full(freq), full(g_qa), full(g_kva),
                  full(w_q), full(w_kv), heads(HW), heads(HW), heads(DV)],
        out_specs=[rowb(H * HW, 0), rowb(H * HW, 0), rowb(RQ, 0), rowb(RKV, 0), rowb(128, 0),
                   pl.BlockSpec((8, RQ), lambda i: (0, 0)), pl.BlockSpec((8, RKV), lambda i: (0, 0))],
        out_shape=[jax.ShapeDtypeStruct((S, H * HW), BF16), jax.ShapeDtypeStruct((S, H * HW), BF16), jax.ShapeDtypeStruct((S, RQ), BF16),
                   jax.ShapeDtypeStruct((S, RKV), BF16), jax.ShapeDtypeStruct((S, 128), BF16),
                   jax.ShapeDtypeStruct((8, RQ), F32), jax.ShapeDtypeStruct((8, RKV), F32)],
        compiler_params=_cparams(("arbitrary",), 56 << 20), name="qkv_bwd",
    )(proj, proj, pos, freq, g_qa, g_kva, w_q, w_kv, dq, dk, dv)


def _pack_dproj(parts, ts):
    S = parts[0].shape[0]
    widths = [p.shape[1] for p in parts]
    assert sum(widths) == NP

    def body(*refs):
        o_ref = refs[-1]
        off = 0
        for r, w in zip(refs[:-1], widths):
            o_ref[:, off:off + w] = r[...]
            off += w

    return pl.pallas_call(
        body, grid=(S // ts,),
        in_specs=[pl.BlockSpec((ts, w), lambda i: (i, 0)) for w in widths],
        out_specs=pl.BlockSpec((ts, NP), lambda i: (i, 0)),
        out_shape=jax.ShapeDtypeStruct((S, NP), BF16),
        compiler_params=_cparams(("parallel",), 48 << 20), name="pack_dproj",
    )(*parts)


def _prenorm_bwd(x, g, dh, dy, ts):
    S = x.shape[0]

    def body(x_ref, g_ref, dh_ref, dy_ref, gx_ref, dg_ref):
        dx, dg = _rms_bwd(x_ref[...], g_ref[...], dh_ref[...])
        gx_ref[...] = dy_ref[...] + dx

        @pl.when(pl.program_id(0) == 0)
        def _():
            dg_ref[...] = jnp.zeros_like(dg_ref)

        dg_ref[0:1, :] += dg

    row = pl.BlockSpec((ts, D), lambda i: (i, 0))
    return pl.pallas_call(
        body, grid=(S // ts,), in_specs=[row, pl.BlockSpec((1, D), lambda i: (0, 0)), row, row],
        out_specs=[row, pl.BlockSpec((8, D), lambda i: (0, 0))],
        out_shape=[jax.ShapeDtypeStruct((S, D), F32), jax.ShapeDtypeStruct((8, D), F32)],
        compiler_params=_cparams(("arbitrary",), 40 << 20), name="prenorm_bwd",
    )(x, g, dh, dy)


def _local_step(x, pos, target, g_pre, g_qa, g_kva, g_post, w_in, other_weights, reduce_grads=None):
    S = x.shape[0]
    ts = min(256, S)
    tq = min(512, S)
    tm = min(512, S)
    mm = functools.partial(_matmul, tm=tm)
    freq = _inv_freq_tile()

    h, ht = _prenorm(x, g_pre, ts)
    if len(other_weights) == 2:
        gather, assemble = other_weights
        proj, gathered = _matmul(h, w_in, mode="nn", out_dtype=BF16, tm=1024, tn=1408, tk=D, name="mm_proj", exchange=gather)
        other_weights = assemble(*gathered)
    else:
        proj = _matmul(h, w_in, mode="nn", out_dtype=BF16, tm=1024, tn=1408, tk=D, name="mm_proj")
    w_q, w_kv, conv_w8, w_o_mla, w_o_conv, w_out = other_weights
    q, k, v, vt, qn, kvn = _qkv_prep(proj, pos, freq, g_qa, g_kva, w_q, w_kv, ts)
    attn, lse = _flash_fwd(q, k, vt, tq)
    a_mla, a_conv = _gates_fwd(proj, attn, conv_w8, ts)
    y_mla = mm(a_mla, w_o_mla, mode="nn", out_dtype=BF16, tn=1024, tk=D, name="mm_y_mla")
    y_conv = mm(a_conv, w_o_conv, mode="nn", out_dtype=BF16, tn=1024, tk=D, name="mm_y_conv")
    merged = _merge_fwd(proj, y_mla, y_conv, ts)
    out = mm(merged, w_out, mode="nn", out_dtype=F32, tn=1024, tk=D, name="mm_out")
    dy, dout, dg_post, loss = _post_loss(out, x, target, g_post, ts)

    dmerged = mm(dout, w_out, mode="nt", out_dtype=BF16, tn=1024, tk=D, name="mm_dmerged")
    dw_out = _matmul(merged, dout, mode="tn", out_dtype=F32, tm=1024, tn=1024, tk=1024, name="mm_dw_out")
    dy_mla, dy_conv, dg_mla, dg_conv = _merge_bwd(proj, y_mla, y_conv, dmerged, ts)
    da_mla = mm(dy_mla, w_o_mla, mode="nt", out_dtype=BF16, tn=1024, tk=D, name="mm_da_mla")
    da_conv = mm(dy_conv, w_o_conv, mode="nt", out_dtype=BF16, tn=1024, tk=D, name="mm_da_conv")
    dw_o_mla = _matmul(a_mla, dy_mla, mode="tn", out_dtype=F32, tm=1024, tn=1024, tk=1024, name="mm_dw_o_mla")
    dw_o_conv = _matmul(a_conv, dy_conv, mode="tn", out_dtype=F32, tm=1024, tn=1024, tk=1024, name="mm_dw_o_conv")
    dattn, delta, dz_mla, dc_in, db_gate, dc_gate, dz_conv, dconv_w = _gates_bwd(proj, attn, da_mla, da_conv, conv_w8, min(128, S))
    dq, dk, dv = _flash_bwd(q, k, v, dattn, lse, delta, tq)
    dqp, dkvp, dq_a, dc_kv, dk_rope, dg_qa, dg_kva = _qkv_bwd(proj, pos, freq, g_qa, g_kva, w_q, w_kv, dq, dk, dv, min(128, S))
    dw_q = _matmul(qn, dqp, mode="tn", out_dtype=F32, tm=RQ, tn=1024, tk=1024, name="mm_dw_q")
    dw_kv = _matmul(kvn, dkvp, mode="tn", out_dtype=F32, tm=RKV, tn=1024, tk=1024, name="mm_dw_kv")
    dproj = _pack_dproj([dz_mla, dc_in, db_gate, dc_gate, dz_conv, dg_mla, dg_conv, dq_a, dc_kv, dk_rope], ts)
    dw_in = _matmul(ht, dproj, mode="nn", out_dtype=F32, tm=1024, tn=1408, tk=1024, name="mm_dw_in")
    res = dict(dw_in=dw_in, dw_q=dw_q, dw_kv=dw_kv, dconv_w=dconv_w, dw_o_mla=dw_o_mla, dw_o_conv=dw_o_conv, dw_out=dw_out)
    if reduce_grads is None:
        dh = _matmul(dproj, w_in, mode="nt", out_dtype=F32, tm=1024, tn=D, tk=1408, name="mm_dh")
    else:
        exchange, finish = reduce_grads(res)
        dh, got = _matmul(dproj, w_in, mode="nt", out_dtype=F32, tm=1024, tn=D, tk=1408, name="mm_dh", exchange=exchange)
        res["reduced"] = finish(got)
    grad_x, dg_pre = _prenorm_bwd(x, g_pre, dh, dy, ts)
    res.update(loss=loss, grad_x=grad_x, dg_pre=dg_pre, dg_qa=dg_qa, dg_kva=dg_kva, dg_post=dg_post)
    return res


def _my_id():
    return lax.axis_index("x") * 4 + lax.axis_index("y") * 2 + lax.axis_index("c")


def _place():
    x, y, c = lax.axis_index("x"), lax.axis_index("y"), lax.axis_index("c")
    return (x, y, c), (x, y, 1 - c), [(1 - x, y), (x, 1 - y), (1 - x, 1 - y)]


def _slot(px, py, pc):
    return 4 * px + 2 * py + pc


def _all_gather(arrays, name):
    n = len(arrays)

    def body(*refs):
        ins, outs = refs[:n], refs[n:2 * n]
        send_sems, recv_sems, local_sems = refs[2 * n:]
        me, sib, chips = _place()
        c = me[2]

        def copy(a, k, block, to, src=None):
            rows = outs[a].at[_slot(*block)]
            return pltpu.make_async_remote_copy(src_ref=rows if src is None else src, dst_ref=rows, send_sem=send_sems.at[a, k],
                                                recv_sem=recv_sems.at[a, k], device_id=to, device_id_type=MESH)

        local = [pltpu.make_async_copy(ins[a], outs[a].at[_slot(*me)], local_sems.at[a]) for a in range(n)]
        for cp in local:
            cp.start()
        sends = []
        for a in range(n):
            sends.append(copy(a, 0, me, sib, src=ins[a]))
            sends += [copy(a, 1 + j, me, (*chip, c), src=ins[a]) for j, chip in enumerate(chips)]
        for cp in sends:
            cp.start()
        for a in range(n):
            for j, chip in enumerate(chips):
                copy(a, 1 + j, (*chip, c), me).wait_recv()
                fwd = copy(a, 4 + j, (*chip, c), sib)
                fwd.start()
                sends.append(fwd)
        for a in range(n):
            copy(a, 0, sib, me).wait_recv()
            for j, chip in enumerate(chips):
                copy(a, 4 + j, (*chip, 1 - c), me).wait_recv()
        for cp in sends:
            cp.wait_send()
        for cp in local:
            cp.wait()

    anyspec = pl.BlockSpec(memory_space=pl.ANY)
    return pl.pallas_call(
        body,
        in_specs=[anyspec] * n,
        out_specs=[anyspec] * n,
        out_shape=[jax.ShapeDtypeStruct((NDEV,) + a.shape, a.dtype) for a in arrays],
        scratch_shapes=[pltpu.SemaphoreType.DMA((n, NDEV - 1)), pltpu.SemaphoreType.DMA((n, NDEV - 1)), pltpu.SemaphoreType.DMA((n,))],
        name=name,
    )(*arrays)


def _slab_exchange(arrays, nslots, pick):
    n = len(arrays)

    def copies(ins, outs, sems):
        send_sems, recv_sems = sems
        return [pltpu.make_async_remote_copy(src_ref=ins[a].at[pick(k)[0]], dst_ref=outs[a].at[k], send_sem=send_sems.at[a, k],
                                             recv_sem=recv_sems.at[a, k], device_id=pick(k)[1], device_id_type=MESH)
                for a in range(n) for k in range(nslots)]

    def start(ins, outs, sems):
        for cp in copies(ins, outs, sems):
            cp.start()

    def finish(ins, outs, sems):
        cps = copies(ins, outs, sems)
        for cp in cps:
            cp.wait_recv()
        for cp in cps:
            cp.wait_send()

    return _Exchange(arrays, [jax.ShapeDtypeStruct((nslots,) + a.shape[1:], a.dtype) for a in arrays],
                     [pltpu.SemaphoreType.DMA((n, nslots)), pltpu.SemaphoreType.DMA((n, nslots))], start, finish)


def _direct_gather(arrays):
    n = len(arrays)

    def peer(d):
        p = (_my_id() + d) % NDEV
        return (p // 4, (p // 2) % 2, p % 2), p

    def copies(ins, outs, sems, receiving):
        send_sems, recv_sems, _ = sems
        slot = lambda d: peer(NDEV - d)[1] if receiving else _my_id()
        return [pltpu.make_async_remote_copy(src_ref=ins[a], dst_ref=outs[a].at[slot(d)], send_sem=send_sems.at[a, d - 1],
                                             recv_sem=recv_sems.at[a, d - 1], device_id=peer(d)[0], device_id_type=MESH)
                for d in range(1, NDEV) for a in range(n)]

    def local(ins, outs, sems):
        return [pltpu.make_async_copy(ins[a], outs[a].at[_my_id()], sems[2].at[a]) for a in range(n)]

    def start(ins, outs, sems):
        for cp in local(ins, outs, sems) + copies(ins, outs, sems, False):
            cp.start()

    def finish(ins, outs, sems):
        for cp in copies(ins, outs, sems, True):
            cp.wait_recv()
        for cp in copies(ins, outs, sems, False):
            cp.wait_send()
        for cp in local(ins, outs, sems):
            cp.wait()

    return _Exchange(arrays, [jax.ShapeDtypeStruct((NDEV,) + a.shape, a.dtype) for a in arrays],
                     [pltpu.SemaphoreType.DMA((n, NDEV - 1)), pltpu.SemaphoreType.DMA((n, NDEV - 1)), pltpu.SemaphoreType.DMA((n,))],
                     start, finish)


def _run_exchange(exchange, name):
    n = len(exchange.arrays)

    def body(*refs):
        ins, outs, sems = refs[:n], refs[n:2 * n], refs[2 * n:]
        exchange.start(ins, outs, sems)
        exchange.finish(ins, outs, sems)

    anyspec = pl.BlockSpec(memory_space=pl.ANY)
    return pl.pallas_call(body, in_specs=[anyspec] * n, out_specs=[anyspec] * n, out_shape=exchange.out_shapes,
                          scratch_shapes=exchange.sem_shapes, name=name)(*exchange.arrays)


def _slabs_bf16(x, rows_per_block, name):
    n, R, C = x.shape
    tr = min(rows_per_block, R)

    def body(x_ref, o_ref):
        o_ref[...] = x_ref[...].astype(BF16)

    blk = pl.BlockSpec((1, tr, C), lambda k, i: (k, i, 0))
    return pl.pallas_call(body, grid=(n, R // tr), in_specs=[blk], out_specs=blk, out_shape=jax.ShapeDtypeStruct(x.shape, BF16),
                          compiler_params=_cparams(("parallel", "parallel"), 32 << 20), name=name)(x)


def _to_sibling(k):
    me, sib, chips = _place()
    dest = sib if k == 0 else (*chips[k - 1], sib[2])
    return _slot(*dest), sib


def _to_chips(k):
    me, sib, chips = _place()
    return k, (*chips[k], me[2])


def _chip_sums(own, got, slots, rows_per_block, name):
    _, R, C = own.shape
    tr = min(rows_per_block, R)

    def body(slots_ref, own_ref, got_ref, o_ref):
        o_ref[0] = (own_ref[0] + got_ref[0].astype(F32)).astype(BF16)

    return pl.pallas_call(
        body,
        grid_spec=pltpu.PrefetchScalarGridSpec(
            num_scalar_prefetch=1, grid=(3, R // tr),
            in_specs=[pl.BlockSpec((1, tr, C), lambda j, i, s: (s[1 + j], i, 0)), pl.BlockSpec((1, tr, C), lambda j, i, s: (1 + j, i, 0))],
            out_specs=pl.BlockSpec((1, tr, C), lambda j, i, s: (j, i, 0))),
        out_shape=jax.ShapeDtypeStruct((3, R, C), BF16),
        compiler_params=_cparams(("parallel", "parallel"), 32 << 20), name=name,
    )(slots, own, got)


def _adamw_math(w, g, m, v):
    m = ADAM_B1 * m + (1.0 - ADAM_B1) * g
    v = ADAM_B2 * v + (1.0 - ADAM_B2) * (g * g)
    m_hat = m / (1.0 - ADAM_B1 ** ADAM_STEP)
    v_hat = v / (1.0 - ADAM_B2 ** ADAM_STEP)
    delta = -ADAM_LR * (m_hat / (jnp.sqrt(v_hat) + ADAM_EPS) + ADAM_WD * w)
    return delta, m, v


def _reduce_adamw(own, got1, got2, slots, w, m, v, rows_per_block, name):
    _, R, C = own.shape
    tr = min(rows_per_block, R)
    assert R % tr == 0

    def body(slots_ref, own_ref, g1_ref, g2_ref, w_ref, m_ref, v_ref, g_ref, d_ref, nm_ref, nv_ref):
        g = own_ref[0] + g1_ref[0].astype(F32)
        for j in range(3):
            g = g + g2_ref[j].astype(F32)
        g_ref[...] = g
        d, nm, nv = _adamw_math(w_ref[...], g, m_ref[...], v_ref[...])
        d_ref[...] = d
        nm_ref[...] = nm
        nv_ref[...] = nv

    blk = pl.BlockSpec((tr, C), lambda i, s: (i, 0))
    return pl.pallas_call(
        body,
        grid_spec=pltpu.PrefetchScalarGridSpec(
            num_scalar_prefetch=1, grid=(R // tr,),
            in_specs=[pl.BlockSpec((1, tr, C), lambda i, s: (s[0], i, 0)), pl.BlockSpec((1, tr, C), lambda i, s: (0, i, 0)),
                      pl.BlockSpec((3, tr, C), lambda i, s: (0, i, 0)), blk, blk, blk],
            out_specs=[blk] * 4),
        out_shape=[jax.ShapeDtypeStruct((R, C), F32)] * 4,
        compiler_params=_cparams(("parallel",), 48 << 20), name=name,
    )(slots, own, got1, got2, w, m, v)


def _sum_adamw(parts, w, m, v, rows_per_block, name):
    n, R, C = parts.shape
    tr = min(rows_per_block, R)
    assert R % tr == 0

    def body(p_ref, w_ref, m_ref, v_ref, g_ref, d_ref, nm_ref, nv_ref):
        g = p_ref[0]
        for j in range(1, n):
            g = g + p_ref[j]
        g_ref[...] = g
        d, nm, nv = _adamw_math(w_ref[...], g, m_ref[...], v_ref[...])
        d_ref[...] = d
        nm_ref[...] = nm
        nv_ref[...] = nv

    blk = pl.BlockSpec((tr, C), lambda i: (i, 0))
    return pl.pallas_call(
        body, grid=(R // tr,),
        in_specs=[pl.BlockSpec((n, tr, C), lambda i: (0, i, 0)), blk, blk, blk],
        out_specs=[blk] * 4,
        out_shape=[jax.ShapeDtypeStruct((R, C), F32)] * 4,
        compiler_params=_cparams(("parallel",), 48 << 20), name=name,
    )(parts, w, m, v)


def kernel(x, positions, pre_norm_g, w_in, q_a_norm_g, w_q_b, kv_a_norm_g, w_kv_b, conv_w, w_o_mla, w_o_conv, w_out, post_norm_g, loss_target, m_pre_norm_g, m_w_in, m_q_a_norm_g, m_w_q_b, m_kv_a_norm_g, m_w_kv_b, m_conv_w, m_w_o_mla, m_w_o_conv, m_w_out, m_post_norm_g, v_pre_norm_g, v_w_in, v_q_a_norm_g, v_w_q_b, v_kv_a_norm_g, v_w_kv_b, v_conv_w, v_w_o_mla, v_w_o_conv, v_w_out, v_post_norm_g):
    S = x.shape[1]
    conv_pad = jnp.zeros((8, 256), F32).at[0:3, :].set(conv_w)
    g_in, = _all_gather([w_in.astype(BF16)], "all_gather_w_in")
    w_in_f = _assemble_w_in(g_in)
    gather_rest = _direct_gather([w_q_b.astype(BF16), w_kv_b.astype(BF16), conv_pad, w_o_mla.astype(BF16), w_o_conv.astype(BF16),
                                  w_out.astype(BF16)])

    def assemble_rest(g_q, g_kv, g_cw, g_om, g_oc, g_out):
        return (_assemble_w_q(g_q), _concat_cols(g_kv, BF16, "assemble_w_kv"), _concat_cols(g_cw, F32, "assemble_conv_w"),
                g_om.reshape(D, D), g_oc.reshape(D, D), g_out.reshape(D, D))

    gnames = ["w_in", "w_q", "w_kv", "conv_w", "w_o_mla", "w_o_conv", "w_out"]
    (mx, my, mc), _, chips = _place()
    slots = jnp.stack([_slot(mx, my, mc)] + [_slot(cx, cy, mc) for cx, cy in chips]).astype(jnp.int32)

    def reduce_grads(r):
        own_in, own_in_bf16 = _split_dw_in(r["dw_in"])
        own = [own_in, _split_dw_q(r["dw_q"]), _split_cols(r["dw_kv"], NDEV, "split_dw_kv"),
               _split_cols(r["dconv_w"], NDEV, "split_dconv_w"), r["dw_o_mla"].reshape(NDEV, D // NDEV, D),
               r["dw_o_conv"].reshape(NDEV, D // NDEV, D), r["dw_out"].reshape(NDEV, D // NDEV, D)]
        slabs = [own_in_bf16] + [_slabs_bf16(g, 128, "bf16_" + nm) for g, nm in zip(own[1:], gnames[1:])]
        to_sibling = _slab_exchange(slabs, 4, _to_sibling)
        got1 = _run_exchange(to_sibling, "grads_to_sibling")
        sums = [_chip_sums(o, g1, slots, 128, "chip_sum_" + nm) for o, g1, nm in zip(own, got1, gnames)]
        return _slab_exchange(sums, 3, _to_chips), lambda got2: (own, got1, got2)

    row2 = lambda a: a.reshape(1, -1)
    r = _local_step(x[0], positions.reshape(S, 1), loss_target[0], row2(pre_norm_g), row2(q_a_norm_g), row2(kv_a_norm_g),
                    row2(post_norm_g), w_in_f, (gather_rest, assemble_rest), reduce_grads)
    own, got1, got2 = r["reduced"]
    small = jnp.concatenate([r["dg_pre"][0:1], r["dg_post"][0:1], jnp.pad(r["dg_qa"][0:1], ((0, 0), (0, D - RQ))),
                             jnp.pad(r["dg_kva"][0:1], ((0, 0), (0, D - RKV))), jnp.pad(r["loss"][0:1], ((0, 0), (0, D - 128))),
                             jnp.zeros((3, D), F32)], axis=0)
    p_small, = _all_gather([small], "all_gather_small")
    pad8 = lambda a: jnp.zeros((8, 256), F32).at[0:3, :].set(a)
    params = [(w_in, m_w_in, v_w_in), (w_q_b, m_w_q_b, v_w_q_b), (w_kv_b, m_w_kv_b, v_w_kv_b), (conv_pad, pad8(m_conv_w), pad8(v_conv_w)),
              (w_o_mla, m_w_o_mla, v_w_o_mla), (w_o_conv, m_w_o_conv, v_w_o_conv), (w_out, m_w_out, v_w_out)]
    o_in, o_q, o_kv, o_cw, o_om, o_oc, o_out = [
        _reduce_adamw(o, g1, g2, slots, w, m, v, 128, "adamw_" + nm)
        for o, g1, g2, (w, m, v), nm in zip(own, got1, got2, params, gnames)]
    o_cw = [a[0:3] for a in o_cw]
    padv = lambda a: jnp.pad(a.reshape(1, -1), ((0, 0), (0, D - a.shape[0])))
    stack = lambda pre, post, qa, kva: jnp.concatenate([row2(pre), row2(post), padv(qa), padv(kva), jnp.zeros((4, D), F32)], axis=0)
    o_g = _sum_adamw(p_small, stack(pre_norm_g, post_norm_g, q_a_norm_g, kv_a_norm_g),
                     stack(m_pre_norm_g, m_post_norm_g, m_q_a_norm_g, m_kv_a_norm_g),
                     stack(v_pre_norm_g, v_post_norm_g, v_q_a_norm_g, v_kv_a_norm_g), 8, "adamw_gains")
    loss = o_g[0][4, 0]
    outs = {}
    for idx, kind in enumerate(("grad", "delta", "new_m", "new_v")):
        o = o_g[idx]
        outs[kind] = dict(pre_norm_g=o[0], w_in=o_in[idx], q_a_norm_g=o[2, 0:RQ], w_q_b=o_q[idx], kv_a_norm_g=o[3, 0:RKV],
                          w_kv_b=o_kv[idx], conv_w=o_cw[idx], w_o_mla=o_om[idx], w_o_conv=o_oc[idx], w_out=o_out[idx], post_norm_g=o[1])
    names = ["pre_norm_g", "w_in", "q_a_norm_g", "w_q_b", "kv_a_norm_g", "w_kv_b", "conv_w", "w_o_mla", "w_o_conv", "w_out", "post_norm_g"]
    return (loss, r["grad_x"][None], *[outs["grad"][n] for n in names], *[outs["delta"][n] for n in names],
            *[outs["new_m"][n] for n in names], *[outs["new_v"][n] for n in names])
```

```python
import functools
import math

import jax
import jax.numpy as jnp
from jax import lax
from jax.experimental import pallas as pl
from jax.experimental.pallas import tpu as pltpu

F32 = jnp.float32
BF16 = jnp.bfloat16

NDEV = 8
D = 2048
H = 16
DN = 128
DR = 64
DV = 128
RQ = 512
RKV = 512
HW = 256
ROPE_THETA = 10000.0
RMS_EPS = 1e-6
N_IN = 15424
SHARD_IN = N_IN // NDEV
SMALL = RQ + RKV + DR
NP = 7 * D + RQ + RKV + 128
SEG = dict(z_mla=0, c_in=1, b_gate=2, c_gate=3, z_conv=4, g_mla=5, g_conv=6)
OFF_QA = 7 * D
OFF_CKV = OFF_QA + RQ
OFF_KR = OFF_CKV + RKV
EXT = 2176
VMEM_CAP = 56 * 1024 * 1024

ADAM_LR = 0.001
ADAM_B1 = 0.9
ADAM_B2 = 0.999
ADAM_EPS = 1e-08
ADAM_WD = 0.01
ADAM_STEP = 10

LOG2E = math.log2(math.e)
NN = (((1,), (0,)), ((), ()))
NT = (((1,), (1,)), ((), ()))
TN = (((0,), (0,)), ((), ()))
MESH = pl.DeviceIdType.MESH


def _cparams(sem, vmem_bytes):
    return pltpu.CompilerParams(dimension_semantics=sem, vmem_limit_bytes=int(min(VMEM_CAP, max(vmem_bytes, 16 << 20))))


def _nbytes(shape, dtype):
    return math.prod(shape) * jnp.dtype(dtype).itemsize


class _Exchange:
    def __init__(self, arrays, out_shapes, sem_shapes, start, finish):
        self.arrays, self.out_shapes, self.sem_shapes, self.start, self.finish = arrays, out_shapes, sem_shapes, start, finish
        self.n_sems = len(sem_shapes)


def _matmul(a, b, *, mode, out_dtype, tm, tn, tk, name, m_outer=False, exchange=None):
    if mode == "nn":
        (M, K), (K2, N) = a.shape, b.shape
    elif mode == "nt":
        (M, K), (N, K2) = a.shape, b.shape
    else:
        (K, M), (K2, N) = a.shape, b.shape
    assert K == K2, (a.shape, b.shape, mode)
    tm, tn, tk = min(tm, M), min(tn, N), min(tk, K)
    assert M % tm == 0 and N % tn == 0 and K % tk == 0, (M, N, K, tm, tn, tk)
    ni, nj, nk = M // tm, N // tn, K // tk
    dims = dict(nn=NN, nt=NT, tn=TN)[mode]

    if m_outer:
        grid = (ni, nj, nk)
        ij = lambda g0, g1: (g0, g1)
    else:
        grid = (nj, ni, nk)
        ij = lambda g0, g1: (g1, g0)

    if mode == "tn":
        a_spec = pl.BlockSpec((tk, tm), lambda g0, g1, k: (k, ij(g0, g1)[0]))
        a_tile = (tk, tm)
    else:
        a_spec = pl.BlockSpec((tm, tk), lambda g0, g1, k: (ij(g0, g1)[0], k))
        a_tile = (tm, tk)
    if mode == "nt":
        b_spec = pl.BlockSpec((tn, tk), lambda g0, g1, k: (ij(g0, g1)[1], k))
    else:
        b_spec = pl.BlockSpec((tk, tn), lambda g0, g1, k: (k, ij(g0, g1)[1]))
    o_spec = pl.BlockSpec((tm, tn), lambda g0, g1, k: ij(g0, g1))

    n_in = len(exchange.arrays) if exchange else 0

    def body(a_ref, b_ref, *refs):
        x_in, o_ref, x_out, scratch = refs[:n_in], refs[n_in], refs[n_in + 1:2 * n_in + 1], refs[2 * n_in + 1:]
        if exchange:
            sems = scratch[len(scratch) - exchange.n_sems:]
            step = (pl.program_id(0) * grid[1] + pl.program_id(1)) * grid[2] + pl.program_id(2)

            @pl.when(step == 0)
            def _():
                exchange.start(x_in, x_out, sems)

        prod = lax.dot_general(a_ref[...], b_ref[...], dims, preferred_element_type=F32)
        if nk == 1:
            o_ref[...] = prod.astype(o_ref.dtype)
        else:
            acc_ref = scratch[0]
            k = pl.program_id(2)

            @pl.when(k == 0)
            def _():
                acc_ref[...] = prod

            @pl.when((k > 0) & (k < nk - 1))
            def _():
                acc_ref[...] += prod

            @pl.when(k == nk - 1)
            def _():
                o_ref[...] = (acc_ref[...] + prod).astype(o_ref.dtype)

        if exchange:
            @pl.when(step == grid[0] * grid[1] * grid[2] - 1)
            def _():
                exchange.finish(x_in, x_out, sems)

    vmem = 2 * (_nbytes(a_tile, a.dtype) + _nbytes((tk, tn), b.dtype) + _nbytes((tm, tn), out_dtype)) + 2 * _nbytes((tm, tn), F32)
    anyspec = pl.BlockSpec(memory_space=pl.ANY)
    outs = pl.pallas_call(
        body,
        grid=grid,
        in_specs=[a_spec, b_spec] + [anyspec] * n_in,
        out_specs=[o_spec] + [anyspec] * n_in,
        out_shape=[jax.ShapeDtypeStruct((M, N), out_dtype)] + (exchange.out_shapes if exchange else []),
        scratch_shapes=([] if nk == 1 else [pltpu.VMEM((tm, tn), F32)]) + (exchange.sem_shapes if exchange else []),
        compiler_params=_cparams(("arbitrary",) * 3 if exchange else ("parallel", "parallel", "arbitrary"), vmem + (8 << 20)),
        name=name,
    )(a, b, *(exchange.arrays if exchange else []))
    return (outs[0], outs[1:]) if exchange else outs[0]


def _in_dest(k):
    return SHARD_IN * k - SMALL


def _assemble_w_in(g):
    R = 128

    def body(g_ref, o_ref, ext, acc):
        ext[...] = jnp.zeros_like(ext)
        acc[...] = jnp.zeros_like(acc)
        lane = lax.broadcasted_iota(jnp.int32, (R, EXT), 1)
        ext[:, 0:SHARD_IN] = g_ref[0].astype(F32)
        v = ext[...]
        acc[:, OFF_QA:NP] = jnp.where(lane[:, 0:NP - OFF_QA] < SMALL, v[:, 0:NP - OFF_QA], 0.0)
        w = v[:, 1024:2048]
        w = pltpu.roll(w, 1024 - 64, 1)
        acc[:, 0:1024] = jnp.where(lane[:, 0:1024] < SHARD_IN - SMALL, w, 0.0)
        for k in range(1, NDEV):
            ext[:, 0:SHARD_IN] = g_ref[k].astype(F32)
            dest = _in_dest(k)
            t, o = dest // 128, dest % 128
            width = -(-(o + SHARD_IN) // 128) * 128
            v = pltpu.roll(ext[...], o, 1)[:, 0:width]
            acc[:, 128 * t:128 * t + width] += v
        o_ref[...] = acc[...].astype(BF16)

    return pl.pallas_call(
        body,
        grid=(D // R,),
        in_specs=[pl.BlockSpec((NDEV, R, SHARD_IN), lambda i: (0, i, 0))],
        out_specs=pl.BlockSpec((R, NP), lambda i: (i, 0)),
        out_shape=jax.ShapeDtypeStruct((D, NP), BF16),
        scratch_shapes=[pltpu.VMEM((R, EXT), F32), pltpu.VMEM((R, NP), F32)],
        compiler_params=_cparams(("parallel",), 40 << 20),
        name="assemble_w_in",
    )(g)


def _split_dw_in(dw):
    R = 128

    def body(dw_ref, o_ref, ob_ref):
        def put(k, cols, v):
            o_ref[k, :, cols] = v
            ob_ref[k, :, cols] = v.astype(BF16)

        lane = lax.broadcasted_iota(jnp.int32, (R, 1024), 1)
        put(0, slice(0, 1024), dw_ref[:, OFF_QA:OFF_QA + 1024])
        tail = dw_ref[:, OFF_QA + 1024:NP]
        tail = jnp.concatenate([tail, jnp.zeros((R, 1024 - 128), F32)], axis=1)
        head = dw_ref[:, 0:1024]
        mixed = jnp.where(lane < 64, tail, pltpu.roll(head, 64, 1))
        put(0, slice(1024, SHARD_IN), mixed[:, 0:SHARD_IN - 1024])
        for k in range(1, NDEV):
            dest = _in_dest(k)
            t, o = dest // 128, dest % 128
            width = -(-(o + SHARD_IN) // 128) * 128
            v = dw_ref[:, 128 * t:128 * t + width]
            v = pltpu.roll(v, width - o, 1)
            put(k, slice(0, SHARD_IN), v[:, 0:SHARD_IN])

    blk = pl.BlockSpec((NDEV, R, SHARD_IN), lambda i: (0, i, 0))
    return pl.pallas_call(
        body,
        grid=(D // R,),
        in_specs=[pl.BlockSpec((R, NP), lambda i: (i, 0))],
        out_specs=[blk, blk],
        out_shape=[jax.ShapeDtypeStruct((NDEV, D, SHARD_IN), F32), jax.ShapeDtypeStruct((NDEV, D, SHARD_IN), BF16)],
        compiler_params=_cparams(("parallel",), 48 << 20),
        name="split_dw_in",
    )(dw)


def _assemble_w_q(g):
    def body(g_ref, o_ref):
        lane = lax.broadcasted_iota(jnp.int32, (RQ, 128), 1)
        lo = lane < 64
        for k in range(NDEV):
            t0 = g_ref[k, :, 0:128].astype(F32)
            t1 = g_ref[k, :, 128:256].astype(F32)
            t2 = g_ref[k, :, 256:384].astype(F32)
            base = 2 * k * HW
            o_ref[:, base:base + 128] = t0.astype(BF16)
            o_ref[:, base + 128:base + 256] = jnp.where(lo, t1, 0.0).astype(BF16)
            o_ref[:, base + 256:base + 384] = pltpu.roll(jnp.where(lo, t2, t1), 64, 1).astype(BF16)
            o_ref[:, base + 384:base + 512] = jnp.where(lo, pltpu.roll(t2, 64, 1), 0.0).astype(BF16)

    return pl.pallas_call(
        body,
        out_shape=jax.ShapeDtypeStruct((RQ, H * HW), BF16),
        compiler_params=_cparams(None, 32 << 20),
        name="assemble_w_q",
    )(g)


def _split_dw_q(dw):
    def body(dw_ref, o_ref):
        lane = lax.broadcasted_iota(jnp.int32, (RQ, 128), 1)
        lo = lane < 64
        for k in range(NDEV):
            base = 2 * k * HW
            a = dw_ref[:, base:base + 128]
            b = dw_ref[:, base + 128:base + 256]
            c = pltpu.roll(dw_ref[:, base + 256:base + 384], 64, 1)
            d = pltpu.roll(dw_ref[:, base + 384:base + 512], 64, 1)
            o_ref[k, :, 0:128] = a
            o_ref[k, :, 128:256] = jnp.where(lo, b, c)
            o_ref[k, :, 256:384] = jnp.where(lo, c, d)

    return pl.pallas_call(
        body,
        out_shape=jax.ShapeDtypeStruct((NDEV, RQ, 384), F32),
        compiler_params=_cparams(None, 32 << 20),
        name="split_dw_q",
    )(dw)


def _concat_cols(g, dtype, name):
    n, R, C = g.shape

    def body(g_ref, o_ref):
        for k in range(n):
            o_ref[:, k * C:(k + 1) * C] = g_ref[k].astype(dtype)

    return pl.pallas_call(body, out_shape=jax.ShapeDtypeStruct((R, n * C), dtype), compiler_params=_cparams(None, 32 << 20), name=name)(g)


def _split_cols(x, n, name):
    R, NC = x.shape
    C = NC // n

    def body(x_ref, o_ref):
        for k in range(n):
            o_ref[k] = x_ref[:, k * C:(k + 1) * C]

    return pl.pallas_call(body, out_shape=jax.ShapeDtypeStruct((n, R, C), x.dtype), compiler_params=_cparams(None, 32 << 20), name=name)(x)


def _rms_scale(xf):
    return lax.rsqrt(jnp.mean(xf * xf, axis=-1, keepdims=True) + RMS_EPS)


def _prenorm(x, g, ts):
    S = x.shape[0]

    def body(x_ref, g_ref, h_ref, ht_ref):
        xf = x_ref[...]
        h = xf * _rms_scale(xf) * g_ref[...]
        h_ref[...] = h.astype(BF16)
        ht_ref[...] = jnp.transpose(h).astype(BF16)

    return pl.pallas_call(
        body,
        grid=(S // ts,),
        in_specs=[pl.BlockSpec((ts, D), lambda i: (i, 0)), pl.BlockSpec((1, D), lambda i: (0, 0))],
        out_specs=[pl.BlockSpec((ts, D), lambda i: (i, 0)), pl.BlockSpec((D, ts), lambda i: (0, i))],
        out_shape=[jax.ShapeDtypeStruct((S, D), BF16), jax.ShapeDtypeStruct((D, S), BF16)],
        compiler_params=_cparams(("parallel",), 32 << 20),
        name="prenorm",
    )(x, g)


def _inv_freq_tile():
    inv_freq = ROPE_THETA ** (-jnp.arange(0, DR, 2, dtype=F32) / DR)
    return jnp.tile(inv_freq, 4).reshape(1, 128)


def _rope_tables(pos_ref, freq_ref, ts):
    lane = lax.broadcasted_iota(jnp.int32, (ts, 128), 1)
    ang = pos_ref[...].astype(F32) * freq_ref[...]
    return jnp.cos(ang), jnp.sin(ang), lane


def _rope_swap(t, lane):
    return jnp.where(lane < 32, pltpu.roll(t, 96, 1), pltpu.roll(t, 32, 1))


def _qkv_prep(proj, pos, freq, g_qa, g_kva, w_q, w_kv, ts):
    S = proj.shape[0]

    def body(qa_ref, ckv_ref, kr_ref, pos_ref, freq_ref, gq_ref, gk_ref, wq_ref, wkv_ref, q_ref, k_ref, v_ref, vt_ref, qn_ref, kvn_ref):
        qa = _f32(qa_ref)
        qn = (qa * _rms_scale(qa) * gq_ref[...]).astype(BF16)
        ckv = _f32(ckv_ref)
        kvn = (ckv * _rms_scale(ckv) * gk_ref[...]).astype(BF16)
        qn_ref[...] = qn
        kvn_ref[...] = kvn
        cos, sin, lane = _rope_tables(pos_ref, freq_ref, ts)
        sgn_sin = jnp.where(lane < 32, -sin, sin)
        live = lane < DR
        kr = _f32(kr_ref)
        kr = jnp.where(live, kr * cos + _rope_swap(kr, lane) * sgn_sin, 0.0).astype(BF16)
        qf = jnp.dot(qn, wq_ref[...], preferred_element_type=F32)
        kvf = jnp.dot(kvn, wkv_ref[...], preferred_element_type=F32)
        for h in range(H):
            q_ref[h, :, 0:DN] = qf[:, h * HW:h * HW + DN].astype(BF16)
            t = qf[:, h * HW + DN:(h + 1) * HW]
            q_ref[h, :, DN:HW] = jnp.where(live, t * cos + _rope_swap(t, lane) * sgn_sin, 0.0).astype(BF16)
            k_ref[h, :, 0:DN] = kvf[:, h * HW:h * HW + DN].astype(BF16)
            k_ref[h, :, DN:HW] = kr
            vh = kvf[:, h * HW + DN:(h + 1) * HW]
            v_ref[h] = vh.astype(BF16)
            vt_ref[h] = jnp.transpose(vh).astype(BF16)

    row = lambda w, blk: pl.BlockSpec((ts, w), lambda i: (i, blk))
    full = lambda a: pl.BlockSpec(a.shape, lambda i: (0,) * a.ndim)
    return pl.pallas_call(
        body,
        grid=(S // ts,),
        in_specs=[row(RQ, OFF_QA // RQ), row(RKV, OFF_CKV // RKV), row(128, OFF_KR // 128),
                  pl.BlockSpec((ts, 1), lambda i: (i, 0)), full(freq), full(g_qa), full(g_kva), full(w_q), full(w_kv)],
        out_specs=[pl.BlockSpec((H, ts, HW), lambda i: (0, i, 0)), pl.BlockSpec((H, ts, HW), lambda i: (0, i, 0)),
                   pl.BlockSpec((H, ts, DV), lambda i: (0, i, 0)), pl.BlockSpec((H, DV, ts), lambda i: (0, 0, i)),
                   pl.BlockSpec((ts, RQ), lambda i: (i, 0)), pl.BlockSpec((ts, RKV), lambda i: (i, 0))],
        out_shape=[jax.ShapeDtypeStruct((H, S, HW), BF16), jax.ShapeDtypeStruct((H, S, HW), BF16),
                   jax.ShapeDtypeStruct((H, S, DV), BF16), jax.ShapeDtypeStruct((H, DV, S), BF16),
                   jax.ShapeDtypeStruct((S, RQ), BF16), jax.ShapeDtypeStruct((S, RKV), BF16)],
        compiler_params=_cparams(("parallel",), 48 << 20),
        name="qkv_prep",
    )(proj, proj, proj, pos, freq, g_qa, g_kva, w_q, w_kv)


def _col_to_row8(col, n):
    return jnp.transpose(jnp.broadcast_to(col, (n, 128)))[0:8, :]


def _causal_pairs(n, key_major):
    if key_major:
        pairs = [(i, j) for j in range(n) for i in range(j, n)]
    else:
        pairs = [(i, j) for i in range(n) for j in range(i + 1)]
    return jnp.array([p[0] for p in pairs], jnp.int32), jnp.array([p[1] for p in pairs], jnp.int32)


def _flash_fwd(q, k, vt, tq, hb=8):
    S = q.shape[1]
    nq = S // tq
    scale = 1.0 / math.sqrt(DN + DR)
    c2 = scale * math.log2(math.e)
    tsub = tq
    nsub = tq // tsub

    qi_tab, kj_tab = _causal_pairs(nq, key_major=False)

    def body(qi_ref, kj_ref, q_ref, k_ref, vt_ref, o_ref, lse_ref, m_sc, l_sc, acc_sc):
        p = pl.program_id(1)
        qi, kj = qi_ref[p], kj_ref[p]

        @pl.when(kj == 0)
        def _():
            m_sc[...] = jnp.full_like(m_sc, -jnp.inf)
            l_sc[...] = jnp.zeros_like(l_sc)
            acc_sc[...] = jnp.zeros_like(acc_sc)

        def step(diag):
            chains = [(h, u) for h in range(hb) for u in range(nsub)]

            def scores(h, u):
                return lax.dot_general(k_ref[h], q_ref[h, u * tsub:(u + 1) * tsub, :], NT, preferred_element_type=F32)

            st_next = scores(*chains[0])
            for ci, (h, u) in enumerate(chains):
                st = st_next
                if ci + 1 < len(chains):
                    st_next = scores(*chains[ci + 1])
                cols = slice(u * tsub, (u + 1) * tsub)
                if diag:
                    r = lax.broadcasted_iota(jnp.int32, (tq, tsub), 0)
                    c = lax.broadcasted_iota(jnp.int32, (tq, tsub), 1) + u * tsub
                    st = jnp.where(r <= c, st, -jnp.inf)
                m_prev = m_sc[h, 0:1, cols]
                m_new = jnp.maximum(m_prev, jnp.max(st, axis=0, keepdims=True))
                alpha = jnp.exp2((m_prev - m_new) * c2)
                pt = jnp.exp2((st - m_new) * c2)
                l_sc[h, :, cols] = jnp.broadcast_to(alpha * l_sc[h, 0:1, cols] + jnp.sum(pt, axis=0, keepdims=True), (8, tsub))
                m_sc[h, :, cols] = jnp.broadcast_to(m_new, (8, tsub))
                acc_sc[h, :, cols] = alpha * acc_sc[h, :, cols] + jnp.dot(vt_ref[h], pt.astype(BF16), preferred_element_type=F32)

        @pl.when(kj < qi)
        def _():
            step(False)

        @pl.when(kj == qi)
        def _():
            step(True)
            for h in range(hb):
                l = l_sc[h, 0:1, :]
                o_ref[:, h * DV:(h + 1) * DV] = jnp.transpose(acc_sc[h] / l).astype(BF16)
                lse_ref[h] = m_sc[h] * scale + jnp.log(l_sc[h])

    return pl.pallas_call(
        body,
        grid_spec=pltpu.PrefetchScalarGridSpec(
            num_scalar_prefetch=2,
            grid=(H // hb, len(qi_tab)),
            in_specs=[pl.BlockSpec((hb, tq, HW), lambda h, p, qi, kj: (h, qi[p], 0)),
                      pl.BlockSpec((hb, tq, HW), lambda h, p, qi, kj: (h, kj[p], 0)),
                      pl.BlockSpec((hb, DV, tq), lambda h, p, qi, kj: (h, 0, kj[p]))],
            out_specs=[pl.BlockSpec((tq, hb * DV), lambda h, p, qi, kj: (qi[p], h)),
                       pl.BlockSpec((hb, 8, tq), lambda h, p, qi, kj: (h, 0, qi[p]))],
            scratch_shapes=[pltpu.VMEM((hb, 8, tq), F32), pltpu.VMEM((hb, 8, tq), F32), pltpu.VMEM((hb, DV, tq), F32)],
        ),
        out_shape=[jax.ShapeDtypeStruct((S, H * DV), BF16), jax.ShapeDtypeStruct((H, 8, S), F32)],
        compiler_params=_cparams(("parallel", "arbitrary"), 40 << 20),
        name="flash_fwd",
    )(qi_tab, kj_tab, q, k, vt)


def _sigmoid(x):
    return 1.0 / (1.0 + jnp.exp(-x))


HALO = 16


def _f32(ref):
    return ref[...].astype(F32)


def _shift_rows(u, prev, n, first):
    ts = u.shape[0]
    row = lax.broadcasted_iota(jnp.int32, u.shape, 0)
    out = pltpu.roll(u, n, 0)
    for j in range(n):
        halo = jnp.where(first, 0.0, prev[HALO - n + j:HALO - n + j + 1, :])
        out = jnp.where(row == j, halo, out)
    return out


def _gates_fwd(proj, attn, conv_w, ts):
    S = proj.shape[0]

    def body(attn_ref, zm_ref, cin_ref, bg_ref, cg_ref, zc_ref, cin_p, cg_p, w_ref, am_ref, ac_ref):
        first = pl.program_id(0) == 0
        zm = _f32(zm_ref)
        am_ref[...] = (_f32(attn_ref) * (zm * _sigmoid(zm))).astype(BF16)
        u = _f32(cg_ref) * _f32(cin_ref)
        up = _f32(cg_p) * _f32(cin_p)
        w = w_ref[...]
        conv = w[0:1, :] * _shift_rows(u, up, 2, first) + w[1:2, :] * _shift_rows(u, up, 1, first) + w[2:3, :] * u
        zc = _f32(zc_ref)
        ac_ref[...] = (_f32(bg_ref) * conv * (zc * _sigmoid(zc))).astype(BF16)

    seg = lambda name: pl.BlockSpec((ts, D), lambda i: (i, SEG[name]))
    prev = lambda name: pl.BlockSpec((HALO, D), lambda i: (jnp.maximum(i * (ts // HALO) - 1, 0), SEG[name]))
    return pl.pallas_call(
        body,
        grid=(S // ts,),
        in_specs=[pl.BlockSpec((ts, D), lambda i: (i, 0)), seg("z_mla"), seg("c_in"), seg("b_gate"), seg("c_gate"), seg("z_conv"),
                  prev("c_in"), prev("c_gate"), pl.BlockSpec((8, D), lambda i: (0, 0))],
        out_specs=[pl.BlockSpec((ts, D), lambda i: (i, 0))] * 2,
        out_shape=[jax.ShapeDtypeStruct((S, D), BF16)] * 2,
        compiler_params=_cparams(("arbitrary",), 48 << 20),
        name="gates_fwd",
    )(attn, proj, proj, proj, proj, proj, proj, proj, conv_w)


def _merge_fwd(proj, y_mla, y_conv, ts):
    S = proj.shape[0]

    def body(gm_ref, gc_ref, ym_ref, yc_ref, o_ref):
        o_ref[...] = (_sigmoid(_f32(gm_ref)) * _f32(ym_ref) + _sigmoid(_f32(gc_ref)) * _f32(yc_ref)).astype(BF16)

    seg = lambda name: pl.BlockSpec((ts, D), lambda i: (i, SEG[name]))
    row = pl.BlockSpec((ts, D), lambda i: (i, 0))
    return pl.pallas_call(
        body, grid=(S // ts,), in_specs=[seg("g_mla"), seg("g_conv"), row, row], out_specs=row,
        out_shape=jax.ShapeDtypeStruct((S, D), BF16), compiler_params=_cparams(("parallel",), 32 << 20), name="merge_fwd",
    )(proj, proj, y_mla, y_conv)


def _post_loss(out, x, target, g_post, ts):
    S = out.shape[0]

    def body(o_ref, x_ref, t_ref, g_ref, dy_ref, do_ref, dg_ref, loss_ref):
        i = pl.program_id(0)
        o = o_ref[...]
        r = _rms_scale(o)
        n = o * r
        g = g_ref[...]
        err = x_ref[...] + n * g - t_ref[...]
        dy = err * (1.0 / D)
        dy_ref[...] = dy
        dn = dy * g
        do_ref[...] = (r * (dn - n * jnp.mean(dn * n, axis=-1, keepdims=True))).astype(BF16)
        dg = jnp.sum(dy * n, axis=0, keepdims=True)
        part = jnp.sum(jnp.sum(err * err, axis=0, keepdims=True), axis=1, keepdims=True) * (0.5 / D)

        @pl.when(i == 0)
        def _():
            dg_ref[...] = jnp.zeros_like(dg_ref)
            loss_ref[...] = jnp.zeros_like(loss_ref)

        dg_ref[0:1, :] += dg
        loss_ref[...] += jnp.broadcast_to(part, loss_ref.shape)

    row = pl.BlockSpec((ts, D), lambda i: (i, 0))
    return pl.pallas_call(
        body, grid=(S // ts,),
        in_specs=[row, row, row, pl.BlockSpec((1, D), lambda i: (0, 0))],
        out_specs=[row, row, pl.BlockSpec((8, D), lambda i: (0, 0)), pl.BlockSpec((8, 128), lambda i: (0, 0))],
        out_shape=[jax.ShapeDtypeStruct((S, D), F32), jax.ShapeDtypeStruct((S, D), BF16),
                   jax.ShapeDtypeStruct((8, D), F32), jax.ShapeDtypeStruct((8, 128), F32)],
        compiler_params=_cparams(("arbitrary",), 40 << 20), name="post_loss",
    )(out, x, target, g_post)


def _merge_bwd(proj, y_mla, y_conv, dmerged, ts):
    S = proj.shape[0]

    def body(gm_ref, gc_ref, ym_ref, yc_ref, dm_ref, dym_ref, dyc_ref, dgm_ref, dgc_ref):
        dm = _f32(dm_ref)
        sm = _sigmoid(_f32(gm_ref))
        sc = _sigmoid(_f32(gc_ref))
        dym_ref[...] = (dm * sm).astype(BF16)
        dyc_ref[...] = (dm * sc).astype(BF16)
        dgm_ref[...] = (dm * _f32(ym_ref) * (sm * (1.0 - sm))).astype(BF16)
        dgc_ref[...] = (dm * _f32(yc_ref) * (sc * (1.0 - sc))).astype(BF16)

    seg = lambda name: pl.BlockSpec((ts, D), lambda i: (i, SEG[name]))
    row = pl.BlockSpec((ts, D), lambda i: (i, 0))
    return pl.pallas_call(
        body, grid=(S // ts,), in_specs=[seg("g_mla"), seg("g_conv"), row, row, row], out_specs=[row] * 4,
        out_shape=[jax.ShapeDtypeStruct((S, D), BF16)] * 4, compiler_params=_cparams(("parallel",), 40 << 20), name="merge_bwd",
    )(proj, proj, y_mla, y_conv, dmerged)


def _gates_bwd(proj, attn, da_mla, da_conv, conv_w, ts):
    S = proj.shape[0]
    nblk = S // ts

    def body(attn_ref, zm_ref, cin_ref, bg_ref, cg_ref, zc_ref, dam_ref, dac_ref, cin_p, cg_p, bg_n, zc_n, dac_n, w_ref,
             dattn_ref, delta_ref, dzm_ref, dcin_ref, dbg_ref, dcg_ref, dzc_ref, dw_ref):
        i = pl.program_id(0)
        first = i == 0
        last = i == nblk - 1
        zm = _f32(zm_ref)
        sg = _sigmoid(zm)
        silu = zm * sg
        attn = _f32(attn_ref)
        dam = _f32(dam_ref)
        dattn = dam * silu
        dattn_ref[...] = dattn.astype(BF16)
        dzm_ref[...] = (dam * attn * (sg * (1.0 + zm * (1.0 - sg)))).astype(BF16)
        prod = dattn * attn
        for h in range(H):
            col = jnp.sum(prod[:, h * DV:(h + 1) * DV], axis=1, keepdims=True)
            delta_ref[h] = _col_to_row8(col, ts)
        w = w_ref[...]
        cin, cg, bg, zc = _f32(cin_ref), _f32(cg_ref), _f32(bg_ref), _f32(zc_ref)
        u = cg * cin
        up = _f32(cg_p) * _f32(cin_p)
        u1 = _shift_rows(u, up, 1, first)
        u2 = _shift_rows(u, up, 2, first)
        conv = w[0:1, :] * u2 + w[1:2, :] * u1 + w[2:3, :] * u
        sgc = _sigmoid(zc)
        siluc = zc * sgc
        dac = _f32(dac_ref)
        dbg_ref[...] = (dac * conv * siluc).astype(BF16)
        dzc_ref[...] = (dac * bg * conv * (sgc * (1.0 + zc * (1.0 - sgc)))).astype(BF16)
        dconv = dac * bg * siluc
        zn = _f32(zc_n)
        dconv_n = jnp.where(last, 0.0, _f32(dac_n) * _f32(bg_n) * (zn * _sigmoid(zn)))
        row = lax.broadcasted_iota(jnp.int32, dconv.shape, 0)
        d1 = jnp.where(row == ts - 1, dconv_n[0:1, :], pltpu.roll(dconv, ts - 1, 0))
        d2 = jnp.where(row == ts - 1, dconv_n[1:2, :], jnp.where(row == ts - 2, dconv_n[0:1, :], pltpu.roll(dconv, ts - 2, 0)))
        du = w[2:3, :] * dconv + w[1:2, :] * d1 + w[0:1, :] * d2
        dcg_ref[...] = (du * cin).astype(BF16)
        dcin_ref[...] = (du * cg).astype(BF16)

        @pl.when(first)
        def _():
            dw_ref[...] = jnp.zeros_like(dw_ref)

        dw_ref[0:1, :] += jnp.sum(dconv * u2, axis=0, keepdims=True)
        dw_ref[1:2, :] += jnp.sum(dconv * u1, axis=0, keepdims=True)
        dw_ref[2:3, :] += jnp.sum(dconv * u, axis=0, keepdims=True)

    seg = lambda name: pl.BlockSpec((ts, D), lambda i: (i, SEG[name]))
    prev = lambda name: pl.BlockSpec((HALO, D), lambda i: (jnp.maximum(i * (ts // HALO) - 1, 0), SEG[name]))
    nxt = lambda blk: pl.BlockSpec((HALO, D), lambda i: (jnp.minimum((i + 1) * (ts // HALO), S // HALO - 1), blk))
    row = pl.BlockSpec((ts, D), lambda i: (i, 0))
    return pl.pallas_call(
        body, grid=(nblk,),
        in_specs=[row, seg("z_mla"), seg("c_in"), seg("b_gate"), seg("c_gate"), seg("z_conv"), row, row,
                  prev("c_in"), prev("c_gate"), nxt(SEG["b_gate"]), nxt(SEG["z_conv"]), nxt(0), pl.BlockSpec((8, D), lambda i: (0, 0))],
        out_specs=[row, pl.BlockSpec((H, 8, ts), lambda i: (0, 0, i)), row, row, row, row, row, pl.BlockSpec((8, D), lambda i: (0, 0))],
        out_shape=[jax.ShapeDtypeStruct((S, D), BF16), jax.ShapeDtypeStruct((H, 8, S), F32)] + [jax.ShapeDtypeStruct((S, D), BF16)] * 5
        + [jax.ShapeDtypeStruct((8, D), F32)],
        compiler_params=_cparams(("arbitrary",), 56 << 20), name="gates_bwd",
    )(attn, proj, proj, proj, proj, proj, da_mla, da_conv, proj, proj, proj, proj, da_conv, conv_w)


def _flash_bwd(q, k, v, do, lse, delta, tq, hb=2):
    S = q.shape[1]
    nq = S // tq
    scale = 1.0 / math.sqrt(DN + DR)
    c2 = scale * LOG2E
    tsub = min(256, tq)
    nsub = tq // tsub

    qi_tab, kj_tab = _causal_pairs(nq, key_major=True)
    npairs = nq * (nq + 1) // 2

    def body(qi_ref, kj_ref, q_ref, k_ref, v_ref, do_ref, lse_ref, dl_ref, dq_ref, dk_ref, dv_ref, dk_sc, dv_sc):
        p = pl.program_id(1)
        qi, kj = qi_ref[p], kj_ref[p]

        @pl.when(p == 0)
        def _():
            dq_ref[...] = jnp.zeros_like(dq_ref)

        @pl.when(qi == kj)
        def _():
            dk_sc[...] = jnp.zeros_like(dk_sc)
            dv_sc[...] = jnp.zeros_like(dv_sc)

        def step(diag):
            chains = [(h, u) for h in range(hb) for u in range(nsub)]

            def first_matmuls(h, u):
                sub = slice(u * tsub, (u + 1) * tsub)
                st = lax.dot_general(k_ref[h], q_ref[h, sub, :], NT, preferred_element_type=F32)
                dpt = lax.dot_general(v_ref[h], do_ref[sub, h * DV:(h + 1) * DV], NT, preferred_element_type=F32)
                return st, dpt

            nxt = first_matmuls(*chains[0])
            for ci, (h, u) in enumerate(chains):
                st, dpt = nxt
                if ci + 1 < len(chains):
                    nxt = first_matmuls(*chains[ci + 1])
                sub = slice(u * tsub, (u + 1) * tsub)
                pt = jnp.exp2(st * c2 - lse_ref[h, 0:1, sub] * LOG2E)
                if diag:
                    r = lax.broadcasted_iota(jnp.int32, (tq, tsub), 0)
                    c = lax.broadcasted_iota(jnp.int32, (tq, tsub), 1) + u * tsub
                    pt = jnp.where(r <= c, pt, 0.0)
                dst = (pt * (dpt - dl_ref[h, 0:1, sub])).astype(BF16)
                dv_sc[h] += jnp.dot(pt.astype(BF16), do_ref[sub, h * DV:(h + 1) * DV], preferred_element_type=F32)
                dk_sc[h] += jnp.dot(dst, q_ref[h, sub, :], preferred_element_type=F32)
                rows = pl.ds(pl.multiple_of(qi * tq + u * tsub, tsub), tsub)
                dq_ref[h, rows, :] += lax.dot_general(dst, k_ref[h], TN, preferred_element_type=F32)

        @pl.when(qi > kj)
        def _():
            step(False)

        @pl.when(qi == kj)
        def _():
            step(True)

        @pl.when(qi == nq - 1)
        def _():
            dk_ref[...] = dk_sc[...] * scale
            dv_ref[...] = dv_sc[...]

        @pl.when(p == npairs - 1)
        def _():
            dq_ref[...] = dq_ref[...] * scale

    return pl.pallas_call(
        body,
        grid_spec=pltpu.PrefetchScalarGridSpec(
            num_scalar_prefetch=2,
            grid=(H // hb, npairs),
            in_specs=[pl.BlockSpec((hb, tq, HW), lambda h, p, qi, kj: (h, qi[p], 0)),
                      pl.BlockSpec((hb, tq, HW), lambda h, p, qi, kj: (h, kj[p], 0)),
                      pl.BlockSpec((hb, tq, DV), lambda h, p, qi, kj: (h, kj[p], 0)),
                      pl.BlockSpec((tq, hb * DV), lambda h, p, qi, kj: (qi[p], h)),
                      pl.BlockSpec((hb, 8, tq), lambda h, p, qi, kj: (h, 0, qi[p])),
                      pl.BlockSpec((hb, 8, tq), lambda h, p, qi, kj: (h, 0, qi[p]))],
            out_specs=[pl.BlockSpec((hb, S, HW), lambda h, p, qi, kj: (h, 0, 0)),
                       pl.BlockSpec((hb, tq, HW), lambda h, p, qi, kj: (h, kj[p], 0)),
                       pl.BlockSpec((hb, tq, DV), lambda h, p, qi, kj: (h, kj[p], 0))],
            scratch_shapes=[pltpu.VMEM((hb, tq, HW), F32), pltpu.VMEM((hb, tq, DV), F32)],
        ),
        out_shape=[jax.ShapeDtypeStruct((H, S, HW), F32), jax.ShapeDtypeStruct((H, S, HW), F32), jax.ShapeDtypeStruct((H, S, DV), F32)],
        compiler_params=_cparams(("parallel", "arbitrary"), VMEM_CAP),
        name="flash_bwd",
    )(qi_tab, kj_tab, q, k, v, do, lse, delta)


def _rms_bwd(xf, g, dn_out):
    r = _rms_scale(xf)
    n = xf * r
    dn = dn_out * g
    dx = r * (dn - n * jnp.mean(dn * n, axis=-1, keepdims=True))
    return dx, jnp.sum(dn_out * n, axis=0, keepdims=True)


def _qkv_bwd(proj, pos, freq, g_qa, g_kva, w_q, w_kv, dq, dk, dv, ts):
    S = proj.shape[0]

    def body(qa_ref, ckv_ref, pos_ref, freq_ref, gq_ref, gk_ref, wq_ref, wkv_ref, dq_ref, dk_ref, dv_ref,
             dqp_ref, dkvp_ref, dqa_ref, dckv_ref, dkr_ref, dgq_ref, dgk_ref):
        i = pl.program_id(0)
        cos, sin, lane = _rope_tables(pos_ref, freq_ref, ts)
        sgn_sin = jnp.where(lane < 32, sin, -sin)
        live = lane < DR
        kr_sum = jnp.zeros((ts, 128), F32)
        for h in range(H):
            dqh = dq_ref[h]
            dqp_ref[:, h * HW:h * HW + DN] = dqh[:, 0:DN].astype(BF16)
            t = dqh[:, DN:HW]
            dqp_ref[:, h * HW + DN:(h + 1) * HW] = jnp.where(live, t * cos + _rope_swap(t, lane) * sgn_sin, 0.0).astype(BF16)
            dkh = dk_ref[h]
            dkvp_ref[:, h * HW:h * HW + DN] = dkh[:, 0:DN].astype(BF16)
            dkvp_ref[:, h * HW + DN:(h + 1) * HW] = dv_ref[h].astype(BF16)
            kr_sum = kr_sum + dkh[:, DN:HW]
        dkr_ref[...] = jnp.where(live, kr_sum * cos + _rope_swap(kr_sum, lane) * sgn_sin, 0.0).astype(BF16)
        dqn = lax.dot_general(dqp_ref[...], wq_ref[...], NT, preferred_element_type=F32)
        dkvn = lax.dot_general(dkvp_ref[...], wkv_ref[...], NT, preferred_element_type=F32)
        dqa, dgq = _rms_bwd(_f32(qa_ref), gq_ref[...], dqn)
        dckv, dgk = _rms_bwd(_f32(ckv_ref), gk_ref[...], dkvn)
        dqa_ref[...] = dqa.astype(BF16)
        dckv_ref[...] = dckv.astype(BF16)

        @pl.when(i == 0)
        def _():
            dgq_ref[...] = jnp.zeros_like(dgq_ref)
            dgk_ref[...] = jnp.zeros_like(dgk_ref)

        dgq_ref[0:1, :] += dgq
        dgk_ref[0:1, :] += dgk

    rowb = lambda w, blk: pl.BlockSpec((ts, w), lambda i: (i, blk))
    full = lambda a: pl.BlockSpec(a.shape, lambda i: (0,) * a.ndim)
    heads = lambda w: pl.BlockSpec((H, ts, w), lambda i: (0, i, 0))
    return pl.pallas_call(
        body, grid=(S // ts,),
        in_specs=[rowb(RQ, OFF_QA // RQ), rowb(RKV, OFF_CKV // RKV), pl.BlockSpec((ts, 1), lambda i: (i, 0)), full(freq), full(g_qa), full(g_kva),
                  full(w_q), full(w_kv), heads(HW), heads(HW), heads(DV)],
        out_specs=[rowb(H * HW, 0), rowb(H * HW, 0), rowb(RQ, 0), rowb(RKV, 0), rowb(128, 0),
                   pl.BlockSpec((8, RQ), lambda i: (0, 0)), pl.BlockSpec((8, RKV), lambda i: (0, 0))],
        out_shape=[jax.ShapeDtypeStruct((S, H * HW), BF16), jax.ShapeDtypeStruct((S, H * HW), BF16), jax.ShapeDtypeStruct((S, RQ), BF16),
                   jax.ShapeDtypeStruct((S, RKV), BF16), jax.ShapeDtypeStruct((S, 128), BF16),
                   jax.ShapeDtypeStruct((8, RQ), F32), jax.ShapeDtypeStruct((8, RKV), F32)],
        compiler_params=_cparams(("arbitrary",), 56 << 20), name="qkv_bwd",
    )(proj, proj, pos, freq, g_qa, g_kva, w_q, w_kv, dq, dk, dv)


def _pack_dproj(parts, ts):
    S = parts[0].shape[0]
    widths = [p.shape[1] for p in parts]
    assert sum(widths) == NP

    def body(*refs):
        o_ref = refs[-1]
        off = 0
        for r, w in zip(refs[:-1], widths):
            o_ref[:, off:off + w] = r[...]
            off += w

    return pl.pallas_call(
        body, grid=(S // ts,),
        in_specs=[pl.BlockSpec((ts, w), lambda i: (i, 0)) for w in widths],
        out_specs=pl.BlockSpec((ts, NP), lambda i: (i, 0)),
        out_shape=jax.ShapeDtypeStruct((S, NP), BF16),
        compiler_params=_cparams(("parallel",), 48 << 20), name="pack_dproj",
    )(*parts)


def _prenorm_bwd(x, g, dh, dy, ts):
    S = x.shape[0]

    def body(x_ref, g_ref, dh_ref, dy_ref, gx_ref, dg_ref):
        dx, dg = _rms_bwd(x_ref[...], g_ref[...], dh_ref[...])
        gx_ref[...] = dy_ref[...] + dx

        @pl.when(pl.program_id(0) == 0)
        def _():
            dg_ref[...] = jnp.zeros_like(dg_ref)

        dg_ref[0:1, :] += dg

    row = pl.BlockSpec((ts, D), lambda i: (i, 0))
    return pl.pallas_call(
        body, grid=(S // ts,), in_specs=[row, pl.BlockSpec((1, D), lambda i: (0, 0)), row, row],
        out_specs=[row, pl.BlockSpec((8, D), lambda i: (0, 0))],
        out_shape=[jax.ShapeDtypeStruct((S, D), F32), jax.ShapeDtypeStruct((8, D), F32)],
        compiler_params=_cparams(("arbitrary",), 40 << 20), name="prenorm_bwd",
    )(x, g, dh, dy)


def _local_step(x, pos, target, g_pre, g_qa, g_kva, g_post, w_in, other_weights, reduce_grads=None):
    S = x.shape[0]
    ts = min(256, S)
    tq = min(512, S)
    tm = min(512, S)
    mm = functools.partial(_matmul, tm=tm)
    freq = _inv_freq_tile()

    h, ht = _prenorm(x, g_pre, ts)
    if len(other_weights) == 2:
        gather, assemble = other_weights
        proj, gathered = _matmul(h, w_in, mode="nn", out_dtype=BF16, tm=1024, tn=1408, tk=D, name="mm_proj", exchange=gather)
        other_weights = assemble(*gathered)
    else:
        proj = _matmul(h, w_in, mode="nn", out_dtype=BF16, tm=1024, tn=1408, tk=D, name="mm_proj")
    w_q, w_kv, conv_w8, w_o_mla, w_o_conv, w_out = other_weights
    q, k, v, vt, qn, kvn = _qkv_prep(proj, pos, freq, g_qa, g_kva, w_q, w_kv, ts)
    attn, lse = _flash_fwd(q, k, vt, tq)
    a_mla, a_conv = _gates_fwd(proj, attn, conv_w8, ts)
    y_mla = mm(a_mla, w_o_mla, mode="nn", out_dtype=BF16, tn=1024, tk=D, name="mm_y_mla")
    y_conv = mm(a_conv, w_o_conv, mode="nn", out_dtype=BF16, tn=1024, tk=D, name="mm_y_conv")
    merged = _merge_fwd(proj, y_mla, y_conv, ts)
    out = mm(merged, w_out, mode="nn", out_dtype=F32, tn=1024, tk=D, name="mm_out")
    dy, dout, dg_post, loss = _post_loss(out, x, target, g_post, ts)

    dmerged = mm(dout, w_out, mode="nt", out_dtype=BF16, tn=1024, tk=D, name="mm_dmerged")
    dw_out = _matmul(merged, dout, mode="tn", out_dtype=F32, tm=1024, tn=1024, tk=2048, name="mm_dw_out")
    dy_mla, dy_conv, dg_mla, dg_conv = _merge_bwd(proj, y_mla, y_conv, dmerged, ts)
    da_mla = mm(dy_mla, w_o_mla, mode="nt", out_dtype=BF16, tn=1024, tk=D, name="mm_da_mla")
    da_conv = mm(dy_conv, w_o_conv, mode="nt", out_dtype=BF16, tn=1024, tk=D, name="mm_da_conv")
    dw_o_mla = _matmul(a_mla, dy_mla, mode="tn", out_dtype=F32, tm=1024, tn=1024, tk=2048, name="mm_dw_o_mla")
    dw_o_conv = _matmul(a_conv, dy_conv, mode="tn", out_dtype=F32, tm=1024, tn=1024, tk=2048, name="mm_dw_o_conv")
    dattn, delta, dz_mla, dc_in, db_gate, dc_gate, dz_conv, dconv_w = _gates_bwd(proj, attn, da_mla, da_conv, conv_w8, min(128, S))
    dq, dk, dv = _flash_bwd(q, k, v, dattn, lse, delta, tq)
    dqp, dkvp, dq_a, dc_kv, dk_rope, dg_qa, dg_kva = _qkv_bwd(proj, pos, freq, g_qa, g_kva, w_q, w_kv, dq, dk, dv, min(128, S))
    dw_q = _matmul(qn, dqp, mode="tn", out_dtype=F32, tm=RQ, tn=1024, tk=2048, name="mm_dw_q")
    dw_kv = _matmul(kvn, dkvp, mode="tn", out_dtype=F32, tm=RKV, tn=1024, tk=2048, name="mm_dw_kv")
    dproj = _pack_dproj([dz_mla, dc_in, db_gate, dc_gate, dz_conv, dg_mla, dg_conv, dq_a, dc_kv, dk_rope], ts)
    dw_in = _matmul(ht, dproj, mode="nn", out_dtype=F32, tm=1024, tn=1408, tk=2048, name="mm_dw_in")
    res = dict(dw_in=dw_in, dw_q=dw_q, dw_kv=dw_kv, dconv_w=dconv_w, dw_o_mla=dw_o_mla, dw_o_conv=dw_o_conv, dw_out=dw_out)
    if reduce_grads is None:
        dh = _matmul(dproj, w_in, mode="nt", out_dtype=F32, tm=1024, tn=D, tk=1408, name="mm_dh")
    else:
        exchange, finish = reduce_grads(res)
        dh, got = _matmul(dproj, w_in, mode="nt", out_dtype=F32, tm=1024, tn=D, tk=1408, name="mm_dh", exchange=exchange)
        res["reduced"] = finish(got)
    grad_x, dg_pre = _prenorm_bwd(x, g_pre, dh, dy, ts)
    res.update(loss=loss, grad_x=grad_x, dg_pre=dg_pre, dg_qa=dg_qa, dg_kva=dg_kva, dg_post=dg_post)
    return res


def _my_id():
    return lax.axis_index("x") * 4 + lax.axis_index("y") * 2 + lax.axis_index("c")


def _place():
    x, y, c = lax.axis_index("x"), lax.axis_index("y"), lax.axis_index("c")
    return (x, y, c), (x, y, 1 - c), [(1 - x, y), (x, 1 - y), (1 - x, 1 - y)]


def _slot(px, py, pc):
    return 4 * px + 2 * py + pc


def _all_gather(arrays, name):
    n = len(arrays)

    def body(*refs):
        ins, outs = refs[:n], refs[n:2 * n]
        send_sems, recv_sems, local_sems = refs[2 * n:]
        me, sib, chips = _place()
        c = me[2]

        def copy(a, k, block, to, src=None):
            rows = outs[a].at[_slot(*block)]
            return pltpu.make_async_remote_copy(src_ref=rows if src is None else src, dst_ref=rows, send_sem=send_sems.at[a, k],
                                                recv_sem=recv_sems.at[a, k], device_id=to, device_id_type=MESH)

        local = [pltpu.make_async_copy(ins[a], outs[a].at[_slot(*me)], local_sems.at[a]) for a in range(n)]
        for cp in local:
            cp.start()
        sends = []
        for a in range(n):
            sends.append(copy(a, 0, me, sib, src=ins[a]))
            sends += [copy(a, 1 + j, me, (*chip, c), src=ins[a]) for j, chip in enumerate(chips)]
        for cp in sends:
            cp.start()
        for a in range(n):
            for j, chip in enumerate(chips):
                copy(a, 1 + j, (*chip, c), me).wait_recv()
                fwd = copy(a, 4 + j, (*chip, c), sib)
                fwd.start()
                sends.append(fwd)
        for a in range(n):
            copy(a, 0, sib, me).wait_recv()
            for j, chip in enumerate(chips):
                copy(a, 4 + j, (*chip, 1 - c), me).wait_recv()
        for cp in sends:
            cp.wait_send()
        for cp in local:
            cp.wait()

    anyspec = pl.BlockSpec(memory_space=pl.ANY)
    return pl.pallas_call(
        body,
        in_specs=[anyspec] * n,
        out_specs=[anyspec] * n,
        out_shape=[jax.ShapeDtypeStruct((NDEV,) + a.shape, a.dtype) for a in arrays],
        scratch_shapes=[pltpu.SemaphoreType.DMA((n, NDEV - 1)), pltpu.SemaphoreType.DMA((n, NDEV - 1)), pltpu.SemaphoreType.DMA((n,))],
        name=name,
    )(*arrays)


def _slab_exchange(arrays, nslots, pick):
    n = len(arrays)

    def copies(ins, outs, sems):
        send_sems, recv_sems = sems
        return [pltpu.make_async_remote_copy(src_ref=ins[a].at[pick(k)[0]], dst_ref=outs[a].at[k], send_sem=send_sems.at[a, k],
                                             recv_sem=recv_sems.at[a, k], device_id=pick(k)[1], device_id_type=MESH)
                for a in range(n) for k in range(nslots)]

    def start(ins, outs, sems):
        for cp in copies(ins, outs, sems):
            cp.start()

    def finish(ins, outs, sems):
        cps = copies(ins, outs, sems)
        for cp in cps:
            cp.wait_recv()
        for cp in cps:
            cp.wait_send()

    return _Exchange(arrays, [jax.ShapeDtypeStruct((nslots,) + a.shape[1:], a.dtype) for a in arrays],
                     [pltpu.SemaphoreType.DMA((n, nslots)), pltpu.SemaphoreType.DMA((n, nslots))], start, finish)


def _direct_gather(arrays):
    n = len(arrays)

    def peer(d):
        p = (_my_id() + d) % NDEV
        return (p // 4, (p // 2) % 2, p % 2), p

    def copies(ins, outs, sems, receiving):
        send_sems, recv_sems, _ = sems
        slot = lambda d: peer(NDEV - d)[1] if receiving else _my_id()
        return [pltpu.make_async_remote_copy(src_ref=ins[a], dst_ref=outs[a].at[slot(d)], send_sem=send_sems.at[a, d - 1],
                                             recv_sem=recv_sems.at[a, d - 1], device_id=peer(d)[0], device_id_type=MESH)
                for d in range(1, NDEV) for a in range(n)]

    def local(ins, outs, sems):
        return [pltpu.make_async_copy(ins[a], outs[a].at[_my_id()], sems[2].at[a]) for a in range(n)]

    def start(ins, outs, sems):
        for cp in local(ins, outs, sems) + copies(ins, outs, sems, False):
            cp.start()

    def finish(ins, outs, sems):
        for cp in copies(ins, outs, sems, True):
            cp.wait_recv()
        for cp in copies(ins, outs, sems, False):
            cp.wait_send()
        for cp in local(ins, outs, sems):
            cp.wait()

    return _Exchange(arrays, [jax.ShapeDtypeStruct((NDEV,) + a.shape, a.dtype) for a in arrays],
                     [pltpu.SemaphoreType.DMA((n, NDEV - 1)), pltpu.SemaphoreType.DMA((n, NDEV - 1)), pltpu.SemaphoreType.DMA((n,))],
                     start, finish)


def _run_exchange(exchange, name):
    n = len(exchange.arrays)

    def body(*refs):
        ins, outs, sems = refs[:n], refs[n:2 * n], refs[2 * n:]
        exchange.start(ins, outs, sems)
        exchange.finish(ins, outs, sems)

    anyspec = pl.BlockSpec(memory_space=pl.ANY)
    return pl.pallas_call(body, in_specs=[anyspec] * n, out_specs=[anyspec] * n, out_shape=exchange.out_shapes,
                          scratch_shapes=exchange.sem_shapes, name=name)(*exchange.arrays)


def _slabs_bf16(x, rows_per_block, name):
    n, R, C = x.shape
    tr = min(rows_per_block, R)

    def body(x_ref, o_ref):
        o_ref[...] = x_ref[...].astype(BF16)

    blk = pl.BlockSpec((1, tr, C), lambda k, i: (k, i, 0))
    return pl.pallas_call(body, grid=(n, R // tr), in_specs=[blk], out_specs=blk, out_shape=jax.ShapeDtypeStruct(x.shape, BF16),
                          compiler_params=_cparams(("parallel", "parallel"), 32 << 20), name=name)(x)


def _to_sibling(k):
    me, sib, chips = _place()
    dest = sib if k == 0 else (*chips[k - 1], sib[2])
    return _slot(*dest), sib


def _to_chips(k):
    me, sib, chips = _place()
    return k, (*chips[k], me[2])


def _chip_sums(own, got, slots, rows_per_block, name):
    _, R, C = own.shape
    tr = min(rows_per_block, R)

    def body(slots_ref, own_ref, got_ref, o_ref):
        o_ref[0] = (own_ref[0] + got_ref[0].astype(F32)).astype(BF16)

    return pl.pallas_call(
        body,
        grid_spec=pltpu.PrefetchScalarGridSpec(
            num_scalar_prefetch=1, grid=(3, R // tr),
            in_specs=[pl.BlockSpec((1, tr, C), lambda j, i, s: (s[1 + j], i, 0)), pl.BlockSpec((1, tr, C), lambda j, i, s: (1 + j, i, 0))],
            out_specs=pl.BlockSpec((1, tr, C), lambda j, i, s: (j, i, 0))),
        out_shape=jax.ShapeDtypeStruct((3, R, C), BF16),
        compiler_params=_cparams(("parallel", "parallel"), 32 << 20), name=name,
    )(slots, own, got)


def _adamw_math(w, g, m, v):
    m = ADAM_B1 * m + (1.0 - ADAM_B1) * g
    v = ADAM_B2 * v + (1.0 - ADAM_B2) * (g * g)
    m_hat = m / (1.0 - ADAM_B1 ** ADAM_STEP)
    v_hat = v / (1.0 - ADAM_B2 ** ADAM_STEP)
    delta = -ADAM_LR * (m_hat / (jnp.sqrt(v_hat) + ADAM_EPS) + ADAM_WD * w)
    return delta, m, v


def _reduce_adamw(own, got1, got2, slots, w, m, v, rows_per_block, name):
    _, R, C = own.shape
    tr = min(rows_per_block, R)
    assert R % tr == 0

    def body(slots_ref, own_ref, g1_ref, g2_ref, w_ref, m_ref, v_ref, g_ref, d_ref, nm_ref, nv_ref):
        g = own_ref[0] + g1_ref[0].astype(F32)
        for j in range(3):
            g = g + g2_ref[j].astype(F32)
        g_ref[...] = g
        d, nm, nv = _adamw_math(w_ref[...], g, m_ref[...], v_ref[...])
        d_ref[...] = d
        nm_ref[...] = nm
        nv_ref[...] = nv

    blk = pl.BlockSpec((tr, C), lambda i, s: (i, 0))
    return pl.pallas_call(
        body,
        grid_spec=pltpu.PrefetchScalarGridSpec(
            num_scalar_prefetch=1, grid=(R // tr,),
            in_specs=[pl.BlockSpec((1, tr, C), lambda i, s: (s[0], i, 0)), pl.BlockSpec((1, tr, C), lambda i, s: (0, i, 0)),
                      pl.BlockSpec((3, tr, C), lambda i, s: (0, i, 0)), blk, blk, blk],
            out_specs=[blk] * 4),
        out_shape=[jax.ShapeDtypeStruct((R, C), F32)] * 4,
        compiler_params=_cparams(("parallel",), 48 << 20), name=name,
    )(slots, own, got1, got2, w, m, v)


def _sum_adamw(parts, w, m, v, rows_per_block, name):
    n, R, C = parts.shape
    tr = min(rows_per_block, R)
    assert R % tr == 0

    def body(p_ref, w_ref, m_ref, v_ref, g_ref, d_ref, nm_ref, nv_ref):
        g = p_ref[0]
        for j in range(1, n):
            g = g + p_ref[j]
        g_ref[...] = g
        d, nm, nv = _adamw_math(w_ref[...], g, m_ref[...], v_ref[...])
        d_ref[...] = d
        nm_ref[...] = nm
        nv_ref[...] = nv

    blk = pl.BlockSpec((tr, C), lambda i: (i, 0))
    return pl.pallas_call(
        body, grid=(R // tr,),
        in_specs=[pl.BlockSpec((n, tr, C), lambda i: (0, i, 0)), blk, blk, blk],
        out_specs=[blk] * 4,
        out_shape=[jax.ShapeDtypeStruct((R, C), F32)] * 4,
        compiler_params=_cparams(("parallel",), 48 << 20), name=name,
    )(parts, w, m, v)


def kernel(x, positions, pre_norm_g, w_in, q_a_norm_g, w_q_b, kv_a_norm_g, w_kv_b, conv_w, w_o_mla, w_o_conv, w_out, post_norm_g, loss_target, m_pre_norm_g, m_w_in, m_q_a_norm_g, m_w_q_b, m_kv_a_norm_g, m_w_kv_b, m_conv_w, m_w_o_mla, m_w_o_conv, m_w_out, m_post_norm_g, v_pre_norm_g, v_w_in, v_q_a_norm_g, v_w_q_b, v_kv_a_norm_g, v_w_kv_b, v_conv_w, v_w_o_mla, v_w_o_conv, v_w_out, v_post_norm_g):
    S = x.shape[1]
    conv_pad = jnp.zeros((8, 256), F32).at[0:3, :].set(conv_w)
    g_in, = _all_gather([w_in.astype(BF16)], "all_gather_w_in")
    w_in_f = _assemble_w_in(g_in)
    gather_rest = _direct_gather([w_q_b.astype(BF16), w_kv_b.astype(BF16), conv_pad, w_o_mla.astype(BF16), w_o_conv.astype(BF16),
                                  w_out.astype(BF16)])

    def assemble_rest(g_q, g_kv, g_cw, g_om, g_oc, g_out):
        return (_assemble_w_q(g_q), _concat_cols(g_kv, BF16, "assemble_w_kv"), _concat_cols(g_cw, F32, "assemble_conv_w"),
                g_om.reshape(D, D), g_oc.reshape(D, D), g_out.reshape(D, D))

    gnames = ["w_in", "w_q", "w_kv", "conv_w", "w_o_mla", "w_o_conv", "w_out"]
    (mx, my, mc), _, chips = _place()
    slots = jnp.stack([_slot(mx, my, mc)] + [_slot(cx, cy, mc) for cx, cy in chips]).astype(jnp.int32)

    def reduce_grads(r):
        own_in, own_in_bf16 = _split_dw_in(r["dw_in"])
        own = [own_in, _split_dw_q(r["dw_q"]), _split_cols(r["dw_kv"], NDEV, "split_dw_kv"),
               _split_cols(r["dconv_w"], NDEV, "split_dconv_w"), r["dw_o_mla"].reshape(NDEV, D // NDEV, D),
               r["dw_o_conv"].reshape(NDEV, D // NDEV, D), r["dw_out"].reshape(NDEV, D // NDEV, D)]
        slabs = [own_in_bf16] + [_slabs_bf16(g, 128, "bf16_" + nm) for g, nm in zip(own[1:], gnames[1:])]
        to_sibling = _slab_exchange(slabs, 4, _to_sibling)
        got1 = _run_exchange(to_sibling, "grads_to_sibling")
        sums = [_chip_sums(o, g1, slots, 128, "chip_sum_" + nm) for o, g1, nm in zip(own, got1, gnames)]
        return _slab_exchange(sums, 3, _to_chips), lambda got2: (own, got1, got2)

    row2 = lambda a: a.reshape(1, -1)
    r = _local_step(x[0], positions.reshape(S, 1), loss_target[0], row2(pre_norm_g), row2(q_a_norm_g), row2(kv_a_norm_g),
                    row2(post_norm_g), w_in_f, (gather_rest, assemble_rest), reduce_grads)
    own, got1, got2 = r["reduced"]
    small = jnp.concatenate([r["dg_pre"][0:1], r["dg_post"][0:1], jnp.pad(r["dg_qa"][0:1], ((0, 0), (0, D - RQ))),
                             jnp.pad(r["dg_kva"][0:1], ((0, 0), (0, D - RKV))), jnp.pad(r["loss"][0:1], ((0, 0), (0, D - 128))),
                             jnp.zeros((3, D), F32)], axis=0)
    p_small, = _all_gather([small], "all_gather_small")
    pad8 = lambda a: jnp.zeros((8, 256), F32).at[0:3, :].set(a)
    params = [(w_in, m_w_in, v_w_in), (w_q_b, m_w_q_b, v_w_q_b), (w_kv_b, m_w_kv_b, v_w_kv_b), (conv_pad, pad8(m_conv_w), pad8(v_conv_w)),
              (w_o_mla, m_w_o_mla, v_w_o_mla), (w_o_conv, m_w_o_conv, v_w_o_conv), (w_out, m_w_out, v_w_out)]
    o_in, o_q, o_kv, o_cw, o_om, o_oc, o_out = [
        _reduce_adamw(o, g1, g2, slots, w, m, v, 128, "adamw_" + nm)
        for o, g1, g2, (w, m, v), nm in zip(own, got1, got2, params, gnames)]
    o_cw = [a[0:3] for a in o_cw]
    padv = lambda a: jnp.pad(a.reshape(1, -1), ((0, 0), (0, D - a.shape[0])))
    stack = lambda pre, post, qa, kva: jnp.concatenate([row2(pre), row2(post), padv(qa), padv(kva), jnp.zeros((4, D), F32)], axis=0)
    o_g = _sum_adamw(p_small, stack(pre_norm_g, post_norm_g, q_a_norm_g, kv_a_norm_g),
                     stack(m_pre_norm_g, m_post_norm_g, m_q_a_norm_g, m_kv_a_norm_g),
                     stack(v_pre_norm_g, v_post_norm_g, v_q_a_norm_g, v_kv_a_norm_g), 8, "adamw_gains")
    loss = o_g[0][4, 0]
    outs = {}
    for idx, kind in enumerate(("grad", "delta", "new_m", "new_v")):
        o = o_g[idx]
        outs[kind] = dict(pre_norm_g=o[0], w_in=o_in[idx], q_a_norm_g=o[2, 0:RQ], w_q_b=o_q[idx], kv_a_norm_g=o[3, 0:RKV],
                          w_kv_b=o_kv[idx], conv_w=o_cw[idx], w_o_mla=o_om[idx], w_o_conv=o_oc[idx], w_out=o_out[idx], post_norm_g=o[1])
    names = ["pre_norm_g", "w_in", "q_a_norm_g", "w_q_b", "kv_a_norm_g", "w_kv_b", "conv_w", "w_o_mla", "w_o_conv", "w_out", "post_norm_g"]
    return (loss, r["grad_x"][None], *[outs["grad"][n] for n in names], *[outs["delta"][n] for n in names],
            *[outs["new_m"][n] for n in names], *[outs["new_v"][n] for n in names])
```

```python
import functools
import math

import jax
import jax.numpy as jnp
from jax import lax
from jax.experimental import pallas as pl
from jax.experimental.pallas import tpu as pltpu

F32 = jnp.float32
BF16 = jnp.bfloat16

NDEV = 8
D = 2048
H = 16
DN = 128
DR = 64
DV = 128
RQ = 512
RKV = 512
HW = 256
ROPE_THETA = 10000.0
RMS_EPS = 1e-6
N_IN = 15424
SHARD_IN = N_IN // NDEV
SMALL = RQ + RKV + DR
NP = 7 * D + RQ + RKV + 128
SEG = dict(z_mla=0, c_in=1, b_gate=2, c_gate=3, z_conv=4, g_mla=5, g_conv=6)
OFF_QA = 7 * D
OFF_CKV = OFF_QA + RQ
OFF_KR = OFF_CKV + RKV
EXT = 2176
VMEM_CAP = 56 * 1024 * 1024

ADAM_LR = 0.001
ADAM_B1 = 0.9
ADAM_B2 = 0.999
ADAM_EPS = 1e-08
ADAM_WD = 0.01
ADAM_STEP = 10

LOG2E = math.log2(math.e)
NN = (((1,), (0,)), ((), ()))
NT = (((1,), (1,)), ((), ()))
TN = (((0,), (0,)), ((), ()))
MESH = pl.DeviceIdType.MESH


def _cparams(sem, vmem_bytes):
    return pltpu.CompilerParams(dimension_semantics=sem, vmem_limit_bytes=int(min(VMEM_CAP, max(vmem_bytes, 16 << 20))))


def _nbytes(shape, dtype):
    return math.prod(shape) * jnp.dtype(dtype).itemsize


class _Exchange:
    def __init__(self, arrays, out_shapes, sem_shapes, start, finish):
        self.arrays, self.out_shapes, self.sem_shapes, self.start, self.finish = arrays, out_shapes, sem_shapes, start, finish
        self.n_sems = len(sem_shapes)


def _matmul(a, b, *, mode, out_dtype, tm, tn, tk, name, m_outer=False, exchange=None):
    if mode == "nn":
        (M, K), (K2, N) = a.shape, b.shape
    elif mode == "nt":
        (M, K), (N, K2) = a.shape, b.shape
    else:
        (K, M), (K2, N) = a.shape, b.shape
    assert K == K2, (a.shape, b.shape, mode)
    tm, tn, tk = min(tm, M), min(tn, N), min(tk, K)
    assert M % tm == 0 and N % tn == 0 and K % tk == 0, (M, N, K, tm, tn, tk)
    ni, nj, nk = M // tm, N // tn, K // tk
    dims = dict(nn=NN, nt=NT, tn=TN)[mode]

    if m_outer:
        grid = (ni, nj, nk)
        ij = lambda g0, g1: (g0, g1)
    else:
        grid = (nj, ni, nk)
        ij = lambda g0, g1: (g1, g0)

    if mode == "tn":
        a_spec = pl.BlockSpec((tk, tm), lambda g0, g1, k: (k, ij(g0, g1)[0]))
        a_tile = (tk, tm)
    else:
        a_spec = pl.BlockSpec((tm, tk), lambda g0, g1, k: (ij(g0, g1)[0], k))
        a_tile = (tm, tk)
    if mode == "nt":
        b_spec = pl.BlockSpec((tn, tk), lambda g0, g1, k: (ij(g0, g1)[1], k))
    else:
        b_spec = pl.BlockSpec((tk, tn), lambda g0, g1, k: (k, ij(g0, g1)[1]))
    o_spec = pl.BlockSpec((tm, tn), lambda g0, g1, k: ij(g0, g1))

    n_in = len(exchange.arrays) if exchange else 0

    def body(a_ref, b_ref, *refs):
        x_in, o_ref, x_out, scratch = refs[:n_in], refs[n_in], refs[n_in + 1:2 * n_in + 1], refs[2 * n_in + 1:]
        if exchange:
            sems = scratch[len(scratch) - exchange.n_sems:]
            step = (pl.program_id(0) * grid[1] + pl.program_id(1)) * grid[2] + pl.program_id(2)

            @pl.when(step == 0)
            def _():
                exchange.start(x_in, x_out, sems)

        prod = lax.dot_general(a_ref[...], b_ref[...], dims, preferred_element_type=F32)
        if nk == 1:
            o_ref[...] = prod.astype(o_ref.dtype)
        else:
            acc_ref = scratch[0]
            k = pl.program_id(2)

            @pl.when(k == 0)
            def _():
                acc_ref[...] = prod

            @pl.when((k > 0) & (k < nk - 1))
            def _():
                acc_ref[...] += prod

            @pl.when(k == nk - 1)
            def _():
                o_ref[...] = (acc_ref[...] + prod).astype(o_ref.dtype)

        if exchange:
            @pl.when(step == grid[0] * grid[1] * grid[2] - 1)
            def _():
                exchange.finish(x_in, x_out, sems)

    vmem = 2 * (_nbytes(a_tile, a.dtype) + _nbytes((tk, tn), b.dtype) + _nbytes((tm, tn), out_dtype)) + 2 * _nbytes((tm, tn), F32)
    anyspec = pl.BlockSpec(memory_space=pl.ANY)
    outs = pl.pallas_call(
        body,
        grid=grid,
        in_specs=[a_spec, b_spec] + [anyspec] * n_in,
        out_specs=[o_spec] + [anyspec] * n_in,
        out_shape=[jax.ShapeDtypeStruct((M, N), out_dtype)] + (exchange.out_shapes if exchange else []),
        scratch_shapes=([] if nk == 1 else [pltpu.VMEM((tm, tn), F32)]) + (exchange.sem_shapes if exchange else []),
        compiler_params=_cparams(("arbitrary",) * 3 if exchange else ("parallel", "parallel", "arbitrary"), vmem + (8 << 20)),
        name=name,
    )(a, b, *(exchange.arrays if exchange else []))
    return (outs[0], outs[1:]) if exchange else outs[0]


def _in_dest(k):
    return SHARD_IN * k - SMALL


def _assemble_w_in(g):
    R = 128

    def body(g_ref, o_ref, ext, acc):
        ext[...] = jnp.zeros_like(ext)
        acc[...] = jnp.zeros_like(acc)
        lane = lax.broadcasted_iota(jnp.int32, (R, EXT), 1)
        ext[:, 0:SHARD_IN] = g_ref[0].astype(F32)
        v = ext[...]
        acc[:, OFF_QA:NP] = jnp.where(lane[:, 0:NP - OFF_QA] < SMALL, v[:, 0:NP - OFF_QA], 0.0)
        w = v[:, 1024:2048]
        w = pltpu.roll(w, 1024 - 64, 1)
        acc[:, 0:1024] = jnp.where(lane[:, 0:1024] < SHARD_IN - SMALL, w, 0.0)
        for k in range(1, NDEV):
            ext[:, 0:SHARD_IN] = g_ref[k].astype(F32)
            dest = _in_dest(k)
            t, o = dest // 128, dest % 128
            width = -(-(o + SHARD_IN) // 128) * 128
            v = pltpu.roll(ext[...], o, 1)[:, 0:width]
            acc[:, 128 * t:128 * t + width] += v
        o_ref[...] = acc[...].astype(BF16)

    return pl.pallas_call(
        body,
        grid=(D // R,),
        in_specs=[pl.BlockSpec((NDEV, R, SHARD_IN), lambda i: (0, i, 0))],
        out_specs=pl.BlockSpec((R, NP), lambda i: (i, 0)),
        out_shape=jax.ShapeDtypeStruct((D, NP), BF16),
        scratch_shapes=[pltpu.VMEM((R, EXT), F32), pltpu.VMEM((R, NP), F32)],
        compiler_params=_cparams(("parallel",), 40 << 20),
        name="assemble_w_in",
    )(g)


def _split_dw_in(dw):
    R = 128

    def body(dw_ref, o_ref, ob_ref):
        def put(k, cols, v):
            o_ref[k, :, cols] = v
            ob_ref[k, :, cols] = v.astype(BF16)

        lane = lax.broadcasted_iota(jnp.int32, (R, 1024), 1)
        put(0, slice(0, 1024), dw_ref[:, OFF_QA:OFF_QA + 1024])
        tail = dw_ref[:, OFF_QA + 1024:NP]
        tail = jnp.concatenate([tail, jnp.zeros((R, 1024 - 128), F32)], axis=1)
        head = dw_ref[:, 0:1024]
        mixed = jnp.where(lane < 64, tail, pltpu.roll(head, 64, 1))
        put(0, slice(1024, SHARD_IN), mixed[:, 0:SHARD_IN - 1024])
        for k in range(1, NDEV):
            dest = _in_dest(k)
            t, o = dest // 128, dest % 128
            width = -(-(o + SHARD_IN) // 128) * 128
            v = dw_ref[:, 128 * t:128 * t + width]
            v = pltpu.roll(v, width - o, 1)
            put(k, slice(0, SHARD_IN), v[:, 0:SHARD_IN])

    blk = pl.BlockSpec((NDEV, R, SHARD_IN), lambda i: (0, i, 0))
    return pl.pallas_call(
        body,
        grid=(D // R,),
        in_specs=[pl.BlockSpec((R, NP), lambda i: (i, 0))],
        out_specs=[blk, blk],
        out_shape=[jax.ShapeDtypeStruct((NDEV, D, SHARD_IN), F32), jax.ShapeDtypeStruct((NDEV, D, SHARD_IN), BF16)],
        compiler_params=_cparams(("parallel",), 48 << 20),
        name="split_dw_in",
    )(dw)


def _assemble_w_q(g):
    def body(g_ref, o_ref):
        lane = lax.broadcasted_iota(jnp.int32, (RQ, 128), 1)
        lo = lane < 64
        for k in range(NDEV):
            t0 = g_ref[k, :, 0:128].astype(F32)
            t1 = g_ref[k, :, 128:256].astype(F32)
            t2 = g_ref[k, :, 256:384].astype(F32)
            base = 2 * k * HW
            o_ref[:, base:base + 128] = t0.astype(BF16)
            o_ref[:, base + 128:base + 256] = jnp.where(lo, t1, 0.0).astype(BF16)
            o_ref[:, base + 256:base + 384] = pltpu.roll(jnp.where(lo, t2, t1), 64, 1).astype(BF16)
            o_ref[:, base + 384:base + 512] = jnp.where(lo, pltpu.roll(t2, 64, 1), 0.0).astype(BF16)

    return pl.pallas_call(
        body,
        out_shape=jax.ShapeDtypeStruct((RQ, H * HW), BF16),
        compiler_params=_cparams(None, 32 << 20),
        name="assemble_w_q",
    )(g)


def _split_dw_q(dw):
    def body(dw_ref, o_ref):
        lane = lax.broadcasted_iota(jnp.int32, (RQ, 128), 1)
        lo = lane < 64
        for k in range(NDEV):
            base = 2 * k * HW
            a = dw_ref[:, base:base + 128]
            b = dw_ref[:, base + 128:base + 256]
            c = pltpu.roll(dw_ref[:, base + 256:base + 384], 64, 1)
            d = pltpu.roll(dw_ref[:, base + 384:base + 512], 64, 1)
            o_ref[k, :, 0:128] = a
            o_ref[k, :, 128:256] = jnp.where(lo, b, c)
            o_ref[k, :, 256:384] = jnp.where(lo, c, d)

    return pl.pallas_call(
        body,
        out_shape=jax.ShapeDtypeStruct((NDEV, RQ, 384), F32),
        compiler_params=_cparams(None, 32 << 20),
        name="split_dw_q",
    )(dw)


def _concat_cols(g, dtype, name):
    n, R, C = g.shape

    def body(g_ref, o_ref):
        for k in range(n):
            o_ref[:, k * C:(k + 1) * C] = g_ref[k].astype(dtype)

    return pl.pallas_call(body, out_shape=jax.ShapeDtypeStruct((R, n * C), dtype), compiler_params=_cparams(None, 32 << 20), name=name)(g)


def _split_cols(x, n, name):
    R, NC = x.shape
    C = NC // n

    def body(x_ref, o_ref):
        for k in range(n):
            o_ref[k] = x_ref[:, k * C:(k + 1) * C]

    return pl.pallas_call(body, out_shape=jax.ShapeDtypeStruct((n, R, C), x.dtype), compiler_params=_cparams(None, 32 << 20), name=name)(x)


def _rms_scale(xf):
    return lax.rsqrt(jnp.mean(xf * xf, axis=-1, keepdims=True) + RMS_EPS)


def _prenorm(x, g, ts):
    S = x.shape[0]

    def body(x_ref, g_ref, h_ref, ht_ref):
        xf = x_ref[...]
        h = xf * _rms_scale(xf) * g_ref[...]
        h_ref[...] = h.astype(BF16)
        ht_ref[...] = jnp.transpose(h).astype(BF16)

    return pl.pallas_call(
        body,
        grid=(S // ts,),
        in_specs=[pl.BlockSpec((ts, D), lambda i: (i, 0)), pl.BlockSpec((1, D), lambda i: (0, 0))],
        out_specs=[pl.BlockSpec((ts, D), lambda i: (i, 0)), pl.BlockSpec((D, ts), lambda i: (0, i))],
        out_shape=[jax.ShapeDtypeStruct((S, D), BF16), jax.ShapeDtypeStruct((D, S), BF16)],
        compiler_params=_cparams(("parallel",), 32 << 20),
        name="prenorm",
    )(x, g)


def _inv_freq_tile():
    inv_freq = ROPE_THETA ** (-jnp.arange(0, DR, 2, dtype=F32) / DR)
    return jnp.tile(inv_freq, 4).reshape(1, 128)


def _rope_tables(pos_ref, freq_ref, ts):
    lane = lax.broadcasted_iota(jnp.int32, (ts, 128), 1)
    ang = pos_ref[...].astype(F32) * freq_ref[...]
    return jnp.cos(ang), jnp.sin(ang), lane


def _rope_swap(t, lane):
    return jnp.where(lane < 32, pltpu.roll(t, 96, 1), pltpu.roll(t, 32, 1))


def _qkv_prep(proj, pos, freq, g_qa, g_kva, w_q, w_kv, ts):
    S = proj.shape[0]

    def body(qa_ref, ckv_ref, kr_ref, pos_ref, freq_ref, gq_ref, gk_ref, wq_ref, wkv_ref, q_ref, k_ref, v_ref, vt_ref, qn_ref, kvn_ref):
        qa = _f32(qa_ref)
        qn = (qa * _rms_scale(qa) * gq_ref[...]).astype(BF16)
        ckv = _f32(ckv_ref)
        kvn = (ckv * _rms_scale(ckv) * gk_ref[...]).astype(BF16)
        qn_ref[...] = qn
        kvn_ref[...] = kvn
        cos, sin, lane = _rope_tables(pos_ref, freq_ref, ts)
        sgn_sin = jnp.where(lane < 32, -sin, sin)
        live = lane < DR
        kr = _f32(kr_ref)
        kr = jnp.where(live, kr * cos + _rope_swap(kr, lane) * sgn_sin, 0.0).astype(BF16)
        qf = jnp.dot(qn, wq_ref[...], preferred_element_type=F32)
        kvf = jnp.dot(kvn, wkv_ref[...], preferred_element_type=F32)
        for h in range(H):
            q_ref[h, :, 0:DN] = qf[:, h * HW:h * HW + DN].astype(BF16)
            t = qf[:, h * HW + DN:(h + 1) * HW]
            q_ref[h, :, DN:HW] = jnp.where(live, t * cos + _rope_swap(t, lane) * sgn_sin, 0.0).astype(BF16)
            k_ref[h, :, 0:DN] = kvf[:, h * HW:h * HW + DN].astype(BF16)
            k_ref[h, :, DN:HW] = kr
            vh = kvf[:, h * HW + DN:(h + 1) * HW]
            v_ref[h] = vh.astype(BF16)
            vt_ref[h] = jnp.transpose(vh).astype(BF16)

    row = lambda w, blk: pl.BlockSpec((ts, w), lambda i: (i, blk))
    full = lambda a: pl.BlockSpec(a.shape, lambda i: (0,) * a.ndim)
    return pl.pallas_call(
        body,
        grid=(S // ts,),
        in_specs=[row(RQ, OFF_QA // RQ), row(RKV, OFF_CKV // RKV), row(128, OFF_KR // 128),
                  pl.BlockSpec((ts, 1), lambda i: (i, 0)), full(freq), full(g_qa), full(g_kva), full(w_q), full(w_kv)],
        out_specs=[pl.BlockSpec((H, ts, HW), lambda i: (0, i, 0)), pl.BlockSpec((H, ts, HW), lambda i: (0, i, 0)),
                   pl.BlockSpec((H, ts, DV), lambda i: (0, i, 0)), pl.BlockSpec((H, DV, ts), lambda i: (0, 0, i)),
                   pl.BlockSpec((ts, RQ), lambda i: (i, 0)), pl.BlockSpec((ts, RKV), lambda i: (i, 0))],
        out_shape=[jax.ShapeDtypeStruct((H, S, HW), BF16), jax.ShapeDtypeStruct((H, S, HW), BF16),
                   jax.ShapeDtypeStruct((H, S, DV), BF16), jax.ShapeDtypeStruct((H, DV, S), BF16),
                   jax.ShapeDtypeStruct((S, RQ), BF16), jax.ShapeDtypeStruct((S, RKV), BF16)],
        compiler_params=_cparams(("parallel",), 48 << 20),
        name="qkv_prep",
    )(proj, proj, proj, pos, freq, g_qa, g_kva, w_q, w_kv)


def _col_to_row8(col, n):
    return jnp.transpose(jnp.broadcast_to(col, (n, 128)))[0:8, :]


def _flash_fwd(q, k, vt, tq, hb=8):
    S = q.shape[1]
    nq = S // tq
    scale = 1.0 / math.sqrt(DN + DR)
    c2 = scale * LOG2E

    def body(q_ref, k_hbm, vt_hbm, o_ref, lse_ref, kbuf, vbuf, sems, m_sc, l_sc, acc_sc):
        h0 = pl.program_id(0) * hb
        qi = pl.program_id(1)

        def fetch(j, slot):
            keys = pl.ds(pl.multiple_of(j * tq, tq), tq)
            return (pltpu.make_async_copy(k_hbm.at[pl.ds(h0, hb), keys, :], kbuf.at[slot], sems.at[slot, 0]),
                    pltpu.make_async_copy(vt_hbm.at[pl.ds(h0, hb), :, keys], vbuf.at[slot], sems.at[slot, 1]))

        def start(j, slot):
            for cp in fetch(j, slot):
                cp.start()

        def wait(j, slot):
            for cp in fetch(j, slot):
                cp.wait()

        m_sc[...] = jnp.full_like(m_sc, -jnp.inf)
        l_sc[...] = jnp.zeros_like(l_sc)
        acc_sc[...] = jnp.zeros_like(acc_sc)
        start(0, 0)

        def step(slot, diag):
            def scores(h):
                return lax.dot_general(kbuf[slot, h], q_ref[h], NT, preferred_element_type=F32)

            st_next = scores(0)
            for h in range(hb):
                st = st_next
                if h + 1 < hb:
                    st_next = scores(h + 1)
                if diag:
                    r = lax.broadcasted_iota(jnp.int32, (tq, tq), 0)
                    c = lax.broadcasted_iota(jnp.int32, (tq, tq), 1)
                    st = jnp.where(r <= c, st, -jnp.inf)
                m_prev = m_sc[h, 0:1, :]
                m_new = jnp.maximum(m_prev, jnp.max(st, axis=0, keepdims=True))
                alpha = jnp.exp2((m_prev - m_new) * c2)
                pt = jnp.exp2((st - m_new) * c2)
                l_sc[h] = jnp.broadcast_to(alpha * l_sc[h, 0:1, :] + jnp.sum(pt, axis=0, keepdims=True), (8, tq))
                m_sc[h] = jnp.broadcast_to(m_new, (8, tq))
                acc_sc[h] = alpha * acc_sc[h] + jnp.dot(vbuf[slot, h], pt.astype(BF16), preferred_element_type=F32)

        def below_diagonal(j, carry):
            slot = j % 2
            wait(j, slot)
            start(j + 1, 1 - slot)
            step(slot, False)
            return carry

        lax.fori_loop(0, qi, below_diagonal, 0)
        last = qi % 2
        wait(qi, last)
        step(last, True)
        for h in range(hb):
            l = l_sc[h, 0:1, :]
            o_ref[:, h * DV:(h + 1) * DV] = jnp.transpose(acc_sc[h] / l).astype(BF16)
            lse_ref[h] = m_sc[h] * scale + jnp.log(l_sc[h])

    anyspec = pl.BlockSpec(memory_space=pl.ANY)
    return pl.pallas_call(
        body,
        grid=(H // hb, nq),
        in_specs=[pl.BlockSpec((hb, tq, HW), lambda h, i: (h, i, 0)), anyspec, anyspec],
        out_specs=[pl.BlockSpec((tq, hb * DV), lambda h, i: (i, h)), pl.BlockSpec((hb, 8, tq), lambda h, i: (h, 0, i))],
        out_shape=[jax.ShapeDtypeStruct((S, H * DV), BF16), jax.ShapeDtypeStruct((H, 8, S), F32)],
        scratch_shapes=[pltpu.VMEM((2, hb, tq, HW), BF16), pltpu.VMEM((2, hb, DV, tq), BF16), pltpu.SemaphoreType.DMA((2, 2)),
                        pltpu.VMEM((hb, 8, tq), F32), pltpu.VMEM((hb, 8, tq), F32), pltpu.VMEM((hb, DV, tq), F32)],
        compiler_params=_cparams(("parallel", "arbitrary"), 48 << 20),
        name="flash_fwd",
    )(q, k, vt)


def _sigmoid(x):
    return 1.0 / (1.0 + jnp.exp(-x))


HALO = 16


def _f32(ref):
    return ref[...].astype(F32)


def _shift_rows(u, prev, n, first):
    ts = u.shape[0]
    row = lax.broadcasted_iota(jnp.int32, u.shape, 0)
    out = pltpu.roll(u, n, 0)
    for j in range(n):
        halo = jnp.where(first, 0.0, prev[HALO - n + j:HALO - n + j + 1, :])
        out = jnp.where(row == j, halo, out)
    return out


def _gates_fwd(proj, attn, conv_w, ts):
    S = proj.shape[0]

    def body(attn_ref, zm_ref, cin_ref, bg_ref, cg_ref, zc_ref, cin_p, cg_p, w_ref, am_ref, ac_ref):
        first = pl.program_id(0) == 0
        zm = _f32(zm_ref)
        am_ref[...] = (_f32(attn_ref) * (zm * _sigmoid(zm))).astype(BF16)
        u = _f32(cg_ref) * _f32(cin_ref)
        up = _f32(cg_p) * _f32(cin_p)
        w = w_ref[...]
        conv = w[0:1, :] * _shift_rows(u, up, 2, first) + w[1:2, :] * _shift_rows(u, up, 1, first) + w[2:3, :] * u
        zc = _f32(zc_ref)
        ac_ref[...] = (_f32(bg_ref) * conv * (zc * _sigmoid(zc))).astype(BF16)

    seg = lambda name: pl.BlockSpec((ts, D), lambda i: (i, SEG[name]))
    prev = lambda name: pl.BlockSpec((HALO, D), lambda i: (jnp.maximum(i * (ts // HALO) - 1, 0), SEG[name]))
    return pl.pallas_call(
        body,
        grid=(S // ts,),
        in_specs=[pl.BlockSpec((ts, D), lambda i: (i, 0)), seg("z_mla"), seg("c_in"), seg("b_gate"), seg("c_gate"), seg("z_conv"),
                  prev("c_in"), prev("c_gate"), pl.BlockSpec((8, D), lambda i: (0, 0))],
        out_specs=[pl.BlockSpec((ts, D), lambda i: (i, 0))] * 2,
        out_shape=[jax.ShapeDtypeStruct((S, D), BF16)] * 2,
        compiler_params=_cparams(("arbitrary",), 48 << 20),
        name="gates_fwd",
    )(attn, proj, proj, proj, proj, proj, proj, proj, conv_w)


def _merge_fwd(proj, y_mla, y_conv, ts):
    S = proj.shape[0]

    def body(gm_ref, gc_ref, ym_ref, yc_ref, o_ref):
        o_ref[...] = (_sigmoid(_f32(gm_ref)) * _f32(ym_ref) + _sigmoid(_f32(gc_ref)) * _f32(yc_ref)).astype(BF16)

    seg = lambda name: pl.BlockSpec((ts, D), lambda i: (i, SEG[name]))
    row = pl.BlockSpec((ts, D), lambda i: (i, 0))
    return pl.pallas_call(
        body, grid=(S // ts,), in_specs=[seg("g_mla"), seg("g_conv"), row, row], out_specs=row,
        out_shape=jax.ShapeDtypeStruct((S, D), BF16), compiler_params=_cparams(("parallel",), 32 << 20), name="merge_fwd",
    )(proj, proj, y_mla, y_conv)


def _post_loss(out, x, target, g_post, ts):
    S = out.shape[0]

    def body(o_ref, x_ref, t_ref, g_ref, dy_ref, do_ref, dg_ref, loss_ref):
        i = pl.program_id(0)
        o = o_ref[...]
        r = _rms_scale(o)
        n = o * r
        g = g_ref[...]
        err = x_ref[...] + n * g - t_ref[...]
        dy = err * (1.0 / D)
        dy_ref[...] = dy
        dn = dy * g
        do_ref[...] = (r * (dn - n * jnp.mean(dn * n, axis=-1, keepdims=True))).astype(BF16)
        dg = jnp.sum(dy * n, axis=0, keepdims=True)
        part = jnp.sum(jnp.sum(err * err, axis=0, keepdims=True), axis=1, keepdims=True) * (0.5 / D)

        @pl.when(i == 0)
        def _():
            dg_ref[...] = jnp.zeros_like(dg_ref)
            loss_ref[...] = jnp.zeros_like(loss_ref)

        dg_ref[0:1, :] += dg
        loss_ref[...] += jnp.broadcast_to(part, loss_ref.shape)

    row = pl.BlockSpec((ts, D), lambda i: (i, 0))
    return pl.pallas_call(
        body, grid=(S // ts,),
        in_specs=[row, row, row, pl.BlockSpec((1, D), lambda i: (0, 0))],
        out_specs=[row, row, pl.BlockSpec((8, D), lambda i: (0, 0)), pl.BlockSpec((8, 128), lambda i: (0, 0))],
        out_shape=[jax.ShapeDtypeStruct((S, D), F32), jax.ShapeDtypeStruct((S, D), BF16),
                   jax.ShapeDtypeStruct((8, D), F32), jax.ShapeDtypeStruct((8, 128), F32)],
        compiler_params=_cparams(("arbitrary",), 40 << 20), name="post_loss",
    )(out, x, target, g_post)


def _merge_bwd(proj, y_mla, y_conv, dmerged, ts):
    S = proj.shape[0]

    def body(gm_ref, gc_ref, ym_ref, yc_ref, dm_ref, dym_ref, dyc_ref, dgm_ref, dgc_ref):
        dm = _f32(dm_ref)
        sm = _sigmoid(_f32(gm_ref))
        sc = _sigmoid(_f32(gc_ref))
        dym_ref[...] = (dm * sm).astype(BF16)
        dyc_ref[...] = (dm * sc).astype(BF16)
        dgm_ref[...] = (dm * _f32(ym_ref) * (sm * (1.0 - sm))).astype(BF16)
        dgc_ref[...] = (dm * _f32(yc_ref) * (sc * (1.0 - sc))).astype(BF16)

    seg = lambda name: pl.BlockSpec((ts, D), lambda i: (i, SEG[name]))
    row = pl.BlockSpec((ts, D), lambda i: (i, 0))
    return pl.pallas_call(
        body, grid=(S // ts,), in_specs=[seg("g_mla"), seg("g_conv"), row, row, row], out_specs=[row] * 4,
        out_shape=[jax.ShapeDtypeStruct((S, D), BF16)] * 4, compiler_params=_cparams(("parallel",), 40 << 20), name="merge_bwd",
    )(proj, proj, y_mla, y_conv, dmerged)


def _gates_bwd(proj, attn, da_mla, da_conv, conv_w, ts):
    S = proj.shape[0]
    nblk = S // ts

    def body(attn_ref, zm_ref, cin_ref, bg_ref, cg_ref, zc_ref, dam_ref, dac_ref, cin_p, cg_p, bg_n, zc_n, dac_n, w_ref,
             dattn_ref, delta_ref, dzm_ref, dcin_ref, dbg_ref, dcg_ref, dzc_ref, dw_ref):
        i = pl.program_id(0)
        first = i == 0
        last = i == nblk - 1
        zm = _f32(zm_ref)
        sg = _sigmoid(zm)
        silu = zm * sg
        attn = _f32(attn_ref)
        dam = _f32(dam_ref)
        dattn = dam * silu
        dattn_ref[...] = dattn.astype(BF16)
        dzm_ref[...] = (dam * attn * (sg * (1.0 + zm * (1.0 - sg)))).astype(BF16)
        prod = dattn * attn
        for h in range(H):
            col = jnp.sum(prod[:, h * DV:(h + 1) * DV], axis=1, keepdims=True)
            delta_ref[h] = _col_to_row8(col, ts)
        w = w_ref[...]
        cin, cg, bg, zc = _f32(cin_ref), _f32(cg_ref), _f32(bg_ref), _f32(zc_ref)
        u = cg * cin
        up = _f32(cg_p) * _f32(cin_p)
        u1 = _shift_rows(u, up, 1, first)
        u2 = _shift_rows(u, up, 2, first)
        conv = w[0:1, :] * u2 + w[1:2, :] * u1 + w[2:3, :] * u
        sgc = _sigmoid(zc)
        siluc = zc * sgc
        dac = _f32(dac_ref)
        dbg_ref[...] = (dac * conv * siluc).astype(BF16)
        dzc_ref[...] = (dac * bg * conv * (sgc * (1.0 + zc * (1.0 - sgc)))).astype(BF16)
        dconv = dac * bg * siluc
        zn = _f32(zc_n)
        dconv_n = jnp.where(last, 0.0, _f32(dac_n) * _f32(bg_n) * (zn * _sigmoid(zn)))
        row = lax.broadcasted_iota(jnp.int32, dconv.shape, 0)
        d1 = jnp.where(row == ts - 1, dconv_n[0:1, :], pltpu.roll(dconv, ts - 1, 0))
        d2 = jnp.where(row == ts - 1, dconv_n[1:2, :], jnp.where(row == ts - 2, dconv_n[0:1, :], pltpu.roll(dconv, ts - 2, 0)))
        du = w[2:3, :] * dconv + w[1:2, :] * d1 + w[0:1, :] * d2
        dcg_ref[...] = (du * cin).astype(BF16)
        dcin_ref[...] = (du * cg).astype(BF16)

        @pl.when(first)
        def _():
            dw_ref[...] = jnp.zeros_like(dw_ref)

        dw_ref[0:1, :] += jnp.sum(dconv * u2, axis=0, keepdims=True)
        dw_ref[1:2, :] += jnp.sum(dconv * u1, axis=0, keepdims=True)
        dw_ref[2:3, :] += jnp.sum(dconv * u, axis=0, keepdims=True)

    seg = lambda name: pl.BlockSpec((ts, D), lambda i: (i, SEG[name]))
    prev = lambda name: pl.BlockSpec((HALO, D), lambda i: (jnp.maximum(i * (ts // HALO) - 1, 0), SEG[name]))
    nxt = lambda blk: pl.BlockSpec((HALO, D), lambda i: (jnp.minimum((i + 1) * (ts // HALO), S // HALO - 1), blk))
    row = pl.BlockSpec((ts, D), lambda i: (i, 0))
    return pl.pallas_call(
        body, grid=(nblk,),
        in_specs=[row, seg("z_mla"), seg("c_in"), seg("b_gate"), seg("c_gate"), seg("z_conv"), row, row,
                  prev("c_in"), prev("c_gate"), nxt(SEG["b_gate"]), nxt(SEG["z_conv"]), nxt(0), pl.BlockSpec((8, D), lambda i: (0, 0))],
        out_specs=[row, pl.BlockSpec((H, 8, ts), lambda i: (0, 0, i)), row, row, row, row, row, pl.BlockSpec((8, D), lambda i: (0, 0))],
        out_shape=[jax.ShapeDtypeStruct((S, D), BF16), jax.ShapeDtypeStruct((H, 8, S), F32)] + [jax.ShapeDtypeStruct((S, D), BF16)] * 5
        + [jax.ShapeDtypeStruct((8, D), F32)],
        compiler_params=_cparams(("arbitrary",), 56 << 20), name="gates_bwd",
    )(attn, proj, proj, proj, proj, proj, da_mla, da_conv, proj, proj, proj, proj, da_conv, conv_w)


def _flash_bwd(q, k, v, do, lse, delta, tq, hb=2):
    S = q.shape[1]
    nq = S // tq
    scale = 1.0 / math.sqrt(DN + DR)
    c2 = scale * LOG2E
    tsub = min(256, tq)
    nsub = tq // tsub

    def body(k_ref, v_ref, q_hbm, do_hbm, lse_hbm, dl_hbm, dq_ref, dk_ref, dv_ref, qbuf, dobuf, lsebuf, dlbuf, sems, dk_sc, dv_sc):
        h0 = pl.program_id(0) * hb
        kj = pl.program_id(1)

        def fetch(i, slot):
            rows = pl.ds(pl.multiple_of(i * tq, tq), tq)
            heads = pl.ds(h0, hb)
            return (pltpu.make_async_copy(q_hbm.at[heads, rows, :], qbuf.at[slot], sems.at[slot, 0]),
                    pltpu.make_async_copy(do_hbm.at[rows, pl.ds(pl.multiple_of(h0 * DV, hb * DV), hb * DV)], dobuf.at[slot], sems.at[slot, 1]),
                    pltpu.make_async_copy(lse_hbm.at[heads, :, rows], lsebuf.at[slot], sems.at[slot, 2]),
                    pltpu.make_async_copy(dl_hbm.at[heads, :, rows], dlbuf.at[slot], sems.at[slot, 3]))

        def start(i, slot):
            for cp in fetch(i, slot):
                cp.start()

        def wait(i, slot):
            for cp in fetch(i, slot):
                cp.wait()

        @pl.when(kj == 0)
        def _():
            dq_ref[...] = jnp.zeros_like(dq_ref)

        dk_sc[...] = jnp.zeros_like(dk_sc)
        dv_sc[...] = jnp.zeros_like(dv_sc)
        start(kj, 0)

        def step(qi, slot, diag):
            chains = [(h, u) for h in range(hb) for u in range(nsub)]

            def first_matmuls(h, u):
                sub = slice(u * tsub, (u + 1) * tsub)
                st = lax.dot_general(k_ref[h], qbuf[slot, h, sub, :], NT, preferred_element_type=F32)
                dpt = lax.dot_general(v_ref[h], dobuf[slot, sub, h * DV:(h + 1) * DV], NT, preferred_element_type=F32)
                return st, dpt

            nxt = first_matmuls(*chains[0])
            for ci, (h, u) in enumerate(chains):
                st, dpt = nxt
                if ci + 1 < len(chains):
                    nxt = first_matmuls(*chains[ci + 1])
                sub = slice(u * tsub, (u + 1) * tsub)
                pt = jnp.exp2(st * c2 - lsebuf[slot, h, 0:1, sub] * LOG2E)
                if diag:
                    r = lax.broadcasted_iota(jnp.int32, (tq, tsub), 0)
                    c = lax.broadcasted_iota(jnp.int32, (tq, tsub), 1) + u * tsub
                    pt = jnp.where(r <= c, pt, 0.0)
                dst = (pt * (dpt - dlbuf[slot, h, 0:1, sub])).astype(BF16)
                dv_sc[h] += jnp.dot(pt.astype(BF16), dobuf[slot, sub, h * DV:(h + 1) * DV], preferred_element_type=F32)
                dk_sc[h] += jnp.dot(dst, qbuf[slot, h, sub, :], preferred_element_type=F32)
                rows = pl.ds(pl.multiple_of(qi * tq + u * tsub, tsub), tsub)
                dq_ref[h, rows, :] += lax.dot_general(dst, k_ref[h], TN, preferred_element_type=F32)

        wait(kj, 0)

        @pl.when(kj + 1 < nq)
        def _():
            start(kj + 1, 1)

        step(kj, 0, True)

        def after_diagonal(i, carry):
            slot = (i - kj) % 2
            wait(i, slot)

            @pl.when(i + 1 < nq)
            def _():
                start(i + 1, 1 - slot)

            step(i, slot, False)
            return carry

        lax.fori_loop(kj + 1, nq, after_diagonal, 0)
        dk_ref[...] = dk_sc[...] * scale
        dv_ref[...] = dv_sc[...]

        @pl.when(kj == nq - 1)
        def _():
            dq_ref[...] = dq_ref[...] * scale

    anyspec = pl.BlockSpec(memory_space=pl.ANY)
    return pl.pallas_call(
        body,
        grid=(H // hb, nq),
        in_specs=[pl.BlockSpec((hb, tq, HW), lambda h, j: (h, j, 0)), pl.BlockSpec((hb, tq, DV), lambda h, j: (h, j, 0)),
                  anyspec, anyspec, anyspec, anyspec],
        out_specs=[pl.BlockSpec((hb, S, HW), lambda h, j: (h, 0, 0)), pl.BlockSpec((hb, tq, HW), lambda h, j: (h, j, 0)),
                   pl.BlockSpec((hb, tq, DV), lambda h, j: (h, j, 0))],
        out_shape=[jax.ShapeDtypeStruct((H, S, HW), F32), jax.ShapeDtypeStruct((H, S, HW), F32), jax.ShapeDtypeStruct((H, S, DV), F32)],
        scratch_shapes=[pltpu.VMEM((2, hb, tq, HW), BF16), pltpu.VMEM((2, tq, hb * DV), BF16), pltpu.VMEM((2, hb, 8, tq), F32),
                        pltpu.VMEM((2, hb, 8, tq), F32), pltpu.SemaphoreType.DMA((2, 4)),
                        pltpu.VMEM((hb, tq, HW), F32), pltpu.VMEM((hb, tq, DV), F32)],
        compiler_params=_cparams(("parallel", "arbitrary"), VMEM_CAP),
        name="flash_bwd",
    )(k, v, q, do, lse, delta)


def _rms_bwd(xf, g, dn_out):
    r = _rms_scale(xf)
    n = xf * r
    dn = dn_out * g
    dx = r * (dn - n * jnp.mean(dn * n, axis=-1, keepdims=True))
    return dx, jnp.sum(dn_out * n, axis=0, keepdims=True)


def _qkv_bwd(proj, pos, freq, g_qa, g_kva, w_q, w_kv, dq, dk, dv, ts):
    S = proj.shape[0]

    def body(qa_ref, ckv_ref, pos_ref, freq_ref, gq_ref, gk_ref, wq_ref, wkv_ref, dq_ref, dk_ref, dv_ref,
             dqp_ref, dkvp_ref, dqa_ref, dckv_ref, dkr_ref, dgq_ref, dgk_ref):
        i = pl.program_id(0)
        cos, sin, lane = _rope_tables(pos_ref, freq_ref, ts)
        sgn_sin = jnp.where(lane < 32, sin, -sin)
        live = lane < DR
        kr_sum = jnp.zeros((ts, 128), F32)
        for h in range(H):
            dqh = dq_ref[h]
            dqp_ref[:, h * HW:h * HW + DN] = dqh[:, 0:DN].astype(BF16)
            t = dqh[:, DN:HW]
            dqp_ref[:, h * HW + DN:(h + 1) * HW] = jnp.where(live, t * cos + _rope_swap(t, lane) * sgn_sin, 0.0).astype(BF16)
            dkh = dk_ref[h]
            dkvp_ref[:, h * HW:h * HW + DN] = dkh[:, 0:DN].astype(BF16)
            dkvp_ref[:, h * HW + DN:(h + 1) * HW] = dv_ref[h].astype(BF16)
            kr_sum = kr_sum + dkh[:, DN:HW]
        dkr_ref[...] = jnp.where(live, kr_sum * cos + _rope_swap(kr_sum, lane) * sgn_sin, 0.0).astype(BF16)
        dqn = lax.dot_general(dqp_ref[...], wq_ref[...], NT, preferred_element_type=F32)
        dkvn = lax.dot_general(dkvp_ref[...], wkv_ref[...], NT, preferred_element_type=F32)
        dqa, dgq = _rms_bwd(_f32(qa_ref), gq_ref[...], dqn)
        dckv, dgk = _rms_bwd(_f32(ckv_ref), gk_ref[...], dkvn)
        dqa_ref[...] = dqa.astype(BF16)
        dckv_ref[...] = dckv.astype(BF16)

        @pl.when(i == 0)
        def _():
            dgq_ref[...] = jnp.zeros_like(dgq_ref)
            dgk_ref[...] = jnp.zeros_like(dgk_ref)

        dgq_ref[0:1, :] += dgq
        dgk_ref[0:1, :] += dgk

    rowb = lambda w, blk: pl.BlockSpec((ts, w), lambda i: (i, blk))
    full = lambda a: pl.BlockSpec(a.shape, lambda i: (0,) * a.ndim)
    heads = lambda w: pl.BlockSpec((H, ts, w), lambda i: (0, i, 0))
    return pl.pallas_call(
        body, grid=(S // ts,),
        in_specs=[rowb(RQ, OFF_QA // RQ), rowb(RKV, OFF_CKV // RKV), pl.BlockSpec((ts, 1), lambda i: (i, 0)), full(freq), full(g_qa), full(g_kva),
                  full(w_q), full(w_kv), heads(HW), heads(HW), heads(DV)],
        out_specs=[rowb(H * HW, 0), rowb(H * HW, 0), rowb(RQ, 0), rowb(RKV, 0), rowb(128, 0),
                   pl.BlockSpec((8, RQ), lambda i: (0, 0)), pl.BlockSpec((8, RKV), lambda i: (0, 0))],
        out_shape=[jax.ShapeDtypeStruct((S, H * HW), BF16), jax.ShapeDtypeStruct((S, H * HW), BF16), jax.ShapeDtypeStruct((S, RQ), BF16),
                   jax.ShapeDtypeStruct((S, RKV), BF16), jax.ShapeDtypeStruct((S, 128), BF16),
                   jax.ShapeDtypeStruct((8, RQ), F32), jax.ShapeDtypeStruct((8, RKV), F32)],
        compiler_params=_cparams(("arbitrary",), 56 << 20), name="qkv_bwd",
    )(proj, proj, pos, freq, g_qa, g_kva, w_q, w_kv, dq, dk, dv)


def _pack_dproj(parts, ts):
    S = parts[0].shape[0]
    widths = [p.shape[1] for p in parts]
    assert sum(widths) == NP

    def body(*refs):
        o_ref = refs[-1]
        off = 0
        for r, w in zip(refs[:-1], widths):
            o_ref[:, off:off + w] = r[...]
            off += w

    return pl.pallas_call(
        body, grid=(S // ts,),
        in_specs=[pl.BlockSpec((ts, w), lambda i: (i, 0)) for w in widths],
        out_specs=pl.BlockSpec((ts, NP), lambda i: (i, 0)),
        out_shape=jax.ShapeDtypeStruct((S, NP), BF16),
        compiler_params=_cparams(("parallel",), 48 << 20), name="pack_dproj",
    )(*parts)


def _prenorm_bwd(x, g, dh, dy, ts):
    S = x.shape[0]

    def body(x_ref, g_ref, dh_ref, dy_ref, gx_ref, dg_ref):
        dx, dg = _rms_bwd(x_ref[...], g_ref[...], dh_ref[...])
        gx_ref[...] = dy_ref[...] + dx

        @pl.when(pl.program_id(0) == 0)
        def _():
            dg_ref[...] = jnp.zeros_like(dg_ref)

        dg_ref[0:1, :] += dg

    row = pl.BlockSpec((ts, D), lambda i: (i, 0))
    return pl.pallas_call(
        body, grid=(S // ts,), in_specs=[row, pl.BlockSpec((1, D), lambda i: (0, 0)), row, row],
        out_specs=[row, pl.BlockSpec((8, D), lambda i: (0, 0))],
        out_shape=[jax.ShapeDtypeStruct((S, D), F32), jax.ShapeDtypeStruct((8, D), F32)],
        compiler_params=_cparams(("arbitrary",), 40 << 20), name="prenorm_bwd",
    )(x, g, dh, dy)


def _local_step(x, pos, target, g_pre, g_qa, g_kva, g_post, w_in, other_weights, reduce_grads=None):
    S = x.shape[0]
    ts = min(256, S)
    tq = min(512, S)
    tm = min(512, S)
    mm = functools.partial(_matmul, tm=tm)
    freq = _inv_freq_tile()

    h, ht = _prenorm(x, g_pre, ts)
    if len(other_weights) == 2:
        gather, assemble = other_weights
        proj, gathered = _matmul(h, w_in, mode="nn", out_dtype=BF16, tm=1024, tn=1408, tk=D, name="mm_proj", exchange=gather)
        other_weights = assemble(*gathered)
    else:
        proj = _matmul(h, w_in, mode="nn", out_dtype=BF16, tm=1024, tn=1408, tk=D, name="mm_proj")
    w_q, w_kv, conv_w8, w_o_mla, w_o_conv, w_out = other_weights
    q, k, v, vt, qn, kvn = _qkv_prep(proj, pos, freq, g_qa, g_kva, w_q, w_kv, ts)
    attn, lse = _flash_fwd(q, k, vt, tq)
    a_mla, a_conv = _gates_fwd(proj, attn, conv_w8, ts)
    y_mla = mm(a_mla, w_o_mla, mode="nn", out_dtype=BF16, tn=1024, tk=D, name="mm_y_mla")
    y_conv = mm(a_conv, w_o_conv, mode="nn", out_dtype=BF16, tn=1024, tk=D, name="mm_y_conv")
    merged = _merge_fwd(proj, y_mla, y_conv, ts)
    out = mm(merged, w_out, mode="nn", out_dtype=F32, tn=1024, tk=D, name="mm_out")
    dy, dout, dg_post, loss = _post_loss(out, x, target, g_post, ts)

    dmerged = mm(dout, w_out, mode="nt", out_dtype=BF16, tn=1024, tk=D, name="mm_dmerged")
    dw_out = _matmul(merged, dout, mode="tn", out_dtype=F32, tm=1024, tn=1024, tk=2048, name="mm_dw_out")
    dy_mla, dy_conv, dg_mla, dg_conv = _merge_bwd(proj, y_mla, y_conv, dmerged, ts)
    da_mla = mm(dy_mla, w_o_mla, mode="nt", out_dtype=BF16, tn=1024, tk=D, name="mm_da_mla")
    da_conv = mm(dy_conv, w_o_conv, mode="nt", out_dtype=BF16, tn=1024, tk=D, name="mm_da_conv")
    dw_o_mla = _matmul(a_mla, dy_mla, mode="tn", out_dtype=F32, tm=1024, tn=1024, tk=2048, name="mm_dw_o_mla")
    dw_o_conv = _matmul(a_conv, dy_conv, mode="tn", out_dtype=F32, tm=1024, tn=1024, tk=2048, name="mm_dw_o_conv")
    dattn, delta, dz_mla, dc_in, db_gate, dc_gate, dz_conv, dconv_w = _gates_bwd(proj, attn, da_mla, da_conv, conv_w8, min(128, S))
    dq, dk, dv = _flash_bwd(q, k, v, dattn, lse, delta, tq)
    dqp, dkvp, dq_a, dc_kv, dk_rope, dg_qa, dg_kva = _qkv_bwd(proj, pos, freq, g_qa, g_kva, w_q, w_kv, dq, dk, dv, min(128, S))
    dw_q = _matmul(qn, dqp, mode="tn", out_dtype=F32, tm=RQ, tn=1024, tk=2048, name="mm_dw_q")
    dw_kv = _matmul(kvn, dkvp, mode="tn", out_dtype=F32, tm=RKV, tn=1024, tk=2048, name="mm_dw_kv")
    dproj = _pack_dproj([dz_mla, dc_in, db_gate, dc_gate, dz_conv, dg_mla, dg_conv, dq_a, dc_kv, dk_rope], ts)
    dw_in = _matmul(ht, dproj, mode="nn", out_dtype=F32, tm=1024, tn=1408, tk=2048, name="mm_dw_in")
    res = dict(dw_in=dw_in, dw_q=dw_q, dw_kv=dw_kv, dconv_w=dconv_w, dw_o_mla=dw_o_mla, dw_o_conv=dw_o_conv, dw_out=dw_out)
    if reduce_grads is None:
        dh = _matmul(dproj, w_in, mode="nt", out_dtype=F32, tm=1024, tn=D, tk=1408, name="mm_dh")
    else:
        exchange, finish = reduce_grads(res)
        dh, got = _matmul(dproj, w_in, mode="nt", out_dtype=F32, tm=1024, tn=D, tk=1408, name="mm_dh", exchange=exchange)
        res["reduced"] = finish(got)
    grad_x, dg_pre = _prenorm_bwd(x, g_pre, dh, dy, ts)
    res.update(loss=loss, grad_x=grad_x, dg_pre=dg_pre, dg_qa=dg_qa, dg_kva=dg_kva, dg_post=dg_post)
    return res


def _my_id():
    return lax.axis_index("x") * 4 + lax.axis_index("y") * 2 + lax.axis_index("c")


def _place():
    x, y, c = lax.axis_index("x"), lax.axis_index("y"), lax.axis_index("c")
    return (x, y, c), (x, y, 1 - c), [(1 - x, y), (x, 1 - y), (1 - x, 1 - y)]


def _slot(px, py, pc):
    return 4 * px + 2 * py + pc


def _all_gather(arrays, name):
    n = len(arrays)

    def body(*refs):
        ins, outs = refs[:n], refs[n:2 * n]
        send_sems, recv_sems, local_sems = refs[2 * n:]
        me, sib, chips = _place()
        c = me[2]

        def copy(a, k, block, to, src=None):
            rows = outs[a].at[_slot(*block)]
            return pltpu.make_async_remote_copy(src_ref=rows if src is None else src, dst_ref=rows, send_sem=send_sems.at[a, k],
                                                recv_sem=recv_sems.at[a, k], device_id=to, device_id_type=MESH)

        local = [pltpu.make_async_copy(ins[a], outs[a].at[_slot(*me)], local_sems.at[a]) for a in range(n)]
        for cp in local:
            cp.start()
        sends = []
        for a in range(n):
            sends.append(copy(a, 0, me, sib, src=ins[a]))
            sends += [copy(a, 1 + j, me, (*chip, c), src=ins[a]) for j, chip in enumerate(chips)]
        for cp in sends:
            cp.start()
        for a in range(n):
            for j, chip in enumerate(chips):
                copy(a, 1 + j, (*chip, c), me).wait_recv()
                fwd = copy(a, 4 + j, (*chip, c), sib)
                fwd.start()
                sends.append(fwd)
        for a in range(n):
            copy(a, 0, sib, me).wait_recv()
            for j, chip in enumerate(chips):
                copy(a, 4 + j, (*chip, 1 - c), me).wait_recv()
        for cp in sends:
            cp.wait_send()
        for cp in local:
            cp.wait()

    anyspec = pl.BlockSpec(memory_space=pl.ANY)
    return pl.pallas_call(
        body,
        in_specs=[anyspec] * n,
        out_specs=[anyspec] * n,
        out_shape=[jax.ShapeDtypeStruct((NDEV,) + a.shape, a.dtype) for a in arrays],
        scratch_shapes=[pltpu.SemaphoreType.DMA((n, NDEV - 1)), pltpu.SemaphoreType.DMA((n, NDEV - 1)), pltpu.SemaphoreType.DMA((n,))],
        name=name,
    )(*arrays)


def _slab_exchange(arrays, nslots, pick):
    n = len(arrays)

    def copies(ins, outs, sems):
        send_sems, recv_sems = sems
        return [pltpu.make_async_remote_copy(src_ref=ins[a].at[pick(k)[0]], dst_ref=outs[a].at[k], send_sem=send_sems.at[a, k],
                                             recv_sem=recv_sems.at[a, k], device_id=pick(k)[1], device_id_type=MESH)
                for a in range(n) for k in range(nslots)]

    def start(ins, outs, sems):
        for cp in copies(ins, outs, sems):
            cp.start()

    def finish(ins, outs, sems):
        cps = copies(ins, outs, sems)
        for cp in cps:
            cp.wait_recv()
        for cp in cps:
            cp.wait_send()

    return _Exchange(arrays, [jax.ShapeDtypeStruct((nslots,) + a.shape[1:], a.dtype) for a in arrays],
                     [pltpu.SemaphoreType.DMA((n, nslots)), pltpu.SemaphoreType.DMA((n, nslots))], start, finish)


def _direct_gather(arrays):
    n = len(arrays)

    def peer(d):
        p = (_my_id() + d) % NDEV
        return (p // 4, (p // 2) % 2, p % 2), p

    def copies(ins, outs, sems, receiving):
        send_sems, recv_sems, _ = sems
        slot = lambda d: peer(NDEV - d)[1] if receiving else _my_id()
        return [pltpu.make_async_remote_copy(src_ref=ins[a], dst_ref=outs[a].at[slot(d)], send_sem=send_sems.at[a, d - 1],
                                             recv_sem=recv_sems.at[a, d - 1], device_id=peer(d)[0], device_id_type=MESH)
                for d in range(1, NDEV) for a in range(n)]

    def local(ins, outs, sems):
        return [pltpu.make_async_copy(ins[a], outs[a].at[_my_id()], sems[2].at[a]) for a in range(n)]

    def start(ins, outs, sems):
        for cp in local(ins, outs, sems) + copies(ins, outs, sems, False):
            cp.start()

    def finish(ins, outs, sems):
        for cp in copies(ins, outs, sems, True):
            cp.wait_recv()
        for cp in copies(ins, outs, sems, False):
            cp.wait_send()
        for cp in local(ins, outs, sems):
            cp.wait()

    return _Exchange(arrays, [jax.ShapeDtypeStruct((NDEV,) + a.shape, a.dtype) for a in arrays],
                     [pltpu.SemaphoreType.DMA((n, NDEV - 1)), pltpu.SemaphoreType.DMA((n, NDEV - 1)), pltpu.SemaphoreType.DMA((n,))],
                     start, finish)


def _run_exchange(exchange, name):
    n = len(exchange.arrays)

    def body(*refs):
        ins, outs, sems = refs[:n], refs[n:2 * n], refs[2 * n:]
        exchange.start(ins, outs, sems)
        exchange.finish(ins, outs, sems)

    anyspec = pl.BlockSpec(memory_space=pl.ANY)
    return pl.pallas_call(body, in_specs=[anyspec] * n, out_specs=[anyspec] * n, out_shape=exchange.out_shapes,
                          scratch_shapes=exchange.sem_shapes, name=name)(*exchange.arrays)


def _slabs_bf16(x, rows_per_block, name):
    n, R, C = x.shape
    tr = min(rows_per_block, R)

    def body(x_ref, o_ref):
        o_ref[...] = x_ref[...].astype(BF16)

    blk = pl.BlockSpec((1, tr, C), lambda k, i: (k, i, 0))
    return pl.pallas_call(body, grid=(n, R // tr), in_specs=[blk], out_specs=blk, out_shape=jax.ShapeDtypeStruct(x.shape, BF16),
                          compiler_params=_cparams(("parallel", "parallel"), 32 << 20), name=name)(x)


def _to_sibling(k):
    me, sib, chips = _place()
    dest = sib if k == 0 else (*chips[k - 1], sib[2])
    return _slot(*dest), sib


def _to_chips(k):
    me, sib, chips = _place()
    return k, (*chips[k], me[2])


def _chip_sums(own, got, slots, rows_per_block, name):
    _, R, C = own.shape
    tr = min(rows_per_block, R)

    def body(slots_ref, own_ref, got_ref, o_ref):
        o_ref[0] = (own_ref[0] + got_ref[0].astype(F32)).astype(BF16)

    return pl.pallas_call(
        body,
        grid_spec=pltpu.PrefetchScalarGridSpec(
            num_scalar_prefetch=1, grid=(3, R // tr),
            in_specs=[pl.BlockSpec((1, tr, C), lambda j, i, s: (s[1 + j], i, 0)), pl.BlockSpec((1, tr, C), lambda j, i, s: (1 + j, i, 0))],
            out_specs=pl.BlockSpec((1, tr, C), lambda j, i, s: (j, i, 0))),
        out_shape=jax.ShapeDtypeStruct((3, R, C), BF16),
        compiler_params=_cparams(("parallel", "parallel"), 32 << 20), name=name,
    )(slots, own, got)


def _adamw_math(w, g, m, v):
    m = ADAM_B1 * m + (1.0 - ADAM_B1) * g
    v = ADAM_B2 * v + (1.0 - ADAM_B2) * (g * g)
    m_hat = m / (1.0 - ADAM_B1 ** ADAM_STEP)
    v_hat = v / (1.0 - ADAM_B2 ** ADAM_STEP)
    delta = -ADAM_LR * (m_hat / (jnp.sqrt(v_hat) + ADAM_EPS) + ADAM_WD * w)
    return delta, m, v


def _reduce_adamw(own, got1, got2, slots, w, m, v, rows_per_block, name):
    _, R, C = own.shape
    tr = min(rows_per_block, R)
    assert R % tr == 0

    def body(slots_ref, own_ref, g1_ref, g2_ref, w_ref, m_ref, v_ref, g_ref, d_ref, nm_ref, nv_ref):
        g = own_ref[0] + g1_ref[0].astype(F32)
        for j in range(3):
            g = g + g2_ref[j].astype(F32)
        g_ref[...] = g
        d, nm, nv = _adamw_math(w_ref[...], g, m_ref[...], v_ref[...])
        d_ref[...] = d
        nm_ref[...] = nm
        nv_ref[...] = nv

    blk = pl.BlockSpec((tr, C), lambda i, s: (i, 0))
    return pl.pallas_call(
        body,
        grid_spec=pltpu.PrefetchScalarGridSpec(
            num_scalar_prefetch=1, grid=(R // tr,),
            in_specs=[pl.BlockSpec((1, tr, C), lambda i, s: (s[0], i, 0)), pl.BlockSpec((1, tr, C), lambda i, s: (0, i, 0)),
                      pl.BlockSpec((3, tr, C), lambda i, s: (0, i, 0)), blk, blk, blk],
            out_specs=[blk] * 4),
        out_shape=[jax.ShapeDtypeStruct((R, C), F32)] * 4,
        compiler_params=_cparams(("parallel",), 48 << 20), name=name,
    )(slots, own, got1, got2, w, m, v)


def _sum_adamw(parts, w, m, v, rows_per_block, name):
    n, R, C = parts.shape
    tr = min(rows_per_block, R)
    assert R % tr == 0

    def body(p_ref, w_ref, m_ref, v_ref, g_ref, d_ref, nm_ref, nv_ref):
        g = p_ref[0]
        for j in range(1, n):
            g = g + p_ref[j]
        g_ref[...] = g
        d, nm, nv = _adamw_math(w_ref[...], g, m_ref[...], v_ref[...])
        d_ref[...] = d
        nm_ref[...] = nm
        nv_ref[...] = nv

    blk = pl.BlockSpec((tr, C), lambda i: (i, 0))
    return pl.pallas_call(
        body, grid=(R // tr,),
        in_specs=[pl.BlockSpec((n, tr, C), lambda i: (0, i, 0)), blk, blk, blk],
        out_specs=[blk] * 4,
        out_shape=[jax.ShapeDtypeStruct((R, C), F32)] * 4,
        compiler_params=_cparams(("parallel",), 48 << 20), name=name,
    )(parts, w, m, v)


def kernel(x, positions, pre_norm_g, w_in, q_a_norm_g, w_q_b, kv_a_norm_g, w_kv_b, conv_w, w_o_mla, w_o_conv, w_out, post_norm_g, loss_target, m_pre_norm_g, m_w_in, m_q_a_norm_g, m_w_q_b, m_kv_a_norm_g, m_w_kv_b, m_conv_w, m_w_o_mla, m_w_o_conv, m_w_out, m_post_norm_g, v_pre_norm_g, v_w_in, v_q_a_norm_g, v_w_q_b, v_kv_a_norm_g, v_w_kv_b, v_conv_w, v_w_o_mla, v_w_o_conv, v_w_out, v_post_norm_g):
    S = x.shape[1]
    conv_pad = jnp.zeros((8, 256), F32).at[0:3, :].set(conv_w)
    g_in, = _all_gather([w_in.astype(BF16)], "all_gather_w_in")
    w_in_f = _assemble_w_in(g_in)
    gather_rest = _direct_gather([w_q_b.astype(BF16), w_kv_b.astype(BF16), conv_pad, w_o_mla.astype(BF16), w_o_conv.astype(BF16),
                                  w_out.astype(BF16)])

    def assemble_rest(g_q, g_kv, g_cw, g_om, g_oc, g_out):
        return (_assemble_w_q(g_q), _concat_cols(g_kv, BF16, "assemble_w_kv"), _concat_cols(g_cw, F32, "assemble_conv_w"),
                g_om.reshape(D, D), g_oc.reshape(D, D), g_out.reshape(D, D))

    gnames = ["w_in", "w_q", "w_kv", "conv_w", "w_o_mla", "w_o_conv", "w_out"]
    (mx, my, mc), _, chips = _place()
    slots = jnp.stack([_slot(mx, my, mc)] + [_slot(cx, cy, mc) for cx, cy in chips]).astype(jnp.int32)

    def reduce_grads(r):
        own_in, own_in_bf16 = _split_dw_in(r["dw_in"])
        own = [own_in, _split_dw_q(r["dw_q"]), _split_cols(r["dw_kv"], NDEV, "split_dw_kv"),
               _split_cols(r["dconv_w"], NDEV, "split_dconv_w"), r["dw_o_mla"].reshape(NDEV, D // NDEV, D),
               r["dw_o_conv"].reshape(NDEV, D // NDEV, D), r["dw_out"].reshape(NDEV, D // NDEV, D)]
        slabs = [own_in_bf16] + [_slabs_bf16(g, 128, "bf16_" + nm) for g, nm in zip(own[1:], gnames[1:])]
        to_sibling = _slab_exchange(slabs, 4, _to_sibling)
        got1 = _run_exchange(to_sibling, "grads_to_sibling")
        sums = [_chip_sums(o, g1, slots, 128, "chip_sum_" + nm) for o, g1, nm in zip(own, got1, gnames)]
        return _slab_exchange(sums, 3, _to_chips), lambda got2: (own, got1, got2)

    row2 = lambda a: a.reshape(1, -1)
    r = _local_step(x[0], positions.reshape(S, 1), loss_target[0], row2(pre_norm_g), row2(q_a_norm_g), row2(kv_a_norm_g),
                    row2(post_norm_g), w_in_f, (gather_rest, assemble_rest), reduce_grads)
    own, got1, got2 = r["reduced"]
    small = jnp.concatenate([r["dg_pre"][0:1], r["dg_post"][0:1], jnp.pad(r["dg_qa"][0:1], ((0, 0), (0, D - RQ))),
                             jnp.pad(r["dg_kva"][0:1], ((0, 0), (0, D - RKV))), jnp.pad(r["loss"][0:1], ((0, 0), (0, D - 128))),
                             jnp.zeros((3, D), F32)], axis=0)
    p_small, = _all_gather([small], "all_gather_small")
    pad8 = lambda a: jnp.zeros((8, 256), F32).at[0:3, :].set(a)
    params = [(w_in, m_w_in, v_w_in), (w_q_b, m_w_q_b, v_w_q_b), (w_kv_b, m_w_kv_b, v_w_kv_b), (conv_pad, pad8(m_conv_w), pad8(v_conv_w)),
              (w_o_mla, m_w_o_mla, v_w_o_mla), (w_o_conv, m_w_o_conv, v_w_o_conv), (w_out, m_w_out, v_w_out)]
    o_in, o_q, o_kv, o_cw, o_om, o_oc, o_out = [
        _reduce_adamw(o, g1, g2, slots, w, m, v, 128, "adamw_" + nm)
        for o, g1, g2, (w, m, v), nm in zip(own, got1, got2, params, gnames)]
    o_cw = [a[0:3] for a in o_cw]
    padv = lambda a: jnp.pad(a.reshape(1, -1), ((0, 0), (0, D - a.shape[0])))
    stack = lambda pre, post, qa, kva: jnp.concatenate([row2(pre), row2(post), padv(qa), padv(kva), jnp.zeros((4, D), F32)], axis=0)
    o_g = _sum_adamw(p_small, stack(pre_norm_g, post_norm_g, q_a_norm_g, kv_a_norm_g),
                     stack(m_pre_norm_g, m_post_norm_g, m_q_a_norm_g, m_kv_a_norm_g),
                     stack(v_pre_norm_g, v_post_norm_g, v_q_a_norm_g, v_kv_a_norm_g), 8, "adamw_gains")
    loss = o_g[0][4, 0]
    outs = {}
    for idx, kind in enumerate(("grad", "delta", "new_m", "new_v")):
        o = o_g[idx]
        outs[kind] = dict(pre_norm_g=o[0], w_in=o_in[idx], q_a_norm_g=o[2, 0:RQ], w_q_b=o_q[idx], kv_a_norm_g=o[3, 0:RKV],
                          w_kv_b=o_kv[idx], conv_w=o_cw[idx], w_o_mla=o_om[idx], w_o_conv=o_oc[idx], w_out=o_out[idx], post_norm_g=o[1])
    names = ["pre_norm_g", "w_in", "q_a_norm_g", "w_q_b", "kv_a_norm_g", "w_kv_b", "conv_w", "w_o_mla", "w_o_conv", "w_out", "post_norm_g"]
    return (loss, r["grad_x"][None], *[outs["grad"][n] for n in names], *[outs["delta"][n] for n in names],
            *[outs["new_m"][n] for n in names], *[outs["new_v"][n] for n in names])
```

```python
import functools
import math

import jax
import jax.numpy as jnp
from jax import lax
from jax.experimental import pallas as pl
from jax.experimental.pallas import tpu as pltpu

F32 = jnp.float32
BF16 = jnp.bfloat16

NDEV = 8
D = 2048
H = 16
DN = 128
DR = 64
DV = 128
RQ = 512
RKV = 512
HW = 256
ROPE_THETA = 10000.0
RMS_EPS = 1e-6
N_IN = 15424
SHARD_IN = N_IN // NDEV
SMALL = RQ + RKV + DR
NP = 7 * D + RQ + RKV + 128
SEG = dict(z_mla=0, c_in=1, b_gate=2, c_gate=3, z_conv=4, g_mla=5, g_conv=6)
OFF_QA = 7 * D
OFF_CKV = OFF_QA + RQ
OFF_KR = OFF_CKV + RKV
EXT = 2176
VMEM_CAP = 56 * 1024 * 1024

ADAM_LR = 0.001
ADAM_B1 = 0.9
ADAM_B2 = 0.999
ADAM_EPS = 1e-08
ADAM_WD = 0.01
ADAM_STEP = 10

LOG2E = math.log2(math.e)
NN = (((1,), (0,)), ((), ()))
NT = (((1,), (1,)), ((), ()))
TN = (((0,), (0,)), ((), ()))
MESH = pl.DeviceIdType.MESH


def _cparams(sem, vmem_bytes):
    return pltpu.CompilerParams(dimension_semantics=sem, vmem_limit_bytes=int(min(VMEM_CAP, max(vmem_bytes, 16 << 20))))


def _nbytes(shape, dtype):
    return math.prod(shape) * jnp.dtype(dtype).itemsize


class _Exchange:
    def __init__(self, arrays, out_shapes, sem_shapes, start, finish):
        self.arrays, self.out_shapes, self.sem_shapes, self.start, self.finish = arrays, out_shapes, sem_shapes, start, finish
        self.n_sems = len(sem_shapes)


def _matmul(a, b, *, mode, out_dtype, tm, tn, tk, name, m_outer=False, exchange=None):
    if mode == "nn":
        (M, K), (K2, N) = a.shape, b.shape
    elif mode == "nt":
        (M, K), (N, K2) = a.shape, b.shape
    else:
        (K, M), (K2, N) = a.shape, b.shape
    assert K == K2, (a.shape, b.shape, mode)
    tm, tn, tk = min(tm, M), min(tn, N), min(tk, K)
    assert M % tm == 0 and N % tn == 0 and K % tk == 0, (M, N, K, tm, tn, tk)
    ni, nj, nk = M // tm, N // tn, K // tk
    dims = dict(nn=NN, nt=NT, tn=TN)[mode]

    if m_outer:
        grid = (ni, nj, nk)
        ij = lambda g0, g1: (g0, g1)
    else:
        grid = (nj, ni, nk)
        ij = lambda g0, g1: (g1, g0)

    if mode == "tn":
        a_spec = pl.BlockSpec((tk, tm), lambda g0, g1, k: (k, ij(g0, g1)[0]))
        a_tile = (tk, tm)
    else:
        a_spec = pl.BlockSpec((tm, tk), lambda g0, g1, k: (ij(g0, g1)[0], k))
        a_tile = (tm, tk)
    if mode == "nt":
        b_spec = pl.BlockSpec((tn, tk), lambda g0, g1, k: (ij(g0, g1)[1], k))
    else:
        b_spec = pl.BlockSpec((tk, tn), lambda g0, g1, k: (k, ij(g0, g1)[1]))
    o_spec = pl.BlockSpec((tm, tn), lambda g0, g1, k: ij(g0, g1))

    n_in = len(exchange.arrays) if exchange else 0

    def body(a_ref, b_ref, *refs):
        x_in, o_ref, x_out, scratch = refs[:n_in], refs[n_in], refs[n_in + 1:2 * n_in + 1], refs[2 * n_in + 1:]
        if exchange:
            sems = scratch[len(scratch) - exchange.n_sems:]
            step = (pl.program_id(0) * grid[1] + pl.program_id(1)) * grid[2] + pl.program_id(2)

            @pl.when(step == 0)
            def _():
                exchange.start(x_in, x_out, sems)

        prod = lax.dot_general(a_ref[...], b_ref[...], dims, preferred_element_type=F32)
        if nk == 1:
            o_ref[...] = prod.astype(o_ref.dtype)
        else:
            acc_ref = scratch[0]
            k = pl.program_id(2)

            @pl.when(k == 0)
            def _():
                acc_ref[...] = prod

            @pl.when((k > 0) & (k < nk - 1))
            def _():
                acc_ref[...] += prod

            @pl.when(k == nk - 1)
            def _():
                o_ref[...] = (acc_ref[...] + prod).astype(o_ref.dtype)

        if exchange:
            @pl.when(step == grid[0] * grid[1] * grid[2] - 1)
            def _():
                exchange.finish(x_in, x_out, sems)

    vmem = 2 * (_nbytes(a_tile, a.dtype) + _nbytes((tk, tn), b.dtype) + _nbytes((tm, tn), out_dtype)) + 2 * _nbytes((tm, tn), F32)
    anyspec = pl.BlockSpec(memory_space=pl.ANY)
    outs = pl.pallas_call(
        body,
        grid=grid,
        in_specs=[a_spec, b_spec] + [anyspec] * n_in,
        out_specs=[o_spec] + [anyspec] * n_in,
        out_shape=[jax.ShapeDtypeStruct((M, N), out_dtype)] + (exchange.out_shapes if exchange else []),
        scratch_shapes=([] if nk == 1 else [pltpu.VMEM((tm, tn), F32)]) + (exchange.sem_shapes if exchange else []),
        compiler_params=_cparams(("arbitrary",) * 3 if exchange else ("parallel", "parallel", "arbitrary"), vmem + (8 << 20)),
        name=name,
    )(a, b, *(exchange.arrays if exchange else []))
    return (outs[0], outs[1:]) if exchange else outs[0]


def _in_dest(k):
    return SHARD_IN * k - SMALL


def _assemble_w_in(g):
    R = 128

    def body(g_ref, o_ref, ext, acc):
        ext[...] = jnp.zeros_like(ext)
        acc[...] = jnp.zeros_like(acc)
        lane = lax.broadcasted_iota(jnp.int32, (R, EXT), 1)
        ext[:, 0:SHARD_IN] = g_ref[0].astype(F32)
        v = ext[...]
        acc[:, OFF_QA:NP] = jnp.where(lane[:, 0:NP - OFF_QA] < SMALL, v[:, 0:NP - OFF_QA], 0.0)
        w = v[:, 1024:2048]
        w = pltpu.roll(w, 1024 - 64, 1)
        acc[:, 0:1024] = jnp.where(lane[:, 0:1024] < SHARD_IN - SMALL, w, 0.0)
        for k in range(1, NDEV):
            ext[:, 0:SHARD_IN] = g_ref[k].astype(F32)
            dest = _in_dest(k)
            t, o = dest // 128, dest % 128
            width = -(-(o + SHARD_IN) // 128) * 128
            v = pltpu.roll(ext[...], o, 1)[:, 0:width]
            acc[:, 128 * t:128 * t + width] += v
        o_ref[...] = acc[...].astype(BF16)

    return pl.pallas_call(
        body,
        grid=(D // R,),
        in_specs=[pl.BlockSpec((NDEV, R, SHARD_IN), lambda i: (0, i, 0))],
        out_specs=pl.BlockSpec((R, NP), lambda i: (i, 0)),
        out_shape=jax.ShapeDtypeStruct((D, NP), BF16),
        scratch_shapes=[pltpu.VMEM((R, EXT), F32), pltpu.VMEM((R, NP), F32)],
        compiler_params=_cparams(("parallel",), 40 << 20),
        name="assemble_w_in",
    )(g)


def _split_dw_in(dw):
    R = 128

    def body(dw_ref, o_ref, ob_ref):
        def put(k, cols, v):
            o_ref[k, :, cols] = v
            ob_ref[k, :, cols] = v.astype(BF16)

        lane = lax.broadcasted_iota(jnp.int32, (R, 1024), 1)
        put(0, slice(0, 1024), dw_ref[:, OFF_QA:OFF_QA + 1024])
        tail = dw_ref[:, OFF_QA + 1024:NP]
        tail = jnp.concatenate([tail, jnp.zeros((R, 1024 - 128), F32)], axis=1)
        head = dw_ref[:, 0:1024]
        mixed = jnp.where(lane < 64, tail, pltpu.roll(head, 64, 1))
        put(0, slice(1024, SHARD_IN), mixed[:, 0:SHARD_IN - 1024])
        for k in range(1, NDEV):
            dest = _in_dest(k)
            t, o = dest // 128, dest % 128
            width = -(-(o + SHARD_IN) // 128) * 128
            v = dw_ref[:, 128 * t:128 * t + width]
            v = pltpu.roll(v, width - o, 1)
            put(k, slice(0, SHARD_IN), v[:, 0:SHARD_IN])

    blk = pl.BlockSpec((NDEV, R, SHARD_IN), lambda i: (0, i, 0))
    return pl.pallas_call(
        body,
        grid=(D // R,),
        in_specs=[pl.BlockSpec((R, NP), lambda i: (i, 0))],
        out_specs=[blk, blk],
        out_shape=[jax.ShapeDtypeStruct((NDEV, D, SHARD_IN), F32), jax.ShapeDtypeStruct((NDEV, D, SHARD_IN), BF16)],
        compiler_params=_cparams(("parallel",), 48 << 20),
        name="split_dw_in",
    )(dw)


def _assemble_w_q(g):
    def body(g_ref, o_ref):
        lane = lax.broadcasted_iota(jnp.int32, (RQ, 128), 1)
        lo = lane < 64
        for k in range(NDEV):
            t0 = g_ref[k, :, 0:128].astype(F32)
            t1 = g_ref[k, :, 128:256].astype(F32)
            t2 = g_ref[k, :, 256:384].astype(F32)
            base = 2 * k * HW
            o_ref[:, base:base + 128] = t0.astype(BF16)
            o_ref[:, base + 128:base + 256] = jnp.where(lo, t1, 0.0).astype(BF16)
            o_ref[:, base + 256:base + 384] = pltpu.roll(jnp.where(lo, t2, t1), 64, 1).astype(BF16)
            o_ref[:, base + 384:base + 512] = jnp.where(lo, pltpu.roll(t2, 64, 1), 0.0).astype(BF16)

    return pl.pallas_call(
        body,
        out_shape=jax.ShapeDtypeStruct((RQ, H * HW), BF16),
        compiler_params=_cparams(None, 32 << 20),
        name="assemble_w_q",
    )(g)


def _split_dw_q(dw):
    def body(dw_ref, o_ref):
        lane = lax.broadcasted_iota(jnp.int32, (RQ, 128), 1)
        lo = lane < 64
        for k in range(NDEV):
            base = 2 * k * HW
            a = dw_ref[:, base:base + 128]
            b = dw_ref[:, base + 128:base + 256]
            c = pltpu.roll(dw_ref[:, base + 256:base + 384], 64, 1)
            d = pltpu.roll(dw_ref[:, base + 384:base + 512], 64, 1)
            o_ref[k, :, 0:128] = a
            o_ref[k, :, 128:256] = jnp.where(lo, b, c)
            o_ref[k, :, 256:384] = jnp.where(lo, c, d)

    return pl.pallas_call(
        body,
        out_shape=jax.ShapeDtypeStruct((NDEV, RQ, 384), F32),
        compiler_params=_cparams(None, 32 << 20),
        name="split_dw_q",
    )(dw)


def _concat_cols(g, dtype, name):
    n, R, C = g.shape

    def body(g_ref, o_ref):
        for k in range(n):
            o_ref[:, k * C:(k + 1) * C] = g_ref[k].astype(dtype)

    return pl.pallas_call(body, out_shape=jax.ShapeDtypeStruct((R, n * C), dtype), compiler_params=_cparams(None, 32 << 20), name=name)(g)


def _split_cols(x, n, name):
    R, NC = x.shape
    C = NC // n

    def body(x_ref, o_ref):
        for k in range(n):
            o_ref[k] = x_ref[:, k * C:(k + 1) * C]

    return pl.pallas_call(body, out_shape=jax.ShapeDtypeStruct((n, R, C), x.dtype), compiler_params=_cparams(None, 32 << 20), name=name)(x)


def _rms_scale(xf):
    return lax.rsqrt(jnp.mean(xf * xf, axis=-1, keepdims=True) + RMS_EPS)


def _prenorm(x, g, ts):
    S = x.shape[0]

    def body(x_ref, g_ref, h_ref, ht_ref):
        xf = x_ref[...]
        h = xf * _rms_scale(xf) * g_ref[...]
        h_ref[...] = h.astype(BF16)
        ht_ref[...] = jnp.transpose(h).astype(BF16)

    return pl.pallas_call(
        body,
        grid=(S // ts,),
        in_specs=[pl.BlockSpec((ts, D), lambda i: (i, 0)), pl.BlockSpec((1, D), lambda i: (0, 0))],
        out_specs=[pl.BlockSpec((ts, D), lambda i: (i, 0)), pl.BlockSpec((D, ts), lambda i: (0, i))],
        out_shape=[jax.ShapeDtypeStruct((S, D), BF16), jax.ShapeDtypeStruct((D, S), BF16)],
        compiler_params=_cparams(("parallel",), 32 << 20),
        name="prenorm",
    )(x, g)


def _inv_freq_tile():
    inv_freq = ROPE_THETA ** (-jnp.arange(0, DR, 2, dtype=F32) / DR)
    return jnp.tile(inv_freq, 4).reshape(1, 128)


def _rope_tables(pos_ref, freq_ref, ts):
    lane = lax.broadcasted_iota(jnp.int32, (ts, 128), 1)
    ang = pos_ref[...].astype(F32) * freq_ref[...]
    return jnp.cos(ang), jnp.sin(ang), lane


def _rope_swap(t, lane):
    return jnp.where(lane < 32, pltpu.roll(t, 96, 1), pltpu.roll(t, 32, 1))


def _qkv_prep(proj, pos, freq, g_qa, g_kva, w_q, w_kv, ts):
    S = proj.shape[0]

    def body(qa_ref, ckv_ref, kr_ref, pos_ref, freq_ref, gq_ref, gk_ref, wq_ref, wkv_ref, q_ref, k_ref, v_ref, vt_ref, qn_ref, kvn_ref):
        qa = _f32(qa_ref)
        qn = (qa * _rms_scale(qa) * gq_ref[...]).astype(BF16)
        ckv = _f32(ckv_ref)
        kvn = (ckv * _rms_scale(ckv) * gk_ref[...]).astype(BF16)
        qn_ref[...] = qn
        kvn_ref[...] = kvn
        cos, sin, lane = _rope_tables(pos_ref, freq_ref, ts)
        sgn_sin = jnp.where(lane < 32, -sin, sin)
        live = lane < DR
        kr = _f32(kr_ref)
        kr = jnp.where(live, kr * cos + _rope_swap(kr, lane) * sgn_sin, 0.0).astype(BF16)
        qf = jnp.dot(qn, wq_ref[...], preferred_element_type=F32)
        kvf = jnp.dot(kvn, wkv_ref[...], preferred_element_type=F32)
        for h in range(H):
            q_ref[h, :, 0:DN] = qf[:, h * HW:h * HW + DN].astype(BF16)
            t = qf[:, h * HW + DN:(h + 1) * HW]
            q_ref[h, :, DN:HW] = jnp.where(live, t * cos + _rope_swap(t, lane) * sgn_sin, 0.0).astype(BF16)
            k_ref[h, :, 0:DN] = kvf[:, h * HW:h * HW + DN].astype(BF16)
            k_ref[h, :, DN:HW] = kr
            vh = kvf[:, h * HW + DN:(h + 1) * HW]
            v_ref[h] = vh.astype(BF16)
            vt_ref[h] = jnp.transpose(vh).astype(BF16)

    row = lambda w, blk: pl.BlockSpec((ts, w), lambda i: (i, blk))
    full = lambda a: pl.BlockSpec(a.shape, lambda i: (0,) * a.ndim)
    return pl.pallas_call(
        body,
        grid=(S // ts,),
        in_specs=[row(RQ, OFF_QA // RQ), row(RKV, OFF_CKV // RKV), row(128, OFF_KR // 128),
                  pl.BlockSpec((ts, 1), lambda i: (i, 0)), full(freq), full(g_qa), full(g_kva), full(w_q), full(w_kv)],
        out_specs=[pl.BlockSpec((H, ts, HW), lambda i: (0, i, 0)), pl.BlockSpec((H, ts, HW), lambda i: (0, i, 0)),
                   pl.BlockSpec((H, ts, DV), lambda i: (0, i, 0)), pl.BlockSpec((H, DV, ts), lambda i: (0, 0, i)),
                   pl.BlockSpec((ts, RQ), lambda i: (i, 0)), pl.BlockSpec((ts, RKV), lambda i: (i, 0))],
        out_shape=[jax.ShapeDtypeStruct((H, S, HW), BF16), jax.ShapeDtypeStruct((H, S, HW), BF16),
                   jax.ShapeDtypeStruct((H, S, DV), BF16), jax.ShapeDtypeStruct((H, DV, S), BF16),
                   jax.ShapeDtypeStruct((S, RQ), BF16), jax.ShapeDtypeStruct((S, RKV), BF16)],
        compiler_params=_cparams(("parallel",), 48 << 20),
        name="qkv_prep",
    )(proj, proj, proj, pos, freq, g_qa, g_kva, w_q, w_kv)


def _col_to_row8(col, n):
    return jnp.transpose(jnp.broadcast_to(col, (n, 128)))[0:8, :]


def _causal_pairs(n, key_major):
    if key_major:
        pairs = [(i, j) for j in range(n) for i in range(j, n)]
    else:
        pairs = [(i, j) for i in range(n) for j in range(i + 1)]
    return jnp.array([p[0] for p in pairs], jnp.int32), jnp.array([p[1] for p in pairs], jnp.int32)


def _flash_fwd(q, k, vt, tq, hb=8):
    S = q.shape[1]
    nq = S // tq
    scale = 1.0 / math.sqrt(DN + DR)
    c2 = scale * math.log2(math.e)
    tsub = tq
    nsub = tq // tsub

    qi_tab, kj_tab = _causal_pairs(nq, key_major=False)

    def body(qi_ref, kj_ref, q_ref, k_ref, vt_ref, o_ref, lse_ref, m_sc, l_sc, acc_sc):
        p = pl.program_id(1)
        qi, kj = qi_ref[p], kj_ref[p]

        @pl.when(kj == 0)
        def _():
            m_sc[...] = jnp.full_like(m_sc, -jnp.inf)
            l_sc[...] = jnp.zeros_like(l_sc)
            acc_sc[...] = jnp.zeros_like(acc_sc)

        def step(diag):
            chains = [(h, u) for h in range(hb) for u in range(nsub)]

            def scores(h, u):
                return lax.dot_general(k_ref[h], q_ref[h, u * tsub:(u + 1) * tsub, :], NT, preferred_element_type=F32)

            st_next = scores(*chains[0])
            for ci, (h, u) in enumerate(chains):
                st = st_next
                if ci + 1 < len(chains):
                    st_next = scores(*chains[ci + 1])
                cols = slice(u * tsub, (u + 1) * tsub)
                if diag:
                    r = lax.broadcasted_iota(jnp.int32, (tq, tsub), 0)
                    c = lax.broadcasted_iota(jnp.int32, (tq, tsub), 1) + u * tsub
                    st = jnp.where(r <= c, st, -jnp.inf)
                m_prev = m_sc[h, 0:1, cols]
                m_new = jnp.maximum(m_prev, jnp.max(st, axis=0, keepdims=True))
                alpha = jnp.exp2((m_prev - m_new) * c2)
                pt = jnp.exp2((st - m_new) * c2)
                l_sc[h, :, cols] = jnp.broadcast_to(alpha * l_sc[h, 0:1, cols] + jnp.sum(pt, axis=0, keepdims=True), (8, tsub))
                m_sc[h, :, cols] = jnp.broadcast_to(m_new, (8, tsub))
                acc_sc[h, :, cols] = alpha * acc_sc[h, :, cols] + jnp.dot(vt_ref[h], pt.astype(BF16), preferred_element_type=F32)

        @pl.when(kj < qi)
        def _():
            step(False)

        @pl.when(kj == qi)
        def _():
            step(True)
            for h in range(hb):
                l = l_sc[h, 0:1, :]
                o_ref[:, h * DV:(h + 1) * DV] = jnp.transpose(acc_sc[h] / l).astype(BF16)
                lse_ref[h] = m_sc[h] * scale + jnp.log(l_sc[h])

    return pl.pallas_call(
        body,
        grid_spec=pltpu.PrefetchScalarGridSpec(
            num_scalar_prefetch=2,
            grid=(H // hb, len(qi_tab)),
            in_specs=[pl.BlockSpec((hb, tq, HW), lambda h, p, qi, kj: (h, qi[p], 0)),
                      pl.BlockSpec((hb, tq, HW), lambda h, p, qi, kj: (h, kj[p], 0)),
                      pl.BlockSpec((hb, DV, tq), lambda h, p, qi, kj: (h, 0, kj[p]))],
            out_specs=[pl.BlockSpec((tq, hb * DV), lambda h, p, qi, kj: (qi[p], h)),
                       pl.BlockSpec((hb, 8, tq), lambda h, p, qi, kj: (h, 0, qi[p]))],
            scratch_shapes=[pltpu.VMEM((hb, 8, tq), F32), pltpu.VMEM((hb, 8, tq), F32), pltpu.VMEM((hb, DV, tq), F32)],
        ),
        out_shape=[jax.ShapeDtypeStruct((S, H * DV), BF16), jax.ShapeDtypeStruct((H, 8, S), F32)],
        compiler_params=_cparams(("parallel", "arbitrary"), 40 << 20),
        name="flash_fwd",
    )(qi_tab, kj_tab, q, k, vt)


def _sigmoid(x):
    return 1.0 / (1.0 + jnp.exp(-x))


HALO = 16


def _f32(ref):
    return ref[...].astype(F32)


def _shift_rows(u, prev, n, first):
    ts = u.shape[0]
    row = lax.broadcasted_iota(jnp.int32, u.shape, 0)
    out = pltpu.roll(u, n, 0)
    for j in range(n):
        halo = jnp.where(first, 0.0, prev[HALO - n + j:HALO - n + j + 1, :])
        out = jnp.where(row == j, halo, out)
    return out


def _gates_fwd(proj, attn, conv_w, ts):
    S = proj.shape[0]

    def body(attn_ref, zm_ref, cin_ref, bg_ref, cg_ref, zc_ref, cin_p, cg_p, w_ref, am_ref, ac_ref):
        first = pl.program_id(0) == 0
        zm = _f32(zm_ref)
        am_ref[...] = (_f32(attn_ref) * (zm * _sigmoid(zm))).astype(BF16)
        u = _f32(cg_ref) * _f32(cin_ref)
        up = _f32(cg_p) * _f32(cin_p)
        w = w_ref[...]
        conv = w[0:1, :] * _shift_rows(u, up, 2, first) + w[1:2, :] * _shift_rows(u, up, 1, first) + w[2:3, :] * u
        zc = _f32(zc_ref)
        ac_ref[...] = (_f32(bg_ref) * conv * (zc * _sigmoid(zc))).astype(BF16)

    seg = lambda name: pl.BlockSpec((ts, D), lambda i: (i, SEG[name]))
    prev = lambda name: pl.BlockSpec((HALO, D), lambda i: (jnp.maximum(i * (ts // HALO) - 1, 0), SEG[name]))
    return pl.pallas_call(
        body,
        grid=(S // ts,),
        in_specs=[pl.BlockSpec((ts, D), lambda i: (i, 0)), seg("z_mla"), seg("c_in"), seg("b_gate"), seg("c_gate"), seg("z_conv"),
                  prev("c_in"), prev("c_gate"), pl.BlockSpec((8, D), lambda i: (0, 0))],
        out_specs=[pl.BlockSpec((ts, D), lambda i: (i, 0))] * 2,
        out_shape=[jax.ShapeDtypeStruct((S, D), BF16)] * 2,
        compiler_params=_cparams(("arbitrary",), 48 << 20),
        name="gates_fwd",
    )(attn, proj, proj, proj, proj, proj, proj, proj, conv_w)


def _merge_fwd(proj, y_mla, y_conv, ts):
    S = proj.shape[0]

    def body(gm_ref, gc_ref, ym_ref, yc_ref, o_ref):
        o_ref[...] = (_sigmoid(_f32(gm_ref)) * _f32(ym_ref) + _sigmoid(_f32(gc_ref)) * _f32(yc_ref)).astype(BF16)

    seg = lambda name: pl.BlockSpec((ts, D), lambda i: (i, SEG[name]))
    row = pl.BlockSpec((ts, D), lambda i: (i, 0))
    return pl.pallas_call(
        body, grid=(S // ts,), in_specs=[seg("g_mla"), seg("g_conv"), row, row], out_specs=row,
        out_shape=jax.ShapeDtypeStruct((S, D), BF16), compiler_params=_cparams(("parallel",), 32 << 20), name="merge_fwd",
    )(proj, proj, y_mla, y_conv)


def _post_loss(out, x, target, g_post, ts):
    S = out.shape[0]

    def body(o_ref, x_ref, t_ref, g_ref, dy_ref, do_ref, dg_ref, loss_ref):
        i = pl.program_id(0)
        o = o_ref[...]
        r = _rms_scale(o)
        n = o * r
        g = g_ref[...]
        err = x_ref[...] + n * g - t_ref[...]
        dy = err * (1.0 / D)
        dy_ref[...] = dy
        dn = dy * g
        do_ref[...] = (r * (dn - n * jnp.mean(dn * n, axis=-1, keepdims=True))).astype(BF16)
        dg = jnp.sum(dy * n, axis=0, keepdims=True)
        part = jnp.sum(jnp.sum(err * err, axis=0, keepdims=True), axis=1, keepdims=True) * (0.5 / D)

        @pl.when(i == 0)
        def _():
            dg_ref[...] = jnp.zeros_like(dg_ref)
            loss_ref[...] = jnp.zeros_like(loss_ref)

        dg_ref[0:1, :] += dg
        loss_ref[...] += jnp.broadcast_to(part, loss_ref.shape)

    row = pl.BlockSpec((ts, D), lambda i: (i, 0))
    return pl.pallas_call(
        body, grid=(S // ts,),
        in_specs=[row, row, row, pl.BlockSpec((1, D), lambda i: (0, 0))],
        out_specs=[row, row, pl.BlockSpec((8, D), lambda i: (0, 0)), pl.BlockSpec((8, 128), lambda i: (0, 0))],
        out_shape=[jax.ShapeDtypeStruct((S, D), F32), jax.ShapeDtypeStruct((S, D), BF16),
                   jax.ShapeDtypeStruct((8, D), F32), jax.ShapeDtypeStruct((8, 128), F32)],
        compiler_params=_cparams(("arbitrary",), 40 << 20), name="post_loss",
    )(out, x, target, g_post)


def _merge_bwd(proj, y_mla, y_conv, dmerged, ts):
    S = proj.shape[0]

    def body(gm_ref, gc_ref, ym_ref, yc_ref, dm_ref, dym_ref, dyc_ref, dgm_ref, dgc_ref):
        dm = _f32(dm_ref)
        sm = _sigmoid(_f32(gm_ref))
        sc = _sigmoid(_f32(gc_ref))
        dym_ref[...] = (dm * sm).astype(BF16)
        dyc_ref[...] = (dm * sc).astype(BF16)
        dgm_ref[...] = (dm * _f32(ym_ref) * (sm * (1.0 - sm))).astype(BF16)
        dgc_ref[...] = (dm * _f32(yc_ref) * (sc * (1.0 - sc))).astype(BF16)

    seg = lambda name: pl.BlockSpec((ts, D), lambda i: (i, SEG[name]))
    row = pl.BlockSpec((ts, D), lambda i: (i, 0))
    return pl.pallas_call(
        body, grid=(S // ts,), in_specs=[seg("g_mla"), seg("g_conv"), row, row, row], out_specs=[row] * 4,
        out_shape=[jax.ShapeDtypeStruct((S, D), BF16)] * 4, compiler_params=_cparams(("parallel",), 40 << 20), name="merge_bwd",
    )(proj, proj, y_mla, y_conv, dmerged)


def _gates_bwd(proj, attn, da_mla, da_conv, conv_w, ts):
    S = proj.shape[0]
    nblk = S // ts

    def body(attn_ref, zm_ref, cin_ref, bg_ref, cg_ref, zc_ref, dam_ref, dac_ref, cin_p, cg_p, bg_n, zc_n, dac_n, w_ref,
             dattn_ref, delta_ref, dzm_ref, dcin_ref, dbg_ref, dcg_ref, dzc_ref, dw_ref):
        i = pl.program_id(0)
        first = i == 0
        last = i == nblk - 1

        @pl.when(first)
        def _():
            dw_ref[...] = jnp.zeros_like(dw_ref)

        row = lax.broadcasted_iota(jnp.int32, (ts, DV), 0)
        for c in range(D // DV):
            cols = slice(c * DV, (c + 1) * DV)
            ld = lambda ref: ref[:, cols].astype(F32)
            zm = ld(zm_ref)
            sg = _sigmoid(zm)
            attn = ld(attn_ref)
            dam = ld(dam_ref)
            dattn = dam * (zm * sg)
            dattn_ref[:, cols] = dattn.astype(BF16)
            dzm_ref[:, cols] = (dam * attn * (sg * (1.0 + zm * (1.0 - sg)))).astype(BF16)
            delta_ref[c] = _col_to_row8(jnp.sum(dattn * attn, axis=1, keepdims=True), ts)
            w = w_ref[:, cols]
            cin, cg, bg, zc = ld(cin_ref), ld(cg_ref), ld(bg_ref), ld(zc_ref)
            u = cg * cin
            up = ld(cg_p) * ld(cin_p)
            u1 = _shift_rows(u, up, 1, first)
            u2 = _shift_rows(u, up, 2, first)
            conv = w[0:1, :] * u2 + w[1:2, :] * u1 + w[2:3, :] * u
            sgc = _sigmoid(zc)
            siluc = zc * sgc
            dac = ld(dac_ref)
            dbg_ref[:, cols] = (dac * conv * siluc).astype(BF16)
            dzc_ref[:, cols] = (dac * bg * conv * (sgc * (1.0 + zc * (1.0 - sgc)))).astype(BF16)
            dconv = dac * bg * siluc
            zn = ld(zc_n)
            dconv_n = jnp.where(last, 0.0, ld(dac_n) * ld(bg_n) * (zn * _sigmoid(zn)))
            d1 = jnp.where(row == ts - 1, dconv_n[0:1, :], pltpu.roll(dconv, ts - 1, 0))
            d2 = jnp.where(row == ts - 1, dconv_n[1:2, :], jnp.where(row == ts - 2, dconv_n[0:1, :], pltpu.roll(dconv, ts - 2, 0)))
            du = w[2:3, :] * dconv + w[1:2, :] * d1 + w[0:1, :] * d2
            dcg_ref[:, cols] = (du * cin).astype(BF16)
            dcin_ref[:, cols] = (du * cg).astype(BF16)
            dw_ref[0:1, cols] += jnp.sum(dconv * u2, axis=0, keepdims=True)
            dw_ref[1:2, cols] += jnp.sum(dconv * u1, axis=0, keepdims=True)
            dw_ref[2:3, cols] += jnp.sum(dconv * u, axis=0, keepdims=True)

    seg = lambda name: pl.BlockSpec((ts, D), lambda i: (i, SEG[name]))
    prev = lambda name: pl.BlockSpec((HALO, D), lambda i: (jnp.maximum(i * (ts // HALO) - 1, 0), SEG[name]))
    nxt = lambda blk: pl.BlockSpec((HALO, D), lambda i: (jnp.minimum((i + 1) * (ts // HALO), S // HALO - 1), blk))
    row = pl.BlockSpec((ts, D), lambda i: (i, 0))
    return pl.pallas_call(
        body, grid=(nblk,),
        in_specs=[row, seg("z_mla"), seg("c_in"), seg("b_gate"), seg("c_gate"), seg("z_conv"), row, row,
                  prev("c_in"), prev("c_gate"), nxt(SEG["b_gate"]), nxt(SEG["z_conv"]), nxt(0), pl.BlockSpec((8, D), lambda i: (0, 0))],
        out_specs=[row, pl.BlockSpec((H, 8, ts), lambda i: (0, 0, i)), row, row, row, row, row, pl.BlockSpec((8, D), lambda i: (0, 0))],
        out_shape=[jax.ShapeDtypeStruct((S, D), BF16), jax.ShapeDtypeStruct((H, 8, S), F32)] + [jax.ShapeDtypeStruct((S, D), BF16)] * 5
        + [jax.ShapeDtypeStruct((8, D), F32)],
        compiler_params=_cparams(("arbitrary",), 56 << 20), name="gates_bwd",
    )(attn, proj, proj, proj, proj, proj, da_mla, da_conv, proj, proj, proj, proj, da_conv, conv_w)


def _flash_bwd(q, k, v, do, lse, delta, tq, hb=2):
    S = q.shape[1]
    nq = S // tq
    scale = 1.0 / math.sqrt(DN + DR)
    c2 = scale * LOG2E
    tsub = min(256, tq)
    nsub = tq // tsub

    qi_tab, kj_tab = _causal_pairs(nq, key_major=True)
    npairs = nq * (nq + 1) // 2

    def body(qi_ref, kj_ref, q_ref, k_ref, v_ref, do_ref, lse_ref, dl_ref, dq_ref, dk_ref, dv_ref, dk_sc, dv_sc):
        p = pl.program_id(1)
        qi, kj = qi_ref[p], kj_ref[p]

        @pl.when(p == 0)
        def _():
            dq_ref[...] = jnp.zeros_like(dq_ref)

        @pl.when(qi == kj)
        def _():
            dk_sc[...] = jnp.zeros_like(dk_sc)
            dv_sc[...] = jnp.zeros_like(dv_sc)

        def step(diag):
            chains = [(h, u) for h in range(hb) for u in range(nsub)]

            def first_matmuls(h, u):
                sub = slice(u * tsub, (u + 1) * tsub)
                st = lax.dot_general(k_ref[h], q_ref[h, sub, :], NT, preferred_element_type=F32)
                dpt = lax.dot_general(v_ref[h], do_ref[sub, h * DV:(h + 1) * DV], NT, preferred_element_type=F32)
                return st, dpt

            nxt = first_matmuls(*chains[0])
            for ci, (h, u) in enumerate(chains):
                st, dpt = nxt
                if ci + 1 < len(chains):
                    nxt = first_matmuls(*chains[ci + 1])
                sub = slice(u * tsub, (u + 1) * tsub)
                pt = jnp.exp2(st * c2 - lse_ref[h, 0:1, sub] * LOG2E)
                if diag:
                    r = lax.broadcasted_iota(jnp.int32, (tq, tsub), 0)
                    c = lax.broadcasted_iota(jnp.int32, (tq, tsub), 1) + u * tsub
                    pt = jnp.where(r <= c, pt, 0.0)
                dst = (pt * (dpt - dl_ref[h, 0:1, sub])).astype(BF16)
                dv_sc[h] += jnp.dot(pt.astype(BF16), do_ref[sub, h * DV:(h + 1) * DV], preferred_element_type=F32)
                dk_sc[h] += jnp.dot(dst, q_ref[h, sub, :], preferred_element_type=F32)
                rows = pl.ds(pl.multiple_of(qi * tq + u * tsub, tsub), tsub)
                dq_ref[h, rows, :] += lax.dot_general(dst, k_ref[h], TN, preferred_element_type=F32)

        @pl.when(qi > kj)
        def _():
            step(False)

        @pl.when(qi == kj)
        def _():
            step(True)

        @pl.when(qi == nq - 1)
        def _():
            dk_ref[...] = dk_sc[...] * scale
            dv_ref[...] = dv_sc[...]

        @pl.when(p == npairs - 1)
        def _():
            dq_ref[...] = dq_ref[...] * scale

    return pl.pallas_call(
        body,
        grid_spec=pltpu.PrefetchScalarGridSpec(
            num_scalar_prefetch=2,
            grid=(H // hb, npairs),
            in_specs=[pl.BlockSpec((hb, tq, HW), lambda h, p, qi, kj: (h, qi[p], 0)),
                      pl.BlockSpec((hb, tq, HW), lambda h, p, qi, kj: (h, kj[p], 0)),
                      pl.BlockSpec((hb, tq, DV), lambda h, p, qi, kj: (h, kj[p], 0)),
                      pl.BlockSpec((tq, hb * DV), lambda h, p, qi, kj: (qi[p], h)),
                      pl.BlockSpec((hb, 8, tq), lambda h, p, qi, kj: (h, 0, qi[p])),
                      pl.BlockSpec((hb, 8, tq), lambda h, p, qi, kj: (h, 0, qi[p]))],
            out_specs=[pl.BlockSpec((hb, S, HW), lambda h, p, qi, kj: (h, 0, 0)),
                       pl.BlockSpec((hb, tq, HW), lambda h, p, qi, kj: (h, kj[p], 0)),
                       pl.BlockSpec((hb, tq, DV), lambda h, p, qi, kj: (h, kj[p], 0))],
            scratch_shapes=[pltpu.VMEM((hb, tq, HW), F32), pltpu.VMEM((hb, tq, DV), F32)],
        ),
        out_shape=[jax.ShapeDtypeStruct((H, S, HW), F32), jax.ShapeDtypeStruct((H, S, HW), F32), jax.ShapeDtypeStruct((H, S, DV), F32)],
        compiler_params=_cparams(("parallel", "arbitrary"), VMEM_CAP),
        name="flash_bwd",
    )(qi_tab, kj_tab, q, k, v, do, lse, delta)


def _rms_bwd(xf, g, dn_out):
    r = _rms_scale(xf)
    n = xf * r
    dn = dn_out * g
    dx = r * (dn - n * jnp.mean(dn * n, axis=-1, keepdims=True))
    return dx, jnp.sum(dn_out * n, axis=0, keepdims=True)


def _qkv_bwd(proj, pos, freq, g_qa, g_kva, w_q, w_kv, dq, dk, dv, ts):
    S = proj.shape[0]

    def body(qa_ref, ckv_ref, pos_ref, freq_ref, gq_ref, gk_ref, wq_ref, wkv_ref, dq_ref, dk_ref, dv_ref,
             dqp_ref, dkvp_ref, dqa_ref, dckv_ref, dkr_ref, dgq_ref, dgk_ref):
        i = pl.program_id(0)
        cos, sin, lane = _rope_tables(pos_ref, freq_ref, ts)
        sgn_sin = jnp.where(lane < 32, sin, -sin)
        live = lane < DR
        kr_sum = jnp.zeros((ts, 128), F32)
        for h in range(H):
            dqh = dq_ref[h]
            dqp_ref[:, h * HW:h * HW + DN] = dqh[:, 0:DN].astype(BF16)
            t = dqh[:, DN:HW]
            dqp_ref[:, h * HW + DN:(h + 1) * HW] = jnp.where(live, t * cos + _rope_swap(t, lane) * sgn_sin, 0.0).astype(BF16)
            dkh = dk_ref[h]
            dkvp_ref[:, h * HW:h * HW + DN] = dkh[:, 0:DN].astype(BF16)
            dkvp_ref[:, h * HW + DN:(h + 1) * HW] = dv_ref[h].astype(BF16)
            kr_sum = kr_sum + dkh[:, DN:HW]
        dkr_ref[...] = jnp.where(live, kr_sum * cos + _rope_swap(kr_sum, lane) * sgn_sin, 0.0).astype(BF16)
        dqn = lax.dot_general(dqp_ref[...], wq_ref[...], NT, preferred_element_type=F32)
        dkvn = lax.dot_general(dkvp_ref[...], wkv_ref[...], NT, preferred_element_type=F32)
        dqa, dgq = _rms_bwd(_f32(qa_ref), gq_ref[...], dqn)
        dckv, dgk = _rms_bwd(_f32(ckv_ref), gk_ref[...], dkvn)
        dqa_ref[...] = dqa.astype(BF16)
        dckv_ref[...] = dckv.astype(BF16)

        @pl.when(i == 0)
        def _():
            dgq_ref[...] = jnp.zeros_like(dgq_ref)
            dgk_ref[...] = jnp.zeros_like(dgk_ref)

        dgq_ref[0:1, :] += dgq
        dgk_ref[0:1, :] += dgk

    rowb = lambda w, blk: pl.BlockSpec((ts, w), lambda i: (i, blk))
    full = lambda a: pl.BlockSpec(a.shape, lambda i: (0,) * a.ndim)
    heads = lambda w: pl.BlockSpec((H, ts, w), lambda i: (0, i, 0))
    return pl.pallas_call(
        body, grid=(S // ts,),
        in_specs=[rowb(RQ, OFF_QA // RQ), rowb(RKV, OFF_CKV // RKV), pl.BlockSpec((ts, 1), lambda i: (i, 0)), full(freq), full(g_qa), full(g_kva),
                  full(w_q), full(w_kv), heads(HW), heads(HW), heads(DV)],
        out_specs=[rowb(H * HW, 0), rowb(H * HW, 0), rowb(RQ, 0), rowb(RKV, 0), rowb(128, 0),
                   pl.BlockSpec((8, RQ), lambda i: (0, 0)), pl.BlockSpec((8, RKV), lambda i: (0, 0))],
        out_shape=[jax.ShapeDtypeStruct((S, H * HW), BF16), jax.ShapeDtypeStruct((S, H * HW), BF16), jax.ShapeDtypeStruct((S, RQ), BF16),
                   jax.ShapeDtypeStruct((S, RKV), BF16), jax.ShapeDtypeStruct((S, 128), BF16),
                   jax.ShapeDtypeStruct((8, RQ), F32), jax.ShapeDtypeStruct((8, RKV), F32)],
        compiler_params=_cparams(("arbitrary",), 56 << 20), name="qkv_bwd",
    )(proj, proj, pos, freq, g_qa, g_kva, w_q, w_kv, dq, dk, dv)


def _pack_dproj(parts, ts):
    S = parts[0].shape[0]
    widths = [p.shape[1] for p in parts]
    assert sum(widths) == NP

    def body(*refs):
        o_ref = refs[-1]
        off = 0
        for r, w in zip(refs[:-1], widths):
            o_ref[:, off:off + w] = r[...]
            off += w

    return pl.pallas_call(
        body, grid=(S // ts,),
        in_specs=[pl.BlockSpec((ts, w), lambda i: (i, 0)) for w in widths],
        out_specs=pl.BlockSpec((ts, NP), lambda i: (i, 0)),
        out_shape=jax.ShapeDtypeStruct((S, NP), BF16),
        compiler_params=_cparams(("parallel",), 48 << 20), name="pack_dproj",
    )(*parts)


def _prenorm_bwd(x, g, dh, dy, ts):
    S = x.shape[0]

    def body(x_ref, g_ref, dh_ref, dy_ref, gx_ref, dg_ref):
        dx, dg = _rms_bwd(x_ref[...], g_ref[...], dh_ref[...])
        gx_ref[...] = dy_ref[...] + dx

        @pl.when(pl.program_id(0) == 0)
        def _():
            dg_ref[...] = jnp.zeros_like(dg_ref)

        dg_ref[0:1, :] += dg

    row = pl.BlockSpec((ts, D), lambda i: (i, 0))
    return pl.pallas_call(
        body, grid=(S // ts,), in_specs=[row, pl.BlockSpec((1, D), lambda i: (0, 0)), row, row],
        out_specs=[row, pl.BlockSpec((8, D), lambda i: (0, 0))],
        out_shape=[jax.ShapeDtypeStruct((S, D), F32), jax.ShapeDtypeStruct((8, D), F32)],
        compiler_params=_cparams(("arbitrary",), 40 << 20), name="prenorm_bwd",
    )(x, g, dh, dy)


def _local_step(x, pos, target, g_pre, g_qa, g_kva, g_post, w_in, other_weights, reduce_grads=None):
    S = x.shape[0]
    ts = min(256, S)
    tq = min(512, S)
    tm = min(512, S)
    mm = functools.partial(_matmul, tm=tm)
    freq = _inv_freq_tile()

    h, ht = _prenorm(x, g_pre, ts)
    if len(other_weights) == 2:
        gather, assemble = other_weights
        proj, gathered = _matmul(h, w_in, mode="nn", out_dtype=BF16, tm=1024, tn=1408, tk=D, name="mm_proj", exchange=gather)
        other_weights = assemble(*gathered)
    else:
        proj = _matmul(h, w_in, mode="nn", out_dtype=BF16, tm=1024, tn=1408, tk=D, name="mm_proj")
    w_q, w_kv, conv_w8, w_o_mla, w_o_conv, w_out = other_weights
    q, k, v, vt, qn, kvn = _qkv_prep(proj, pos, freq, g_qa, g_kva, w_q, w_kv, ts)
    attn, lse = _flash_fwd(q, k, vt, tq)
    a_mla, a_conv = _gates_fwd(proj, attn, conv_w8, ts)
    y_mla = mm(a_mla, w_o_mla, mode="nn", out_dtype=BF16, tn=1024, tk=D, name="mm_y_mla")
    y_conv = mm(a_conv, w_o_conv, mode="nn", out_dtype=BF16, tn=1024, tk=D, name="mm_y_conv")
    merged = _merge_fwd(proj, y_mla, y_conv, ts)
    out = mm(merged, w_out, mode="nn", out_dtype=F32, tn=1024, tk=D, name="mm_out")
    dy, dout, dg_post, loss = _post_loss(out, x, target, g_post, ts)

    dmerged = mm(dout, w_out, mode="nt", out_dtype=BF16, tn=1024, tk=D, name="mm_dmerged")
    dw_out = _matmul(merged, dout, mode="tn", out_dtype=F32, tm=1024, tn=1024, tk=2048, name="mm_dw_out")
    dy_mla, dy_conv, dg_mla, dg_conv = _merge_bwd(proj, y_mla, y_conv, dmerged, ts)
    da_mla = mm(dy_mla, w_o_mla, mode="nt", out_dtype=BF16, tn=1024, tk=D, name="mm_da_mla")
    da_conv = mm(dy_conv, w_o_conv, mode="nt", out_dtype=BF16, tn=1024, tk=D, name="mm_da_conv")
    dw_o_mla = _matmul(a_mla, dy_mla, mode="tn", out_dtype=F32, tm=1024, tn=1024, tk=2048, name="mm_dw_o_mla")
    dw_o_conv = _matmul(a_conv, dy_conv, mode="tn", out_dtype=F32, tm=1024, tn=1024, tk=2048, name="mm_dw_o_conv")
    dattn, delta, dz_mla, dc_in, db_gate, dc_gate, dz_conv, dconv_w = _gates_bwd(proj, attn, da_mla, da_conv, conv_w8, min(128, S))
    dq, dk, dv = _flash_bwd(q, k, v, dattn, lse, delta, tq)
    dqp, dkvp, dq_a, dc_kv, dk_rope, dg_qa, dg_kva = _qkv_bwd(proj, pos, freq, g_qa, g_kva, w_q, w_kv, dq, dk, dv, min(128, S))
    dw_q = _matmul(qn, dqp, mode="tn", out_dtype=F32, tm=RQ, tn=1024, tk=2048, name="mm_dw_q")
    dw_kv = _matmul(kvn, dkvp, mode="tn", out_dtype=F32, tm=RKV, tn=1024, tk=2048, name="mm_dw_kv")
    dproj = _pack_dproj([dz_mla, dc_in, db_gate, dc_gate, dz_conv, dg_mla, dg_conv, dq_a, dc_kv, dk_rope], ts)
    dw_in = _matmul(ht, dproj, mode="nn", out_dtype=F32, tm=1024, tn=1408, tk=2048, name="mm_dw_in")
    res = dict(dw_in=dw_in, dw_q=dw_q, dw_kv=dw_kv, dconv_w=dconv_w, dw_o_mla=dw_o_mla, dw_o_conv=dw_o_conv, dw_out=dw_out)
    if reduce_grads is None:
        dh = _matmul(dproj, w_in, mode="nt", out_dtype=F32, tm=1024, tn=D, tk=1408, name="mm_dh")
    else:
        exchange, finish = reduce_grads(res)
        dh, got = _matmul(dproj, w_in, mode="nt", out_dtype=F32, tm=1024, tn=D, tk=1408, name="mm_dh", exchange=exchange)
        res["reduced"] = finish(got)
    grad_x, dg_pre = _prenorm_bwd(x, g_pre, dh, dy, ts)
    res.update(loss=loss, grad_x=grad_x, dg_pre=dg_pre, dg_qa=dg_qa, dg_kva=dg_kva, dg_post=dg_post)
    return res


def _my_id():
    return lax.axis_index("x") * 4 + lax.axis_index("y") * 2 + lax.axis_index("c")


def _place():
    x, y, c = lax.axis_index("x"), lax.axis_index("y"), lax.axis_index("c")
    return (x, y, c), (x, y, 1 - c), [(1 - x, y), (x, 1 - y), (1 - x, 1 - y)]


def _slot(px, py, pc):
    return 4 * px + 2 * py + pc


def _all_gather(arrays, name):
    n = len(arrays)

    def body(*refs):
        ins, outs = refs[:n], refs[n:2 * n]
        send_sems, recv_sems, local_sems = refs[2 * n:]
        me, sib, chips = _place()
        c = me[2]

        def copy(a, k, block, to, src=None):
            rows = outs[a].at[_slot(*block)]
            return pltpu.make_async_remote_copy(src_ref=rows if src is None else src, dst_ref=rows, send_sem=send_sems.at[a, k],
                                                recv_sem=recv_sems.at[a, k], device_id=to, device_id_type=MESH)

        local = [pltpu.make_async_copy(ins[a], outs[a].at[_slot(*me)], local_sems.at[a]) for a in range(n)]
        for cp in local:
            cp.start()
        sends = []
        for a in range(n):
            sends.append(copy(a, 0, me, sib, src=ins[a]))
            sends += [copy(a, 1 + j, me, (*chip, c), src=ins[a]) for j, chip in enumerate(chips)]
        for cp in sends:
            cp.start()
        for a in range(n):
            for j, chip in enumerate(chips):
                copy(a, 1 + j, (*chip, c), me).wait_recv()
                fwd = copy(a, 4 + j, (*chip, c), sib)
                fwd.start()
                sends.append(fwd)
        for a in range(n):
            copy(a, 0, sib, me).wait_recv()
            for j, chip in enumerate(chips):
                copy(a, 4 + j, (*chip, 1 - c), me).wait_recv()
        for cp in sends:
            cp.wait_send()
        for cp in local:
            cp.wait()

    anyspec = pl.BlockSpec(memory_space=pl.ANY)
    return pl.pallas_call(
        body,
        in_specs=[anyspec] * n,
        out_specs=[anyspec] * n,
        out_shape=[jax.ShapeDtypeStruct((NDEV,) + a.shape, a.dtype) for a in arrays],
        scratch_shapes=[pltpu.SemaphoreType.DMA((n, NDEV - 1)), pltpu.SemaphoreType.DMA((n, NDEV - 1)), pltpu.SemaphoreType.DMA((n,))],
        name=name,
    )(*arrays)


def _slab_exchange(arrays, nslots, pick):
    n = len(arrays)

    def copies(ins, outs, sems):
        send_sems, recv_sems = sems
        return [pltpu.make_async_remote_copy(src_ref=ins[a].at[pick(k)[0]], dst_ref=outs[a].at[k], send_sem=send_sems.at[a, k],
                                             recv_sem=recv_sems.at[a, k], device_id=pick(k)[1], device_id_type=MESH)
                for a in range(n) for k in range(nslots)]

    def start(ins, outs, sems):
        for cp in copies(ins, outs, sems):
            cp.start()

    def finish(ins, outs, sems):
        cps = copies(ins, outs, sems)
        for cp in cps:
            cp.wait_recv()
        for cp in cps:
            cp.wait_send()

    return _Exchange(arrays, [jax.ShapeDtypeStruct((nslots,) + a.shape[1:], a.dtype) for a in arrays],
                     [pltpu.SemaphoreType.DMA((n, nslots)), pltpu.SemaphoreType.DMA((n, nslots))], start, finish)


def _direct_gather(arrays):
    n = len(arrays)

    def peer(d):
        p = (_my_id() + d) % NDEV
        return (p // 4, (p // 2) % 2, p % 2), p

    def copies(ins, outs, sems, receiving):
        send_sems, recv_sems, _ = sems
        slot = lambda d: peer(NDEV - d)[1] if receiving else _my_id()
        return [pltpu.make_async_remote_copy(src_ref=ins[a], dst_ref=outs[a].at[slot(d)], send_sem=send_sems.at[a, d - 1],
                                             recv_sem=recv_sems.at[a, d - 1], device_id=peer(d)[0], device_id_type=MESH)
                for d in range(1, NDEV) for a in range(n)]

    def local(ins, outs, sems):
        return [pltpu.make_async_copy(ins[a], outs[a].at[_my_id()], sems[2].at[a]) for a in range(n)]

    def start(ins, outs, sems):
        for cp in local(ins, outs, sems) + copies(ins, outs, sems, False):
            cp.start()

    def finish(ins, outs, sems):
        for cp in copies(ins, outs, sems, True):
            cp.wait_recv()
        for cp in copies(ins, outs, sems, False):
            cp.wait_send()
        for cp in local(ins, outs, sems):
            cp.wait()

    return _Exchange(arrays, [jax.ShapeDtypeStruct((NDEV,) + a.shape, a.dtype) for a in arrays],
                     [pltpu.SemaphoreType.DMA((n, NDEV - 1)), pltpu.SemaphoreType.DMA((n, NDEV - 1)), pltpu.SemaphoreType.DMA((n,))],
                     start, finish)


def _run_exchange(exchange, name):
    n = len(exchange.arrays)

    def body(*refs):
        ins, outs, sems = refs[:n], refs[n:2 * n], refs[2 * n:]
        exchange.start(ins, outs, sems)
        exchange.finish(ins, outs, sems)

    anyspec = pl.BlockSpec(memory_space=pl.ANY)
    return pl.pallas_call(body, in_specs=[anyspec] * n, out_specs=[anyspec] * n, out_shape=exchange.out_shapes,
                          scratch_shapes=exchange.sem_shapes, name=name)(*exchange.arrays)


def _slabs_bf16(x, rows_per_block, name):
    n, R, C = x.shape
    tr = min(rows_per_block, R)

    def body(x_ref, o_ref):
        o_ref[...] = x_ref[...].astype(BF16)

    blk = pl.BlockSpec((1, tr, C), lambda k, i: (k, i, 0))
    return pl.pallas_call(body, grid=(n, R // tr), in_specs=[blk], out_specs=blk, out_shape=jax.ShapeDtypeStruct(x.shape, BF16),
                          compiler_params=_cparams(("parallel", "parallel"), 32 << 20), name=name)(x)


def _to_sibling(k):
    me, sib, chips = _place()
    dest = sib if k == 0 else (*chips[k - 1], sib[2])
    return _slot(*dest), sib


def _to_chips(k):
    me, sib, chips = _place()
    return k, (*chips[k], me[2])


def _chip_sums(own, got, slots, rows_per_block, name):
    _, R, C = own.shape
    tr = min(rows_per_block, R)

    def body(slots_ref, own_ref, got_ref, o_ref):
        o_ref[0] = (own_ref[0] + got_ref[0].astype(F32)).astype(BF16)

    return pl.pallas_call(
        body,
        grid_spec=pltpu.PrefetchScalarGridSpec(
            num_scalar_prefetch=1, grid=(3, R // tr),
            in_specs=[pl.BlockSpec((1, tr, C), lambda j, i, s: (s[1 + j], i, 0)), pl.BlockSpec((1, tr, C), lambda j, i, s: (1 + j, i, 0))],
            out_specs=pl.BlockSpec((1, tr, C), lambda j, i, s: (j, i, 0))),
        out_shape=jax.ShapeDtypeStruct((3, R, C), BF16),
        compiler_params=_cparams(("parallel", "parallel"), 32 << 20), name=name,
    )(slots, own, got)


def _adamw_math(w, g, m, v):
    m = ADAM_B1 * m + (1.0 - ADAM_B1) * g
    v = ADAM_B2 * v + (1.0 - ADAM_B2) * (g * g)
    m_hat = m / (1.0 - ADAM_B1 ** ADAM_STEP)
    v_hat = v / (1.0 - ADAM_B2 ** ADAM_STEP)
    delta = -ADAM_LR * (m_hat / (jnp.sqrt(v_hat) + ADAM_EPS) + ADAM_WD * w)
    return delta, m, v


def _reduce_adamw(own, got1, got2, slots, w, m, v, rows_per_block, name):
    _, R, C = own.shape
    tr = min(rows_per_block, R)
    assert R % tr == 0

    def body(slots_ref, own_ref, g1_ref, g2_ref, w_ref, m_ref, v_ref, g_ref, d_ref, nm_ref, nv_ref):
        g = own_ref[0] + g1_ref[0].astype(F32)
        for j in range(3):
            g = g + g2_ref[j].astype(F32)
        g_ref[...] = g
        d, nm, nv = _adamw_math(w_ref[...], g, m_ref[...], v_ref[...])
        d_ref[...] = d
        nm_ref[...] = nm
        nv_ref[...] = nv

    blk = pl.BlockSpec((tr, C), lambda i, s: (i, 0))
    return pl.pallas_call(
        body,
        grid_spec=pltpu.PrefetchScalarGridSpec(
            num_scalar_prefetch=1, grid=(R // tr,),
            in_specs=[pl.BlockSpec((1, tr, C), lambda i, s: (s[0], i, 0)), pl.BlockSpec((1, tr, C), lambda i, s: (0, i, 0)),
                      pl.BlockSpec((3, tr, C), lambda i, s: (0, i, 0)), blk, blk, blk],
            out_specs=[blk] * 4),
        out_shape=[jax.ShapeDtypeStruct((R, C), F32)] * 4,
        compiler_params=_cparams(("parallel",), 48 << 20), name=name,
    )(slots, own, got1, got2, w, m, v)


def _reduce_adamw_t(own, got1, got2, slots, wt, mt, vt, rows_per_block, name):
    _, R, C = own.shape
    tr = min(rows_per_block, R)
    assert R % tr == 0

    def body(slots_ref, own_ref, g1_ref, g2_ref, w_ref, m_ref, v_ref, g_ref, d_ref, nm_ref, nv_ref):
        g = own_ref[0] + g1_ref[0].astype(F32)
        for j in range(3):
            g = g + g2_ref[j].astype(F32)
        g = jnp.transpose(g)
        g_ref[...] = g
        d, nm, nv = _adamw_math(w_ref[...], g, m_ref[...], v_ref[...])
        d_ref[...] = d
        nm_ref[...] = nm
        nv_ref[...] = nv

    blk = pl.BlockSpec((C, tr), lambda i, s: (0, i))
    return pl.pallas_call(
        body,
        grid_spec=pltpu.PrefetchScalarGridSpec(
            num_scalar_prefetch=1, grid=(R // tr,),
            in_specs=[pl.BlockSpec((1, tr, C), lambda i, s: (s[0], i, 0)), pl.BlockSpec((1, tr, C), lambda i, s: (0, i, 0)),
                      pl.BlockSpec((3, tr, C), lambda i, s: (0, i, 0)), blk, blk, blk],
            out_specs=[blk] * 4),
        out_shape=[jax.ShapeDtypeStruct((C, R), F32)] * 4,
        compiler_params=_cparams(("parallel",), 48 << 20), name=name,
    )(slots, own, got1, got2, wt, mt, vt)


def _sum_adamw(parts, w, m, v, rows_per_block, name):
    n, R, C = parts.shape
    tr = min(rows_per_block, R)
    assert R % tr == 0

    def body(p_ref, w_ref, m_ref, v_ref, g_ref, d_ref, nm_ref, nv_ref):
        g = p_ref[0]
        for j in range(1, n):
            g = g + p_ref[j]
        g_ref[...] = g
        d, nm, nv = _adamw_math(w_ref[...], g, m_ref[...], v_ref[...])
        d_ref[...] = d
        nm_ref[...] = nm
        nv_ref[...] = nv

    blk = pl.BlockSpec((tr, C), lambda i: (i, 0))
    return pl.pallas_call(
        body, grid=(R // tr,),
        in_specs=[pl.BlockSpec((n, tr, C), lambda i: (0, i, 0)), blk, blk, blk],
        out_specs=[blk] * 4,
        out_shape=[jax.ShapeDtypeStruct((R, C), F32)] * 4,
        compiler_params=_cparams(("parallel",), 48 << 20), name=name,
    )(parts, w, m, v)


def kernel(x, positions, pre_norm_g, w_in, q_a_norm_g, w_q_b, kv_a_norm_g, w_kv_b, conv_w, w_o_mla, w_o_conv, w_out, post_norm_g, loss_target, m_pre_norm_g, m_w_in, m_q_a_norm_g, m_w_q_b, m_kv_a_norm_g, m_w_kv_b, m_conv_w, m_w_o_mla, m_w_o_conv, m_w_out, m_post_norm_g, v_pre_norm_g, v_w_in, v_q_a_norm_g, v_w_q_b, v_kv_a_norm_g, v_w_kv_b, v_conv_w, v_w_o_mla, v_w_o_conv, v_w_out, v_post_norm_g):
    S = x.shape[1]
    conv_pad = jnp.zeros((8, 256), F32).at[0:3, :].set(conv_w)
    g_in, = _all_gather([w_in.astype(BF16)], "all_gather_w_in")
    w_in_f = _assemble_w_in(g_in)
    gather_rest = _direct_gather([w_q_b.astype(BF16), w_kv_b.astype(BF16), conv_pad, w_o_mla.astype(BF16), w_o_conv.astype(BF16),
                                  w_out.astype(BF16)])

    def assemble_rest(g_q, g_kv, g_cw, g_om, g_oc, g_out):
        return (_assemble_w_q(g_q), _concat_cols(g_kv, BF16, "assemble_w_kv"), _concat_cols(g_cw, F32, "assemble_conv_w"),
                g_om.reshape(D, D), g_oc.reshape(D, D), g_out.reshape(D, D))

    gnames = ["w_in", "w_q", "w_kv", "conv_w", "w_o_mla", "w_o_conv", "w_out"]
    (mx, my, mc), _, chips = _place()
    slots = jnp.stack([_slot(mx, my, mc)] + [_slot(cx, cy, mc) for cx, cy in chips]).astype(jnp.int32)

    def reduce_grads(r):
        own_in, own_in_bf16 = _split_dw_in(r["dw_in"])
        own = [own_in, _split_dw_q(r["dw_q"]), _split_cols(r["dw_kv"], NDEV, "split_dw_kv"),
               _split_cols(r["dconv_w"], NDEV, "split_dconv_w"), r["dw_o_mla"].reshape(NDEV, D // NDEV, D),
               r["dw_o_conv"].reshape(NDEV, D // NDEV, D), r["dw_out"].reshape(NDEV, D // NDEV, D)]
        slabs = [own_in_bf16] + [_slabs_bf16(g, 128, "bf16_" + nm) for g, nm in zip(own[1:], gnames[1:])]
        to_sibling = _slab_exchange(slabs, 4, _to_sibling)
        got1 = _run_exchange(to_sibling, "grads_to_sibling")
        sums = [_chip_sums(o, g1, slots, 128, "chip_sum_" + nm) for o, g1, nm in zip(own, got1, gnames)]
        return _slab_exchange(sums, 3, _to_chips), lambda got2: (own, got1, got2)

    row2 = lambda a: a.reshape(1, -1)
    r = _local_step(x[0], positions.reshape(S, 1), loss_target[0], row2(pre_norm_g), row2(q_a_norm_g), row2(kv_a_norm_g),
                    row2(post_norm_g), w_in_f, (gather_rest, assemble_rest), reduce_grads)
    own, got1, got2 = r["reduced"]
    small = jnp.concatenate([r["dg_pre"][0:1], r["dg_post"][0:1], jnp.pad(r["dg_qa"][0:1], ((0, 0), (0, D - RQ))),
                             jnp.pad(r["dg_kva"][0:1], ((0, 0), (0, D - RKV))), jnp.pad(r["loss"][0:1], ((0, 0), (0, D - 128))),
                             jnp.zeros((3, D), F32)], axis=0)
    p_small, = _all_gather([small], "all_gather_small")
    pad8 = lambda a: jnp.zeros((8, 256), F32).at[0:3, :].set(a)
    params = [(w_in, m_w_in, v_w_in), (w_q_b, m_w_q_b, v_w_q_b), (w_kv_b, m_w_kv_b, v_w_kv_b), (conv_pad, pad8(m_conv_w), pad8(v_conv_w)),
              (w_o_mla, m_w_o_mla, v_w_o_mla), (w_o_conv, m_w_o_conv, v_w_o_conv), (w_out, m_w_out, v_w_out)]
    o_q, o_kv, o_cw, o_om, o_oc, o_out = [
        _reduce_adamw(o, g1, g2, slots, w, m, v, 128, "adamw_" + nm)
        for o, g1, g2, (w, m, v), nm in list(zip(own, got1, got2, params, gnames))[1:]]
    o_cw = [a[0:3] for a in o_cw]
    o_in = [a.T for a in _reduce_adamw_t(own[0], got1[0], got2[0], slots, w_in.T, m_w_in.T, v_w_in.T, 128, "adamw_w_in")]
    padv = lambda a: jnp.pad(a.reshape(1, -1), ((0, 0), (0, D - a.shape[0])))
    stack = lambda pre, post, qa, kva: jnp.concatenate([row2(pre), row2(post), padv(qa), padv(kva), jnp.zeros((4, D), F32)], axis=0)
    o_g = _sum_adamw(p_small, stack(pre_norm_g, post_norm_g, q_a_norm_g, kv_a_norm_g),
                     stack(m_pre_norm_g, m_post_norm_g, m_q_a_norm_g, m_kv_a_norm_g),
                     stack(v_pre_norm_g, v_post_norm_g, v_q_a_norm_g, v_kv_a_norm_g), 8, "adamw_gains")
    loss = o_g[0][4, 0]
    outs = {}
    for idx, kind in enumerate(("grad", "delta", "new_m", "new_v")):
        o = o_g[idx]
        outs[kind] = dict(pre_norm_g=o[0], w_in=o_in[idx], q_a_norm_g=o[2, 0:RQ], w_q_b=o_q[idx], kv_a_norm_g=o[3, 0:RKV],
                          w_kv_b=o_kv[idx], conv_w=o_cw[idx], w_o_mla=o_om[idx], w_o_conv=o_oc[idx], w_out=o_out[idx], post_norm_g=o[1])
    names = ["pre_norm_g", "w_in", "q_a_norm_g", "w_q_b", "kv_a_norm_g", "w_kv_b", "conv_w", "w_o_mla", "w_o_conv", "w_out", "post_norm_g"]
    return (loss, r["grad_x"][None], *[outs["grad"][n] for n in names], *[outs["delta"][n] for n in names],
            *[outs["new_m"][n] for n in names], *[outs["new_v"][n] for n in names])
```

```python
import functools
import math

import jax
import jax.numpy as jnp
from jax import lax
from jax.experimental import pallas as pl
from jax.experimental.pallas import tpu as pltpu

F32 = jnp.float32
BF16 = jnp.bfloat16

NDEV = 8
D = 2048
H = 16
DN = 128
DR = 64
DV = 128
RQ = 512
RKV = 512
HW = 256
ROPE_THETA = 10000.0
RMS_EPS = 1e-6
N_IN = 15424
SHARD_IN = N_IN // NDEV
SMALL = RQ + RKV + DR
NP = 7 * D + RQ + RKV + 128
SEG = dict(z_mla=0, c_in=1, b_gate=2, c_gate=3, z_conv=4, g_mla=5, g_conv=6)
OFF_QA = 7 * D
OFF_CKV = OFF_QA + RQ
OFF_KR = OFF_CKV + RKV
EXT = 2176
VMEM_CAP = 56 * 1024 * 1024

ADAM_LR = 0.001
ADAM_B1 = 0.9
ADAM_B2 = 0.999
ADAM_EPS = 1e-08
ADAM_WD = 0.01
ADAM_STEP = 10

LOG2E = math.log2(math.e)
NN = (((1,), (0,)), ((), ()))
NT = (((1,), (1,)), ((), ()))
TN = (((0,), (0,)), ((), ()))
MESH = pl.DeviceIdType.MESH


def _cparams(sem, vmem_bytes):
    return pltpu.CompilerParams(dimension_semantics=sem, vmem_limit_bytes=int(min(VMEM_CAP, max(vmem_bytes, 16 << 20))))


def _nbytes(shape, dtype):
    return math.prod(shape) * jnp.dtype(dtype).itemsize


class _Exchange:
    def __init__(self, arrays, out_shapes, sem_shapes, start, finish):
        self.arrays, self.out_shapes, self.sem_shapes, self.start, self.finish = arrays, out_shapes, sem_shapes, start, finish
        self.n_sems = len(sem_shapes)


def _matmul(a, b, *, mode, out_dtype, tm, tn, tk, name, m_outer=False, exchange=None):
    if mode == "nn":
        (M, K), (K2, N) = a.shape, b.shape
    elif mode == "nt":
        (M, K), (N, K2) = a.shape, b.shape
    else:
        (K, M), (K2, N) = a.shape, b.shape
    assert K == K2, (a.shape, b.shape, mode)
    tm, tn, tk = min(tm, M), min(tn, N), min(tk, K)
    assert M % tm == 0 and N % tn == 0 and K % tk == 0, (M, N, K, tm, tn, tk)
    ni, nj, nk = M // tm, N // tn, K // tk
    dims = dict(nn=NN, nt=NT, tn=TN)[mode]

    if m_outer:
        grid = (ni, nj, nk)
        ij = lambda g0, g1: (g0, g1)
    else:
        grid = (nj, ni, nk)
        ij = lambda g0, g1: (g1, g0)

    if mode == "tn":
        a_spec = pl.BlockSpec((tk, tm), lambda g0, g1, k: (k, ij(g0, g1)[0]))
        a_tile = (tk, tm)
    else:
        a_spec = pl.BlockSpec((tm, tk), lambda g0, g1, k: (ij(g0, g1)[0], k))
        a_tile = (tm, tk)
    if mode == "nt":
        b_spec = pl.BlockSpec((tn, tk), lambda g0, g1, k: (ij(g0, g1)[1], k))
    else:
        b_spec = pl.BlockSpec((tk, tn), lambda g0, g1, k: (k, ij(g0, g1)[1]))
    o_spec = pl.BlockSpec((tm, tn), lambda g0, g1, k: ij(g0, g1))

    n_in = len(exchange.arrays) if exchange else 0

    def body(a_ref, b_ref, *refs):
        x_in, o_ref, x_out, scratch = refs[:n_in], refs[n_in], refs[n_in + 1:2 * n_in + 1], refs[2 * n_in + 1:]
        if exchange:
            sems = scratch[len(scratch) - exchange.n_sems:]
            step = (pl.program_id(0) * grid[1] + pl.program_id(1)) * grid[2] + pl.program_id(2)

            @pl.when(step == 0)
            def _():
                exchange.start(x_in, x_out, sems)

        prod = lax.dot_general(a_ref[...], b_ref[...], dims, preferred_element_type=F32)
        if nk == 1:
            o_ref[...] = prod.astype(o_ref.dtype)
        else:
            acc_ref = scratch[0]
            k = pl.program_id(2)

            @pl.when(k == 0)
            def _():
                acc_ref[...] = prod

            @pl.when((k > 0) & (k < nk - 1))
            def _():
                acc_ref[...] += prod

            @pl.when(k == nk - 1)
            def _():
                o_ref[...] = (acc_ref[...] + prod).astype(o_ref.dtype)

        if exchange:
            @pl.when(step == grid[0] * grid[1] * grid[2] - 1)
            def _():
                exchange.finish(x_in, x_out, sems)

    vmem = 2 * (_nbytes(a_tile, a.dtype) + _nbytes((tk, tn), b.dtype) + _nbytes((tm, tn), out_dtype)) + 2 * _nbytes((tm, tn), F32)
    anyspec = pl.BlockSpec(memory_space=pl.ANY)
    outs = pl.pallas_call(
        body,
        grid=grid,
        in_specs=[a_spec, b_spec] + [anyspec] * n_in,
        out_specs=[o_spec] + [anyspec] * n_in,
        out_shape=[jax.ShapeDtypeStruct((M, N), out_dtype)] + (exchange.out_shapes if exchange else []),
        scratch_shapes=([] if nk == 1 else [pltpu.VMEM((tm, tn), F32)]) + (exchange.sem_shapes if exchange else []),
        compiler_params=_cparams(("arbitrary",) * 3 if exchange else ("parallel", "parallel", "arbitrary"), vmem + (8 << 20)),
        name=name,
    )(a, b, *(exchange.arrays if exchange else []))
    return (outs[0], outs[1:]) if exchange else outs[0]


def _in_dest(k):
    return SHARD_IN * k - SMALL


def _assemble_w_in(g):
    R = 128

    def body(g_ref, o_ref, ext, acc):
        ext[...] = jnp.zeros_like(ext)
        acc[...] = jnp.zeros_like(acc)
        lane = lax.broadcasted_iota(jnp.int32, (R, EXT), 1)
        ext[:, 0:SHARD_IN] = g_ref[0].astype(F32)
        v = ext[...]
        acc[:, OFF_QA:NP] = jnp.where(lane[:, 0:NP - OFF_QA] < SMALL, v[:, 0:NP - OFF_QA], 0.0)
        w = v[:, 1024:2048]
        w = pltpu.roll(w, 1024 - 64, 1)
        acc[:, 0:1024] = jnp.where(lane[:, 0:1024] < SHARD_IN - SMALL, w, 0.0)
        for k in range(1, NDEV):
            ext[:, 0:SHARD_IN] = g_ref[k].astype(F32)
            dest = _in_dest(k)
            t, o = dest // 128, dest % 128
            width = -(-(o + SHARD_IN) // 128) * 128
            v = pltpu.roll(ext[...], o, 1)[:, 0:width]
            acc[:, 128 * t:128 * t + width] += v
        o_ref[...] = acc[...].astype(BF16)

    return pl.pallas_call(
        body,
        grid=(D // R,),
        in_specs=[pl.BlockSpec((NDEV, R, SHARD_IN), lambda i: (0, i, 0))],
        out_specs=pl.BlockSpec((R, NP), lambda i: (i, 0)),
        out_shape=jax.ShapeDtypeStruct((D, NP), BF16),
        scratch_shapes=[pltpu.VMEM((R, EXT), F32), pltpu.VMEM((R, NP), F32)],
        compiler_params=_cparams(("parallel",), 40 << 20),
        name="assemble_w_in",
    )(g)


def _split_dw_in(dw):
    R = 128

    def body(dw_ref, o_ref, ob_ref):
        def put(k, cols, v):
            o_ref[k, :, cols] = v
            ob_ref[k, :, cols] = v.astype(BF16)

        lane = lax.broadcasted_iota(jnp.int32, (R, 1024), 1)
        put(0, slice(0, 1024), dw_ref[:, OFF_QA:OFF_QA + 1024])
        tail = dw_ref[:, OFF_QA + 1024:NP]
        tail = jnp.concatenate([tail, jnp.zeros((R, 1024 - 128), F32)], axis=1)
        head = dw_ref[:, 0:1024]
        mixed = jnp.where(lane < 64, tail, pltpu.roll(head, 64, 1))
        put(0, slice(1024, SHARD_IN), mixed[:, 0:SHARD_IN - 1024])
        for k in range(1, NDEV):
            dest = _in_dest(k)
            t, o = dest // 128, dest % 128
            width = -(-(o + SHARD_IN) // 128) * 128
            v = dw_ref[:, 128 * t:128 * t + width]
            v = pltpu.roll(v, width - o, 1)
            put(k, slice(0, SHARD_IN), v[:, 0:SHARD_IN])

    blk = pl.BlockSpec((NDEV, R, SHARD_IN), lambda i: (0, i, 0))
    return pl.pallas_call(
        body,
        grid=(D // R,),
        in_specs=[pl.BlockSpec((R, NP), lambda i: (i, 0))],
        out_specs=[blk, blk],
        out_shape=[jax.ShapeDtypeStruct((NDEV, D, SHARD_IN), F32), jax.ShapeDtypeStruct((NDEV, D, SHARD_IN), BF16)],
        compiler_params=_cparams(("parallel",), 48 << 20),
        name="split_dw_in",
    )(dw)


def _assemble_w_q(g):
    def body(g_ref, o_ref):
        lane = lax.broadcasted_iota(jnp.int32, (RQ, 128), 1)
        lo = lane < 64
        for k in range(NDEV):
            t0 = g_ref[k, :, 0:128].astype(F32)
            t1 = g_ref[k, :, 128:256].astype(F32)
            t2 = g_ref[k, :, 256:384].astype(F32)
            base = 2 * k * HW
            o_ref[:, base:base + 128] = t0.astype(BF16)
            o_ref[:, base + 128:base + 256] = jnp.where(lo, t1, 0.0).astype(BF16)
            o_ref[:, base + 256:base + 384] = pltpu.roll(jnp.where(lo, t2, t1), 64, 1).astype(BF16)
            o_ref[:, base + 384:base + 512] = jnp.where(lo, pltpu.roll(t2, 64, 1), 0.0).astype(BF16)

    return pl.pallas_call(
        body,
        out_shape=jax.ShapeDtypeStruct((RQ, H * HW), BF16),
        compiler_params=_cparams(None, 32 << 20),
        name="assemble_w_q",
    )(g)


def _split_dw_q(dw):
    def body(dw_ref, o_ref):
        lane = lax.broadcasted_iota(jnp.int32, (RQ, 128), 1)
        lo = lane < 64
        for k in range(NDEV):
            base = 2 * k * HW
            a = dw_ref[:, base:base + 128]
            b = dw_ref[:, base + 128:base + 256]
            c = pltpu.roll(dw_ref[:, base + 256:base + 384], 64, 1)
            d = pltpu.roll(dw_ref[:, base + 384:base + 512], 64, 1)
            o_ref[k, :, 0:128] = a
            o_ref[k, :, 128:256] = jnp.where(lo, b, c)
            o_ref[k, :, 256:384] = jnp.where(lo, c, d)

    return pl.pallas_call(
        body,
        out_shape=jax.ShapeDtypeStruct((NDEV, RQ, 384), F32),
        compiler_params=_cparams(None, 32 << 20),
        name="split_dw_q",
    )(dw)


def _concat_cols(g, dtype, name):
    n, R, C = g.shape

    def body(g_ref, o_ref):
        for k in range(n):
            o_ref[:, k * C:(k + 1) * C] = g_ref[k].astype(dtype)

    return pl.pallas_call(body, out_shape=jax.ShapeDtypeStruct((R, n * C), dtype), compiler_params=_cparams(None, 32 << 20), name=name)(g)


def _split_cols(x, n, name):
    R, NC = x.shape
    C = NC // n

    def body(x_ref, o_ref):
        for k in range(n):
            o_ref[k] = x_ref[:, k * C:(k + 1) * C]

    return pl.pallas_call(body, out_shape=jax.ShapeDtypeStruct((n, R, C), x.dtype), compiler_params=_cparams(None, 32 << 20), name=name)(x)


def _rms_scale(xf):
    return lax.rsqrt(jnp.mean(xf * xf, axis=-1, keepdims=True) + RMS_EPS)


def _prenorm(x, g, ts):
    S = x.shape[0]

    def body(x_ref, g_ref, h_ref, ht_ref):
        xf = x_ref[...]
        h = xf * _rms_scale(xf) * g_ref[...]
        h_ref[...] = h.astype(BF16)
        ht_ref[...] = jnp.transpose(h).astype(BF16)

    return pl.pallas_call(
        body,
        grid=(S // ts,),
        in_specs=[pl.BlockSpec((ts, D), lambda i: (i, 0)), pl.BlockSpec((1, D), lambda i: (0, 0))],
        out_specs=[pl.BlockSpec((ts, D), lambda i: (i, 0)), pl.BlockSpec((D, ts), lambda i: (0, i))],
        out_shape=[jax.ShapeDtypeStruct((S, D), BF16), jax.ShapeDtypeStruct((D, S), BF16)],
        compiler_params=_cparams(("parallel",), 32 << 20),
        name="prenorm",
    )(x, g)


def _inv_freq_tile():
    inv_freq = ROPE_THETA ** (-jnp.arange(0, DR, 2, dtype=F32) / DR)
    return jnp.tile(inv_freq, 4).reshape(1, 128)


def _rope_tables(pos_ref, freq_ref, ts):
    lane = lax.broadcasted_iota(jnp.int32, (ts, 128), 1)
    ang = pos_ref[...].astype(F32) * freq_ref[...]
    return jnp.cos(ang), jnp.sin(ang), lane


def _rope_swap(t, lane):
    return jnp.where(lane < 32, pltpu.roll(t, 96, 1), pltpu.roll(t, 32, 1))


def _qkv_prep(proj, pos, freq, g_qa, g_kva, w_q, w_kv, ts):
    S = proj.shape[0]

    def body(qa_ref, ckv_ref, kr_ref, pos_ref, freq_ref, gq_ref, gk_ref, wq_ref, wkv_ref, q_ref, k_ref, v_ref, vt_ref, qn_ref, kvn_ref):
        qa = _f32(qa_ref)
        qn = (qa * _rms_scale(qa) * gq_ref[...]).astype(BF16)
        ckv = _f32(ckv_ref)
        kvn = (ckv * _rms_scale(ckv) * gk_ref[...]).astype(BF16)
        qn_ref[...] = qn
        kvn_ref[...] = kvn
        cos, sin, lane = _rope_tables(pos_ref, freq_ref, ts)
        sgn_sin = jnp.where(lane < 32, -sin, sin)
        live = lane < DR
        kr = _f32(kr_ref)
        kr = jnp.where(live, kr * cos + _rope_swap(kr, lane) * sgn_sin, 0.0).astype(BF16)
        qf = jnp.dot(qn, wq_ref[...], preferred_element_type=F32)
        kvf = jnp.dot(kvn, wkv_ref[...], preferred_element_type=F32)
        for h in range(H):
            q_ref[h, :, 0:DN] = qf[:, h * HW:h * HW + DN].astype(BF16)
            t = qf[:, h * HW + DN:(h + 1) * HW]
            q_ref[h, :, DN:HW] = jnp.where(live, t * cos + _rope_swap(t, lane) * sgn_sin, 0.0).astype(BF16)
            k_ref[h, :, 0:DN] = kvf[:, h * HW:h * HW + DN].astype(BF16)
            k_ref[h, :, DN:HW] = kr
            vh = kvf[:, h * HW + DN:(h + 1) * HW]
            v_ref[h] = vh.astype(BF16)
            vt_ref[h] = jnp.transpose(vh).astype(BF16)

    row = lambda w, blk: pl.BlockSpec((ts, w), lambda i: (i, blk))
    full = lambda a: pl.BlockSpec(a.shape, lambda i: (0,) * a.ndim)
    return pl.pallas_call(
        body,
        grid=(S // ts,),
        in_specs=[row(RQ, OFF_QA // RQ), row(RKV, OFF_CKV // RKV), row(128, OFF_KR // 128),
                  pl.BlockSpec((ts, 1), lambda i: (i, 0)), full(freq), full(g_qa), full(g_kva), full(w_q), full(w_kv)],
        out_specs=[pl.BlockSpec((H, ts, HW), lambda i: (0, i, 0)), pl.BlockSpec((H, ts, HW), lambda i: (0, i, 0)),
                   pl.BlockSpec((H, ts, DV), lambda i: (0, i, 0)), pl.BlockSpec((H, DV, ts), lambda i: (0, 0, i)),
                   pl.BlockSpec((ts, RQ), lambda i: (i, 0)), pl.BlockSpec((ts, RKV), lambda i: (i, 0))],
        out_shape=[jax.ShapeDtypeStruct((H, S, HW), BF16), jax.ShapeDtypeStruct((H, S, HW), BF16),
                   jax.ShapeDtypeStruct((H, S, DV), BF16), jax.ShapeDtypeStruct((H, DV, S), BF16),
                   jax.ShapeDtypeStruct((S, RQ), BF16), jax.ShapeDtypeStruct((S, RKV), BF16)],
        compiler_params=_cparams(("parallel",), 48 << 20),
        name="qkv_prep",
    )(proj, proj, proj, pos, freq, g_qa, g_kva, w_q, w_kv)


def _col_to_row8(col, n):
    return jnp.transpose(jnp.broadcast_to(col, (n, 128)))[0:8, :]


def _causal_pairs(n, key_major):
    if key_major:
        pairs = [(i, j) for j in range(n) for i in range(j, n)]
    else:
        pairs = [(i, j) for i in range(n) for j in range(i + 1)]
    return jnp.array([p[0] for p in pairs], jnp.int32), jnp.array([p[1] for p in pairs], jnp.int32)


def _flash_fwd(q, k, vt, tq, hb=8):
    S = q.shape[1]
    nq = S // tq
    scale = 1.0 / math.sqrt(DN + DR)
    c2 = scale * math.log2(math.e)
    tsub = tq
    nsub = tq // tsub

    qi_tab, kj_tab = _causal_pairs(nq, key_major=False)

    def body(qi_ref, kj_ref, q_ref, k_ref, vt_ref, o_ref, lse_ref, m_sc, l_sc, acc_sc):
        p = pl.program_id(1)
        qi, kj = qi_ref[p], kj_ref[p]

        @pl.when(kj == 0)
        def _():
            m_sc[...] = jnp.full_like(m_sc, -jnp.inf)
            l_sc[...] = jnp.zeros_like(l_sc)
            acc_sc[...] = jnp.zeros_like(acc_sc)

        def step(diag):
            chains = [(h, u) for h in range(hb) for u in range(nsub)]

            def scores(h, u):
                return lax.dot_general(k_ref[h], q_ref[h, u * tsub:(u + 1) * tsub, :], NT, preferred_element_type=F32)

            st_next = scores(*chains[0])
            for ci, (h, u) in enumerate(chains):
                st = st_next
                if ci + 1 < len(chains):
                    st_next = scores(*chains[ci + 1])
                cols = slice(u * tsub, (u + 1) * tsub)
                if diag:
                    r = lax.broadcasted_iota(jnp.int32, (tq, tsub), 0)
                    c = lax.broadcasted_iota(jnp.int32, (tq, tsub), 1) + u * tsub
                    st = jnp.where(r <= c, st, -jnp.inf)
                m_prev = m_sc[h, 0:1, cols]
                m_new = jnp.maximum(m_prev, jnp.max(st, axis=0, keepdims=True))
                alpha = jnp.exp2((m_prev - m_new) * c2)
                pt = jnp.exp2((st - m_new) * c2)
                l_sc[h, :, cols] = jnp.broadcast_to(alpha * l_sc[h, 0:1, cols] + jnp.sum(pt, axis=0, keepdims=True), (8, tsub))
                m_sc[h, :, cols] = jnp.broadcast_to(m_new, (8, tsub))
                acc_sc[h, :, cols] = alpha * acc_sc[h, :, cols] + jnp.dot(vt_ref[h], pt.astype(BF16), preferred_element_type=F32)

        @pl.when(kj < qi)
        def _():
            step(False)

        @pl.when(kj == qi)
        def _():
            step(True)
            for h in range(hb):
                l = l_sc[h, 0:1, :]
                o_ref[:, h * DV:(h + 1) * DV] = jnp.transpose(acc_sc[h] / l).astype(BF16)
                lse_ref[h] = m_sc[h] * scale + jnp.log(l_sc[h])

    return pl.pallas_call(
        body,
        grid_spec=pltpu.PrefetchScalarGridSpec(
            num_scalar_prefetch=2,
            grid=(H // hb, len(qi_tab)),
            in_specs=[pl.BlockSpec((hb, tq, HW), lambda h, p, qi, kj: (h, qi[p], 0)),
                      pl.BlockSpec((hb, tq, HW), lambda h, p, qi, kj: (h, kj[p], 0)),
                      pl.BlockSpec((hb, DV, tq), lambda h, p, qi, kj: (h, 0, kj[p]))],
            out_specs=[pl.BlockSpec((tq, hb * DV), lambda h, p, qi, kj: (qi[p], h)),
                       pl.BlockSpec((hb, 8, tq), lambda h, p, qi, kj: (h, 0, qi[p]))],
            scratch_shapes=[pltpu.VMEM((hb, 8, tq), F32), pltpu.VMEM((hb, 8, tq), F32), pltpu.VMEM((hb, DV, tq), F32)],
        ),
        out_shape=[jax.ShapeDtypeStruct((S, H * DV), BF16), jax.ShapeDtypeStruct((H, 8, S), F32)],
        compiler_params=_cparams(("parallel", "arbitrary"), 40 << 20),
        name="flash_fwd",
    )(qi_tab, kj_tab, q, k, vt)


def _sigmoid(x):
    return 1.0 / (1.0 + jnp.exp(-x))


HALO = 16


def _f32(ref):
    return ref[...].astype(F32)


def _shift_rows(u, prev, n, first):
    ts = u.shape[0]
    row = lax.broadcasted_iota(jnp.int32, u.shape, 0)
    out = pltpu.roll(u, n, 0)
    for j in range(n):
        halo = jnp.where(first, 0.0, prev[HALO - n + j:HALO - n + j + 1, :])
        out = jnp.where(row == j, halo, out)
    return out


def _gates_fwd(proj, attn, conv_w, ts):
    S = proj.shape[0]

    def body(attn_ref, zm_ref, cin_ref, bg_ref, cg_ref, zc_ref, cin_p, cg_p, w_ref, am_ref, ac_ref):
        first = pl.program_id(0) == 0
        zm = _f32(zm_ref)
        am_ref[...] = (_f32(attn_ref) * (zm * _sigmoid(zm))).astype(BF16)
        u = _f32(cg_ref) * _f32(cin_ref)
        up = _f32(cg_p) * _f32(cin_p)
        w = w_ref[...]
        conv = w[0:1, :] * _shift_rows(u, up, 2, first) + w[1:2, :] * _shift_rows(u, up, 1, first) + w[2:3, :] * u
        zc = _f32(zc_ref)
        ac_ref[...] = (_f32(bg_ref) * conv * (zc * _sigmoid(zc))).astype(BF16)

    seg = lambda name: pl.BlockSpec((ts, D), lambda i: (i, SEG[name]))
    prev = lambda name: pl.BlockSpec((HALO, D), lambda i: (jnp.maximum(i * (ts // HALO) - 1, 0), SEG[name]))
    return pl.pallas_call(
        body,
        grid=(S // ts,),
        in_specs=[pl.BlockSpec((ts, D), lambda i: (i, 0)), seg("z_mla"), seg("c_in"), seg("b_gate"), seg("c_gate"), seg("z_conv"),
                  prev("c_in"), prev("c_gate"), pl.BlockSpec((8, D), lambda i: (0, 0))],
        out_specs=[pl.BlockSpec((ts, D), lambda i: (i, 0))] * 2,
        out_shape=[jax.ShapeDtypeStruct((S, D), BF16)] * 2,
        compiler_params=_cparams(("arbitrary",), 48 << 20),
        name="gates_fwd",
    )(attn, proj, proj, proj, proj, proj, proj, proj, conv_w)


def _merge_fwd(proj, y_mla, y_conv, ts):
    S = proj.shape[0]

    def body(gm_ref, gc_ref, ym_ref, yc_ref, o_ref):
        o_ref[...] = (_sigmoid(_f32(gm_ref)) * _f32(ym_ref) + _sigmoid(_f32(gc_ref)) * _f32(yc_ref)).astype(BF16)

    seg = lambda name: pl.BlockSpec((ts, D), lambda i: (i, SEG[name]))
    row = pl.BlockSpec((ts, D), lambda i: (i, 0))
    return pl.pallas_call(
        body, grid=(S // ts,), in_specs=[seg("g_mla"), seg("g_conv"), row, row], out_specs=row,
        out_shape=jax.ShapeDtypeStruct((S, D), BF16), compiler_params=_cparams(("parallel",), 32 << 20), name="merge_fwd",
    )(proj, proj, y_mla, y_conv)


def _post_loss(out, x, target, g_post, ts):
    S = out.shape[0]

    def body(o_ref, x_ref, t_ref, g_ref, dy_ref, do_ref, dg_ref, loss_ref):
        i = pl.program_id(0)
        o = o_ref[...]
        r = _rms_scale(o)
        n = o * r
        g = g_ref[...]
        err = x_ref[...] + n * g - t_ref[...]
        dy = err * (1.0 / D)
        dy_ref[...] = dy
        dn = dy * g
        do_ref[...] = (r * (dn - n * jnp.mean(dn * n, axis=-1, keepdims=True))).astype(BF16)
        dg = jnp.sum(dy * n, axis=0, keepdims=True)
        part = jnp.sum(jnp.sum(err * err, axis=0, keepdims=True), axis=1, keepdims=True) * (0.5 / D)

        @pl.when(i == 0)
        def _():
            dg_ref[...] = jnp.zeros_like(dg_ref)
            loss_ref[...] = jnp.zeros_like(loss_ref)

        dg_ref[0:1, :] += dg
        loss_ref[...] += jnp.broadcast_to(part, loss_ref.shape)

    row = pl.BlockSpec((ts, D), lambda i: (i, 0))
    return pl.pallas_call(
        body, grid=(S // ts,),
        in_specs=[row, row, row, pl.BlockSpec((1, D), lambda i: (0, 0))],
        out_specs=[row, row, pl.BlockSpec((8, D), lambda i: (0, 0)), pl.BlockSpec((8, 128), lambda i: (0, 0))],
        out_shape=[jax.ShapeDtypeStruct((S, D), F32), jax.ShapeDtypeStruct((S, D), BF16),
                   jax.ShapeDtypeStruct((8, D), F32), jax.ShapeDtypeStruct((8, 128), F32)],
        compiler_params=_cparams(("arbitrary",), 40 << 20), name="post_loss",
    )(out, x, target, g_post)


def _merge_bwd(proj, y_mla, y_conv, dmerged, ts):
    S = proj.shape[0]

    def body(gm_ref, gc_ref, ym_ref, yc_ref, dm_ref, dym_ref, dyc_ref, dgm_ref, dgc_ref):
        dm = _f32(dm_ref)
        sm = _sigmoid(_f32(gm_ref))
        sc = _sigmoid(_f32(gc_ref))
        dym_ref[...] = (dm * sm).astype(BF16)
        dyc_ref[...] = (dm * sc).astype(BF16)
        dgm_ref[...] = (dm * _f32(ym_ref) * (sm * (1.0 - sm))).astype(BF16)
        dgc_ref[...] = (dm * _f32(yc_ref) * (sc * (1.0 - sc))).astype(BF16)

    seg = lambda name: pl.BlockSpec((ts, D), lambda i: (i, SEG[name]))
    row = pl.BlockSpec((ts, D), lambda i: (i, 0))
    return pl.pallas_call(
        body, grid=(S // ts,), in_specs=[seg("g_mla"), seg("g_conv"), row, row, row], out_specs=[row] * 4,
        out_shape=[jax.ShapeDtypeStruct((S, D), BF16)] * 4, compiler_params=_cparams(("parallel",), 40 << 20), name="merge_bwd",
    )(proj, proj, y_mla, y_conv, dmerged)


def _gates_bwd(proj, attn, da_mla, da_conv, conv_w, ts):
    S = proj.shape[0]
    nblk = S // ts

    def body(attn_ref, zm_ref, cin_ref, bg_ref, cg_ref, zc_ref, dam_ref, dac_ref, cin_p, cg_p, bg_n, zc_n, dac_n, w_ref,
             dattn_ref, delta_ref, dzm_ref, dcin_ref, dbg_ref, dcg_ref, dzc_ref, dw_ref):
        i = pl.program_id(0)
        first = i == 0
        last = i == nblk - 1

        @pl.when(first)
        def _():
            dw_ref[...] = jnp.zeros_like(dw_ref)

        row = lax.broadcasted_iota(jnp.int32, (ts, DV), 0)
        for c in range(D // DV):
            cols = slice(c * DV, (c + 1) * DV)
            ld = lambda ref: ref[:, cols].astype(F32)
            zm = ld(zm_ref)
            sg = _sigmoid(zm)
            attn = ld(attn_ref)
            dam = ld(dam_ref)
            dattn = dam * (zm * sg)
            dattn_ref[:, cols] = dattn.astype(BF16)
            dzm_ref[:, cols] = (dam * attn * (sg * (1.0 + zm * (1.0 - sg)))).astype(BF16)
            delta_ref[c] = _col_to_row8(jnp.sum(dattn * attn, axis=1, keepdims=True), ts)
            w = w_ref[:, cols]
            cin, cg, bg, zc = ld(cin_ref), ld(cg_ref), ld(bg_ref), ld(zc_ref)
            u = cg * cin
            up = ld(cg_p) * ld(cin_p)
            u1 = _shift_rows(u, up, 1, first)
            u2 = _shift_rows(u, up, 2, first)
            conv = w[0:1, :] * u2 + w[1:2, :] * u1 + w[2:3, :] * u
            sgc = _sigmoid(zc)
            siluc = zc * sgc
            dac = ld(dac_ref)
            dbg_ref[:, cols] = (dac * conv * siluc).astype(BF16)
            dzc_ref[:, cols] = (dac * bg * conv * (sgc * (1.0 + zc * (1.0 - sgc)))).astype(BF16)
            dconv = dac * bg * siluc
            zn = ld(zc_n)
            dconv_n = jnp.where(last, 0.0, ld(dac_n) * ld(bg_n) * (zn * _sigmoid(zn)))
            d1 = jnp.where(row == ts - 1, dconv_n[0:1, :], pltpu.roll(dconv, ts - 1, 0))
            d2 = jnp.where(row == ts - 1, dconv_n[1:2, :], jnp.where(row == ts - 2, dconv_n[0:1, :], pltpu.roll(dconv, ts - 2, 0)))
            du = w[2:3, :] * dconv + w[1:2, :] * d1 + w[0:1, :] * d2
            dcg_ref[:, cols] = (du * cin).astype(BF16)
            dcin_ref[:, cols] = (du * cg).astype(BF16)
            dw_ref[0:1, cols] += jnp.sum(dconv * u2, axis=0, keepdims=True)
            dw_ref[1:2, cols] += jnp.sum(dconv * u1, axis=0, keepdims=True)
            dw_ref[2:3, cols] += jnp.sum(dconv * u, axis=0, keepdims=True)

    seg = lambda name: pl.BlockSpec((ts, D), lambda i: (i, SEG[name]))
    prev = lambda name: pl.BlockSpec((HALO, D), lambda i: (jnp.maximum(i * (ts // HALO) - 1, 0), SEG[name]))
    nxt = lambda blk: pl.BlockSpec((HALO, D), lambda i: (jnp.minimum((i + 1) * (ts // HALO), S // HALO - 1), blk))
    row = pl.BlockSpec((ts, D), lambda i: (i, 0))
    return pl.pallas_call(
        body, grid=(nblk,),
        in_specs=[row, seg("z_mla"), seg("c_in"), seg("b_gate"), seg("c_gate"), seg("z_conv"), row, row,
                  prev("c_in"), prev("c_gate"), nxt(SEG["b_gate"]), nxt(SEG["z_conv"]), nxt(0), pl.BlockSpec((8, D), lambda i: (0, 0))],
        out_specs=[row, pl.BlockSpec((H, 8, ts), lambda i: (0, 0, i)), row, row, row, row, row, pl.BlockSpec((8, D), lambda i: (0, 0))],
        out_shape=[jax.ShapeDtypeStruct((S, D), BF16), jax.ShapeDtypeStruct((H, 8, S), F32)] + [jax.ShapeDtypeStruct((S, D), BF16)] * 5
        + [jax.ShapeDtypeStruct((8, D), F32)],
        compiler_params=_cparams(("arbitrary",), 56 << 20), name="gates_bwd",
    )(attn, proj, proj, proj, proj, proj, da_mla, da_conv, proj, proj, proj, proj, da_conv, conv_w)


def _flash_bwd(q, k, v, do, lse, delta, tq, hb=2):
    S = q.shape[1]
    nq = S // tq
    scale = 1.0 / math.sqrt(DN + DR)
    c2 = scale * LOG2E
    tsub = min(256, tq)
    nsub = tq // tsub

    qi_tab, kj_tab = _causal_pairs(nq, key_major=True)
    npairs = nq * (nq + 1) // 2

    def body(qi_ref, kj_ref, q_ref, k_ref, v_ref, do_ref, lse_ref, dl_ref, dq_ref, dk_ref, dv_ref, dk_sc, dv_sc, dq_sc):
        p = pl.program_id(1)
        qi, kj = qi_ref[p], kj_ref[p]

        @pl.when(p == 0)
        def _():
            dq_sc[...] = jnp.zeros_like(dq_sc)

        @pl.when(qi == kj)
        def _():
            dk_sc[...] = jnp.zeros_like(dk_sc)
            dv_sc[...] = jnp.zeros_like(dv_sc)

        def step(diag):
            chains = [(h, u) for h in range(hb) for u in range(nsub)]

            def first_matmuls(h, u):
                sub = slice(u * tsub, (u + 1) * tsub)
                st = lax.dot_general(k_ref[h], q_ref[h, sub, :], NT, preferred_element_type=F32)
                dpt = lax.dot_general(v_ref[h], do_ref[sub, h * DV:(h + 1) * DV], NT, preferred_element_type=F32)
                return st, dpt

            nxt = first_matmuls(*chains[0])
            for ci, (h, u) in enumerate(chains):
                st, dpt = nxt
                if ci + 1 < len(chains):
                    nxt = first_matmuls(*chains[ci + 1])
                sub = slice(u * tsub, (u + 1) * tsub)
                pt = jnp.exp2(st * c2 - lse_ref[h, 0:1, sub] * LOG2E)
                if diag:
                    r = lax.broadcasted_iota(jnp.int32, (tq, tsub), 0)
                    c = lax.broadcasted_iota(jnp.int32, (tq, tsub), 1) + u * tsub
                    pt = jnp.where(r <= c, pt, 0.0)
                dst = (pt * (dpt - dl_ref[h, 0:1, sub])).astype(BF16)
                dv_sc[h] += jnp.dot(pt.astype(BF16), do_ref[sub, h * DV:(h + 1) * DV], preferred_element_type=F32)
                dk_sc[h] += jnp.dot(dst, q_ref[h, sub, :], preferred_element_type=F32)
                rows = pl.ds(pl.multiple_of(qi * tq + u * tsub, tsub), tsub)
                dq_sc[h, rows, :] += lax.dot_general(dst, k_ref[h], TN, preferred_element_type=F32)

        @pl.when(qi > kj)
        def _():
            step(False)

        @pl.when(qi == kj)
        def _():
            step(True)

        @pl.when(qi == nq - 1)
        def _():
            dk_ref[...] = (dk_sc[...] * scale).astype(BF16)
            dv_ref[...] = dv_sc[...].astype(BF16)

        @pl.when(p == npairs - 1)
        def _():
            dq_ref[...] = (dq_sc[...] * scale).astype(BF16)

    return pl.pallas_call(
        body,
        grid_spec=pltpu.PrefetchScalarGridSpec(
            num_scalar_prefetch=2,
            grid=(H // hb, npairs),
            in_specs=[pl.BlockSpec((hb, tq, HW), lambda h, p, qi, kj: (h, qi[p], 0)),
                      pl.BlockSpec((hb, tq, HW), lambda h, p, qi, kj: (h, kj[p], 0)),
                      pl.BlockSpec((hb, tq, DV), lambda h, p, qi, kj: (h, kj[p], 0)),
                      pl.BlockSpec((tq, hb * DV), lambda h, p, qi, kj: (qi[p], h)),
                      pl.BlockSpec((hb, 8, tq), lambda h, p, qi, kj: (h, 0, qi[p])),
                      pl.BlockSpec((hb, 8, tq), lambda h, p, qi, kj: (h, 0, qi[p]))],
            out_specs=[pl.BlockSpec((hb, S, HW), lambda h, p, qi, kj: (h, 0, 0)),
                       pl.BlockSpec((hb, tq, HW), lambda h, p, qi, kj: (h, kj[p], 0)),
                       pl.BlockSpec((hb, tq, DV), lambda h, p, qi, kj: (h, kj[p], 0))],
            scratch_shapes=[pltpu.VMEM((hb, tq, HW), F32), pltpu.VMEM((hb, tq, DV), F32), pltpu.VMEM((hb, S, HW), F32)],
        ),
        out_shape=[jax.ShapeDtypeStruct((H, S, HW), BF16), jax.ShapeDtypeStruct((H, S, HW), BF16), jax.ShapeDtypeStruct((H, S, DV), BF16)],
        compiler_params=_cparams(("parallel", "arbitrary"), VMEM_CAP),
        name="flash_bwd",
    )(qi_tab, kj_tab, q, k, v, do, lse, delta)


def _rms_bwd(xf, g, dn_out):
    r = _rms_scale(xf)
    n = xf * r
    dn = dn_out * g
    dx = r * (dn - n * jnp.mean(dn * n, axis=-1, keepdims=True))
    return dx, jnp.sum(dn_out * n, axis=0, keepdims=True)


def _qkv_bwd(proj, pos, freq, g_qa, g_kva, w_q, w_kv, dq, dk, dv, ts):
    S = proj.shape[0]

    def body(qa_ref, ckv_ref, pos_ref, freq_ref, gq_ref, gk_ref, wq_ref, wkv_ref, dq_ref, dk_ref, dv_ref,
             dqp_ref, dkvp_ref, dqa_ref, dckv_ref, dkr_ref, dgq_ref, dgk_ref):
        i = pl.program_id(0)
        cos, sin, lane = _rope_tables(pos_ref, freq_ref, ts)
        sgn_sin = jnp.where(lane < 32, sin, -sin)
        live = lane < DR
        kr_sum = jnp.zeros((ts, 128), F32)
        for h in range(H):
            dqp_ref[:, h * HW:h * HW + DN] = dq_ref[h, :, 0:DN]
            t = dq_ref[h, :, DN:HW].astype(F32)
            dqp_ref[:, h * HW + DN:(h + 1) * HW] = jnp.where(live, t * cos + _rope_swap(t, lane) * sgn_sin, 0.0).astype(BF16)
            dkvp_ref[:, h * HW:h * HW + DN] = dk_ref[h, :, 0:DN]
            dkvp_ref[:, h * HW + DN:(h + 1) * HW] = dv_ref[h]
            kr_sum = kr_sum + dk_ref[h, :, DN:HW].astype(F32)
        dkr_ref[...] = jnp.where(live, kr_sum * cos + _rope_swap(kr_sum, lane) * sgn_sin, 0.0).astype(BF16)
        dqn = lax.dot_general(dqp_ref[...], wq_ref[...], NT, preferred_element_type=F32)
        dkvn = lax.dot_general(dkvp_ref[...], wkv_ref[...], NT, preferred_element_type=F32)
        dqa, dgq = _rms_bwd(_f32(qa_ref), gq_ref[...], dqn)
        dckv, dgk = _rms_bwd(_f32(ckv_ref), gk_ref[...], dkvn)
        dqa_ref[...] = dqa.astype(BF16)
        dckv_ref[...] = dckv.astype(BF16)

        @pl.when(i == 0)
        def _():
            dgq_ref[...] = jnp.zeros_like(dgq_ref)
            dgk_ref[...] = jnp.zeros_like(dgk_ref)

        dgq_ref[0:1, :] += dgq
        dgk_ref[0:1, :] += dgk

    rowb = lambda w, blk: pl.BlockSpec((ts, w), lambda i: (i, blk))
    full = lambda a: pl.BlockSpec(a.shape, lambda i: (0,) * a.ndim)
    heads = lambda w: pl.BlockSpec((H, ts, w), lambda i: (0, i, 0))
    return pl.pallas_call(
        body, grid=(S // ts,),
        in_specs=[rowb(RQ, OFF_QA // RQ), rowb(RKV, OFF_CKV // RKV), pl.BlockSpec((ts, 1), lambda i: (i, 0)), full(freq), full(g_qa), full(g_kva),
                  full(w_q), full(w_kv), heads(HW), heads(HW), heads(DV)],
        out_specs=[rowb(H * HW, 0), rowb(H * HW, 0), rowb(RQ, 0), rowb(RKV, 0), rowb(128, 0),
                   pl.BlockSpec((8, RQ), lambda i: (0, 0)), pl.BlockSpec((8, RKV), lambda i: (0, 0))],
        out_shape=[jax.ShapeDtypeStruct((S, H * HW), BF16), jax.ShapeDtypeStruct((S, H * HW), BF16), jax.ShapeDtypeStruct((S, RQ), BF16),
                   jax.ShapeDtypeStruct((S, RKV), BF16), jax.ShapeDtypeStruct((S, 128), BF16),
                   jax.ShapeDtypeStruct((8, RQ), F32), jax.ShapeDtypeStruct((8, RKV), F32)],
        compiler_params=_cparams(("arbitrary",), 56 << 20), name="qkv_bwd",
    )(proj, proj, pos, freq, g_qa, g_kva, w_q, w_kv, dq, dk, dv)


def _pack_dproj(parts, ts):
    S = parts[0].shape[0]
    widths = [p.shape[1] for p in parts]
    assert sum(widths) == NP

    def body(*refs):
        o_ref = refs[-1]
        off = 0
        for r, w in zip(refs[:-1], widths):
            o_ref[:, off:off + w] = r[...]
            off += w

    return pl.pallas_call(
        body, grid=(S // ts,),
        in_specs=[pl.BlockSpec((ts, w), lambda i: (i, 0)) for w in widths],
        out_specs=pl.BlockSpec((ts, NP), lambda i: (i, 0)),
        out_shape=jax.ShapeDtypeStruct((S, NP), BF16),
        compiler_params=_cparams(("parallel",), 48 << 20), name="pack_dproj",
    )(*parts)


def _prenorm_bwd(x, g, dh, dy, ts):
    S = x.shape[0]

    def body(x_ref, g_ref, dh_ref, dy_ref, gx_ref, dg_ref):
        dx, dg = _rms_bwd(x_ref[...], g_ref[...], dh_ref[...])
        gx_ref[...] = dy_ref[...] + dx

        @pl.when(pl.program_id(0) == 0)
        def _():
            dg_ref[...] = jnp.zeros_like(dg_ref)

        dg_ref[0:1, :] += dg

    row = pl.BlockSpec((ts, D), lambda i: (i, 0))
    return pl.pallas_call(
        body, grid=(S // ts,), in_specs=[row, pl.BlockSpec((1, D), lambda i: (0, 0)), row, row],
        out_specs=[row, pl.BlockSpec((8, D), lambda i: (0, 0))],
        out_shape=[jax.ShapeDtypeStruct((S, D), F32), jax.ShapeDtypeStruct((8, D), F32)],
        compiler_params=_cparams(("arbitrary",), 40 << 20), name="prenorm_bwd",
    )(x, g, dh, dy)


def _local_step(x, pos, target, g_pre, g_qa, g_kva, g_post, w_in, other_weights, reduce_grads=None):
    S = x.shape[0]
    ts = min(256, S)
    tq = min(512, S)
    tm = min(1024, S)
    mm = functools.partial(_matmul, tm=tm)
    freq = _inv_freq_tile()

    h, ht = _prenorm(x, g_pre, ts)
    if len(other_weights) == 2:
        gather, assemble = other_weights
        proj, gathered = _matmul(h, w_in, mode="nn", out_dtype=BF16, tm=1024, tn=1408, tk=D, name="mm_proj", exchange=gather)
        other_weights = assemble(*gathered)
    else:
        proj = _matmul(h, w_in, mode="nn", out_dtype=BF16, tm=1024, tn=1408, tk=D, name="mm_proj")
    w_q, w_kv, conv_w8, w_o_mla, w_o_conv, w_out = other_weights
    q, k, v, vt, qn, kvn = _qkv_prep(proj, pos, freq, g_qa, g_kva, w_q, w_kv, ts)
    attn, lse = _flash_fwd(q, k, vt, tq)
    a_mla, a_conv = _gates_fwd(proj, attn, conv_w8, ts)
    y_mla = mm(a_mla, w_o_mla, mode="nn", out_dtype=BF16, tn=D, tk=D, name="mm_y_mla")
    y_conv = mm(a_conv, w_o_conv, mode="nn", out_dtype=BF16, tn=D, tk=D, name="mm_y_conv")
    merged = _merge_fwd(proj, y_mla, y_conv, ts)
    out = mm(merged, w_out, mode="nn", out_dtype=F32, tn=D, tk=D, name="mm_out")
    dy, dout, dg_post, loss = _post_loss(out, x, target, g_post, ts)

    dmerged = mm(dout, w_out, mode="nt", out_dtype=BF16, tn=D, tk=D, name="mm_dmerged")
    dw_out = _matmul(merged, dout, mode="tn", out_dtype=F32, tm=1024, tn=1024, tk=2048, name="mm_dw_out")
    dy_mla, dy_conv, dg_mla, dg_conv = _merge_bwd(proj, y_mla, y_conv, dmerged, ts)
    da_mla = mm(dy_mla, w_o_mla, mode="nt", out_dtype=BF16, tn=D, tk=D, name="mm_da_mla")
    da_conv = mm(dy_conv, w_o_conv, mode="nt", out_dtype=BF16, tn=D, tk=D, name="mm_da_conv")
    dw_o_mla = _matmul(a_mla, dy_mla, mode="tn", out_dtype=F32, tm=1024, tn=1024, tk=2048, name="mm_dw_o_mla")
    dw_o_conv = _matmul(a_conv, dy_conv, mode="tn", out_dtype=F32, tm=1024, tn=1024, tk=2048, name="mm_dw_o_conv")
    dattn, delta, dz_mla, dc_in, db_gate, dc_gate, dz_conv, dconv_w = _gates_bwd(proj, attn, da_mla, da_conv, conv_w8, min(128, S))
    dq, dk, dv = _flash_bwd(q, k, v, dattn, lse, delta, tq)
    dqp, dkvp, dq_a, dc_kv, dk_rope, dg_qa, dg_kva = _qkv_bwd(proj, pos, freq, g_qa, g_kva, w_q, w_kv, dq, dk, dv, min(128, S))
    dw_q = _matmul(qn, dqp, mode="tn", out_dtype=F32, tm=RQ, tn=1024, tk=2048, name="mm_dw_q")
    dw_kv = _matmul(kvn, dkvp, mode="tn", out_dtype=F32, tm=RKV, tn=1024, tk=2048, name="mm_dw_kv")
    dproj = _pack_dproj([dz_mla, dc_in, db_gate, dc_gate, dz_conv, dg_mla, dg_conv, dq_a, dc_kv, dk_rope], ts)
    dw_in = _matmul(ht, dproj, mode="nn", out_dtype=F32, tm=1024, tn=1408, tk=2048, name="mm_dw_in")
    res = dict(dw_in=dw_in, dw_q=dw_q, dw_kv=dw_kv, dconv_w=dconv_w, dw_o_mla=dw_o_mla, dw_o_conv=dw_o_conv, dw_out=dw_out)
    if reduce_grads is None:
        dh = _matmul(dproj, w_in, mode="nt", out_dtype=F32, tm=1024, tn=D, tk=1408, name="mm_dh")
    else:
        exchange, finish = reduce_grads(res)
        dh, got = _matmul(dproj, w_in, mode="nt", out_dtype=F32, tm=1024, tn=D, tk=1408, name="mm_dh", exchange=exchange)
        res["reduced"] = finish(got)
    grad_x, dg_pre = _prenorm_bwd(x, g_pre, dh, dy, ts)
    res.update(loss=loss, grad_x=grad_x, dg_pre=dg_pre, dg_qa=dg_qa, dg_kva=dg_kva, dg_post=dg_post)
    return res


def _my_id():
    return lax.axis_index("x") * 4 + lax.axis_index("y") * 2 + lax.axis_index("c")


def _place():
    x, y, c = lax.axis_index("x"), lax.axis_index("y"), lax.axis_index("c")
    return (x, y, c), (x, y, 1 - c), [(1 - x, y), (x, 1 - y), (1 - x, 1 - y)]


def _slot(px, py, pc):
    return 4 * px + 2 * py + pc


def _all_gather(arrays, name):
    n = len(arrays)

    def body(*refs):
        ins, outs = refs[:n], refs[n:2 * n]
        send_sems, recv_sems, local_sems = refs[2 * n:]
        me, sib, chips = _place()
        c = me[2]

        def copy(a, k, block, to, src=None):
            rows = outs[a].at[_slot(*block)]
            return pltpu.make_async_remote_copy(src_ref=rows if src is None else src, dst_ref=rows, send_sem=send_sems.at[a, k],
                                                recv_sem=recv_sems.at[a, k], device_id=to, device_id_type=MESH)

        local = [pltpu.make_async_copy(ins[a], outs[a].at[_slot(*me)], local_sems.at[a]) for a in range(n)]
        for cp in local:
            cp.start()
        sends = []
        for a in range(n):
            sends.append(copy(a, 0, me, sib, src=ins[a]))
            sends += [copy(a, 1 + j, me, (*chip, c), src=ins[a]) for j, chip in enumerate(chips)]
        for cp in sends:
            cp.start()
        for a in range(n):
            for j, chip in enumerate(chips):
                copy(a, 1 + j, (*chip, c), me).wait_recv()
                fwd = copy(a, 4 + j, (*chip, c), sib)
                fwd.start()
                sends.append(fwd)
        for a in range(n):
            copy(a, 0, sib, me).wait_recv()
            for j, chip in enumerate(chips):
                copy(a, 4 + j, (*chip, 1 - c), me).wait_recv()
        for cp in sends:
            cp.wait_send()
        for cp in local:
            cp.wait()

    anyspec = pl.BlockSpec(memory_space=pl.ANY)
    return pl.pallas_call(
        body,
        in_specs=[anyspec] * n,
        out_specs=[anyspec] * n,
        out_shape=[jax.ShapeDtypeStruct((NDEV,) + a.shape, a.dtype) for a in arrays],
        scratch_shapes=[pltpu.SemaphoreType.DMA((n, NDEV - 1)), pltpu.SemaphoreType.DMA((n, NDEV - 1)), pltpu.SemaphoreType.DMA((n,))],
        name=name,
    )(*arrays)


def _slab_exchange(arrays, nslots, pick):
    n = len(arrays)

    def copies(ins, outs, sems):
        send_sems, recv_sems = sems
        return [pltpu.make_async_remote_copy(src_ref=ins[a].at[pick(k)[0]], dst_ref=outs[a].at[k], send_sem=send_sems.at[a, k],
                                             recv_sem=recv_sems.at[a, k], device_id=pick(k)[1], device_id_type=MESH)
                for a in range(n) for k in range(nslots)]

    def start(ins, outs, sems):
        for cp in copies(ins, outs, sems):
            cp.start()

    def finish(ins, outs, sems):
        cps = copies(ins, outs, sems)
        for cp in cps:
            cp.wait_recv()
        for cp in cps:
            cp.wait_send()

    return _Exchange(arrays, [jax.ShapeDtypeStruct((nslots,) + a.shape[1:], a.dtype) for a in arrays],
                     [pltpu.SemaphoreType.DMA((n, nslots)), pltpu.SemaphoreType.DMA((n, nslots))], start, finish)


def _direct_gather(arrays):
    n = len(arrays)

    def peer(d):
        p = (_my_id() + d) % NDEV
        return (p // 4, (p // 2) % 2, p % 2), p

    def copies(ins, outs, sems, receiving):
        send_sems, recv_sems, _ = sems
        slot = lambda d: peer(NDEV - d)[1] if receiving else _my_id()
        return [pltpu.make_async_remote_copy(src_ref=ins[a], dst_ref=outs[a].at[slot(d)], send_sem=send_sems.at[a, d - 1],
                                             recv_sem=recv_sems.at[a, d - 1], device_id=peer(d)[0], device_id_type=MESH)
                for d in range(1, NDEV) for a in range(n)]

    def local(ins, outs, sems):
        return [pltpu.make_async_copy(ins[a], outs[a].at[_my_id()], sems[2].at[a]) for a in range(n)]

    def start(ins, outs, sems):
        for cp in local(ins, outs, sems) + copies(ins, outs, sems, False):
            cp.start()

    def finish(ins, outs, sems):
        for cp in copies(ins, outs, sems, True):
            cp.wait_recv()
        for cp in copies(ins, outs, sems, False):
            cp.wait_send()
        for cp in local(ins, outs, sems):
            cp.wait()

    return _Exchange(arrays, [jax.ShapeDtypeStruct((NDEV,) + a.shape, a.dtype) for a in arrays],
                     [pltpu.SemaphoreType.DMA((n, NDEV - 1)), pltpu.SemaphoreType.DMA((n, NDEV - 1)), pltpu.SemaphoreType.DMA((n,))],
                     start, finish)


def _run_exchange(exchange, name):
    n = len(exchange.arrays)

    def body(*refs):
        ins, outs, sems = refs[:n], refs[n:2 * n], refs[2 * n:]
        exchange.start(ins, outs, sems)
        exchange.finish(ins, outs, sems)

    anyspec = pl.BlockSpec(memory_space=pl.ANY)
    return pl.pallas_call(body, in_specs=[anyspec] * n, out_specs=[anyspec] * n, out_shape=exchange.out_shapes,
                          scratch_shapes=exchange.sem_shapes, name=name)(*exchange.arrays)


def _slabs_bf16(x, rows_per_block, name):
    n, R, C = x.shape
    tr = min(rows_per_block, R)

    def body(x_ref, o_ref):
        o_ref[...] = x_ref[...].astype(BF16)

    blk = pl.BlockSpec((1, tr, C), lambda k, i: (k, i, 0))
    return pl.pallas_call(body, grid=(n, R // tr), in_specs=[blk], out_specs=blk, out_shape=jax.ShapeDtypeStruct(x.shape, BF16),
                          compiler_params=_cparams(("parallel", "parallel"), 32 << 20), name=name)(x)


def _to_sibling(k):
    me, sib, chips = _place()
    dest = sib if k == 0 else (*chips[k - 1], sib[2])
    return _slot(*dest), sib


def _to_chips(k):
    me, sib, chips = _place()
    return k, (*chips[k], me[2])


def _chip_sums(own, got, slots, rows_per_block, name):
    _, R, C = own.shape
    tr = min(rows_per_block, R)

    def body(slots_ref, own_ref, got_ref, o_ref):
        o_ref[0] = (own_ref[0] + got_ref[0].astype(F32)).astype(BF16)

    return pl.pallas_call(
        body,
        grid_spec=pltpu.PrefetchScalarGridSpec(
            num_scalar_prefetch=1, grid=(3, R // tr),
            in_specs=[pl.BlockSpec((1, tr, C), lambda j, i, s: (s[1 + j], i, 0)), pl.BlockSpec((1, tr, C), lambda j, i, s: (1 + j, i, 0))],
            out_specs=pl.BlockSpec((1, tr, C), lambda j, i, s: (j, i, 0))),
        out_shape=jax.ShapeDtypeStruct((3, R, C), BF16),
        compiler_params=_cparams(("parallel", "parallel"), 32 << 20), name=name,
    )(slots, own, got)


def _adamw_math(w, g, m, v):
    m = ADAM_B1 * m + (1.0 - ADAM_B1) * g
    v = ADAM_B2 * v + (1.0 - ADAM_B2) * (g * g)
    m_hat = m / (1.0 - ADAM_B1 ** ADAM_STEP)
    v_hat = v / (1.0 - ADAM_B2 ** ADAM_STEP)
    delta = -ADAM_LR * (m_hat / (jnp.sqrt(v_hat) + ADAM_EPS) + ADAM_WD * w)
    return delta, m, v


def _reduce_adamw(own, got1, got2, slots, w, m, v, rows_per_block, name):
    _, R, C = own.shape
    tr = min(rows_per_block, R)
    assert R % tr == 0

    def body(slots_ref, own_ref, g1_ref, g2_ref, w_ref, m_ref, v_ref, g_ref, d_ref, nm_ref, nv_ref):
        g = own_ref[0] + g1_ref[0].astype(F32)
        for j in range(3):
            g = g + g2_ref[j].astype(F32)
        g_ref[...] = g
        d, nm, nv = _adamw_math(w_ref[...], g, m_ref[...], v_ref[...])
        d_ref[...] = d
        nm_ref[...] = nm
        nv_ref[...] = nv

    blk = pl.BlockSpec((tr, C), lambda i, s: (i, 0))
    return pl.pallas_call(
        body,
        grid_spec=pltpu.PrefetchScalarGridSpec(
            num_scalar_prefetch=1, grid=(R // tr,),
            in_specs=[pl.BlockSpec((1, tr, C), lambda i, s: (s[0], i, 0)), pl.BlockSpec((1, tr, C), lambda i, s: (0, i, 0)),
                      pl.BlockSpec((3, tr, C), lambda i, s: (0, i, 0)), blk, blk, blk],
            out_specs=[blk] * 4),
        out_shape=[jax.ShapeDtypeStruct((R, C), F32)] * 4,
        compiler_params=_cparams(("parallel",), 48 << 20), name=name,
    )(slots, own, got1, got2, w, m, v)


def _reduce_adamw_t(own, got1, got2, slots, wt, mt, vt, rows_per_block, name):
    _, R, C = own.shape
    tr = min(rows_per_block, R)
    assert R % tr == 0

    def body(slots_ref, own_ref, g1_ref, g2_ref, w_ref, m_ref, v_ref, g_ref, d_ref, nm_ref, nv_ref):
        g = own_ref[0] + g1_ref[0].astype(F32)
        for j in range(3):
            g = g + g2_ref[j].astype(F32)
        g = jnp.transpose(g)
        g_ref[...] = g
        d, nm, nv = _adamw_math(w_ref[...], g, m_ref[...], v_ref[...])
        d_ref[...] = d
        nm_ref[...] = nm
        nv_ref[...] = nv

    blk = pl.BlockSpec((C, tr), lambda i, s: (0, i))
    return pl.pallas_call(
        body,
        grid_spec=pltpu.PrefetchScalarGridSpec(
            num_scalar_prefetch=1, grid=(R // tr,),
            in_specs=[pl.BlockSpec((1, tr, C), lambda i, s: (s[0], i, 0)), pl.BlockSpec((1, tr, C), lambda i, s: (0, i, 0)),
                      pl.BlockSpec((3, tr, C), lambda i, s: (0, i, 0)), blk, blk, blk],
            out_specs=[blk] * 4),
        out_shape=[jax.ShapeDtypeStruct((C, R), F32)] * 4,
        compiler_params=_cparams(("parallel",), 48 << 20), name=name,
    )(slots, own, got1, got2, wt, mt, vt)


def _sum_adamw(parts, w, m, v, rows_per_block, name):
    n, R, C = parts.shape
    tr = min(rows_per_block, R)
    assert R % tr == 0

    def body(p_ref, w_ref, m_ref, v_ref, g_ref, d_ref, nm_ref, nv_ref):
        g = p_ref[0]
        for j in range(1, n):
            g = g + p_ref[j]
        g_ref[...] = g
        d, nm, nv = _adamw_math(w_ref[...], g, m_ref[...], v_ref[...])
        d_ref[...] = d
        nm_ref[...] = nm
        nv_ref[...] = nv

    blk = pl.BlockSpec((tr, C), lambda i: (i, 0))
    return pl.pallas_call(
        body, grid=(R // tr,),
        in_specs=[pl.BlockSpec((n, tr, C), lambda i: (0, i, 0)), blk, blk, blk],
        out_specs=[blk] * 4,
        out_shape=[jax.ShapeDtypeStruct((R, C), F32)] * 4,
        compiler_params=_cparams(("parallel",), 48 << 20), name=name,
    )(parts, w, m, v)


def kernel(x, positions, pre_norm_g, w_in, q_a_norm_g, w_q_b, kv_a_norm_g, w_kv_b, conv_w, w_o_mla, w_o_conv, w_out, post_norm_g, loss_target, m_pre_norm_g, m_w_in, m_q_a_norm_g, m_w_q_b, m_kv_a_norm_g, m_w_kv_b, m_conv_w, m_w_o_mla, m_w_o_conv, m_w_out, m_post_norm_g, v_pre_norm_g, v_w_in, v_q_a_norm_g, v_w_q_b, v_kv_a_norm_g, v_w_kv_b, v_conv_w, v_w_o_mla, v_w_o_conv, v_w_out, v_post_norm_g):
    S = x.shape[1]
    conv_pad = jnp.zeros((8, 256), F32).at[0:3, :].set(conv_w)
    g_in, = _all_gather([w_in.astype(BF16)], "all_gather_w_in")
    w_in_f = _assemble_w_in(g_in)
    gather_rest = _direct_gather([w_q_b.astype(BF16), w_kv_b.astype(BF16), conv_pad, w_o_mla.astype(BF16), w_o_conv.astype(BF16),
                                  w_out.astype(BF16)])

    def assemble_rest(g_q, g_kv, g_cw, g_om, g_oc, g_out):
        return (_assemble_w_q(g_q), _concat_cols(g_kv, BF16, "assemble_w_kv"), _concat_cols(g_cw, F32, "assemble_conv_w"),
                g_om.reshape(D, D), g_oc.reshape(D, D), g_out.reshape(D, D))

    gnames = ["w_in", "w_q", "w_kv", "conv_w", "w_o_mla", "w_o_conv", "w_out"]
    (mx, my, mc), _, chips = _place()
    slots = jnp.stack([_slot(mx, my, mc)] + [_slot(cx, cy, mc) for cx, cy in chips]).astype(jnp.int32)

    def reduce_grads(r):
        own_in, own_in_bf16 = _split_dw_in(r["dw_in"])
        own = [own_in, _split_dw_q(r["dw_q"]), _split_cols(r["dw_kv"], NDEV, "split_dw_kv"),
               _split_cols(r["dconv_w"], NDEV, "split_dconv_w"), r["dw_o_mla"].reshape(NDEV, D // NDEV, D),
               r["dw_o_conv"].reshape(NDEV, D // NDEV, D), r["dw_out"].reshape(NDEV, D // NDEV, D)]
        slabs = [own_in_bf16] + [_slabs_bf16(g, 128, "bf16_" + nm) for g, nm in zip(own[1:], gnames[1:])]
        to_sibling = _slab_exchange(slabs, 4, _to_sibling)
        got1 = _run_exchange(to_sibling, "grads_to_sibling")
        sums = [_chip_sums(o, g1, slots, 128, "chip_sum_" + nm) for o, g1, nm in zip(own, got1, gnames)]
        return _slab_exchange(sums, 3, _to_chips), lambda got2: (own, got1, got2)

    row2 = lambda a: a.reshape(1, -1)
    r = _local_step(x[0], positions.reshape(S, 1), loss_target[0], row2(pre_norm_g), row2(q_a_norm_g), row2(kv_a_norm_g),
                    row2(post_norm_g), w_in_f, (gather_rest, assemble_rest), reduce_grads)
    own, got1, got2 = r["reduced"]
    small = jnp.concatenate([r["dg_pre"][0:1], r["dg_post"][0:1], jnp.pad(r["dg_qa"][0:1], ((0, 0), (0, D - RQ))),
                             jnp.pad(r["dg_kva"][0:1], ((0, 0), (0, D - RKV))), jnp.pad(r["loss"][0:1], ((0, 0), (0, D - 128))),
                             jnp.zeros((3, D), F32)], axis=0)
    p_small, = _all_gather([small], "all_gather_small")
    pad8 = lambda a: jnp.zeros((8, 256), F32).at[0:3, :].set(a)
    params = [(w_in, m_w_in, v_w_in), (w_q_b, m_w_q_b, v_w_q_b), (w_kv_b, m_w_kv_b, v_w_kv_b), (conv_pad, pad8(m_conv_w), pad8(v_conv_w)),
              (w_o_mla, m_w_o_mla, v_w_o_mla), (w_o_conv, m_w_o_conv, v_w_o_conv), (w_out, m_w_out, v_w_out)]
    o_q, o_kv, o_cw, o_om, o_oc, o_out = [
        _reduce_adamw(o, g1, g2, slots, w, m, v, 128, "adamw_" + nm)
        for o, g1, g2, (w, m, v), nm in list(zip(own, got1, got2, params, gnames))[1:]]
    o_cw = [a[0:3] for a in o_cw]
    o_in = [a.T for a in _reduce_adamw_t(own[0], got1[0], got2[0], slots, w_in.T, m_w_in.T, v_w_in.T, 128, "adamw_w_in")]
    padv = lambda a: jnp.pad(a.reshape(1, -1), ((0, 0), (0, D - a.shape[0])))
    stack = lambda pre, post, qa, kva: jnp.concatenate([row2(pre), row2(post), padv(qa), padv(kva), jnp.zeros((4, D), F32)], axis=0)
    o_g = _sum_adamw(p_small, stack(pre_norm_g, post_norm_g, q_a_norm_g, kv_a_norm_g),
                     stack(m_pre_norm_g, m_post_norm_g, m_q_a_norm_g, m_kv_a_norm_g),
                     stack(v_pre_norm_g, v_post_norm_g, v_q_a_norm_g, v_kv_a_norm_g), 8, "adamw_gains")
    loss = o_g[0][4, 0]
    outs = {}
    for idx, kind in enumerate(("grad", "delta", "new_m", "new_v")):
        o = o_g[idx]
        outs[kind] = dict(pre_norm_g=o[0], w_in=o_in[idx], q_a_norm_g=o[2, 0:RQ], w_q_b=o_q[idx], kv_a_norm_g=o[3, 0:RKV],
                          w_kv_b=o_kv[idx], conv_w=o_cw[idx], w_o_mla=o_om[idx], w_o_conv=o_oc[idx], w_out=o_out[idx], post_norm_g=o[1])
    names = ["pre_norm_g", "w_in", "q_a_norm_g", "w_q_b", "kv_a_norm_g", "w_kv_b", "conv_w", "w_o_mla", "w_o_conv", "w_out", "post_norm_g"]
    return (loss, r["grad_x"][None], *[outs["grad"][n] for n in names], *[outs["delta"][n] for n in names],
            *[outs["new_m"][n] for n in names], *[outs["new_v"][n] for n in names])
```

```python
import functools
import math

import jax
import jax.numpy as jnp
from jax import lax
from jax.experimental import pallas as pl
from jax.experimental.pallas import tpu as pltpu

F32 = jnp.float32
BF16 = jnp.bfloat16

NDEV = 8
D = 2048
H = 16
DN = 128
DR = 64
DV = 128
RQ = 512
RKV = 512
HW = 256
ROPE_THETA = 10000.0
RMS_EPS = 1e-6
N_IN = 15424
SHARD_IN = N_IN // NDEV
SMALL = RQ + RKV + DR
NP = 7 * D + RQ + RKV + 128
SEG = dict(z_mla=0, c_in=1, b_gate=2, c_gate=3, z_conv=4, g_mla=5, g_conv=6)
OFF_QA = 7 * D
OFF_CKV = OFF_QA + RQ
OFF_KR = OFF_CKV + RKV
EXT = 2176
VMEM_CAP = 56 * 1024 * 1024

ADAM_LR = 0.001
ADAM_B1 = 0.9
ADAM_B2 = 0.999
ADAM_EPS = 1e-08
ADAM_WD = 0.01
ADAM_STEP = 10

LOG2E = math.log2(math.e)
NN = (((1,), (0,)), ((), ()))
NT = (((1,), (1,)), ((), ()))
TN = (((0,), (0,)), ((), ()))
MESH = pl.DeviceIdType.MESH


def _cparams(sem, vmem_bytes):
    return pltpu.CompilerParams(dimension_semantics=sem, vmem_limit_bytes=int(min(VMEM_CAP, max(vmem_bytes, 16 << 20))))


def _nbytes(shape, dtype):
    return math.prod(shape) * jnp.dtype(dtype).itemsize


class _Exchange:
    def __init__(self, arrays, out_shapes, sem_shapes, start, finish):
        self.arrays, self.out_shapes, self.sem_shapes, self.start, self.finish = arrays, out_shapes, sem_shapes, start, finish
        self.n_sems = len(sem_shapes)


def _matmul(a, b, *, mode, out_dtype, tm, tn, tk, name, m_outer=False, exchange=None):
    if mode == "nn":
        (M, K), (K2, N) = a.shape, b.shape
    elif mode == "nt":
        (M, K), (N, K2) = a.shape, b.shape
    else:
        (K, M), (K2, N) = a.shape, b.shape
    assert K == K2, (a.shape, b.shape, mode)
    tm, tn, tk = min(tm, M), min(tn, N), min(tk, K)
    assert M % tm == 0 and N % tn == 0 and K % tk == 0, (M, N, K, tm, tn, tk)
    ni, nj, nk = M // tm, N // tn, K // tk
    dims = dict(nn=NN, nt=NT, tn=TN)[mode]

    if m_outer:
        grid = (ni, nj, nk)
        ij = lambda g0, g1: (g0, g1)
    else:
        grid = (nj, ni, nk)
        ij = lambda g0, g1: (g1, g0)

    if mode == "tn":
        a_spec = pl.BlockSpec((tk, tm), lambda g0, g1, k: (k, ij(g0, g1)[0]))
        a_tile = (tk, tm)
    else:
        a_spec = pl.BlockSpec((tm, tk), lambda g0, g1, k: (ij(g0, g1)[0], k))
        a_tile = (tm, tk)
    if mode == "nt":
        b_spec = pl.BlockSpec((tn, tk), lambda g0, g1, k: (ij(g0, g1)[1], k))
    else:
        b_spec = pl.BlockSpec((tk, tn), lambda g0, g1, k: (k, ij(g0, g1)[1]))
    o_spec = pl.BlockSpec((tm, tn), lambda g0, g1, k: ij(g0, g1))

    n_in = len(exchange.arrays) if exchange else 0

    def body(a_ref, b_ref, *refs):
        x_in, o_ref, x_out, scratch = refs[:n_in], refs[n_in], refs[n_in + 1:2 * n_in + 1], refs[2 * n_in + 1:]
        if exchange:
            sems = scratch[len(scratch) - exchange.n_sems:]
            step = (pl.program_id(0) * grid[1] + pl.program_id(1)) * grid[2] + pl.program_id(2)

            @pl.when(step == 0)
            def _():
                exchange.start(x_in, x_out, sems)

        prod = lax.dot_general(a_ref[...], b_ref[...], dims, preferred_element_type=F32)
        if nk == 1:
            o_ref[...] = prod.astype(o_ref.dtype)
        else:
            acc_ref = scratch[0]
            k = pl.program_id(2)

            @pl.when(k == 0)
            def _():
                acc_ref[...] = prod

            @pl.when((k > 0) & (k < nk - 1))
            def _():
                acc_ref[...] += prod

            @pl.when(k == nk - 1)
            def _():
                o_ref[...] = (acc_ref[...] + prod).astype(o_ref.dtype)

        if exchange:
            @pl.when(step == grid[0] * grid[1] * grid[2] - 1)
            def _():
                exchange.finish(x_in, x_out, sems)

    vmem = 2 * (_nbytes(a_tile, a.dtype) + _nbytes((tk, tn), b.dtype) + _nbytes((tm, tn), out_dtype)) + 2 * _nbytes((tm, tn), F32)
    anyspec = pl.BlockSpec(memory_space=pl.ANY)
    outs = pl.pallas_call(
        body,
        grid=grid,
        in_specs=[a_spec, b_spec] + [anyspec] * n_in,
        out_specs=[o_spec] + [anyspec] * n_in,
        out_shape=[jax.ShapeDtypeStruct((M, N), out_dtype)] + (exchange.out_shapes if exchange else []),
        scratch_shapes=([] if nk == 1 else [pltpu.VMEM((tm, tn), F32)]) + (exchange.sem_shapes if exchange else []),
        compiler_params=_cparams(("arbitrary",) * 3 if exchange else ("parallel", "parallel", "arbitrary"), vmem + (8 << 20)),
        name=name,
    )(a, b, *(exchange.arrays if exchange else []))
    return (outs[0], outs[1:]) if exchange else outs[0]


def _in_dest(k):
    return SHARD_IN * k - SMALL


def _assemble_w_in(g):
    R = 128

    def body(g_ref, o_ref, ext, acc):
        ext[...] = jnp.zeros_like(ext)
        acc[...] = jnp.zeros_like(acc)
        lane = lax.broadcasted_iota(jnp.int32, (R, EXT), 1)
        ext[:, 0:SHARD_IN] = g_ref[0].astype(F32)
        v = ext[...]
        acc[:, OFF_QA:NP] = jnp.where(lane[:, 0:NP - OFF_QA] < SMALL, v[:, 0:NP - OFF_QA], 0.0)
        w = v[:, 1024:2048]
        w = pltpu.roll(w, 1024 - 64, 1)
        acc[:, 0:1024] = jnp.where(lane[:, 0:1024] < SHARD_IN - SMALL, w, 0.0)
        for k in range(1, NDEV):
            ext[:, 0:SHARD_IN] = g_ref[k].astype(F32)
            dest = _in_dest(k)
            t, o = dest // 128, dest % 128
            width = -(-(o + SHARD_IN) // 128) * 128
            v = pltpu.roll(ext[...], o, 1)[:, 0:width]
            acc[:, 128 * t:128 * t + width] += v
        o_ref[...] = acc[...].astype(BF16)

    return pl.pallas_call(
        body,
        grid=(D // R,),
        in_specs=[pl.BlockSpec((NDEV, R, SHARD_IN), lambda i: (0, i, 0))],
        out_specs=pl.BlockSpec((R, NP), lambda i: (i, 0)),
        out_shape=jax.ShapeDtypeStruct((D, NP), BF16),
        scratch_shapes=[pltpu.VMEM((R, EXT), F32), pltpu.VMEM((R, NP), F32)],
        compiler_params=_cparams(("parallel",), 40 << 20),
        name="assemble_w_in",
    )(g)


def _split_dw_in(dw, slots):
    R = 128

    def body(slots_ref, dw_ref, o_ref, ob_ref):
        def put(k, cols, v):
            ob_ref[k, :, cols] = v.astype(BF16)

            @pl.when(slots_ref[0] == k)
            def _():
                o_ref[:, cols] = v

        lane = lax.broadcasted_iota(jnp.int32, (R, 1024), 1)
        put(0, slice(0, 1024), dw_ref[:, OFF_QA:OFF_QA + 1024])
        tail = dw_ref[:, OFF_QA + 1024:NP]
        tail = jnp.concatenate([tail, jnp.zeros((R, 1024 - 128), F32)], axis=1)
        head = dw_ref[:, 0:1024]
        mixed = jnp.where(lane < 64, tail, pltpu.roll(head, 64, 1))
        put(0, slice(1024, SHARD_IN), mixed[:, 0:SHARD_IN - 1024])
        for k in range(1, NDEV):
            dest = _in_dest(k)
            t, o = dest // 128, dest % 128
            width = -(-(o + SHARD_IN) // 128) * 128
            v = dw_ref[:, 128 * t:128 * t + width]
            v = pltpu.roll(v, width - o, 1)
            put(k, slice(0, SHARD_IN), v[:, 0:SHARD_IN])

    return pl.pallas_call(
        body,
        grid_spec=pltpu.PrefetchScalarGridSpec(
            num_scalar_prefetch=1, grid=(D // R,),
            in_specs=[pl.BlockSpec((R, NP), lambda i, s: (i, 0))],
            out_specs=[pl.BlockSpec((R, SHARD_IN), lambda i, s: (i, 0)), pl.BlockSpec((NDEV, R, SHARD_IN), lambda i, s: (0, i, 0))]),
        out_shape=[jax.ShapeDtypeStruct((D, SHARD_IN), F32), jax.ShapeDtypeStruct((NDEV, D, SHARD_IN), BF16)],
        compiler_params=_cparams(("parallel",), 48 << 20),
        name="split_dw_in",
    )(slots, dw)


def _assemble_w_q(g):
    def body(g_ref, o_ref):
        lane = lax.broadcasted_iota(jnp.int32, (RQ, 128), 1)
        lo = lane < 64
        for k in range(NDEV):
            t0 = g_ref[k, :, 0:128].astype(F32)
            t1 = g_ref[k, :, 128:256].astype(F32)
            t2 = g_ref[k, :, 256:384].astype(F32)
            base = 2 * k * HW
            o_ref[:, base:base + 128] = t0.astype(BF16)
            o_ref[:, base + 128:base + 256] = jnp.where(lo, t1, 0.0).astype(BF16)
            o_ref[:, base + 256:base + 384] = pltpu.roll(jnp.where(lo, t2, t1), 64, 1).astype(BF16)
            o_ref[:, base + 384:base + 512] = jnp.where(lo, pltpu.roll(t2, 64, 1), 0.0).astype(BF16)

    return pl.pallas_call(
        body,
        out_shape=jax.ShapeDtypeStruct((RQ, H * HW), BF16),
        compiler_params=_cparams(None, 32 << 20),
        name="assemble_w_q",
    )(g)


def _split_dw_q(dw):
    def body(dw_ref, o_ref):
        lane = lax.broadcasted_iota(jnp.int32, (RQ, 128), 1)
        lo = lane < 64
        for k in range(NDEV):
            base = 2 * k * HW
            a = dw_ref[:, base:base + 128]
            b = dw_ref[:, base + 128:base + 256]
            c = pltpu.roll(dw_ref[:, base + 256:base + 384], 64, 1)
            d = pltpu.roll(dw_ref[:, base + 384:base + 512], 64, 1)
            o_ref[k, :, 0:128] = a
            o_ref[k, :, 128:256] = jnp.where(lo, b, c)
            o_ref[k, :, 256:384] = jnp.where(lo, c, d)

    return pl.pallas_call(
        body,
        out_shape=jax.ShapeDtypeStruct((NDEV, RQ, 384), F32),
        compiler_params=_cparams(None, 32 << 20),
        name="split_dw_q",
    )(dw)


def _concat_cols(g, dtype, name):
    n, R, C = g.shape

    def body(g_ref, o_ref):
        for k in range(n):
            o_ref[:, k * C:(k + 1) * C] = g_ref[k].astype(dtype)

    return pl.pallas_call(body, out_shape=jax.ShapeDtypeStruct((R, n * C), dtype), compiler_params=_cparams(None, 32 << 20), name=name)(g)


def _split_cols(x, n, name):
    R, NC = x.shape
    C = NC // n

    def body(x_ref, o_ref):
        for k in range(n):
            o_ref[k] = x_ref[:, k * C:(k + 1) * C]

    return pl.pallas_call(body, out_shape=jax.ShapeDtypeStruct((n, R, C), x.dtype), compiler_params=_cparams(None, 32 << 20), name=name)(x)


def _rms_scale(xf):
    return lax.rsqrt(jnp.mean(xf * xf, axis=-1, keepdims=True) + RMS_EPS)


def _prenorm(x, g, ts):
    S = x.shape[0]

    def body(x_ref, g_ref, h_ref, ht_ref):
        xf = x_ref[...]
        h = xf * _rms_scale(xf) * g_ref[...]
        h_ref[...] = h.astype(BF16)
        ht_ref[...] = jnp.transpose(h).astype(BF16)

    return pl.pallas_call(
        body,
        grid=(S // ts,),
        in_specs=[pl.BlockSpec((ts, D), lambda i: (i, 0)), pl.BlockSpec((1, D), lambda i: (0, 0))],
        out_specs=[pl.BlockSpec((ts, D), lambda i: (i, 0)), pl.BlockSpec((D, ts), lambda i: (0, i))],
        out_shape=[jax.ShapeDtypeStruct((S, D), BF16), jax.ShapeDtypeStruct((D, S), BF16)],
        compiler_params=_cparams(("parallel",), 32 << 20),
        name="prenorm",
    )(x, g)


def _inv_freq_tile():
    inv_freq = ROPE_THETA ** (-jnp.arange(0, DR, 2, dtype=F32) / DR)
    return jnp.tile(inv_freq, 4).reshape(1, 128)


def _rope_tables(pos_ref, freq_ref, ts):
    lane = lax.broadcasted_iota(jnp.int32, (ts, 128), 1)
    ang = pos_ref[...].astype(F32) * freq_ref[...]
    return jnp.cos(ang), jnp.sin(ang), lane


def _rope_swap(t, lane):
    return jnp.where(lane < 32, pltpu.roll(t, 96, 1), pltpu.roll(t, 32, 1))


def _qkv_prep(proj, pos, freq, g_qa, g_kva, w_q, w_kv, ts):
    S = proj.shape[0]

    def body(qa_ref, ckv_ref, kr_ref, pos_ref, freq_ref, gq_ref, gk_ref, wq_ref, wkv_ref, q_ref, k_ref, v_ref, vt_ref, qn_ref, kvn_ref):
        qa = _f32(qa_ref)
        qn = (qa * _rms_scale(qa) * gq_ref[...]).astype(BF16)
        ckv = _f32(ckv_ref)
        kvn = (ckv * _rms_scale(ckv) * gk_ref[...]).astype(BF16)
        qn_ref[...] = qn
        kvn_ref[...] = kvn
        cos, sin, lane = _rope_tables(pos_ref, freq_ref, ts)
        sgn_sin = jnp.where(lane < 32, -sin, sin)
        live = lane < DR
        kr = _f32(kr_ref)
        kr = jnp.where(live, kr * cos + _rope_swap(kr, lane) * sgn_sin, 0.0).astype(BF16)
        qf = jnp.dot(qn, wq_ref[...], preferred_element_type=F32)
        kvf = jnp.dot(kvn, wkv_ref[...], preferred_element_type=F32)
        for h in range(H):
            q_ref[h, :, 0:DN] = qf[:, h * HW:h * HW + DN].astype(BF16)
            t = qf[:, h * HW + DN:(h + 1) * HW]
            q_ref[h, :, DN:HW] = jnp.where(live, t * cos + _rope_swap(t, lane) * sgn_sin, 0.0).astype(BF16)
            k_ref[h, :, 0:DN] = kvf[:, h * HW:h * HW + DN].astype(BF16)
            k_ref[h, :, DN:HW] = kr
            vh = kvf[:, h * HW + DN:(h + 1) * HW]
            v_ref[h] = vh.astype(BF16)
            vt_ref[h] = jnp.transpose(vh).astype(BF16)

    row = lambda w, blk: pl.BlockSpec((ts, w), lambda i: (i, blk))
    full = lambda a: pl.BlockSpec(a.shape, lambda i: (0,) * a.ndim)
    return pl.pallas_call(
        body,
        grid=(S // ts,),
        in_specs=[row(RQ, OFF_QA // RQ), row(RKV, OFF_CKV // RKV), row(128, OFF_KR // 128),
                  pl.BlockSpec((ts, 1), lambda i: (i, 0)), full(freq), full(g_qa), full(g_kva), full(w_q), full(w_kv)],
        out_specs=[pl.BlockSpec((H, ts, HW), lambda i: (0, i, 0)), pl.BlockSpec((H, ts, HW), lambda i: (0, i, 0)),
                   pl.BlockSpec((H, ts, DV), lambda i: (0, i, 0)), pl.BlockSpec((H, DV, ts), lambda i: (0, 0, i)),
                   pl.BlockSpec((ts, RQ), lambda i: (i, 0)), pl.BlockSpec((ts, RKV), lambda i: (i, 0))],
        out_shape=[jax.ShapeDtypeStruct((H, S, HW), BF16), jax.ShapeDtypeStruct((H, S, HW), BF16),
                   jax.ShapeDtypeStruct((H, S, DV), BF16), jax.ShapeDtypeStruct((H, DV, S), BF16),
                   jax.ShapeDtypeStruct((S, RQ), BF16), jax.ShapeDtypeStruct((S, RKV), BF16)],
        compiler_params=_cparams(("parallel",), 48 << 20),
        name="qkv_prep",
    )(proj, proj, proj, pos, freq, g_qa, g_kva, w_q, w_kv)


def _col_to_row8(col, n):
    return jnp.transpose(jnp.broadcast_to(col, (n, 128)))[0:8, :]


def _causal_pairs(n, key_major):
    if key_major:
        pairs = [(i, j) for j in range(n) for i in range(j, n)]
    else:
        pairs = [(i, j) for i in range(n) for j in range(i + 1)]
    return jnp.array([p[0] for p in pairs], jnp.int32), jnp.array([p[1] for p in pairs], jnp.int32)


def _flash_fwd(q, k, vt, tq, hb=8):
    S = q.shape[1]
    nq = S // tq
    scale = 1.0 / math.sqrt(DN + DR)
    c2 = scale * math.log2(math.e)
    tsub = tq
    nsub = tq // tsub

    qi_tab, kj_tab = _causal_pairs(nq, key_major=False)

    def body(qi_ref, kj_ref, q_ref, k_ref, vt_ref, o_ref, lse_ref, m_sc, l_sc, acc_sc):
        p = pl.program_id(1)
        qi, kj = qi_ref[p], kj_ref[p]

        @pl.when(kj == 0)
        def _():
            m_sc[...] = jnp.full_like(m_sc, -jnp.inf)
            l_sc[...] = jnp.zeros_like(l_sc)
            acc_sc[...] = jnp.zeros_like(acc_sc)

        def step(diag):
            chains = [(h, u) for h in range(hb) for u in range(nsub)]

            def scores(h, u):
                return lax.dot_general(k_ref[h], q_ref[h, u * tsub:(u + 1) * tsub, :], NT, preferred_element_type=F32)

            st_next = scores(*chains[0])
            for ci, (h, u) in enumerate(chains):
                st = st_next
                if ci + 1 < len(chains):
                    st_next = scores(*chains[ci + 1])
                cols = slice(u * tsub, (u + 1) * tsub)
                if diag:
                    r = lax.broadcasted_iota(jnp.int32, (tq, tsub), 0)
                    c = lax.broadcasted_iota(jnp.int32, (tq, tsub), 1) + u * tsub
                    st = jnp.where(r <= c, st, -jnp.inf)
                m_prev = m_sc[h, 0:1, cols]
                m_new = jnp.maximum(m_prev, jnp.max(st, axis=0, keepdims=True))
                alpha = jnp.exp2((m_prev - m_new) * c2)
                pt = jnp.exp2((st - m_new) * c2)
                l_sc[h, :, cols] = jnp.broadcast_to(alpha * l_sc[h, 0:1, cols] + jnp.sum(pt, axis=0, keepdims=True), (8, tsub))
                m_sc[h, :, cols] = jnp.broadcast_to(m_new, (8, tsub))
                acc_sc[h, :, cols] = alpha * acc_sc[h, :, cols] + jnp.dot(vt_ref[h], pt.astype(BF16), preferred_element_type=F32)

        @pl.when(kj < qi)
        def _():
            step(False)

        @pl.when(kj == qi)
        def _():
            step(True)
            for h in range(hb):
                l = l_sc[h, 0:1, :]
                o_ref[:, h * DV:(h + 1) * DV] = jnp.transpose(acc_sc[h] / l).astype(BF16)
                lse_ref[h] = m_sc[h] * scale + jnp.log(l_sc[h])

    return pl.pallas_call(
        body,
        grid_spec=pltpu.PrefetchScalarGridSpec(
            num_scalar_prefetch=2,
            grid=(H // hb, len(qi_tab)),
            in_specs=[pl.BlockSpec((hb, tq, HW), lambda h, p, qi, kj: (h, qi[p], 0)),
                      pl.BlockSpec((hb, tq, HW), lambda h, p, qi, kj: (h, kj[p], 0)),
                      pl.BlockSpec((hb, DV, tq), lambda h, p, qi, kj: (h, 0, kj[p]))],
            out_specs=[pl.BlockSpec((tq, hb * DV), lambda h, p, qi, kj: (qi[p], h)),
                       pl.BlockSpec((hb, 8, tq), lambda h, p, qi, kj: (h, 0, qi[p]))],
            scratch_shapes=[pltpu.VMEM((hb, 8, tq), F32), pltpu.VMEM((hb, 8, tq), F32), pltpu.VMEM((hb, DV, tq), F32)],
        ),
        out_shape=[jax.ShapeDtypeStruct((S, H * DV), BF16), jax.ShapeDtypeStruct((H, 8, S), F32)],
        compiler_params=_cparams(("parallel", "arbitrary"), 40 << 20),
        name="flash_fwd",
    )(qi_tab, kj_tab, q, k, vt)


def _sigmoid(x):
    return 1.0 / (1.0 + jnp.exp(-x))


HALO = 16


def _f32(ref):
    return ref[...].astype(F32)


def _shift_rows(u, prev, n, first):
    ts = u.shape[0]
    row = lax.broadcasted_iota(jnp.int32, u.shape, 0)
    out = pltpu.roll(u, n, 0)
    for j in range(n):
        halo = jnp.where(first, 0.0, prev[HALO - n + j:HALO - n + j + 1, :])
        out = jnp.where(row == j, halo, out)
    return out


def _gates_fwd(proj, attn, conv_w, ts):
    S = proj.shape[0]

    def body(attn_ref, zm_ref, cin_ref, bg_ref, cg_ref, zc_ref, cin_p, cg_p, w_ref, am_ref, ac_ref):
        first = pl.program_id(0) == 0
        zm = _f32(zm_ref)
        am_ref[...] = (_f32(attn_ref) * (zm * _sigmoid(zm))).astype(BF16)
        u = _f32(cg_ref) * _f32(cin_ref)
        up = _f32(cg_p) * _f32(cin_p)
        w = w_ref[...]
        conv = w[0:1, :] * _shift_rows(u, up, 2, first) + w[1:2, :] * _shift_rows(u, up, 1, first) + w[2:3, :] * u
        zc = _f32(zc_ref)
        ac_ref[...] = (_f32(bg_ref) * conv * (zc * _sigmoid(zc))).astype(BF16)

    seg = lambda name: pl.BlockSpec((ts, D), lambda i: (i, SEG[name]))
    prev = lambda name: pl.BlockSpec((HALO, D), lambda i: (jnp.maximum(i * (ts // HALO) - 1, 0), SEG[name]))
    return pl.pallas_call(
        body,
        grid=(S // ts,),
        in_specs=[pl.BlockSpec((ts, D), lambda i: (i, 0)), seg("z_mla"), seg("c_in"), seg("b_gate"), seg("c_gate"), seg("z_conv"),
                  prev("c_in"), prev("c_gate"), pl.BlockSpec((8, D), lambda i: (0, 0))],
        out_specs=[pl.BlockSpec((ts, D), lambda i: (i, 0))] * 2,
        out_shape=[jax.ShapeDtypeStruct((S, D), BF16)] * 2,
        compiler_params=_cparams(("arbitrary",), 48 << 20),
        name="gates_fwd",
    )(attn, proj, proj, proj, proj, proj, proj, proj, conv_w)


def _merge_fwd(proj, y_mla, y_conv, ts):
    S = proj.shape[0]

    def body(gm_ref, gc_ref, ym_ref, yc_ref, o_ref):
        o_ref[...] = (_sigmoid(_f32(gm_ref)) * _f32(ym_ref) + _sigmoid(_f32(gc_ref)) * _f32(yc_ref)).astype(BF16)

    seg = lambda name: pl.BlockSpec((ts, D), lambda i: (i, SEG[name]))
    row = pl.BlockSpec((ts, D), lambda i: (i, 0))
    return pl.pallas_call(
        body, grid=(S // ts,), in_specs=[seg("g_mla"), seg("g_conv"), row, row], out_specs=row,
        out_shape=jax.ShapeDtypeStruct((S, D), BF16), compiler_params=_cparams(("parallel",), 32 << 20), name="merge_fwd",
    )(proj, proj, y_mla, y_conv)


def _post_loss(out, x, target, g_post, ts):
    S = out.shape[0]

    def body(o_ref, x_ref, t_ref, g_ref, dy_ref, do_ref, dg_ref, loss_ref):
        i = pl.program_id(0)
        o = o_ref[...]
        r = _rms_scale(o)
        n = o * r
        g = g_ref[...]
        err = x_ref[...] + n * g - t_ref[...]
        dy = err * (1.0 / D)
        dy_ref[...] = dy
        dn = dy * g
        do_ref[...] = (r * (dn - n * jnp.mean(dn * n, axis=-1, keepdims=True))).astype(BF16)
        dg = jnp.sum(dy * n, axis=0, keepdims=True)
        part = jnp.sum(jnp.sum(err * err, axis=0, keepdims=True), axis=1, keepdims=True) * (0.5 / D)

        @pl.when(i == 0)
        def _():
            dg_ref[...] = jnp.zeros_like(dg_ref)
            loss_ref[...] = jnp.zeros_like(loss_ref)

        dg_ref[0:1, :] += dg
        loss_ref[...] += jnp.broadcast_to(part, loss_ref.shape)

    row = pl.BlockSpec((ts, D), lambda i: (i, 0))
    return pl.pallas_call(
        body, grid=(S // ts,),
        in_specs=[row, row, row, pl.BlockSpec((1, D), lambda i: (0, 0))],
        out_specs=[row, row, pl.BlockSpec((8, D), lambda i: (0, 0)), pl.BlockSpec((8, 128), lambda i: (0, 0))],
        out_shape=[jax.ShapeDtypeStruct((S, D), F32), jax.ShapeDtypeStruct((S, D), BF16),
                   jax.ShapeDtypeStruct((8, D), F32), jax.ShapeDtypeStruct((8, 128), F32)],
        compiler_params=_cparams(("arbitrary",), 40 << 20), name="post_loss",
    )(out, x, target, g_post)


def _merge_bwd(proj, y_mla, y_conv, dmerged, ts):
    S = proj.shape[0]

    def body(gm_ref, gc_ref, ym_ref, yc_ref, dm_ref, dym_ref, dyc_ref, dgm_ref, dgc_ref):
        dm = _f32(dm_ref)
        sm = _sigmoid(_f32(gm_ref))
        sc = _sigmoid(_f32(gc_ref))
        dym_ref[...] = (dm * sm).astype(BF16)
        dyc_ref[...] = (dm * sc).astype(BF16)
        dgm_ref[...] = (dm * _f32(ym_ref) * (sm * (1.0 - sm))).astype(BF16)
        dgc_ref[...] = (dm * _f32(yc_ref) * (sc * (1.0 - sc))).astype(BF16)

    seg = lambda name: pl.BlockSpec((ts, D), lambda i: (i, SEG[name]))
    row = pl.BlockSpec((ts, D), lambda i: (i, 0))
    return pl.pallas_call(
        body, grid=(S // ts,), in_specs=[seg("g_mla"), seg("g_conv"), row, row, row], out_specs=[row] * 4,
        out_shape=[jax.ShapeDtypeStruct((S, D), BF16)] * 4, compiler_params=_cparams(("parallel",), 40 << 20), name="merge_bwd",
    )(proj, proj, y_mla, y_conv, dmerged)


def _gates_bwd(proj, attn, da_mla, da_conv, conv_w, ts):
    S = proj.shape[0]
    nblk = S // ts

    def body(attn_ref, zm_ref, cin_ref, bg_ref, cg_ref, zc_ref, dam_ref, dac_ref, cin_p, cg_p, bg_n, zc_n, dac_n, w_ref,
             dattn_ref, delta_ref, dzm_ref, dcin_ref, dbg_ref, dcg_ref, dzc_ref, dw_ref):
        i = pl.program_id(0)
        first = i == 0
        last = i == nblk - 1

        @pl.when(first)
        def _():
            dw_ref[...] = jnp.zeros_like(dw_ref)

        row = lax.broadcasted_iota(jnp.int32, (ts, DV), 0)
        for c in range(D // DV):
            cols = slice(c * DV, (c + 1) * DV)
            ld = lambda ref: ref[:, cols].astype(F32)
            zm = ld(zm_ref)
            sg = _sigmoid(zm)
            attn = ld(attn_ref)
            dam = ld(dam_ref)
            dattn = dam * (zm * sg)
            dattn_ref[:, cols] = dattn.astype(BF16)
            dzm_ref[:, cols] = (dam * attn * (sg * (1.0 + zm * (1.0 - sg)))).astype(BF16)
            delta_ref[c] = _col_to_row8(jnp.sum(dattn * attn, axis=1, keepdims=True), ts)
            w = w_ref[:, cols]
            cin, cg, bg, zc = ld(cin_ref), ld(cg_ref), ld(bg_ref), ld(zc_ref)
            u = cg * cin
            up = ld(cg_p) * ld(cin_p)
            u1 = _shift_rows(u, up, 1, first)
            u2 = _shift_rows(u, up, 2, first)
            conv = w[0:1, :] * u2 + w[1:2, :] * u1 + w[2:3, :] * u
            sgc = _sigmoid(zc)
            siluc = zc * sgc
            dac = ld(dac_ref)
            dbg_ref[:, cols] = (dac * conv * siluc).astype(BF16)
            dzc_ref[:, cols] = (dac * bg * conv * (sgc * (1.0 + zc * (1.0 - sgc)))).astype(BF16)
            dconv = dac * bg * siluc
            zn = ld(zc_n)
            dconv_n = jnp.where(last, 0.0, ld(dac_n) * ld(bg_n) * (zn * _sigmoid(zn)))
            d1 = jnp.where(row == ts - 1, dconv_n[0:1, :], pltpu.roll(dconv, ts - 1, 0))
            d2 = jnp.where(row == ts - 1, dconv_n[1:2, :], jnp.where(row == ts - 2, dconv_n[0:1, :], pltpu.roll(dconv, ts - 2, 0)))
            du = w[2:3, :] * dconv + w[1:2, :] * d1 + w[0:1, :] * d2
            dcg_ref[:, cols] = (du * cin).astype(BF16)
            dcin_ref[:, cols] = (du * cg).astype(BF16)
            dw_ref[0:1, cols] += jnp.sum(dconv * u2, axis=0, keepdims=True)
            dw_ref[1:2, cols] += jnp.sum(dconv * u1, axis=0, keepdims=True)
            dw_ref[2:3, cols] += jnp.sum(dconv * u, axis=0, keepdims=True)

    seg = lambda name: pl.BlockSpec((ts, D), lambda i: (i, SEG[name]))
    prev = lambda name: pl.BlockSpec((HALO, D), lambda i: (jnp.maximum(i * (ts // HALO) - 1, 0), SEG[name]))
    nxt = lambda blk: pl.BlockSpec((HALO, D), lambda i: (jnp.minimum((i + 1) * (ts // HALO), S // HALO - 1), blk))
    row = pl.BlockSpec((ts, D), lambda i: (i, 0))
    return pl.pallas_call(
        body, grid=(nblk,),
        in_specs=[row, seg("z_mla"), seg("c_in"), seg("b_gate"), seg("c_gate"), seg("z_conv"), row, row,
                  prev("c_in"), prev("c_gate"), nxt(SEG["b_gate"]), nxt(SEG["z_conv"]), nxt(0), pl.BlockSpec((8, D), lambda i: (0, 0))],
        out_specs=[row, pl.BlockSpec((H, 8, ts), lambda i: (0, 0, i)), row, row, row, row, row, pl.BlockSpec((8, D), lambda i: (0, 0))],
        out_shape=[jax.ShapeDtypeStruct((S, D), BF16), jax.ShapeDtypeStruct((H, 8, S), F32)] + [jax.ShapeDtypeStruct((S, D), BF16)] * 5
        + [jax.ShapeDtypeStruct((8, D), F32)],
        compiler_params=_cparams(("arbitrary",), 56 << 20), name="gates_bwd",
    )(attn, proj, proj, proj, proj, proj, da_mla, da_conv, proj, proj, proj, proj, da_conv, conv_w)


def _flash_bwd(q, k, v, do, lse, delta, tq, hb=2):
    S = q.shape[1]
    nq = S // tq
    scale = 1.0 / math.sqrt(DN + DR)
    c2 = scale * LOG2E
    tsub = min(256, tq)
    nsub = tq // tsub

    qi_tab, kj_tab = _causal_pairs(nq, key_major=True)
    npairs = nq * (nq + 1) // 2

    def body(qi_ref, kj_ref, q_ref, k_ref, v_ref, do_ref, lse_ref, dl_ref, dq_ref, dk_ref, dv_ref, dk_sc, dv_sc, dq_sc):
        p = pl.program_id(1)
        qi, kj = qi_ref[p], kj_ref[p]

        @pl.when(p == 0)
        def _():
            dq_sc[...] = jnp.zeros_like(dq_sc)

        @pl.when(qi == kj)
        def _():
            dk_sc[...] = jnp.zeros_like(dk_sc)
            dv_sc[...] = jnp.zeros_like(dv_sc)

        def step(diag):
            chains = [(h, u) for h in range(hb) for u in range(nsub)]

            def first_matmuls(h, u):
                sub = slice(u * tsub, (u + 1) * tsub)
                st = lax.dot_general(k_ref[h], q_ref[h, sub, :], NT, preferred_element_type=F32)
                dpt = lax.dot_general(v_ref[h], do_ref[sub, h * DV:(h + 1) * DV], NT, preferred_element_type=F32)
                return st, dpt

            nxt = first_matmuls(*chains[0])
            for ci, (h, u) in enumerate(chains):
                st, dpt = nxt
                if ci + 1 < len(chains):
                    nxt = first_matmuls(*chains[ci + 1])
                sub = slice(u * tsub, (u + 1) * tsub)
                pt = jnp.exp2(st * c2 - lse_ref[h, 0:1, sub] * LOG2E)
                if diag:
                    r = lax.broadcasted_iota(jnp.int32, (tq, tsub), 0)
                    c = lax.broadcasted_iota(jnp.int32, (tq, tsub), 1) + u * tsub
                    pt = jnp.where(r <= c, pt, 0.0)
                dst = (pt * (dpt - dl_ref[h, 0:1, sub])).astype(BF16)
                dv_sc[h] += jnp.dot(pt.astype(BF16), do_ref[sub, h * DV:(h + 1) * DV], preferred_element_type=F32)
                dk_sc[h] += jnp.dot(dst, q_ref[h, sub, :], preferred_element_type=F32)
                rows = pl.ds(pl.multiple_of(qi * tq + u * tsub, tsub), tsub)
                dq_sc[h, rows, :] += lax.dot_general(dst, k_ref[h], TN, preferred_element_type=F32)

        @pl.when(qi > kj)
        def _():
            step(False)

        @pl.when(qi == kj)
        def _():
            step(True)

        @pl.when(qi == nq - 1)
        def _():
            dk_ref[...] = (dk_sc[...] * scale).astype(BF16)
            dv_ref[...] = dv_sc[...].astype(BF16)

        @pl.when(p == npairs - 1)
        def _():
            dq_ref[...] = (dq_sc[...] * scale).astype(BF16)

    return pl.pallas_call(
        body,
        grid_spec=pltpu.PrefetchScalarGridSpec(
            num_scalar_prefetch=2,
            grid=(H // hb, npairs),
            in_specs=[pl.BlockSpec((hb, tq, HW), lambda h, p, qi, kj: (h, qi[p], 0)),
                      pl.BlockSpec((hb, tq, HW), lambda h, p, qi, kj: (h, kj[p], 0)),
                      pl.BlockSpec((hb, tq, DV), lambda h, p, qi, kj: (h, kj[p], 0)),
                      pl.BlockSpec((tq, hb * DV), lambda h, p, qi, kj: (qi[p], h)),
                      pl.BlockSpec((hb, 8, tq), lambda h, p, qi, kj: (h, 0, qi[p])),
                      pl.BlockSpec((hb, 8, tq), lambda h, p, qi, kj: (h, 0, qi[p]))],
            out_specs=[pl.BlockSpec((hb, S, HW), lambda h, p, qi, kj: (h, 0, 0)),
                       pl.BlockSpec((hb, tq, HW), lambda h, p, qi, kj: (h, kj[p], 0)),
                       pl.BlockSpec((hb, tq, DV), lambda h, p, qi, kj: (h, kj[p], 0))],
            scratch_shapes=[pltpu.VMEM((hb, tq, HW), F32), pltpu.VMEM((hb, tq, DV), F32), pltpu.VMEM((hb, S, HW), F32)],
        ),
        out_shape=[jax.ShapeDtypeStruct((H, S, HW), BF16), jax.ShapeDtypeStruct((H, S, HW), BF16), jax.ShapeDtypeStruct((H, S, DV), BF16)],
        compiler_params=_cparams(("parallel", "arbitrary"), VMEM_CAP),
        name="flash_bwd",
    )(qi_tab, kj_tab, q, k, v, do, lse, delta)


def _rms_bwd(xf, g, dn_out):
    r = _rms_scale(xf)
    n = xf * r
    dn = dn_out * g
    dx = r * (dn - n * jnp.mean(dn * n, axis=-1, keepdims=True))
    return dx, jnp.sum(dn_out * n, axis=0, keepdims=True)


def _qkv_bwd(proj, pos, freq, g_qa, g_kva, w_q, w_kv, dq, dk, dv, segments, ts):
    S = proj.shape[0]
    nseg = len(segments)

    def body(qa_ref, ckv_ref, pos_ref, freq_ref, gq_ref, gk_ref, wq_ref, wkv_ref, dq_ref, dk_ref, dv_ref, *rest):
        seg_refs, (dqp_ref, dkvp_ref, dproj_ref, dgq_ref, dgk_ref) = rest[:nseg], rest[nseg:]
        for j, seg_ref in enumerate(seg_refs):
            dproj_ref[:, j * D:(j + 1) * D] = seg_ref[...]
        dqa_ref = dproj_ref.at[:, OFF_QA:OFF_QA + RQ]
        dckv_ref = dproj_ref.at[:, OFF_CKV:OFF_CKV + RKV]
        dkr_ref = dproj_ref.at[:, OFF_KR:OFF_KR + 128]
        i = pl.program_id(0)
        cos, sin, lane = _rope_tables(pos_ref, freq_ref, ts)
        sgn_sin = jnp.where(lane < 32, sin, -sin)
        live = lane < DR
        kr_sum = jnp.zeros((ts, 128), F32)
        for h in range(H):
            dqp_ref[:, h * HW:h * HW + DN] = dq_ref[h, :, 0:DN]
            t = dq_ref[h, :, DN:HW].astype(F32)
            dqp_ref[:, h * HW + DN:(h + 1) * HW] = jnp.where(live, t * cos + _rope_swap(t, lane) * sgn_sin, 0.0).astype(BF16)
            dkvp_ref[:, h * HW:h * HW + DN] = dk_ref[h, :, 0:DN]
            dkvp_ref[:, h * HW + DN:(h + 1) * HW] = dv_ref[h]
            kr_sum = kr_sum + dk_ref[h, :, DN:HW].astype(F32)
        dkr_ref[...] = jnp.where(live, kr_sum * cos + _rope_swap(kr_sum, lane) * sgn_sin, 0.0).astype(BF16)
        dqn = lax.dot_general(dqp_ref[...], wq_ref[...], NT, preferred_element_type=F32)
        dkvn = lax.dot_general(dkvp_ref[...], wkv_ref[...], NT, preferred_element_type=F32)
        dqa, dgq = _rms_bwd(_f32(qa_ref), gq_ref[...], dqn)
        dckv, dgk = _rms_bwd(_f32(ckv_ref), gk_ref[...], dkvn)
        dqa_ref[...] = dqa.astype(BF16)
        dckv_ref[...] = dckv.astype(BF16)

        @pl.when(i == 0)
        def _():
            dgq_ref[...] = jnp.zeros_like(dgq_ref)
            dgk_ref[...] = jnp.zeros_like(dgk_ref)

        dgq_ref[0:1, :] += dgq
        dgk_ref[0:1, :] += dgk

    rowb = lambda w, blk: pl.BlockSpec((ts, w), lambda i: (i, blk))
    full = lambda a: pl.BlockSpec(a.shape, lambda i: (0,) * a.ndim)
    heads = lambda w: pl.BlockSpec((H, ts, w), lambda i: (0, i, 0))
    return pl.pallas_call(
        body, grid=(S // ts,),
        in_specs=[rowb(RQ, OFF_QA // RQ), rowb(RKV, OFF_CKV // RKV), pl.BlockSpec((ts, 1), lambda i: (i, 0)), full(freq), full(g_qa), full(g_kva),
                  full(w_q), full(w_kv), heads(HW), heads(HW), heads(DV)] + [rowb(D, 0)] * nseg,
        out_specs=[rowb(H * HW, 0), rowb(H * HW, 0), rowb(NP, 0),
                   pl.BlockSpec((8, RQ), lambda i: (0, 0)), pl.BlockSpec((8, RKV), lambda i: (0, 0))],
        out_shape=[jax.ShapeDtypeStruct((S, H * HW), BF16), jax.ShapeDtypeStruct((S, H * HW), BF16), jax.ShapeDtypeStruct((S, NP), BF16),
                   jax.ShapeDtypeStruct((8, RQ), F32), jax.ShapeDtypeStruct((8, RKV), F32)],
        compiler_params=_cparams(("arbitrary",), 56 << 20), name="qkv_bwd",
    )(proj, proj, pos, freq, g_qa, g_kva, w_q, w_kv, dq, dk, dv, *segments)


def _prenorm_bwd(x, g, dh, dy, ts):
    S = x.shape[0]

    def body(x_ref, g_ref, dh_ref, dy_ref, gx_ref, dg_ref):
        dx, dg = _rms_bwd(x_ref[...], g_ref[...], dh_ref[...])
        gx_ref[...] = dy_ref[...] + dx

        @pl.when(pl.program_id(0) == 0)
        def _():
            dg_ref[...] = jnp.zeros_like(dg_ref)

        dg_ref[0:1, :] += dg

    row = pl.BlockSpec((ts, D), lambda i: (i, 0))
    return pl.pallas_call(
        body, grid=(S // ts,), in_specs=[row, pl.BlockSpec((1, D), lambda i: (0, 0)), row, row],
        out_specs=[row, pl.BlockSpec((8, D), lambda i: (0, 0))],
        out_shape=[jax.ShapeDtypeStruct((S, D), F32), jax.ShapeDtypeStruct((8, D), F32)],
        compiler_params=_cparams(("arbitrary",), 40 << 20), name="prenorm_bwd",
    )(x, g, dh, dy)


def _local_step(x, pos, target, g_pre, g_qa, g_kva, g_post, w_in, other_weights, reduce_grads=None):
    S = x.shape[0]
    ts = min(256, S)
    tq = min(512, S)
    tm = min(1024, S)
    mm = functools.partial(_matmul, tm=tm)
    freq = _inv_freq_tile()

    h, ht = _prenorm(x, g_pre, ts)
    if len(other_weights) == 2:
        gather, assemble = other_weights
        proj, gathered = _matmul(h, w_in, mode="nn", out_dtype=BF16, tm=1024, tn=1408, tk=D, name="mm_proj", exchange=gather)
        other_weights = assemble(*gathered)
    else:
        proj = _matmul(h, w_in, mode="nn", out_dtype=BF16, tm=1024, tn=1408, tk=D, name="mm_proj")
    w_q, w_kv, conv_w8, w_o_mla, w_o_conv, w_out = other_weights
    q, k, v, vt, qn, kvn = _qkv_prep(proj, pos, freq, g_qa, g_kva, w_q, w_kv, ts)
    attn, lse = _flash_fwd(q, k, vt, tq)
    a_mla, a_conv = _gates_fwd(proj, attn, conv_w8, ts)
    y_mla = mm(a_mla, w_o_mla, mode="nn", out_dtype=BF16, tn=D, tk=D, name="mm_y_mla")
    y_conv = mm(a_conv, w_o_conv, mode="nn", out_dtype=BF16, tn=D, tk=D, name="mm_y_conv")
    merged = _merge_fwd(proj, y_mla, y_conv, ts)
    out = mm(merged, w_out, mode="nn", out_dtype=F32, tn=D, tk=D, name="mm_out")
    dy, dout, dg_post, loss = _post_loss(out, x, target, g_post, ts)

    dmerged = mm(dout, w_out, mode="nt", out_dtype=BF16, tn=D, tk=D, name="mm_dmerged")
    dw_out = _matmul(merged, dout, mode="tn", out_dtype=F32, tm=1024, tn=1024, tk=2048, name="mm_dw_out")
    dy_mla, dy_conv, dg_mla, dg_conv = _merge_bwd(proj, y_mla, y_conv, dmerged, ts)
    da_mla = mm(dy_mla, w_o_mla, mode="nt", out_dtype=BF16, tn=D, tk=D, name="mm_da_mla")
    da_conv = mm(dy_conv, w_o_conv, mode="nt", out_dtype=BF16, tn=D, tk=D, name="mm_da_conv")
    dw_o_mla = _matmul(a_mla, dy_mla, mode="tn", out_dtype=F32, tm=1024, tn=1024, tk=2048, name="mm_dw_o_mla")
    dw_o_conv = _matmul(a_conv, dy_conv, mode="tn", out_dtype=F32, tm=1024, tn=1024, tk=2048, name="mm_dw_o_conv")
    dattn, delta, dz_mla, dc_in, db_gate, dc_gate, dz_conv, dconv_w = _gates_bwd(proj, attn, da_mla, da_conv, conv_w8, min(128, S))
    dq, dk, dv = _flash_bwd(q, k, v, dattn, lse, delta, tq)
    dqp, dkvp, dproj, dg_qa, dg_kva = _qkv_bwd(proj, pos, freq, g_qa, g_kva, w_q, w_kv, dq, dk, dv,
                                               [dz_mla, dc_in, db_gate, dc_gate, dz_conv, dg_mla, dg_conv], min(128, S))
    dw_q = _matmul(qn, dqp, mode="tn", out_dtype=F32, tm=RQ, tn=1024, tk=2048, name="mm_dw_q")
    dw_kv = _matmul(kvn, dkvp, mode="tn", out_dtype=F32, tm=RKV, tn=1024, tk=2048, name="mm_dw_kv")
    dw_in = _matmul(ht, dproj, mode="nn", out_dtype=F32, tm=1024, tn=1408, tk=2048, name="mm_dw_in")
    res = dict(dw_in=dw_in, dw_q=dw_q, dw_kv=dw_kv, dconv_w=dconv_w, dw_o_mla=dw_o_mla, dw_o_conv=dw_o_conv, dw_out=dw_out)
    if reduce_grads is None:
        dh = _matmul(dproj, w_in, mode="nt", out_dtype=F32, tm=1024, tn=D, tk=1408, name="mm_dh")
    else:
        exchange, finish = reduce_grads(res)
        dh, got = _matmul(dproj, w_in, mode="nt", out_dtype=F32, tm=1024, tn=D, tk=1408, name="mm_dh", exchange=exchange)
        res["reduced"] = finish(got)
    grad_x, dg_pre = _prenorm_bwd(x, g_pre, dh, dy, ts)
    res.update(loss=loss, grad_x=grad_x, dg_pre=dg_pre, dg_qa=dg_qa, dg_kva=dg_kva, dg_post=dg_post)
    return res


def _my_id():
    return lax.axis_index("x") * 4 + lax.axis_index("y") * 2 + lax.axis_index("c")


def _place():
    x, y, c = lax.axis_index("x"), lax.axis_index("y"), lax.axis_index("c")
    return (x, y, c), (x, y, 1 - c), [(1 - x, y), (x, 1 - y), (1 - x, 1 - y)]


def _slot(px, py, pc):
    return 4 * px + 2 * py + pc


def _all_gather(arrays, name):
    n = len(arrays)

    def body(*refs):
        ins, outs = refs[:n], refs[n:2 * n]
        send_sems, recv_sems, local_sems = refs[2 * n:]
        me, sib, chips = _place()
        c = me[2]

        def copy(a, k, block, to, src=None):
            rows = outs[a].at[_slot(*block)]
            return pltpu.make_async_remote_copy(src_ref=rows if src is None else src, dst_ref=rows, send_sem=send_sems.at[a, k],
                                                recv_sem=recv_sems.at[a, k], device_id=to, device_id_type=MESH)

        local = [pltpu.make_async_copy(ins[a], outs[a].at[_slot(*me)], local_sems.at[a]) for a in range(n)]
        for cp in local:
            cp.start()
        sends = []
        for a in range(n):
            sends.append(copy(a, 0, me, sib, src=ins[a]))
            sends += [copy(a, 1 + j, me, (*chip, c), src=ins[a]) for j, chip in enumerate(chips)]
        for cp in sends:
            cp.start()
        for a in range(n):
            for j, chip in enumerate(chips):
                copy(a, 1 + j, (*chip, c), me).wait_recv()
                fwd = copy(a, 4 + j, (*chip, c), sib)
                fwd.start()
                sends.append(fwd)
        for a in range(n):
            copy(a, 0, sib, me).wait_recv()
            for j, chip in enumerate(chips):
                copy(a, 4 + j, (*chip, 1 - c), me).wait_recv()
        for cp in sends:
            cp.wait_send()
        for cp in local:
            cp.wait()

    anyspec = pl.BlockSpec(memory_space=pl.ANY)
    return pl.pallas_call(
        body,
        in_specs=[anyspec] * n,
        out_specs=[anyspec] * n,
        out_shape=[jax.ShapeDtypeStruct((NDEV,) + a.shape, a.dtype) for a in arrays],
        scratch_shapes=[pltpu.SemaphoreType.DMA((n, NDEV - 1)), pltpu.SemaphoreType.DMA((n, NDEV - 1)), pltpu.SemaphoreType.DMA((n,))],
        name=name,
    )(*arrays)


def _slab_exchange(arrays, nslots, pick):
    n = len(arrays)

    def copies(ins, outs, sems):
        send_sems, recv_sems = sems
        return [pltpu.make_async_remote_copy(src_ref=ins[a].at[pick(k)[0]], dst_ref=outs[a].at[k], send_sem=send_sems.at[a, k],
                                             recv_sem=recv_sems.at[a, k], device_id=pick(k)[1], device_id_type=MESH)
                for a in range(n) for k in range(nslots)]

    def start(ins, outs, sems):
        for cp in copies(ins, outs, sems):
            cp.start()

    def finish(ins, outs, sems):
        cps = copies(ins, outs, sems)
        for cp in cps:
            cp.wait_recv()
        for cp in cps:
            cp.wait_send()

    return _Exchange(arrays, [jax.ShapeDtypeStruct((nslots,) + a.shape[1:], a.dtype) for a in arrays],
                     [pltpu.SemaphoreType.DMA((n, nslots)), pltpu.SemaphoreType.DMA((n, nslots))], start, finish)


def _direct_gather(arrays):
    n = len(arrays)

    def peer(d):
        p = (_my_id() + d) % NDEV
        return (p // 4, (p // 2) % 2, p % 2), p

    def copies(ins, outs, sems, receiving):
        send_sems, recv_sems, _ = sems
        slot = lambda d: peer(NDEV - d)[1] if receiving else _my_id()
        return [pltpu.make_async_remote_copy(src_ref=ins[a], dst_ref=outs[a].at[slot(d)], send_sem=send_sems.at[a, d - 1],
                                             recv_sem=recv_sems.at[a, d - 1], device_id=peer(d)[0], device_id_type=MESH)
                for d in range(1, NDEV) for a in range(n)]

    def local(ins, outs, sems):
        return [pltpu.make_async_copy(ins[a], outs[a].at[_my_id()], sems[2].at[a]) for a in range(n)]

    def start(ins, outs, sems):
        for cp in local(ins, outs, sems) + copies(ins, outs, sems, False):
            cp.start()

    def finish(ins, outs, sems):
        for cp in copies(ins, outs, sems, True):
            cp.wait_recv()
        for cp in copies(ins, outs, sems, False):
            cp.wait_send()
        for cp in local(ins, outs, sems):
            cp.wait()

    return _Exchange(arrays, [jax.ShapeDtypeStruct((NDEV,) + a.shape, a.dtype) for a in arrays],
                     [pltpu.SemaphoreType.DMA((n, NDEV - 1)), pltpu.SemaphoreType.DMA((n, NDEV - 1)), pltpu.SemaphoreType.DMA((n,))],
                     start, finish)


def _run_exchange(exchange, name):
    n = len(exchange.arrays)

    def body(*refs):
        ins, outs, sems = refs[:n], refs[n:2 * n], refs[2 * n:]
        exchange.start(ins, outs, sems)
        exchange.finish(ins, outs, sems)

    anyspec = pl.BlockSpec(memory_space=pl.ANY)
    return pl.pallas_call(body, in_specs=[anyspec] * n, out_specs=[anyspec] * n, out_shape=exchange.out_shapes,
                          scratch_shapes=exchange.sem_shapes, name=name)(*exchange.arrays)


def _slabs_bf16(x, rows_per_block, name):
    n, R, C = x.shape
    tr = min(rows_per_block, R)

    def body(x_ref, o_ref):
        o_ref[...] = x_ref[...].astype(BF16)

    blk = pl.BlockSpec((1, tr, C), lambda k, i: (k, i, 0))
    return pl.pallas_call(body, grid=(n, R // tr), in_specs=[blk], out_specs=blk, out_shape=jax.ShapeDtypeStruct(x.shape, BF16),
                          compiler_params=_cparams(("parallel", "parallel"), 32 << 20), name=name)(x)


def _to_sibling(k):
    me, sib, chips = _place()
    dest = sib if k == 0 else (*chips[k - 1], sib[2])
    return _slot(*dest), sib


def _to_chips(k):
    me, sib, chips = _place()
    return k, (*chips[k], me[2])


def _chip_sums(own, got, slots, rows_per_block, name):
    _, R, C = own.shape
    tr = min(rows_per_block, R)

    def body(slots_ref, own_ref, got_ref, o_ref):
        o_ref[0] = (own_ref[0].astype(F32) + got_ref[0].astype(F32)).astype(BF16)

    return pl.pallas_call(
        body,
        grid_spec=pltpu.PrefetchScalarGridSpec(
            num_scalar_prefetch=1, grid=(3, R // tr),
            in_specs=[pl.BlockSpec((1, tr, C), lambda j, i, s: (s[1 + j], i, 0)), pl.BlockSpec((1, tr, C), lambda j, i, s: (1 + j, i, 0))],
            out_specs=pl.BlockSpec((1, tr, C), lambda j, i, s: (j, i, 0))),
        out_shape=jax.ShapeDtypeStruct((3, R, C), BF16),
        compiler_params=_cparams(("parallel", "parallel"), 32 << 20), name=name,
    )(slots, own, got)


def _adamw_math(w, g, m, v):
    m = ADAM_B1 * m + (1.0 - ADAM_B1) * g
    v = ADAM_B2 * v + (1.0 - ADAM_B2) * (g * g)
    m_hat = m / (1.0 - ADAM_B1 ** ADAM_STEP)
    v_hat = v / (1.0 - ADAM_B2 ** ADAM_STEP)
    delta = -ADAM_LR * (m_hat / (jnp.sqrt(v_hat) + ADAM_EPS) + ADAM_WD * w)
    return delta, m, v


def _reduce_adamw(own, got1, got2, slots, w, m, v, rows_per_block, name):
    _, R, C = own.shape
    tr = min(rows_per_block, R)
    assert R % tr == 0

    def body(slots_ref, own_ref, g1_ref, g2_ref, w_ref, m_ref, v_ref, g_ref, d_ref, nm_ref, nv_ref):
        g = own_ref[0] + g1_ref[0].astype(F32)
        for j in range(3):
            g = g + g2_ref[j].astype(F32)
        g_ref[...] = g
        d, nm, nv = _adamw_math(w_ref[...], g, m_ref[...], v_ref[...])
        d_ref[...] = d
        nm_ref[...] = nm
        nv_ref[...] = nv

    blk = pl.BlockSpec((tr, C), lambda i, s: (i, 0))
    return pl.pallas_call(
        body,
        grid_spec=pltpu.PrefetchScalarGridSpec(
            num_scalar_prefetch=1, grid=(R // tr,),
            in_specs=[pl.BlockSpec((1, tr, C), lambda i, s: (s[0], i, 0)), pl.BlockSpec((1, tr, C), lambda i, s: (0, i, 0)),
                      pl.BlockSpec((3, tr, C), lambda i, s: (0, i, 0)), blk, blk, blk],
            out_specs=[blk] * 4),
        out_shape=[jax.ShapeDtypeStruct((R, C), F32)] * 4,
        compiler_params=_cparams(("parallel",), 48 << 20), name=name,
    )(slots, own, got1, got2, w, m, v)


def _reduce_adamw_t(own, got1, got2, slots, wt, mt, vt, rows_per_block, name):
    R, C = own.shape
    tr = min(rows_per_block, R)
    assert R % tr == 0

    def body(slots_ref, own_ref, g1_ref, g2_ref, w_ref, m_ref, v_ref, g_ref, d_ref, nm_ref, nv_ref):
        g = own_ref[...] + g1_ref[0].astype(F32)
        for j in range(3):
            g = g + g2_ref[j].astype(F32)
        g = jnp.transpose(g)
        g_ref[...] = g
        d, nm, nv = _adamw_math(w_ref[...], g, m_ref[...], v_ref[...])
        d_ref[...] = d
        nm_ref[...] = nm
        nv_ref[...] = nv

    blk = pl.BlockSpec((C, tr), lambda i, s: (0, i))
    return pl.pallas_call(
        body,
        grid_spec=pltpu.PrefetchScalarGridSpec(
            num_scalar_prefetch=1, grid=(R // tr,),
            in_specs=[pl.BlockSpec((tr, C), lambda i, s: (i, 0)), pl.BlockSpec((1, tr, C), lambda i, s: (0, i, 0)),
                      pl.BlockSpec((3, tr, C), lambda i, s: (0, i, 0)), blk, blk, blk],
            out_specs=[blk] * 4),
        out_shape=[jax.ShapeDtypeStruct((C, R), F32)] * 4,
        compiler_params=_cparams(("parallel",), 48 << 20), name=name,
    )(slots, own, got1, got2, wt, mt, vt)


def _sum_adamw(parts, w, m, v, rows_per_block, name):
    n, R, C = parts.shape
    tr = min(rows_per_block, R)
    assert R % tr == 0

    def body(p_ref, w_ref, m_ref, v_ref, g_ref, d_ref, nm_ref, nv_ref):
        g = p_ref[0]
        for j in range(1, n):
            g = g + p_ref[j]
        g_ref[...] = g
        d, nm, nv = _adamw_math(w_ref[...], g, m_ref[...], v_ref[...])
        d_ref[...] = d
        nm_ref[...] = nm
        nv_ref[...] = nv

    blk = pl.BlockSpec((tr, C), lambda i: (i, 0))
    return pl.pallas_call(
        body, grid=(R // tr,),
        in_specs=[pl.BlockSpec((n, tr, C), lambda i: (0, i, 0)), blk, blk, blk],
        out_specs=[blk] * 4,
        out_shape=[jax.ShapeDtypeStruct((R, C), F32)] * 4,
        compiler_params=_cparams(("parallel",), 48 << 20), name=name,
    )(parts, w, m, v)


def kernel(x, positions, pre_norm_g, w_in, q_a_norm_g, w_q_b, kv_a_norm_g, w_kv_b, conv_w, w_o_mla, w_o_conv, w_out, post_norm_g, loss_target, m_pre_norm_g, m_w_in, m_q_a_norm_g, m_w_q_b, m_kv_a_norm_g, m_w_kv_b, m_conv_w, m_w_o_mla, m_w_o_conv, m_w_out, m_post_norm_g, v_pre_norm_g, v_w_in, v_q_a_norm_g, v_w_q_b, v_kv_a_norm_g, v_w_kv_b, v_conv_w, v_w_o_mla, v_w_o_conv, v_w_out, v_post_norm_g):
    S = x.shape[1]
    conv_pad = jnp.zeros((8, 256), F32).at[0:3, :].set(conv_w)
    g_in, = _all_gather([w_in.astype(BF16)], "all_gather_w_in")
    w_in_f = _assemble_w_in(g_in)
    gather_rest = _direct_gather([w_q_b.astype(BF16), w_kv_b.astype(BF16), conv_pad, w_o_mla.astype(BF16), w_o_conv.astype(BF16),
                                  w_out.astype(BF16)])

    def assemble_rest(g_q, g_kv, g_cw, g_om, g_oc, g_out):
        return (_assemble_w_q(g_q), _concat_cols(g_kv, BF16, "assemble_w_kv"), _concat_cols(g_cw, F32, "assemble_conv_w"),
                g_om.reshape(D, D), g_oc.reshape(D, D), g_out.reshape(D, D))

    gnames = ["w_in", "w_q", "w_kv", "conv_w", "w_o_mla", "w_o_conv", "w_out"]
    (mx, my, mc), _, chips = _place()
    slots = jnp.stack([_slot(mx, my, mc)] + [_slot(cx, cy, mc) for cx, cy in chips]).astype(jnp.int32)

    def reduce_grads(r):
        own_in, slabs_in = _split_dw_in(r["dw_in"], slots)
        own = [own_in, _split_dw_q(r["dw_q"]), _split_cols(r["dw_kv"], NDEV, "split_dw_kv"),
               _split_cols(r["dconv_w"], NDEV, "split_dconv_w"), r["dw_o_mla"].reshape(NDEV, D // NDEV, D),
               r["dw_o_conv"].reshape(NDEV, D // NDEV, D), r["dw_out"].reshape(NDEV, D // NDEV, D)]
        slabs = [slabs_in] + [_slabs_bf16(g, 128, "bf16_" + nm) for g, nm in zip(own[1:], gnames[1:])]
        to_sibling = _slab_exchange(slabs, 4, _to_sibling)
        got1 = _run_exchange(to_sibling, "grads_to_sibling")
        sums = [_chip_sums(o, g1, slots, 128, "chip_sum_" + nm) for o, g1, nm in zip([slabs_in] + own[1:], got1, gnames)]
        return _slab_exchange(sums, 3, _to_chips), lambda got2: (own, got1, got2)

    row2 = lambda a: a.reshape(1, -1)
    r = _local_step(x[0], positions.reshape(S, 1), loss_target[0], row2(pre_norm_g), row2(q_a_norm_g), row2(kv_a_norm_g),
                    row2(post_norm_g), w_in_f, (gather_rest, assemble_rest), reduce_grads)
    own, got1, got2 = r["reduced"]
    small = jnp.concatenate([r["dg_pre"][0:1], r["dg_post"][0:1], jnp.pad(r["dg_qa"][0:1], ((0, 0), (0, D - RQ))),
                             jnp.pad(r["dg_kva"][0:1], ((0, 0), (0, D - RKV))), jnp.pad(r["loss"][0:1], ((0, 0), (0, D - 128))),
                             jnp.zeros((3, D), F32)], axis=0)
    p_small, = _all_gather([small], "all_gather_small")
    pad8 = lambda a: jnp.zeros((8, 256), F32).at[0:3, :].set(a)
    params = [(w_in, m_w_in, v_w_in), (w_q_b, m_w_q_b, v_w_q_b), (w_kv_b, m_w_kv_b, v_w_kv_b), (conv_pad, pad8(m_conv_w), pad8(v_conv_w)),
              (w_o_mla, m_w_o_mla, v_w_o_mla), (w_o_conv, m_w_o_conv, v_w_o_conv), (w_out, m_w_out, v_w_out)]
    o_q, o_kv, o_cw, o_om, o_oc, o_out = [
        _reduce_adamw(o, g1, g2, slots, w, m, v, 128, "adamw_" + nm)
        for o, g1, g2, (w, m, v), nm in list(zip(own, got1, got2, params, gnames))[1:]]
    o_cw = [a[0:3] for a in o_cw]
    o_in = [a.T for a in _reduce_adamw_t(own[0], got1[0], got2[0], slots, w_in.T, m_w_in.T, v_w_in.T, 128, "adamw_w_in")]
    padv = lambda a: jnp.pad(a.reshape(1, -1), ((0, 0), (0, D - a.shape[0])))
    stack = lambda pre, post, qa, kva: jnp.concatenate([row2(pre), row2(post), padv(qa), padv(kva), jnp.zeros((4, D), F32)], axis=0)
    o_g = _sum_adamw(p_small, stack(pre_norm_g, post_norm_g, q_a_norm_g, kv_a_norm_g),
                     stack(m_pre_norm_g, m_post_norm_g, m_q_a_norm_g, m_kv_a_norm_g),
                     stack(v_pre_norm_g, v_post_norm_g, v_q_a_norm_g, v_kv_a_norm_g), 8, "adamw_gains")
    loss = o_g[0][4, 0]
    outs = {}
    for idx, kind in enumerate(("grad", "delta", "new_m", "new_v")):
        o = o_g[idx]
        outs[kind] = dict(pre_norm_g=o[0], w_in=o_in[idx], q_a_norm_g=o[2, 0:RQ], w_q_b=o_q[idx], kv_a_norm_g=o[3, 0:RKV],
                          w_kv_b=o_kv[idx], conv_w=o_cw[idx], w_o_mla=o_om[idx], w_o_conv=o_oc[idx], w_out=o_out[idx], post_norm_g=o[1])
    names = ["pre_norm_g", "w_in", "q_a_norm_g", "w_q_b", "kv_a_norm_g", "w_kv_b", "conv_w", "w_o_mla", "w_o_conv", "w_out", "post_norm_g"]
    return (loss, r["grad_x"][None], *[outs["grad"][n] for n in names], *[outs["delta"][n] for n in names],
            *[outs["new_m"][n] for n in names], *[outs["new_v"][n] for n in names])
```

```python
import functools
import math

import jax
import jax.numpy as jnp
from jax import lax
from jax.experimental import pallas as pl
from jax.experimental.pallas import tpu as pltpu

F32 = jnp.float32
BF16 = jnp.bfloat16

NDEV = 8
D = 2048
H = 16
DN = 128
DR = 64
DV = 128
RQ = 512
RKV = 512
HW = 256
ROPE_THETA = 10000.0
RMS_EPS = 1e-6
N_IN = 15424
SHARD_IN = N_IN // NDEV
SMALL = RQ + RKV + DR
NP = 7 * D + RQ + RKV + 128
SEG = dict(z_mla=0, c_in=1, b_gate=2, c_gate=3, z_conv=4, g_mla=5, g_conv=6)
OFF_QA = 7 * D
OFF_CKV = OFF_QA + RQ
OFF_KR = OFF_CKV + RKV
EXT = 2176
VMEM_CAP = 56 * 1024 * 1024

ADAM_LR = 0.001
ADAM_B1 = 0.9
ADAM_B2 = 0.999
ADAM_EPS = 1e-08
ADAM_WD = 0.01
ADAM_STEP = 10

LOG2E = math.log2(math.e)
SM_SCALE = 1.0 / math.sqrt(DN + DR)
QSCALE = SM_SCALE * LOG2E
NN = (((1,), (0,)), ((), ()))
NT = (((1,), (1,)), ((), ()))
TN = (((0,), (0,)), ((), ()))
MESH = pl.DeviceIdType.MESH


def _cparams(sem, vmem_bytes):
    return pltpu.CompilerParams(dimension_semantics=sem, vmem_limit_bytes=int(min(VMEM_CAP, max(vmem_bytes, 16 << 20))))


def _nbytes(shape, dtype):
    return math.prod(shape) * jnp.dtype(dtype).itemsize


class _Exchange:
    def __init__(self, arrays, out_shapes, sem_shapes, start, finish):
        self.arrays, self.out_shapes, self.sem_shapes, self.start, self.finish = arrays, out_shapes, sem_shapes, start, finish
        self.n_sems = len(sem_shapes)


def _matmul(a, b, *, mode, out_dtype, tm, tn, tk, name, m_outer=False, exchange=None):
    if mode == "nn":
        (M, K), (K2, N) = a.shape, b.shape
    elif mode == "nt":
        (M, K), (N, K2) = a.shape, b.shape
    else:
        (K, M), (K2, N) = a.shape, b.shape
    assert K == K2, (a.shape, b.shape, mode)
    tm, tn, tk = min(tm, M), min(tn, N), min(tk, K)
    assert M % tm == 0 and N % tn == 0 and K % tk == 0, (M, N, K, tm, tn, tk)
    ni, nj, nk = M // tm, N // tn, K // tk
    dims = dict(nn=NN, nt=NT, tn=TN)[mode]

    if m_outer:
        grid = (ni, nj, nk)
        ij = lambda g0, g1: (g0, g1)
    else:
        grid = (nj, ni, nk)
        ij = lambda g0, g1: (g1, g0)

    if mode == "tn":
        a_spec = pl.BlockSpec((tk, tm), lambda g0, g1, k: (k, ij(g0, g1)[0]))
        a_tile = (tk, tm)
    else:
        a_spec = pl.BlockSpec((tm, tk), lambda g0, g1, k: (ij(g0, g1)[0], k))
        a_tile = (tm, tk)
    if mode == "nt":
        b_spec = pl.BlockSpec((tn, tk), lambda g0, g1, k: (ij(g0, g1)[1], k))
    else:
        b_spec = pl.BlockSpec((tk, tn), lambda g0, g1, k: (k, ij(g0, g1)[1]))
    o_spec = pl.BlockSpec((tm, tn), lambda g0, g1, k: ij(g0, g1))

    n_in = len(exchange.arrays) if exchange else 0

    def body(a_ref, b_ref, *refs):
        x_in, o_ref, x_out, scratch = refs[:n_in], refs[n_in], refs[n_in + 1:2 * n_in + 1], refs[2 * n_in + 1:]
        if exchange:
            sems = scratch[len(scratch) - exchange.n_sems:]
            step = (pl.program_id(0) * grid[1] + pl.program_id(1)) * grid[2] + pl.program_id(2)

            @pl.when(step == 0)
            def _():
                exchange.start(x_in, x_out, sems)

        prod = lax.dot_general(a_ref[...], b_ref[...], dims, preferred_element_type=F32)
        if nk == 1:
            o_ref[...] = prod.astype(o_ref.dtype)
        else:
            acc_ref = scratch[0]
            k = pl.program_id(2)

            @pl.when(k == 0)
            def _():
                acc_ref[...] = prod

            @pl.when((k > 0) & (k < nk - 1))
            def _():
                acc_ref[...] += prod

            @pl.when(k == nk - 1)
            def _():
                o_ref[...] = (acc_ref[...] + prod).astype(o_ref.dtype)

        if exchange:
            @pl.when(step == grid[0] * grid[1] * grid[2] - 1)
            def _():
                exchange.finish(x_in, x_out, sems)

    vmem = 2 * (_nbytes(a_tile, a.dtype) + _nbytes((tk, tn), b.dtype) + _nbytes((tm, tn), out_dtype)) + 2 * _nbytes((tm, tn), F32)
    anyspec = pl.BlockSpec(memory_space=pl.ANY)
    outs = pl.pallas_call(
        body,
        grid=grid,
        in_specs=[a_spec, b_spec] + [anyspec] * n_in,
        out_specs=[o_spec] + [anyspec] * n_in,
        out_shape=[jax.ShapeDtypeStruct((M, N), out_dtype)] + (exchange.out_shapes if exchange else []),
        scratch_shapes=([] if nk == 1 else [pltpu.VMEM((tm, tn), F32)]) + (exchange.sem_shapes if exchange else []),
        compiler_params=_cparams(("arbitrary",) * 3 if exchange else ("parallel", "parallel", "arbitrary"), vmem + (8 << 20)),
        name=name,
    )(a, b, *(exchange.arrays if exchange else []))
    return (outs[0], outs[1:]) if exchange else outs[0]


def _in_dest(k):
    return SHARD_IN * k - SMALL


def _assemble_w_in(g):
    R = 128

    def body(g_ref, o_ref, ext, acc):
        ext[...] = jnp.zeros_like(ext)
        acc[...] = jnp.zeros_like(acc)
        lane = lax.broadcasted_iota(jnp.int32, (R, EXT), 1)
        ext[:, 0:SHARD_IN] = g_ref[0].astype(F32)
        v = ext[...]
        acc[:, OFF_QA:NP] = jnp.where(lane[:, 0:NP - OFF_QA] < SMALL, v[:, 0:NP - OFF_QA], 0.0)
        w = v[:, 1024:2048]
        w = pltpu.roll(w, 1024 - 64, 1)
        acc[:, 0:1024] = jnp.where(lane[:, 0:1024] < SHARD_IN - SMALL, w, 0.0)
        for k in range(1, NDEV):
            ext[:, 0:SHARD_IN] = g_ref[k].astype(F32)
            dest = _in_dest(k)
            t, o = dest // 128, dest % 128
            width = -(-(o + SHARD_IN) // 128) * 128
            v = pltpu.roll(ext[...], o, 1)[:, 0:width]
            acc[:, 128 * t:128 * t + width] += v
        o_ref[...] = acc[...].astype(BF16)

    return pl.pallas_call(
        body,
        grid=(D // R,),
        in_specs=[pl.BlockSpec((NDEV, R, SHARD_IN), lambda i: (0, i, 0))],
        out_specs=pl.BlockSpec((R, NP), lambda i: (i, 0)),
        out_shape=jax.ShapeDtypeStruct((D, NP), BF16),
        scratch_shapes=[pltpu.VMEM((R, EXT), F32), pltpu.VMEM((R, NP), F32)],
        compiler_params=_cparams(("parallel",), 40 << 20),
        name="assemble_w_in",
    )(g)


def _split_dw_in(dw, slots):
    R = 128

    def body(slots_ref, dw_ref, o_ref, ob_ref):
        def put(k, cols, v):
            ob_ref[k, :, cols] = v.astype(BF16)

            @pl.when(slots_ref[0] == k)
            def _():
                o_ref[:, cols] = v

        lane = lax.broadcasted_iota(jnp.int32, (R, 1024), 1)
        put(0, slice(0, 1024), dw_ref[:, OFF_QA:OFF_QA + 1024])
        tail = dw_ref[:, OFF_QA + 1024:NP]
        tail = jnp.concatenate([tail, jnp.zeros((R, 1024 - 128), F32)], axis=1)
        head = dw_ref[:, 0:1024]
        mixed = jnp.where(lane < 64, tail, pltpu.roll(head, 64, 1))
        put(0, slice(1024, SHARD_IN), mixed[:, 0:SHARD_IN - 1024])
        for k in range(1, NDEV):
            dest = _in_dest(k)
            t, o = dest // 128, dest % 128
            width = -(-(o + SHARD_IN) // 128) * 128
            v = dw_ref[:, 128 * t:128 * t + width]
            v = pltpu.roll(v, width - o, 1)
            put(k, slice(0, SHARD_IN), v[:, 0:SHARD_IN])

    return pl.pallas_call(
        body,
        grid_spec=pltpu.PrefetchScalarGridSpec(
            num_scalar_prefetch=1, grid=(D // R,),
            in_specs=[pl.BlockSpec((R, NP), lambda i, s: (i, 0))],
            out_specs=[pl.BlockSpec((R, SHARD_IN), lambda i, s: (i, 0)), pl.BlockSpec((NDEV, R, SHARD_IN), lambda i, s: (0, i, 0))]),
        out_shape=[jax.ShapeDtypeStruct((D, SHARD_IN), F32), jax.ShapeDtypeStruct((NDEV, D, SHARD_IN), BF16)],
        compiler_params=_cparams(("parallel",), 48 << 20),
        name="split_dw_in",
    )(slots, dw)


def _assemble_w_q(g):
    def body(g_ref, o_ref):
        lane = lax.broadcasted_iota(jnp.int32, (RQ, 128), 1)
        lo = lane < 64
        for k in range(NDEV):
            t0 = g_ref[k, :, 0:128].astype(F32)
            t1 = g_ref[k, :, 128:256].astype(F32)
            t2 = g_ref[k, :, 256:384].astype(F32)
            base = 2 * k * HW
            o_ref[:, base:base + 128] = t0.astype(BF16)
            o_ref[:, base + 128:base + 256] = jnp.where(lo, t1, 0.0).astype(BF16)
            o_ref[:, base + 256:base + 384] = pltpu.roll(jnp.where(lo, t2, t1), 64, 1).astype(BF16)
            o_ref[:, base + 384:base + 512] = jnp.where(lo, pltpu.roll(t2, 64, 1), 0.0).astype(BF16)

    return pl.pallas_call(
        body,
        out_shape=jax.ShapeDtypeStruct((RQ, H * HW), BF16),
        compiler_params=_cparams(None, 32 << 20),
        name="assemble_w_q",
    )(g)


def _split_dw_q(dw):
    def body(dw_ref, o_ref):
        lane = lax.broadcasted_iota(jnp.int32, (RQ, 128), 1)
        lo = lane < 64
        for k in range(NDEV):
            base = 2 * k * HW
            a = dw_ref[:, base:base + 128]
            b = dw_ref[:, base + 128:base + 256]
            c = pltpu.roll(dw_ref[:, base + 256:base + 384], 64, 1)
            d = pltpu.roll(dw_ref[:, base + 384:base + 512], 64, 1)
            o_ref[k, :, 0:128] = a
            o_ref[k, :, 128:256] = jnp.where(lo, b, c)
            o_ref[k, :, 256:384] = jnp.where(lo, c, d)

    return pl.pallas_call(
        body,
        out_shape=jax.ShapeDtypeStruct((NDEV, RQ, 384), F32),
        compiler_params=_cparams(None, 32 << 20),
        name="split_dw_q",
    )(dw)


def _concat_cols(g, dtype, name):
    n, R, C = g.shape

    def body(g_ref, o_ref):
        for k in range(n):
            o_ref[:, k * C:(k + 1) * C] = g_ref[k].astype(dtype)

    return pl.pallas_call(body, out_shape=jax.ShapeDtypeStruct((R, n * C), dtype), compiler_params=_cparams(None, 32 << 20), name=name)(g)


def _split_cols(x, n, name):
    R, NC = x.shape
    C = NC // n

    def body(x_ref, o_ref):
        for k in range(n):
            o_ref[k] = x_ref[:, k * C:(k + 1) * C]

    return pl.pallas_call(body, out_shape=jax.ShapeDtypeStruct((n, R, C), x.dtype), compiler_params=_cparams(None, 32 << 20), name=name)(x)


def _rms_scale(xf):
    return lax.rsqrt(jnp.mean(xf * xf, axis=-1, keepdims=True) + RMS_EPS)


def _prenorm(x, g, ts):
    S = x.shape[0]

    def body(x_ref, g_ref, h_ref, ht_ref):
        xf = x_ref[...]
        h = xf * _rms_scale(xf) * g_ref[...]
        h_ref[...] = h.astype(BF16)
        ht_ref[...] = jnp.transpose(h).astype(BF16)

    return pl.pallas_call(
        body,
        grid=(S // ts,),
        in_specs=[pl.BlockSpec((ts, D), lambda i: (i, 0)), pl.BlockSpec((1, D), lambda i: (0, 0))],
        out_specs=[pl.BlockSpec((ts, D), lambda i: (i, 0)), pl.BlockSpec((D, ts), lambda i: (0, i))],
        out_shape=[jax.ShapeDtypeStruct((S, D), BF16), jax.ShapeDtypeStruct((D, S), BF16)],
        compiler_params=_cparams(("parallel",), 32 << 20),
        name="prenorm",
    )(x, g)


def _inv_freq_tile():
    inv_freq = ROPE_THETA ** (-jnp.arange(0, DR, 2, dtype=F32) / DR)
    return jnp.tile(inv_freq, 4).reshape(1, 128)


def _rope_tables(pos_ref, freq_ref, ts):
    lane = lax.broadcasted_iota(jnp.int32, (ts, 128), 1)
    ang = pos_ref[...].astype(F32) * freq_ref[...]
    return jnp.cos(ang), jnp.sin(ang), lane


def _rope_swap(t, lane):
    return jnp.where(lane < 32, pltpu.roll(t, 96, 1), pltpu.roll(t, 32, 1))


def _qkv_prep(proj, pos, freq, g_qa, g_kva, w_q, w_kv, ts):
    S = proj.shape[0]

    def body(qa_ref, ckv_ref, kr_ref, pos_ref, freq_ref, gq_ref, gk_ref, wq_ref, wkv_ref, q_ref, k_ref, v_ref, vt_ref, qn_ref, kvn_ref):
        qa = _f32(qa_ref)
        qn = (qa * _rms_scale(qa) * gq_ref[...]).astype(BF16)
        ckv = _f32(ckv_ref)
        kvn = (ckv * _rms_scale(ckv) * gk_ref[...]).astype(BF16)
        qn_ref[...] = qn
        kvn_ref[...] = kvn
        cos, sin, lane = _rope_tables(pos_ref, freq_ref, ts)
        sgn_sin = jnp.where(lane < 32, -sin, sin)
        live = lane < DR
        kr = _f32(kr_ref)
        kr = jnp.where(live, kr * cos + _rope_swap(kr, lane) * sgn_sin, 0.0).astype(BF16)
        qf = jnp.dot(qn, wq_ref[...], preferred_element_type=F32)
        kvf = jnp.dot(kvn, wkv_ref[...], preferred_element_type=F32)
        for h in range(H):
            q_ref[h, :, 0:DN] = (qf[:, h * HW:h * HW + DN] * QSCALE).astype(BF16)
            t = qf[:, h * HW + DN:(h + 1) * HW]
            q_ref[h, :, DN:HW] = jnp.where(live, (t * cos + _rope_swap(t, lane) * sgn_sin) * QSCALE, 0.0).astype(BF16)
            k_ref[h, :, 0:DN] = kvf[:, h * HW:h * HW + DN].astype(BF16)
            k_ref[h, :, DN:HW] = kr
            vh = kvf[:, h * HW + DN:(h + 1) * HW]
            v_ref[h] = vh.astype(BF16)
            vt_ref[h] = jnp.transpose(vh).astype(BF16)

    row = lambda w, blk: pl.BlockSpec((ts, w), lambda i: (i, blk))
    full = lambda a: pl.BlockSpec(a.shape, lambda i: (0,) * a.ndim)
    return pl.pallas_call(
        body,
        grid=(S // ts,),
        in_specs=[row(RQ, OFF_QA // RQ), row(RKV, OFF_CKV // RKV), row(128, OFF_KR // 128),
                  pl.BlockSpec((ts, 1), lambda i: (i, 0)), full(freq), full(g_qa), full(g_kva), full(w_q), full(w_kv)],
        out_specs=[pl.BlockSpec((H, ts, HW), lambda i: (0, i, 0)), pl.BlockSpec((H, ts, HW), lambda i: (0, i, 0)),
                   pl.BlockSpec((H, ts, DV), lambda i: (0, i, 0)), pl.BlockSpec((H, DV, ts), lambda i: (0, 0, i)),
                   pl.BlockSpec((ts, RQ), lambda i: (i, 0)), pl.BlockSpec((ts, RKV), lambda i: (i, 0))],
        out_shape=[jax.ShapeDtypeStruct((H, S, HW), BF16), jax.ShapeDtypeStruct((H, S, HW), BF16),
                   jax.ShapeDtypeStruct((H, S, DV), BF16), jax.ShapeDtypeStruct((H, DV, S), BF16),
                   jax.ShapeDtypeStruct((S, RQ), BF16), jax.ShapeDtypeStruct((S, RKV), BF16)],
        compiler_params=_cparams(("parallel",), 48 << 20),
        name="qkv_prep",
    )(proj, proj, proj, pos, freq, g_qa, g_kva, w_q, w_kv)


def _col_to_row8(col, n):
    return jnp.transpose(jnp.broadcast_to(col, (n, 128)))[0:8, :]


def _causal_pairs(n, key_major):
    if key_major:
        pairs = [(i, j) for j in range(n) for i in range(j, n)]
    else:
        pairs = [(i, j) for i in range(n) for j in range(i + 1)]
    return jnp.array([p[0] for p in pairs], jnp.int32), jnp.array([p[1] for p in pairs], jnp.int32)


def _flash_fwd(q, k, vt, tq, hb=8):
    S = q.shape[1]
    nq = S // tq
    tsub = tq
    nsub = tq // tsub

    qi_tab, kj_tab = _causal_pairs(nq, key_major=False)

    def body(qi_ref, kj_ref, q_ref, k_ref, vt_ref, o_ref, lse_ref, m_sc, l_sc, acc_sc):
        p = pl.program_id(1)
        qi, kj = qi_ref[p], kj_ref[p]

        @pl.when(kj == 0)
        def _():
            m_sc[...] = jnp.full_like(m_sc, -jnp.inf)
            l_sc[...] = jnp.zeros_like(l_sc)
            acc_sc[...] = jnp.zeros_like(acc_sc)

        def step(diag):
            chains = [(h, u) for h in range(hb) for u in range(nsub)]

            def scores(h, u):
                return lax.dot_general(k_ref[h], q_ref[h, u * tsub:(u + 1) * tsub, :], NT, preferred_element_type=F32)

            st_next = scores(*chains[0])
            for ci, (h, u) in enumerate(chains):
                st = st_next
                if ci + 1 < len(chains):
                    st_next = scores(*chains[ci + 1])
                cols = slice(u * tsub, (u + 1) * tsub)
                if diag:
                    r = lax.broadcasted_iota(jnp.int32, (tq, tsub), 0)
                    c = lax.broadcasted_iota(jnp.int32, (tq, tsub), 1) + u * tsub
                    st = jnp.where(r <= c, st, -jnp.inf)
                m_prev = m_sc[h, 0:1, cols]
                m_new = jnp.maximum(m_prev, jnp.max(st, axis=0, keepdims=True))
                alpha = jnp.exp2(m_prev - m_new)
                pt = jnp.exp2(st - m_new)
                l_sc[h, :, cols] = jnp.broadcast_to(alpha * l_sc[h, 0:1, cols] + jnp.sum(pt, axis=0, keepdims=True), (8, tsub))
                m_sc[h, :, cols] = jnp.broadcast_to(m_new, (8, tsub))
                acc_sc[h, :, cols] = alpha * acc_sc[h, :, cols] + jnp.dot(vt_ref[h], pt.astype(BF16), preferred_element_type=F32)

        @pl.when(kj < qi)
        def _():
            step(False)

        @pl.when(kj == qi)
        def _():
            step(True)
            for h in range(hb):
                l = l_sc[h, 0:1, :]
                o_ref[:, h * DV:(h + 1) * DV] = jnp.transpose(acc_sc[h] / l).astype(BF16)
                lse_ref[h] = m_sc[h] + jnp.log2(l_sc[h])

    return pl.pallas_call(
        body,
        grid_spec=pltpu.PrefetchScalarGridSpec(
            num_scalar_prefetch=2,
            grid=(H // hb, len(qi_tab)),
            in_specs=[pl.BlockSpec((hb, tq, HW), lambda h, p, qi, kj: (h, qi[p], 0)),
                      pl.BlockSpec((hb, tq, HW), lambda h, p, qi, kj: (h, kj[p], 0)),
                      pl.BlockSpec((hb, DV, tq), lambda h, p, qi, kj: (h, 0, kj[p]))],
            out_specs=[pl.BlockSpec((tq, hb * DV), lambda h, p, qi, kj: (qi[p], h)),
                       pl.BlockSpec((hb, 8, tq), lambda h, p, qi, kj: (h, 0, qi[p]))],
            scratch_shapes=[pltpu.VMEM((hb, 8, tq), F32), pltpu.VMEM((hb, 8, tq), F32), pltpu.VMEM((hb, DV, tq), F32)],
        ),
        out_shape=[jax.ShapeDtypeStruct((S, H * DV), BF16), jax.ShapeDtypeStruct((H, 8, S), F32)],
        compiler_params=_cparams(("parallel", "arbitrary"), 40 << 20),
        name="flash_fwd",
    )(qi_tab, kj_tab, q, k, vt)


def _sigmoid(x):
    return 1.0 / (1.0 + jnp.exp(-x))


HALO = 16


def _f32(ref):
    return ref[...].astype(F32)


def _shift_rows(u, prev, n, first):
    ts = u.shape[0]
    row = lax.broadcasted_iota(jnp.int32, u.shape, 0)
    out = pltpu.roll(u, n, 0)
    for j in range(n):
        halo = jnp.where(first, 0.0, prev[HALO - n + j:HALO - n + j + 1, :])
        out = jnp.where(row == j, halo, out)
    return out


def _gates_fwd(proj, attn, conv_w, ts):
    S = proj.shape[0]

    def body(attn_ref, zm_ref, cin_ref, bg_ref, cg_ref, zc_ref, cin_p, cg_p, w_ref, am_ref, ac_ref):
        first = pl.program_id(0) == 0
        zm = _f32(zm_ref)
        am_ref[...] = (_f32(attn_ref) * (zm * _sigmoid(zm))).astype(BF16)
        u = _f32(cg_ref) * _f32(cin_ref)
        up = _f32(cg_p) * _f32(cin_p)
        w = w_ref[...]
        conv = w[0:1, :] * _shift_rows(u, up, 2, first) + w[1:2, :] * _shift_rows(u, up, 1, first) + w[2:3, :] * u
        zc = _f32(zc_ref)
        ac_ref[...] = (_f32(bg_ref) * conv * (zc * _sigmoid(zc))).astype(BF16)

    seg = lambda name: pl.BlockSpec((ts, D), lambda i: (i, SEG[name]))
    prev = lambda name: pl.BlockSpec((HALO, D), lambda i: (jnp.maximum(i * (ts // HALO) - 1, 0), SEG[name]))
    return pl.pallas_call(
        body,
        grid=(S // ts,),
        in_specs=[pl.BlockSpec((ts, D), lambda i: (i, 0)), seg("z_mla"), seg("c_in"), seg("b_gate"), seg("c_gate"), seg("z_conv"),
                  prev("c_in"), prev("c_gate"), pl.BlockSpec((8, D), lambda i: (0, 0))],
        out_specs=[pl.BlockSpec((ts, D), lambda i: (i, 0))] * 2,
        out_shape=[jax.ShapeDtypeStruct((S, D), BF16)] * 2,
        compiler_params=_cparams(("arbitrary",), 48 << 20),
        name="gates_fwd",
    )(attn, proj, proj, proj, proj, proj, proj, proj, conv_w)


def _merge_fwd(proj, y_mla, y_conv, ts):
    S = proj.shape[0]

    def body(gm_ref, gc_ref, ym_ref, yc_ref, o_ref):
        o_ref[...] = (_sigmoid(_f32(gm_ref)) * _f32(ym_ref) + _sigmoid(_f32(gc_ref)) * _f32(yc_ref)).astype(BF16)

    seg = lambda name: pl.BlockSpec((ts, D), lambda i: (i, SEG[name]))
    row = pl.BlockSpec((ts, D), lambda i: (i, 0))
    return pl.pallas_call(
        body, grid=(S // ts,), in_specs=[seg("g_mla"), seg("g_conv"), row, row], out_specs=row,
        out_shape=jax.ShapeDtypeStruct((S, D), BF16), compiler_params=_cparams(("parallel",), 32 << 20), name="merge_fwd",
    )(proj, proj, y_mla, y_conv)


def _post_loss(out, x, target, g_post, ts):
    S = out.shape[0]

    def body(o_ref, x_ref, t_ref, g_ref, dy_ref, do_ref, dg_ref, loss_ref):
        i = pl.program_id(0)
        o = o_ref[...]
        r = _rms_scale(o)
        n = o * r
        g = g_ref[...]
        err = x_ref[...] + n * g - t_ref[...]
        dy = err * (1.0 / D)
        dy_ref[...] = dy
        dn = dy * g
        do_ref[...] = (r * (dn - n * jnp.mean(dn * n, axis=-1, keepdims=True))).astype(BF16)
        dg = jnp.sum(dy * n, axis=0, keepdims=True)
        part = jnp.sum(jnp.sum(err * err, axis=0, keepdims=True), axis=1, keepdims=True) * (0.5 / D)

        @pl.when(i == 0)
        def _():
            dg_ref[...] = jnp.zeros_like(dg_ref)
            loss_ref[...] = jnp.zeros_like(loss_ref)

        dg_ref[0:1, :] += dg
        loss_ref[...] += jnp.broadcast_to(part, loss_ref.shape)

    row = pl.BlockSpec((ts, D), lambda i: (i, 0))
    return pl.pallas_call(
        body, grid=(S // ts,),
        in_specs=[row, row, row, pl.BlockSpec((1, D), lambda i: (0, 0))],
        out_specs=[row, row, pl.BlockSpec((8, D), lambda i: (0, 0)), pl.BlockSpec((8, 128), lambda i: (0, 0))],
        out_shape=[jax.ShapeDtypeStruct((S, D), F32), jax.ShapeDtypeStruct((S, D), BF16),
                   jax.ShapeDtypeStruct((8, D), F32), jax.ShapeDtypeStruct((8, 128), F32)],
        compiler_params=_cparams(("arbitrary",), 40 << 20), name="post_loss",
    )(out, x, target, g_post)


def _merge_bwd(proj, y_mla, y_conv, dmerged, ts):
    S = proj.shape[0]

    def body(gm_ref, gc_ref, ym_ref, yc_ref, dm_ref, dym_ref, dyc_ref, dgm_ref, dgc_ref):
        dm = _f32(dm_ref)
        sm = _sigmoid(_f32(gm_ref))
        sc = _sigmoid(_f32(gc_ref))
        dym_ref[...] = (dm * sm).astype(BF16)
        dyc_ref[...] = (dm * sc).astype(BF16)
        dgm_ref[...] = (dm * _f32(ym_ref) * (sm * (1.0 - sm))).astype(BF16)
        dgc_ref[...] = (dm * _f32(yc_ref) * (sc * (1.0 - sc))).astype(BF16)

    seg = lambda name: pl.BlockSpec((ts, D), lambda i: (i, SEG[name]))
    row = pl.BlockSpec((ts, D), lambda i: (i, 0))
    return pl.pallas_call(
        body, grid=(S // ts,), in_specs=[seg("g_mla"), seg("g_conv"), row, row, row], out_specs=[row] * 4,
        out_shape=[jax.ShapeDtypeStruct((S, D), BF16)] * 4, compiler_params=_cparams(("parallel",), 40 << 20), name="merge_bwd",
    )(proj, proj, y_mla, y_conv, dmerged)


def _gates_bwd(proj, attn, da_mla, da_conv, conv_w, ts):
    S = proj.shape[0]
    nblk = S // ts

    def body(attn_ref, zm_ref, cin_ref, bg_ref, cg_ref, zc_ref, dam_ref, dac_ref, cin_p, cg_p, bg_n, zc_n, dac_n, w_ref,
             dattn_ref, delta_ref, dzm_ref, dcin_ref, dbg_ref, dcg_ref, dzc_ref, dw_ref):
        i = pl.program_id(0)
        first = i == 0
        last = i == nblk - 1

        @pl.when(first)
        def _():
            dw_ref[...] = jnp.zeros_like(dw_ref)

        row = lax.broadcasted_iota(jnp.int32, (ts, DV), 0)
        for c in range(D // DV):
            cols = slice(c * DV, (c + 1) * DV)
            ld = lambda ref: ref[:, cols].astype(F32)
            zm = ld(zm_ref)
            sg = _sigmoid(zm)
            attn = ld(attn_ref)
            dam = ld(dam_ref)
            dattn = dam * (zm * sg)
            dattn_ref[:, cols] = dattn.astype(BF16)
            dzm_ref[:, cols] = (dam * attn * (sg * (1.0 + zm * (1.0 - sg)))).astype(BF16)
            delta_ref[c] = _col_to_row8(jnp.sum(dattn * attn, axis=1, keepdims=True), ts)
            w = w_ref[:, cols]
            cin, cg, bg, zc = ld(cin_ref), ld(cg_ref), ld(bg_ref), ld(zc_ref)
            u = cg * cin
            up = ld(cg_p) * ld(cin_p)
            u1 = _shift_rows(u, up, 1, first)
            u2 = _shift_rows(u, up, 2, first)
            conv = w[0:1, :] * u2 + w[1:2, :] * u1 + w[2:3, :] * u
            sgc = _sigmoid(zc)
            siluc = zc * sgc
            dac = ld(dac_ref)
            dbg_ref[:, cols] = (dac * conv * siluc).astype(BF16)
            dzc_ref[:, cols] = (dac * bg * conv * (sgc * (1.0 + zc * (1.0 - sgc)))).astype(BF16)
            dconv = dac * bg * siluc
            zn = ld(zc_n)
            dconv_n = jnp.where(last, 0.0, ld(dac_n) * ld(bg_n) * (zn * _sigmoid(zn)))
            d1 = jnp.where(row == ts - 1, dconv_n[0:1, :], pltpu.roll(dconv, ts - 1, 0))
            d2 = jnp.where(row == ts - 1, dconv_n[1:2, :], jnp.where(row == ts - 2, dconv_n[0:1, :], pltpu.roll(dconv, ts - 2, 0)))
            du = w[2:3, :] * dconv + w[1:2, :] * d1 + w[0:1, :] * d2
            dcg_ref[:, cols] = (du * cin).astype(BF16)
            dcin_ref[:, cols] = (du * cg).astype(BF16)
            dw_ref[0:1, cols] += jnp.sum(dconv * u2, axis=0, keepdims=True)
            dw_ref[1:2, cols] += jnp.sum(dconv * u1, axis=0, keepdims=True)
            dw_ref[2:3, cols] += jnp.sum(dconv * u, axis=0, keepdims=True)

    seg = lambda name: pl.BlockSpec((ts, D), lambda i: (i, SEG[name]))
    prev = lambda name: pl.BlockSpec((HALO, D), lambda i: (jnp.maximum(i * (ts // HALO) - 1, 0), SEG[name]))
    nxt = lambda blk: pl.BlockSpec((HALO, D), lambda i: (jnp.minimum((i + 1) * (ts // HALO), S // HALO - 1), blk))
    row = pl.BlockSpec((ts, D), lambda i: (i, 0))
    return pl.pallas_call(
        body, grid=(nblk,),
        in_specs=[row, seg("z_mla"), seg("c_in"), seg("b_gate"), seg("c_gate"), seg("z_conv"), row, row,
                  prev("c_in"), prev("c_gate"), nxt(SEG["b_gate"]), nxt(SEG["z_conv"]), nxt(0), pl.BlockSpec((8, D), lambda i: (0, 0))],
        out_specs=[row, pl.BlockSpec((H, 8, ts), lambda i: (0, 0, i)), row, row, row, row, row, pl.BlockSpec((8, D), lambda i: (0, 0))],
        out_shape=[jax.ShapeDtypeStruct((S, D), BF16), jax.ShapeDtypeStruct((H, 8, S), F32)] + [jax.ShapeDtypeStruct((S, D), BF16)] * 5
        + [jax.ShapeDtypeStruct((8, D), F32)],
        compiler_params=_cparams(("arbitrary",), 56 << 20), name="gates_bwd",
    )(attn, proj, proj, proj, proj, proj, da_mla, da_conv, proj, proj, proj, proj, da_conv, conv_w)


def _flash_bwd(q, k, v, do, lse, delta, tq, hb=2):
    S = q.shape[1]
    nq = S // tq
    scale = SM_SCALE
    tsub = min(256, tq)
    nsub = tq // tsub

    qi_tab, kj_tab = _causal_pairs(nq, key_major=True)
    npairs = nq * (nq + 1) // 2

    def body(qi_ref, kj_ref, q_ref, k_ref, v_ref, do_ref, lse_ref, dl_ref, dq_ref, dk_ref, dv_ref, dk_sc, dv_sc, dq_sc):
        p = pl.program_id(1)
        qi, kj = qi_ref[p], kj_ref[p]

        @pl.when(p == 0)
        def _():
            dq_sc[...] = jnp.zeros_like(dq_sc)

        @pl.when(qi == kj)
        def _():
            dk_sc[...] = jnp.zeros_like(dk_sc)
            dv_sc[...] = jnp.zeros_like(dv_sc)

        def step(diag):
            chains = [(h, u) for h in range(hb) for u in range(nsub)]

            def first_matmuls(h, u):
                sub = slice(u * tsub, (u + 1) * tsub)
                st = lax.dot_general(k_ref[h], q_ref[h, sub, :], NT, preferred_element_type=F32)
                dpt = lax.dot_general(v_ref[h], do_ref[sub, h * DV:(h + 1) * DV], NT, preferred_element_type=F32)
                return st, dpt

            nxt = first_matmuls(*chains[0])
            for ci, (h, u) in enumerate(chains):
                st, dpt = nxt
                if ci + 1 < len(chains):
                    nxt = first_matmuls(*chains[ci + 1])
                sub = slice(u * tsub, (u + 1) * tsub)
                pt = jnp.exp2(st - lse_ref[h, 0:1, sub])
                if diag:
                    r = lax.broadcasted_iota(jnp.int32, (tq, tsub), 0)
                    c = lax.broadcasted_iota(jnp.int32, (tq, tsub), 1) + u * tsub
                    pt = jnp.where(r <= c, pt, 0.0)
                dst = (pt * (dpt - dl_ref[h, 0:1, sub])).astype(BF16)
                dv_sc[h] += jnp.dot(pt.astype(BF16), do_ref[sub, h * DV:(h + 1) * DV], preferred_element_type=F32)
                dk_sc[h] += jnp.dot(dst, q_ref[h, sub, :], preferred_element_type=F32)
                rows = pl.ds(pl.multiple_of(qi * tq + u * tsub, tsub), tsub)
                dq_sc[h, rows, :] += lax.dot_general(dst, k_ref[h], TN, preferred_element_type=F32)

        @pl.when(qi > kj)
        def _():
            step(False)

        @pl.when(qi == kj)
        def _():
            step(True)

        @pl.when(qi == nq - 1)
        def _():
            dk_ref[...] = (dk_sc[...] * (1.0 / LOG2E)).astype(BF16)
            dv_ref[...] = dv_sc[...].astype(BF16)

        @pl.when(p == npairs - 1)
        def _():
            dq_ref[...] = (dq_sc[...] * scale).astype(BF16)

    return pl.pallas_call(
        body,
        grid_spec=pltpu.PrefetchScalarGridSpec(
            num_scalar_prefetch=2,
            grid=(H // hb, npairs),
            in_specs=[pl.BlockSpec((hb, tq, HW), lambda h, p, qi, kj: (h, qi[p], 0)),
                      pl.BlockSpec((hb, tq, HW), lambda h, p, qi, kj: (h, kj[p], 0)),
                      pl.BlockSpec((hb, tq, DV), lambda h, p, qi, kj: (h, kj[p], 0)),
                      pl.BlockSpec((tq, hb * DV), lambda h, p, qi, kj: (qi[p], h)),
                      pl.BlockSpec((hb, 8, tq), lambda h, p, qi, kj: (h, 0, qi[p])),
                      pl.BlockSpec((hb, 8, tq), lambda h, p, qi, kj: (h, 0, qi[p]))],
            out_specs=[pl.BlockSpec((hb, S, HW), lambda h, p, qi, kj: (h, 0, 0)),
                       pl.BlockSpec((hb, tq, HW), lambda h, p, qi, kj: (h, kj[p], 0)),
                       pl.BlockSpec((hb, tq, DV), lambda h, p, qi, kj: (h, kj[p], 0))],
            scratch_shapes=[pltpu.VMEM((hb, tq, HW), F32), pltpu.VMEM((hb, tq, DV), F32), pltpu.VMEM((hb, S, HW), F32)],
        ),
        out_shape=[jax.ShapeDtypeStruct((H, S, HW), BF16), jax.ShapeDtypeStruct((H, S, HW), BF16), jax.ShapeDtypeStruct((H, S, DV), BF16)],
        compiler_params=_cparams(("parallel", "arbitrary"), VMEM_CAP),
        name="flash_bwd",
    )(qi_tab, kj_tab, q, k, v, do, lse, delta)


def _rms_bwd(xf, g, dn_out):
    r = _rms_scale(xf)
    n = xf * r
    dn = dn_out * g
    dx = r * (dn - n * jnp.mean(dn * n, axis=-1, keepdims=True))
    return dx, jnp.sum(dn_out * n, axis=0, keepdims=True)


def _qkv_bwd(proj, pos, freq, g_qa, g_kva, w_q, w_kv, dq, dk, dv, segments, ts):
    S = proj.shape[0]
    nseg = len(segments)

    def body(qa_ref, ckv_ref, pos_ref, freq_ref, gq_ref, gk_ref, wq_ref, wkv_ref, dq_ref, dk_ref, dv_ref, *rest):
        seg_refs, (dqp_ref, dkvp_ref, dproj_ref, dgq_ref, dgk_ref) = rest[:nseg], rest[nseg:]
        for j, seg_ref in enumerate(seg_refs):
            dproj_ref[:, j * D:(j + 1) * D] = seg_ref[...]
        dqa_ref = dproj_ref.at[:, OFF_QA:OFF_QA + RQ]
        dckv_ref = dproj_ref.at[:, OFF_CKV:OFF_CKV + RKV]
        dkr_ref = dproj_ref.at[:, OFF_KR:OFF_KR + 128]
        i = pl.program_id(0)
        cos, sin, lane = _rope_tables(pos_ref, freq_ref, ts)
        sgn_sin = jnp.where(lane < 32, sin, -sin)
        live = lane < DR
        kr_sum = jnp.zeros((ts, 128), F32)
        for h in range(H):
            dqp_ref[:, h * HW:h * HW + DN] = dq_ref[h, :, 0:DN]
            t = dq_ref[h, :, DN:HW].astype(F32)
            dqp_ref[:, h * HW + DN:(h + 1) * HW] = jnp.where(live, t * cos + _rope_swap(t, lane) * sgn_sin, 0.0).astype(BF16)
            dkvp_ref[:, h * HW:h * HW + DN] = dk_ref[h, :, 0:DN]
            dkvp_ref[:, h * HW + DN:(h + 1) * HW] = dv_ref[h]
            kr_sum = kr_sum + dk_ref[h, :, DN:HW].astype(F32)
        dkr_ref[...] = jnp.where(live, kr_sum * cos + _rope_swap(kr_sum, lane) * sgn_sin, 0.0).astype(BF16)
        dqn = lax.dot_general(dqp_ref[...], wq_ref[...], NT, preferred_element_type=F32)
        dkvn = lax.dot_general(dkvp_ref[...], wkv_ref[...], NT, preferred_element_type=F32)
        dqa, dgq = _rms_bwd(_f32(qa_ref), gq_ref[...], dqn)
        dckv, dgk = _rms_bwd(_f32(ckv_ref), gk_ref[...], dkvn)
        dqa_ref[...] = dqa.astype(BF16)
        dckv_ref[...] = dckv.astype(BF16)

        @pl.when(i == 0)
        def _():
            dgq_ref[...] = jnp.zeros_like(dgq_ref)
            dgk_ref[...] = jnp.zeros_like(dgk_ref)

        dgq_ref[0:1, :] += dgq
        dgk_ref[0:1, :] += dgk

    rowb = lambda w, blk: pl.BlockSpec((ts, w), lambda i: (i, blk))
    full = lambda a: pl.BlockSpec(a.shape, lambda i: (0,) * a.ndim)
    heads = lambda w: pl.BlockSpec((H, ts, w), lambda i: (0, i, 0))
    return pl.pallas_call(
        body, grid=(S // ts,),
        in_specs=[rowb(RQ, OFF_QA // RQ), rowb(RKV, OFF_CKV // RKV), pl.BlockSpec((ts, 1), lambda i: (i, 0)), full(freq), full(g_qa), full(g_kva),
                  full(w_q), full(w_kv), heads(HW), heads(HW), heads(DV)] + [rowb(D, 0)] * nseg,
        out_specs=[rowb(H * HW, 0), rowb(H * HW, 0), rowb(NP, 0),
                   pl.BlockSpec((8, RQ), lambda i: (0, 0)), pl.BlockSpec((8, RKV), lambda i: (0, 0))],
        out_shape=[jax.ShapeDtypeStruct((S, H * HW), BF16), jax.ShapeDtypeStruct((S, H * HW), BF16), jax.ShapeDtypeStruct((S, NP), BF16),
                   jax.ShapeDtypeStruct((8, RQ), F32), jax.ShapeDtypeStruct((8, RKV), F32)],
        compiler_params=_cparams(("arbitrary",), 56 << 20), name="qkv_bwd",
    )(proj, proj, pos, freq, g_qa, g_kva, w_q, w_kv, dq, dk, dv, *segments)


def _prenorm_bwd(x, g, dh, dy, ts):
    S = x.shape[0]

    def body(x_ref, g_ref, dh_ref, dy_ref, gx_ref, dg_ref):
        dx, dg = _rms_bwd(x_ref[...], g_ref[...], dh_ref[...])
        gx_ref[...] = dy_ref[...] + dx

        @pl.when(pl.program_id(0) == 0)
        def _():
            dg_ref[...] = jnp.zeros_like(dg_ref)

        dg_ref[0:1, :] += dg

    row = pl.BlockSpec((ts, D), lambda i: (i, 0))
    return pl.pallas_call(
        body, grid=(S // ts,), in_specs=[row, pl.BlockSpec((1, D), lambda i: (0, 0)), row, row],
        out_specs=[row, pl.BlockSpec((8, D), lambda i: (0, 0))],
        out_shape=[jax.ShapeDtypeStruct((S, D), F32), jax.ShapeDtypeStruct((8, D), F32)],
        compiler_params=_cparams(("arbitrary",), 40 << 20), name="prenorm_bwd",
    )(x, g, dh, dy)


def _local_step(x, pos, target, g_pre, g_qa, g_kva, g_post, w_in, other_weights, reduce_grads=None):
    S = x.shape[0]
    ts = min(256, S)
    tq = min(512, S)
    tm = min(1024, S)
    mm = functools.partial(_matmul, tm=tm)
    freq = _inv_freq_tile()

    h, ht = _prenorm(x, g_pre, ts)
    if len(other_weights) == 2:
        gather, assemble = other_weights
        proj, gathered = _matmul(h, w_in, mode="nn", out_dtype=BF16, tm=1024, tn=1408, tk=D, name="mm_proj", exchange=gather)
        other_weights = assemble(*gathered)
    else:
        proj = _matmul(h, w_in, mode="nn", out_dtype=BF16, tm=1024, tn=1408, tk=D, name="mm_proj")
    w_q, w_kv, conv_w8, w_o_mla, w_o_conv, w_out = other_weights
    q, k, v, vt, qn, kvn = _qkv_prep(proj, pos, freq, g_qa, g_kva, w_q, w_kv, ts)
    attn, lse = _flash_fwd(q, k, vt, tq)
    a_mla, a_conv = _gates_fwd(proj, attn, conv_w8, ts)
    y_mla = mm(a_mla, w_o_mla, mode="nn", out_dtype=BF16, tn=D, tk=D, name="mm_y_mla")
    y_conv = mm(a_conv, w_o_conv, mode="nn", out_dtype=BF16, tn=D, tk=D, name="mm_y_conv")
    merged = _merge_fwd(proj, y_mla, y_conv, ts)
    out = mm(merged, w_out, mode="nn", out_dtype=F32, tn=D, tk=D, name="mm_out")
    dy, dout, dg_post, loss = _post_loss(out, x, target, g_post, ts)

    dmerged = mm(dout, w_out, mode="nt", out_dtype=BF16, tn=D, tk=D, name="mm_dmerged")
    dw_out = _matmul(merged, dout, mode="tn", out_dtype=F32, tm=1024, tn=1024, tk=2048, name="mm_dw_out")
    dy_mla, dy_conv, dg_mla, dg_conv = _merge_bwd(proj, y_mla, y_conv, dmerged, ts)
    da_mla = mm(dy_mla, w_o_mla, mode="nt", out_dtype=BF16, tn=D, tk=D, name="mm_da_mla")
    da_conv = mm(dy_conv, w_o_conv, mode="nt", out_dtype=BF16, tn=D, tk=D, name="mm_da_conv")
    dw_o_mla = _matmul(a_mla, dy_mla, mode="tn", out_dtype=F32, tm=1024, tn=1024, tk=2048, name="mm_dw_o_mla")
    dw_o_conv = _matmul(a_conv, dy_conv, mode="tn", out_dtype=F32, tm=1024, tn=1024, tk=2048, name="mm_dw_o_conv")
    dattn, delta, dz_mla, dc_in, db_gate, dc_gate, dz_conv, dconv_w = _gates_bwd(proj, attn, da_mla, da_conv, conv_w8, min(128, S))
    dq, dk, dv = _flash_bwd(q, k, v, dattn, lse, delta, tq)
    dqp, dkvp, dproj, dg_qa, dg_kva = _qkv_bwd(proj, pos, freq, g_qa, g_kva, w_q, w_kv, dq, dk, dv,
                                               [dz_mla, dc_in, db_gate, dc_gate, dz_conv, dg_mla, dg_conv], min(128, S))
    dw_q = _matmul(qn, dqp, mode="tn", out_dtype=F32, tm=RQ, tn=1024, tk=2048, name="mm_dw_q")
    dw_kv = _matmul(kvn, dkvp, mode="tn", out_dtype=F32, tm=RKV, tn=1024, tk=2048, name="mm_dw_kv")
    dw_in = _matmul(ht, dproj, mode="nn", out_dtype=F32, tm=1024, tn=1408, tk=2048, name="mm_dw_in")
    res = dict(dw_in=dw_in, dw_q=dw_q, dw_kv=dw_kv, dconv_w=dconv_w, dw_o_mla=dw_o_mla, dw_o_conv=dw_o_conv, dw_out=dw_out)
    if reduce_grads is None:
        dh = _matmul(dproj, w_in, mode="nt", out_dtype=F32, tm=1024, tn=D, tk=1408, name="mm_dh")
    else:
        exchange, finish = reduce_grads(res)
        dh, got = _matmul(dproj, w_in, mode="nt", out_dtype=F32, tm=1024, tn=D, tk=1408, name="mm_dh", exchange=exchange)
        res["reduced"] = finish(got)
    grad_x, dg_pre = _prenorm_bwd(x, g_pre, dh, dy, ts)
    res.update(loss=loss, grad_x=grad_x, dg_pre=dg_pre, dg_qa=dg_qa, dg_kva=dg_kva, dg_post=dg_post)
    return res


def _my_id():
    return lax.axis_index("x") * 4 + lax.axis_index("y") * 2 + lax.axis_index("c")


def _place():
    x, y, c = lax.axis_index("x"), lax.axis_index("y"), lax.axis_index("c")
    return (x, y, c), (x, y, 1 - c), [(1 - x, y), (x, 1 - y), (1 - x, 1 - y)]


def _slot(px, py, pc):
    return 4 * px + 2 * py + pc


def _all_gather(arrays, name):
    n = len(arrays)

    def body(*refs):
        ins, outs = refs[:n], refs[n:2 * n]
        send_sems, recv_sems, local_sems = refs[2 * n:]
        me, sib, chips = _place()
        c = me[2]

        def copy(a, k, block, to, src=None):
            rows = outs[a].at[_slot(*block)]
            return pltpu.make_async_remote_copy(src_ref=rows if src is None else src, dst_ref=rows, send_sem=send_sems.at[a, k],
                                                recv_sem=recv_sems.at[a, k], device_id=to, device_id_type=MESH)

        local = [pltpu.make_async_copy(ins[a], outs[a].at[_slot(*me)], local_sems.at[a]) for a in range(n)]
        for cp in local:
            cp.start()
        sends = []
        for a in range(n):
            sends.append(copy(a, 0, me, sib, src=ins[a]))
            sends += [copy(a, 1 + j, me, (*chip, c), src=ins[a]) for j, chip in enumerate(chips)]
        for cp in sends:
            cp.start()
        for a in range(n):
            for j, chip in enumerate(chips):
                copy(a, 1 + j, (*chip, c), me).wait_recv()
                fwd = copy(a, 4 + j, (*chip, c), sib)
                fwd.start()
                sends.append(fwd)
        for a in range(n):
            copy(a, 0, sib, me).wait_recv()
            for j, chip in enumerate(chips):
                copy(a, 4 + j, (*chip, 1 - c), me).wait_recv()
        for cp in sends:
            cp.wait_send()
        for cp in local:
            cp.wait()

    anyspec = pl.BlockSpec(memory_space=pl.ANY)
    return pl.pallas_call(
        body,
        in_specs=[anyspec] * n,
        out_specs=[anyspec] * n,
        out_shape=[jax.ShapeDtypeStruct((NDEV,) + a.shape, a.dtype) for a in arrays],
        scratch_shapes=[pltpu.SemaphoreType.DMA((n, NDEV - 1)), pltpu.SemaphoreType.DMA((n, NDEV - 1)), pltpu.SemaphoreType.DMA((n,))],
        name=name,
    )(*arrays)


def _slab_exchange(arrays, nslots, pick):
    n = len(arrays)

    def copies(ins, outs, sems):
        send_sems, recv_sems = sems
        return [pltpu.make_async_remote_copy(src_ref=ins[a].at[pick(k)[0]], dst_ref=outs[a].at[k], send_sem=send_sems.at[a, k],
                                             recv_sem=recv_sems.at[a, k], device_id=pick(k)[1], device_id_type=MESH)
                for a in range(n) for k in range(nslots)]

    def start(ins, outs, sems):
        for cp in copies(ins, outs, sems):
            cp.start()

    def finish(ins, outs, sems):
        cps = copies(ins, outs, sems)
        for cp in cps:
            cp.wait_recv()
        for cp in cps:
            cp.wait_send()

    return _Exchange(arrays, [jax.ShapeDtypeStruct((nslots,) + a.shape[1:], a.dtype) for a in arrays],
                     [pltpu.SemaphoreType.DMA((n, nslots)), pltpu.SemaphoreType.DMA((n, nslots))], start, finish)


def _direct_gather(arrays):
    n = len(arrays)

    def peer(d):
        p = (_my_id() + d) % NDEV
        return (p // 4, (p // 2) % 2, p % 2), p

    def copies(ins, outs, sems, receiving):
        send_sems, recv_sems, _ = sems
        slot = lambda d: peer(NDEV - d)[1] if receiving else _my_id()
        return [pltpu.make_async_remote_copy(src_ref=ins[a], dst_ref=outs[a].at[slot(d)], send_sem=send_sems.at[a, d - 1],
                                             recv_sem=recv_sems.at[a, d - 1], device_id=peer(d)[0], device_id_type=MESH)
                for d in range(1, NDEV) for a in range(n)]

    def local(ins, outs, sems):
        return [pltpu.make_async_copy(ins[a], outs[a].at[_my_id()], sems[2].at[a]) for a in range(n)]

    def start(ins, outs, sems):
        for cp in local(ins, outs, sems) + copies(ins, outs, sems, False):
            cp.start()

    def finish(ins, outs, sems):
        for cp in copies(ins, outs, sems, True):
            cp.wait_recv()
        for cp in copies(ins, outs, sems, False):
            cp.wait_send()
        for cp in local(ins, outs, sems):
            cp.wait()

    return _Exchange(arrays, [jax.ShapeDtypeStruct((NDEV,) + a.shape, a.dtype) for a in arrays],
                     [pltpu.SemaphoreType.DMA((n, NDEV - 1)), pltpu.SemaphoreType.DMA((n, NDEV - 1)), pltpu.SemaphoreType.DMA((n,))],
                     start, finish)


def _run_exchange(exchange, name):
    n = len(exchange.arrays)

    def body(*refs):
        ins, outs, sems = refs[:n], refs[n:2 * n], refs[2 * n:]
        exchange.start(ins, outs, sems)
        exchange.finish(ins, outs, sems)

    anyspec = pl.BlockSpec(memory_space=pl.ANY)
    return pl.pallas_call(body, in_specs=[anyspec] * n, out_specs=[anyspec] * n, out_shape=exchange.out_shapes,
                          scratch_shapes=exchange.sem_shapes, name=name)(*exchange.arrays)


def _slabs_bf16(x, rows_per_block, name):
    n, R, C = x.shape
    tr = min(rows_per_block, R)

    def body(x_ref, o_ref):
        o_ref[...] = x_ref[...].astype(BF16)

    blk = pl.BlockSpec((1, tr, C), lambda k, i: (k, i, 0))
    return pl.pallas_call(body, grid=(n, R // tr), in_specs=[blk], out_specs=blk, out_shape=jax.ShapeDtypeStruct(x.shape, BF16),
                          compiler_params=_cparams(("parallel", "parallel"), 32 << 20), name=name)(x)


def _to_sibling(k):
    me, sib, chips = _place()
    dest = sib if k == 0 else (*chips[k - 1], sib[2])
    return _slot(*dest), sib


def _to_chips(k):
    me, sib, chips = _place()
    return k, (*chips[k], me[2])


def _chip_sums(own, got, slots, rows_per_block, name):
    _, R, C = own.shape
    tr = min(rows_per_block, R)

    def body(slots_ref, own_ref, got_ref, o_ref):
        o_ref[0] = (own_ref[0].astype(F32) + got_ref[0].astype(F32)).astype(BF16)

    return pl.pallas_call(
        body,
        grid_spec=pltpu.PrefetchScalarGridSpec(
            num_scalar_prefetch=1, grid=(3, R // tr),
            in_specs=[pl.BlockSpec((1, tr, C), lambda j, i, s: (s[1 + j], i, 0)), pl.BlockSpec((1, tr, C), lambda j, i, s: (1 + j, i, 0))],
            out_specs=pl.BlockSpec((1, tr, C), lambda j, i, s: (j, i, 0))),
        out_shape=jax.ShapeDtypeStruct((3, R, C), BF16),
        compiler_params=_cparams(("parallel", "parallel"), 32 << 20), name=name,
    )(slots, own, got)


def _adamw_math(w, g, m, v):
    m = ADAM_B1 * m + (1.0 - ADAM_B1) * g
    v = ADAM_B2 * v + (1.0 - ADAM_B2) * (g * g)
    m_hat = m / (1.0 - ADAM_B1 ** ADAM_STEP)
    v_hat = v / (1.0 - ADAM_B2 ** ADAM_STEP)
    delta = -ADAM_LR * (m_hat / (jnp.sqrt(v_hat) + ADAM_EPS) + ADAM_WD * w)
    return delta, m, v


def _reduce_adamw(own, got1, got2, slots, w, m, v, rows_per_block, name):
    _, R, C = own.shape
    tr = min(rows_per_block, R)
    assert R % tr == 0

    def body(slots_ref, own_ref, g1_ref, g2_ref, w_ref, m_ref, v_ref, g_ref, d_ref, nm_ref, nv_ref):
        g = own_ref[0] + g1_ref[0].astype(F32)
        for j in range(3):
            g = g + g2_ref[j].astype(F32)
        g_ref[...] = g
        d, nm, nv = _adamw_math(w_ref[...], g, m_ref[...], v_ref[...])
        d_ref[...] = d
        nm_ref[...] = nm
        nv_ref[...] = nv

    blk = pl.BlockSpec((tr, C), lambda i, s: (i, 0))
    return pl.pallas_call(
        body,
        grid_spec=pltpu.PrefetchScalarGridSpec(
            num_scalar_prefetch=1, grid=(R // tr,),
            in_specs=[pl.BlockSpec((1, tr, C), lambda i, s: (s[0], i, 0)), pl.BlockSpec((1, tr, C), lambda i, s: (0, i, 0)),
                      pl.BlockSpec((3, tr, C), lambda i, s: (0, i, 0)), blk, blk, blk],
            out_specs=[blk] * 4),
        out_shape=[jax.ShapeDtypeStruct((R, C), F32)] * 4,
        compiler_params=_cparams(("parallel",), 48 << 20), name=name,
    )(slots, own, got1, got2, w, m, v)


def _reduce_adamw_t(own, got1, got2, slots, wt, mt, vt, rows_per_block, name):
    R, C = own.shape
    tr = min(rows_per_block, R)
    assert R % tr == 0

    def body(slots_ref, own_ref, g1_ref, g2_ref, w_ref, m_ref, v_ref, g_ref, d_ref, nm_ref, nv_ref):
        g = own_ref[...] + g1_ref[0].astype(F32)
        for j in range(3):
            g = g + g2_ref[j].astype(F32)
        g = jnp.transpose(g)
        g_ref[...] = g
        d, nm, nv = _adamw_math(w_ref[...], g, m_ref[...], v_ref[...])
        d_ref[...] = d
        nm_ref[...] = nm
        nv_ref[...] = nv

    blk = pl.BlockSpec((C, tr), lambda i, s: (0, i))
    return pl.pallas_call(
        body,
        grid_spec=pltpu.PrefetchScalarGridSpec(
            num_scalar_prefetch=1, grid=(R // tr,),
            in_specs=[pl.BlockSpec((tr, C), lambda i, s: (i, 0)), pl.BlockSpec((1, tr, C), lambda i, s: (0, i, 0)),
                      pl.BlockSpec((3, tr, C), lambda i, s: (0, i, 0)), blk, blk, blk],
            out_specs=[blk] * 4),
        out_shape=[jax.ShapeDtypeStruct((C, R), F32)] * 4,
        compiler_params=_cparams(("parallel",), 48 << 20), name=name,
    )(slots, own, got1, got2, wt, mt, vt)


def _sum_adamw(parts, w, m, v, rows_per_block, name):
    n, R, C = parts.shape
    tr = min(rows_per_block, R)
    assert R % tr == 0

    def body(p_ref, w_ref, m_ref, v_ref, g_ref, d_ref, nm_ref, nv_ref):
        g = p_ref[0]
        for j in range(1, n):
            g = g + p_ref[j]
        g_ref[...] = g
        d, nm, nv = _adamw_math(w_ref[...], g, m_ref[...], v_ref[...])
        d_ref[...] = d
        nm_ref[...] = nm
        nv_ref[...] = nv

    blk = pl.BlockSpec((tr, C), lambda i: (i, 0))
    return pl.pallas_call(
        body, grid=(R // tr,),
        in_specs=[pl.BlockSpec((n, tr, C), lambda i: (0, i, 0)), blk, blk, blk],
        out_specs=[blk] * 4,
        out_shape=[jax.ShapeDtypeStruct((R, C), F32)] * 4,
        compiler_params=_cparams(("parallel",), 48 << 20), name=name,
    )(parts, w, m, v)


def kernel(x, positions, pre_norm_g, w_in, q_a_norm_g, w_q_b, kv_a_norm_g, w_kv_b, conv_w, w_o_mla, w_o_conv, w_out, post_norm_g, loss_target, m_pre_norm_g, m_w_in, m_q_a_norm_g, m_w_q_b, m_kv_a_norm_g, m_w_kv_b, m_conv_w, m_w_o_mla, m_w_o_conv, m_w_out, m_post_norm_g, v_pre_norm_g, v_w_in, v_q_a_norm_g, v_w_q_b, v_kv_a_norm_g, v_w_kv_b, v_conv_w, v_w_o_mla, v_w_o_conv, v_w_out, v_post_norm_g):
    S = x.shape[1]
    conv_pad = jnp.zeros((8, 256), F32).at[0:3, :].set(conv_w)
    g_in, = _all_gather([w_in.astype(BF16)], "all_gather_w_in")
    w_in_f = _assemble_w_in(g_in)
    gather_rest = _direct_gather([w_q_b.astype(BF16), w_kv_b.astype(BF16), conv_pad, w_o_mla.astype(BF16), w_o_conv.astype(BF16),
                                  w_out.astype(BF16)])

    def assemble_rest(g_q, g_kv, g_cw, g_om, g_oc, g_out):
        return (_assemble_w_q(g_q), _concat_cols(g_kv, BF16, "assemble_w_kv"), _concat_cols(g_cw, F32, "assemble_conv_w"),
                g_om.reshape(D, D), g_oc.reshape(D, D), g_out.reshape(D, D))

    gnames = ["w_in", "w_q", "w_kv", "conv_w", "w_o_mla", "w_o_conv", "w_out"]
    (mx, my, mc), _, chips = _place()
    slots = jnp.stack([_slot(mx, my, mc)] + [_slot(cx, cy, mc) for cx, cy in chips]).astype(jnp.int32)

    def reduce_grads(r):
        own_in, slabs_in = _split_dw_in(r["dw_in"], slots)
        own = [own_in, _split_dw_q(r["dw_q"]), _split_cols(r["dw_kv"], NDEV, "split_dw_kv"),
               _split_cols(r["dconv_w"], NDEV, "split_dconv_w"), r["dw_o_mla"].reshape(NDEV, D // NDEV, D),
               r["dw_o_conv"].reshape(NDEV, D // NDEV, D), r["dw_out"].reshape(NDEV, D // NDEV, D)]
        slabs = [slabs_in] + [_slabs_bf16(g, 128, "bf16_" + nm) for g, nm in zip(own[1:], gnames[1:])]
        to_sibling = _slab_exchange(slabs, 4, _to_sibling)
        got1 = _run_exchange(to_sibling, "grads_to_sibling")
        sums = [_chip_sums(o, g1, slots, 128, "chip_sum_" + nm) for o, g1, nm in zip([slabs_in] + own[1:], got1, gnames)]
        return _slab_exchange(sums, 3, _to_chips), lambda got2: (own, got1, got2)

    row2 = lambda a: a.reshape(1, -1)
    r = _local_step(x[0], positions.reshape(S, 1), loss_target[0], row2(pre_norm_g), row2(q_a_norm_g), row2(kv_a_norm_g),
                    row2(post_norm_g), w_in_f, (gather_rest, assemble_rest), reduce_grads)
    own, got1, got2 = r["reduced"]
    small = jnp.concatenate([r["dg_pre"][0:1], r["dg_post"][0:1], jnp.pad(r["dg_qa"][0:1], ((0, 0), (0, D - RQ))),
                             jnp.pad(r["dg_kva"][0:1], ((0, 0), (0, D - RKV))), jnp.pad(r["loss"][0:1], ((0, 0), (0, D - 128))),
                             jnp.zeros((3, D), F32)], axis=0)
    p_small, = _all_gather([small], "all_gather_small")
    pad8 = lambda a: jnp.zeros((8, 256), F32).at[0:3, :].set(a)
    params = [(w_in, m_w_in, v_w_in), (w_q_b, m_w_q_b, v_w_q_b), (w_kv_b, m_w_kv_b, v_w_kv_b), (conv_pad, pad8(m_conv_w), pad8(v_conv_w)),
              (w_o_mla, m_w_o_mla, v_w_o_mla), (w_o_conv, m_w_o_conv, v_w_o_conv), (w_out, m_w_out, v_w_out)]
    o_q, o_kv, o_cw, o_om, o_oc, o_out = [
        _reduce_adamw(o, g1, g2, slots, w, m, v, 128, "adamw_" + nm)
        for o, g1, g2, (w, m, v), nm in list(zip(own, got1, got2, params, gnames))[1:]]
    o_cw = [a[0:3] for a in o_cw]
    o_in = [a.T for a in _reduce_adamw_t(own[0], got1[0], got2[0], slots, w_in.T, m_w_in.T, v_w_in.T, 128, "adamw_w_in")]
    padv = lambda a: jnp.pad(a.reshape(1, -1), ((0, 0), (0, D - a.shape[0])))
    stack = lambda pre, post, qa, kva: jnp.concatenate([row2(pre), row2(post), padv(qa), padv(kva), jnp.zeros((4, D), F32)], axis=0)
    o_g = _sum_adamw(p_small, stack(pre_norm_g, post_norm_g, q_a_norm_g, kv_a_norm_g),
                     stack(m_pre_norm_g, m_post_norm_g, m_q_a_norm_g, m_kv_a_norm_g),
                     stack(v_pre_norm_g, v_post_norm_g, v_q_a_norm_g, v_kv_a_norm_g), 8, "adamw_gains")
    loss = o_g[0][4, 0]
    outs = {}
    for idx, kind in enumerate(("grad", "delta", "new_m", "new_v")):
        o = o_g[idx]
        outs[kind] = dict(pre_norm_g=o[0], w_in=o_in[idx], q_a_norm_g=o[2, 0:RQ], w_q_b=o_q[idx], kv_a_norm_g=o[3, 0:RKV],
                          w_kv_b=o_kv[idx], conv_w=o_cw[idx], w_o_mla=o_om[idx], w_o_conv=o_oc[idx], w_out=o_out[idx], post_norm_g=o[1])
    names = ["pre_norm_g", "w_in", "q_a_norm_g", "w_q_b", "kv_a_norm_g", "w_kv_b", "conv_w", "w_o_mla", "w_o_conv", "w_out", "post_norm_g"]
    return (loss, r["grad_x"][None], *[outs["grad"][n] for n in names], *[outs["delta"][n] for n in names],
            *[outs["new_m"][n] for n in names], *[outs["new_v"][n] for n in names])
```

```python
import functools
import math

import jax
import jax.numpy as jnp
from jax import lax
from jax.experimental import pallas as pl
from jax.experimental.pallas import tpu as pltpu

F32 = jnp.float32
BF16 = jnp.bfloat16

NDEV = 8
D = 2048
H = 16
DN = 128
DR = 64
DV = 128
RQ = 512
RKV = 512
HW = 256
DVA = DV + 16
ROPE_THETA = 10000.0
RMS_EPS = 1e-6
N_IN = 15424
SHARD_IN = N_IN // NDEV
SMALL = RQ + RKV + DR
NP = 7 * D + RQ + RKV + 128
SEG = dict(z_mla=0, c_in=1, b_gate=2, c_gate=3, z_conv=4, g_mla=5, g_conv=6)
OFF_QA = 7 * D
OFF_CKV = OFF_QA + RQ
OFF_KR = OFF_CKV + RKV
EXT = 2176
VMEM_CAP = 56 * 1024 * 1024

ADAM_LR = 0.001
ADAM_B1 = 0.9
ADAM_B2 = 0.999
ADAM_EPS = 1e-08
ADAM_WD = 0.01
ADAM_STEP = 10

LOG2E = math.log2(math.e)
SM_SCALE = 1.0 / math.sqrt(DN + DR)
QSCALE = SM_SCALE * LOG2E
NN = (((1,), (0,)), ((), ()))
NT = (((1,), (1,)), ((), ()))
TN = (((0,), (0,)), ((), ()))
MESH = pl.DeviceIdType.MESH


def _cparams(sem, vmem_bytes):
    return pltpu.CompilerParams(dimension_semantics=sem, vmem_limit_bytes=int(min(VMEM_CAP, max(vmem_bytes, 16 << 20))))


def _nbytes(shape, dtype):
    return math.prod(shape) * jnp.dtype(dtype).itemsize


class _Exchange:
    def __init__(self, arrays, out_shapes, sem_shapes, start, finish):
        self.arrays, self.out_shapes, self.sem_shapes, self.start, self.finish = arrays, out_shapes, sem_shapes, start, finish
        self.n_sems = len(sem_shapes)


def _matmul(a, b, *, mode, out_dtype, tm, tn, tk, name, m_outer=False, exchange=None):
    if mode == "nn":
        (M, K), (K2, N) = a.shape, b.shape
    elif mode == "nt":
        (M, K), (N, K2) = a.shape, b.shape
    else:
        (K, M), (K2, N) = a.shape, b.shape
    assert K == K2, (a.shape, b.shape, mode)
    tm, tn, tk = min(tm, M), min(tn, N), min(tk, K)
    assert M % tm == 0 and N % tn == 0 and K % tk == 0, (M, N, K, tm, tn, tk)
    ni, nj, nk = M // tm, N // tn, K // tk
    dims = dict(nn=NN, nt=NT, tn=TN)[mode]

    if m_outer:
        grid = (ni, nj, nk)
        ij = lambda g0, g1: (g0, g1)
    else:
        grid = (nj, ni, nk)
        ij = lambda g0, g1: (g1, g0)

    if mode == "tn":
        a_spec = pl.BlockSpec((tk, tm), lambda g0, g1, k: (k, ij(g0, g1)[0]))
        a_tile = (tk, tm)
    else:
        a_spec = pl.BlockSpec((tm, tk), lambda g0, g1, k: (ij(g0, g1)[0], k))
        a_tile = (tm, tk)
    if mode == "nt":
        b_spec = pl.BlockSpec((tn, tk), lambda g0, g1, k: (ij(g0, g1)[1], k))
    else:
        b_spec = pl.BlockSpec((tk, tn), lambda g0, g1, k: (k, ij(g0, g1)[1]))
    o_spec = pl.BlockSpec((tm, tn), lambda g0, g1, k: ij(g0, g1))

    n_in = len(exchange.arrays) if exchange else 0

    def body(a_ref, b_ref, *refs):
        x_in, o_ref, x_out, scratch = refs[:n_in], refs[n_in], refs[n_in + 1:2 * n_in + 1], refs[2 * n_in + 1:]
        if exchange:
            sems = scratch[len(scratch) - exchange.n_sems:]
            step = (pl.program_id(0) * grid[1] + pl.program_id(1)) * grid[2] + pl.program_id(2)

            @pl.when(step == 0)
            def _():
                exchange.start(x_in, x_out, sems)

        prod = lax.dot_general(a_ref[...], b_ref[...], dims, preferred_element_type=F32)
        if nk == 1:
            o_ref[...] = prod.astype(o_ref.dtype)
        else:
            acc_ref = scratch[0]
            k = pl.program_id(2)

            @pl.when(k == 0)
            def _():
                acc_ref[...] = prod

            @pl.when((k > 0) & (k < nk - 1))
            def _():
                acc_ref[...] += prod

            @pl.when(k == nk - 1)
            def _():
                o_ref[...] = (acc_ref[...] + prod).astype(o_ref.dtype)

        if exchange:
            @pl.when(step == grid[0] * grid[1] * grid[2] - 1)
            def _():
                exchange.finish(x_in, x_out, sems)

    vmem = 2 * (_nbytes(a_tile, a.dtype) + _nbytes((tk, tn), b.dtype) + _nbytes((tm, tn), out_dtype)) + 2 * _nbytes((tm, tn), F32)
    anyspec = pl.BlockSpec(memory_space=pl.ANY)
    outs = pl.pallas_call(
        body,
        grid=grid,
        in_specs=[a_spec, b_spec] + [anyspec] * n_in,
        out_specs=[o_spec] + [anyspec] * n_in,
        out_shape=[jax.ShapeDtypeStruct((M, N), out_dtype)] + (exchange.out_shapes if exchange else []),
        scratch_shapes=([] if nk == 1 else [pltpu.VMEM((tm, tn), F32)]) + (exchange.sem_shapes if exchange else []),
        compiler_params=_cparams(("arbitrary",) * 3 if exchange else ("parallel", "parallel", "arbitrary"), vmem + (8 << 20)),
        name=name,
    )(a, b, *(exchange.arrays if exchange else []))
    return (outs[0], outs[1:]) if exchange else outs[0]


def _in_dest(k):
    return SHARD_IN * k - SMALL


def _assemble_w_in(g):
    R = 128

    def body(g_ref, o_ref, ext, acc):
        ext[...] = jnp.zeros_like(ext)
        acc[...] = jnp.zeros_like(acc)
        lane = lax.broadcasted_iota(jnp.int32, (R, EXT), 1)
        ext[:, 0:SHARD_IN] = g_ref[0].astype(F32)
        v = ext[...]
        acc[:, OFF_QA:NP] = jnp.where(lane[:, 0:NP - OFF_QA] < SMALL, v[:, 0:NP - OFF_QA], 0.0)
        w = v[:, 1024:2048]
        w = pltpu.roll(w, 1024 - 64, 1)
        acc[:, 0:1024] = jnp.where(lane[:, 0:1024] < SHARD_IN - SMALL, w, 0.0)
        for k in range(1, NDEV):
            ext[:, 0:SHARD_IN] = g_ref[k].astype(F32)
            dest = _in_dest(k)
            t, o = dest // 128, dest % 128
            width = -(-(o + SHARD_IN) // 128) * 128
            v = pltpu.roll(ext[...], o, 1)[:, 0:width]
            acc[:, 128 * t:128 * t + width] += v
        o_ref[...] = acc[...].astype(BF16)

    return pl.pallas_call(
        body,
        grid=(D // R,),
        in_specs=[pl.BlockSpec((NDEV, R, SHARD_IN), lambda i: (0, i, 0))],
        out_specs=pl.BlockSpec((R, NP), lambda i: (i, 0)),
        out_shape=jax.ShapeDtypeStruct((D, NP), BF16),
        scratch_shapes=[pltpu.VMEM((R, EXT), F32), pltpu.VMEM((R, NP), F32)],
        compiler_params=_cparams(("parallel",), 40 << 20),
        name="assemble_w_in",
    )(g)


def _split_dw_in(dw, slots):
    R = 128

    def body(slots_ref, dw_ref, o_ref, ob_ref):
        def put(k, cols, v):
            ob_ref[k, :, cols] = v.astype(BF16)

            @pl.when(slots_ref[0] == k)
            def _():
                o_ref[:, cols] = v

        lane = lax.broadcasted_iota(jnp.int32, (R, 1024), 1)
        put(0, slice(0, 1024), dw_ref[:, OFF_QA:OFF_QA + 1024])
        tail = dw_ref[:, OFF_QA + 1024:NP]
        tail = jnp.concatenate([tail, jnp.zeros((R, 1024 - 128), F32)], axis=1)
        head = dw_ref[:, 0:1024]
        mixed = jnp.where(lane < 64, tail, pltpu.roll(head, 64, 1))
        put(0, slice(1024, SHARD_IN), mixed[:, 0:SHARD_IN - 1024])
        for k in range(1, NDEV):
            dest = _in_dest(k)
            t, o = dest // 128, dest % 128
            width = -(-(o + SHARD_IN) // 128) * 128
            v = dw_ref[:, 128 * t:128 * t + width]
            v = pltpu.roll(v, width - o, 1)
            put(k, slice(0, SHARD_IN), v[:, 0:SHARD_IN])

    return pl.pallas_call(
        body,
        grid_spec=pltpu.PrefetchScalarGridSpec(
            num_scalar_prefetch=1, grid=(D // R,),
            in_specs=[pl.BlockSpec((R, NP), lambda i, s: (i, 0))],
            out_specs=[pl.BlockSpec((R, SHARD_IN), lambda i, s: (i, 0)), pl.BlockSpec((NDEV, R, SHARD_IN), lambda i, s: (0, i, 0))]),
        out_shape=[jax.ShapeDtypeStruct((D, SHARD_IN), F32), jax.ShapeDtypeStruct((NDEV, D, SHARD_IN), BF16)],
        compiler_params=_cparams(("parallel",), 48 << 20),
        name="split_dw_in",
    )(slots, dw)


def _assemble_w_q(g):
    def body(g_ref, o_ref):
        lane = lax.broadcasted_iota(jnp.int32, (RQ, 128), 1)
        lo = lane < 64
        for k in range(NDEV):
            t0 = g_ref[k, :, 0:128].astype(F32)
            t1 = g_ref[k, :, 128:256].astype(F32)
            t2 = g_ref[k, :, 256:384].astype(F32)
            base = 2 * k * HW
            o_ref[:, base:base + 128] = t0.astype(BF16)
            o_ref[:, base + 128:base + 256] = jnp.where(lo, t1, 0.0).astype(BF16)
            o_ref[:, base + 256:base + 384] = pltpu.roll(jnp.where(lo, t2, t1), 64, 1).astype(BF16)
            o_ref[:, base + 384:base + 512] = jnp.where(lo, pltpu.roll(t2, 64, 1), 0.0).astype(BF16)

    return pl.pallas_call(
        body,
        out_shape=jax.ShapeDtypeStruct((RQ, H * HW), BF16),
        compiler_params=_cparams(None, 32 << 20),
        name="assemble_w_q",
    )(g)


def _split_dw_q(dw):
    def body(dw_ref, o_ref):
        lane = lax.broadcasted_iota(jnp.int32, (RQ, 128), 1)
        lo = lane < 64
        for k in range(NDEV):
            base = 2 * k * HW
            a = dw_ref[:, base:base + 128]
            b = dw_ref[:, base + 128:base + 256]
            c = pltpu.roll(dw_ref[:, base + 256:base + 384], 64, 1)
            d = pltpu.roll(dw_ref[:, base + 384:base + 512], 64, 1)
            o_ref[k, :, 0:128] = a
            o_ref[k, :, 128:256] = jnp.where(lo, b, c)
            o_ref[k, :, 256:384] = jnp.where(lo, c, d)

    return pl.pallas_call(
        body,
        out_shape=jax.ShapeDtypeStruct((NDEV, RQ, 384), F32),
        compiler_params=_cparams(None, 32 << 20),
        name="split_dw_q",
    )(dw)


def _concat_cols(g, dtype, name):
    n, R, C = g.shape

    def body(g_ref, o_ref):
        for k in range(n):
            o_ref[:, k * C:(k + 1) * C] = g_ref[k].astype(dtype)

    return pl.pallas_call(body, out_shape=jax.ShapeDtypeStruct((R, n * C), dtype), compiler_params=_cparams(None, 32 << 20), name=name)(g)


def _split_cols(x, n, name):
    R, NC = x.shape
    C = NC // n

    def body(x_ref, o_ref):
        for k in range(n):
            o_ref[k] = x_ref[:, k * C:(k + 1) * C]

    return pl.pallas_call(body, out_shape=jax.ShapeDtypeStruct((n, R, C), x.dtype), compiler_params=_cparams(None, 32 << 20), name=name)(x)


def _rms_scale(xf):
    return lax.rsqrt(jnp.mean(xf * xf, axis=-1, keepdims=True) + RMS_EPS)


def _prenorm(x, g, ts):
    S = x.shape[0]

    def body(x_ref, g_ref, h_ref, ht_ref):
        xf = x_ref[...]
        h = xf * _rms_scale(xf) * g_ref[...]
        h_ref[...] = h.astype(BF16)
        ht_ref[...] = jnp.transpose(h).astype(BF16)

    return pl.pallas_call(
        body,
        grid=(S // ts,),
        in_specs=[pl.BlockSpec((ts, D), lambda i: (i, 0)), pl.BlockSpec((1, D), lambda i: (0, 0))],
        out_specs=[pl.BlockSpec((ts, D), lambda i: (i, 0)), pl.BlockSpec((D, ts), lambda i: (0, i))],
        out_shape=[jax.ShapeDtypeStruct((S, D), BF16), jax.ShapeDtypeStruct((D, S), BF16)],
        compiler_params=_cparams(("parallel",), 32 << 20),
        name="prenorm",
    )(x, g)


def _inv_freq_tile():
    inv_freq = ROPE_THETA ** (-jnp.arange(0, DR, 2, dtype=F32) / DR)
    return jnp.tile(inv_freq, 4).reshape(1, 128)


def _rope_tables(pos_ref, freq_ref, ts):
    lane = lax.broadcasted_iota(jnp.int32, (ts, 128), 1)
    ang = pos_ref[...].astype(F32) * freq_ref[...]
    return jnp.cos(ang), jnp.sin(ang), lane


def _rope_swap(t, lane):
    return jnp.where(lane < 32, pltpu.roll(t, 96, 1), pltpu.roll(t, 32, 1))


def _qkv_prep(proj, pos, freq, g_qa, g_kva, w_q, w_kv, ts):
    S = proj.shape[0]

    def body(qa_ref, ckv_ref, kr_ref, pos_ref, freq_ref, gq_ref, gk_ref, wq_ref, wkv_ref, q_ref, k_ref, v_ref, vt_ref, qn_ref, kvn_ref):
        qa = _f32(qa_ref)
        qn = (qa * _rms_scale(qa) * gq_ref[...]).astype(BF16)
        ckv = _f32(ckv_ref)
        kvn = (ckv * _rms_scale(ckv) * gk_ref[...]).astype(BF16)
        qn_ref[...] = qn
        kvn_ref[...] = kvn
        cos, sin, lane = _rope_tables(pos_ref, freq_ref, ts)
        sgn_sin = jnp.where(lane < 32, -sin, sin)
        live = lane < DR
        kr = _f32(kr_ref)
        kr = jnp.where(live, kr * cos + _rope_swap(kr, lane) * sgn_sin, 0.0).astype(BF16)
        qf = jnp.dot(qn, wq_ref[...], preferred_element_type=F32)
        kvf = jnp.dot(kvn, wkv_ref[...], preferred_element_type=F32)
        for h in range(H):
            q_ref[h, :, 0:DN] = (qf[:, h * HW:h * HW + DN] * QSCALE).astype(BF16)
            t = qf[:, h * HW + DN:(h + 1) * HW]
            q_ref[h, :, DN:HW] = jnp.where(live, (t * cos + _rope_swap(t, lane) * sgn_sin) * QSCALE, 0.0).astype(BF16)
            k_ref[h, :, 0:DN] = kvf[:, h * HW:h * HW + DN].astype(BF16)
            k_ref[h, :, DN:HW] = kr
            vh = kvf[:, h * HW + DN:(h + 1) * HW]
            v_ref[h] = vh.astype(BF16)
            vt_ref[h, 0:DV, :] = jnp.transpose(vh).astype(BF16)
            vt_ref[h, DV:DVA, :] = jnp.ones((DVA - DV, ts), BF16)

    row = lambda w, blk: pl.BlockSpec((ts, w), lambda i: (i, blk))
    full = lambda a: pl.BlockSpec(a.shape, lambda i: (0,) * a.ndim)
    return pl.pallas_call(
        body,
        grid=(S // ts,),
        in_specs=[row(RQ, OFF_QA // RQ), row(RKV, OFF_CKV // RKV), row(128, OFF_KR // 128),
                  pl.BlockSpec((ts, 1), lambda i: (i, 0)), full(freq), full(g_qa), full(g_kva), full(w_q), full(w_kv)],
        out_specs=[pl.BlockSpec((H, ts, HW), lambda i: (0, i, 0)), pl.BlockSpec((H, ts, HW), lambda i: (0, i, 0)),
                   pl.BlockSpec((H, ts, DV), lambda i: (0, i, 0)), pl.BlockSpec((H, DVA, ts), lambda i: (0, 0, i)),
                   pl.BlockSpec((ts, RQ), lambda i: (i, 0)), pl.BlockSpec((ts, RKV), lambda i: (i, 0))],
        out_shape=[jax.ShapeDtypeStruct((H, S, HW), BF16), jax.ShapeDtypeStruct((H, S, HW), BF16),
                   jax.ShapeDtypeStruct((H, S, DV), BF16), jax.ShapeDtypeStruct((H, DVA, S), BF16),
                   jax.ShapeDtypeStruct((S, RQ), BF16), jax.ShapeDtypeStruct((S, RKV), BF16)],
        compiler_params=_cparams(("parallel",), 48 << 20),
        name="qkv_prep",
    )(proj, proj, proj, pos, freq, g_qa, g_kva, w_q, w_kv)


def _col_to_row8(col, n):
    return jnp.transpose(jnp.broadcast_to(col, (n, 128)))[0:8, :]


def _causal_pairs(n, key_major):
    if key_major:
        pairs = [(i, j) for j in range(n) for i in range(j, n)]
    else:
        pairs = [(i, j) for i in range(n) for j in range(i + 1)]
    return jnp.array([p[0] for p in pairs], jnp.int32), jnp.array([p[1] for p in pairs], jnp.int32)


def _flash_fwd(q, k, vt, tq, hb=8):
    S = q.shape[1]
    nq = S // tq
    tsub = tq
    nsub = tq // tsub

    qi_tab, kj_tab = _causal_pairs(nq, key_major=False)

    def body(qi_ref, kj_ref, q_ref, k_ref, vt_ref, o_ref, lse_ref, m_sc, acc_sc):
        p = pl.program_id(1)
        qi, kj = qi_ref[p], kj_ref[p]

        @pl.when(kj == 0)
        def _():
            m_sc[...] = jnp.full_like(m_sc, -jnp.inf)
            acc_sc[...] = jnp.zeros_like(acc_sc)

        def step(diag):
            chains = [(h, u) for h in range(hb) for u in range(nsub)]

            def scores(h, u):
                return lax.dot_general(k_ref[h], q_ref[h, u * tsub:(u + 1) * tsub, :], NT, preferred_element_type=F32)

            st_next = scores(*chains[0])
            for ci, (h, u) in enumerate(chains):
                st = st_next
                if ci + 1 < len(chains):
                    st_next = scores(*chains[ci + 1])
                cols = slice(u * tsub, (u + 1) * tsub)
                if diag:
                    r = lax.broadcasted_iota(jnp.int32, (tq, tsub), 0)
                    c = lax.broadcasted_iota(jnp.int32, (tq, tsub), 1) + u * tsub
                    st = jnp.where(r <= c, st, -jnp.inf)
                m_prev = m_sc[h, 0:1, cols]
                m_new = jnp.maximum(m_prev, jnp.max(st, axis=0, keepdims=True))
                alpha = jnp.exp2(m_prev - m_new)
                pt = jnp.exp2(st - m_new)
                m_sc[h, :, cols] = jnp.broadcast_to(m_new, (8, tsub))
                acc_sc[h, :, cols] = alpha * acc_sc[h, :, cols] + jnp.dot(vt_ref[h], pt.astype(BF16), preferred_element_type=F32)

        @pl.when(kj < qi)
        def _():
            step(False)

        @pl.when(kj == qi)
        def _():
            step(True)
            for h in range(hb):
                l = acc_sc[h, DV:DV + 8, :]
                o_ref[:, h * DV:(h + 1) * DV] = jnp.transpose(acc_sc[h, 0:DV, :] / l[0:1, :]).astype(BF16)
                lse_ref[h] = m_sc[h] + jnp.log2(l)

    return pl.pallas_call(
        body,
        grid_spec=pltpu.PrefetchScalarGridSpec(
            num_scalar_prefetch=2,
            grid=(H // hb, len(qi_tab)),
            in_specs=[pl.BlockSpec((hb, tq, HW), lambda h, p, qi, kj: (h, qi[p], 0)),
                      pl.BlockSpec((hb, tq, HW), lambda h, p, qi, kj: (h, kj[p], 0)),
                      pl.BlockSpec((hb, DVA, tq), lambda h, p, qi, kj: (h, 0, kj[p]))],
            out_specs=[pl.BlockSpec((tq, hb * DV), lambda h, p, qi, kj: (qi[p], h)),
                       pl.BlockSpec((hb, 8, tq), lambda h, p, qi, kj: (h, 0, qi[p]))],
            scratch_shapes=[pltpu.VMEM((hb, 8, tq), F32), pltpu.VMEM((hb, DVA, tq), F32)],
        ),
        out_shape=[jax.ShapeDtypeStruct((S, H * DV), BF16), jax.ShapeDtypeStruct((H, 8, S), F32)],
        compiler_params=_cparams(("parallel", "arbitrary"), 40 << 20),
        name="flash_fwd",
    )(qi_tab, kj_tab, q, k, vt)


def _sigmoid(x):
    return 1.0 / (1.0 + jnp.exp(-x))


HALO = 16


def _f32(ref):
    return ref[...].astype(F32)


def _shift_rows(u, prev, n, first):
    ts = u.shape[0]
    row = lax.broadcasted_iota(jnp.int32, u.shape, 0)
    out = pltpu.roll(u, n, 0)
    for j in range(n):
        halo = jnp.where(first, 0.0, prev[HALO - n + j:HALO - n + j + 1, :])
        out = jnp.where(row == j, halo, out)
    return out


def _gates_fwd(proj, attn, conv_w, ts):
    S = proj.shape[0]

    def body(attn_ref, zm_ref, cin_ref, bg_ref, cg_ref, zc_ref, cin_p, cg_p, w_ref, am_ref, ac_ref):
        first = pl.program_id(0) == 0
        zm = _f32(zm_ref)
        am_ref[...] = (_f32(attn_ref) * (zm * _sigmoid(zm))).astype(BF16)
        u = _f32(cg_ref) * _f32(cin_ref)
        up = _f32(cg_p) * _f32(cin_p)
        w = w_ref[...]
        conv = w[0:1, :] * _shift_rows(u, up, 2, first) + w[1:2, :] * _shift_rows(u, up, 1, first) + w[2:3, :] * u
        zc = _f32(zc_ref)
        ac_ref[...] = (_f32(bg_ref) * conv * (zc * _sigmoid(zc))).astype(BF16)

    seg = lambda name: pl.BlockSpec((ts, D), lambda i: (i, SEG[name]))
    prev = lambda name: pl.BlockSpec((HALO, D), lambda i: (jnp.maximum(i * (ts // HALO) - 1, 0), SEG[name]))
    return pl.pallas_call(
        body,
        grid=(S // ts,),
        in_specs=[pl.BlockSpec((ts, D), lambda i: (i, 0)), seg("z_mla"), seg("c_in"), seg("b_gate"), seg("c_gate"), seg("z_conv"),
                  prev("c_in"), prev("c_gate"), pl.BlockSpec((8, D), lambda i: (0, 0))],
        out_specs=[pl.BlockSpec((ts, D), lambda i: (i, 0))] * 2,
        out_shape=[jax.ShapeDtypeStruct((S, D), BF16)] * 2,
        compiler_params=_cparams(("arbitrary",), 48 << 20),
        name="gates_fwd",
    )(attn, proj, proj, proj, proj, proj, proj, proj, conv_w)


def _merge_fwd(proj, y_mla, y_conv, ts):
    S = proj.shape[0]

    def body(gm_ref, gc_ref, ym_ref, yc_ref, o_ref):
        o_ref[...] = (_sigmoid(_f32(gm_ref)) * _f32(ym_ref) + _sigmoid(_f32(gc_ref)) * _f32(yc_ref)).astype(BF16)

    seg = lambda name: pl.BlockSpec((ts, D), lambda i: (i, SEG[name]))
    row = pl.BlockSpec((ts, D), lambda i: (i, 0))
    return pl.pallas_call(
        body, grid=(S // ts,), in_specs=[seg("g_mla"), seg("g_conv"), row, row], out_specs=row,
        out_shape=jax.ShapeDtypeStruct((S, D), BF16), compiler_params=_cparams(("parallel",), 32 << 20), name="merge_fwd",
    )(proj, proj, y_mla, y_conv)


def _post_loss(out, x, target, g_post, ts):
    S = out.shape[0]

    def body(o_ref, x_ref, t_ref, g_ref, dy_ref, do_ref, dg_ref, loss_ref):
        i = pl.program_id(0)
        o = o_ref[...]
        r = _rms_scale(o)
        n = o * r
        g = g_ref[...]
        err = x_ref[...] + n * g - t_ref[...]
        dy = err * (1.0 / D)
        dy_ref[...] = dy
        dn = dy * g
        do_ref[...] = (r * (dn - n * jnp.mean(dn * n, axis=-1, keepdims=True))).astype(BF16)
        dg = jnp.sum(dy * n, axis=0, keepdims=True)
        part = jnp.sum(jnp.sum(err * err, axis=0, keepdims=True), axis=1, keepdims=True) * (0.5 / D)

        @pl.when(i == 0)
        def _():
            dg_ref[...] = jnp.zeros_like(dg_ref)
            loss_ref[...] = jnp.zeros_like(loss_ref)

        dg_ref[0:1, :] += dg
        loss_ref[...] += jnp.broadcast_to(part, loss_ref.shape)

    row = pl.BlockSpec((ts, D), lambda i: (i, 0))
    return pl.pallas_call(
        body, grid=(S // ts,),
        in_specs=[row, row, row, pl.BlockSpec((1, D), lambda i: (0, 0))],
        out_specs=[row, row, pl.BlockSpec((8, D), lambda i: (0, 0)), pl.BlockSpec((8, 128), lambda i: (0, 0))],
        out_shape=[jax.ShapeDtypeStruct((S, D), F32), jax.ShapeDtypeStruct((S, D), BF16),
                   jax.ShapeDtypeStruct((8, D), F32), jax.ShapeDtypeStruct((8, 128), F32)],
        compiler_params=_cparams(("arbitrary",), 40 << 20), name="post_loss",
    )(out, x, target, g_post)


def _merge_bwd(proj, y_mla, y_conv, dmerged, ts):
    S = proj.shape[0]

    def body(gm_ref, gc_ref, ym_ref, yc_ref, dm_ref, dym_ref, dyc_ref, dgm_ref, dgc_ref):
        dm = _f32(dm_ref)
        sm = _sigmoid(_f32(gm_ref))
        sc = _sigmoid(_f32(gc_ref))
        dym_ref[...] = (dm * sm).astype(BF16)
        dyc_ref[...] = (dm * sc).astype(BF16)
        dgm_ref[...] = (dm * _f32(ym_ref) * (sm * (1.0 - sm))).astype(BF16)
        dgc_ref[...] = (dm * _f32(yc_ref) * (sc * (1.0 - sc))).astype(BF16)

    seg = lambda name: pl.BlockSpec((ts, D), lambda i: (i, SEG[name]))
    row = pl.BlockSpec((ts, D), lambda i: (i, 0))
    return pl.pallas_call(
        body, grid=(S // ts,), in_specs=[seg("g_mla"), seg("g_conv"), row, row, row], out_specs=[row] * 4,
        out_shape=[jax.ShapeDtypeStruct((S, D), BF16)] * 4, compiler_params=_cparams(("parallel",), 40 << 20), name="merge_bwd",
    )(proj, proj, y_mla, y_conv, dmerged)


def _gates_bwd(proj, attn, da_mla, da_conv, conv_w, ts):
    S = proj.shape[0]
    nblk = S // ts

    def body(attn_ref, zm_ref, cin_ref, bg_ref, cg_ref, zc_ref, dam_ref, dac_ref, cin_p, cg_p, bg_n, zc_n, dac_n, w_ref,
             dattn_ref, delta_ref, dzm_ref, dcin_ref, dbg_ref, dcg_ref, dzc_ref, dw_ref):
        i = pl.program_id(0)
        first = i == 0
        last = i == nblk - 1

        @pl.when(first)
        def _():
            dw_ref[...] = jnp.zeros_like(dw_ref)

        row = lax.broadcasted_iota(jnp.int32, (ts, DV), 0)
        for c in range(D // DV):
            cols = slice(c * DV, (c + 1) * DV)
            ld = lambda ref: ref[:, cols].astype(F32)
            zm = ld(zm_ref)
            sg = _sigmoid(zm)
            attn = ld(attn_ref)
            dam = ld(dam_ref)
            dattn = dam * (zm * sg)
            dattn_ref[:, cols] = dattn.astype(BF16)
            dzm_ref[:, cols] = (dam * attn * (sg * (1.0 + zm * (1.0 - sg)))).astype(BF16)
            delta_ref[c] = _col_to_row8(jnp.sum(dattn * attn, axis=1, keepdims=True), ts)
            w = w_ref[:, cols]
            cin, cg, bg, zc = ld(cin_ref), ld(cg_ref), ld(bg_ref), ld(zc_ref)
            u = cg * cin
            up = ld(cg_p) * ld(cin_p)
            u1 = _shift_rows(u, up, 1, first)
            u2 = _shift_rows(u, up, 2, first)
            conv = w[0:1, :] * u2 + w[1:2, :] * u1 + w[2:3, :] * u
            sgc = _sigmoid(zc)
            siluc = zc * sgc
            dac = ld(dac_ref)
            dbg_ref[:, cols] = (dac * conv * siluc).astype(BF16)
            dzc_ref[:, cols] = (dac * bg * conv * (sgc * (1.0 + zc * (1.0 - sgc)))).astype(BF16)
            dconv = dac * bg * siluc
            zn = ld(zc_n)
            dconv_n = jnp.where(last, 0.0, ld(dac_n) * ld(bg_n) * (zn * _sigmoid(zn)))
            d1 = jnp.where(row == ts - 1, dconv_n[0:1, :], pltpu.roll(dconv, ts - 1, 0))
            d2 = jnp.where(row == ts - 1, dconv_n[1:2, :], jnp.where(row == ts - 2, dconv_n[0:1, :], pltpu.roll(dconv, ts - 2, 0)))
            du = w[2:3, :] * dconv + w[1:2, :] * d1 + w[0:1, :] * d2
            dcg_ref[:, cols] = (du * cin).astype(BF16)
            dcin_ref[:, cols] = (du * cg).astype(BF16)
            dw_ref[0:1, cols] += jnp.sum(dconv * u2, axis=0, keepdims=True)
            dw_ref[1:2, cols] += jnp.sum(dconv * u1, axis=0, keepdims=True)
            dw_ref[2:3, cols] += jnp.sum(dconv * u, axis=0, keepdims=True)

    seg = lambda name: pl.BlockSpec((ts, D), lambda i: (i, SEG[name]))
    prev = lambda name: pl.BlockSpec((HALO, D), lambda i: (jnp.maximum(i * (ts // HALO) - 1, 0), SEG[name]))
    nxt = lambda blk: pl.BlockSpec((HALO, D), lambda i: (jnp.minimum((i + 1) * (ts // HALO), S // HALO - 1), blk))
    row = pl.BlockSpec((ts, D), lambda i: (i, 0))
    return pl.pallas_call(
        body, grid=(nblk,),
        in_specs=[row, seg("z_mla"), seg("c_in"), seg("b_gate"), seg("c_gate"), seg("z_conv"), row, row,
                  prev("c_in"), prev("c_gate"), nxt(SEG["b_gate"]), nxt(SEG["z_conv"]), nxt(0), pl.BlockSpec((8, D), lambda i: (0, 0))],
        out_specs=[row, pl.BlockSpec((H, 8, ts), lambda i: (0, 0, i)), row, row, row, row, row, pl.BlockSpec((8, D), lambda i: (0, 0))],
        out_shape=[jax.ShapeDtypeStruct((S, D), BF16), jax.ShapeDtypeStruct((H, 8, S), F32)] + [jax.ShapeDtypeStruct((S, D), BF16)] * 5
        + [jax.ShapeDtypeStruct((8, D), F32)],
        compiler_params=_cparams(("arbitrary",), 56 << 20), name="gates_bwd",
    )(attn, proj, proj, proj, proj, proj, da_mla, da_conv, proj, proj, proj, proj, da_conv, conv_w)


def _flash_bwd(q, k, v, do, lse, delta, tq, hb=2):
    S = q.shape[1]
    nq = S // tq
    scale = SM_SCALE
    tsub = min(256, tq)
    nsub = tq // tsub

    qi_tab, kj_tab = _causal_pairs(nq, key_major=True)
    npairs = nq * (nq + 1) // 2

    def body(qi_ref, kj_ref, q_ref, k_ref, v_ref, do_ref, lse_ref, dl_ref, dq_ref, dk_ref, dv_ref, dk_sc, dv_sc, dq_sc):
        p = pl.program_id(1)
        qi, kj = qi_ref[p], kj_ref[p]

        @pl.when(p == 0)
        def _():
            dq_sc[...] = jnp.zeros_like(dq_sc)

        @pl.when(qi == kj)
        def _():
            dk_sc[...] = jnp.zeros_like(dk_sc)
            dv_sc[...] = jnp.zeros_like(dv_sc)

        def step(diag):
            chains = [(h, u) for h in range(hb) for u in range(nsub)]

            def first_matmuls(h, u):
                sub = slice(u * tsub, (u + 1) * tsub)
                st = lax.dot_general(k_ref[h], q_ref[h, sub, :], NT, preferred_element_type=F32)
                dpt = lax.dot_general(v_ref[h], do_ref[sub, h * DV:(h + 1) * DV], NT, preferred_element_type=F32)
                return st, dpt

            nxt = first_matmuls(*chains[0])
            for ci, (h, u) in enumerate(chains):
                st, dpt = nxt
                if ci + 1 < len(chains):
                    nxt = first_matmuls(*chains[ci + 1])
                sub = slice(u * tsub, (u + 1) * tsub)
                pt = jnp.exp2(st - lse_ref[h, 0:1, sub])
                if diag:
                    r = lax.broadcasted_iota(jnp.int32, (tq, tsub), 0)
                    c = lax.broadcasted_iota(jnp.int32, (tq, tsub), 1) + u * tsub
                    pt = jnp.where(r <= c, pt, 0.0)
                dst = (pt * (dpt - dl_ref[h, 0:1, sub])).astype(BF16)
                dv_sc[h] += jnp.dot(pt.astype(BF16), do_ref[sub, h * DV:(h + 1) * DV], preferred_element_type=F32)
                dk_sc[h] += jnp.dot(dst, q_ref[h, sub, :], preferred_element_type=F32)
                rows = pl.ds(pl.multiple_of(qi * tq + u * tsub, tsub), tsub)
                dq_sc[h, rows, :] += lax.dot_general(dst, k_ref[h], TN, preferred_element_type=F32)

        @pl.when(qi > kj)
        def _():
            step(False)

        @pl.when(qi == kj)
        def _():
            step(True)

        @pl.when(qi == nq - 1)
        def _():
            dk_ref[...] = (dk_sc[...] * (1.0 / LOG2E)).astype(BF16)
            dv_ref[...] = dv_sc[...].astype(BF16)

        @pl.when(p == npairs - 1)
        def _():
            dq_ref[...] = (dq_sc[...] * scale).astype(BF16)

    return pl.pallas_call(
        body,
        grid_spec=pltpu.PrefetchScalarGridSpec(
            num_scalar_prefetch=2,
            grid=(H // hb, npairs),
            in_specs=[pl.BlockSpec((hb, tq, HW), lambda h, p, qi, kj: (h, qi[p], 0)),
                      pl.BlockSpec((hb, tq, HW), lambda h, p, qi, kj: (h, kj[p], 0)),
                      pl.BlockSpec((hb, tq, DV), lambda h, p, qi, kj: (h, kj[p], 0)),
                      pl.BlockSpec((tq, hb * DV), lambda h, p, qi, kj: (qi[p], h)),
                      pl.BlockSpec((hb, 8, tq), lambda h, p, qi, kj: (h, 0, qi[p])),
                      pl.BlockSpec((hb, 8, tq), lambda h, p, qi, kj: (h, 0, qi[p]))],
            out_specs=[pl.BlockSpec((hb, S, HW), lambda h, p, qi, kj: (h, 0, 0)),
                       pl.BlockSpec((hb, tq, HW), lambda h, p, qi, kj: (h, kj[p], 0)),
                       pl.BlockSpec((hb, tq, DV), lambda h, p, qi, kj: (h, kj[p], 0))],
            scratch_shapes=[pltpu.VMEM((hb, tq, HW), F32), pltpu.VMEM((hb, tq, DV), F32), pltpu.VMEM((hb, S, HW), F32)],
        ),
        out_shape=[jax.ShapeDtypeStruct((H, S, HW), BF16), jax.ShapeDtypeStruct((H, S, HW), BF16), jax.ShapeDtypeStruct((H, S, DV), BF16)],
        compiler_params=_cparams(("parallel", "arbitrary"), VMEM_CAP),
        name="flash_bwd",
    )(qi_tab, kj_tab, q, k, v, do, lse, delta)


def _rms_bwd(xf, g, dn_out):
    r = _rms_scale(xf)
    n = xf * r
    dn = dn_out * g
    dx = r * (dn - n * jnp.mean(dn * n, axis=-1, keepdims=True))
    return dx, jnp.sum(dn_out * n, axis=0, keepdims=True)


def _qkv_bwd(proj, pos, freq, g_qa, g_kva, w_q, w_kv, dq, dk, dv, segments, ts):
    S = proj.shape[0]
    nseg = len(segments)

    def body(qa_ref, ckv_ref, pos_ref, freq_ref, gq_ref, gk_ref, wq_ref, wkv_ref, dq_ref, dk_ref, dv_ref, *rest):
        seg_refs, (dqp_ref, dkvp_ref, dproj_ref, dgq_ref, dgk_ref) = rest[:nseg], rest[nseg:]
        for j, seg_ref in enumerate(seg_refs):
            dproj_ref[:, j * D:(j + 1) * D] = seg_ref[...]
        dqa_ref = dproj_ref.at[:, OFF_QA:OFF_QA + RQ]
        dckv_ref = dproj_ref.at[:, OFF_CKV:OFF_CKV + RKV]
        dkr_ref = dproj_ref.at[:, OFF_KR:OFF_KR + 128]
        i = pl.program_id(0)
        cos, sin, lane = _rope_tables(pos_ref, freq_ref, ts)
        sgn_sin = jnp.where(lane < 32, sin, -sin)
        live = lane < DR
        kr_sum = jnp.zeros((ts, 128), F32)
        for h in range(H):
            dqp_ref[:, h * HW:h * HW + DN] = dq_ref[h, :, 0:DN]
            t = dq_ref[h, :, DN:HW].astype(F32)
            dqp_ref[:, h * HW + DN:(h + 1) * HW] = jnp.where(live, t * cos + _rope_swap(t, lane) * sgn_sin, 0.0).astype(BF16)
            dkvp_ref[:, h * HW:h * HW + DN] = dk_ref[h, :, 0:DN]
            dkvp_ref[:, h * HW + DN:(h + 1) * HW] = dv_ref[h]
            kr_sum = kr_sum + dk_ref[h, :, DN:HW].astype(F32)
        dkr_ref[...] = jnp.where(live, kr_sum * cos + _rope_swap(kr_sum, lane) * sgn_sin, 0.0).astype(BF16)
        dqn = lax.dot_general(dqp_ref[...], wq_ref[...], NT, preferred_element_type=F32)
        dkvn = lax.dot_general(dkvp_ref[...], wkv_ref[...], NT, preferred_element_type=F32)
        dqa, dgq = _rms_bwd(_f32(qa_ref), gq_ref[...], dqn)
        dckv, dgk = _rms_bwd(_f32(ckv_ref), gk_ref[...], dkvn)
        dqa_ref[...] = dqa.astype(BF16)
        dckv_ref[...] = dckv.astype(BF16)

        @pl.when(i == 0)
        def _():
            dgq_ref[...] = jnp.zeros_like(dgq_ref)
            dgk_ref[...] = jnp.zeros_like(dgk_ref)

        dgq_ref[0:1, :] += dgq
        dgk_ref[0:1, :] += dgk

    rowb = lambda w, blk: pl.BlockSpec((ts, w), lambda i: (i, blk))
    full = lambda a: pl.BlockSpec(a.shape, lambda i: (0,) * a.ndim)
    heads = lambda w: pl.BlockSpec((H, ts, w), lambda i: (0, i, 0))
    return pl.pallas_call(
        body, grid=(S // ts,),
        in_specs=[rowb(RQ, OFF_QA // RQ), rowb(RKV, OFF_CKV // RKV), pl.BlockSpec((ts, 1), lambda i: (i, 0)), full(freq), full(g_qa), full(g_kva),
                  full(w_q), full(w_kv), heads(HW), heads(HW), heads(DV)] + [rowb(D, 0)] * nseg,
        out_specs=[rowb(H * HW, 0), rowb(H * HW, 0), rowb(NP, 0),
                   pl.BlockSpec((8, RQ), lambda i: (0, 0)), pl.BlockSpec((8, RKV), lambda i: (0, 0))],
        out_shape=[jax.ShapeDtypeStruct((S, H * HW), BF16), jax.ShapeDtypeStruct((S, H * HW), BF16), jax.ShapeDtypeStruct((S, NP), BF16),
                   jax.ShapeDtypeStruct((8, RQ), F32), jax.ShapeDtypeStruct((8, RKV), F32)],
        compiler_params=_cparams(("arbitrary",), 56 << 20), name="qkv_bwd",
    )(proj, proj, pos, freq, g_qa, g_kva, w_q, w_kv, dq, dk, dv, *segments)


def _prenorm_bwd(x, g, dh, dy, ts):
    S = x.shape[0]

    def body(x_ref, g_ref, dh_ref, dy_ref, gx_ref, dg_ref):
        dx, dg = _rms_bwd(x_ref[...], g_ref[...], dh_ref[...])
        gx_ref[...] = dy_ref[...] + dx

        @pl.when(pl.program_id(0) == 0)
        def _():
            dg_ref[...] = jnp.zeros_like(dg_ref)

        dg_ref[0:1, :] += dg

    row = pl.BlockSpec((ts, D), lambda i: (i, 0))
    return pl.pallas_call(
        body, grid=(S // ts,), in_specs=[row, pl.BlockSpec((1, D), lambda i: (0, 0)), row, row],
        out_specs=[row, pl.BlockSpec((8, D), lambda i: (0, 0))],
        out_shape=[jax.ShapeDtypeStruct((S, D), F32), jax.ShapeDtypeStruct((8, D), F32)],
        compiler_params=_cparams(("arbitrary",), 40 << 20), name="prenorm_bwd",
    )(x, g, dh, dy)


def _local_step(x, pos, target, g_pre, g_qa, g_kva, g_post, w_in, other_weights, reduce_grads=None):
    S = x.shape[0]
    ts = min(256, S)
    tq = min(512, S)
    tm = min(1024, S)
    mm = functools.partial(_matmul, tm=tm)
    freq = _inv_freq_tile()

    h, ht = _prenorm(x, g_pre, ts)
    if len(other_weights) == 2:
        gather, assemble = other_weights
        proj, gathered = _matmul(h, w_in, mode="nn", out_dtype=BF16, tm=1024, tn=1408, tk=D, name="mm_proj", exchange=gather)
        other_weights = assemble(*gathered)
    else:
        proj = _matmul(h, w_in, mode="nn", out_dtype=BF16, tm=1024, tn=1408, tk=D, name="mm_proj")
    w_q, w_kv, conv_w8, w_o_mla, w_o_conv, w_out = other_weights
    q, k, v, vt, qn, kvn = _qkv_prep(proj, pos, freq, g_qa, g_kva, w_q, w_kv, ts)
    attn, lse = _flash_fwd(q, k, vt, tq)
    a_mla, a_conv = _gates_fwd(proj, attn, conv_w8, ts)
    y_mla = mm(a_mla, w_o_mla, mode="nn", out_dtype=BF16, tn=D, tk=D, name="mm_y_mla")
    y_conv = mm(a_conv, w_o_conv, mode="nn", out_dtype=BF16, tn=D, tk=D, name="mm_y_conv")
    merged = _merge_fwd(proj, y_mla, y_conv, ts)
    out = mm(merged, w_out, mode="nn", out_dtype=F32, tn=D, tk=D, name="mm_out")
    dy, dout, dg_post, loss = _post_loss(out, x, target, g_post, ts)

    dmerged = mm(dout, w_out, mode="nt", out_dtype=BF16, tn=D, tk=D, name="mm_dmerged")
    dw_out = _matmul(merged, dout, mode="tn", out_dtype=F32, tm=1024, tn=1024, tk=2048, name="mm_dw_out")
    dy_mla, dy_conv, dg_mla, dg_conv = _merge_bwd(proj, y_mla, y_conv, dmerged, ts)
    da_mla = mm(dy_mla, w_o_mla, mode="nt", out_dtype=BF16, tn=D, tk=D, name="mm_da_mla")
    da_conv = mm(dy_conv, w_o_conv, mode="nt", out_dtype=BF16, tn=D, tk=D, name="mm_da_conv")
    dw_o_mla = _matmul(a_mla, dy_mla, mode="tn", out_dtype=F32, tm=1024, tn=1024, tk=2048, name="mm_dw_o_mla")
    dw_o_conv = _matmul(a_conv, dy_conv, mode="tn", out_dtype=F32, tm=1024, tn=1024, tk=2048, name="mm_dw_o_conv")
    dattn, delta, dz_mla, dc_in, db_gate, dc_gate, dz_conv, dconv_w = _gates_bwd(proj, attn, da_mla, da_conv, conv_w8, min(128, S))
    dq, dk, dv = _flash_bwd(q, k, v, dattn, lse, delta, tq)
    dqp, dkvp, dproj, dg_qa, dg_kva = _qkv_bwd(proj, pos, freq, g_qa, g_kva, w_q, w_kv, dq, dk, dv,
                                               [dz_mla, dc_in, db_gate, dc_gate, dz_conv, dg_mla, dg_conv], min(128, S))
    dw_q = _matmul(qn, dqp, mode="tn", out_dtype=F32, tm=RQ, tn=1024, tk=2048, name="mm_dw_q")
    dw_kv = _matmul(kvn, dkvp, mode="tn", out_dtype=F32, tm=RKV, tn=1024, tk=2048, name="mm_dw_kv")
    dw_in = _matmul(ht, dproj, mode="nn", out_dtype=F32, tm=1024, tn=1408, tk=2048, name="mm_dw_in")
    res = dict(dw_in=dw_in, dw_q=dw_q, dw_kv=dw_kv, dconv_w=dconv_w, dw_o_mla=dw_o_mla, dw_o_conv=dw_o_conv, dw_out=dw_out)
    if reduce_grads is None:
        dh = _matmul(dproj, w_in, mode="nt", out_dtype=F32, tm=1024, tn=D, tk=1408, name="mm_dh")
    else:
        exchange, finish = reduce_grads(res)
        dh, got = _matmul(dproj, w_in, mode="nt", out_dtype=F32, tm=1024, tn=D, tk=1408, name="mm_dh", exchange=exchange)
        res["reduced"] = finish(got)
    grad_x, dg_pre = _prenorm_bwd(x, g_pre, dh, dy, ts)
    res.update(loss=loss, grad_x=grad_x, dg_pre=dg_pre, dg_qa=dg_qa, dg_kva=dg_kva, dg_post=dg_post)
    return res


def _my_id():
    return lax.axis_index("x") * 4 + lax.axis_index("y") * 2 + lax.axis_index("c")


def _place():
    x, y, c = lax.axis_index("x"), lax.axis_index("y"), lax.axis_index("c")
    return (x, y, c), (x, y, 1 - c), [(1 - x, y), (x, 1 - y), (1 - x, 1 - y)]


def _slot(px, py, pc):
    return 4 * px + 2 * py + pc


def _all_gather(arrays, name):
    n = len(arrays)

    def body(*refs):
        ins, outs = refs[:n], refs[n:2 * n]
        send_sems, recv_sems, local_sems = refs[2 * n:]
        me, sib, chips = _place()
        c = me[2]

        def copy(a, k, block, to, src=None):
            rows = outs[a].at[_slot(*block)]
            return pltpu.make_async_remote_copy(src_ref=rows if src is None else src, dst_ref=rows, send_sem=send_sems.at[a, k],
                                                recv_sem=recv_sems.at[a, k], device_id=to, device_id_type=MESH)

        local = [pltpu.make_async_copy(ins[a], outs[a].at[_slot(*me)], local_sems.at[a]) for a in range(n)]
        for cp in local:
            cp.start()
        sends = []
        for a in range(n):
            sends.append(copy(a, 0, me, sib, src=ins[a]))
            sends += [copy(a, 1 + j, me, (*chip, c), src=ins[a]) for j, chip in enumerate(chips)]
        for cp in sends:
            cp.start()
        for a in range(n):
            for j, chip in enumerate(chips):
                copy(a, 1 + j, (*chip, c), me).wait_recv()
                fwd = copy(a, 4 + j, (*chip, c), sib)
                fwd.start()
                sends.append(fwd)
        for a in range(n):
            copy(a, 0, sib, me).wait_recv()
            for j, chip in enumerate(chips):
                copy(a, 4 + j, (*chip, 1 - c), me).wait_recv()
        for cp in sends:
            cp.wait_send()
        for cp in local:
            cp.wait()

    anyspec = pl.BlockSpec(memory_space=pl.ANY)
    return pl.pallas_call(
        body,
        in_specs=[anyspec] * n,
        out_specs=[anyspec] * n,
        out_shape=[jax.ShapeDtypeStruct((NDEV,) + a.shape, a.dtype) for a in arrays],
        scratch_shapes=[pltpu.SemaphoreType.DMA((n, NDEV - 1)), pltpu.SemaphoreType.DMA((n, NDEV - 1)), pltpu.SemaphoreType.DMA((n,))],
        name=name,
    )(*arrays)


def _slab_exchange(arrays, nslots, pick):
    n = len(arrays)

    def copies(ins, outs, sems):
        send_sems, recv_sems = sems
        return [pltpu.make_async_remote_copy(src_ref=ins[a].at[pick(k)[0]], dst_ref=outs[a].at[k], send_sem=send_sems.at[a, k],
                                             recv_sem=recv_sems.at[a, k], device_id=pick(k)[1], device_id_type=MESH)
                for a in range(n) for k in range(nslots)]

    def start(ins, outs, sems):
        for cp in copies(ins, outs, sems):
            cp.start()

    def finish(ins, outs, sems):
        cps = copies(ins, outs, sems)
        for cp in cps:
            cp.wait_recv()
        for cp in cps:
            cp.wait_send()

    return _Exchange(arrays, [jax.ShapeDtypeStruct((nslots,) + a.shape[1:], a.dtype) for a in arrays],
                     [pltpu.SemaphoreType.DMA((n, nslots)), pltpu.SemaphoreType.DMA((n, nslots))], start, finish)


def _direct_gather(arrays):
    n = len(arrays)

    def peer(d):
        p = (_my_id() + d) % NDEV
        return (p // 4, (p // 2) % 2, p % 2), p

    def copies(ins, outs, sems, receiving):
        send_sems, recv_sems, _ = sems
        slot = lambda d: peer(NDEV - d)[1] if receiving else _my_id()
        return [pltpu.make_async_remote_copy(src_ref=ins[a], dst_ref=outs[a].at[slot(d)], send_sem=send_sems.at[a, d - 1],
                                             recv_sem=recv_sems.at[a, d - 1], device_id=peer(d)[0], device_id_type=MESH)
                for d in range(1, NDEV) for a in range(n)]

    def local(ins, outs, sems):
        return [pltpu.make_async_copy(ins[a], outs[a].at[_my_id()], sems[2].at[a]) for a in range(n)]

    def start(ins, outs, sems):
        for cp in local(ins, outs, sems) + copies(ins, outs, sems, False):
            cp.start()

    def finish(ins, outs, sems):
        for cp in copies(ins, outs, sems, True):
            cp.wait_recv()
        for cp in copies(ins, outs, sems, False):
            cp.wait_send()
        for cp in local(ins, outs, sems):
            cp.wait()

    return _Exchange(arrays, [jax.ShapeDtypeStruct((NDEV,) + a.shape, a.dtype) for a in arrays],
                     [pltpu.SemaphoreType.DMA((n, NDEV - 1)), pltpu.SemaphoreType.DMA((n, NDEV - 1)), pltpu.SemaphoreType.DMA((n,))],
                     start, finish)


def _run_exchange(exchange, name):
    n = len(exchange.arrays)

    def body(*refs):
        ins, outs, sems = refs[:n], refs[n:2 * n], refs[2 * n:]
        exchange.start(ins, outs, sems)
        exchange.finish(ins, outs, sems)

    anyspec = pl.BlockSpec(memory_space=pl.ANY)
    return pl.pallas_call(body, in_specs=[anyspec] * n, out_specs=[anyspec] * n, out_shape=exchange.out_shapes,
                          scratch_shapes=exchange.sem_shapes, name=name)(*exchange.arrays)


def _slabs_bf16(x, rows_per_block, name):
    n, R, C = x.shape
    tr = min(rows_per_block, R)

    def body(x_ref, o_ref):
        o_ref[...] = x_ref[...].astype(BF16)

    blk = pl.BlockSpec((1, tr, C), lambda k, i: (k, i, 0))
    return pl.pallas_call(body, grid=(n, R // tr), in_specs=[blk], out_specs=blk, out_shape=jax.ShapeDtypeStruct(x.shape, BF16),
                          compiler_params=_cparams(("parallel", "parallel"), 32 << 20), name=name)(x)


def _to_sibling(k):
    me, sib, chips = _place()
    dest = sib if k == 0 else (*chips[k - 1], sib[2])
    return _slot(*dest), sib


def _to_chips(k):
    me, sib, chips = _place()
    return k, (*chips[k], me[2])


def _chip_sums(own, got, slots, rows_per_block, name):
    _, R, C = own.shape
    tr = min(rows_per_block, R)

    def body(slots_ref, own_ref, got_ref, o_ref):
        o_ref[0] = (own_ref[0].astype(F32) + got_ref[0].astype(F32)).astype(BF16)

    return pl.pallas_call(
        body,
        grid_spec=pltpu.PrefetchScalarGridSpec(
            num_scalar_prefetch=1, grid=(3, R // tr),
            in_specs=[pl.BlockSpec((1, tr, C), lambda j, i, s: (s[1 + j], i, 0)), pl.BlockSpec((1, tr, C), lambda j, i, s: (1 + j, i, 0))],
            out_specs=pl.BlockSpec((1, tr, C), lambda j, i, s: (j, i, 0))),
        out_shape=jax.ShapeDtypeStruct((3, R, C), BF16),
        compiler_params=_cparams(("parallel", "parallel"), 32 << 20), name=name,
    )(slots, own, got)


def _adamw_math(w, g, m, v):
    m = ADAM_B1 * m + (1.0 - ADAM_B1) * g
    v = ADAM_B2 * v + (1.0 - ADAM_B2) * (g * g)
    m_hat = m / (1.0 - ADAM_B1 ** ADAM_STEP)
    v_hat = v / (1.0 - ADAM_B2 ** ADAM_STEP)
    delta = -ADAM_LR * (m_hat / (jnp.sqrt(v_hat) + ADAM_EPS) + ADAM_WD * w)
    return delta, m, v


def _reduce_adamw(own, got1, got2, slots, w, m, v, rows_per_block, name):
    _, R, C = own.shape
    tr = min(rows_per_block, R)
    assert R % tr == 0

    def body(slots_ref, own_ref, g1_ref, g2_ref, w_ref, m_ref, v_ref, g_ref, d_ref, nm_ref, nv_ref):
        g = own_ref[0] + g1_ref[0].astype(F32)
        for j in range(3):
            g = g + g2_ref[j].astype(F32)
        g_ref[...] = g
        d, nm, nv = _adamw_math(w_ref[...], g, m_ref[...], v_ref[...])
        d_ref[...] = d
        nm_ref[...] = nm
        nv_ref[...] = nv

    blk = pl.BlockSpec((tr, C), lambda i, s: (i, 0))
    return pl.pallas_call(
        body,
        grid_spec=pltpu.PrefetchScalarGridSpec(
            num_scalar_prefetch=1, grid=(R // tr,),
            in_specs=[pl.BlockSpec((1, tr, C), lambda i, s: (s[0], i, 0)), pl.BlockSpec((1, tr, C), lambda i, s: (0, i, 0)),
                      pl.BlockSpec((3, tr, C), lambda i, s: (0, i, 0)), blk, blk, blk],
            out_specs=[blk] * 4),
        out_shape=[jax.ShapeDtypeStruct((R, C), F32)] * 4,
        compiler_params=_cparams(("parallel",), 48 << 20), name=name,
    )(slots, own, got1, got2, w, m, v)


def _reduce_adamw_t(own, got1, got2, slots, wt, mt, vt, rows_per_block, name):
    R, C = own.shape
    tr = min(rows_per_block, R)
    assert R % tr == 0

    def body(slots_ref, own_ref, g1_ref, g2_ref, w_ref, m_ref, v_ref, g_ref, d_ref, nm_ref, nv_ref):
        g = own_ref[...] + g1_ref[0].astype(F32)
        for j in range(3):
            g = g + g2_ref[j].astype(F32)
        g = jnp.transpose(g)
        g_ref[...] = g
        d, nm, nv = _adamw_math(w_ref[...], g, m_ref[...], v_ref[...])
        d_ref[...] = d
        nm_ref[...] = nm
        nv_ref[...] = nv

    blk = pl.BlockSpec((C, tr), lambda i, s: (0, i))
    return pl.pallas_call(
        body,
        grid_spec=pltpu.PrefetchScalarGridSpec(
            num_scalar_prefetch=1, grid=(R // tr,),
            in_specs=[pl.BlockSpec((tr, C), lambda i, s: (i, 0)), pl.BlockSpec((1, tr, C), lambda i, s: (0, i, 0)),
                      pl.BlockSpec((3, tr, C), lambda i, s: (0, i, 0)), blk, blk, blk],
            out_specs=[blk] * 4),
        out_shape=[jax.ShapeDtypeStruct((C, R), F32)] * 4,
        compiler_params=_cparams(("parallel",), 48 << 20), name=name,
    )(slots, own, got1, got2, wt, mt, vt)


def _sum_adamw(parts, w, m, v, rows_per_block, name):
    n, R, C = parts.shape
    tr = min(rows_per_block, R)
    assert R % tr == 0

    def body(p_ref, w_ref, m_ref, v_ref, g_ref, d_ref, nm_ref, nv_ref):
        g = p_ref[0]
        for j in range(1, n):
            g = g + p_ref[j]
        g_ref[...] = g
        d, nm, nv = _adamw_math(w_ref[...], g, m_ref[...], v_ref[...])
        d_ref[...] = d
        nm_ref[...] = nm
        nv_ref[...] = nv

    blk = pl.BlockSpec((tr, C), lambda i: (i, 0))
    return pl.pallas_call(
        body, grid=(R // tr,),
        in_specs=[pl.BlockSpec((n, tr, C), lambda i: (0, i, 0)), blk, blk, blk],
        out_specs=[blk] * 4,
        out_shape=[jax.ShapeDtypeStruct((R, C), F32)] * 4,
        compiler_params=_cparams(("parallel",), 48 << 20), name=name,
    )(parts, w, m, v)


def kernel(x, positions, pre_norm_g, w_in, q_a_norm_g, w_q_b, kv_a_norm_g, w_kv_b, conv_w, w_o_mla, w_o_conv, w_out, post_norm_g, loss_target, m_pre_norm_g, m_w_in, m_q_a_norm_g, m_w_q_b, m_kv_a_norm_g, m_w_kv_b, m_conv_w, m_w_o_mla, m_w_o_conv, m_w_out, m_post_norm_g, v_pre_norm_g, v_w_in, v_q_a_norm_g, v_w_q_b, v_kv_a_norm_g, v_w_kv_b, v_conv_w, v_w_o_mla, v_w_o_conv, v_w_out, v_post_norm_g):
    S = x.shape[1]
    conv_pad = jnp.zeros((8, 256), F32).at[0:3, :].set(conv_w)
    g_in, = _all_gather([w_in.astype(BF16)], "all_gather_w_in")
    w_in_f = _assemble_w_in(g_in)
    gather_rest = _direct_gather([w_q_b.astype(BF16), w_kv_b.astype(BF16), conv_pad, w_o_mla.astype(BF16), w_o_conv.astype(BF16),
                                  w_out.astype(BF16)])

    def assemble_rest(g_q, g_kv, g_cw, g_om, g_oc, g_out):
        return (_assemble_w_q(g_q), _concat_cols(g_kv, BF16, "assemble_w_kv"), _concat_cols(g_cw, F32, "assemble_conv_w"),
                g_om.reshape(D, D), g_oc.reshape(D, D), g_out.reshape(D, D))

    gnames = ["w_in", "w_q", "w_kv", "conv_w", "w_o_mla", "w_o_conv", "w_out"]
    (mx, my, mc), _, chips = _place()
    slots = jnp.stack([_slot(mx, my, mc)] + [_slot(cx, cy, mc) for cx, cy in chips]).astype(jnp.int32)

    def reduce_grads(r):
        own_in, slabs_in = _split_dw_in(r["dw_in"], slots)
        own = [own_in, _split_dw_q(r["dw_q"]), _split_cols(r["dw_kv"], NDEV, "split_dw_kv"),
               _split_cols(r["dconv_w"], NDEV, "split_dconv_w"), r["dw_o_mla"].reshape(NDEV, D // NDEV, D),
               r["dw_o_conv"].reshape(NDEV, D // NDEV, D), r["dw_out"].reshape(NDEV, D // NDEV, D)]
        slabs = [slabs_in] + [_slabs_bf16(g, 128, "bf16_" + nm) for g, nm in zip(own[1:], gnames[1:])]
        to_sibling = _slab_exchange(slabs, 4, _to_sibling)
        got1 = _run_exchange(to_sibling, "grads_to_sibling")
        sums = [_chip_sums(o, g1, slots, 128, "chip_sum_" + nm) for o, g1, nm in zip([slabs_in] + own[1:], got1, gnames)]
        return _slab_exchange(sums, 3, _to_chips), lambda got2: (own, got1, got2)

    row2 = lambda a: a.reshape(1, -1)
    r = _local_step(x[0], positions.reshape(S, 1), loss_target[0], row2(pre_norm_g), row2(q_a_norm_g), row2(kv_a_norm_g),
                    row2(post_norm_g), w_in_f, (gather_rest, assemble_rest), reduce_grads)
    own, got1, got2 = r["reduced"]
    small = jnp.concatenate([r["dg_pre"][0:1], r["dg_post"][0:1], jnp.pad(r["dg_qa"][0:1], ((0, 0), (0, D - RQ))),
                             jnp.pad(r["dg_kva"][0:1], ((0, 0), (0, D - RKV))), jnp.pad(r["loss"][0:1], ((0, 0), (0, D - 128))),
                             jnp.zeros((3, D), F32)], axis=0)
    p_small, = _all_gather([small], "all_gather_small")
    pad8 = lambda a: jnp.zeros((8, 256), F32).at[0:3, :].set(a)
    params = [(w_in, m_w_in, v_w_in), (w_q_b, m_w_q_b, v_w_q_b), (w_kv_b, m_w_kv_b, v_w_kv_b), (conv_pad, pad8(m_conv_w), pad8(v_conv_w)),
              (w_o_mla, m_w_o_mla, v_w_o_mla), (w_o_conv, m_w_o_conv, v_w_o_conv), (w_out, m_w_out, v_w_out)]
    o_q, o_kv, o_cw, o_om, o_oc, o_out = [
        _reduce_adamw(o, g1, g2, slots, w, m, v, 128, "adamw_" + nm)
        for o, g1, g2, (w, m, v), nm in list(zip(own, got1, got2, params, gnames))[1:]]
    o_cw = [a[0:3] for a in o_cw]
    o_in = [a.T for a in _reduce_adamw_t(own[0], got1[0], got2[0], slots, w_in.T, m_w_in.T, v_w_in.T, 128, "adamw_w_in")]
    padv = lambda a: jnp.pad(a.reshape(1, -1), ((0, 0), (0, D - a.shape[0])))
    stack = lambda pre, post, qa, kva: jnp.concatenate([row2(pre), row2(post), padv(qa), padv(kva), jnp.zeros((4, D), F32)], axis=0)
    o_g = _sum_adamw(p_small, stack(pre_norm_g, post_norm_g, q_a_norm_g, kv_a_norm_g),
                     stack(m_pre_norm_g, m_post_norm_g, m_q_a_norm_g, m_kv_a_norm_g),
                     stack(v_pre_norm_g, v_post_norm_g, v_q_a_norm_g, v_kv_a_norm_g), 8, "adamw_gains")
    loss = o_g[0][4, 0]
    outs = {}
    for idx, kind in enumerate(("grad", "delta", "new_m", "new_v")):
        o = o_g[idx]
        outs[kind] = dict(pre_norm_g=o[0], w_in=o_in[idx], q_a_norm_g=o[2, 0:RQ], w_q_b=o_q[idx], kv_a_norm_g=o[3, 0:RKV],
                          w_kv_b=o_kv[idx], conv_w=o_cw[idx], w_o_mla=o_om[idx], w_o_conv=o_oc[idx], w_out=o_out[idx], post_norm_g=o[1])
    names = ["pre_norm_g", "w_in", "q_a_norm_g", "w_q_b", "kv_a_norm_g", "w_kv_b", "conv_w", "w_o_mla", "w_o_conv", "w_out", "post_norm_g"]
    return (loss, r["grad_x"][None], *[outs["grad"][n] for n in names], *[outs["delta"][n] for n in names],
            *[outs["new_m"][n] for n in names], *[outs["new_v"][n] for n in names])
```

```python
import functools
import math

import jax
import jax.numpy as jnp
from jax import lax
from jax.experimental import pallas as pl
from jax.experimental.pallas import tpu as pltpu

F32 = jnp.float32
BF16 = jnp.bfloat16

NDEV = 8
D = 2048
H = 16
DN = 128
DR = 64
DV = 128
RQ = 512
RKV = 512
HW = 256
ROPE_THETA = 10000.0
RMS_EPS = 1e-6
N_IN = 15424
SHARD_IN = N_IN // NDEV
SMALL = RQ + RKV + DR
NP = 7 * D + RQ + RKV + 128
SEG = dict(z_mla=0, c_in=1, b_gate=2, c_gate=3, z_conv=4, g_mla=5, g_conv=6)
OFF_QA = 7 * D
OFF_CKV = OFF_QA + RQ
OFF_KR = OFF_CKV + RKV
EXT = 2176
VMEM_CAP = 56 * 1024 * 1024

ADAM_LR = 0.001
ADAM_B1 = 0.9
ADAM_B2 = 0.999
ADAM_EPS = 1e-08
ADAM_WD = 0.01
ADAM_STEP = 10

LOG2E = math.log2(math.e)
SM_SCALE = 1.0 / math.sqrt(DN + DR)
QSCALE = SM_SCALE * LOG2E
NN = (((1,), (0,)), ((), ()))
NT = (((1,), (1,)), ((), ()))
TN = (((0,), (0,)), ((), ()))
MESH = pl.DeviceIdType.MESH


def _cparams(sem, vmem_bytes):
    return pltpu.CompilerParams(dimension_semantics=sem, vmem_limit_bytes=int(min(VMEM_CAP, max(vmem_bytes, 16 << 20))))


def _nbytes(shape, dtype):
    return math.prod(shape) * jnp.dtype(dtype).itemsize


class _Exchange:
    def __init__(self, arrays, out_shapes, sem_shapes, start, finish):
        self.arrays, self.out_shapes, self.sem_shapes, self.start, self.finish = arrays, out_shapes, sem_shapes, start, finish
        self.n_sems = len(sem_shapes)


def _matmul(a, b, *, mode, out_dtype, tm, tn, tk, name, m_outer=False, exchange=None):
    if mode == "nn":
        (M, K), (K2, N) = a.shape, b.shape
    elif mode == "nt":
        (M, K), (N, K2) = a.shape, b.shape
    else:
        (K, M), (K2, N) = a.shape, b.shape
    assert K == K2, (a.shape, b.shape, mode)
    tm, tn, tk = min(tm, M), min(tn, N), min(tk, K)
    assert M % tm == 0 and N % tn == 0 and K % tk == 0, (M, N, K, tm, tn, tk)
    ni, nj, nk = M // tm, N // tn, K // tk
    dims = dict(nn=NN, nt=NT, tn=TN)[mode]

    if m_outer:
        grid = (ni, nj, nk)
        ij = lambda g0, g1: (g0, g1)
    else:
        grid = (nj, ni, nk)
        ij = lambda g0, g1: (g1, g0)

    if mode == "tn":
        a_spec = pl.BlockSpec((tk, tm), lambda g0, g1, k: (k, ij(g0, g1)[0]))
        a_tile = (tk, tm)
    else:
        a_spec = pl.BlockSpec((tm, tk), lambda g0, g1, k: (ij(g0, g1)[0], k))
        a_tile = (tm, tk)
    if mode == "nt":
        b_spec = pl.BlockSpec((tn, tk), lambda g0, g1, k: (ij(g0, g1)[1], k))
    else:
        b_spec = pl.BlockSpec((tk, tn), lambda g0, g1, k: (k, ij(g0, g1)[1]))
    o_spec = pl.BlockSpec((tm, tn), lambda g0, g1, k: ij(g0, g1))

    n_in = len(exchange.arrays) if exchange else 0

    def body(a_ref, b_ref, *refs):
        x_in, o_ref, x_out, scratch = refs[:n_in], refs[n_in], refs[n_in + 1:2 * n_in + 1], refs[2 * n_in + 1:]
        if exchange:
            sems = scratch[len(scratch) - exchange.n_sems:]
            step = (pl.program_id(0) * grid[1] + pl.program_id(1)) * grid[2] + pl.program_id(2)

            @pl.when(step == 0)
            def _():
                exchange.start(x_in, x_out, sems)

        prod = lax.dot_general(a_ref[...], b_ref[...], dims, preferred_element_type=F32)
        if nk == 1:
            o_ref[...] = prod.astype(o_ref.dtype)
        else:
            acc_ref = scratch[0]
            k = pl.program_id(2)

            @pl.when(k == 0)
            def _():
                acc_ref[...] = prod

            @pl.when((k > 0) & (k < nk - 1))
            def _():
                acc_ref[...] += prod

            @pl.when(k == nk - 1)
            def _():
                o_ref[...] = (acc_ref[...] + prod).astype(o_ref.dtype)

        if exchange:
            @pl.when(step == grid[0] * grid[1] * grid[2] - 1)
            def _():
                exchange.finish(x_in, x_out, sems)

    vmem = 2 * (_nbytes(a_tile, a.dtype) + _nbytes((tk, tn), b.dtype) + _nbytes((tm, tn), out_dtype)) + 2 * _nbytes((tm, tn), F32)
    anyspec = pl.BlockSpec(memory_space=pl.ANY)
    outs = pl.pallas_call(
        body,
        grid=grid,
        in_specs=[a_spec, b_spec] + [anyspec] * n_in,
        out_specs=[o_spec] + [anyspec] * n_in,
        out_shape=[jax.ShapeDtypeStruct((M, N), out_dtype)] + (exchange.out_shapes if exchange else []),
        scratch_shapes=([] if nk == 1 else [pltpu.VMEM((tm, tn), F32)]) + (exchange.sem_shapes if exchange else []),
        compiler_params=_cparams(("arbitrary",) * 3 if exchange else ("parallel", "parallel", "arbitrary"), vmem + (8 << 20)),
        name=name,
    )(a, b, *(exchange.arrays if exchange else []))
    return (outs[0], outs[1:]) if exchange else outs[0]


def _in_dest(k):
    return SHARD_IN * k - SMALL


def _assemble_w_in(g):
    R = 128

    def body(g_ref, o_ref, ext, acc):
        ext[...] = jnp.zeros_like(ext)
        acc[...] = jnp.zeros_like(acc)
        lane = lax.broadcasted_iota(jnp.int32, (R, EXT), 1)
        ext[:, 0:SHARD_IN] = g_ref[0].astype(F32)
        v = ext[...]
        acc[:, OFF_QA:NP] = jnp.where(lane[:, 0:NP - OFF_QA] < SMALL, v[:, 0:NP - OFF_QA], 0.0)
        w = v[:, 1024:2048]
        w = pltpu.roll(w, 1024 - 64, 1)
        acc[:, 0:1024] = jnp.where(lane[:, 0:1024] < SHARD_IN - SMALL, w, 0.0)
        for k in range(1, NDEV):
            ext[:, 0:SHARD_IN] = g_ref[k].astype(F32)
            dest = _in_dest(k)
            t, o = dest // 128, dest % 128
            width = -(-(o + SHARD_IN) // 128) * 128
            v = pltpu.roll(ext[...], o, 1)[:, 0:width]
            acc[:, 128 * t:128 * t + width] += v
        o_ref[...] = acc[...].astype(BF16)

    return pl.pallas_call(
        body,
        grid=(D // R,),
        in_specs=[pl.BlockSpec((NDEV, R, SHARD_IN), lambda i: (0, i, 0))],
        out_specs=pl.BlockSpec((R, NP), lambda i: (i, 0)),
        out_shape=jax.ShapeDtypeStruct((D, NP), BF16),
        scratch_shapes=[pltpu.VMEM((R, EXT), F32), pltpu.VMEM((R, NP), F32)],
        compiler_params=_cparams(("parallel",), 40 << 20),
        name="assemble_w_in",
    )(g)


def _split_dw_in(dw, slots):
    R = 128

    def body(slots_ref, dw_ref, o_ref, ob_ref):
        def put(k, cols, v):
            ob_ref[k, :, cols] = v.astype(BF16)

            @pl.when(slots_ref[0] == k)
            def _():
                o_ref[:, cols] = v

        lane = lax.broadcasted_iota(jnp.int32, (R, 1024), 1)
        put(0, slice(0, 1024), dw_ref[:, OFF_QA:OFF_QA + 1024])
        tail = dw_ref[:, OFF_QA + 1024:NP]
        tail = jnp.concatenate([tail, jnp.zeros((R, 1024 - 128), F32)], axis=1)
        head = dw_ref[:, 0:1024]
        mixed = jnp.where(lane < 64, tail, pltpu.roll(head, 64, 1))
        put(0, slice(1024, SHARD_IN), mixed[:, 0:SHARD_IN - 1024])
        for k in range(1, NDEV):
            dest = _in_dest(k)
            t, o = dest // 128, dest % 128
            width = -(-(o + SHARD_IN) // 128) * 128
            v = dw_ref[:, 128 * t:128 * t + width]
            v = pltpu.roll(v, width - o, 1)
            put(k, slice(0, SHARD_IN), v[:, 0:SHARD_IN])

    return pl.pallas_call(
        body,
        grid_spec=pltpu.PrefetchScalarGridSpec(
            num_scalar_prefetch=1, grid=(D // R,),
            in_specs=[pl.BlockSpec((R, NP), lambda i, s: (i, 0))],
            out_specs=[pl.BlockSpec((R, SHARD_IN), lambda i, s: (i, 0)), pl.BlockSpec((NDEV, R, SHARD_IN), lambda i, s: (0, i, 0))]),
        out_shape=[jax.ShapeDtypeStruct((D, SHARD_IN), F32), jax.ShapeDtypeStruct((NDEV, D, SHARD_IN), BF16)],
        compiler_params=_cparams(("parallel",), 48 << 20),
        name="split_dw_in",
    )(slots, dw)


def _assemble_w_q(g):
    def body(g_ref, o_ref):
        lane = lax.broadcasted_iota(jnp.int32, (RQ, 128), 1)
        lo = lane < 64
        for k in range(NDEV):
            t0 = g_ref[k, :, 0:128].astype(F32)
            t1 = g_ref[k, :, 128:256].astype(F32)
            t2 = g_ref[k, :, 256:384].astype(F32)
            base = 2 * k * HW
            o_ref[:, base:base + 128] = t0.astype(BF16)
            o_ref[:, base + 128:base + 256] = jnp.where(lo, t1, 0.0).astype(BF16)
            o_ref[:, base + 256:base + 384] = pltpu.roll(jnp.where(lo, t2, t1), 64, 1).astype(BF16)
            o_ref[:, base + 384:base + 512] = jnp.where(lo, pltpu.roll(t2, 64, 1), 0.0).astype(BF16)

    return pl.pallas_call(
        body,
        out_shape=jax.ShapeDtypeStruct((RQ, H * HW), BF16),
        compiler_params=_cparams(None, 32 << 20),
        name="assemble_w_q",
    )(g)


def _split_dw_q(dw):
    def body(dw_ref, o_ref):
        lane = lax.broadcasted_iota(jnp.int32, (RQ, 128), 1)
        lo = lane < 64
        for k in range(NDEV):
            base = 2 * k * HW
            a = dw_ref[:, base:base + 128]
            b = dw_ref[:, base + 128:base + 256]
            c = pltpu.roll(dw_ref[:, base + 256:base + 384], 64, 1)
            d = pltpu.roll(dw_ref[:, base + 384:base + 512], 64, 1)
            o_ref[k, :, 0:128] = a
            o_ref[k, :, 128:256] = jnp.where(lo, b, c)
            o_ref[k, :, 256:384] = jnp.where(lo, c, d)

    return pl.pallas_call(
        body,
        out_shape=jax.ShapeDtypeStruct((NDEV, RQ, 384), F32),
        compiler_params=_cparams(None, 32 << 20),
        name="split_dw_q",
    )(dw)


def _concat_cols(g, dtype, name):
    n, R, C = g.shape

    def body(g_ref, o_ref):
        for k in range(n):
            o_ref[:, k * C:(k + 1) * C] = g_ref[k].astype(dtype)

    return pl.pallas_call(body, out_shape=jax.ShapeDtypeStruct((R, n * C), dtype), compiler_params=_cparams(None, 32 << 20), name=name)(g)


def _split_cols(x, n, name):
    R, NC = x.shape
    C = NC // n

    def body(x_ref, o_ref):
        for k in range(n):
            o_ref[k] = x_ref[:, k * C:(k + 1) * C]

    return pl.pallas_call(body, out_shape=jax.ShapeDtypeStruct((n, R, C), x.dtype), compiler_params=_cparams(None, 32 << 20), name=name)(x)


def _rms_scale(xf):
    return lax.rsqrt(jnp.mean(xf * xf, axis=-1, keepdims=True) + RMS_EPS)


def _prenorm(x, g, ts):
    S = x.shape[0]

    def body(x_ref, g_ref, h_ref, ht_ref):
        xf = x_ref[...]
        h = xf * _rms_scale(xf) * g_ref[...]
        h_ref[...] = h.astype(BF16)
        ht_ref[...] = jnp.transpose(h).astype(BF16)

    return pl.pallas_call(
        body,
        grid=(S // ts,),
        in_specs=[pl.BlockSpec((ts, D), lambda i: (i, 0)), pl.BlockSpec((1, D), lambda i: (0, 0))],
        out_specs=[pl.BlockSpec((ts, D), lambda i: (i, 0)), pl.BlockSpec((D, ts), lambda i: (0, i))],
        out_shape=[jax.ShapeDtypeStruct((S, D), BF16), jax.ShapeDtypeStruct((D, S), BF16)],
        compiler_params=_cparams(("parallel",), 32 << 20),
        name="prenorm",
    )(x, g)


def _inv_freq_tile():
    inv_freq = ROPE_THETA ** (-jnp.arange(0, DR, 2, dtype=F32) / DR)
    return jnp.tile(inv_freq, 4).reshape(1, 128)


def _rope_tables(pos_ref, freq_ref, ts):
    lane = lax.broadcasted_iota(jnp.int32, (ts, 128), 1)
    ang = pos_ref[...].astype(F32) * freq_ref[...]
    return jnp.cos(ang), jnp.sin(ang), lane


def _rope_swap(t, lane):
    return jnp.where(lane < 32, pltpu.roll(t, 96, 1), pltpu.roll(t, 32, 1))


def _qkv_prep(proj, pos, freq, g_qa, g_kva, w_q, w_kv, ts):
    S = proj.shape[0]

    def body(qa_ref, ckv_ref, kr_ref, pos_ref, freq_ref, gq_ref, gk_ref, wq_ref, wkv_ref, q_ref, k_ref, v_ref, vt_ref, qn_ref, kvn_ref):
        qa = _f32(qa_ref)
        qn = (qa * _rms_scale(qa) * gq_ref[...]).astype(BF16)
        ckv = _f32(ckv_ref)
        kvn = (ckv * _rms_scale(ckv) * gk_ref[...]).astype(BF16)
        qn_ref[...] = qn
        kvn_ref[...] = kvn
        cos, sin, lane = _rope_tables(pos_ref, freq_ref, ts)
        sgn_sin = jnp.where(lane < 32, -sin, sin)
        live = lane < DR
        kr = _f32(kr_ref)
        kr = jnp.where(live, kr * cos + _rope_swap(kr, lane) * sgn_sin, 0.0).astype(BF16)
        qf = jnp.dot(qn, wq_ref[...], preferred_element_type=F32)
        kvf = jnp.dot(kvn, wkv_ref[...], preferred_element_type=F32)
        for h in range(H):
            q_ref[h, :, 0:DN] = (qf[:, h * HW:h * HW + DN] * QSCALE).astype(BF16)
            t = qf[:, h * HW + DN:(h + 1) * HW]
            q_ref[h, :, DN:HW] = jnp.where(live, (t * cos + _rope_swap(t, lane) * sgn_sin) * QSCALE, 0.0).astype(BF16)
            k_ref[h, :, 0:DN] = kvf[:, h * HW:h * HW + DN].astype(BF16)
            k_ref[h, :, DN:HW] = kr
            vh = kvf[:, h * HW + DN:(h + 1) * HW]
            v_ref[h] = vh.astype(BF16)
            vt_ref[h] = jnp.transpose(vh).astype(BF16)

    row = lambda w, blk: pl.BlockSpec((ts, w), lambda i: (i, blk))
    full = lambda a: pl.BlockSpec(a.shape, lambda i: (0,) * a.ndim)
    return pl.pallas_call(
        body,
        grid=(S // ts,),
        in_specs=[row(RQ, OFF_QA // RQ), row(RKV, OFF_CKV // RKV), row(128, OFF_KR // 128),
                  pl.BlockSpec((ts, 1), lambda i: (i, 0)), full(freq), full(g_qa), full(g_kva), full(w_q), full(w_kv)],
        out_specs=[pl.BlockSpec((H, ts, HW), lambda i: (0, i, 0)), pl.BlockSpec((H, ts, HW), lambda i: (0, i, 0)),
                   pl.BlockSpec((H, ts, DV), lambda i: (0, i, 0)), pl.BlockSpec((H, DV, ts), lambda i: (0, 0, i)),
                   pl.BlockSpec((ts, RQ), lambda i: (i, 0)), pl.BlockSpec((ts, RKV), lambda i: (i, 0))],
        out_shape=[jax.ShapeDtypeStruct((H, S, HW), BF16), jax.ShapeDtypeStruct((H, S, HW), BF16),
                   jax.ShapeDtypeStruct((H, S, DV), BF16), jax.ShapeDtypeStruct((H, DV, S), BF16),
                   jax.ShapeDtypeStruct((S, RQ), BF16), jax.ShapeDtypeStruct((S, RKV), BF16)],
        compiler_params=_cparams(("parallel",), 48 << 20),
        name="qkv_prep",
    )(proj, proj, proj, pos, freq, g_qa, g_kva, w_q, w_kv)


def _col_to_row8(col, n):
    return jnp.transpose(jnp.broadcast_to(col, (n, 128)))[0:8, :]


def _causal_pairs(n, key_major):
    if key_major:
        pairs = [(i, j) for j in range(n) for i in range(j, n)]
    else:
        pairs = [(i, j) for i in range(n) for j in range(i + 1)]
    return jnp.array([p[0] for p in pairs], jnp.int32), jnp.array([p[1] for p in pairs], jnp.int32)


def _flash_fwd(q, k, vt, tq, hb=8):
    S = q.shape[1]
    nq = S // tq
    tsub = tq
    nsub = tq // tsub

    qi_tab, kj_tab = _causal_pairs(nq, key_major=False)

    def body(qi_ref, kj_ref, q_ref, k_ref, vt_ref, o_ref, lse_ref, m_sc, l_sc, acc_sc):
        p = pl.program_id(1)
        qi, kj = qi_ref[p], kj_ref[p]

        @pl.when(kj == 0)
        def _():
            m_sc[...] = jnp.full_like(m_sc, -jnp.inf)
            l_sc[...] = jnp.zeros_like(l_sc)
            acc_sc[...] = jnp.zeros_like(acc_sc)

        def step(diag):
            chains = [(h, u) for h in range(hb) for u in range(nsub)]

            def scores(h, u):
                return lax.dot_general(k_ref[h], q_ref[h, u * tsub:(u + 1) * tsub, :], NT, preferred_element_type=F32)

            st_next = scores(*chains[0])
            for ci, (h, u) in enumerate(chains):
                st = st_next
                if ci + 1 < len(chains):
                    st_next = scores(*chains[ci + 1])
                cols = slice(u * tsub, (u + 1) * tsub)
                if diag:
                    r = lax.broadcasted_iota(jnp.int32, (tq, tsub), 0)
                    c = lax.broadcasted_iota(jnp.int32, (tq, tsub), 1) + u * tsub
                    st = jnp.where(r <= c, st, -jnp.inf)
                m_prev = m_sc[h, 0:1, cols]
                m_new = jnp.maximum(m_prev, jnp.max(st, axis=0, keepdims=True))
                alpha = jnp.exp2(m_prev - m_new)
                pt = jnp.exp2(st - m_new)
                l_sc[h, :, cols] = jnp.broadcast_to(alpha * l_sc[h, 0:1, cols] + jnp.sum(pt, axis=0, keepdims=True), (8, tsub))
                m_sc[h, :, cols] = jnp.broadcast_to(m_new, (8, tsub))
                acc_sc[h, :, cols] = alpha * acc_sc[h, :, cols] + jnp.dot(vt_ref[h], pt.astype(BF16), preferred_element_type=F32)

        @pl.when(kj < qi)
        def _():
            step(False)

        @pl.when(kj == qi)
        def _():
            step(True)
            for h in range(hb):
                l = l_sc[h, 0:1, :]
                o_ref[:, h * DV:(h + 1) * DV] = jnp.transpose(acc_sc[h] / l).astype(BF16)
                lse_ref[h] = m_sc[h] + jnp.log2(l_sc[h])

    return pl.pallas_call(
        body,
        grid_spec=pltpu.PrefetchScalarGridSpec(
            num_scalar_prefetch=2,
            grid=(H // hb, len(qi_tab)),
            in_specs=[pl.BlockSpec((hb, tq, HW), lambda h, p, qi, kj: (h, qi[p], 0)),
                      pl.BlockSpec((hb, tq, HW), lambda h, p, qi, kj: (h, kj[p], 0)),
                      pl.BlockSpec((hb, DV, tq), lambda h, p, qi, kj: (h, 0, kj[p]))],
            out_specs=[pl.BlockSpec((tq, hb * DV), lambda h, p, qi, kj: (qi[p], h)),
                       pl.BlockSpec((hb, 8, tq), lambda h, p, qi, kj: (h, 0, qi[p]))],
            scratch_shapes=[pltpu.VMEM((hb, 8, tq), F32), pltpu.VMEM((hb, 8, tq), F32), pltpu.VMEM((hb, DV, tq), F32)],
        ),
        out_shape=[jax.ShapeDtypeStruct((S, H * DV), BF16), jax.ShapeDtypeStruct((H, 8, S), F32)],
        compiler_params=_cparams(("parallel", "arbitrary"), 40 << 20),
        name="flash_fwd",
    )(qi_tab, kj_tab, q, k, vt)


def _sigmoid(x):
    return 1.0 / (1.0 + jnp.exp(-x))


HALO = 16


def _f32(ref):
    return ref[...].astype(F32)


def _shift_rows(u, prev, n, first):
    ts = u.shape[0]
    row = lax.broadcasted_iota(jnp.int32, u.shape, 0)
    out = pltpu.roll(u, n, 0)
    for j in range(n):
        halo = jnp.where(first, 0.0, prev[HALO - n + j:HALO - n + j + 1, :])
        out = jnp.where(row == j, halo, out)
    return out


def _gates_fwd(proj, attn, conv_w, ts):
    S = proj.shape[0]

    def body(attn_ref, zm_ref, cin_ref, bg_ref, cg_ref, zc_ref, cin_p, cg_p, w_ref, am_ref, ac_ref):
        first = pl.program_id(0) == 0
        zm = _f32(zm_ref)
        am_ref[...] = (_f32(attn_ref) * (zm * _sigmoid(zm))).astype(BF16)
        u = _f32(cg_ref) * _f32(cin_ref)
        up = _f32(cg_p) * _f32(cin_p)
        w = w_ref[...]
        conv = w[0:1, :] * _shift_rows(u, up, 2, first) + w[1:2, :] * _shift_rows(u, up, 1, first) + w[2:3, :] * u
        zc = _f32(zc_ref)
        ac_ref[...] = (_f32(bg_ref) * conv * (zc * _sigmoid(zc))).astype(BF16)

    seg = lambda name: pl.BlockSpec((ts, D), lambda i: (i, SEG[name]))
    prev = lambda name: pl.BlockSpec((HALO, D), lambda i: (jnp.maximum(i * (ts // HALO) - 1, 0), SEG[name]))
    return pl.pallas_call(
        body,
        grid=(S // ts,),
        in_specs=[pl.BlockSpec((ts, D), lambda i: (i, 0)), seg("z_mla"), seg("c_in"), seg("b_gate"), seg("c_gate"), seg("z_conv"),
                  prev("c_in"), prev("c_gate"), pl.BlockSpec((8, D), lambda i: (0, 0))],
        out_specs=[pl.BlockSpec((ts, D), lambda i: (i, 0))] * 2,
        out_shape=[jax.ShapeDtypeStruct((S, D), BF16)] * 2,
        compiler_params=_cparams(("arbitrary",), 48 << 20),
        name="gates_fwd",
    )(attn, proj, proj, proj, proj, proj, proj, proj, conv_w)


def _merge_fwd(proj, y_mla, y_conv, ts):
    S = proj.shape[0]

    def body(gm_ref, gc_ref, ym_ref, yc_ref, o_ref):
        o_ref[...] = (_sigmoid(_f32(gm_ref)) * _f32(ym_ref) + _sigmoid(_f32(gc_ref)) * _f32(yc_ref)).astype(BF16)

    seg = lambda name: pl.BlockSpec((ts, D), lambda i: (i, SEG[name]))
    row = pl.BlockSpec((ts, D), lambda i: (i, 0))
    return pl.pallas_call(
        body, grid=(S // ts,), in_specs=[seg("g_mla"), seg("g_conv"), row, row], out_specs=row,
        out_shape=jax.ShapeDtypeStruct((S, D), BF16), compiler_params=_cparams(("parallel",), 32 << 20), name="merge_fwd",
    )(proj, proj, y_mla, y_conv)


def _post_loss(out, x, target, g_post, ts):
    S = out.shape[0]

    def body(o_ref, x_ref, t_ref, g_ref, dy_ref, do_ref, dg_ref, loss_ref):
        i = pl.program_id(0)
        o = o_ref[...]
        r = _rms_scale(o)
        n = o * r
        g = g_ref[...]
        err = x_ref[...] + n * g - t_ref[...]
        dy = err * (1.0 / D)
        dy_ref[...] = dy
        dn = dy * g
        do_ref[...] = (r * (dn - n * jnp.mean(dn * n, axis=-1, keepdims=True))).astype(BF16)
        dg = jnp.sum(dy * n, axis=0, keepdims=True)
        part = jnp.sum(jnp.sum(err * err, axis=0, keepdims=True), axis=1, keepdims=True) * (0.5 / D)

        @pl.when(i == 0)
        def _():
            dg_ref[...] = jnp.zeros_like(dg_ref)
            loss_ref[...] = jnp.zeros_like(loss_ref)

        dg_ref[0:1, :] += dg
        loss_ref[...] += jnp.broadcast_to(part, loss_ref.shape)

    row = pl.BlockSpec((ts, D), lambda i: (i, 0))
    return pl.pallas_call(
        body, grid=(S // ts,),
        in_specs=[row, row, row, pl.BlockSpec((1, D), lambda i: (0, 0))],
        out_specs=[row, row, pl.BlockSpec((8, D), lambda i: (0, 0)), pl.BlockSpec((8, 128), lambda i: (0, 0))],
        out_shape=[jax.ShapeDtypeStruct((S, D), F32), jax.ShapeDtypeStruct((S, D), BF16),
                   jax.ShapeDtypeStruct((8, D), F32), jax.ShapeDtypeStruct((8, 128), F32)],
        compiler_params=_cparams(("arbitrary",), 40 << 20), name="post_loss",
    )(out, x, target, g_post)


def _merge_bwd(proj, y_mla, y_conv, dmerged, ts):
    S = proj.shape[0]

    def body(gm_ref, gc_ref, ym_ref, yc_ref, dm_ref, dym_ref, dyc_ref, dgm_ref, dgc_ref):
        dm = _f32(dm_ref)
        sm = _sigmoid(_f32(gm_ref))
        sc = _sigmoid(_f32(gc_ref))
        dym_ref[...] = (dm * sm).astype(BF16)
        dyc_ref[...] = (dm * sc).astype(BF16)
        dgm_ref[...] = (dm * _f32(ym_ref) * (sm * (1.0 - sm))).astype(BF16)
        dgc_ref[...] = (dm * _f32(yc_ref) * (sc * (1.0 - sc))).astype(BF16)

    seg = lambda name: pl.BlockSpec((ts, D), lambda i: (i, SEG[name]))
    row = pl.BlockSpec((ts, D), lambda i: (i, 0))
    return pl.pallas_call(
        body, grid=(S // ts,), in_specs=[seg("g_mla"), seg("g_conv"), row, row, row], out_specs=[row] * 4,
        out_shape=[jax.ShapeDtypeStruct((S, D), BF16)] * 4, compiler_params=_cparams(("parallel",), 40 << 20), name="merge_bwd",
    )(proj, proj, y_mla, y_conv, dmerged)


def _gates_bwd(proj, attn, da_mla, da_conv, conv_w, ts):
    S = proj.shape[0]
    nblk = S // ts

    def body(attn_ref, zm_ref, cin_ref, bg_ref, cg_ref, zc_ref, dam_ref, dac_ref, cin_p, cg_p, bg_n, zc_n, dac_n, w_ref,
             dattn_ref, delta_ref, dzm_ref, dcin_ref, dbg_ref, dcg_ref, dzc_ref, dw_ref):
        i = pl.program_id(0)
        first = i == 0
        last = i == nblk - 1

        @pl.when(first)
        def _():
            dw_ref[...] = jnp.zeros_like(dw_ref)

        row = lax.broadcasted_iota(jnp.int32, (ts, DV), 0)
        for c in range(D // DV):
            cols = slice(c * DV, (c + 1) * DV)
            ld = lambda ref: ref[:, cols].astype(F32)
            zm = ld(zm_ref)
            sg = _sigmoid(zm)
            attn = ld(attn_ref)
            dam = ld(dam_ref)
            dattn = dam * (zm * sg)
            dattn_ref[:, cols] = dattn.astype(BF16)
            dzm_ref[:, cols] = (dam * attn * (sg * (1.0 + zm * (1.0 - sg)))).astype(BF16)
            delta_ref[c] = _col_to_row8(jnp.sum(dattn * attn, axis=1, keepdims=True), ts)
            w = w_ref[:, cols]
            cin, cg, bg, zc = ld(cin_ref), ld(cg_ref), ld(bg_ref), ld(zc_ref)
            u = cg * cin
            up = ld(cg_p) * ld(cin_p)
            u1 = _shift_rows(u, up, 1, first)
            u2 = _shift_rows(u, up, 2, first)
            conv = w[0:1, :] * u2 + w[1:2, :] * u1 + w[2:3, :] * u
            sgc = _sigmoid(zc)
            siluc = zc * sgc
            dac = ld(dac_ref)
            dbg_ref[:, cols] = (dac * conv * siluc).astype(BF16)
            dzc_ref[:, cols] = (dac * bg * conv * (sgc * (1.0 + zc * (1.0 - sgc)))).astype(BF16)
            dconv = dac * bg * siluc
            zn = ld(zc_n)
            dconv_n = jnp.where(last, 0.0, ld(dac_n) * ld(bg_n) * (zn * _sigmoid(zn)))
            d1 = jnp.where(row == ts - 1, dconv_n[0:1, :], pltpu.roll(dconv, ts - 1, 0))
            d2 = jnp.where(row == ts - 1, dconv_n[1:2, :], jnp.where(row == ts - 2, dconv_n[0:1, :], pltpu.roll(dconv, ts - 2, 0)))
            du = w[2:3, :] * dconv + w[1:2, :] * d1 + w[0:1, :] * d2
            dcg_ref[:, cols] = (du * cin).astype(BF16)
            dcin_ref[:, cols] = (du * cg).astype(BF16)
            dw_ref[0:1, cols] += jnp.sum(dconv * u2, axis=0, keepdims=True)
            dw_ref[1:2, cols] += jnp.sum(dconv * u1, axis=0, keepdims=True)
            dw_ref[2:3, cols] += jnp.sum(dconv * u, axis=0, keepdims=True)

    seg = lambda name: pl.BlockSpec((ts, D), lambda i: (i, SEG[name]))
    prev = lambda name: pl.BlockSpec((HALO, D), lambda i: (jnp.maximum(i * (ts // HALO) - 1, 0), SEG[name]))
    nxt = lambda blk: pl.BlockSpec((HALO, D), lambda i: (jnp.minimum((i + 1) * (ts // HALO), S // HALO - 1), blk))
    row = pl.BlockSpec((ts, D), lambda i: (i, 0))
    return pl.pallas_call(
        body, grid=(nblk,),
        in_specs=[row, seg("z_mla"), seg("c_in"), seg("b_gate"), seg("c_gate"), seg("z_conv"), row, row,
                  prev("c_in"), prev("c_gate"), nxt(SEG["b_gate"]), nxt(SEG["z_conv"]), nxt(0), pl.BlockSpec((8, D), lambda i: (0, 0))],
        out_specs=[row, pl.BlockSpec((H, 8, ts), lambda i: (0, 0, i)), row, row, row, row, row, pl.BlockSpec((8, D), lambda i: (0, 0))],
        out_shape=[jax.ShapeDtypeStruct((S, D), BF16), jax.ShapeDtypeStruct((H, 8, S), F32)] + [jax.ShapeDtypeStruct((S, D), BF16)] * 5
        + [jax.ShapeDtypeStruct((8, D), F32)],
        compiler_params=_cparams(("arbitrary",), 56 << 20), name="gates_bwd",
    )(attn, proj, proj, proj, proj, proj, da_mla, da_conv, proj, proj, proj, proj, da_conv, conv_w)


def _flash_bwd(q, k, v, do, lse, delta, tq, hb=2):
    S = q.shape[1]
    nq = S // tq
    scale = SM_SCALE
    tsub = min(256, tq)
    nsub = tq // tsub

    qi_tab, kj_tab = _causal_pairs(nq, key_major=True)
    npairs = nq * (nq + 1) // 2

    def body(qi_ref, kj_ref, q_ref, k_ref, v_ref, do_ref, lse_ref, dl_ref, dq_ref, dk_ref, dv_ref, dk_sc, dv_sc, dq_sc):
        p = pl.program_id(1)
        qi, kj = qi_ref[p], kj_ref[p]

        @pl.when(p == 0)
        def _():
            dq_sc[...] = jnp.zeros_like(dq_sc)

        @pl.when(qi == kj)
        def _():
            dk_sc[...] = jnp.zeros_like(dk_sc)
            dv_sc[...] = jnp.zeros_like(dv_sc)

        def step(diag):
            chains = [(h, u) for h in range(hb) for u in range(nsub)]

            def first_matmuls(h, u):
                sub = slice(u * tsub, (u + 1) * tsub)
                st = lax.dot_general(k_ref[h], q_ref[h, sub, :], NT, preferred_element_type=F32)
                dpt = lax.dot_general(v_ref[h], do_ref[sub, h * DV:(h + 1) * DV], NT, preferred_element_type=F32)
                return st, dpt

            nxt = first_matmuls(*chains[0])
            for ci, (h, u) in enumerate(chains):
                st, dpt = nxt
                if ci + 1 < len(chains):
                    nxt = first_matmuls(*chains[ci + 1])
                sub = slice(u * tsub, (u + 1) * tsub)
                pt = jnp.exp2(st - lse_ref[h, 0:1, sub])
                if diag:
                    r = lax.broadcasted_iota(jnp.int32, (tq, tsub), 0)
                    c = lax.broadcasted_iota(jnp.int32, (tq, tsub), 1) + u * tsub
                    pt = jnp.where(r <= c, pt, 0.0)
                dst = (pt * (dpt - dl_ref[h, 0:1, sub])).astype(BF16)
                dv_sc[h] += jnp.dot(pt.astype(BF16), do_ref[sub, h * DV:(h + 1) * DV], preferred_element_type=F32)
                dk_sc[h] += jnp.dot(dst, q_ref[h, sub, :], preferred_element_type=F32)
                rows = pl.ds(pl.multiple_of(qi * tq + u * tsub, tsub), tsub)
                dq_sc[h, rows, :] += lax.dot_general(dst, k_ref[h], TN, preferred_element_type=F32)

        @pl.when(qi > kj)
        def _():
            step(False)

        @pl.when(qi == kj)
        def _():
            step(True)

        @pl.when(qi == nq - 1)
        def _():
            dk_ref[...] = (dk_sc[...] * (1.0 / LOG2E)).astype(BF16)
            dv_ref[...] = dv_sc[...].astype(BF16)

        @pl.when(p == npairs - 1)
        def _():
            dq_ref[...] = (dq_sc[...] * scale).astype(BF16)

    return pl.pallas_call(
        body,
        grid_spec=pltpu.PrefetchScalarGridSpec(
            num_scalar_prefetch=2,
            grid=(H // hb, npairs),
            in_specs=[pl.BlockSpec((hb, tq, HW), lambda h, p, qi, kj: (h, qi[p], 0)),
                      pl.BlockSpec((hb, tq, HW), lambda h, p, qi, kj: (h, kj[p], 0)),
                      pl.BlockSpec((hb, tq, DV), lambda h, p, qi, kj: (h, kj[p], 0)),
                      pl.BlockSpec((tq, hb * DV), lambda h, p, qi, kj: (qi[p], h)),
                      pl.BlockSpec((hb, 8, tq), lambda h, p, qi, kj: (h, 0, qi[p])),
                      pl.BlockSpec((hb, 8, tq), lambda h, p, qi, kj: (h, 0, qi[p]))],
            out_specs=[pl.BlockSpec((hb, S, HW), lambda h, p, qi, kj: (h, 0, 0)),
                       pl.BlockSpec((hb, tq, HW), lambda h, p, qi, kj: (h, kj[p], 0)),
                       pl.BlockSpec((hb, tq, DV), lambda h, p, qi, kj: (h, kj[p], 0))],
            scratch_shapes=[pltpu.VMEM((hb, tq, HW), F32), pltpu.VMEM((hb, tq, DV), F32), pltpu.VMEM((hb, S, HW), F32)],
        ),
        out_shape=[jax.ShapeDtypeStruct((H, S, HW), BF16), jax.ShapeDtypeStruct((H, S, HW), BF16), jax.ShapeDtypeStruct((H, S, DV), BF16)],
        compiler_params=_cparams(("parallel", "arbitrary"), VMEM_CAP),
        name="flash_bwd",
    )(qi_tab, kj_tab, q, k, v, do, lse, delta)


def _rms_bwd(xf, g, dn_out):
    r = _rms_scale(xf)
    n = xf * r
    dn = dn_out * g
    dx = r * (dn - n * jnp.mean(dn * n, axis=-1, keepdims=True))
    return dx, jnp.sum(dn_out * n, axis=0, keepdims=True)


def _qkv_bwd(proj, pos, freq, g_qa, g_kva, w_q, w_kv, dq, dk, dv, segments, ts):
    S = proj.shape[0]
    nseg = len(segments)

    def body(qa_ref, ckv_ref, pos_ref, freq_ref, gq_ref, gk_ref, wq_ref, wkv_ref, dq_ref, dk_ref, dv_ref, *rest):
        seg_refs, (dqp_ref, dkvp_ref, dproj_ref, dgq_ref, dgk_ref) = rest[:nseg], rest[nseg:]
        for j, seg_ref in enumerate(seg_refs):
            dproj_ref[:, j * D:(j + 1) * D] = seg_ref[...]
        dqa_ref = dproj_ref.at[:, OFF_QA:OFF_QA + RQ]
        dckv_ref = dproj_ref.at[:, OFF_CKV:OFF_CKV + RKV]
        dkr_ref = dproj_ref.at[:, OFF_KR:OFF_KR + 128]
        i = pl.program_id(0)
        cos, sin, lane = _rope_tables(pos_ref, freq_ref, ts)
        sgn_sin = jnp.where(lane < 32, sin, -sin)
        live = lane < DR
        kr_sum = jnp.zeros((ts, 128), F32)
        for h in range(H):
            dqp_ref[:, h * HW:h * HW + DN] = dq_ref[h, :, 0:DN]
            t = dq_ref[h, :, DN:HW].astype(F32)
            dqp_ref[:, h * HW + DN:(h + 1) * HW] = jnp.where(live, t * cos + _rope_swap(t, lane) * sgn_sin, 0.0).astype(BF16)
            dkvp_ref[:, h * HW:h * HW + DN] = dk_ref[h, :, 0:DN]
            dkvp_ref[:, h * HW + DN:(h + 1) * HW] = dv_ref[h]
            kr_sum = kr_sum + dk_ref[h, :, DN:HW].astype(F32)
        dkr_ref[...] = jnp.where(live, kr_sum * cos + _rope_swap(kr_sum, lane) * sgn_sin, 0.0).astype(BF16)
        dqn = lax.dot_general(dqp_ref[...], wq_ref[...], NT, preferred_element_type=F32)
        dkvn = lax.dot_general(dkvp_ref[...], wkv_ref[...], NT, preferred_element_type=F32)
        dqa, dgq = _rms_bwd(_f32(qa_ref), gq_ref[...], dqn)
        dckv, dgk = _rms_bwd(_f32(ckv_ref), gk_ref[...], dkvn)
        dqa_ref[...] = dqa.astype(BF16)
        dckv_ref[...] = dckv.astype(BF16)

        @pl.when(i == 0)
        def _():
            dgq_ref[...] = jnp.zeros_like(dgq_ref)
            dgk_ref[...] = jnp.zeros_like(dgk_ref)

        dgq_ref[0:1, :] += dgq
        dgk_ref[0:1, :] += dgk

    rowb = lambda w, blk: pl.BlockSpec((ts, w), lambda i: (i, blk))
    full = lambda a: pl.BlockSpec(a.shape, lambda i: (0,) * a.ndim)
    heads = lambda w: pl.BlockSpec((H, ts, w), lambda i: (0, i, 0))
    return pl.pallas_call(
        body, grid=(S // ts,),
        in_specs=[rowb(RQ, OFF_QA // RQ), rowb(RKV, OFF_CKV // RKV), pl.BlockSpec((ts, 1), lambda i: (i, 0)), full(freq), full(g_qa), full(g_kva),
                  full(w_q), full(w_kv), heads(HW), heads(HW), heads(DV)] + [rowb(D, 0)] * nseg,
        out_specs=[rowb(H * HW, 0), rowb(H * HW, 0), rowb(NP, 0),
                   pl.BlockSpec((8, RQ), lambda i: (0, 0)), pl.BlockSpec((8, RKV), lambda i: (0, 0))],
        out_shape=[jax.ShapeDtypeStruct((S, H * HW), BF16), jax.ShapeDtypeStruct((S, H * HW), BF16), jax.ShapeDtypeStruct((S, NP), BF16),
                   jax.ShapeDtypeStruct((8, RQ), F32), jax.ShapeDtypeStruct((8, RKV), F32)],
        compiler_params=_cparams(("arbitrary",), 56 << 20), name="qkv_bwd",
    )(proj, proj, pos, freq, g_qa, g_kva, w_q, w_kv, dq, dk, dv, *segments)


def _prenorm_bwd(x, g, dh, dy, ts):
    S = x.shape[0]

    def body(x_ref, g_ref, dh_ref, dy_ref, gx_ref, dg_ref):
        dx, dg = _rms_bwd(x_ref[...], g_ref[...], dh_ref[...])
        gx_ref[...] = dy_ref[...] + dx

        @pl.when(pl.program_id(0) == 0)
        def _():
            dg_ref[...] = jnp.zeros_like(dg_ref)

        dg_ref[0:1, :] += dg

    row = pl.BlockSpec((ts, D), lambda i: (i, 0))
    return pl.pallas_call(
        body, grid=(S // ts,), in_specs=[row, pl.BlockSpec((1, D), lambda i: (0, 0)), row, row],
        out_specs=[row, pl.BlockSpec((8, D), lambda i: (0, 0))],
        out_shape=[jax.ShapeDtypeStruct((S, D), F32), jax.ShapeDtypeStruct((8, D), F32)],
        compiler_params=_cparams(("arbitrary",), 40 << 20), name="prenorm_bwd",
    )(x, g, dh, dy)


def _local_step(x, pos, target, g_pre, g_qa, g_kva, g_post, w_in, other_weights, reduce_grads=None):
    S = x.shape[0]
    ts = min(256, S)
    tq = min(512, S)
    tm = min(1024, S)
    mm = functools.partial(_matmul, tm=tm)
    freq = _inv_freq_tile()

    h, ht = _prenorm(x, g_pre, ts)
    if len(other_weights) == 2:
        gather, assemble = other_weights
        proj, gathered = _matmul(h, w_in, mode="nn", out_dtype=BF16, tm=1024, tn=1408, tk=D, name="mm_proj", exchange=gather)
        other_weights = assemble(*gathered)
    else:
        proj = _matmul(h, w_in, mode="nn", out_dtype=BF16, tm=1024, tn=1408, tk=D, name="mm_proj")
    w_q, w_kv, conv_w8, w_o_mla, w_o_conv, w_out = other_weights
    q, k, v, vt, qn, kvn = _qkv_prep(proj, pos, freq, g_qa, g_kva, w_q, w_kv, ts)
    attn, lse = _flash_fwd(q, k, vt, tq)
    a_mla, a_conv = _gates_fwd(proj, attn, conv_w8, ts)
    y_mla = mm(a_mla, w_o_mla, mode="nn", out_dtype=BF16, tn=D, tk=D, name="mm_y_mla")
    y_conv = mm(a_conv, w_o_conv, mode="nn", out_dtype=BF16, tn=D, tk=D, name="mm_y_conv")
    merged = _merge_fwd(proj, y_mla, y_conv, ts)
    out = mm(merged, w_out, mode="nn", out_dtype=F32, tn=D, tk=D, name="mm_out")
    dy, dout, dg_post, loss = _post_loss(out, x, target, g_post, ts)

    dmerged = mm(dout, w_out, mode="nt", out_dtype=BF16, tn=D, tk=D, name="mm_dmerged")
    dw_out = _matmul(merged, dout, mode="tn", out_dtype=F32, tm=1024, tn=1024, tk=2048, name="mm_dw_out")
    dy_mla, dy_conv, dg_mla, dg_conv = _merge_bwd(proj, y_mla, y_conv, dmerged, ts)
    da_mla = mm(dy_mla, w_o_mla, mode="nt", out_dtype=BF16, tn=D, tk=D, name="mm_da_mla")
    da_conv = mm(dy_conv, w_o_conv, mode="nt", out_dtype=BF16, tn=D, tk=D, name="mm_da_conv")
    dw_o_mla = _matmul(a_mla, dy_mla, mode="tn", out_dtype=F32, tm=1024, tn=1024, tk=2048, name="mm_dw_o_mla")
    dw_o_conv = _matmul(a_conv, dy_conv, mode="tn", out_dtype=F32, tm=1024, tn=1024, tk=2048, name="mm_dw_o_conv")
    dattn, delta, dz_mla, dc_in, db_gate, dc_gate, dz_conv, dconv_w = _gates_bwd(proj, attn, da_mla, da_conv, conv_w8, min(128, S))
    dq, dk, dv = _flash_bwd(q, k, v, dattn, lse, delta, tq)
    dqp, dkvp, dproj, dg_qa, dg_kva = _qkv_bwd(proj, pos, freq, g_qa, g_kva, w_q, w_kv, dq, dk, dv,
                                               [dz_mla, dc_in, db_gate, dc_gate, dz_conv, dg_mla, dg_conv], min(128, S))
    dw_q = _matmul(qn, dqp, mode="tn", out_dtype=F32, tm=RQ, tn=1024, tk=2048, name="mm_dw_q")
    dw_kv = _matmul(kvn, dkvp, mode="tn", out_dtype=F32, tm=RKV, tn=1024, tk=2048, name="mm_dw_kv")
    dw_in = _matmul(ht, dproj, mode="nn", out_dtype=F32, tm=1024, tn=1408, tk=2048, name="mm_dw_in")
    res = dict(dw_in=dw_in, dw_q=dw_q, dw_kv=dw_kv, dconv_w=dconv_w, dw_o_mla=dw_o_mla, dw_o_conv=dw_o_conv, dw_out=dw_out)
    if reduce_grads is None:
        dh = _matmul(dproj, w_in, mode="nt", out_dtype=F32, tm=1024, tn=D, tk=1408, name="mm_dh")
    else:
        exchange, finish = reduce_grads(res)
        dh, got = _matmul(dproj, w_in, mode="nt", out_dtype=F32, tm=1024, tn=D, tk=1408, name="mm_dh", exchange=exchange)
        res["reduced"] = finish(got)
    grad_x, dg_pre = _prenorm_bwd(x, g_pre, dh, dy, ts)
    res.update(loss=loss, grad_x=grad_x, dg_pre=dg_pre, dg_qa=dg_qa, dg_kva=dg_kva, dg_post=dg_post)
    return res


def _my_id():
    return lax.axis_index("x") * 4 + lax.axis_index("y") * 2 + lax.axis_index("c")


def _place():
    x, y, c = lax.axis_index("x"), lax.axis_index("y"), lax.axis_index("c")
    return (x, y, c), (x, y, 1 - c), [(1 - x, y), (x, 1 - y), (1 - x, 1 - y)]


def _slot(px, py, pc):
    return 4 * px + 2 * py + pc


def _all_gather(arrays, name):
    n = len(arrays)

    def body(*refs):
        ins, outs = refs[:n], refs[n:2 * n]
        send_sems, recv_sems, local_sems = refs[2 * n:]
        me, sib, chips = _place()
        c = me[2]

        def copy(a, k, block, to, src=None):
            rows = outs[a].at[_slot(*block)]
            return pltpu.make_async_remote_copy(src_ref=rows if src is None else src, dst_ref=rows, send_sem=send_sems.at[a, k],
                                                recv_sem=recv_sems.at[a, k], device_id=to, device_id_type=MESH)

        local = [pltpu.make_async_copy(ins[a], outs[a].at[_slot(*me)], local_sems.at[a]) for a in range(n)]
        for cp in local:
            cp.start()
        sends = []
        for a in range(n):
            sends.append(copy(a, 0, me, sib, src=ins[a]))
            sends += [copy(a, 1 + j, me, (*chip, c), src=ins[a]) for j, chip in enumerate(chips)]
        for cp in sends:
            cp.start()
        for a in range(n):
            for j, chip in enumerate(chips):
                copy(a, 1 + j, (*chip, c), me).wait_recv()
                fwd = copy(a, 4 + j, (*chip, c), sib)
                fwd.start()
                sends.append(fwd)
        for a in range(n):
            copy(a, 0, sib, me).wait_recv()
            for j, chip in enumerate(chips):
                copy(a, 4 + j, (*chip, 1 - c), me).wait_recv()
        for cp in sends:
            cp.wait_send()
        for cp in local:
            cp.wait()

    anyspec = pl.BlockSpec(memory_space=pl.ANY)
    return pl.pallas_call(
        body,
        in_specs=[anyspec] * n,
        out_specs=[anyspec] * n,
        out_shape=[jax.ShapeDtypeStruct((NDEV,) + a.shape, a.dtype) for a in arrays],
        scratch_shapes=[pltpu.SemaphoreType.DMA((n, NDEV - 1)), pltpu.SemaphoreType.DMA((n, NDEV - 1)), pltpu.SemaphoreType.DMA((n,))],
        name=name,
    )(*arrays)


def _all_gather_relayed(x, name):
    R, C = x.shape
    half = R // 2

    def body(x_ref, out_ref, send_sems, recv_sems, local_sem):
        (mx, my, c), sib, _ = _place()
        xn, yn, dg = (1 - mx, my, c), (mx, 1 - my, c), (1 - mx, 1 - my, c)

        def rows(block, h):
            return out_ref.at[_slot(*block), pl.ds(h * half, half), :]

        def copy(k, dst, to, src=None):
            return pltpu.make_async_remote_copy(src_ref=dst if src is None else src, dst_ref=dst, send_sem=send_sems.at[k],
                                                recv_sem=recv_sems.at[k], device_id=to, device_id_type=MESH)

        me = (mx, my, c)
        local = pltpu.make_async_copy(x_ref, out_ref.at[_slot(*me)], local_sem)
        local.start()
        sends = [copy(0, out_ref.at[_slot(*me)], sib, src=x_ref)]
        for h in range(2):
            sends.append(copy(1 + h, rows(me, h), xn, src=x_ref.at[pl.ds(h * half, half), :]))
            sends.append(copy(3 + h, rows(me, h), yn, src=x_ref.at[pl.ds(h * half, half), :]))
        for cp in sends:
            cp.start()
        copy(3, rows(yn, 0), me).wait_recv()
        relay_x = copy(5, rows(yn, 0), xn)
        relay_x.start()
        copy(2, rows(xn, 1), me).wait_recv()
        relay_y = copy(6, rows(xn, 1), yn)
        relay_y.start()
        sends += [relay_x, relay_y]
        copy(1, rows(xn, 0), me).wait_recv()
        copy(4, rows(yn, 1), me).wait_recv()
        copy(5, rows(dg, 0), me).wait_recv()
        copy(6, rows(dg, 1), me).wait_recv()
        for k, block in ((7, xn), (8, yn), (9, dg)):
            fwd = copy(k, out_ref.at[_slot(*block)], sib)
            fwd.start()
            sends.append(fwd)
        copy(0, out_ref.at[_slot(*sib)], me).wait_recv()
        for k, block in ((7, xn), (8, yn), (9, dg)):
            copy(k, out_ref.at[_slot(block[0], block[1], 1 - c)], me).wait_recv()
        for cp in sends:
            cp.wait_send()
        local.wait()

    anyspec = pl.BlockSpec(memory_space=pl.ANY)
    return pl.pallas_call(
        body, in_specs=[anyspec], out_specs=anyspec, out_shape=jax.ShapeDtypeStruct((NDEV, R, C), x.dtype),
        scratch_shapes=[pltpu.SemaphoreType.DMA((10,)), pltpu.SemaphoreType.DMA((10,)), pltpu.SemaphoreType.DMA],
        name=name,
    )(x)


def _slab_exchange(arrays, nslots, pick):
    n = len(arrays)

    def copies(ins, outs, sems):
        send_sems, recv_sems = sems
        return [pltpu.make_async_remote_copy(src_ref=ins[a].at[pick(k)[0]], dst_ref=outs[a].at[k], send_sem=send_sems.at[a, k],
                                             recv_sem=recv_sems.at[a, k], device_id=pick(k)[1], device_id_type=MESH)
                for a in range(n) for k in range(nslots)]

    def start(ins, outs, sems):
        for cp in copies(ins, outs, sems):
            cp.start()

    def finish(ins, outs, sems):
        cps = copies(ins, outs, sems)
        for cp in cps:
            cp.wait_recv()
        for cp in cps:
            cp.wait_send()

    return _Exchange(arrays, [jax.ShapeDtypeStruct((nslots,) + a.shape[1:], a.dtype) for a in arrays],
                     [pltpu.SemaphoreType.DMA((n, nslots)), pltpu.SemaphoreType.DMA((n, nslots))], start, finish)


def _direct_gather(arrays):
    n = len(arrays)

    def peer(d):
        p = (_my_id() + d) % NDEV
        return (p // 4, (p // 2) % 2, p % 2), p

    def copies(ins, outs, sems, receiving):
        send_sems, recv_sems, _ = sems
        slot = lambda d: peer(NDEV - d)[1] if receiving else _my_id()
        return [pltpu.make_async_remote_copy(src_ref=ins[a], dst_ref=outs[a].at[slot(d)], send_sem=send_sems.at[a, d - 1],
                                             recv_sem=recv_sems.at[a, d - 1], device_id=peer(d)[0], device_id_type=MESH)
                for d in range(1, NDEV) for a in range(n)]

    def local(ins, outs, sems):
        return [pltpu.make_async_copy(ins[a], outs[a].at[_my_id()], sems[2].at[a]) for a in range(n)]

    def start(ins, outs, sems):
        for cp in local(ins, outs, sems) + copies(ins, outs, sems, False):
            cp.start()

    def finish(ins, outs, sems):
        for cp in copies(ins, outs, sems, True):
            cp.wait_recv()
        for cp in copies(ins, outs, sems, False):
            cp.wait_send()
        for cp in local(ins, outs, sems):
            cp.wait()

    return _Exchange(arrays, [jax.ShapeDtypeStruct((NDEV,) + a.shape, a.dtype) for a in arrays],
                     [pltpu.SemaphoreType.DMA((n, NDEV - 1)), pltpu.SemaphoreType.DMA((n, NDEV - 1)), pltpu.SemaphoreType.DMA((n,))],
                     start, finish)


def _run_exchange(exchange, name):
    n = len(exchange.arrays)

    def body(*refs):
        ins, outs, sems = refs[:n], refs[n:2 * n], refs[2 * n:]
        exchange.start(ins, outs, sems)
        exchange.finish(ins, outs, sems)

    anyspec = pl.BlockSpec(memory_space=pl.ANY)
    return pl.pallas_call(body, in_specs=[anyspec] * n, out_specs=[anyspec] * n, out_shape=exchange.out_shapes,
                          scratch_shapes=exchange.sem_shapes, name=name)(*exchange.arrays)


def _slabs_bf16(x, rows_per_block, name):
    n, R, C = x.shape
    tr = min(rows_per_block, R)

    def body(x_ref, o_ref):
        o_ref[...] = x_ref[...].astype(BF16)

    blk = pl.BlockSpec((1, tr, C), lambda k, i: (k, i, 0))
    return pl.pallas_call(body, grid=(n, R // tr), in_specs=[blk], out_specs=blk, out_shape=jax.ShapeDtypeStruct(x.shape, BF16),
                          compiler_params=_cparams(("parallel", "parallel"), 32 << 20), name=name)(x)


def _to_sibling(k):
    me, sib, chips = _place()
    dest = sib if k == 0 else (*chips[k - 1], sib[2])
    return _slot(*dest), sib


def _to_chips(k):
    me, sib, chips = _place()
    return k, (*chips[k], me[2])


def _chip_sums(own, got, slots, rows_per_block, name):
    _, R, C = own.shape
    tr = min(rows_per_block, R)

    def body(slots_ref, own_ref, got_ref, o_ref):
        o_ref[0] = (own_ref[0].astype(F32) + got_ref[0].astype(F32)).astype(BF16)

    return pl.pallas_call(
        body,
        grid_spec=pltpu.PrefetchScalarGridSpec(
            num_scalar_prefetch=1, grid=(3, R // tr),
            in_specs=[pl.BlockSpec((1, tr, C), lambda j, i, s: (s[1 + j], i, 0)), pl.BlockSpec((1, tr, C), lambda j, i, s: (1 + j, i, 0))],
            out_specs=pl.BlockSpec((1, tr, C), lambda j, i, s: (j, i, 0))),
        out_shape=jax.ShapeDtypeStruct((3, R, C), BF16),
        compiler_params=_cparams(("parallel", "parallel"), 32 << 20), name=name,
    )(slots, own, got)


def _adamw_math(w, g, m, v):
    m = ADAM_B1 * m + (1.0 - ADAM_B1) * g
    v = ADAM_B2 * v + (1.0 - ADAM_B2) * (g * g)
    m_hat = m / (1.0 - ADAM_B1 ** ADAM_STEP)
    v_hat = v / (1.0 - ADAM_B2 ** ADAM_STEP)
    delta = -ADAM_LR * (m_hat / (jnp.sqrt(v_hat) + ADAM_EPS) + ADAM_WD * w)
    return delta, m, v


def _reduce_adamw(own, got1, got2, slots, w, m, v, rows_per_block, name):
    _, R, C = own.shape
    tr = min(rows_per_block, R)
    assert R % tr == 0

    def body(slots_ref, own_ref, g1_ref, g2_ref, w_ref, m_ref, v_ref, g_ref, d_ref, nm_ref, nv_ref):
        g = own_ref[0] + g1_ref[0].astype(F32)
        for j in range(3):
            g = g + g2_ref[j].astype(F32)
        g_ref[...] = g
        d, nm, nv = _adamw_math(w_ref[...], g, m_ref[...], v_ref[...])
        d_ref[...] = d
        nm_ref[...] = nm
        nv_ref[...] = nv

    blk = pl.BlockSpec((tr, C), lambda i, s: (i, 0))
    return pl.pallas_call(
        body,
        grid_spec=pltpu.PrefetchScalarGridSpec(
            num_scalar_prefetch=1, grid=(R // tr,),
            in_specs=[pl.BlockSpec((1, tr, C), lambda i, s: (s[0], i, 0)), pl.BlockSpec((1, tr, C), lambda i, s: (0, i, 0)),
                      pl.BlockSpec((3, tr, C), lambda i, s: (0, i, 0)), blk, blk, blk],
            out_specs=[blk] * 4),
        out_shape=[jax.ShapeDtypeStruct((R, C), F32)] * 4,
        compiler_params=_cparams(("parallel",), 48 << 20), name=name,
    )(slots, own, got1, got2, w, m, v)


def _reduce_adamw_t(own, got1, got2, slots, wt, mt, vt, rows_per_block, name):
    R, C = own.shape
    tr = min(rows_per_block, R)
    assert R % tr == 0

    def body(slots_ref, own_ref, g1_ref, g2_ref, w_ref, m_ref, v_ref, g_ref, d_ref, nm_ref, nv_ref):
        g = own_ref[...] + g1_ref[0].astype(F32)
        for j in range(3):
            g = g + g2_ref[j].astype(F32)
        g = jnp.transpose(g)
        g_ref[...] = g
        d, nm, nv = _adamw_math(w_ref[...], g, m_ref[...], v_ref[...])
        d_ref[...] = d
        nm_ref[...] = nm
        nv_ref[...] = nv

    blk = pl.BlockSpec((C, tr), lambda i, s: (0, i))
    return pl.pallas_call(
        body,
        grid_spec=pltpu.PrefetchScalarGridSpec(
            num_scalar_prefetch=1, grid=(R // tr,),
            in_specs=[pl.BlockSpec((tr, C), lambda i, s: (i, 0)), pl.BlockSpec((1, tr, C), lambda i, s: (0, i, 0)),
                      pl.BlockSpec((3, tr, C), lambda i, s: (0, i, 0)), blk, blk, blk],
            out_specs=[blk] * 4),
        out_shape=[jax.ShapeDtypeStruct((C, R), F32)] * 4,
        compiler_params=_cparams(("parallel",), 48 << 20), name=name,
    )(slots, own, got1, got2, wt, mt, vt)


def _sum_adamw(parts, w, m, v, rows_per_block, name):
    n, R, C = parts.shape
    tr = min(rows_per_block, R)
    assert R % tr == 0

    def body(p_ref, w_ref, m_ref, v_ref, g_ref, d_ref, nm_ref, nv_ref):
        g = p_ref[0]
        for j in range(1, n):
            g = g + p_ref[j]
        g_ref[...] = g
        d, nm, nv = _adamw_math(w_ref[...], g, m_ref[...], v_ref[...])
        d_ref[...] = d
        nm_ref[...] = nm
        nv_ref[...] = nv

    blk = pl.BlockSpec((tr, C), lambda i: (i, 0))
    return pl.pallas_call(
        body, grid=(R // tr,),
        in_specs=[pl.BlockSpec((n, tr, C), lambda i: (0, i, 0)), blk, blk, blk],
        out_specs=[blk] * 4,
        out_shape=[jax.ShapeDtypeStruct((R, C), F32)] * 4,
        compiler_params=_cparams(("parallel",), 48 << 20), name=name,
    )(parts, w, m, v)


def kernel(x, positions, pre_norm_g, w_in, q_a_norm_g, w_q_b, kv_a_norm_g, w_kv_b, conv_w, w_o_mla, w_o_conv, w_out, post_norm_g, loss_target, m_pre_norm_g, m_w_in, m_q_a_norm_g, m_w_q_b, m_kv_a_norm_g, m_w_kv_b, m_conv_w, m_w_o_mla, m_w_o_conv, m_w_out, m_post_norm_g, v_pre_norm_g, v_w_in, v_q_a_norm_g, v_w_q_b, v_kv_a_norm_g, v_w_kv_b, v_conv_w, v_w_o_mla, v_w_o_conv, v_w_out, v_post_norm_g):
    S = x.shape[1]
    conv_pad = jnp.zeros((8, 256), F32).at[0:3, :].set(conv_w)
    g_in = _all_gather_relayed(w_in.astype(BF16), "all_gather_w_in")
    w_in_f = _assemble_w_in(g_in)
    gather_rest = _direct_gather([w_q_b.astype(BF16), w_kv_b.astype(BF16), conv_pad, w_o_mla.astype(BF16), w_o_conv.astype(BF16),
                                  w_out.astype(BF16)])

    def assemble_rest(g_q, g_kv, g_cw, g_om, g_oc, g_out):
        return (_assemble_w_q(g_q), _concat_cols(g_kv, BF16, "assemble_w_kv"), _concat_cols(g_cw, F32, "assemble_conv_w"),
                g_om.reshape(D, D), g_oc.reshape(D, D), g_out.reshape(D, D))

    gnames = ["w_in", "w_q", "w_kv", "conv_w", "w_o_mla", "w_o_conv", "w_out"]
    (mx, my, mc), _, chips = _place()
    slots = jnp.stack([_slot(mx, my, mc)] + [_slot(cx, cy, mc) for cx, cy in chips]).astype(jnp.int32)

    def reduce_grads(r):
        own_in, slabs_in = _split_dw_in(r["dw_in"], slots)
        own = [own_in, _split_dw_q(r["dw_q"]), _split_cols(r["dw_kv"], NDEV, "split_dw_kv"),
               _split_cols(r["dconv_w"], NDEV, "split_dconv_w"), r["dw_o_mla"].reshape(NDEV, D // NDEV, D),
               r["dw_o_conv"].reshape(NDEV, D // NDEV, D), r["dw_out"].reshape(NDEV, D // NDEV, D)]
        slabs = [slabs_in] + [_slabs_bf16(g, 128, "bf16_" + nm) for g, nm in zip(own[1:], gnames[1:])]
        to_sibling = _slab_exchange(slabs, 4, _to_sibling)
        got1 = _run_exchange(to_sibling, "grads_to_sibling")
        sums = [_chip_sums(o, g1, slots, 128, "chip_sum_" + nm) for o, g1, nm in zip([slabs_in] + own[1:], got1, gnames)]
        return _slab_exchange(sums, 3, _to_chips), lambda got2: (own, got1, got2)

    row2 = lambda a: a.reshape(1, -1)
    r = _local_step(x[0], positions.reshape(S, 1), loss_target[0], row2(pre_norm_g), row2(q_a_norm_g), row2(kv_a_norm_g),
                    row2(post_norm_g), w_in_f, (gather_rest, assemble_rest), reduce_grads)
    own, got1, got2 = r["reduced"]
    small = jnp.concatenate([r["dg_pre"][0:1], r["dg_post"][0:1], jnp.pad(r["dg_qa"][0:1], ((0, 0), (0, D - RQ))),
                             jnp.pad(r["dg_kva"][0:1], ((0, 0), (0, D - RKV))), jnp.pad(r["loss"][0:1], ((0, 0), (0, D - 128))),
                             jnp.zeros((3, D), F32)], axis=0)
    p_small, = _all_gather([small], "all_gather_small")
    pad8 = lambda a: jnp.zeros((8, 256), F32).at[0:3, :].set(a)
    params = [(w_in, m_w_in, v_w_in), (w_q_b, m_w_q_b, v_w_q_b), (w_kv_b, m_w_kv_b, v_w_kv_b), (conv_pad, pad8(m_conv_w), pad8(v_conv_w)),
              (w_o_mla, m_w_o_mla, v_w_o_mla), (w_o_conv, m_w_o_conv, v_w_o_conv), (w_out, m_w_out, v_w_out)]
    o_q, o_kv, o_cw, o_om, o_oc, o_out = [
        _reduce_adamw(o, g1, g2, slots, w, m, v, 128, "adamw_" + nm)
        for o, g1, g2, (w, m, v), nm in list(zip(own, got1, got2, params, gnames))[1:]]
    o_cw = [a[0:3] for a in o_cw]
    o_in = [a.T for a in _reduce_adamw_t(own[0], got1[0], got2[0], slots, w_in.T, m_w_in.T, v_w_in.T, 128, "adamw_w_in")]
    padv = lambda a: jnp.pad(a.reshape(1, -1), ((0, 0), (0, D - a.shape[0])))
    stack = lambda pre, post, qa, kva: jnp.concatenate([row2(pre), row2(post), padv(qa), padv(kva), jnp.zeros((4, D), F32)], axis=0)
    o_g = _sum_adamw(p_small, stack(pre_norm_g, post_norm_g, q_a_norm_g, kv_a_norm_g),
                     stack(m_pre_norm_g, m_post_norm_g, m_q_a_norm_g, m_kv_a_norm_g),
                     stack(v_pre_norm_g, v_post_norm_g, v_q_a_norm_g, v_kv_a_norm_g), 8, "adamw_gains")
    loss = o_g[0][4, 0]
    outs = {}
    for idx, kind in enumerate(("grad", "delta", "new_m", "new_v")):
        o = o_g[idx]
        outs[kind] = dict(pre_norm_g=o[0], w_in=o_in[idx], q_a_norm_g=o[2, 0:RQ], w_q_b=o_q[idx], kv_a_norm_g=o[3, 0:RKV],
                          w_kv_b=o_kv[idx], conv_w=o_cw[idx], w_o_mla=o_om[idx], w_o_conv=o_oc[idx], w_out=o_out[idx], post_norm_g=o[1])
    names = ["pre_norm_g", "w_in", "q_a_norm_g", "w_q_b", "kv_a_norm_g", "w_kv_b", "conv_w", "w_o_mla", "w_o_conv", "w_out", "post_norm_g"]
    return (loss, r["grad_x"][None], *[outs["grad"][n] for n in names], *[outs["delta"][n] for n in names],
            *[outs["new_m"][n] for n in names], *[outs["new_v"][n] for n in names])
```

```python
import functools
import math

import jax
import jax.numpy as jnp
from jax import lax
from jax.experimental import pallas as pl
from jax.experimental.pallas import tpu as pltpu

F32 = jnp.float32
BF16 = jnp.bfloat16

NDEV = 8
D = 2048
H = 16
DN = 128
DR = 64
DV = 128
RQ = 512
RKV = 512
HW = 256
ROPE_THETA = 10000.0
RMS_EPS = 1e-6
N_IN = 15424
SHARD_IN = N_IN // NDEV
SMALL = RQ + RKV + DR
NP = 7 * D + RQ + RKV + 128
SEG = dict(z_mla=0, c_in=1, b_gate=2, c_gate=3, z_conv=4, g_mla=5, g_conv=6)
OFF_QA = 7 * D
OFF_CKV = OFF_QA + RQ
OFF_KR = OFF_CKV + RKV
EXT = 2176
VMEM_CAP = 56 * 1024 * 1024

ADAM_LR = 0.001
ADAM_B1 = 0.9
ADAM_B2 = 0.999
ADAM_EPS = 1e-08
ADAM_WD = 0.01
ADAM_STEP = 10

LOG2E = math.log2(math.e)
SM_SCALE = 1.0 / math.sqrt(DN + DR)
QSCALE = SM_SCALE * LOG2E
NN = (((1,), (0,)), ((), ()))
NT = (((1,), (1,)), ((), ()))
TN = (((0,), (0,)), ((), ()))
MESH = pl.DeviceIdType.MESH


def _cparams(sem, vmem_bytes):
    return pltpu.CompilerParams(dimension_semantics=sem, vmem_limit_bytes=int(min(VMEM_CAP, max(vmem_bytes, 16 << 20))))


def _nbytes(shape, dtype):
    return math.prod(shape) * jnp.dtype(dtype).itemsize


class _Exchange:
    def __init__(self, arrays, out_shapes, sem_shapes, start, finish):
        self.arrays, self.out_shapes, self.sem_shapes, self.start, self.finish = arrays, out_shapes, sem_shapes, start, finish
        self.n_sems = len(sem_shapes)


def _matmul(a, b, *, mode, out_dtype, tm, tn, tk, name, m_outer=False, exchange=None):
    if mode == "nn":
        (M, K), (K2, N) = a.shape, b.shape
    elif mode == "nt":
        (M, K), (N, K2) = a.shape, b.shape
    else:
        (K, M), (K2, N) = a.shape, b.shape
    assert K == K2, (a.shape, b.shape, mode)
    tm, tn, tk = min(tm, M), min(tn, N), min(tk, K)
    assert M % tm == 0 and N % tn == 0 and K % tk == 0, (M, N, K, tm, tn, tk)
    ni, nj, nk = M // tm, N // tn, K // tk
    dims = dict(nn=NN, nt=NT, tn=TN)[mode]

    if m_outer:
        grid = (ni, nj, nk)
        ij = lambda g0, g1: (g0, g1)
    else:
        grid = (nj, ni, nk)
        ij = lambda g0, g1: (g1, g0)

    if mode == "tn":
        a_spec = pl.BlockSpec((tk, tm), lambda g0, g1, k: (k, ij(g0, g1)[0]))
        a_tile = (tk, tm)
    else:
        a_spec = pl.BlockSpec((tm, tk), lambda g0, g1, k: (ij(g0, g1)[0], k))
        a_tile = (tm, tk)
    if mode == "nt":
        b_spec = pl.BlockSpec((tn, tk), lambda g0, g1, k: (ij(g0, g1)[1], k))
    else:
        b_spec = pl.BlockSpec((tk, tn), lambda g0, g1, k: (k, ij(g0, g1)[1]))
    o_spec = pl.BlockSpec((tm, tn), lambda g0, g1, k: ij(g0, g1))

    n_in = len(exchange.arrays) if exchange else 0

    def body(a_ref, b_ref, *refs):
        x_in, o_ref, x_out, scratch = refs[:n_in], refs[n_in], refs[n_in + 1:2 * n_in + 1], refs[2 * n_in + 1:]
        if exchange:
            sems = scratch[len(scratch) - exchange.n_sems:]
            step = (pl.program_id(0) * grid[1] + pl.program_id(1)) * grid[2] + pl.program_id(2)

            @pl.when(step == 0)
            def _():
                exchange.start(x_in, x_out, sems)

        prod = lax.dot_general(a_ref[...], b_ref[...], dims, preferred_element_type=F32)
        if nk == 1:
            o_ref[...] = prod.astype(o_ref.dtype)
        else:
            acc_ref = scratch[0]
            k = pl.program_id(2)

            @pl.when(k == 0)
            def _():
                acc_ref[...] = prod

            @pl.when((k > 0) & (k < nk - 1))
            def _():
                acc_ref[...] += prod

            @pl.when(k == nk - 1)
            def _():
                o_ref[...] = (acc_ref[...] + prod).astype(o_ref.dtype)

        if exchange:
            @pl.when(step == grid[0] * grid[1] * grid[2] - 1)
            def _():
                exchange.finish(x_in, x_out, sems)

    vmem = 2 * (_nbytes(a_tile, a.dtype) + _nbytes((tk, tn), b.dtype) + _nbytes((tm, tn), out_dtype)) + 2 * _nbytes((tm, tn), F32)
    anyspec = pl.BlockSpec(memory_space=pl.ANY)
    outs = pl.pallas_call(
        body,
        grid=grid,
        in_specs=[a_spec, b_spec] + [anyspec] * n_in,
        out_specs=[o_spec] + [anyspec] * n_in,
        out_shape=[jax.ShapeDtypeStruct((M, N), out_dtype)] + (exchange.out_shapes if exchange else []),
        scratch_shapes=([] if nk == 1 else [pltpu.VMEM((tm, tn), F32)]) + (exchange.sem_shapes if exchange else []),
        compiler_params=_cparams(("arbitrary",) * 3 if exchange else ("parallel", "parallel", "arbitrary"), vmem + (8 << 20)),
        name=name,
    )(a, b, *(exchange.arrays if exchange else []))
    return (outs[0], outs[1:]) if exchange else outs[0]


def _in_dest(k):
    return SHARD_IN * k - SMALL


def _assemble_w_in(g):
    R = 128

    def body(g_ref, o_ref, ext, acc):
        ext[...] = jnp.zeros_like(ext)
        acc[...] = jnp.zeros_like(acc)
        lane = lax.broadcasted_iota(jnp.int32, (R, EXT), 1)
        ext[:, 0:SHARD_IN] = g_ref[0].astype(F32)
        v = ext[...]
        acc[:, OFF_QA:NP] = jnp.where(lane[:, 0:NP - OFF_QA] < SMALL, v[:, 0:NP - OFF_QA], 0.0)
        w = v[:, 1024:2048]
        w = pltpu.roll(w, 1024 - 64, 1)
        acc[:, 0:1024] = jnp.where(lane[:, 0:1024] < SHARD_IN - SMALL, w, 0.0)
        for k in range(1, NDEV):
            ext[:, 0:SHARD_IN] = g_ref[k].astype(F32)
            dest = _in_dest(k)
            t, o = dest // 128, dest % 128
            width = -(-(o + SHARD_IN) // 128) * 128
            v = pltpu.roll(ext[...], o, 1)[:, 0:width]
            acc[:, 128 * t:128 * t + width] += v
        o_ref[...] = acc[...].astype(BF16)

    return pl.pallas_call(
        body,
        grid=(D // R,),
        in_specs=[pl.BlockSpec((NDEV, R, SHARD_IN), lambda i: (0, i, 0))],
        out_specs=pl.BlockSpec((R, NP), lambda i: (i, 0)),
        out_shape=jax.ShapeDtypeStruct((D, NP), BF16),
        scratch_shapes=[pltpu.VMEM((R, EXT), F32), pltpu.VMEM((R, NP), F32)],
        compiler_params=_cparams(("parallel",), 40 << 20),
        name="assemble_w_in",
    )(g)


def _split_dw_in(dw, slots):
    R = 128

    def body(slots_ref, dw_ref, o_ref, ob_ref):
        def put(k, cols, v):
            ob_ref[k, :, cols] = v.astype(BF16)

            @pl.when(slots_ref[0] == k)
            def _():
                o_ref[:, cols] = v

        lane = lax.broadcasted_iota(jnp.int32, (R, 1024), 1)
        put(0, slice(0, 1024), dw_ref[:, OFF_QA:OFF_QA + 1024])
        tail = dw_ref[:, OFF_QA + 1024:NP]
        tail = jnp.concatenate([tail, jnp.zeros((R, 1024 - 128), F32)], axis=1)
        head = dw_ref[:, 0:1024]
        mixed = jnp.where(lane < 64, tail, pltpu.roll(head, 64, 1))
        put(0, slice(1024, SHARD_IN), mixed[:, 0:SHARD_IN - 1024])
        for k in range(1, NDEV):
            dest = _in_dest(k)
            t, o = dest // 128, dest % 128
            width = -(-(o + SHARD_IN) // 128) * 128
            v = dw_ref[:, 128 * t:128 * t + width]
            v = pltpu.roll(v, width - o, 1)
            put(k, slice(0, SHARD_IN), v[:, 0:SHARD_IN])

    return pl.pallas_call(
        body,
        grid_spec=pltpu.PrefetchScalarGridSpec(
            num_scalar_prefetch=1, grid=(D // R,),
            in_specs=[pl.BlockSpec((R, NP), lambda i, s: (i, 0))],
            out_specs=[pl.BlockSpec((R, SHARD_IN), lambda i, s: (i, 0)), pl.BlockSpec((NDEV, R, SHARD_IN), lambda i, s: (0, i, 0))]),
        out_shape=[jax.ShapeDtypeStruct((D, SHARD_IN), F32), jax.ShapeDtypeStruct((NDEV, D, SHARD_IN), BF16)],
        compiler_params=_cparams(("parallel",), 48 << 20),
        name="split_dw_in",
    )(slots, dw)


def _assemble_w_q(g):
    def body(g_ref, o_ref):
        lane = lax.broadcasted_iota(jnp.int32, (RQ, 128), 1)
        lo = lane < 64
        for k in range(NDEV):
            t0 = g_ref[k, :, 0:128].astype(F32)
            t1 = g_ref[k, :, 128:256].astype(F32)
            t2 = g_ref[k, :, 256:384].astype(F32)
            base = 2 * k * HW
            o_ref[:, base:base + 128] = t0.astype(BF16)
            o_ref[:, base + 128:base + 256] = jnp.where(lo, t1, 0.0).astype(BF16)
            o_ref[:, base + 256:base + 384] = pltpu.roll(jnp.where(lo, t2, t1), 64, 1).astype(BF16)
            o_ref[:, base + 384:base + 512] = jnp.where(lo, pltpu.roll(t2, 64, 1), 0.0).astype(BF16)

    return pl.pallas_call(
        body,
        out_shape=jax.ShapeDtypeStruct((RQ, H * HW), BF16),
        compiler_params=_cparams(None, 32 << 20),
        name="assemble_w_q",
    )(g)


def _split_dw_q(dw):
    def body(dw_ref, o_ref):
        lane = lax.broadcasted_iota(jnp.int32, (RQ, 128), 1)
        lo = lane < 64
        for k in range(NDEV):
            base = 2 * k * HW
            a = dw_ref[:, base:base + 128]
            b = dw_ref[:, base + 128:base + 256]
            c = pltpu.roll(dw_ref[:, base + 256:base + 384], 64, 1)
            d = pltpu.roll(dw_ref[:, base + 384:base + 512], 64, 1)
            o_ref[k, :, 0:128] = a
            o_ref[k, :, 128:256] = jnp.where(lo, b, c)
            o_ref[k, :, 256:384] = jnp.where(lo, c, d)

    return pl.pallas_call(
        body,
        out_shape=jax.ShapeDtypeStruct((NDEV, RQ, 384), F32),
        compiler_params=_cparams(None, 32 << 20),
        name="split_dw_q",
    )(dw)


def _concat_cols(g, dtype, name):
    n, R, C = g.shape

    def body(g_ref, o_ref):
        for k in range(n):
            o_ref[:, k * C:(k + 1) * C] = g_ref[k].astype(dtype)

    return pl.pallas_call(body, out_shape=jax.ShapeDtypeStruct((R, n * C), dtype), compiler_params=_cparams(None, 32 << 20), name=name)(g)


def _split_cols(x, n, name):
    R, NC = x.shape
    C = NC // n

    def body(x_ref, o_ref):
        for k in range(n):
            o_ref[k] = x_ref[:, k * C:(k + 1) * C]

    return pl.pallas_call(body, out_shape=jax.ShapeDtypeStruct((n, R, C), x.dtype), compiler_params=_cparams(None, 32 << 20), name=name)(x)


def _rms_scale(xf):
    return lax.rsqrt(jnp.mean(xf * xf, axis=-1, keepdims=True) + RMS_EPS)


def _prenorm(x, g, ts):
    S = x.shape[0]

    def body(x_ref, g_ref, h_ref, ht_ref):
        xf = x_ref[...]
        h = xf * _rms_scale(xf) * g_ref[...]
        h_ref[...] = h.astype(BF16)
        ht_ref[...] = jnp.transpose(h).astype(BF16)

    return pl.pallas_call(
        body,
        grid=(S // ts,),
        in_specs=[pl.BlockSpec((ts, D), lambda i: (i, 0)), pl.BlockSpec((1, D), lambda i: (0, 0))],
        out_specs=[pl.BlockSpec((ts, D), lambda i: (i, 0)), pl.BlockSpec((D, ts), lambda i: (0, i))],
        out_shape=[jax.ShapeDtypeStruct((S, D), BF16), jax.ShapeDtypeStruct((D, S), BF16)],
        compiler_params=_cparams(("parallel",), 32 << 20),
        name="prenorm",
    )(x, g)


def _inv_freq_tile():
    inv_freq = ROPE_THETA ** (-jnp.arange(0, DR, 2, dtype=F32) / DR)
    return jnp.tile(inv_freq, 4).reshape(1, 128)


def _rope_tables(pos_ref, freq_ref, ts):
    lane = lax.broadcasted_iota(jnp.int32, (ts, 128), 1)
    ang = pos_ref[...].astype(F32) * freq_ref[...]
    return jnp.cos(ang), jnp.sin(ang), lane


def _rope_swap(t, lane):
    return jnp.where(lane < 32, pltpu.roll(t, 96, 1), pltpu.roll(t, 32, 1))


def _qkv_prep(proj, pos, freq, g_qa, g_kva, w_q, w_kv, ts):
    S = proj.shape[0]

    def body(qa_ref, ckv_ref, kr_ref, pos_ref, freq_ref, gq_ref, gk_ref, wq_ref, wkv_ref, q_ref, k_ref, v_ref, vt_ref, qn_ref, kvn_ref):
        qa = _f32(qa_ref)
        qn = (qa * _rms_scale(qa) * gq_ref[...]).astype(BF16)
        ckv = _f32(ckv_ref)
        kvn = (ckv * _rms_scale(ckv) * gk_ref[...]).astype(BF16)
        qn_ref[...] = qn
        kvn_ref[...] = kvn
        cos, sin, lane = _rope_tables(pos_ref, freq_ref, ts)
        sgn_sin = jnp.where(lane < 32, -sin, sin)
        live = lane < DR
        kr = _f32(kr_ref)
        kr = jnp.where(live, kr * cos + _rope_swap(kr, lane) * sgn_sin, 0.0).astype(BF16)
        qf = jnp.dot(qn, wq_ref[...], preferred_element_type=F32)
        kvf = jnp.dot(kvn, wkv_ref[...], preferred_element_type=F32)
        for h in range(H):
            q_ref[h, :, 0:DN] = (qf[:, h * HW:h * HW + DN] * QSCALE).astype(BF16)
            t = qf[:, h * HW + DN:(h + 1) * HW]
            q_ref[h, :, DN:HW] = jnp.where(live, (t * cos + _rope_swap(t, lane) * sgn_sin) * QSCALE, 0.0).astype(BF16)
            k_ref[h, :, 0:DN] = kvf[:, h * HW:h * HW + DN].astype(BF16)
            k_ref[h, :, DN:HW] = kr
            vh = kvf[:, h * HW + DN:(h + 1) * HW]
            v_ref[h] = vh.astype(BF16)
            vt_ref[h] = jnp.transpose(vh).astype(BF16)

    row = lambda w, blk: pl.BlockSpec((ts, w), lambda i: (i, blk))
    full = lambda a: pl.BlockSpec(a.shape, lambda i: (0,) * a.ndim)
    return pl.pallas_call(
        body,
        grid=(S // ts,),
        in_specs=[row(RQ, OFF_QA // RQ), row(RKV, OFF_CKV // RKV), row(128, OFF_KR // 128),
                  pl.BlockSpec((ts, 1), lambda i: (i, 0)), full(freq), full(g_qa), full(g_kva), full(w_q), full(w_kv)],
        out_specs=[pl.BlockSpec((H, ts, HW), lambda i: (0, i, 0)), pl.BlockSpec((H, ts, HW), lambda i: (0, i, 0)),
                   pl.BlockSpec((H, ts, DV), lambda i: (0, i, 0)), pl.BlockSpec((H, DV, ts), lambda i: (0, 0, i)),
                   pl.BlockSpec((ts, RQ), lambda i: (i, 0)), pl.BlockSpec((ts, RKV), lambda i: (i, 0))],
        out_shape=[jax.ShapeDtypeStruct((H, S, HW), BF16), jax.ShapeDtypeStruct((H, S, HW), BF16),
                   jax.ShapeDtypeStruct((H, S, DV), BF16), jax.ShapeDtypeStruct((H, DV, S), BF16),
                   jax.ShapeDtypeStruct((S, RQ), BF16), jax.ShapeDtypeStruct((S, RKV), BF16)],
        compiler_params=_cparams(("parallel",), 48 << 20),
        name="qkv_prep",
    )(proj, proj, proj, pos, freq, g_qa, g_kva, w_q, w_kv)


def _col_to_row8(col, n):
    return jnp.transpose(jnp.broadcast_to(col, (n, 128)))[0:8, :]


def _causal_pairs(n, key_major):
    if key_major:
        pairs = [(i, j) for j in range(n) for i in range(j, n)]
    else:
        pairs = [(i, j) for i in range(n) for j in range(i + 1)]
    return jnp.array([p[0] for p in pairs], jnp.int32), jnp.array([p[1] for p in pairs], jnp.int32)


def _flash_fwd(q, k, vt, tq, hb=8):
    S = q.shape[1]
    nq = S // tq
    tsub = tq
    nsub = tq // tsub

    qi_tab, kj_tab = _causal_pairs(nq, key_major=False)

    def body(qi_ref, kj_ref, q_ref, k_ref, vt_ref, o_ref, lse_ref, m_sc, l_sc, acc_sc):
        p = pl.program_id(1)
        qi, kj = qi_ref[p], kj_ref[p]

        @pl.when(kj == 0)
        def _():
            m_sc[...] = jnp.full_like(m_sc, -jnp.inf)
            l_sc[...] = jnp.zeros_like(l_sc)
            acc_sc[...] = jnp.zeros_like(acc_sc)

        def step(diag):
            chains = [(h, u) for h in range(hb) for u in range(nsub)]

            def scores(h, u):
                return lax.dot_general(k_ref[h], q_ref[h, u * tsub:(u + 1) * tsub, :], NT, preferred_element_type=F32)

            st_next = scores(*chains[0])
            for ci, (h, u) in enumerate(chains):
                st = st_next
                if ci + 1 < len(chains):
                    st_next = scores(*chains[ci + 1])
                cols = slice(u * tsub, (u + 1) * tsub)
                if diag:
                    r = lax.broadcasted_iota(jnp.int32, (tq, tsub), 0)
                    c = lax.broadcasted_iota(jnp.int32, (tq, tsub), 1) + u * tsub
                    st = jnp.where(r <= c, st, -jnp.inf)
                m_prev = m_sc[h, 0:1, cols]
                m_new = jnp.maximum(m_prev, jnp.max(st, axis=0, keepdims=True))
                alpha = jnp.exp2(m_prev - m_new)
                pt = jnp.exp2(st - m_new)
                l_sc[h, :, cols] = jnp.broadcast_to(alpha * l_sc[h, 0:1, cols] + jnp.sum(pt, axis=0, keepdims=True), (8, tsub))
                m_sc[h, :, cols] = jnp.broadcast_to(m_new, (8, tsub))
                acc_sc[h, :, cols] = alpha * acc_sc[h, :, cols] + jnp.dot(vt_ref[h], pt.astype(BF16), preferred_element_type=F32)

        @pl.when(kj < qi)
        def _():
            step(False)

        @pl.when(kj == qi)
        def _():
            step(True)
            for h in range(hb):
                l = l_sc[h, 0:1, :]
                o_ref[:, h * DV:(h + 1) * DV] = jnp.transpose(acc_sc[h] / l).astype(BF16)
                lse_ref[h] = m_sc[h] + jnp.log2(l_sc[h])

    return pl.pallas_call(
        body,
        grid_spec=pltpu.PrefetchScalarGridSpec(
            num_scalar_prefetch=2,
            grid=(H // hb, len(qi_tab)),
            in_specs=[pl.BlockSpec((hb, tq, HW), lambda h, p, qi, kj: (h, qi[p], 0)),
                      pl.BlockSpec((hb, tq, HW), lambda h, p, qi, kj: (h, kj[p], 0)),
                      pl.BlockSpec((hb, DV, tq), lambda h, p, qi, kj: (h, 0, kj[p]))],
            out_specs=[pl.BlockSpec((tq, hb * DV), lambda h, p, qi, kj: (qi[p], h)),
                       pl.BlockSpec((hb, 8, tq), lambda h, p, qi, kj: (h, 0, qi[p]))],
            scratch_shapes=[pltpu.VMEM((hb, 8, tq), F32), pltpu.VMEM((hb, 8, tq), F32), pltpu.VMEM((hb, DV, tq), F32)],
        ),
        out_shape=[jax.ShapeDtypeStruct((S, H * DV), BF16), jax.ShapeDtypeStruct((H, 8, S), F32)],
        compiler_params=_cparams(("parallel", "arbitrary"), 40 << 20),
        name="flash_fwd",
    )(qi_tab, kj_tab, q, k, vt)


def _sigmoid(x):
    return 1.0 / (1.0 + jnp.exp(-x))


HALO = 16


def _f32(ref):
    return ref[...].astype(F32)


def _shift_rows(u, prev, n, first):
    ts = u.shape[0]
    row = lax.broadcasted_iota(jnp.int32, u.shape, 0)
    out = pltpu.roll(u, n, 0)
    for j in range(n):
        halo = jnp.where(first, 0.0, prev[HALO - n + j:HALO - n + j + 1, :])
        out = jnp.where(row == j, halo, out)
    return out


def _gates_fwd(proj, attn, conv_w, ts):
    S = proj.shape[0]

    def body(attn_ref, zm_ref, cin_ref, bg_ref, cg_ref, zc_ref, cin_p, cg_p, w_ref, am_ref, ac_ref):
        first = pl.program_id(0) == 0
        zm = _f32(zm_ref)
        am_ref[...] = (_f32(attn_ref) * (zm * _sigmoid(zm))).astype(BF16)
        u = _f32(cg_ref) * _f32(cin_ref)
        up = _f32(cg_p) * _f32(cin_p)
        w = w_ref[...]
        conv = w[0:1, :] * _shift_rows(u, up, 2, first) + w[1:2, :] * _shift_rows(u, up, 1, first) + w[2:3, :] * u
        zc = _f32(zc_ref)
        ac_ref[...] = (_f32(bg_ref) * conv * (zc * _sigmoid(zc))).astype(BF16)

    seg = lambda name: pl.BlockSpec((ts, D), lambda i: (i, SEG[name]))
    prev = lambda name: pl.BlockSpec((HALO, D), lambda i: (jnp.maximum(i * (ts // HALO) - 1, 0), SEG[name]))
    return pl.pallas_call(
        body,
        grid=(S // ts,),
        in_specs=[pl.BlockSpec((ts, D), lambda i: (i, 0)), seg("z_mla"), seg("c_in"), seg("b_gate"), seg("c_gate"), seg("z_conv"),
                  prev("c_in"), prev("c_gate"), pl.BlockSpec((8, D), lambda i: (0, 0))],
        out_specs=[pl.BlockSpec((ts, D), lambda i: (i, 0))] * 2,
        out_shape=[jax.ShapeDtypeStruct((S, D), BF16)] * 2,
        compiler_params=_cparams(("arbitrary",), 48 << 20),
        name="gates_fwd",
    )(attn, proj, proj, proj, proj, proj, proj, proj, conv_w)


def _merge_fwd(proj, y_mla, y_conv, ts):
    S = proj.shape[0]

    def body(gm_ref, gc_ref, ym_ref, yc_ref, o_ref):
        o_ref[...] = (_sigmoid(_f32(gm_ref)) * _f32(ym_ref) + _sigmoid(_f32(gc_ref)) * _f32(yc_ref)).astype(BF16)

    seg = lambda name: pl.BlockSpec((ts, D), lambda i: (i, SEG[name]))
    row = pl.BlockSpec((ts, D), lambda i: (i, 0))
    return pl.pallas_call(
        body, grid=(S // ts,), in_specs=[seg("g_mla"), seg("g_conv"), row, row], out_specs=row,
        out_shape=jax.ShapeDtypeStruct((S, D), BF16), compiler_params=_cparams(("parallel",), 32 << 20), name="merge_fwd",
    )(proj, proj, y_mla, y_conv)


def _post_loss(out, x, target, g_post, ts):
    S = out.shape[0]

    def body(o_ref, x_ref, t_ref, g_ref, dy_ref, do_ref, dg_ref, loss_ref):
        i = pl.program_id(0)
        o = o_ref[...]
        r = _rms_scale(o)
        n = o * r
        g = g_ref[...]
        err = x_ref[...] + n * g - t_ref[...]
        dy = err * (1.0 / D)
        dy_ref[...] = dy
        dn = dy * g
        do_ref[...] = (r * (dn - n * jnp.mean(dn * n, axis=-1, keepdims=True))).astype(BF16)
        dg = jnp.sum(dy * n, axis=0, keepdims=True)
        part = jnp.sum(jnp.sum(err * err, axis=0, keepdims=True), axis=1, keepdims=True) * (0.5 / D)

        @pl.when(i == 0)
        def _():
            dg_ref[...] = jnp.zeros_like(dg_ref)
            loss_ref[...] = jnp.zeros_like(loss_ref)

        dg_ref[0:1, :] += dg
        loss_ref[...] += jnp.broadcast_to(part, loss_ref.shape)

    row = pl.BlockSpec((ts, D), lambda i: (i, 0))
    return pl.pallas_call(
        body, grid=(S // ts,),
        in_specs=[row, row, row, pl.BlockSpec((1, D), lambda i: (0, 0))],
        out_specs=[row, row, pl.BlockSpec((8, D), lambda i: (0, 0)), pl.BlockSpec((8, 128), lambda i: (0, 0))],
        out_shape=[jax.ShapeDtypeStruct((S, D), F32), jax.ShapeDtypeStruct((S, D), BF16),
                   jax.ShapeDtypeStruct((8, D), F32), jax.ShapeDtypeStruct((8, 128), F32)],
        compiler_params=_cparams(("arbitrary",), 40 << 20), name="post_loss",
    )(out, x, target, g_post)


def _merge_bwd(proj, y_mla, y_conv, dmerged, ts):
    S = proj.shape[0]

    def body(gm_ref, gc_ref, ym_ref, yc_ref, dm_ref, dym_ref, dyc_ref, dgm_ref, dgc_ref):
        dm = _f32(dm_ref)
        sm = _sigmoid(_f32(gm_ref))
        sc = _sigmoid(_f32(gc_ref))
        dym_ref[...] = (dm * sm).astype(BF16)
        dyc_ref[...] = (dm * sc).astype(BF16)
        dgm_ref[...] = (dm * _f32(ym_ref) * (sm * (1.0 - sm))).astype(BF16)
        dgc_ref[...] = (dm * _f32(yc_ref) * (sc * (1.0 - sc))).astype(BF16)

    seg = lambda name: pl.BlockSpec((ts, D), lambda i: (i, SEG[name]))
    row = pl.BlockSpec((ts, D), lambda i: (i, 0))
    return pl.pallas_call(
        body, grid=(S // ts,), in_specs=[seg("g_mla"), seg("g_conv"), row, row, row], out_specs=[row] * 4,
        out_shape=[jax.ShapeDtypeStruct((S, D), BF16)] * 4, compiler_params=_cparams(("parallel",), 40 << 20), name="merge_bwd",
    )(proj, proj, y_mla, y_conv, dmerged)


def _gates_bwd(proj, attn, da_mla, da_conv, conv_w, ts):
    S = proj.shape[0]
    nblk = S // ts

    def body(attn_ref, zm_ref, cin_ref, bg_ref, cg_ref, zc_ref, dam_ref, dac_ref, cin_p, cg_p, bg_n, zc_n, dac_n, w_ref,
             dattn_ref, delta_ref, dzm_ref, dcin_ref, dbg_ref, dcg_ref, dzc_ref, dw_ref):
        i = pl.program_id(0)
        first = i == 0
        last = i == nblk - 1

        @pl.when(first)
        def _():
            dw_ref[...] = jnp.zeros_like(dw_ref)

        row = lax.broadcasted_iota(jnp.int32, (ts, DV), 0)
        for c in range(D // DV):
            cols = slice(c * DV, (c + 1) * DV)
            ld = lambda ref: ref[:, cols].astype(F32)
            zm = ld(zm_ref)
            sg = _sigmoid(zm)
            attn = ld(attn_ref)
            dam = ld(dam_ref)
            dattn = dam * (zm * sg)
            dattn_ref[:, cols] = dattn.astype(BF16)
            dzm_ref[:, cols] = (dam * attn * (sg * (1.0 + zm * (1.0 - sg)))).astype(BF16)
            delta_ref[c] = _col_to_row8(jnp.sum(dattn * attn, axis=1, keepdims=True), ts)
            w = w_ref[:, cols]
            cin, cg, bg, zc = ld(cin_ref), ld(cg_ref), ld(bg_ref), ld(zc_ref)
            u = cg * cin
            up = ld(cg_p) * ld(cin_p)
            u1 = _shift_rows(u, up, 1, first)
            u2 = _shift_rows(u, up, 2, first)
            conv = w[0:1, :] * u2 + w[1:2, :] * u1 + w[2:3, :] * u
            sgc = _sigmoid(zc)
            siluc = zc * sgc
            dac = ld(dac_ref)
            dbg_ref[:, cols] = (dac * conv * siluc).astype(BF16)
            dzc_ref[:, cols] = (dac * bg * conv * (sgc * (1.0 + zc * (1.0 - sgc)))).astype(BF16)
            dconv = dac * bg * siluc
            zn = ld(zc_n)
            dconv_n = jnp.where(last, 0.0, ld(dac_n) * ld(bg_n) * (zn * _sigmoid(zn)))
            d1 = jnp.where(row == ts - 1, dconv_n[0:1, :], pltpu.roll(dconv, ts - 1, 0))
            d2 = jnp.where(row == ts - 1, dconv_n[1:2, :], jnp.where(row == ts - 2, dconv_n[0:1, :], pltpu.roll(dconv, ts - 2, 0)))
            du = w[2:3, :] * dconv + w[1:2, :] * d1 + w[0:1, :] * d2
            dcg_ref[:, cols] = (du * cin).astype(BF16)
            dcin_ref[:, cols] = (du * cg).astype(BF16)
            dw_ref[0:1, cols] += jnp.sum(dconv * u2, axis=0, keepdims=True)
            dw_ref[1:2, cols] += jnp.sum(dconv * u1, axis=0, keepdims=True)
            dw_ref[2:3, cols] += jnp.sum(dconv * u, axis=0, keepdims=True)

    seg = lambda name: pl.BlockSpec((ts, D), lambda i: (i, SEG[name]))
    prev = lambda name: pl.BlockSpec((HALO, D), lambda i: (jnp.maximum(i * (ts // HALO) - 1, 0), SEG[name]))
    nxt = lambda blk: pl.BlockSpec((HALO, D), lambda i: (jnp.minimum((i + 1) * (ts // HALO), S // HALO - 1), blk))
    row = pl.BlockSpec((ts, D), lambda i: (i, 0))
    return pl.pallas_call(
        body, grid=(nblk,),
        in_specs=[row, seg("z_mla"), seg("c_in"), seg("b_gate"), seg("c_gate"), seg("z_conv"), row, row,
                  prev("c_in"), prev("c_gate"), nxt(SEG["b_gate"]), nxt(SEG["z_conv"]), nxt(0), pl.BlockSpec((8, D), lambda i: (0, 0))],
        out_specs=[row, pl.BlockSpec((H, 8, ts), lambda i: (0, 0, i)), row, row, row, row, row, pl.BlockSpec((8, D), lambda i: (0, 0))],
        out_shape=[jax.ShapeDtypeStruct((S, D), BF16), jax.ShapeDtypeStruct((H, 8, S), F32)] + [jax.ShapeDtypeStruct((S, D), BF16)] * 5
        + [jax.ShapeDtypeStruct((8, D), F32)],
        compiler_params=_cparams(("arbitrary",), 56 << 20), name="gates_bwd",
    )(attn, proj, proj, proj, proj, proj, da_mla, da_conv, proj, proj, proj, proj, da_conv, conv_w)


def _flash_bwd(q, k, v, do, lse, delta, tq, hb=2):
    S = q.shape[1]
    nq = S // tq
    scale = SM_SCALE
    tsub = min(256, tq)
    nsub = tq // tsub

    qi_tab, kj_tab = _causal_pairs(nq, key_major=True)
    npairs = nq * (nq + 1) // 2

    def body(qi_ref, kj_ref, q_ref, k_ref, v_ref, do_ref, lse_ref, dl_ref, dq_ref, dk_ref, dv_ref, dk_sc, dv_sc, dq_sc):
        p = pl.program_id(1)
        qi, kj = qi_ref[p], kj_ref[p]

        @pl.when(p == 0)
        def _():
            dq_sc[...] = jnp.zeros_like(dq_sc)

        @pl.when(qi == kj)
        def _():
            dk_sc[...] = jnp.zeros_like(dk_sc)
            dv_sc[...] = jnp.zeros_like(dv_sc)

        def step(diag):
            chains = [(h, u) for h in range(hb) for u in range(nsub)]

            def first_matmuls(h, u):
                sub = slice(u * tsub, (u + 1) * tsub)
                st = lax.dot_general(k_ref[h], q_ref[h, sub, :], NT, preferred_element_type=F32)
                dpt = lax.dot_general(v_ref[h], do_ref[sub, h * DV:(h + 1) * DV], NT, preferred_element_type=F32)
                return st, dpt

            nxt = first_matmuls(*chains[0])
            for ci, (h, u) in enumerate(chains):
                st, dpt = nxt
                if ci + 1 < len(chains):
                    nxt = first_matmuls(*chains[ci + 1])
                sub = slice(u * tsub, (u + 1) * tsub)
                pt = jnp.exp2(st - lse_ref[h, 0:1, sub])
                if diag:
                    r = lax.broadcasted_iota(jnp.int32, (tq, tsub), 0)
                    c = lax.broadcasted_iota(jnp.int32, (tq, tsub), 1) + u * tsub
                    pt = jnp.where(r <= c, pt, 0.0)
                dst = (pt * (dpt - dl_ref[h, 0:1, sub])).astype(BF16)
                dv_sc[h] += jnp.dot(pt.astype(BF16), do_ref[sub, h * DV:(h + 1) * DV], preferred_element_type=F32)
                dk_sc[h] += jnp.dot(dst, q_ref[h, sub, :], preferred_element_type=F32)
                rows = pl.ds(pl.multiple_of(qi * tq + u * tsub, tsub), tsub)
                dq_sc[h, rows, :] += lax.dot_general(dst, k_ref[h], TN, preferred_element_type=F32)

        @pl.when(qi > kj)
        def _():
            step(False)

        @pl.when(qi == kj)
        def _():
            step(True)

        @pl.when(qi == nq - 1)
        def _():
            dk_ref[...] = (dk_sc[...] * (1.0 / LOG2E)).astype(BF16)
            dv_ref[...] = dv_sc[...].astype(BF16)

        @pl.when(p == npairs - 1)
        def _():
            dq_ref[...] = (dq_sc[...] * scale).astype(BF16)

    return pl.pallas_call(
        body,
        grid_spec=pltpu.PrefetchScalarGridSpec(
            num_scalar_prefetch=2,
            grid=(H // hb, npairs),
            in_specs=[pl.BlockSpec((hb, tq, HW), lambda h, p, qi, kj: (h, qi[p], 0)),
                      pl.BlockSpec((hb, tq, HW), lambda h, p, qi, kj: (h, kj[p], 0)),
                      pl.BlockSpec((hb, tq, DV), lambda h, p, qi, kj: (h, kj[p], 0)),
                      pl.BlockSpec((tq, hb * DV), lambda h, p, qi, kj: (qi[p], h)),
                      pl.BlockSpec((hb, 8, tq), lambda h, p, qi, kj: (h, 0, qi[p])),
                      pl.BlockSpec((hb, 8, tq), lambda h, p, qi, kj: (h, 0, qi[p]))],
            out_specs=[pl.BlockSpec((hb, S, HW), lambda h, p, qi, kj: (h, 0, 0)),
                       pl.BlockSpec((hb, tq, HW), lambda h, p, qi, kj: (h, kj[p], 0)),
                       pl.BlockSpec((hb, tq, DV), lambda h, p, qi, kj: (h, kj[p], 0))],
            scratch_shapes=[pltpu.VMEM((hb, tq, HW), F32), pltpu.VMEM((hb, tq, DV), F32), pltpu.VMEM((hb, S, HW), F32)],
        ),
        out_shape=[jax.ShapeDtypeStruct((H, S, HW), BF16), jax.ShapeDtypeStruct((H, S, HW), BF16), jax.ShapeDtypeStruct((H, S, DV), BF16)],
        compiler_params=_cparams(("parallel", "arbitrary"), VMEM_CAP),
        name="flash_bwd",
    )(qi_tab, kj_tab, q, k, v, do, lse, delta)


def _rms_bwd(xf, g, dn_out):
    r = _rms_scale(xf)
    n = xf * r
    dn = dn_out * g
    dx = r * (dn - n * jnp.mean(dn * n, axis=-1, keepdims=True))
    return dx, jnp.sum(dn_out * n, axis=0, keepdims=True)


def _qkv_bwd(proj, pos, freq, g_qa, g_kva, w_q, w_kv, dq, dk, dv, segments, ts):
    S = proj.shape[0]
    nseg = len(segments)
    nblk = S // ts
    tail_w = NP - nseg * D

    def body(qa_ref, ckv_ref, pos_ref, freq_ref, gq_ref, gk_ref, wq_ref, wkv_ref, dq_ref, dk_ref, dv_ref, *rest):
        seg_hbm = rest[:nseg]
        dqp_ref, dkvp_ref, dproj_hbm, dgq_ref, dgk_ref, tail_buf, sems = rest[nseg:]
        i = pl.program_id(0)
        slot = i % 2

        def copies(step, sl):
            rows = pl.ds(pl.multiple_of(step * ts, ts), ts)
            cps = [pltpu.make_async_copy(seg_hbm[j].at[rows, :], dproj_hbm.at[rows, pl.ds(j * D, D)], sems.at[sl, j]) for j in range(nseg)]
            return cps, pltpu.make_async_copy(tail_buf.at[sl], dproj_hbm.at[rows, pl.ds(nseg * D, tail_w)], sems.at[sl, nseg])

        def wait_all(step, sl):
            cps, tail = copies(step, sl)
            for cp in cps + [tail]:
                cp.wait()

        @pl.when(i >= 2)
        def _():
            wait_all(i - 2, slot)

        seg_cps, tail_cp = copies(i, slot)
        for cp in seg_cps:
            cp.start()
        dqa_ref = tail_buf.at[slot, :, 0:RQ]
        dckv_ref = tail_buf.at[slot, :, RQ:RQ + RKV]
        dkr_ref = tail_buf.at[slot, :, RQ + RKV:tail_w]
        cos, sin, lane = _rope_tables(pos_ref, freq_ref, ts)
        sgn_sin = jnp.where(lane < 32, sin, -sin)
        live = lane < DR
        kr_sum = jnp.zeros((ts, 128), F32)
        for h in range(H):
            dqp_ref[:, h * HW:h * HW + DN] = dq_ref[h, :, 0:DN]
            t = dq_ref[h, :, DN:HW].astype(F32)
            dqp_ref[:, h * HW + DN:(h + 1) * HW] = jnp.where(live, t * cos + _rope_swap(t, lane) * sgn_sin, 0.0).astype(BF16)
            dkvp_ref[:, h * HW:h * HW + DN] = dk_ref[h, :, 0:DN]
            dkvp_ref[:, h * HW + DN:(h + 1) * HW] = dv_ref[h]
            kr_sum = kr_sum + dk_ref[h, :, DN:HW].astype(F32)
        dkr_ref[...] = jnp.where(live, kr_sum * cos + _rope_swap(kr_sum, lane) * sgn_sin, 0.0).astype(BF16)
        dqn = lax.dot_general(dqp_ref[...], wq_ref[...], NT, preferred_element_type=F32)
        dkvn = lax.dot_general(dkvp_ref[...], wkv_ref[...], NT, preferred_element_type=F32)
        dqa, dgq = _rms_bwd(_f32(qa_ref), gq_ref[...], dqn)
        dckv, dgk = _rms_bwd(_f32(ckv_ref), gk_ref[...], dkvn)
        dqa_ref[...] = dqa.astype(BF16)
        dckv_ref[...] = dckv.astype(BF16)
        tail_cp.start()

        @pl.when(i == 0)
        def _():
            dgq_ref[...] = jnp.zeros_like(dgq_ref)
            dgk_ref[...] = jnp.zeros_like(dgk_ref)

        dgq_ref[0:1, :] += dgq
        dgk_ref[0:1, :] += dgk

        @pl.when(i == nblk - 1)
        def _():
            wait_all(i, slot)
            if nblk >= 2:
                wait_all(i - 1, 1 - slot)

    rowb = lambda w, blk: pl.BlockSpec((ts, w), lambda i: (i, blk))
    full = lambda a: pl.BlockSpec(a.shape, lambda i: (0,) * a.ndim)
    heads = lambda w: pl.BlockSpec((H, ts, w), lambda i: (0, i, 0))
    anyspec = pl.BlockSpec(memory_space=pl.ANY)
    return pl.pallas_call(
        body, grid=(nblk,),
        in_specs=[rowb(RQ, OFF_QA // RQ), rowb(RKV, OFF_CKV // RKV), pl.BlockSpec((ts, 1), lambda i: (i, 0)), full(freq), full(g_qa), full(g_kva),
                  full(w_q), full(w_kv), heads(HW), heads(HW), heads(DV)] + [anyspec] * nseg,
        out_specs=[rowb(H * HW, 0), rowb(H * HW, 0), anyspec,
                   pl.BlockSpec((8, RQ), lambda i: (0, 0)), pl.BlockSpec((8, RKV), lambda i: (0, 0))],
        out_shape=[jax.ShapeDtypeStruct((S, H * HW), BF16), jax.ShapeDtypeStruct((S, H * HW), BF16), jax.ShapeDtypeStruct((S, NP), BF16),
                   jax.ShapeDtypeStruct((8, RQ), F32), jax.ShapeDtypeStruct((8, RKV), F32)],
        scratch_shapes=[pltpu.VMEM((2, ts, tail_w), BF16), pltpu.SemaphoreType.DMA((2, nseg + 1))],
        compiler_params=_cparams(("arbitrary",), 56 << 20), name="qkv_bwd",
    )(proj, proj, pos, freq, g_qa, g_kva, w_q, w_kv, dq, dk, dv, *segments)


def _prenorm_bwd(x, g, dh, dy, ts):
    S = x.shape[0]

    def body(x_ref, g_ref, dh_ref, dy_ref, gx_ref, dg_ref):
        dx, dg = _rms_bwd(x_ref[...], g_ref[...], dh_ref[...])
        gx_ref[...] = dy_ref[...] + dx

        @pl.when(pl.program_id(0) == 0)
        def _():
            dg_ref[...] = jnp.zeros_like(dg_ref)

        dg_ref[0:1, :] += dg

    row = pl.BlockSpec((ts, D), lambda i: (i, 0))
    return pl.pallas_call(
        body, grid=(S // ts,), in_specs=[row, pl.BlockSpec((1, D), lambda i: (0, 0)), row, row],
        out_specs=[row, pl.BlockSpec((8, D), lambda i: (0, 0))],
        out_shape=[jax.ShapeDtypeStruct((S, D), F32), jax.ShapeDtypeStruct((8, D), F32)],
        compiler_params=_cparams(("arbitrary",), 40 << 20), name="prenorm_bwd",
    )(x, g, dh, dy)


def _local_step(x, pos, target, g_pre, g_qa, g_kva, g_post, w_in, other_weights, reduce_grads=None):
    S = x.shape[0]
    ts = min(256, S)
    tq = min(512, S)
    tm = min(1024, S)
    mm = functools.partial(_matmul, tm=tm)
    freq = _inv_freq_tile()

    h, ht = _prenorm(x, g_pre, ts)
    if len(other_weights) == 2:
        gather, assemble = other_weights
        proj, gathered = _matmul(h, w_in, mode="nn", out_dtype=BF16, tm=1024, tn=1408, tk=D, name="mm_proj", exchange=gather)
        other_weights = assemble(*gathered)
    else:
        proj = _matmul(h, w_in, mode="nn", out_dtype=BF16, tm=1024, tn=1408, tk=D, name="mm_proj")
    w_q, w_kv, conv_w8, w_o_mla, w_o_conv, w_out = other_weights
    q, k, v, vt, qn, kvn = _qkv_prep(proj, pos, freq, g_qa, g_kva, w_q, w_kv, ts)
    attn, lse = _flash_fwd(q, k, vt, tq)
    a_mla, a_conv = _gates_fwd(proj, attn, conv_w8, ts)
    y_mla = mm(a_mla, w_o_mla, mode="nn", out_dtype=BF16, tn=D, tk=D, name="mm_y_mla")
    y_conv = mm(a_conv, w_o_conv, mode="nn", out_dtype=BF16, tn=D, tk=D, name="mm_y_conv")
    merged = _merge_fwd(proj, y_mla, y_conv, ts)
    out = mm(merged, w_out, mode="nn", out_dtype=F32, tn=D, tk=D, name="mm_out")
    dy, dout, dg_post, loss = _post_loss(out, x, target, g_post, ts)

    dmerged = mm(dout, w_out, mode="nt", out_dtype=BF16, tn=D, tk=D, name="mm_dmerged")
    dw_out = _matmul(merged, dout, mode="tn", out_dtype=F32, tm=1024, tn=1024, tk=2048, name="mm_dw_out")
    dy_mla, dy_conv, dg_mla, dg_conv = _merge_bwd(proj, y_mla, y_conv, dmerged, ts)
    da_mla = mm(dy_mla, w_o_mla, mode="nt", out_dtype=BF16, tn=D, tk=D, name="mm_da_mla")
    da_conv = mm(dy_conv, w_o_conv, mode="nt", out_dtype=BF16, tn=D, tk=D, name="mm_da_conv")
    dw_o_mla = _matmul(a_mla, dy_mla, mode="tn", out_dtype=F32, tm=1024, tn=1024, tk=2048, name="mm_dw_o_mla")
    dw_o_conv = _matmul(a_conv, dy_conv, mode="tn", out_dtype=F32, tm=1024, tn=1024, tk=2048, name="mm_dw_o_conv")
    dattn, delta, dz_mla, dc_in, db_gate, dc_gate, dz_conv, dconv_w = _gates_bwd(proj, attn, da_mla, da_conv, conv_w8, min(128, S))
    dq, dk, dv = _flash_bwd(q, k, v, dattn, lse, delta, tq)
    dqp, dkvp, dproj, dg_qa, dg_kva = _qkv_bwd(proj, pos, freq, g_qa, g_kva, w_q, w_kv, dq, dk, dv,
                                               [dz_mla, dc_in, db_gate, dc_gate, dz_conv, dg_mla, dg_conv], min(256, S))
    dw_q = _matmul(qn, dqp, mode="tn", out_dtype=F32, tm=RQ, tn=1024, tk=2048, name="mm_dw_q")
    dw_kv = _matmul(kvn, dkvp, mode="tn", out_dtype=F32, tm=RKV, tn=1024, tk=2048, name="mm_dw_kv")
    dw_in = _matmul(ht, dproj, mode="nn", out_dtype=F32, tm=1024, tn=1408, tk=2048, name="mm_dw_in")
    res = dict(dw_in=dw_in, dw_q=dw_q, dw_kv=dw_kv, dconv_w=dconv_w, dw_o_mla=dw_o_mla, dw_o_conv=dw_o_conv, dw_out=dw_out)
    if reduce_grads is None:
        dh = _matmul(dproj, w_in, mode="nt", out_dtype=F32, tm=1024, tn=D, tk=1408, name="mm_dh")
    else:
        exchange, finish = reduce_grads(res)
        dh, got = _matmul(dproj, w_in, mode="nt", out_dtype=F32, tm=1024, tn=D, tk=1408, name="mm_dh", exchange=exchange)
        res["reduced"] = finish(got)
    grad_x, dg_pre = _prenorm_bwd(x, g_pre, dh, dy, ts)
    res.update(loss=loss, grad_x=grad_x, dg_pre=dg_pre, dg_qa=dg_qa, dg_kva=dg_kva, dg_post=dg_post)
    return res


def _my_id():
    return lax.axis_index("x") * 4 + lax.axis_index("y") * 2 + lax.axis_index("c")


def _place():
    x, y, c = lax.axis_index("x"), lax.axis_index("y"), lax.axis_index("c")
    return (x, y, c), (x, y, 1 - c), [(1 - x, y), (x, 1 - y), (1 - x, 1 - y)]


def _slot(px, py, pc):
    return 4 * px + 2 * py + pc


def _all_gather(arrays, name):
    n = len(arrays)

    def body(*refs):
        ins, outs = refs[:n], refs[n:2 * n]
        send_sems, recv_sems, local_sems = refs[2 * n:]
        me, sib, chips = _place()
        c = me[2]

        def copy(a, k, block, to, src=None):
            rows = outs[a].at[_slot(*block)]
            return pltpu.make_async_remote_copy(src_ref=rows if src is None else src, dst_ref=rows, send_sem=send_sems.at[a, k],
                                                recv_sem=recv_sems.at[a, k], device_id=to, device_id_type=MESH)

        local = [pltpu.make_async_copy(ins[a], outs[a].at[_slot(*me)], local_sems.at[a]) for a in range(n)]
        for cp in local:
            cp.start()
        sends = []
        for a in range(n):
            sends.append(copy(a, 0, me, sib, src=ins[a]))
            sends += [copy(a, 1 + j, me, (*chip, c), src=ins[a]) for j, chip in enumerate(chips)]
        for cp in sends:
            cp.start()
        for a in range(n):
            for j, chip in enumerate(chips):
                copy(a, 1 + j, (*chip, c), me).wait_recv()
                fwd = copy(a, 4 + j, (*chip, c), sib)
                fwd.start()
                sends.append(fwd)
        for a in range(n):
            copy(a, 0, sib, me).wait_recv()
            for j, chip in enumerate(chips):
                copy(a, 4 + j, (*chip, 1 - c), me).wait_recv()
        for cp in sends:
            cp.wait_send()
        for cp in local:
            cp.wait()

    anyspec = pl.BlockSpec(memory_space=pl.ANY)
    return pl.pallas_call(
        body,
        in_specs=[anyspec] * n,
        out_specs=[anyspec] * n,
        out_shape=[jax.ShapeDtypeStruct((NDEV,) + a.shape, a.dtype) for a in arrays],
        scratch_shapes=[pltpu.SemaphoreType.DMA((n, NDEV - 1)), pltpu.SemaphoreType.DMA((n, NDEV - 1)), pltpu.SemaphoreType.DMA((n,))],
        name=name,
    )(*arrays)


def _all_gather_relayed(x, name):
    R, C = x.shape
    half = R // 2

    def body(x_ref, out_ref, send_sems, recv_sems, local_sem):
        (mx, my, c), sib, _ = _place()
        xn, yn, dg = (1 - mx, my, c), (mx, 1 - my, c), (1 - mx, 1 - my, c)

        def rows(block, h):
            return out_ref.at[_slot(*block), pl.ds(h * half, half), :]

        def copy(k, dst, to, src=None):
            return pltpu.make_async_remote_copy(src_ref=dst if src is None else src, dst_ref=dst, send_sem=send_sems.at[k],
                                                recv_sem=recv_sems.at[k], device_id=to, device_id_type=MESH)

        me = (mx, my, c)
        local = pltpu.make_async_copy(x_ref, out_ref.at[_slot(*me)], local_sem)
        local.start()
        sends = [copy(0, out_ref.at[_slot(*me)], sib, src=x_ref)]
        for h in range(2):
            sends.append(copy(1 + h, rows(me, h), xn, src=x_ref.at[pl.ds(h * half, half), :]))
            sends.append(copy(3 + h, rows(me, h), yn, src=x_ref.at[pl.ds(h * half, half), :]))
        for cp in sends:
            cp.start()
        copy(3, rows(yn, 0), me).wait_recv()
        relay_x = copy(5, rows(yn, 0), xn)
        relay_x.start()
        copy(1, rows(xn, 0), me).wait_recv()
        copy(2, rows(xn, 1), me).wait_recv()
        relay_y = copy(6, rows(xn, 1), yn)
        relay_y.start()
        sends += [relay_x, relay_y]

        def to_sibling(k, block):
            fwd = copy(k, out_ref.at[_slot(*block)], sib)
            fwd.start()
            sends.append(fwd)

        to_sibling(7, xn)
        copy(4, rows(yn, 1), me).wait_recv()
        to_sibling(8, yn)
        copy(5, rows(dg, 0), me).wait_recv()
        copy(6, rows(dg, 1), me).wait_recv()
        to_sibling(9, dg)
        copy(0, out_ref.at[_slot(*sib)], me).wait_recv()
        for k, block in ((7, xn), (8, yn), (9, dg)):
            copy(k, out_ref.at[_slot(block[0], block[1], 1 - c)], me).wait_recv()
        for cp in sends:
            cp.wait_send()
        local.wait()

    anyspec = pl.BlockSpec(memory_space=pl.ANY)
    return pl.pallas_call(
        body, in_specs=[anyspec], out_specs=anyspec, out_shape=jax.ShapeDtypeStruct((NDEV, R, C), x.dtype),
        scratch_shapes=[pltpu.SemaphoreType.DMA((10,)), pltpu.SemaphoreType.DMA((10,)), pltpu.SemaphoreType.DMA],
        name=name,
    )(x)


def _slab_exchange(arrays, nslots, pick):
    n = len(arrays)

    def copies(ins, outs, sems):
        send_sems, recv_sems = sems
        return [pltpu.make_async_remote_copy(src_ref=ins[a].at[pick(k)[0]], dst_ref=outs[a].at[k], send_sem=send_sems.at[a, k],
                                             recv_sem=recv_sems.at[a, k], device_id=pick(k)[1], device_id_type=MESH)
                for a in range(n) for k in range(nslots)]

    def start(ins, outs, sems):
        for cp in copies(ins, outs, sems):
            cp.start()

    def finish(ins, outs, sems):
        cps = copies(ins, outs, sems)
        for cp in cps:
            cp.wait_recv()
        for cp in cps:
            cp.wait_send()

    return _Exchange(arrays, [jax.ShapeDtypeStruct((nslots,) + a.shape[1:], a.dtype) for a in arrays],
                     [pltpu.SemaphoreType.DMA((n, nslots)), pltpu.SemaphoreType.DMA((n, nslots))], start, finish)


def _direct_gather(arrays):
    n = len(arrays)

    def peer(d):
        p = (_my_id() + d) % NDEV
        return (p // 4, (p // 2) % 2, p % 2), p

    def copies(ins, outs, sems, receiving):
        send_sems, recv_sems, _ = sems
        slot = lambda d: peer(NDEV - d)[1] if receiving else _my_id()
        return [pltpu.make_async_remote_copy(src_ref=ins[a], dst_ref=outs[a].at[slot(d)], send_sem=send_sems.at[a, d - 1],
                                             recv_sem=recv_sems.at[a, d - 1], device_id=peer(d)[0], device_id_type=MESH)
                for d in range(1, NDEV) for a in range(n)]

    def local(ins, outs, sems):
        return [pltpu.make_async_copy(ins[a], outs[a].at[_my_id()], sems[2].at[a]) for a in range(n)]

    def start(ins, outs, sems):
        for cp in local(ins, outs, sems) + copies(ins, outs, sems, False):
            cp.start()

    def finish(ins, outs, sems):
        for cp in copies(ins, outs, sems, True):
            cp.wait_recv()
        for cp in copies(ins, outs, sems, False):
            cp.wait_send()
        for cp in local(ins, outs, sems):
            cp.wait()

    return _Exchange(arrays, [jax.ShapeDtypeStruct((NDEV,) + a.shape, a.dtype) for a in arrays],
                     [pltpu.SemaphoreType.DMA((n, NDEV - 1)), pltpu.SemaphoreType.DMA((n, NDEV - 1)), pltpu.SemaphoreType.DMA((n,))],
                     start, finish)


def _run_exchange(exchange, name):
    n = len(exchange.arrays)

    def body(*refs):
        ins, outs, sems = refs[:n], refs[n:2 * n], refs[2 * n:]
        exchange.start(ins, outs, sems)
        exchange.finish(ins, outs, sems)

    anyspec = pl.BlockSpec(memory_space=pl.ANY)
    return pl.pallas_call(body, in_specs=[anyspec] * n, out_specs=[anyspec] * n, out_shape=exchange.out_shapes,
                          scratch_shapes=exchange.sem_shapes, name=name)(*exchange.arrays)


def _slabs_bf16(x, rows_per_block, name):
    n, R, C = x.shape
    tr = min(rows_per_block, R)

    def body(x_ref, o_ref):
        o_ref[...] = x_ref[...].astype(BF16)

    blk = pl.BlockSpec((1, tr, C), lambda k, i: (k, i, 0))
    return pl.pallas_call(body, grid=(n, R // tr), in_specs=[blk], out_specs=blk, out_shape=jax.ShapeDtypeStruct(x.shape, BF16),
                          compiler_params=_cparams(("parallel", "parallel"), 32 << 20), name=name)(x)


def _to_sibling(k):
    me, sib, chips = _place()
    dest = sib if k == 0 else (*chips[k - 1], sib[2])
    return _slot(*dest), sib


def _to_chips(k):
    me, sib, chips = _place()
    return k, (*chips[k], me[2])


def _chip_sums(own, got, slots, rows_per_block, name):
    _, R, C = own.shape
    tr = min(rows_per_block, R)

    def body(slots_ref, own_ref, got_ref, o_ref):
        o_ref[0] = (own_ref[0].astype(F32) + got_ref[0].astype(F32)).astype(BF16)

    return pl.pallas_call(
        body,
        grid_spec=pltpu.PrefetchScalarGridSpec(
            num_scalar_prefetch=1, grid=(3, R // tr),
            in_specs=[pl.BlockSpec((1, tr, C), lambda j, i, s: (s[1 + j], i, 0)), pl.BlockSpec((1, tr, C), lambda j, i, s: (1 + j, i, 0))],
            out_specs=pl.BlockSpec((1, tr, C), lambda j, i, s: (j, i, 0))),
        out_shape=jax.ShapeDtypeStruct((3, R, C), BF16),
        compiler_params=_cparams(("parallel", "parallel"), 32 << 20), name=name,
    )(slots, own, got)


def _adamw_math(w, g, m, v):
    m = ADAM_B1 * m + (1.0 - ADAM_B1) * g
    v = ADAM_B2 * v + (1.0 - ADAM_B2) * (g * g)
    m_hat = m / (1.0 - ADAM_B1 ** ADAM_STEP)
    v_hat = v / (1.0 - ADAM_B2 ** ADAM_STEP)
    delta = -ADAM_LR * (m_hat / (jnp.sqrt(v_hat) + ADAM_EPS) + ADAM_WD * w)
    return delta, m, v


def _reduce_adamw(own, got1, got2, slots, w, m, v, rows_per_block, name):
    _, R, C = own.shape
    tr = min(rows_per_block, R)
    assert R % tr == 0

    def body(slots_ref, own_ref, g1_ref, g2_ref, w_ref, m_ref, v_ref, g_ref, d_ref, nm_ref, nv_ref):
        g = own_ref[0] + g1_ref[0].astype(F32)
        for j in range(3):
            g = g + g2_ref[j].astype(F32)
        g_ref[...] = g
        d, nm, nv = _adamw_math(w_ref[...], g, m_ref[...], v_ref[...])
        d_ref[...] = d
        nm_ref[...] = nm
        nv_ref[...] = nv

    blk = pl.BlockSpec((tr, C), lambda i, s: (i, 0))
    return pl.pallas_call(
        body,
        grid_spec=pltpu.PrefetchScalarGridSpec(
            num_scalar_prefetch=1, grid=(R // tr,),
            in_specs=[pl.BlockSpec((1, tr, C), lambda i, s: (s[0], i, 0)), pl.BlockSpec((1, tr, C), lambda i, s: (0, i, 0)),
                      pl.BlockSpec((3, tr, C), lambda i, s: (0, i, 0)), blk, blk, blk],
            out_specs=[blk] * 4),
        out_shape=[jax.ShapeDtypeStruct((R, C), F32)] * 4,
        compiler_params=_cparams(("parallel",), 48 << 20), name=name,
    )(slots, own, got1, got2, w, m, v)


def _reduce_adamw_t(own, got1, got2, slots, wt, mt, vt, rows_per_block, name):
    R, C = own.shape
    tr = min(rows_per_block, R)
    assert R % tr == 0

    def body(slots_ref, own_ref, g1_ref, g2_ref, w_ref, m_ref, v_ref, g_ref, d_ref, nm_ref, nv_ref):
        g = own_ref[...] + g1_ref[0].astype(F32)
        for j in range(3):
            g = g + g2_ref[j].astype(F32)
        g = jnp.transpose(g)
        g_ref[...] = g
        d, nm, nv = _adamw_math(w_ref[...], g, m_ref[...], v_ref[...])
        d_ref[...] = d
        nm_ref[...] = nm
        nv_ref[...] = nv

    blk = pl.BlockSpec((C, tr), lambda i, s: (0, i))
    return pl.pallas_call(
        body,
        grid_spec=pltpu.PrefetchScalarGridSpec(
            num_scalar_prefetch=1, grid=(R // tr,),
            in_specs=[pl.BlockSpec((tr, C), lambda i, s: (i, 0)), pl.BlockSpec((1, tr, C), lambda i, s: (0, i, 0)),
                      pl.BlockSpec((3, tr, C), lambda i, s: (0, i, 0)), blk, blk, blk],
            out_specs=[blk] * 4),
        out_shape=[jax.ShapeDtypeStruct((C, R), F32)] * 4,
        compiler_params=_cparams(("parallel",), 48 << 20), name=name,
    )(slots, own, got1, got2, wt, mt, vt)


def _sum_adamw(parts, w, m, v, rows_per_block, name):
    n, R, C = parts.shape
    tr = min(rows_per_block, R)
    assert R % tr == 0

    def body(p_ref, w_ref, m_ref, v_ref, g_ref, d_ref, nm_ref, nv_ref):
        g = p_ref[0]
        for j in range(1, n):
            g = g + p_ref[j]
        g_ref[...] = g
        d, nm, nv = _adamw_math(w_ref[...], g, m_ref[...], v_ref[...])
        d_ref[...] = d
        nm_ref[...] = nm
        nv_ref[...] = nv

    blk = pl.BlockSpec((tr, C), lambda i: (i, 0))
    return pl.pallas_call(
        body, grid=(R // tr,),
        in_specs=[pl.BlockSpec((n, tr, C), lambda i: (0, i, 0)), blk, blk, blk],
        out_specs=[blk] * 4,
        out_shape=[jax.ShapeDtypeStruct((R, C), F32)] * 4,
        compiler_params=_cparams(("parallel",), 48 << 20), name=name,
    )(parts, w, m, v)


def kernel(x, positions, pre_norm_g, w_in, q_a_norm_g, w_q_b, kv_a_norm_g, w_kv_b, conv_w, w_o_mla, w_o_conv, w_out, post_norm_g, loss_target, m_pre_norm_g, m_w_in, m_q_a_norm_g, m_w_q_b, m_kv_a_norm_g, m_w_kv_b, m_conv_w, m_w_o_mla, m_w_o_conv, m_w_out, m_post_norm_g, v_pre_norm_g, v_w_in, v_q_a_norm_g, v_w_q_b, v_kv_a_norm_g, v_w_kv_b, v_conv_w, v_w_o_mla, v_w_o_conv, v_w_out, v_post_norm_g):
    S = x.shape[1]
    conv_pad = jnp.zeros((8, 256), F32).at[0:3, :].set(conv_w)
    g_in = _all_gather_relayed(w_in.astype(BF16), "all_gather_w_in")
    w_in_f = _assemble_w_in(g_in)
    gather_rest = _direct_gather([w_q_b.astype(BF16), w_kv_b.astype(BF16), conv_pad, w_o_mla.astype(BF16), w_o_conv.astype(BF16),
                                  w_out.astype(BF16)])

    def assemble_rest(g_q, g_kv, g_cw, g_om, g_oc, g_out):
        return (_assemble_w_q(g_q), _concat_cols(g_kv, BF16, "assemble_w_kv"), _concat_cols(g_cw, F32, "assemble_conv_w"),
                g_om.reshape(D, D), g_oc.reshape(D, D), g_out.reshape(D, D))

    gnames = ["w_in", "w_q", "w_kv", "conv_w", "w_o_mla", "w_o_conv", "w_out"]
    (mx, my, mc), _, chips = _place()
    slots = jnp.stack([_slot(mx, my, mc)] + [_slot(cx, cy, mc) for cx, cy in chips]).astype(jnp.int32)

    def reduce_grads(r):
        own_in, slabs_in = _split_dw_in(r["dw_in"], slots)
        own = [own_in, _split_dw_q(r["dw_q"]), _split_cols(r["dw_kv"], NDEV, "split_dw_kv"),
               _split_cols(r["dconv_w"], NDEV, "split_dconv_w"), r["dw_o_mla"].reshape(NDEV, D // NDEV, D),
               r["dw_o_conv"].reshape(NDEV, D // NDEV, D), r["dw_out"].reshape(NDEV, D // NDEV, D)]
        slabs = [slabs_in] + [_slabs_bf16(g, 128, "bf16_" + nm) for g, nm in zip(own[1:], gnames[1:])]
        to_sibling = _slab_exchange(slabs, 4, _to_sibling)
        got1 = _run_exchange(to_sibling, "grads_to_sibling")
        sums = [_chip_sums(o, g1, slots, 128, "chip_sum_" + nm) for o, g1, nm in zip([slabs_in] + own[1:], got1, gnames)]
        return _slab_exchange(sums, 3, _to_chips), lambda got2: (own, got1, got2)

    row2 = lambda a: a.reshape(1, -1)
    r = _local_step(x[0], positions.reshape(S, 1), loss_target[0], row2(pre_norm_g), row2(q_a_norm_g), row2(kv_a_norm_g),
                    row2(post_norm_g), w_in_f, (gather_rest, assemble_rest), reduce_grads)
    own, got1, got2 = r["reduced"]
    small = jnp.concatenate([r["dg_pre"][0:1], r["dg_post"][0:1], jnp.pad(r["dg_qa"][0:1], ((0, 0), (0, D - RQ))),
                             jnp.pad(r["dg_kva"][0:1], ((0, 0), (0, D - RKV))), jnp.pad(r["loss"][0:1], ((0, 0), (0, D - 128))),
                             jnp.zeros((3, D), F32)], axis=0)
    p_small, = _all_gather([small], "all_gather_small")
    pad8 = lambda a: jnp.zeros((8, 256), F32).at[0:3, :].set(a)
    params = [(w_in, m_w_in, v_w_in), (w_q_b, m_w_q_b, v_w_q_b), (w_kv_b, m_w_kv_b, v_w_kv_b), (conv_pad, pad8(m_conv_w), pad8(v_conv_w)),
              (w_o_mla, m_w_o_mla, v_w_o_mla), (w_o_conv, m_w_o_conv, v_w_o_conv), (w_out, m_w_out, v_w_out)]
    o_q, o_kv, o_cw, o_om, o_oc, o_out = [
        _reduce_adamw(o, g1, g2, slots, w, m, v, 128, "adamw_" + nm)
        for o, g1, g2, (w, m, v), nm in list(zip(own, got1, got2, params, gnames))[1:]]
    o_cw = [a[0:3] for a in o_cw]
    o_in = [a.T for a in _reduce_adamw_t(own[0], got1[0], got2[0], slots, w_in.T, m_w_in.T, v_w_in.T, 128, "adamw_w_in")]
    padv = lambda a: jnp.pad(a.reshape(1, -1), ((0, 0), (0, D - a.shape[0])))
    stack = lambda pre, post, qa, kva: jnp.concatenate([row2(pre), row2(post), padv(qa), padv(kva), jnp.zeros((4, D), F32)], axis=0)
    o_g = _sum_adamw(p_small, stack(pre_norm_g, post_norm_g, q_a_norm_g, kv_a_norm_g),
                     stack(m_pre_norm_g, m_post_norm_g, m_q_a_norm_g, m_kv_a_norm_g),
                     stack(v_pre_norm_g, v_post_norm_g, v_q_a_norm_g, v_kv_a_norm_g), 8, "adamw_gains")
    loss = o_g[0][4, 0]
    outs = {}
    for idx, kind in enumerate(("grad", "delta", "new_m", "new_v")):
        o = o_g[idx]
        outs[kind] = dict(pre_norm_g=o[0], w_in=o_in[idx], q_a_norm_g=o[2, 0:RQ], w_q_b=o_q[idx], kv_a_norm_g=o[3, 0:RKV],
                          w_kv_b=o_kv[idx], conv_w=o_cw[idx], w_o_mla=o_om[idx], w_o_conv=o_oc[idx], w_out=o_out[idx], post_norm_g=o[1])
    names = ["pre_norm_g", "w_in", "q_a_norm_g", "w_q_b", "kv_a_norm_g", "w_kv_b", "conv_w", "w_o_mla", "w_o_conv", "w_out", "post_norm_g"]
    return (loss, r["grad_x"][None], *[outs["grad"][n] for n in names], *[outs["delta"][n] for n in names],
            *[outs["new_m"][n] for n in names], *[outs["new_v"][n] for n in names])
```

```python
import functools
import math

import jax
import jax.numpy as jnp
from jax import lax
from jax.experimental import pallas as pl
from jax.experimental.pallas import tpu as pltpu

F32 = jnp.float32
BF16 = jnp.bfloat16

NDEV = 8
D = 2048
H = 16
DN = 128
DR = 64
DV = 128
RQ = 512
RKV = 512
HW = 256
ROPE_THETA = 10000.0
RMS_EPS = 1e-6
N_IN = 15424
SHARD_IN = N_IN // NDEV
SMALL = RQ + RKV + DR
NP = 7 * D + RQ + RKV + 128
SEG = dict(z_mla=0, c_in=1, b_gate=2, c_gate=3, z_conv=4, g_mla=5, g_conv=6)
OFF_QA = 7 * D
OFF_CKV = OFF_QA + RQ
OFF_KR = OFF_CKV + RKV
EXT = 2176
VMEM_CAP = 56 * 1024 * 1024

ADAM_LR = 0.001
ADAM_B1 = 0.9
ADAM_B2 = 0.999
ADAM_EPS = 1e-08
ADAM_WD = 0.01
ADAM_STEP = 10

LOG2E = math.log2(math.e)
SM_SCALE = 1.0 / math.sqrt(DN + DR)
QSCALE = SM_SCALE * LOG2E
NN = (((1,), (0,)), ((), ()))
NT = (((1,), (1,)), ((), ()))
TN = (((0,), (0,)), ((), ()))
MESH = pl.DeviceIdType.MESH


def _cparams(sem, vmem_bytes):
    return pltpu.CompilerParams(dimension_semantics=sem, vmem_limit_bytes=int(min(VMEM_CAP, max(vmem_bytes, 16 << 20))))


def _nbytes(shape, dtype):
    return math.prod(shape) * jnp.dtype(dtype).itemsize


class _Exchange:
    def __init__(self, arrays, out_shapes, sem_shapes, start, finish):
        self.arrays, self.out_shapes, self.sem_shapes, self.start, self.finish = arrays, out_shapes, sem_shapes, start, finish
        self.n_sems = len(sem_shapes)


def _matmul(a, b, *, mode, out_dtype, tm, tn, tk, name, m_outer=False, exchange=None):
    if mode == "nn":
        (M, K), (K2, N) = a.shape, b.shape
    elif mode == "nt":
        (M, K), (N, K2) = a.shape, b.shape
    else:
        (K, M), (K2, N) = a.shape, b.shape
    assert K == K2, (a.shape, b.shape, mode)
    tm, tn, tk = min(tm, M), min(tn, N), min(tk, K)
    assert M % tm == 0 and N % tn == 0 and K % tk == 0, (M, N, K, tm, tn, tk)
    ni, nj, nk = M // tm, N // tn, K // tk
    dims = dict(nn=NN, nt=NT, tn=TN)[mode]

    if m_outer:
        grid = (ni, nj, nk)
        ij = lambda g0, g1: (g0, g1)
    else:
        grid = (nj, ni, nk)
        ij = lambda g0, g1: (g1, g0)

    if mode == "tn":
        a_spec = pl.BlockSpec((tk, tm), lambda g0, g1, k: (k, ij(g0, g1)[0]))
        a_tile = (tk, tm)
    else:
        a_spec = pl.BlockSpec((tm, tk), lambda g0, g1, k: (ij(g0, g1)[0], k))
        a_tile = (tm, tk)
    if mode == "nt":
        b_spec = pl.BlockSpec((tn, tk), lambda g0, g1, k: (ij(g0, g1)[1], k))
    else:
        b_spec = pl.BlockSpec((tk, tn), lambda g0, g1, k: (k, ij(g0, g1)[1]))
    o_spec = pl.BlockSpec((tm, tn), lambda g0, g1, k: ij(g0, g1))

    n_in = len(exchange.arrays) if exchange else 0

    def body(a_ref, b_ref, *refs):
        x_in, o_ref, x_out, scratch = refs[:n_in], refs[n_in], refs[n_in + 1:2 * n_in + 1], refs[2 * n_in + 1:]
        if exchange:
            sems = scratch[len(scratch) - exchange.n_sems:]
            step = (pl.program_id(0) * grid[1] + pl.program_id(1)) * grid[2] + pl.program_id(2)

            @pl.when(step == 0)
            def _():
                exchange.start(x_in, x_out, sems)

        prod = lax.dot_general(a_ref[...], b_ref[...], dims, preferred_element_type=F32)
        if nk == 1:
            o_ref[...] = prod.astype(o_ref.dtype)
        else:
            acc_ref = scratch[0]
            k = pl.program_id(2)

            @pl.when(k == 0)
            def _():
                acc_ref[...] = prod

            @pl.when((k > 0) & (k < nk - 1))
            def _():
                acc_ref[...] += prod

            @pl.when(k == nk - 1)
            def _():
                o_ref[...] = (acc_ref[...] + prod).astype(o_ref.dtype)

        if exchange:
            @pl.when(step == grid[0] * grid[1] * grid[2] - 1)
            def _():
                exchange.finish(x_in, x_out, sems)

    vmem = 2 * (_nbytes(a_tile, a.dtype) + _nbytes((tk, tn), b.dtype) + _nbytes((tm, tn), out_dtype)) + 2 * _nbytes((tm, tn), F32)
    anyspec = pl.BlockSpec(memory_space=pl.ANY)
    outs = pl.pallas_call(
        body,
        grid=grid,
        in_specs=[a_spec, b_spec] + [anyspec] * n_in,
        out_specs=[o_spec] + [anyspec] * n_in,
        out_shape=[jax.ShapeDtypeStruct((M, N), out_dtype)] + (exchange.out_shapes if exchange else []),
        scratch_shapes=([] if nk == 1 else [pltpu.VMEM((tm, tn), F32)]) + (exchange.sem_shapes if exchange else []),
        compiler_params=_cparams(("arbitrary",) * 3 if exchange else ("parallel", "parallel", "arbitrary"), vmem + (8 << 20)),
        name=name,
    )(a, b, *(exchange.arrays if exchange else []))
    return (outs[0], outs[1:]) if exchange else outs[0]


def _in_dest(k):
    return SHARD_IN * k - SMALL


def _assemble_w_in(g):
    R = 128

    def body(g_ref, o_ref, ext, acc):
        ext[...] = jnp.zeros_like(ext)
        acc[...] = jnp.zeros_like(acc)
        lane = lax.broadcasted_iota(jnp.int32, (R, EXT), 1)
        ext[:, 0:SHARD_IN] = g_ref[0].astype(F32)
        v = ext[...]
        acc[:, OFF_QA:NP] = jnp.where(lane[:, 0:NP - OFF_QA] < SMALL, v[:, 0:NP - OFF_QA], 0.0)
        w = v[:, 1024:2048]
        w = pltpu.roll(w, 1024 - 64, 1)
        acc[:, 0:1024] = jnp.where(lane[:, 0:1024] < SHARD_IN - SMALL, w, 0.0)
        for k in range(1, NDEV):
            ext[:, 0:SHARD_IN] = g_ref[k].astype(F32)
            dest = _in_dest(k)
            t, o = dest // 128, dest % 128
            width = -(-(o + SHARD_IN) // 128) * 128
            v = pltpu.roll(ext[...], o, 1)[:, 0:width]
            acc[:, 128 * t:128 * t + width] += v
        o_ref[...] = acc[...].astype(BF16)

    return pl.pallas_call(
        body,
        grid=(D // R,),
        in_specs=[pl.BlockSpec((NDEV, R, SHARD_IN), lambda i: (0, i, 0))],
        out_specs=pl.BlockSpec((R, NP), lambda i: (i, 0)),
        out_shape=jax.ShapeDtypeStruct((D, NP), BF16),
        scratch_shapes=[pltpu.VMEM((R, EXT), F32), pltpu.VMEM((R, NP), F32)],
        compiler_params=_cparams(("parallel",), 40 << 20),
        name="assemble_w_in",
    )(g)


def _split_dw_in(dw, slots):
    R = 128

    def body(slots_ref, dw_ref, o_ref, ob_ref):
        def put(k, cols, v):
            ob_ref[k, :, cols] = v.astype(BF16)

            @pl.when(slots_ref[0] == k)
            def _():
                o_ref[:, cols] = v

        lane = lax.broadcasted_iota(jnp.int32, (R, 1024), 1)
        put(0, slice(0, 1024), dw_ref[:, OFF_QA:OFF_QA + 1024])
        tail = dw_ref[:, OFF_QA + 1024:NP]
        tail = jnp.concatenate([tail, jnp.zeros((R, 1024 - 128), F32)], axis=1)
        head = dw_ref[:, 0:1024]
        mixed = jnp.where(lane < 64, tail, pltpu.roll(head, 64, 1))
        put(0, slice(1024, SHARD_IN), mixed[:, 0:SHARD_IN - 1024])
        for k in range(1, NDEV):
            dest = _in_dest(k)
            t, o = dest // 128, dest % 128
            width = -(-(o + SHARD_IN) // 128) * 128
            v = dw_ref[:, 128 * t:128 * t + width]
            v = pltpu.roll(v, width - o, 1)
            put(k, slice(0, SHARD_IN), v[:, 0:SHARD_IN])

    return pl.pallas_call(
        body,
        grid_spec=pltpu.PrefetchScalarGridSpec(
            num_scalar_prefetch=1, grid=(D // R,),
            in_specs=[pl.BlockSpec((R, NP), lambda i, s: (i, 0))],
            out_specs=[pl.BlockSpec((R, SHARD_IN), lambda i, s: (i, 0)), pl.BlockSpec((NDEV, R, SHARD_IN), lambda i, s: (0, i, 0))]),
        out_shape=[jax.ShapeDtypeStruct((D, SHARD_IN), F32), jax.ShapeDtypeStruct((NDEV, D, SHARD_IN), BF16)],
        compiler_params=_cparams(("parallel",), 48 << 20),
        name="split_dw_in",
    )(slots, dw)


def _assemble_w_q(g):
    def body(g_ref, o_ref):
        lane = lax.broadcasted_iota(jnp.int32, (RQ, 128), 1)
        lo = lane < 64
        for k in range(NDEV):
            t0 = g_ref[k, :, 0:128].astype(F32)
            t1 = g_ref[k, :, 128:256].astype(F32)
            t2 = g_ref[k, :, 256:384].astype(F32)
            base = 2 * k * HW
            o_ref[:, base:base + 128] = t0.astype(BF16)
            o_ref[:, base + 128:base + 256] = jnp.where(lo, t1, 0.0).astype(BF16)
            o_ref[:, base + 256:base + 384] = pltpu.roll(jnp.where(lo, t2, t1), 64, 1).astype(BF16)
            o_ref[:, base + 384:base + 512] = jnp.where(lo, pltpu.roll(t2, 64, 1), 0.0).astype(BF16)

    return pl.pallas_call(
        body,
        out_shape=jax.ShapeDtypeStruct((RQ, H * HW), BF16),
        compiler_params=_cparams(None, 32 << 20),
        name="assemble_w_q",
    )(g)


def _split_dw_q(dw):
    def body(dw_ref, o_ref):
        lane = lax.broadcasted_iota(jnp.int32, (RQ, 128), 1)
        lo = lane < 64
        for k in range(NDEV):
            base = 2 * k * HW
            a = dw_ref[:, base:base + 128]
            b = dw_ref[:, base + 128:base + 256]
            c = pltpu.roll(dw_ref[:, base + 256:base + 384], 64, 1)
            d = pltpu.roll(dw_ref[:, base + 384:base + 512], 64, 1)
            o_ref[k, :, 0:128] = a
            o_ref[k, :, 128:256] = jnp.where(lo, b, c)
            o_ref[k, :, 256:384] = jnp.where(lo, c, d)

    return pl.pallas_call(
        body,
        out_shape=jax.ShapeDtypeStruct((NDEV, RQ, 384), F32),
        compiler_params=_cparams(None, 32 << 20),
        name="split_dw_q",
    )(dw)


def _concat_cols(g, dtype, name):
    n, R, C = g.shape

    def body(g_ref, o_ref):
        for k in range(n):
            o_ref[:, k * C:(k + 1) * C] = g_ref[k].astype(dtype)

    return pl.pallas_call(body, out_shape=jax.ShapeDtypeStruct((R, n * C), dtype), compiler_params=_cparams(None, 32 << 20), name=name)(g)


def _split_cols(x, n, name):
    R, NC = x.shape
    C = NC // n

    def body(x_ref, o_ref):
        for k in range(n):
            o_ref[k] = x_ref[:, k * C:(k + 1) * C]

    return pl.pallas_call(body, out_shape=jax.ShapeDtypeStruct((n, R, C), x.dtype), compiler_params=_cparams(None, 32 << 20), name=name)(x)


def _rms_scale(xf):
    return lax.rsqrt(jnp.mean(xf * xf, axis=-1, keepdims=True) + RMS_EPS)


def _prenorm(x, g, ts):
    S = x.shape[0]

    def body(x_ref, g_ref, h_ref, ht_ref):
        xf = x_ref[...]
        h = xf * _rms_scale(xf) * g_ref[...]
        h_ref[...] = h.astype(BF16)
        ht_ref[...] = jnp.transpose(h).astype(BF16)

    return pl.pallas_call(
        body,
        grid=(S // ts,),
        in_specs=[pl.BlockSpec((ts, D), lambda i: (i, 0)), pl.BlockSpec((1, D), lambda i: (0, 0))],
        out_specs=[pl.BlockSpec((ts, D), lambda i: (i, 0)), pl.BlockSpec((D, ts), lambda i: (0, i))],
        out_shape=[jax.ShapeDtypeStruct((S, D), BF16), jax.ShapeDtypeStruct((D, S), BF16)],
        compiler_params=_cparams(("parallel",), 32 << 20),
        name="prenorm",
    )(x, g)


def _inv_freq_tile():
    inv_freq = ROPE_THETA ** (-jnp.arange(0, DR, 2, dtype=F32) / DR)
    return jnp.tile(inv_freq, 4).reshape(1, 128)


def _rope_tables(pos_ref, freq_ref, ts):
    lane = lax.broadcasted_iota(jnp.int32, (ts, 128), 1)
    ang = pos_ref[...].astype(F32) * freq_ref[...]
    return jnp.cos(ang), jnp.sin(ang), lane


def _rope_swap(t, lane):
    return jnp.where(lane < 32, pltpu.roll(t, 96, 1), pltpu.roll(t, 32, 1))


def _qkv_prep(proj, pos, freq, g_qa, g_kva, w_q, w_kv, ts):
    S = proj.shape[0]

    def body(qa_ref, ckv_ref, kr_ref, pos_ref, freq_ref, gq_ref, gk_ref, wq_ref, wkv_ref, q_ref, k_ref, v_ref, vt_ref, qn_ref, kvn_ref):
        qa = _f32(qa_ref)
        qn = (qa * _rms_scale(qa) * gq_ref[...]).astype(BF16)
        ckv = _f32(ckv_ref)
        kvn = (ckv * _rms_scale(ckv) * gk_ref[...]).astype(BF16)
        qn_ref[...] = qn
        kvn_ref[...] = kvn
        cos, sin, lane = _rope_tables(pos_ref, freq_ref, ts)
        sgn_sin = jnp.where(lane < 32, -sin, sin)
        live = lane < DR
        kr = _f32(kr_ref)
        kr = jnp.where(live, kr * cos + _rope_swap(kr, lane) * sgn_sin, 0.0).astype(BF16)
        qf = jnp.dot(qn, wq_ref[...], preferred_element_type=F32)
        kvf = jnp.dot(kvn, wkv_ref[...], preferred_element_type=F32)
        for h in range(H):
            q_ref[h, :, 0:DN] = (qf[:, h * HW:h * HW + DN] * QSCALE).astype(BF16)
            t = qf[:, h * HW + DN:(h + 1) * HW]
            q_ref[h, :, DN:HW] = jnp.where(live, (t * cos + _rope_swap(t, lane) * sgn_sin) * QSCALE, 0.0).astype(BF16)
            k_ref[h, :, 0:DN] = kvf[:, h * HW:h * HW + DN].astype(BF16)
            k_ref[h, :, DN:HW] = kr
            vh = kvf[:, h * HW + DN:(h + 1) * HW]
            v_ref[h] = vh.astype(BF16)
            vt_ref[h] = jnp.transpose(vh).astype(BF16)

    row = lambda w, blk: pl.BlockSpec((ts, w), lambda i: (i, blk))
    full = lambda a: pl.BlockSpec(a.shape, lambda i: (0,) * a.ndim)
    return pl.pallas_call(
        body,
        grid=(S // ts,),
        in_specs=[row(RQ, OFF_QA // RQ), row(RKV, OFF_CKV // RKV), row(128, OFF_KR // 128),
                  pl.BlockSpec((ts, 1), lambda i: (i, 0)), full(freq), full(g_qa), full(g_kva), full(w_q), full(w_kv)],
        out_specs=[pl.BlockSpec((H, ts, HW), lambda i: (0, i, 0)), pl.BlockSpec((H, ts, HW), lambda i: (0, i, 0)),
                   pl.BlockSpec((H, ts, DV), lambda i: (0, i, 0)), pl.BlockSpec((H, DV, ts), lambda i: (0, 0, i)),
                   pl.BlockSpec((ts, RQ), lambda i: (i, 0)), pl.BlockSpec((ts, RKV), lambda i: (i, 0))],
        out_shape=[jax.ShapeDtypeStruct((H, S, HW), BF16), jax.ShapeDtypeStruct((H, S, HW), BF16),
                   jax.ShapeDtypeStruct((H, S, DV), BF16), jax.ShapeDtypeStruct((H, DV, S), BF16),
                   jax.ShapeDtypeStruct((S, RQ), BF16), jax.ShapeDtypeStruct((S, RKV), BF16)],
        compiler_params=_cparams(("parallel",), 48 << 20),
        name="qkv_prep",
    )(proj, proj, proj, pos, freq, g_qa, g_kva, w_q, w_kv)


def _col_to_row8(col, n):
    return jnp.transpose(jnp.broadcast_to(col, (n, 128)))[0:8, :]


def _causal_pairs(n, key_major):
    if key_major:
        pairs = [(i, j) for j in range(n) for i in range(j, n)]
    else:
        pairs = [(i, j) for i in range(n) for j in range(i + 1)]
    return jnp.array([p[0] for p in pairs], jnp.int32), jnp.array([p[1] for p in pairs], jnp.int32)


def _flash_fwd(q, k, vt, tq, hb=8):
    S = q.shape[1]
    nq = S // tq
    tsub = tq
    nsub = tq // tsub

    qi_tab, kj_tab = _causal_pairs(nq, key_major=False)

    def body(qi_ref, kj_ref, q_ref, k_ref, vt_ref, o_ref, lse_ref, m_sc, l_sc, acc_sc):
        p = pl.program_id(1)
        qi, kj = qi_ref[p], kj_ref[p]

        @pl.when(kj == 0)
        def _():
            m_sc[...] = jnp.full_like(m_sc, -jnp.inf)
            l_sc[...] = jnp.zeros_like(l_sc)
            acc_sc[...] = jnp.zeros_like(acc_sc)

        def step(diag):
            chains = [(h, u) for h in range(hb) for u in range(nsub)]

            def scores(h, u):
                return lax.dot_general(k_ref[h], q_ref[h, u * tsub:(u + 1) * tsub, :], NT, preferred_element_type=F32)

            st_next = scores(*chains[0])
            for ci, (h, u) in enumerate(chains):
                st = st_next
                if ci + 1 < len(chains):
                    st_next = scores(*chains[ci + 1])
                cols = slice(u * tsub, (u + 1) * tsub)
                if diag:
                    r = lax.broadcasted_iota(jnp.int32, (tq, tsub), 0)
                    c = lax.broadcasted_iota(jnp.int32, (tq, tsub), 1) + u * tsub
                    st = jnp.where(r <= c, st, -jnp.inf)
                m_prev = m_sc[h, 0:1, cols]
                m_new = jnp.maximum(m_prev, jnp.max(st, axis=0, keepdims=True))
                alpha = jnp.exp2(m_prev - m_new)
                pt = jnp.exp2(st - m_new)
                l_sc[h, :, cols] = jnp.broadcast_to(alpha * l_sc[h, 0:1, cols] + jnp.sum(pt, axis=0, keepdims=True), (8, tsub))
                m_sc[h, :, cols] = jnp.broadcast_to(m_new, (8, tsub))
                acc_sc[h, :, cols] = alpha * acc_sc[h, :, cols] + jnp.dot(vt_ref[h], pt.astype(BF16), preferred_element_type=F32)

        @pl.when(kj < qi)
        def _():
            step(False)

        @pl.when(kj == qi)
        def _():
            step(True)
            for h in range(hb):
                l = l_sc[h, 0:1, :]
                o_ref[:, h * DV:(h + 1) * DV] = jnp.transpose(acc_sc[h] / l).astype(BF16)
                lse_ref[h] = m_sc[h] + jnp.log2(l_sc[h])

    return pl.pallas_call(
        body,
        grid_spec=pltpu.PrefetchScalarGridSpec(
            num_scalar_prefetch=2,
            grid=(H // hb, len(qi_tab)),
            in_specs=[pl.BlockSpec((hb, tq, HW), lambda h, p, qi, kj: (h, qi[p], 0)),
                      pl.BlockSpec((hb, tq, HW), lambda h, p, qi, kj: (h, kj[p], 0)),
                      pl.BlockSpec((hb, DV, tq), lambda h, p, qi, kj: (h, 0, kj[p]))],
            out_specs=[pl.BlockSpec((tq, hb * DV), lambda h, p, qi, kj: (qi[p], h)),
                       pl.BlockSpec((hb, 8, tq), lambda h, p, qi, kj: (h, 0, qi[p]))],
            scratch_shapes=[pltpu.VMEM((hb, 8, tq), F32), pltpu.VMEM((hb, 8, tq), F32), pltpu.VMEM((hb, DV, tq), F32)],
        ),
        out_shape=[jax.ShapeDtypeStruct((S, H * DV), BF16), jax.ShapeDtypeStruct((H, 8, S), F32)],
        compiler_params=_cparams(("parallel", "arbitrary"), 40 << 20),
        name="flash_fwd",
    )(qi_tab, kj_tab, q, k, vt)


def _sigmoid(x):
    return 1.0 / (1.0 + jnp.exp(-x))


HALO = 16


def _f32(ref):
    return ref[...].astype(F32)


def _shift_rows(u, prev, n, first):
    ts = u.shape[0]
    row = lax.broadcasted_iota(jnp.int32, u.shape, 0)
    out = pltpu.roll(u, n, 0)
    for j in range(n):
        halo = jnp.where(first, 0.0, prev[HALO - n + j:HALO - n + j + 1, :])
        out = jnp.where(row == j, halo, out)
    return out


def _gates_fwd(proj, attn, conv_w, ts):
    S = proj.shape[0]

    def body(attn_ref, zm_ref, cin_ref, bg_ref, cg_ref, zc_ref, cin_p, cg_p, w_ref, am_ref, ac_ref):
        first = pl.program_id(0) == 0
        zm = _f32(zm_ref)
        am_ref[...] = (_f32(attn_ref) * (zm * _sigmoid(zm))).astype(BF16)
        u = _f32(cg_ref) * _f32(cin_ref)
        up = _f32(cg_p) * _f32(cin_p)
        w = w_ref[...]
        conv = w[0:1, :] * _shift_rows(u, up, 2, first) + w[1:2, :] * _shift_rows(u, up, 1, first) + w[2:3, :] * u
        zc = _f32(zc_ref)
        ac_ref[...] = (_f32(bg_ref) * conv * (zc * _sigmoid(zc))).astype(BF16)

    seg = lambda name: pl.BlockSpec((ts, D), lambda i: (i, SEG[name]))
    prev = lambda name: pl.BlockSpec((HALO, D), lambda i: (jnp.maximum(i * (ts // HALO) - 1, 0), SEG[name]))
    return pl.pallas_call(
        body,
        grid=(S // ts,),
        in_specs=[pl.BlockSpec((ts, D), lambda i: (i, 0)), seg("z_mla"), seg("c_in"), seg("b_gate"), seg("c_gate"), seg("z_conv"),
                  prev("c_in"), prev("c_gate"), pl.BlockSpec((8, D), lambda i: (0, 0))],
        out_specs=[pl.BlockSpec((ts, D), lambda i: (i, 0))] * 2,
        out_shape=[jax.ShapeDtypeStruct((S, D), BF16)] * 2,
        compiler_params=_cparams(("arbitrary",), 48 << 20),
        name="gates_fwd",
    )(attn, proj, proj, proj, proj, proj, proj, proj, conv_w)


def _merge_fwd(proj, y_mla, y_conv, ts):
    S = proj.shape[0]

    def body(gm_ref, gc_ref, ym_ref, yc_ref, o_ref):
        o_ref[...] = (_sigmoid(_f32(gm_ref)) * _f32(ym_ref) + _sigmoid(_f32(gc_ref)) * _f32(yc_ref)).astype(BF16)

    seg = lambda name: pl.BlockSpec((ts, D), lambda i: (i, SEG[name]))
    row = pl.BlockSpec((ts, D), lambda i: (i, 0))
    return pl.pallas_call(
        body, grid=(S // ts,), in_specs=[seg("g_mla"), seg("g_conv"), row, row], out_specs=row,
        out_shape=jax.ShapeDtypeStruct((S, D), BF16), compiler_params=_cparams(("parallel",), 32 << 20), name="merge_fwd",
    )(proj, proj, y_mla, y_conv)


def _post_loss(out, x, target, g_post, ts):
    S = out.shape[0]

    def body(o_ref, x_ref, t_ref, g_ref, dy_ref, do_ref, dg_ref, loss_ref):
        i = pl.program_id(0)
        o = o_ref[...]
        r = _rms_scale(o)
        n = o * r
        g = g_ref[...]
        err = x_ref[...] + n * g - t_ref[...]
        dy = err * (1.0 / D)
        dy_ref[...] = dy
        dn = dy * g
        do_ref[...] = (r * (dn - n * jnp.mean(dn * n, axis=-1, keepdims=True))).astype(BF16)
        dg = jnp.sum(dy * n, axis=0, keepdims=True)
        part = jnp.sum(jnp.sum(err * err, axis=0, keepdims=True), axis=1, keepdims=True) * (0.5 / D)

        @pl.when(i == 0)
        def _():
            dg_ref[...] = jnp.zeros_like(dg_ref)
            loss_ref[...] = jnp.zeros_like(loss_ref)

        dg_ref[0:1, :] += dg
        loss_ref[...] += jnp.broadcast_to(part, loss_ref.shape)

    row = pl.BlockSpec((ts, D), lambda i: (i, 0))
    return pl.pallas_call(
        body, grid=(S // ts,),
        in_specs=[row, row, row, pl.BlockSpec((1, D), lambda i: (0, 0))],
        out_specs=[row, row, pl.BlockSpec((8, D), lambda i: (0, 0)), pl.BlockSpec((8, 128), lambda i: (0, 0))],
        out_shape=[jax.ShapeDtypeStruct((S, D), F32), jax.ShapeDtypeStruct((S, D), BF16),
                   jax.ShapeDtypeStruct((8, D), F32), jax.ShapeDtypeStruct((8, 128), F32)],
        compiler_params=_cparams(("arbitrary",), 40 << 20), name="post_loss",
    )(out, x, target, g_post)


def _merge_bwd(proj, y_mla, y_conv, dmerged, ts):
    S = proj.shape[0]

    def body(gm_ref, gc_ref, ym_ref, yc_ref, dm_ref, dym_ref, dyc_ref, dgm_ref, dgc_ref):
        dm = _f32(dm_ref)
        sm = _sigmoid(_f32(gm_ref))
        sc = _sigmoid(_f32(gc_ref))
        dym_ref[...] = (dm * sm).astype(BF16)
        dyc_ref[...] = (dm * sc).astype(BF16)
        dgm_ref[...] = (dm * _f32(ym_ref) * (sm * (1.0 - sm))).astype(BF16)
        dgc_ref[...] = (dm * _f32(yc_ref) * (sc * (1.0 - sc))).astype(BF16)

    seg = lambda name: pl.BlockSpec((ts, D), lambda i: (i, SEG[name]))
    row = pl.BlockSpec((ts, D), lambda i: (i, 0))
    return pl.pallas_call(
        body, grid=(S // ts,), in_specs=[seg("g_mla"), seg("g_conv"), row, row, row], out_specs=[row] * 4,
        out_shape=[jax.ShapeDtypeStruct((S, D), BF16)] * 4, compiler_params=_cparams(("parallel",), 40 << 20), name="merge_bwd",
    )(proj, proj, y_mla, y_conv, dmerged)


def _gates_bwd(proj, attn, da_mla, da_conv, conv_w, ts):
    S = proj.shape[0]
    nblk = S // ts

    def body(attn_ref, zm_ref, cin_ref, bg_ref, cg_ref, zc_ref, dam_ref, dac_ref, cin_p, cg_p, bg_n, zc_n, dac_n, w_ref,
             dattn_ref, delta_ref, dzm_ref, dcin_ref, dbg_ref, dcg_ref, dzc_ref, dw_ref):
        i = pl.program_id(0)
        first = i == 0
        last = i == nblk - 1

        @pl.when(first)
        def _():
            dw_ref[...] = jnp.zeros_like(dw_ref)

        row = lax.broadcasted_iota(jnp.int32, (ts, DV), 0)
        for c in range(D // DV):
            cols = slice(c * DV, (c + 1) * DV)
            ld = lambda ref: ref[:, cols].astype(F32)
            zm = ld(zm_ref)
            sg = _sigmoid(zm)
            attn = ld(attn_ref)
            dam = ld(dam_ref)
            dattn = dam * (zm * sg)
            dattn_ref[:, cols] = dattn.astype(BF16)
            dzm_ref[:, cols] = (dam * attn * (sg * (1.0 + zm * (1.0 - sg)))).astype(BF16)
            delta_ref[c] = _col_to_row8(jnp.sum(dattn * attn, axis=1, keepdims=True), ts)
            w = w_ref[:, cols]
            cin, cg, bg, zc = ld(cin_ref), ld(cg_ref), ld(bg_ref), ld(zc_ref)
            u = cg * cin
            up = ld(cg_p) * ld(cin_p)
            u1 = _shift_rows(u, up, 1, first)
            u2 = _shift_rows(u, up, 2, first)
            conv = w[0:1, :] * u2 + w[1:2, :] * u1 + w[2:3, :] * u
            sgc = _sigmoid(zc)
            siluc = zc * sgc
            dac = ld(dac_ref)
            dbg_ref[:, cols] = (dac * conv * siluc).astype(BF16)
            dzc_ref[:, cols] = (dac * bg * conv * (sgc * (1.0 + zc * (1.0 - sgc)))).astype(BF16)
            dconv = dac * bg * siluc
            zn = ld(zc_n)
            dconv_n = jnp.where(last, 0.0, ld(dac_n) * ld(bg_n) * (zn * _sigmoid(zn)))
            d1 = jnp.where(row == ts - 1, dconv_n[0:1, :], pltpu.roll(dconv, ts - 1, 0))
            d2 = jnp.where(row == ts - 1, dconv_n[1:2, :], jnp.where(row == ts - 2, dconv_n[0:1, :], pltpu.roll(dconv, ts - 2, 0)))
            du = w[2:3, :] * dconv + w[1:2, :] * d1 + w[0:1, :] * d2
            dcg_ref[:, cols] = (du * cin).astype(BF16)
            dcin_ref[:, cols] = (du * cg).astype(BF16)
            dw_ref[0:1, cols] += jnp.sum(dconv * u2, axis=0, keepdims=True)
            dw_ref[1:2, cols] += jnp.sum(dconv * u1, axis=0, keepdims=True)
            dw_ref[2:3, cols] += jnp.sum(dconv * u, axis=0, keepdims=True)

    seg = lambda name: pl.BlockSpec((ts, D), lambda i: (i, SEG[name]))
    prev = lambda name: pl.BlockSpec((HALO, D), lambda i: (jnp.maximum(i * (ts // HALO) - 1, 0), SEG[name]))
    nxt = lambda blk: pl.BlockSpec((HALO, D), lambda i: (jnp.minimum((i + 1) * (ts // HALO), S // HALO - 1), blk))
    row = pl.BlockSpec((ts, D), lambda i: (i, 0))
    return pl.pallas_call(
        body, grid=(nblk,),
        in_specs=[row, seg("z_mla"), seg("c_in"), seg("b_gate"), seg("c_gate"), seg("z_conv"), row, row,
                  prev("c_in"), prev("c_gate"), nxt(SEG["b_gate"]), nxt(SEG["z_conv"]), nxt(0), pl.BlockSpec((8, D), lambda i: (0, 0))],
        out_specs=[row, pl.BlockSpec((H, 8, ts), lambda i: (0, 0, i)), row, row, row, row, row, pl.BlockSpec((8, D), lambda i: (0, 0))],
        out_shape=[jax.ShapeDtypeStruct((S, D), BF16), jax.ShapeDtypeStruct((H, 8, S), F32)] + [jax.ShapeDtypeStruct((S, D), BF16)] * 5
        + [jax.ShapeDtypeStruct((8, D), F32)],
        compiler_params=_cparams(("arbitrary",), 56 << 20), name="gates_bwd",
    )(attn, proj, proj, proj, proj, proj, da_mla, da_conv, proj, proj, proj, proj, da_conv, conv_w)


def _flash_bwd(q, k, v, do, lse, delta, tq, hb=2):
    S = q.shape[1]
    nq = S // tq
    scale = SM_SCALE
    tsub = min(256, tq)
    nsub = tq // tsub

    qi_tab, kj_tab = _causal_pairs(nq, key_major=True)
    npairs = nq * (nq + 1) // 2

    def body(qi_ref, kj_ref, q_ref, k_ref, v_ref, do_ref, lse_ref, dl_ref, dq_ref, dk_ref, dv_ref, dk_sc, dv_sc, dq_sc):
        p = pl.program_id(1)
        qi, kj = qi_ref[p], kj_ref[p]

        @pl.when(p == 0)
        def _():
            dq_sc[...] = jnp.zeros_like(dq_sc)

        @pl.when(qi == kj)
        def _():
            dk_sc[...] = jnp.zeros_like(dk_sc)
            dv_sc[...] = jnp.zeros_like(dv_sc)

        def step(diag):
            chains = [(h, u) for h in range(hb) for u in range(nsub)]

            def first_matmuls(h, u):
                sub = slice(u * tsub, (u + 1) * tsub)
                st = lax.dot_general(k_ref[h], q_ref[h, sub, :], NT, preferred_element_type=F32)
                dpt = lax.dot_general(v_ref[h], do_ref[sub, h * DV:(h + 1) * DV], NT, preferred_element_type=F32)
                return st, dpt

            nxt = first_matmuls(*chains[0])
            for ci, (h, u) in enumerate(chains):
                st, dpt = nxt
                if ci + 1 < len(chains):
                    nxt = first_matmuls(*chains[ci + 1])
                sub = slice(u * tsub, (u + 1) * tsub)
                pt = jnp.exp2(st - lse_ref[h, 0:1, sub])
                if diag:
                    r = lax.broadcasted_iota(jnp.int32, (tq, tsub), 0)
                    c = lax.broadcasted_iota(jnp.int32, (tq, tsub), 1) + u * tsub
                    pt = jnp.where(r <= c, pt, 0.0)
                dst = (pt * (dpt - dl_ref[h, 0:1, sub])).astype(BF16)
                dv_sc[h] += jnp.dot(pt.astype(BF16), do_ref[sub, h * DV:(h + 1) * DV], preferred_element_type=F32)
                dk_sc[h] += jnp.dot(dst, q_ref[h, sub, :], preferred_element_type=F32)
                rows = pl.ds(pl.multiple_of(qi * tq + u * tsub, tsub), tsub)
                dq_sc[h, rows, :] += lax.dot_general(dst, k_ref[h], TN, preferred_element_type=F32)

        @pl.when(qi > kj)
        def _():
            step(False)

        @pl.when(qi == kj)
        def _():
            step(True)

        @pl.when(qi == nq - 1)
        def _():
            dk_ref[...] = (dk_sc[...] * (1.0 / LOG2E)).astype(BF16)
            dv_ref[...] = dv_sc[...].astype(BF16)

        @pl.when(p == npairs - 1)
        def _():
            dq_ref[...] = (dq_sc[...] * scale).astype(BF16)

    return pl.pallas_call(
        body,
        grid_spec=pltpu.PrefetchScalarGridSpec(
            num_scalar_prefetch=2,
            grid=(H // hb, npairs),
            in_specs=[pl.BlockSpec((hb, tq, HW), lambda h, p, qi, kj: (h, qi[p], 0)),
                      pl.BlockSpec((hb, tq, HW), lambda h, p, qi, kj: (h, kj[p], 0)),
                      pl.BlockSpec((hb, tq, DV), lambda h, p, qi, kj: (h, kj[p], 0)),
                      pl.BlockSpec((tq, hb * DV), lambda h, p, qi, kj: (qi[p], h)),
                      pl.BlockSpec((hb, 8, tq), lambda h, p, qi, kj: (h, 0, qi[p])),
                      pl.BlockSpec((hb, 8, tq), lambda h, p, qi, kj: (h, 0, qi[p]))],
            out_specs=[pl.BlockSpec((hb, S, HW), lambda h, p, qi, kj: (h, 0, 0)),
                       pl.BlockSpec((hb, tq, HW), lambda h, p, qi, kj: (h, kj[p], 0)),
                       pl.BlockSpec((hb, tq, DV), lambda h, p, qi, kj: (h, kj[p], 0))],
            scratch_shapes=[pltpu.VMEM((hb, tq, HW), F32), pltpu.VMEM((hb, tq, DV), F32), pltpu.VMEM((hb, S, HW), F32)],
        ),
        out_shape=[jax.ShapeDtypeStruct((H, S, HW), BF16), jax.ShapeDtypeStruct((H, S, HW), BF16), jax.ShapeDtypeStruct((H, S, DV), BF16)],
        compiler_params=_cparams(("parallel", "arbitrary"), VMEM_CAP),
        name="flash_bwd",
    )(qi_tab, kj_tab, q, k, v, do, lse, delta)


def _rms_bwd(xf, g, dn_out):
    r = _rms_scale(xf)
    n = xf * r
    dn = dn_out * g
    dx = r * (dn - n * jnp.mean(dn * n, axis=-1, keepdims=True))
    return dx, jnp.sum(dn_out * n, axis=0, keepdims=True)


def _qkv_bwd(proj, pos, freq, g_qa, g_kva, w_q, w_kv, dq, dk, dv, segments, ts):
    S = proj.shape[0]
    nseg = len(segments)

    def body(qa_ref, ckv_ref, pos_ref, freq_ref, gq_ref, gk_ref, wq_ref, wkv_ref, dq_ref, dk_ref, dv_ref, *rest):
        seg_refs, (dqp_ref, dkvp_ref, dproj_ref, dgq_ref, dgk_ref) = rest[:nseg], rest[nseg:]
        for j, seg_ref in enumerate(seg_refs):
            dproj_ref[:, j * D:(j + 1) * D] = seg_ref[...]
        dqa_ref = dproj_ref.at[:, OFF_QA:OFF_QA + RQ]
        dckv_ref = dproj_ref.at[:, OFF_CKV:OFF_CKV + RKV]
        dkr_ref = dproj_ref.at[:, OFF_KR:OFF_KR + 128]
        i = pl.program_id(0)
        cos, sin, lane = _rope_tables(pos_ref, freq_ref, ts)
        sgn_sin = jnp.where(lane < 32, sin, -sin)
        live = lane < DR
        kr_sum = jnp.zeros((ts, 128), F32)
        for h in range(H):
            dqp_ref[:, h * HW:h * HW + DN] = dq_ref[h, :, 0:DN]
            t = dq_ref[h, :, DN:HW].astype(F32)
            dqp_ref[:, h * HW + DN:(h + 1) * HW] = jnp.where(live, t * cos + _rope_swap(t, lane) * sgn_sin, 0.0).astype(BF16)
            dkvp_ref[:, h * HW:h * HW + DN] = dk_ref[h, :, 0:DN]
            dkvp_ref[:, h * HW + DN:(h + 1) * HW] = dv_ref[h]
            kr_sum = kr_sum + dk_ref[h, :, DN:HW].astype(F32)
        dkr_ref[...] = jnp.where(live, kr_sum * cos + _rope_swap(kr_sum, lane) * sgn_sin, 0.0).astype(BF16)
        dqn = lax.dot_general(dqp_ref[...], wq_ref[...], NT, preferred_element_type=F32)
        dkvn = lax.dot_general(dkvp_ref[...], wkv_ref[...], NT, preferred_element_type=F32)
        dqa, dgq = _rms_bwd(_f32(qa_ref), gq_ref[...], dqn)
        dckv, dgk = _rms_bwd(_f32(ckv_ref), gk_ref[...], dkvn)
        dqa_ref[...] = dqa.astype(BF16)
        dckv_ref[...] = dckv.astype(BF16)

        @pl.when(i == 0)
        def _():
            dgq_ref[...] = jnp.zeros_like(dgq_ref)
            dgk_ref[...] = jnp.zeros_like(dgk_ref)

        dgq_ref[0:1, :] += dgq
        dgk_ref[0:1, :] += dgk

    rowb = lambda w, blk: pl.BlockSpec((ts, w), lambda i: (i, blk))
    full = lambda a: pl.BlockSpec(a.shape, lambda i: (0,) * a.ndim)
    heads = lambda w: pl.BlockSpec((H, ts, w), lambda i: (0, i, 0))
    return pl.pallas_call(
        body, grid=(S // ts,),
        in_specs=[rowb(RQ, OFF_QA // RQ), rowb(RKV, OFF_CKV // RKV), pl.BlockSpec((ts, 1), lambda i: (i, 0)), full(freq), full(g_qa), full(g_kva),
                  full(w_q), full(w_kv), heads(HW), heads(HW), heads(DV)] + [rowb(D, 0)] * nseg,
        out_specs=[rowb(H * HW, 0), rowb(H * HW, 0), rowb(NP, 0),
                   pl.BlockSpec((8, RQ), lambda i: (0, 0)), pl.BlockSpec((8, RKV), lambda i: (0, 0))],
        out_shape=[jax.ShapeDtypeStruct((S, H * HW), BF16), jax.ShapeDtypeStruct((S, H * HW), BF16), jax.ShapeDtypeStruct((S, NP), BF16),
                   jax.ShapeDtypeStruct((8, RQ), F32), jax.ShapeDtypeStruct((8, RKV), F32)],
        compiler_params=_cparams(("arbitrary",), 56 << 20), name="qkv_bwd",
    )(proj, proj, pos, freq, g_qa, g_kva, w_q, w_kv, dq, dk, dv, *segments)


def _prenorm_bwd(x, g, dh, dy, ts):
    S = x.shape[0]

    def body(x_ref, g_ref, dh_ref, dy_ref, gx_ref, dg_ref):
        dx, dg = _rms_bwd(x_ref[...], g_ref[...], dh_ref[...])
        gx_ref[...] = dy_ref[...] + dx

        @pl.when(pl.program_id(0) == 0)
        def _():
            dg_ref[...] = jnp.zeros_like(dg_ref)

        dg_ref[0:1, :] += dg

    row = pl.BlockSpec((ts, D), lambda i: (i, 0))
    return pl.pallas_call(
        body, grid=(S // ts,), in_specs=[row, pl.BlockSpec((1, D), lambda i: (0, 0)), row, row],
        out_specs=[row, pl.BlockSpec((8, D), lambda i: (0, 0))],
        out_shape=[jax.ShapeDtypeStruct((S, D), F32), jax.ShapeDtypeStruct((8, D), F32)],
        compiler_params=_cparams(("arbitrary",), 40 << 20), name="prenorm_bwd",
    )(x, g, dh, dy)


def _local_step(x, pos, target, g_pre, g_qa, g_kva, g_post, w_in, other_weights, reduce_grads=None):
    S = x.shape[0]
    ts = min(256, S)
    tq = min(512, S)
    tm = min(1024, S)
    mm = functools.partial(_matmul, tm=tm)
    freq = _inv_freq_tile()

    h, ht = _prenorm(x, g_pre, ts)
    if len(other_weights) == 2:
        gather, assemble = other_weights
        proj, gathered = _matmul(h, w_in, mode="nn", out_dtype=BF16, tm=1024, tn=1408, tk=D, name="mm_proj", exchange=gather)
        other_weights = assemble(*gathered)
    else:
        proj = _matmul(h, w_in, mode="nn", out_dtype=BF16, tm=1024, tn=1408, tk=D, name="mm_proj")
    w_q, w_kv, conv_w8, w_o_mla, w_o_conv, w_out = other_weights
    q, k, v, vt, qn, kvn = _qkv_prep(proj, pos, freq, g_qa, g_kva, w_q, w_kv, ts)
    attn, lse = _flash_fwd(q, k, vt, tq)
    a_mla, a_conv = _gates_fwd(proj, attn, conv_w8, ts)
    y_mla = mm(a_mla, w_o_mla, mode="nn", out_dtype=BF16, tn=D, tk=D, name="mm_y_mla")
    y_conv = mm(a_conv, w_o_conv, mode="nn", out_dtype=BF16, tn=D, tk=D, name="mm_y_conv")
    merged = _merge_fwd(proj, y_mla, y_conv, ts)
    out = mm(merged, w_out, mode="nn", out_dtype=F32, tn=D, tk=D, name="mm_out")
    dy, dout, dg_post, loss = _post_loss(out, x, target, g_post, ts)

    dmerged = mm(dout, w_out, mode="nt", out_dtype=BF16, tn=D, tk=D, name="mm_dmerged")
    dw_out = _matmul(merged, dout, mode="tn", out_dtype=F32, tm=1024, tn=1024, tk=2048, name="mm_dw_out")
    dy_mla, dy_conv, dg_mla, dg_conv = _merge_bwd(proj, y_mla, y_conv, dmerged, ts)
    da_mla = mm(dy_mla, w_o_mla, mode="nt", out_dtype=BF16, tn=D, tk=D, name="mm_da_mla")
    da_conv = mm(dy_conv, w_o_conv, mode="nt", out_dtype=BF16, tn=D, tk=D, name="mm_da_conv")
    dw_o_mla = _matmul(a_mla, dy_mla, mode="tn", out_dtype=F32, tm=1024, tn=1024, tk=2048, name="mm_dw_o_mla")
    dw_o_conv = _matmul(a_conv, dy_conv, mode="tn", out_dtype=F32, tm=1024, tn=1024, tk=2048, name="mm_dw_o_conv")
    dattn, delta, dz_mla, dc_in, db_gate, dc_gate, dz_conv, dconv_w = _gates_bwd(proj, attn, da_mla, da_conv, conv_w8, min(128, S))
    dq, dk, dv = _flash_bwd(q, k, v, dattn, lse, delta, tq)
    dqp, dkvp, dproj, dg_qa, dg_kva = _qkv_bwd(proj, pos, freq, g_qa, g_kva, w_q, w_kv, dq, dk, dv,
                                               [dz_mla, dc_in, db_gate, dc_gate, dz_conv, dg_mla, dg_conv], min(128, S))
    dw_q = _matmul(qn, dqp, mode="tn", out_dtype=F32, tm=RQ, tn=1024, tk=2048, name="mm_dw_q")
    dw_kv = _matmul(kvn, dkvp, mode="tn", out_dtype=F32, tm=RKV, tn=1024, tk=2048, name="mm_dw_kv")
    dw_in = _matmul(ht, dproj, mode="nn", out_dtype=F32, tm=1024, tn=1408, tk=2048, name="mm_dw_in")
    res = dict(dw_in=dw_in, dw_q=dw_q, dw_kv=dw_kv, dconv_w=dconv_w, dw_o_mla=dw_o_mla, dw_o_conv=dw_o_conv, dw_out=dw_out)
    if reduce_grads is None:
        dh = _matmul(dproj, w_in, mode="nt", out_dtype=F32, tm=1024, tn=D, tk=1408, name="mm_dh")
    else:
        exchange, finish = reduce_grads(res)
        dh, got = _matmul(dproj, w_in, mode="nt", out_dtype=F32, tm=1024, tn=D, tk=1408, name="mm_dh", exchange=exchange)
        res["reduced"] = finish(got)
    grad_x, dg_pre = _prenorm_bwd(x, g_pre, dh, dy, ts)
    res.update(loss=loss, grad_x=grad_x, dg_pre=dg_pre, dg_qa=dg_qa, dg_kva=dg_kva, dg_post=dg_post)
    return res


def _my_id():
    return lax.axis_index("x") * 4 + lax.axis_index("y") * 2 + lax.axis_index("c")


def _place():
    x, y, c = lax.axis_index("x"), lax.axis_index("y"), lax.axis_index("c")
    return (x, y, c), (x, y, 1 - c), [(1 - x, y), (x, 1 - y), (1 - x, 1 - y)]


def _slot(px, py, pc):
    return 4 * px + 2 * py + pc


def _all_gather(arrays, name):
    n = len(arrays)

    def body(*refs):
        ins, outs = refs[:n], refs[n:2 * n]
        send_sems, recv_sems, local_sems = refs[2 * n:]
        me, sib, chips = _place()
        c = me[2]

        def copy(a, k, block, to, src=None):
            rows = outs[a].at[_slot(*block)]
            return pltpu.make_async_remote_copy(src_ref=rows if src is None else src, dst_ref=rows, send_sem=send_sems.at[a, k],
                                                recv_sem=recv_sems.at[a, k], device_id=to, device_id_type=MESH)

        local = [pltpu.make_async_copy(ins[a], outs[a].at[_slot(*me)], local_sems.at[a]) for a in range(n)]
        for cp in local:
            cp.start()
        sends = []
        for a in range(n):
            sends.append(copy(a, 0, me, sib, src=ins[a]))
            sends += [copy(a, 1 + j, me, (*chip, c), src=ins[a]) for j, chip in enumerate(chips)]
        for cp in sends:
            cp.start()
        for a in range(n):
            for j, chip in enumerate(chips):
                copy(a, 1 + j, (*chip, c), me).wait_recv()
                fwd = copy(a, 4 + j, (*chip, c), sib)
                fwd.start()
                sends.append(fwd)
        for a in range(n):
            copy(a, 0, sib, me).wait_recv()
            for j, chip in enumerate(chips):
                copy(a, 4 + j, (*chip, 1 - c), me).wait_recv()
        for cp in sends:
            cp.wait_send()
        for cp in local:
            cp.wait()

    anyspec = pl.BlockSpec(memory_space=pl.ANY)
    return pl.pallas_call(
        body,
        in_specs=[anyspec] * n,
        out_specs=[anyspec] * n,
        out_shape=[jax.ShapeDtypeStruct((NDEV,) + a.shape, a.dtype) for a in arrays],
        scratch_shapes=[pltpu.SemaphoreType.DMA((n, NDEV - 1)), pltpu.SemaphoreType.DMA((n, NDEV - 1)), pltpu.SemaphoreType.DMA((n,))],
        name=name,
    )(*arrays)


def _all_gather_relayed(x, name):
    R, C = x.shape
    half = R // 2

    def body(x_ref, out_ref, send_sems, recv_sems, local_sem):
        (mx, my, c), sib, _ = _place()
        xn, yn, dg = (1 - mx, my, c), (mx, 1 - my, c), (1 - mx, 1 - my, c)

        def rows(block, h):
            return out_ref.at[_slot(*block), pl.ds(h * half, half), :]

        def copy(k, dst, to, src=None):
            return pltpu.make_async_remote_copy(src_ref=dst if src is None else src, dst_ref=dst, send_sem=send_sems.at[k],
                                                recv_sem=recv_sems.at[k], device_id=to, device_id_type=MESH)

        me = (mx, my, c)
        local = pltpu.make_async_copy(x_ref, out_ref.at[_slot(*me)], local_sem)
        local.start()
        sends = [copy(0, out_ref.at[_slot(*me)], sib, src=x_ref)]
        for h in range(2):
            sends.append(copy(1 + h, rows(me, h), xn, src=x_ref.at[pl.ds(h * half, half), :]))
            sends.append(copy(3 + h, rows(me, h), yn, src=x_ref.at[pl.ds(h * half, half), :]))
        for cp in sends:
            cp.start()
        copy(3, rows(yn, 0), me).wait_recv()
        relay_x = copy(5, rows(yn, 0), xn)
        relay_x.start()
        copy(1, rows(xn, 0), me).wait_recv()
        copy(2, rows(xn, 1), me).wait_recv()
        relay_y = copy(6, rows(xn, 1), yn)
        relay_y.start()
        sends += [relay_x, relay_y]

        def to_sibling(k, block):
            fwd = copy(k, out_ref.at[_slot(*block)], sib)
            fwd.start()
            sends.append(fwd)

        to_sibling(7, xn)
        copy(4, rows(yn, 1), me).wait_recv()
        to_sibling(8, yn)
        copy(5, rows(dg, 0), me).wait_recv()
        copy(6, rows(dg, 1), me).wait_recv()
        to_sibling(9, dg)
        copy(0, out_ref.at[_slot(*sib)], me).wait_recv()
        for k, block in ((7, xn), (8, yn), (9, dg)):
            copy(k, out_ref.at[_slot(block[0], block[1], 1 - c)], me).wait_recv()
        for cp in sends:
            cp.wait_send()
        local.wait()

    anyspec = pl.BlockSpec(memory_space=pl.ANY)
    return pl.pallas_call(
        body, in_specs=[anyspec], out_specs=anyspec, out_shape=jax.ShapeDtypeStruct((NDEV, R, C), x.dtype),
        scratch_shapes=[pltpu.SemaphoreType.DMA((10,)), pltpu.SemaphoreType.DMA((10,)), pltpu.SemaphoreType.DMA],
        name=name,
    )(x)


def _slab_exchange(arrays, nslots, pick):
    n = len(arrays)

    def copies(ins, outs, sems):
        send_sems, recv_sems = sems
        return [pltpu.make_async_remote_copy(src_ref=ins[a].at[pick(k)[0]], dst_ref=outs[a].at[k], send_sem=send_sems.at[a, k],
                                             recv_sem=recv_sems.at[a, k], device_id=pick(k)[1], device_id_type=MESH)
                for a in range(n) for k in range(nslots)]

    def start(ins, outs, sems):
        for cp in copies(ins, outs, sems):
            cp.start()

    def finish(ins, outs, sems):
        cps = copies(ins, outs, sems)
        for cp in cps:
            cp.wait_recv()
        for cp in cps:
            cp.wait_send()

    return _Exchange(arrays, [jax.ShapeDtypeStruct((nslots,) + a.shape[1:], a.dtype) for a in arrays],
                     [pltpu.SemaphoreType.DMA((n, nslots)), pltpu.SemaphoreType.DMA((n, nslots))], start, finish)


def _direct_gather(arrays):
    n = len(arrays)

    def peer(d):
        p = (_my_id() + d) % NDEV
        return (p // 4, (p // 2) % 2, p % 2), p

    def copies(ins, outs, sems, receiving):
        send_sems, recv_sems, _ = sems
        slot = lambda d: peer(NDEV - d)[1] if receiving else _my_id()
        return [pltpu.make_async_remote_copy(src_ref=ins[a], dst_ref=outs[a].at[slot(d)], send_sem=send_sems.at[a, d - 1],
                                             recv_sem=recv_sems.at[a, d - 1], device_id=peer(d)[0], device_id_type=MESH)
                for d in range(1, NDEV) for a in range(n)]

    def local(ins, outs, sems):
        return [pltpu.make_async_copy(ins[a], outs[a].at[_my_id()], sems[2].at[a]) for a in range(n)]

    def start(ins, outs, sems):
        for cp in local(ins, outs, sems) + copies(ins, outs, sems, False):
            cp.start()

    def finish(ins, outs, sems):
        for cp in copies(ins, outs, sems, True):
            cp.wait_recv()
        for cp in copies(ins, outs, sems, False):
            cp.wait_send()
        for cp in local(ins, outs, sems):
            cp.wait()

    return _Exchange(arrays, [jax.ShapeDtypeStruct((NDEV,) + a.shape, a.dtype) for a in arrays],
                     [pltpu.SemaphoreType.DMA((n, NDEV - 1)), pltpu.SemaphoreType.DMA((n, NDEV - 1)), pltpu.SemaphoreType.DMA((n,))],
                     start, finish)


def _run_exchange(exchange, name):
    n = len(exchange.arrays)

    def body(*refs):
        ins, outs, sems = refs[:n], refs[n:2 * n], refs[2 * n:]
        exchange.start(ins, outs, sems)
        exchange.finish(ins, outs, sems)

    anyspec = pl.BlockSpec(memory_space=pl.ANY)
    return pl.pallas_call(body, in_specs=[anyspec] * n, out_specs=[anyspec] * n, out_shape=exchange.out_shapes,
                          scratch_shapes=exchange.sem_shapes, name=name)(*exchange.arrays)


def _slabs_bf16(x, rows_per_block, name):
    n, R, C = x.shape
    tr = min(rows_per_block, R)

    def body(x_ref, o_ref):
        o_ref[...] = x_ref[...].astype(BF16)

    blk = pl.BlockSpec((1, tr, C), lambda k, i: (k, i, 0))
    return pl.pallas_call(body, grid=(n, R // tr), in_specs=[blk], out_specs=blk, out_shape=jax.ShapeDtypeStruct(x.shape, BF16),
                          compiler_params=_cparams(("parallel", "parallel"), 32 << 20), name=name)(x)


def _to_sibling(k):
    me, sib, chips = _place()
    dest = sib if k == 0 else (*chips[k - 1], sib[2])
    return _slot(*dest), sib


def _to_chips(k):
    me, sib, chips = _place()
    return k, (*chips[k], me[2])


def _chip_sums(own, got, slots, rows_per_block, name):
    _, R, C = own.shape
    tr = min(rows_per_block, R)

    def body(slots_ref, own_ref, got_ref, o_ref):
        o_ref[0] = (own_ref[0].astype(F32) + got_ref[0].astype(F32)).astype(BF16)

    return pl.pallas_call(
        body,
        grid_spec=pltpu.PrefetchScalarGridSpec(
            num_scalar_prefetch=1, grid=(3, R // tr),
            in_specs=[pl.BlockSpec((1, tr, C), lambda j, i, s: (s[1 + j], i, 0)), pl.BlockSpec((1, tr, C), lambda j, i, s: (1 + j, i, 0))],
            out_specs=pl.BlockSpec((1, tr, C), lambda j, i, s: (j, i, 0))),
        out_shape=jax.ShapeDtypeStruct((3, R, C), BF16),
        compiler_params=_cparams(("parallel", "parallel"), 32 << 20), name=name,
    )(slots, own, got)


def _adamw_math(w, g, m, v):
    m = ADAM_B1 * m + (1.0 - ADAM_B1) * g
    v = ADAM_B2 * v + (1.0 - ADAM_B2) * (g * g)
    m_hat = m / (1.0 - ADAM_B1 ** ADAM_STEP)
    v_hat = v / (1.0 - ADAM_B2 ** ADAM_STEP)
    delta = -ADAM_LR * (m_hat / (jnp.sqrt(v_hat) + ADAM_EPS) + ADAM_WD * w)
    return delta, m, v


def _reduce_adamw(own, got1, got2, slots, w, m, v, rows_per_block, name):
    _, R, C = own.shape
    tr = min(rows_per_block, R)
    assert R % tr == 0

    def body(slots_ref, own_ref, g1_ref, g2_ref, w_ref, m_ref, v_ref, g_ref, d_ref, nm_ref, nv_ref):
        g = own_ref[0] + g1_ref[0].astype(F32)
        for j in range(3):
            g = g + g2_ref[j].astype(F32)
        g_ref[...] = g
        d, nm, nv = _adamw_math(w_ref[...], g, m_ref[...], v_ref[...])
        d_ref[...] = d
        nm_ref[...] = nm
        nv_ref[...] = nv

    blk = pl.BlockSpec((tr, C), lambda i, s: (i, 0))
    return pl.pallas_call(
        body,
        grid_spec=pltpu.PrefetchScalarGridSpec(
            num_scalar_prefetch=1, grid=(R // tr,),
            in_specs=[pl.BlockSpec((1, tr, C), lambda i, s: (s[0], i, 0)), pl.BlockSpec((1, tr, C), lambda i, s: (0, i, 0)),
                      pl.BlockSpec((3, tr, C), lambda i, s: (0, i, 0)), blk, blk, blk],
            out_specs=[blk] * 4),
        out_shape=[jax.ShapeDtypeStruct((R, C), F32)] * 4,
        compiler_params=_cparams(("parallel",), 48 << 20), name=name,
    )(slots, own, got1, got2, w, m, v)


def _reduce_adamw_t(own, got1, got2, slots, wt, mt, vt, rows_per_block, name):
    R, C = own.shape
    tr = min(rows_per_block, R)
    assert R % tr == 0

    def body(slots_ref, own_ref, g1_ref, g2_ref, w_ref, m_ref, v_ref, g_ref, d_ref, nm_ref, nv_ref):
        g = own_ref[...] + g1_ref[0].astype(F32)
        for j in range(3):
            g = g + g2_ref[j].astype(F32)
        g = jnp.transpose(g)
        g_ref[...] = g
        d, nm, nv = _adamw_math(w_ref[...], g, m_ref[...], v_ref[...])
        d_ref[...] = d
        nm_ref[...] = nm
        nv_ref[...] = nv

    blk = pl.BlockSpec((C, tr), lambda i, s: (0, i))
    return pl.pallas_call(
        body,
        grid_spec=pltpu.PrefetchScalarGridSpec(
            num_scalar_prefetch=1, grid=(R // tr,),
            in_specs=[pl.BlockSpec((tr, C), lambda i, s: (i, 0)), pl.BlockSpec((1, tr, C), lambda i, s: (0, i, 0)),
                      pl.BlockSpec((3, tr, C), lambda i, s: (0, i, 0)), blk, blk, blk],
            out_specs=[blk] * 4),
        out_shape=[jax.ShapeDtypeStruct((C, R), F32)] * 4,
        compiler_params=_cparams(("parallel",), 48 << 20), name=name,
    )(slots, own, got1, got2, wt, mt, vt)


def _sum_adamw(parts, w, m, v, rows_per_block, name):
    n, R, C = parts.shape
    tr = min(rows_per_block, R)
    assert R % tr == 0

    def body(p_ref, w_ref, m_ref, v_ref, g_ref, d_ref, nm_ref, nv_ref):
        g = p_ref[0]
        for j in range(1, n):
            g = g + p_ref[j]
        g_ref[...] = g
        d, nm, nv = _adamw_math(w_ref[...], g, m_ref[...], v_ref[...])
        d_ref[...] = d
        nm_ref[...] = nm
        nv_ref[...] = nv

    blk = pl.BlockSpec((tr, C), lambda i: (i, 0))
    return pl.pallas_call(
        body, grid=(R // tr,),
        in_specs=[pl.BlockSpec((n, tr, C), lambda i: (0, i, 0)), blk, blk, blk],
        out_specs=[blk] * 4,
        out_shape=[jax.ShapeDtypeStruct((R, C), F32)] * 4,
        compiler_params=_cparams(("parallel",), 48 << 20), name=name,
    )(parts, w, m, v)


def kernel(x, positions, pre_norm_g, w_in, q_a_norm_g, w_q_b, kv_a_norm_g, w_kv_b, conv_w, w_o_mla, w_o_conv, w_out, post_norm_g, loss_target, m_pre_norm_g, m_w_in, m_q_a_norm_g, m_w_q_b, m_kv_a_norm_g, m_w_kv_b, m_conv_w, m_w_o_mla, m_w_o_conv, m_w_out, m_post_norm_g, v_pre_norm_g, v_w_in, v_q_a_norm_g, v_w_q_b, v_kv_a_norm_g, v_w_kv_b, v_conv_w, v_w_o_mla, v_w_o_conv, v_w_out, v_post_norm_g):
    S = x.shape[1]
    conv_pad = jnp.zeros((8, 256), F32).at[0:3, :].set(conv_w)
    g_in = _all_gather_relayed(w_in.astype(BF16), "all_gather_w_in")
    w_in_f = _assemble_w_in(g_in)
    gather_rest = _direct_gather([w_q_b.astype(BF16), w_kv_b.astype(BF16), conv_pad, w_o_mla.astype(BF16), w_o_conv.astype(BF16),
                                  w_out.astype(BF16)])

    def assemble_rest(g_q, g_kv, g_cw, g_om, g_oc, g_out):
        return (_assemble_w_q(g_q), _concat_cols(g_kv, BF16, "assemble_w_kv"), _concat_cols(g_cw, F32, "assemble_conv_w"),
                g_om.reshape(D, D), g_oc.reshape(D, D), g_out.reshape(D, D))

    gnames = ["w_in", "w_q", "w_kv", "conv_w", "w_o_mla", "w_o_conv", "w_out"]
    (mx, my, mc), _, chips = _place()
    slots = jnp.stack([_slot(mx, my, mc)] + [_slot(cx, cy, mc) for cx, cy in chips]).astype(jnp.int32)

    def reduce_grads(r):
        own_in, slabs_in = _split_dw_in(r["dw_in"], slots)
        own = [own_in, _split_dw_q(r["dw_q"]), _split_cols(r["dw_kv"], NDEV, "split_dw_kv"),
               _split_cols(r["dconv_w"], NDEV, "split_dconv_w"), r["dw_o_mla"].reshape(NDEV, D // NDEV, D),
               r["dw_o_conv"].reshape(NDEV, D // NDEV, D), r["dw_out"].reshape(NDEV, D // NDEV, D)]
        slabs = [slabs_in] + [_slabs_bf16(g, 128, "bf16_" + nm) for g, nm in zip(own[1:], gnames[1:])]
        to_sibling = _slab_exchange(slabs, 4, _to_sibling)
        got1 = _run_exchange(to_sibling, "grads_to_sibling")
        sums = [_chip_sums(o, g1, slots, 128, "chip_sum_" + nm) for o, g1, nm in zip([slabs_in] + own[1:], got1, gnames)]
        return _slab_exchange(sums, 3, _to_chips), lambda got2: (own, got1, got2)

    row2 = lambda a: a.reshape(1, -1)
    r = _local_step(x[0], positions.reshape(S, 1), loss_target[0], row2(pre_norm_g), row2(q_a_norm_g), row2(kv_a_norm_g),
                    row2(post_norm_g), w_in_f, (gather_rest, assemble_rest), reduce_grads)
    own, got1, got2 = r["reduced"]
    small = jnp.concatenate([r["dg_pre"][0:1], r["dg_post"][0:1], jnp.pad(r["dg_qa"][0:1], ((0, 0), (0, D - RQ))),
                             jnp.pad(r["dg_kva"][0:1], ((0, 0), (0, D - RKV))), jnp.pad(r["loss"][0:1], ((0, 0), (0, D - 128))),
                             jnp.zeros((3, D), F32)], axis=0)
    p_small, = _all_gather([small], "all_gather_small")
    pad8 = lambda a: jnp.zeros((8, 256), F32).at[0:3, :].set(a)
    params = [(w_in, m_w_in, v_w_in), (w_q_b, m_w_q_b, v_w_q_b), (w_kv_b, m_w_kv_b, v_w_kv_b), (conv_pad, pad8(m_conv_w), pad8(v_conv_w)),
              (w_o_mla, m_w_o_mla, v_w_o_mla), (w_o_conv, m_w_o_conv, v_w_o_conv), (w_out, m_w_out, v_w_out)]
    o_q, o_kv, o_cw, o_om, o_oc, o_out = [
        _reduce_adamw(o, g1, g2, slots, w, m, v, 128, "adamw_" + nm)
        for o, g1, g2, (w, m, v), nm in list(zip(own, got1, got2, params, gnames))[1:]]
    o_cw = [a[0:3] for a in o_cw]
    o_in = [a.T for a in _reduce_adamw_t(own[0], got1[0], got2[0], slots, w_in.T, m_w_in.T, v_w_in.T, 128, "adamw_w_in")]
    padv = lambda a: jnp.pad(a.reshape(1, -1), ((0, 0), (0, D - a.shape[0])))
    stack = lambda pre, post, qa, kva: jnp.concatenate([row2(pre), row2(post), padv(qa), padv(kva), jnp.zeros((4, D), F32)], axis=0)
    o_g = _sum_adamw(p_small, stack(pre_norm_g, post_norm_g, q_a_norm_g, kv_a_norm_g),
                     stack(m_pre_norm_g, m_post_norm_g, m_q_a_norm_g, m_kv_a_norm_g),
                     stack(v_pre_norm_g, v_post_norm_g, v_q_a_norm_g, v_kv_a_norm_g), 8, "adamw_gains")
    loss = o_g[0][4, 0]
    outs = {}
    for idx, kind in enumerate(("grad", "delta", "new_m", "new_v")):
        o = o_g[idx]
        outs[kind] = dict(pre_norm_g=o[0], w_in=o_in[idx], q_a_norm_g=o[2, 0:RQ], w_q_b=o_q[idx], kv_a_norm_g=o[3, 0:RKV],
                          w_kv_b=o_kv[idx], conv_w=o_cw[idx], w_o_mla=o_om[idx], w_o_conv=o_oc[idx], w_out=o_out[idx], post_norm_g=o[1])
    names = ["pre_norm_g", "w_in", "q_a_norm_g", "w_q_b", "kv_a_norm_g", "w_kv_b", "conv_w", "w_o_mla", "w_o_conv", "w_out", "post_norm_g"]
    return (loss, r["grad_x"][None], *[outs["grad"][n] for n in names], *[outs["delta"][n] for n in names],
            *[outs["new_m"][n] for n in names], *[outs["new_v"][n] for n in names])
```

```python
import functools
import math

import jax
import jax.numpy as jnp
from jax import lax
from jax.experimental import pallas as pl
from jax.experimental.pallas import tpu as pltpu

F32 = jnp.float32
BF16 = jnp.bfloat16

NDEV = 8
D = 2048
H = 16
DN = 128
DR = 64
DV = 128
RQ = 512
RKV = 512
HW = 256
ROPE_THETA = 10000.0
RMS_EPS = 1e-6
N_IN = 15424
SHARD_IN = N_IN // NDEV
SMALL = RQ + RKV + DR
NP = 7 * D + RQ + RKV + 128
SEG = dict(z_mla=0, c_in=1, b_gate=2, c_gate=3, z_conv=4, g_mla=5, g_conv=6)
OFF_QA = 7 * D
OFF_CKV = OFF_QA + RQ
OFF_KR = OFF_CKV + RKV
EXT = 2176
VMEM_CAP = 56 * 1024 * 1024

ADAM_LR = 0.001
ADAM_B1 = 0.9
ADAM_B2 = 0.999
ADAM_EPS = 1e-08
ADAM_WD = 0.01
ADAM_STEP = 10

LOG2E = math.log2(math.e)
SM_SCALE = 1.0 / math.sqrt(DN + DR)
QSCALE = SM_SCALE * LOG2E
NN = (((1,), (0,)), ((), ()))
NT = (((1,), (1,)), ((), ()))
TN = (((0,), (0,)), ((), ()))
MESH = pl.DeviceIdType.MESH


def _cparams(sem, vmem_bytes):
    return pltpu.CompilerParams(dimension_semantics=sem, vmem_limit_bytes=int(min(VMEM_CAP, max(vmem_bytes, 16 << 20))))


def _nbytes(shape, dtype):
    return math.prod(shape) * jnp.dtype(dtype).itemsize


class _Exchange:
    def __init__(self, arrays, out_shapes, sem_shapes, start, finish):
        self.arrays, self.out_shapes, self.sem_shapes, self.start, self.finish = arrays, out_shapes, sem_shapes, start, finish
        self.n_sems = len(sem_shapes)


def _matmul(a, b, *, mode, out_dtype, tm, tn, tk, name, m_outer=False, exchange=None):
    if mode == "nn":
        (M, K), (K2, N) = a.shape, b.shape
    elif mode == "nt":
        (M, K), (N, K2) = a.shape, b.shape
    else:
        (K, M), (K2, N) = a.shape, b.shape
    assert K == K2, (a.shape, b.shape, mode)
    tm, tn, tk = min(tm, M), min(tn, N), min(tk, K)
    assert M % tm == 0 and N % tn == 0 and K % tk == 0, (M, N, K, tm, tn, tk)
    ni, nj, nk = M // tm, N // tn, K // tk
    dims = dict(nn=NN, nt=NT, tn=TN)[mode]

    if m_outer:
        grid = (ni, nj, nk)
        ij = lambda g0, g1: (g0, g1)
    else:
        grid = (nj, ni, nk)
        ij = lambda g0, g1: (g1, g0)

    if mode == "tn":
        a_spec = pl.BlockSpec((tk, tm), lambda g0, g1, k: (k, ij(g0, g1)[0]))
        a_tile = (tk, tm)
    else:
        a_spec = pl.BlockSpec((tm, tk), lambda g0, g1, k: (ij(g0, g1)[0], k))
        a_tile = (tm, tk)
    if mode == "nt":
        b_spec = pl.BlockSpec((tn, tk), lambda g0, g1, k: (ij(g0, g1)[1], k))
    else:
        b_spec = pl.BlockSpec((tk, tn), lambda g0, g1, k: (k, ij(g0, g1)[1]))
    o_spec = pl.BlockSpec((tm, tn), lambda g0, g1, k: ij(g0, g1))

    n_in = len(exchange.arrays) if exchange else 0

    def body(a_ref, b_ref, *refs):
        x_in, o_ref, x_out, scratch = refs[:n_in], refs[n_in], refs[n_in + 1:2 * n_in + 1], refs[2 * n_in + 1:]
        if exchange:
            sems = scratch[len(scratch) - exchange.n_sems:]
            step = (pl.program_id(0) * grid[1] + pl.program_id(1)) * grid[2] + pl.program_id(2)

            @pl.when(step == 0)
            def _():
                exchange.start(x_in, x_out, sems)

        prod = lax.dot_general(a_ref[...], b_ref[...], dims, preferred_element_type=F32)
        if nk == 1:
            o_ref[...] = prod.astype(o_ref.dtype)
        else:
            acc_ref = scratch[0]
            k = pl.program_id(2)

            @pl.when(k == 0)
            def _():
                acc_ref[...] = prod

            @pl.when((k > 0) & (k < nk - 1))
            def _():
                acc_ref[...] += prod

            @pl.when(k == nk - 1)
            def _():
                o_ref[...] = (acc_ref[...] + prod).astype(o_ref.dtype)

        if exchange:
            @pl.when(step == grid[0] * grid[1] * grid[2] - 1)
            def _():
                exchange.finish(x_in, x_out, sems)

    vmem = 2 * (_nbytes(a_tile, a.dtype) + _nbytes((tk, tn), b.dtype) + _nbytes((tm, tn), out_dtype)) + 2 * _nbytes((tm, tn), F32)
    anyspec = pl.BlockSpec(memory_space=pl.ANY)
    outs = pl.pallas_call(
        body,
        grid=grid,
        in_specs=[a_spec, b_spec] + [anyspec] * n_in,
        out_specs=[o_spec] + [anyspec] * n_in,
        out_shape=[jax.ShapeDtypeStruct((M, N), out_dtype)] + (exchange.out_shapes if exchange else []),
        scratch_shapes=([] if nk == 1 else [pltpu.VMEM((tm, tn), F32)]) + (exchange.sem_shapes if exchange else []),
        compiler_params=_cparams(("arbitrary",) * 3 if exchange else ("parallel", "parallel", "arbitrary"), vmem + (8 << 20)),
        name=name,
    )(a, b, *(exchange.arrays if exchange else []))
    return (outs[0], outs[1:]) if exchange else outs[0]


def _in_dest(k):
    return SHARD_IN * k - SMALL


def _assemble_w_in(g):
    R = 128

    def body(g_ref, o_ref, ext, acc):
        ext[...] = jnp.zeros_like(ext)
        acc[...] = jnp.zeros_like(acc)
        lane = lax.broadcasted_iota(jnp.int32, (R, EXT), 1)
        ext[:, 0:SHARD_IN] = g_ref[0].astype(F32)
        v = ext[...]
        acc[:, OFF_QA:NP] = jnp.where(lane[:, 0:NP - OFF_QA] < SMALL, v[:, 0:NP - OFF_QA], 0.0)
        w = v[:, 1024:2048]
        w = pltpu.roll(w, 1024 - 64, 1)
        acc[:, 0:1024] = jnp.where(lane[:, 0:1024] < SHARD_IN - SMALL, w, 0.0)
        for k in range(1, NDEV):
            ext[:, 0:SHARD_IN] = g_ref[k].astype(F32)
            dest = _in_dest(k)
            t, o = dest // 128, dest % 128
            width = -(-(o + SHARD_IN) // 128) * 128
            v = pltpu.roll(ext[...], o, 1)[:, 0:width]
            acc[:, 128 * t:128 * t + width] += v
        o_ref[...] = acc[...].astype(BF16)

    return pl.pallas_call(
        body,
        grid=(D // R,),
        in_specs=[pl.BlockSpec((NDEV, R, SHARD_IN), lambda i: (0, i, 0))],
        out_specs=pl.BlockSpec((R, NP), lambda i: (i, 0)),
        out_shape=jax.ShapeDtypeStruct((D, NP), BF16),
        scratch_shapes=[pltpu.VMEM((R, EXT), F32), pltpu.VMEM((R, NP), F32)],
        compiler_params=_cparams(("parallel",), 40 << 20),
        name="assemble_w_in",
    )(g)


def _split_dw_in(dw, slots):
    R = 128

    def body(slots_ref, dw_ref, o_ref, ob_ref):
        def put(k, cols, v):
            ob_ref[k, :, cols] = v.astype(BF16)

            @pl.when(slots_ref[0] == k)
            def _():
                o_ref[:, cols] = v

        lane = lax.broadcasted_iota(jnp.int32, (R, 1024), 1)
        put(0, slice(0, 1024), dw_ref[:, OFF_QA:OFF_QA + 1024])
        tail = dw_ref[:, OFF_QA + 1024:NP]
        tail = jnp.concatenate([tail, jnp.zeros((R, 1024 - 128), F32)], axis=1)
        head = dw_ref[:, 0:1024]
        mixed = jnp.where(lane < 64, tail, pltpu.roll(head, 64, 1))
        put(0, slice(1024, SHARD_IN), mixed[:, 0:SHARD_IN - 1024])
        for k in range(1, NDEV):
            dest = _in_dest(k)
            t, o = dest // 128, dest % 128
            width = -(-(o + SHARD_IN) // 128) * 128
            v = dw_ref[:, 128 * t:128 * t + width]
            v = pltpu.roll(v, width - o, 1)
            put(k, slice(0, SHARD_IN), v[:, 0:SHARD_IN])

    return pl.pallas_call(
        body,
        grid_spec=pltpu.PrefetchScalarGridSpec(
            num_scalar_prefetch=1, grid=(D // R,),
            in_specs=[pl.BlockSpec((R, NP), lambda i, s: (i, 0))],
            out_specs=[pl.BlockSpec((R, SHARD_IN), lambda i, s: (i, 0)), pl.BlockSpec((NDEV, R, SHARD_IN), lambda i, s: (0, i, 0))]),
        out_shape=[jax.ShapeDtypeStruct((D, SHARD_IN), F32), jax.ShapeDtypeStruct((NDEV, D, SHARD_IN), BF16)],
        compiler_params=_cparams(("parallel",), 48 << 20),
        name="split_dw_in",
    )(slots, dw)


def _assemble_w_q(g):
    def body(g_ref, o_ref):
        lane = lax.broadcasted_iota(jnp.int32, (RQ, 128), 1)
        lo = lane < 64
        for k in range(NDEV):
            t0 = g_ref[k, :, 0:128].astype(F32)
            t1 = g_ref[k, :, 128:256].astype(F32)
            t2 = g_ref[k, :, 256:384].astype(F32)
            base = 2 * k * HW
            o_ref[:, base:base + 128] = t0.astype(BF16)
            o_ref[:, base + 128:base + 256] = jnp.where(lo, t1, 0.0).astype(BF16)
            o_ref[:, base + 256:base + 384] = pltpu.roll(jnp.where(lo, t2, t1), 64, 1).astype(BF16)
            o_ref[:, base + 384:base + 512] = jnp.where(lo, pltpu.roll(t2, 64, 1), 0.0).astype(BF16)

    return pl.pallas_call(
        body,
        out_shape=jax.ShapeDtypeStruct((RQ, H * HW), BF16),
        compiler_params=_cparams(None, 32 << 20),
        name="assemble_w_q",
    )(g)


def _split_dw_q(dw):
    def body(dw_ref, o_ref):
        lane = lax.broadcasted_iota(jnp.int32, (RQ, 128), 1)
        lo = lane < 64
        for k in range(NDEV):
            base = 2 * k * HW
            a = dw_ref[:, base:base + 128]
            b = dw_ref[:, base + 128:base + 256]
            c = pltpu.roll(dw_ref[:, base + 256:base + 384], 64, 1)
            d = pltpu.roll(dw_ref[:, base + 384:base + 512], 64, 1)
            o_ref[k, :, 0:128] = a
            o_ref[k, :, 128:256] = jnp.where(lo, b, c)
            o_ref[k, :, 256:384] = jnp.where(lo, c, d)

    return pl.pallas_call(
        body,
        out_shape=jax.ShapeDtypeStruct((NDEV, RQ, 384), F32),
        compiler_params=_cparams(None, 32 << 20),
        name="split_dw_q",
    )(dw)


def _concat_cols(g, dtype, name):
    n, R, C = g.shape

    def body(g_ref, o_ref):
        for k in range(n):
            o_ref[:, k * C:(k + 1) * C] = g_ref[k].astype(dtype)

    return pl.pallas_call(body, out_shape=jax.ShapeDtypeStruct((R, n * C), dtype), compiler_params=_cparams(None, 32 << 20), name=name)(g)


def _split_cols(x, n, name):
    R, NC = x.shape
    C = NC // n

    def body(x_ref, o_ref):
        for k in range(n):
            o_ref[k] = x_ref[:, k * C:(k + 1) * C]

    return pl.pallas_call(body, out_shape=jax.ShapeDtypeStruct((n, R, C), x.dtype), compiler_params=_cparams(None, 32 << 20), name=name)(x)


def _rms_scale(xf):
    return lax.rsqrt(jnp.mean(xf * xf, axis=-1, keepdims=True) + RMS_EPS)


def _prenorm(x, g, ts):
    S = x.shape[0]

    def body(x_ref, g_ref, h_ref, ht_ref):
        xf = x_ref[...]
        h = xf * _rms_scale(xf) * g_ref[...]
        h_ref[...] = h.astype(BF16)
        ht_ref[...] = jnp.transpose(h).astype(BF16)

    return pl.pallas_call(
        body,
        grid=(S // ts,),
        in_specs=[pl.BlockSpec((ts, D), lambda i: (i, 0)), pl.BlockSpec((1, D), lambda i: (0, 0))],
        out_specs=[pl.BlockSpec((ts, D), lambda i: (i, 0)), pl.BlockSpec((D, ts), lambda i: (0, i))],
        out_shape=[jax.ShapeDtypeStruct((S, D), BF16), jax.ShapeDtypeStruct((D, S), BF16)],
        compiler_params=_cparams(("parallel",), 32 << 20),
        name="prenorm",
    )(x, g)


def _inv_freq_tile():
    inv_freq = ROPE_THETA ** (-jnp.arange(0, DR, 2, dtype=F32) / DR)
    return jnp.tile(inv_freq, 4).reshape(1, 128)


def _rope_tables(pos_ref, freq_ref, ts):
    lane = lax.broadcasted_iota(jnp.int32, (ts, 128), 1)
    ang = pos_ref[...].astype(F32) * freq_ref[...]
    return jnp.cos(ang), jnp.sin(ang), lane


def _rope_swap(t, lane):
    return jnp.where(lane < 32, pltpu.roll(t, 96, 1), pltpu.roll(t, 32, 1))


def _qkv_prep(proj, pos, freq, g_qa, g_kva, w_q, w_kv, ts):
    S = proj.shape[0]

    def body(qa_ref, ckv_ref, kr_ref, pos_ref, freq_ref, gq_ref, gk_ref, wq_ref, wkv_ref, q_ref, k_ref, v_ref, vt_ref, qn_ref, kvn_ref):
        qa = _f32(qa_ref)
        qn = (qa * _rms_scale(qa) * gq_ref[...]).astype(BF16)
        ckv = _f32(ckv_ref)
        kvn = (ckv * _rms_scale(ckv) * gk_ref[...]).astype(BF16)
        qn_ref[...] = qn
        kvn_ref[...] = kvn
        cos, sin, lane = _rope_tables(pos_ref, freq_ref, ts)
        sgn_sin = jnp.where(lane < 32, -sin, sin)
        live = lane < DR
        kr = _f32(kr_ref)
        kr = jnp.where(live, kr * cos + _rope_swap(kr, lane) * sgn_sin, 0.0).astype(BF16)
        qf = jnp.dot(qn, wq_ref[...], preferred_element_type=F32)
        kvf = jnp.dot(kvn, wkv_ref[...], preferred_element_type=F32)
        for h in range(H):
            q_ref[h, :, 0:DN] = (qf[:, h * HW:h * HW + DN] * QSCALE).astype(BF16)
            t = qf[:, h * HW + DN:(h + 1) * HW]
            q_ref[h, :, DN:HW] = jnp.where(live, (t * cos + _rope_swap(t, lane) * sgn_sin) * QSCALE, 0.0).astype(BF16)
            k_ref[h, :, 0:DN] = kvf[:, h * HW:h * HW + DN].astype(BF16)
            k_ref[h, :, DN:HW] = kr
            vh = kvf[:, h * HW + DN:(h + 1) * HW]
            v_ref[h] = vh.astype(BF16)
            vt_ref[h] = jnp.transpose(vh).astype(BF16)

    row = lambda w, blk: pl.BlockSpec((ts, w), lambda i: (i, blk))
    full = lambda a: pl.BlockSpec(a.shape, lambda i: (0,) * a.ndim)
    return pl.pallas_call(
        body,
        grid=(S // ts,),
        in_specs=[row(RQ, OFF_QA // RQ), row(RKV, OFF_CKV // RKV), row(128, OFF_KR // 128),
                  pl.BlockSpec((ts, 1), lambda i: (i, 0)), full(freq), full(g_qa), full(g_kva), full(w_q), full(w_kv)],
        out_specs=[pl.BlockSpec((H, ts, HW), lambda i: (0, i, 0)), pl.BlockSpec((H, ts, HW), lambda i: (0, i, 0)),
                   pl.BlockSpec((H, ts, DV), lambda i: (0, i, 0)), pl.BlockSpec((H, DV, ts), lambda i: (0, 0, i)),
                   pl.BlockSpec((ts, RQ), lambda i: (i, 0)), pl.BlockSpec((ts, RKV), lambda i: (i, 0))],
        out_shape=[jax.ShapeDtypeStruct((H, S, HW), BF16), jax.ShapeDtypeStruct((H, S, HW), BF16),
                   jax.ShapeDtypeStruct((H, S, DV), BF16), jax.ShapeDtypeStruct((H, DV, S), BF16),
                   jax.ShapeDtypeStruct((S, RQ), BF16), jax.ShapeDtypeStruct((S, RKV), BF16)],
        compiler_params=_cparams(("parallel",), 48 << 20),
        name="qkv_prep",
    )(proj, proj, proj, pos, freq, g_qa, g_kva, w_q, w_kv)


def _col_to_row8(col, n):
    return jnp.transpose(jnp.broadcast_to(col, (n, 128)))[0:8, :]


def _causal_pairs(n, key_major):
    if key_major:
        pairs = [(i, j) for j in range(n) for i in range(j, n)]
    else:
        pairs = [(i, j) for i in range(n) for j in range(i + 1)]
    return jnp.array([p[0] for p in pairs], jnp.int32), jnp.array([p[1] for p in pairs], jnp.int32)


def _flash_fwd(q, k, vt, tq, hb=16):
    S = q.shape[1]
    nq = S // tq
    tsub = tq
    nsub = tq // tsub

    qi_tab, kj_tab = _causal_pairs(nq, key_major=False)

    def body(qi_ref, kj_ref, q_ref, k_ref, vt_ref, o_ref, lse_ref, m_sc, l_sc, acc_sc):
        p = pl.program_id(1)
        qi, kj = qi_ref[p], kj_ref[p]

        @pl.when(kj == 0)
        def _():
            m_sc[...] = jnp.full_like(m_sc, -jnp.inf)
            l_sc[...] = jnp.zeros_like(l_sc)
            acc_sc[...] = jnp.zeros_like(acc_sc)

        def step(diag):
            chains = [(h, u) for h in range(hb) for u in range(nsub)]

            def scores(h, u):
                return lax.dot_general(k_ref[h], q_ref[h, u * tsub:(u + 1) * tsub, :], NT, preferred_element_type=F32)

            st_next = scores(*chains[0])
            for ci, (h, u) in enumerate(chains):
                st = st_next
                if ci + 1 < len(chains):
                    st_next = scores(*chains[ci + 1])
                cols = slice(u * tsub, (u + 1) * tsub)
                if diag:
                    r = lax.broadcasted_iota(jnp.int32, (tq, tsub), 0)
                    c = lax.broadcasted_iota(jnp.int32, (tq, tsub), 1) + u * tsub
                    st = jnp.where(r <= c, st, -jnp.inf)
                m_prev = m_sc[h, 0:1, cols]
                m_new = jnp.maximum(m_prev, jnp.max(st, axis=0, keepdims=True))
                alpha = jnp.exp2(m_prev - m_new)
                pt = jnp.exp2(st - m_new)
                l_sc[h, :, cols] = jnp.broadcast_to(alpha * l_sc[h, 0:1, cols] + jnp.sum(pt, axis=0, keepdims=True), (8, tsub))
                m_sc[h, :, cols] = jnp.broadcast_to(m_new, (8, tsub))
                acc_sc[h, :, cols] = alpha * acc_sc[h, :, cols] + jnp.dot(vt_ref[h], pt.astype(BF16), preferred_element_type=F32)

        @pl.when(kj < qi)
        def _():
            step(False)

        @pl.when(kj == qi)
        def _():
            step(True)
            for h in range(hb):
                l = l_sc[h, 0:1, :]
                o_ref[:, h * DV:(h + 1) * DV] = jnp.transpose(acc_sc[h] / l).astype(BF16)
                lse_ref[h] = m_sc[h] + jnp.log2(l_sc[h])

    return pl.pallas_call(
        body,
        grid_spec=pltpu.PrefetchScalarGridSpec(
            num_scalar_prefetch=2,
            grid=(H // hb, len(qi_tab)),
            in_specs=[pl.BlockSpec((hb, tq, HW), lambda h, p, qi, kj: (h, qi[p], 0)),
                      pl.BlockSpec((hb, tq, HW), lambda h, p, qi, kj: (h, kj[p], 0)),
                      pl.BlockSpec((hb, DV, tq), lambda h, p, qi, kj: (h, 0, kj[p]))],
            out_specs=[pl.BlockSpec((tq, hb * DV), lambda h, p, qi, kj: (qi[p], h)),
                       pl.BlockSpec((hb, 8, tq), lambda h, p, qi, kj: (h, 0, qi[p]))],
            scratch_shapes=[pltpu.VMEM((hb, 8, tq), F32), pltpu.VMEM((hb, 8, tq), F32), pltpu.VMEM((hb, DV, tq), F32)],
        ),
        out_shape=[jax.ShapeDtypeStruct((S, H * DV), BF16), jax.ShapeDtypeStruct((H, 8, S), F32)],
        compiler_params=_cparams(("parallel", "arbitrary"), 40 << 20),
        name="flash_fwd",
    )(qi_tab, kj_tab, q, k, vt)


def _sigmoid(x):
    return 1.0 / (1.0 + jnp.exp(-x))


HALO = 16


def _f32(ref):
    return ref[...].astype(F32)


def _shift_rows(u, prev, n, first):
    ts = u.shape[0]
    row = lax.broadcasted_iota(jnp.int32, u.shape, 0)
    out = pltpu.roll(u, n, 0)
    for j in range(n):
        halo = jnp.where(first, 0.0, prev[HALO - n + j:HALO - n + j + 1, :])
        out = jnp.where(row == j, halo, out)
    return out


def _gates_fwd(proj, attn, conv_w, ts):
    S = proj.shape[0]

    def body(attn_ref, zm_ref, cin_ref, bg_ref, cg_ref, zc_ref, cin_p, cg_p, w_ref, am_ref, ac_ref):
        first = pl.program_id(0) == 0
        zm = _f32(zm_ref)
        am_ref[...] = (_f32(attn_ref) * (zm * _sigmoid(zm))).astype(BF16)
        u = _f32(cg_ref) * _f32(cin_ref)
        up = _f32(cg_p) * _f32(cin_p)
        w = w_ref[...]
        conv = w[0:1, :] * _shift_rows(u, up, 2, first) + w[1:2, :] * _shift_rows(u, up, 1, first) + w[2:3, :] * u
        zc = _f32(zc_ref)
        ac_ref[...] = (_f32(bg_ref) * conv * (zc * _sigmoid(zc))).astype(BF16)

    seg = lambda name: pl.BlockSpec((ts, D), lambda i: (i, SEG[name]))
    prev = lambda name: pl.BlockSpec((HALO, D), lambda i: (jnp.maximum(i * (ts // HALO) - 1, 0), SEG[name]))
    return pl.pallas_call(
        body,
        grid=(S // ts,),
        in_specs=[pl.BlockSpec((ts, D), lambda i: (i, 0)), seg("z_mla"), seg("c_in"), seg("b_gate"), seg("c_gate"), seg("z_conv"),
                  prev("c_in"), prev("c_gate"), pl.BlockSpec((8, D), lambda i: (0, 0))],
        out_specs=[pl.BlockSpec((ts, D), lambda i: (i, 0))] * 2,
        out_shape=[jax.ShapeDtypeStruct((S, D), BF16)] * 2,
        compiler_params=_cparams(("arbitrary",), 48 << 20),
        name="gates_fwd",
    )(attn, proj, proj, proj, proj, proj, proj, proj, conv_w)


def _merge_fwd(proj, y_mla, y_conv, ts):
    S = proj.shape[0]

    def body(gm_ref, gc_ref, ym_ref, yc_ref, o_ref):
        o_ref[...] = (_sigmoid(_f32(gm_ref)) * _f32(ym_ref) + _sigmoid(_f32(gc_ref)) * _f32(yc_ref)).astype(BF16)

    seg = lambda name: pl.BlockSpec((ts, D), lambda i: (i, SEG[name]))
    row = pl.BlockSpec((ts, D), lambda i: (i, 0))
    return pl.pallas_call(
        body, grid=(S // ts,), in_specs=[seg("g_mla"), seg("g_conv"), row, row], out_specs=row,
        out_shape=jax.ShapeDtypeStruct((S, D), BF16), compiler_params=_cparams(("parallel",), 32 << 20), name="merge_fwd",
    )(proj, proj, y_mla, y_conv)


def _post_loss(out, x, target, g_post, ts):
    S = out.shape[0]

    def body(o_ref, x_ref, t_ref, g_ref, dy_ref, do_ref, dg_ref, loss_ref):
        i = pl.program_id(0)
        o = o_ref[...]
        r = _rms_scale(o)
        n = o * r
        g = g_ref[...]
        err = x_ref[...] + n * g - t_ref[...]
        dy = err * (1.0 / D)
        dy_ref[...] = dy
        dn = dy * g
        do_ref[...] = (r * (dn - n * jnp.mean(dn * n, axis=-1, keepdims=True))).astype(BF16)
        dg = jnp.sum(dy * n, axis=0, keepdims=True)
        part = jnp.sum(jnp.sum(err * err, axis=0, keepdims=True), axis=1, keepdims=True) * (0.5 / D)

        @pl.when(i == 0)
        def _():
            dg_ref[...] = jnp.zeros_like(dg_ref)
            loss_ref[...] = jnp.zeros_like(loss_ref)

        dg_ref[0:1, :] += dg
        loss_ref[...] += jnp.broadcast_to(part, loss_ref.shape)

    row = pl.BlockSpec((ts, D), lambda i: (i, 0))
    return pl.pallas_call(
        body, grid=(S // ts,),
        in_specs=[row, row, row, pl.BlockSpec((1, D), lambda i: (0, 0))],
        out_specs=[row, row, pl.BlockSpec((8, D), lambda i: (0, 0)), pl.BlockSpec((8, 128), lambda i: (0, 0))],
        out_shape=[jax.ShapeDtypeStruct((S, D), F32), jax.ShapeDtypeStruct((S, D), BF16),
                   jax.ShapeDtypeStruct((8, D), F32), jax.ShapeDtypeStruct((8, 128), F32)],
        compiler_params=_cparams(("arbitrary",), 40 << 20), name="post_loss",
    )(out, x, target, g_post)


def _merge_bwd(proj, y_mla, y_conv, dmerged, ts):
    S = proj.shape[0]

    def body(gm_ref, gc_ref, ym_ref, yc_ref, dm_ref, dym_ref, dyc_ref, dgm_ref, dgc_ref):
        dm = _f32(dm_ref)
        sm = _sigmoid(_f32(gm_ref))
        sc = _sigmoid(_f32(gc_ref))
        dym_ref[...] = (dm * sm).astype(BF16)
        dyc_ref[...] = (dm * sc).astype(BF16)
        dgm_ref[...] = (dm * _f32(ym_ref) * (sm * (1.0 - sm))).astype(BF16)
        dgc_ref[...] = (dm * _f32(yc_ref) * (sc * (1.0 - sc))).astype(BF16)

    seg = lambda name: pl.BlockSpec((ts, D), lambda i: (i, SEG[name]))
    row = pl.BlockSpec((ts, D), lambda i: (i, 0))
    return pl.pallas_call(
        body, grid=(S // ts,), in_specs=[seg("g_mla"), seg("g_conv"), row, row, row], out_specs=[row] * 4,
        out_shape=[jax.ShapeDtypeStruct((S, D), BF16)] * 4, compiler_params=_cparams(("parallel",), 40 << 20), name="merge_bwd",
    )(proj, proj, y_mla, y_conv, dmerged)


def _gates_bwd(proj, attn, da_mla, da_conv, conv_w, ts):
    S = proj.shape[0]
    nblk = S // ts

    def body(attn_ref, zm_ref, cin_ref, bg_ref, cg_ref, zc_ref, dam_ref, dac_ref, cin_p, cg_p, bg_n, zc_n, dac_n, w_ref,
             dattn_ref, delta_ref, dzm_ref, dcin_ref, dbg_ref, dcg_ref, dzc_ref, dw_ref):
        i = pl.program_id(0)
        first = i == 0
        last = i == nblk - 1

        @pl.when(first)
        def _():
            dw_ref[...] = jnp.zeros_like(dw_ref)

        row = lax.broadcasted_iota(jnp.int32, (ts, DV), 0)
        for c in range(D // DV):
            cols = slice(c * DV, (c + 1) * DV)
            ld = lambda ref: ref[:, cols].astype(F32)
            zm = ld(zm_ref)
            sg = _sigmoid(zm)
            attn = ld(attn_ref)
            dam = ld(dam_ref)
            dattn = dam * (zm * sg)
            dattn_ref[:, cols] = dattn.astype(BF16)
            dzm_ref[:, cols] = (dam * attn * (sg * (1.0 + zm * (1.0 - sg)))).astype(BF16)
            delta_ref[c] = _col_to_row8(jnp.sum(dattn * attn, axis=1, keepdims=True), ts)
            w = w_ref[:, cols]
            cin, cg, bg, zc = ld(cin_ref), ld(cg_ref), ld(bg_ref), ld(zc_ref)
            u = cg * cin
            up = ld(cg_p) * ld(cin_p)
            u1 = _shift_rows(u, up, 1, first)
            u2 = _shift_rows(u, up, 2, first)
            conv = w[0:1, :] * u2 + w[1:2, :] * u1 + w[2:3, :] * u
            sgc = _sigmoid(zc)
            siluc = zc * sgc
            dac = ld(dac_ref)
            dbg_ref[:, cols] = (dac * conv * siluc).astype(BF16)
            dzc_ref[:, cols] = (dac * bg * conv * (sgc * (1.0 + zc * (1.0 - sgc)))).astype(BF16)
            dconv = dac * bg * siluc
            zn = ld(zc_n)
            dconv_n = jnp.where(last, 0.0, ld(dac_n) * ld(bg_n) * (zn * _sigmoid(zn)))
            d1 = jnp.where(row == ts - 1, dconv_n[0:1, :], pltpu.roll(dconv, ts - 1, 0))
            d2 = jnp.where(row == ts - 1, dconv_n[1:2, :], jnp.where(row == ts - 2, dconv_n[0:1, :], pltpu.roll(dconv, ts - 2, 0)))
            du = w[2:3, :] * dconv + w[1:2, :] * d1 + w[0:1, :] * d2
            dcg_ref[:, cols] = (du * cin).astype(BF16)
            dcin_ref[:, cols] = (du * cg).astype(BF16)
            dw_ref[0:1, cols] += jnp.sum(dconv * u2, axis=0, keepdims=True)
            dw_ref[1:2, cols] += jnp.sum(dconv * u1, axis=0, keepdims=True)
            dw_ref[2:3, cols] += jnp.sum(dconv * u, axis=0, keepdims=True)

    seg = lambda name: pl.BlockSpec((ts, D), lambda i: (i, SEG[name]))
    prev = lambda name: pl.BlockSpec((HALO, D), lambda i: (jnp.maximum(i * (ts // HALO) - 1, 0), SEG[name]))
    nxt = lambda blk: pl.BlockSpec((HALO, D), lambda i: (jnp.minimum((i + 1) * (ts // HALO), S // HALO - 1), blk))
    row = pl.BlockSpec((ts, D), lambda i: (i, 0))
    return pl.pallas_call(
        body, grid=(nblk,),
        in_specs=[row, seg("z_mla"), seg("c_in"), seg("b_gate"), seg("c_gate"), seg("z_conv"), row, row,
                  prev("c_in"), prev("c_gate"), nxt(SEG["b_gate"]), nxt(SEG["z_conv"]), nxt(0), pl.BlockSpec((8, D), lambda i: (0, 0))],
        out_specs=[row, pl.BlockSpec((H, 8, ts), lambda i: (0, 0, i)), row, row, row, row, row, pl.BlockSpec((8, D), lambda i: (0, 0))],
        out_shape=[jax.ShapeDtypeStruct((S, D), BF16), jax.ShapeDtypeStruct((H, 8, S), F32)] + [jax.ShapeDtypeStruct((S, D), BF16)] * 5
        + [jax.ShapeDtypeStruct((8, D), F32)],
        compiler_params=_cparams(("arbitrary",), 56 << 20), name="gates_bwd",
    )(attn, proj, proj, proj, proj, proj, da_mla, da_conv, proj, proj, proj, proj, da_conv, conv_w)


def _flash_bwd(q, k, v, do, lse, delta, tq, hb=2):
    S = q.shape[1]
    nq = S // tq
    scale = SM_SCALE
    tsub = min(256, tq)
    nsub = tq // tsub

    qi_tab, kj_tab = _causal_pairs(nq, key_major=True)
    npairs = nq * (nq + 1) // 2

    def body(qi_ref, kj_ref, q_ref, k_ref, v_ref, do_ref, lse_ref, dl_ref, dq_ref, dk_ref, dv_ref, dk_sc, dv_sc, dq_sc):
        p = pl.program_id(1)
        qi, kj = qi_ref[p], kj_ref[p]

        @pl.when(p == 0)
        def _():
            dq_sc[...] = jnp.zeros_like(dq_sc)

        @pl.when(qi == kj)
        def _():
            dk_sc[...] = jnp.zeros_like(dk_sc)
            dv_sc[...] = jnp.zeros_like(dv_sc)

        def step(diag):
            chains = [(h, u) for h in range(hb) for u in range(nsub)]

            def first_matmuls(h, u):
                sub = slice(u * tsub, (u + 1) * tsub)
                st = lax.dot_general(k_ref[h], q_ref[h, sub, :], NT, preferred_element_type=F32)
                dpt = lax.dot_general(v_ref[h], do_ref[sub, h * DV:(h + 1) * DV], NT, preferred_element_type=F32)
                return st, dpt

            nxt = first_matmuls(*chains[0])
            for ci, (h, u) in enumerate(chains):
                st, dpt = nxt
                if ci + 1 < len(chains):
                    nxt = first_matmuls(*chains[ci + 1])
                sub = slice(u * tsub, (u + 1) * tsub)
                pt = jnp.exp2(st - lse_ref[h, 0:1, sub])
                if diag:
                    r = lax.broadcasted_iota(jnp.int32, (tq, tsub), 0)
                    c = lax.broadcasted_iota(jnp.int32, (tq, tsub), 1) + u * tsub
                    pt = jnp.where(r <= c, pt, 0.0)
                dst = (pt * (dpt - dl_ref[h, 0:1, sub])).astype(BF16)
                dv_sc[h] += jnp.dot(pt.astype(BF16), do_ref[sub, h * DV:(h + 1) * DV], preferred_element_type=F32)
                dk_sc[h] += jnp.dot(dst, q_ref[h, sub, :], preferred_element_type=F32)
                rows = pl.ds(pl.multiple_of(qi * tq + u * tsub, tsub), tsub)
                dq_sc[h, rows, :] += lax.dot_general(dst, k_ref[h], TN, preferred_element_type=F32)

        @pl.when(qi > kj)
        def _():
            step(False)

        @pl.when(qi == kj)
        def _():
            step(True)

        @pl.when(qi == nq - 1)
        def _():
            dk_ref[...] = (dk_sc[...] * (1.0 / LOG2E)).astype(BF16)
            dv_ref[...] = dv_sc[...].astype(BF16)

        @pl.when(p == npairs - 1)
        def _():
            dq_ref[...] = (dq_sc[...] * scale).astype(BF16)

    return pl.pallas_call(
        body,
        grid_spec=pltpu.PrefetchScalarGridSpec(
            num_scalar_prefetch=2,
            grid=(H // hb, npairs),
            in_specs=[pl.BlockSpec((hb, tq, HW), lambda h, p, qi, kj: (h, qi[p], 0)),
                      pl.BlockSpec((hb, tq, HW), lambda h, p, qi, kj: (h, kj[p], 0)),
                      pl.BlockSpec((hb, tq, DV), lambda h, p, qi, kj: (h, kj[p], 0)),
                      pl.BlockSpec((tq, hb * DV), lambda h, p, qi, kj: (qi[p], h)),
                      pl.BlockSpec((hb, 8, tq), lambda h, p, qi, kj: (h, 0, qi[p])),
                      pl.BlockSpec((hb, 8, tq), lambda h, p, qi, kj: (h, 0, qi[p]))],
            out_specs=[pl.BlockSpec((hb, S, HW), lambda h, p, qi, kj: (h, 0, 0)),
                       pl.BlockSpec((hb, tq, HW), lambda h, p, qi, kj: (h, kj[p], 0)),
                       pl.BlockSpec((hb, tq, DV), lambda h, p, qi, kj: (h, kj[p], 0))],
            scratch_shapes=[pltpu.VMEM((hb, tq, HW), F32), pltpu.VMEM((hb, tq, DV), F32), pltpu.VMEM((hb, S, HW), F32)],
        ),
        out_shape=[jax.ShapeDtypeStruct((H, S, HW), BF16), jax.ShapeDtypeStruct((H, S, HW), BF16), jax.ShapeDtypeStruct((H, S, DV), BF16)],
        compiler_params=_cparams(("parallel", "arbitrary"), VMEM_CAP),
        name="flash_bwd",
    )(qi_tab, kj_tab, q, k, v, do, lse, delta)


def _rms_bwd(xf, g, dn_out):
    r = _rms_scale(xf)
    n = xf * r
    dn = dn_out * g
    dx = r * (dn - n * jnp.mean(dn * n, axis=-1, keepdims=True))
    return dx, jnp.sum(dn_out * n, axis=0, keepdims=True)


def _qkv_bwd(proj, pos, freq, g_qa, g_kva, w_q, w_kv, dq, dk, dv, segments, ts):
    S = proj.shape[0]
    nseg = len(segments)

    def body(qa_ref, ckv_ref, pos_ref, freq_ref, gq_ref, gk_ref, wq_ref, wkv_ref, dq_ref, dk_ref, dv_ref, *rest):
        seg_refs, (dqp_ref, dkvp_ref, dproj_ref, dgq_ref, dgk_ref) = rest[:nseg], rest[nseg:]
        for j, seg_ref in enumerate(seg_refs):
            dproj_ref[:, j * D:(j + 1) * D] = seg_ref[...]
        dqa_ref = dproj_ref.at[:, OFF_QA:OFF_QA + RQ]
        dckv_ref = dproj_ref.at[:, OFF_CKV:OFF_CKV + RKV]
        dkr_ref = dproj_ref.at[:, OFF_KR:OFF_KR + 128]
        i = pl.program_id(0)
        cos, sin, lane = _rope_tables(pos_ref, freq_ref, ts)
        sgn_sin = jnp.where(lane < 32, sin, -sin)
        live = lane < DR
        kr_sum = jnp.zeros((ts, 128), F32)
        for h in range(H):
            dqp_ref[:, h * HW:h * HW + DN] = dq_ref[h, :, 0:DN]
            t = dq_ref[h, :, DN:HW].astype(F32)
            dqp_ref[:, h * HW + DN:(h + 1) * HW] = jnp.where(live, t * cos + _rope_swap(t, lane) * sgn_sin, 0.0).astype(BF16)
            dkvp_ref[:, h * HW:h * HW + DN] = dk_ref[h, :, 0:DN]
            dkvp_ref[:, h * HW + DN:(h + 1) * HW] = dv_ref[h]
            kr_sum = kr_sum + dk_ref[h, :, DN:HW].astype(F32)
        dkr_ref[...] = jnp.where(live, kr_sum * cos + _rope_swap(kr_sum, lane) * sgn_sin, 0.0).astype(BF16)
        dqn = lax.dot_general(dqp_ref[...], wq_ref[...], NT, preferred_element_type=F32)
        dkvn = lax.dot_general(dkvp_ref[...], wkv_ref[...], NT, preferred_element_type=F32)
        dqa, dgq = _rms_bwd(_f32(qa_ref), gq_ref[...], dqn)
        dckv, dgk = _rms_bwd(_f32(ckv_ref), gk_ref[...], dkvn)
        dqa_ref[...] = dqa.astype(BF16)
        dckv_ref[...] = dckv.astype(BF16)

        @pl.when(i == 0)
        def _():
            dgq_ref[...] = jnp.zeros_like(dgq_ref)
            dgk_ref[...] = jnp.zeros_like(dgk_ref)

        dgq_ref[0:1, :] += dgq
        dgk_ref[0:1, :] += dgk

    rowb = lambda w, blk: pl.BlockSpec((ts, w), lambda i: (i, blk))
    full = lambda a: pl.BlockSpec(a.shape, lambda i: (0,) * a.ndim)
    heads = lambda w: pl.BlockSpec((H, ts, w), lambda i: (0, i, 0))
    return pl.pallas_call(
        body, grid=(S // ts,),
        in_specs=[rowb(RQ, OFF_QA // RQ), rowb(RKV, OFF_CKV // RKV), pl.BlockSpec((ts, 1), lambda i: (i, 0)), full(freq), full(g_qa), full(g_kva),
                  full(w_q), full(w_kv), heads(HW), heads(HW), heads(DV)] + [rowb(D, 0)] * nseg,
        out_specs=[rowb(H * HW, 0), rowb(H * HW, 0), rowb(NP, 0),
                   pl.BlockSpec((8, RQ), lambda i: (0, 0)), pl.BlockSpec((8, RKV), lambda i: (0, 0))],
        out_shape=[jax.ShapeDtypeStruct((S, H * HW), BF16), jax.ShapeDtypeStruct((S, H * HW), BF16), jax.ShapeDtypeStruct((S, NP), BF16),
                   jax.ShapeDtypeStruct((8, RQ), F32), jax.ShapeDtypeStruct((8, RKV), F32)],
        compiler_params=_cparams(("arbitrary",), 56 << 20), name="qkv_bwd",
    )(proj, proj, pos, freq, g_qa, g_kva, w_q, w_kv, dq, dk, dv, *segments)


def _prenorm_bwd(x, g, dh, dy, ts):
    S = x.shape[0]

    def body(x_ref, g_ref, dh_ref, dy_ref, gx_ref, dg_ref):
        dx, dg = _rms_bwd(x_ref[...], g_ref[...], dh_ref[...])
        gx_ref[...] = dy_ref[...] + dx

        @pl.when(pl.program_id(0) == 0)
        def _():
            dg_ref[...] = jnp.zeros_like(dg_ref)

        dg_ref[0:1, :] += dg

    row = pl.BlockSpec((ts, D), lambda i: (i, 0))
    return pl.pallas_call(
        body, grid=(S // ts,), in_specs=[row, pl.BlockSpec((1, D), lambda i: (0, 0)), row, row],
        out_specs=[row, pl.BlockSpec((8, D), lambda i: (0, 0))],
        out_shape=[jax.ShapeDtypeStruct((S, D), F32), jax.ShapeDtypeStruct((8, D), F32)],
        compiler_params=_cparams(("arbitrary",), 40 << 20), name="prenorm_bwd",
    )(x, g, dh, dy)


def _local_step(x, pos, target, g_pre, g_qa, g_kva, g_post, w_in, other_weights, reduce_grads=None):
    S = x.shape[0]
    ts = min(256, S)
    tq = min(512, S)
    tm = min(1024, S)
    mm = functools.partial(_matmul, tm=tm)
    freq = _inv_freq_tile()

    h, ht = _prenorm(x, g_pre, ts)
    if len(other_weights) == 2:
        gather, assemble = other_weights
        proj, gathered = _matmul(h, w_in, mode="nn", out_dtype=BF16, tm=2048, tn=1408, tk=D, name="mm_proj", exchange=gather)
        other_weights = assemble(*gathered)
    else:
        proj = _matmul(h, w_in, mode="nn", out_dtype=BF16, tm=2048, tn=1408, tk=D, name="mm_proj")
    w_q, w_kv, conv_w8, w_o_mla, w_o_conv, w_out = other_weights
    q, k, v, vt, qn, kvn = _qkv_prep(proj, pos, freq, g_qa, g_kva, w_q, w_kv, ts)
    attn, lse = _flash_fwd(q, k, vt, tq)
    a_mla, a_conv = _gates_fwd(proj, attn, conv_w8, ts)
    y_mla = mm(a_mla, w_o_mla, mode="nn", out_dtype=BF16, tn=D, tk=D, name="mm_y_mla")
    y_conv = mm(a_conv, w_o_conv, mode="nn", out_dtype=BF16, tn=D, tk=D, name="mm_y_conv")
    merged = _merge_fwd(proj, y_mla, y_conv, ts)
    out = mm(merged, w_out, mode="nn", out_dtype=F32, tn=D, tk=D, name="mm_out")
    dy, dout, dg_post, loss = _post_loss(out, x, target, g_post, ts)

    dmerged = mm(dout, w_out, mode="nt", out_dtype=BF16, tn=D, tk=D, name="mm_dmerged")
    dw_out = _matmul(merged, dout, mode="tn", out_dtype=F32, tm=1024, tn=1024, tk=2048, name="mm_dw_out")
    dy_mla, dy_conv, dg_mla, dg_conv = _merge_bwd(proj, y_mla, y_conv, dmerged, ts)
    da_mla = mm(dy_mla, w_o_mla, mode="nt", out_dtype=BF16, tn=D, tk=D, name="mm_da_mla")
    da_conv = mm(dy_conv, w_o_conv, mode="nt", out_dtype=BF16, tn=D, tk=D, name="mm_da_conv")
    dw_o_mla = _matmul(a_mla, dy_mla, mode="tn", out_dtype=F32, tm=1024, tn=1024, tk=2048, name="mm_dw_o_mla")
    dw_o_conv = _matmul(a_conv, dy_conv, mode="tn", out_dtype=F32, tm=1024, tn=1024, tk=2048, name="mm_dw_o_conv")
    dattn, delta, dz_mla, dc_in, db_gate, dc_gate, dz_conv, dconv_w = _gates_bwd(proj, attn, da_mla, da_conv, conv_w8, min(128, S))
    dq, dk, dv = _flash_bwd(q, k, v, dattn, lse, delta, tq)
    dqp, dkvp, dproj, dg_qa, dg_kva = _qkv_bwd(proj, pos, freq, g_qa, g_kva, w_q, w_kv, dq, dk, dv,
                                               [dz_mla, dc_in, db_gate, dc_gate, dz_conv, dg_mla, dg_conv], min(128, S))
    dw_q = _matmul(qn, dqp, mode="tn", out_dtype=F32, tm=RQ, tn=1024, tk=2048, name="mm_dw_q")
    dw_kv = _matmul(kvn, dkvp, mode="tn", out_dtype=F32, tm=RKV, tn=1024, tk=2048, name="mm_dw_kv")
    dw_in = _matmul(ht, dproj, mode="nn", out_dtype=F32, tm=1024, tn=1408, tk=2048, name="mm_dw_in")
    res = dict(dw_in=dw_in, dw_q=dw_q, dw_kv=dw_kv, dconv_w=dconv_w, dw_o_mla=dw_o_mla, dw_o_conv=dw_o_conv, dw_out=dw_out)
    if reduce_grads is None:
        dh = _matmul(dproj, w_in, mode="nt", out_dtype=F32, tm=1024, tn=D, tk=1408, name="mm_dh")
    else:
        exchange, finish = reduce_grads(res)
        dh, got = _matmul(dproj, w_in, mode="nt", out_dtype=F32, tm=1024, tn=D, tk=1408, name="mm_dh", exchange=exchange)
        res["reduced"] = finish(got)
    grad_x, dg_pre = _prenorm_bwd(x, g_pre, dh, dy, ts)
    res.update(loss=loss, grad_x=grad_x, dg_pre=dg_pre, dg_qa=dg_qa, dg_kva=dg_kva, dg_post=dg_post)
    return res


def _my_id():
    return lax.axis_index("x") * 4 + lax.axis_index("y") * 2 + lax.axis_index("c")


def _place():
    x, y, c = lax.axis_index("x"), lax.axis_index("y"), lax.axis_index("c")
    return (x, y, c), (x, y, 1 - c), [(1 - x, y), (x, 1 - y), (1 - x, 1 - y)]


def _slot(px, py, pc):
    return 4 * px + 2 * py + pc


def _all_gather(arrays, name):
    n = len(arrays)

    def body(*refs):
        ins, outs = refs[:n], refs[n:2 * n]
        send_sems, recv_sems, local_sems = refs[2 * n:]
        me, sib, chips = _place()
        c = me[2]

        def copy(a, k, block, to, src=None):
            rows = outs[a].at[_slot(*block)]
            return pltpu.make_async_remote_copy(src_ref=rows if src is None else src, dst_ref=rows, send_sem=send_sems.at[a, k],
                                                recv_sem=recv_sems.at[a, k], device_id=to, device_id_type=MESH)

        local = [pltpu.make_async_copy(ins[a], outs[a].at[_slot(*me)], local_sems.at[a]) for a in range(n)]
        for cp in local:
            cp.start()
        sends = []
        for a in range(n):
            sends.append(copy(a, 0, me, sib, src=ins[a]))
            sends += [copy(a, 1 + j, me, (*chip, c), src=ins[a]) for j, chip in enumerate(chips)]
        for cp in sends:
            cp.start()
        for a in range(n):
            for j, chip in enumerate(chips):
                copy(a, 1 + j, (*chip, c), me).wait_recv()
                fwd = copy(a, 4 + j, (*chip, c), sib)
                fwd.start()
                sends.append(fwd)
        for a in range(n):
            copy(a, 0, sib, me).wait_recv()
            for j, chip in enumerate(chips):
                copy(a, 4 + j, (*chip, 1 - c), me).wait_recv()
        for cp in sends:
            cp.wait_send()
        for cp in local:
            cp.wait()

    anyspec = pl.BlockSpec(memory_space=pl.ANY)
    return pl.pallas_call(
        body,
        in_specs=[anyspec] * n,
        out_specs=[anyspec] * n,
        out_shape=[jax.ShapeDtypeStruct((NDEV,) + a.shape, a.dtype) for a in arrays],
        scratch_shapes=[pltpu.SemaphoreType.DMA((n, NDEV - 1)), pltpu.SemaphoreType.DMA((n, NDEV - 1)), pltpu.SemaphoreType.DMA((n,))],
        name=name,
    )(*arrays)


def _all_gather_relayed(x, name):
    R, C = x.shape
    half = R // 2

    def body(x_ref, out_ref, send_sems, recv_sems, local_sem):
        (mx, my, c), sib, _ = _place()
        xn, yn, dg = (1 - mx, my, c), (mx, 1 - my, c), (1 - mx, 1 - my, c)

        def rows(block, h):
            return out_ref.at[_slot(*block), pl.ds(h * half, half), :]

        def copy(k, dst, to, src=None):
            return pltpu.make_async_remote_copy(src_ref=dst if src is None else src, dst_ref=dst, send_sem=send_sems.at[k],
                                                recv_sem=recv_sems.at[k], device_id=to, device_id_type=MESH)

        me = (mx, my, c)
        local = pltpu.make_async_copy(x_ref, out_ref.at[_slot(*me)], local_sem)
        local.start()
        sends = [copy(0, out_ref.at[_slot(*me)], sib, src=x_ref)]
        for h in range(2):
            sends.append(copy(1 + h, rows(me, h), xn, src=x_ref.at[pl.ds(h * half, half), :]))
            sends.append(copy(3 + h, rows(me, h), yn, src=x_ref.at[pl.ds(h * half, half), :]))
        for cp in sends:
            cp.start()
        copy(3, rows(yn, 0), me).wait_recv()
        relay_x = copy(5, rows(yn, 0), xn)
        relay_x.start()
        copy(1, rows(xn, 0), me).wait_recv()
        copy(2, rows(xn, 1), me).wait_recv()
        relay_y = copy(6, rows(xn, 1), yn)
        relay_y.start()
        sends += [relay_x, relay_y]

        def to_sibling(k, block):
            fwd = copy(k, out_ref.at[_slot(*block)], sib)
            fwd.start()
            sends.append(fwd)

        to_sibling(7, xn)
        copy(4, rows(yn, 1), me).wait_recv()
        to_sibling(8, yn)
        copy(5, rows(dg, 0), me).wait_recv()
        copy(6, rows(dg, 1), me).wait_recv()
        to_sibling(9, dg)
        copy(0, out_ref.at[_slot(*sib)], me).wait_recv()
        for k, block in ((7, xn), (8, yn), (9, dg)):
            copy(k, out_ref.at[_slot(block[0], block[1], 1 - c)], me).wait_recv()
        for cp in sends:
            cp.wait_send()
        local.wait()

    anyspec = pl.BlockSpec(memory_space=pl.ANY)
    return pl.pallas_call(
        body, in_specs=[anyspec], out_specs=anyspec, out_shape=jax.ShapeDtypeStruct((NDEV, R, C), x.dtype),
        scratch_shapes=[pltpu.SemaphoreType.DMA((10,)), pltpu.SemaphoreType.DMA((10,)), pltpu.SemaphoreType.DMA],
        name=name,
    )(x)


def _slab_exchange(arrays, nslots, pick):
    n = len(arrays)

    def copies(ins, outs, sems):
        send_sems, recv_sems = sems
        return [pltpu.make_async_remote_copy(src_ref=ins[a].at[pick(k)[0]], dst_ref=outs[a].at[k], send_sem=send_sems.at[a, k],
                                             recv_sem=recv_sems.at[a, k], device_id=pick(k)[1], device_id_type=MESH)
                for a in range(n) for k in range(nslots)]

    def start(ins, outs, sems):
        for cp in copies(ins, outs, sems):
            cp.start()

    def finish(ins, outs, sems):
        cps = copies(ins, outs, sems)
        for cp in cps:
            cp.wait_recv()
        for cp in cps:
            cp.wait_send()

    return _Exchange(arrays, [jax.ShapeDtypeStruct((nslots,) + a.shape[1:], a.dtype) for a in arrays],
                     [pltpu.SemaphoreType.DMA((n, nslots)), pltpu.SemaphoreType.DMA((n, nslots))], start, finish)


def _direct_gather(arrays):
    n = len(arrays)

    def peer(d):
        p = (_my_id() + d) % NDEV
        return (p // 4, (p // 2) % 2, p % 2), p

    def copies(ins, outs, sems, receiving):
        send_sems, recv_sems, _ = sems
        slot = lambda d: peer(NDEV - d)[1] if receiving else _my_id()
        return [pltpu.make_async_remote_copy(src_ref=ins[a], dst_ref=outs[a].at[slot(d)], send_sem=send_sems.at[a, d - 1],
                                             recv_sem=recv_sems.at[a, d - 1], device_id=peer(d)[0], device_id_type=MESH)
                for d in range(1, NDEV) for a in range(n)]

    def local(ins, outs, sems):
        return [pltpu.make_async_copy(ins[a], outs[a].at[_my_id()], sems[2].at[a]) for a in range(n)]

    def start(ins, outs, sems):
        for cp in local(ins, outs, sems) + copies(ins, outs, sems, False):
            cp.start()

    def finish(ins, outs, sems):
        for cp in copies(ins, outs, sems, True):
            cp.wait_recv()
        for cp in copies(ins, outs, sems, False):
            cp.wait_send()
        for cp in local(ins, outs, sems):
            cp.wait()

    return _Exchange(arrays, [jax.ShapeDtypeStruct((NDEV,) + a.shape, a.dtype) for a in arrays],
                     [pltpu.SemaphoreType.DMA((n, NDEV - 1)), pltpu.SemaphoreType.DMA((n, NDEV - 1)), pltpu.SemaphoreType.DMA((n,))],
                     start, finish)


def _run_exchange(exchange, name):
    n = len(exchange.arrays)

    def body(*refs):
        ins, outs, sems = refs[:n], refs[n:2 * n], refs[2 * n:]
        exchange.start(ins, outs, sems)
        exchange.finish(ins, outs, sems)

    anyspec = pl.BlockSpec(memory_space=pl.ANY)
    return pl.pallas_call(body, in_specs=[anyspec] * n, out_specs=[anyspec] * n, out_shape=exchange.out_shapes,
                          scratch_shapes=exchange.sem_shapes, name=name)(*exchange.arrays)


def _slabs_bf16(x, rows_per_block, name):
    n, R, C = x.shape
    tr = min(rows_per_block, R)

    def body(x_ref, o_ref):
        o_ref[...] = x_ref[...].astype(BF16)

    blk = pl.BlockSpec((1, tr, C), lambda k, i: (k, i, 0))
    return pl.pallas_call(body, grid=(n, R // tr), in_specs=[blk], out_specs=blk, out_shape=jax.ShapeDtypeStruct(x.shape, BF16),
                          compiler_params=_cparams(("parallel", "parallel"), 32 << 20), name=name)(x)


def _to_sibling(k):
    me, sib, chips = _place()
    dest = sib if k == 0 else (*chips[k - 1], sib[2])
    return _slot(*dest), sib


def _to_chips(k):
    me, sib, chips = _place()
    return k, (*chips[k], me[2])


def _chip_sums(own, got, slots, rows_per_block, name):
    _, R, C = own.shape
    tr = min(rows_per_block, R)

    def body(slots_ref, own_ref, got_ref, o_ref):
        o_ref[0] = (own_ref[0].astype(F32) + got_ref[0].astype(F32)).astype(BF16)

    return pl.pallas_call(
        body,
        grid_spec=pltpu.PrefetchScalarGridSpec(
            num_scalar_prefetch=1, grid=(3, R // tr),
            in_specs=[pl.BlockSpec((1, tr, C), lambda j, i, s: (s[1 + j], i, 0)), pl.BlockSpec((1, tr, C), lambda j, i, s: (1 + j, i, 0))],
            out_specs=pl.BlockSpec((1, tr, C), lambda j, i, s: (j, i, 0))),
        out_shape=jax.ShapeDtypeStruct((3, R, C), BF16),
        compiler_params=_cparams(("parallel", "parallel"), 32 << 20), name=name,
    )(slots, own, got)


def _adamw_math(w, g, m, v):
    m = ADAM_B1 * m + (1.0 - ADAM_B1) * g
    v = ADAM_B2 * v + (1.0 - ADAM_B2) * (g * g)
    m_hat = m / (1.0 - ADAM_B1 ** ADAM_STEP)
    v_hat = v / (1.0 - ADAM_B2 ** ADAM_STEP)
    delta = -ADAM_LR * (m_hat / (jnp.sqrt(v_hat) + ADAM_EPS) + ADAM_WD * w)
    return delta, m, v


def _reduce_adamw(own, got1, got2, slots, w, m, v, rows_per_block, name):
    _, R, C = own.shape
    tr = min(rows_per_block, R)
    assert R % tr == 0

    def body(slots_ref, own_ref, g1_ref, g2_ref, w_ref, m_ref, v_ref, g_ref, d_ref, nm_ref, nv_ref):
        g = own_ref[0] + g1_ref[0].astype(F32)
        for j in range(3):
            g = g + g2_ref[j].astype(F32)
        g_ref[...] = g
        d, nm, nv = _adamw_math(w_ref[...], g, m_ref[...], v_ref[...])
        d_ref[...] = d
        nm_ref[...] = nm
        nv_ref[...] = nv

    blk = pl.BlockSpec((tr, C), lambda i, s: (i, 0))
    return pl.pallas_call(
        body,
        grid_spec=pltpu.PrefetchScalarGridSpec(
            num_scalar_prefetch=1, grid=(R // tr,),
            in_specs=[pl.BlockSpec((1, tr, C), lambda i, s: (s[0], i, 0)), pl.BlockSpec((1, tr, C), lambda i, s: (0, i, 0)),
                      pl.BlockSpec((3, tr, C), lambda i, s: (0, i, 0)), blk, blk, blk],
            out_specs=[blk] * 4),
        out_shape=[jax.ShapeDtypeStruct((R, C), F32)] * 4,
        compiler_params=_cparams(("parallel",), 48 << 20), name=name,
    )(slots, own, got1, got2, w, m, v)


def _reduce_adamw_t(own, got1, got2, slots, wt, mt, vt, rows_per_block, name):
    R, C = own.shape
    tr = min(rows_per_block, R)
    assert R % tr == 0

    def body(slots_ref, own_ref, g1_ref, g2_ref, w_ref, m_ref, v_ref, g_ref, d_ref, nm_ref, nv_ref):
        g = own_ref[...] + g1_ref[0].astype(F32)
        for j in range(3):
            g = g + g2_ref[j].astype(F32)
        g = jnp.transpose(g)
        g_ref[...] = g
        d, nm, nv = _adamw_math(w_ref[...], g, m_ref[...], v_ref[...])
        d_ref[...] = d
        nm_ref[...] = nm
        nv_ref[...] = nv

    blk = pl.BlockSpec((C, tr), lambda i, s: (0, i))
    return pl.pallas_call(
        body,
        grid_spec=pltpu.PrefetchScalarGridSpec(
            num_scalar_prefetch=1, grid=(R // tr,),
            in_specs=[pl.BlockSpec((tr, C), lambda i, s: (i, 0)), pl.BlockSpec((1, tr, C), lambda i, s: (0, i, 0)),
                      pl.BlockSpec((3, tr, C), lambda i, s: (0, i, 0)), blk, blk, blk],
            out_specs=[blk] * 4),
        out_shape=[jax.ShapeDtypeStruct((C, R), F32)] * 4,
        compiler_params=_cparams(("parallel",), 48 << 20), name=name,
    )(slots, own, got1, got2, wt, mt, vt)


def _sum_adamw(parts, w, m, v, rows_per_block, name):
    n, R, C = parts.shape
    tr = min(rows_per_block, R)
    assert R % tr == 0

    def body(p_ref, w_ref, m_ref, v_ref, g_ref, d_ref, nm_ref, nv_ref):
        g = p_ref[0]
        for j in range(1, n):
            g = g + p_ref[j]
        g_ref[...] = g
        d, nm, nv = _adamw_math(w_ref[...], g, m_ref[...], v_ref[...])
        d_ref[...] = d
        nm_ref[...] = nm
        nv_ref[...] = nv

    blk = pl.BlockSpec((tr, C), lambda i: (i, 0))
    return pl.pallas_call(
        body, grid=(R // tr,),
        in_specs=[pl.BlockSpec((n, tr, C), lambda i: (0, i, 0)), blk, blk, blk],
        out_specs=[blk] * 4,
        out_shape=[jax.ShapeDtypeStruct((R, C), F32)] * 4,
        compiler_params=_cparams(("parallel",), 48 << 20), name=name,
    )(parts, w, m, v)


def kernel(x, positions, pre_norm_g, w_in, q_a_norm_g, w_q_b, kv_a_norm_g, w_kv_b, conv_w, w_o_mla, w_o_conv, w_out, post_norm_g, loss_target, m_pre_norm_g, m_w_in, m_q_a_norm_g, m_w_q_b, m_kv_a_norm_g, m_w_kv_b, m_conv_w, m_w_o_mla, m_w_o_conv, m_w_out, m_post_norm_g, v_pre_norm_g, v_w_in, v_q_a_norm_g, v_w_q_b, v_kv_a_norm_g, v_w_kv_b, v_conv_w, v_w_o_mla, v_w_o_conv, v_w_out, v_post_norm_g):
    S = x.shape[1]
    conv_pad = jnp.zeros((8, 256), F32).at[0:3, :].set(conv_w)
    g_in = _all_gather_relayed(w_in.astype(BF16), "all_gather_w_in")
    w_in_f = _assemble_w_in(g_in)
    gather_rest = _direct_gather([w_q_b.astype(BF16), w_kv_b.astype(BF16), conv_pad, w_o_mla.astype(BF16), w_o_conv.astype(BF16),
                                  w_out.astype(BF16)])

    def assemble_rest(g_q, g_kv, g_cw, g_om, g_oc, g_out):
        return (_assemble_w_q(g_q), _concat_cols(g_kv, BF16, "assemble_w_kv"), _concat_cols(g_cw, F32, "assemble_conv_w"),
                g_om.reshape(D, D), g_oc.reshape(D, D), g_out.reshape(D, D))

    gnames = ["w_in", "w_q", "w_kv", "conv_w", "w_o_mla", "w_o_conv", "w_out"]
    (mx, my, mc), _, chips = _place()
    slots = jnp.stack([_slot(mx, my, mc)] + [_slot(cx, cy, mc) for cx, cy in chips]).astype(jnp.int32)

    def reduce_grads(r):
        own_in, slabs_in = _split_dw_in(r["dw_in"], slots)
        own = [own_in, _split_dw_q(r["dw_q"]), _split_cols(r["dw_kv"], NDEV, "split_dw_kv"),
               _split_cols(r["dconv_w"], NDEV, "split_dconv_w"), r["dw_o_mla"].reshape(NDEV, D // NDEV, D),
               r["dw_o_conv"].reshape(NDEV, D // NDEV, D), r["dw_out"].reshape(NDEV, D // NDEV, D)]
        slabs = [slabs_in] + [_slabs_bf16(g, 128, "bf16_" + nm) for g, nm in zip(own[1:], gnames[1:])]
        to_sibling = _slab_exchange(slabs, 4, _to_sibling)
        got1 = _run_exchange(to_sibling, "grads_to_sibling")
        sums = [_chip_sums(o, g1, slots, 128, "chip_sum_" + nm) for o, g1, nm in zip([slabs_in] + own[1:], got1, gnames)]
        return _slab_exchange(sums, 3, _to_chips), lambda got2: (own, got1, got2)

    row2 = lambda a: a.reshape(1, -1)
    r = _local_step(x[0], positions.reshape(S, 1), loss_target[0], row2(pre_norm_g), row2(q_a_norm_g), row2(kv_a_norm_g),
                    row2(post_norm_g), w_in_f, (gather_rest, assemble_rest), reduce_grads)
    own, got1, got2 = r["reduced"]
    small = jnp.concatenate([r["dg_pre"][0:1], r["dg_post"][0:1], jnp.pad(r["dg_qa"][0:1], ((0, 0), (0, D - RQ))),
                             jnp.pad(r["dg_kva"][0:1], ((0, 0), (0, D - RKV))), jnp.pad(r["loss"][0:1], ((0, 0), (0, D - 128))),
                             jnp.zeros((3, D), F32)], axis=0)
    p_small, = _all_gather([small], "all_gather_small")
    pad8 = lambda a: jnp.zeros((8, 256), F32).at[0:3, :].set(a)
    params = [(w_in, m_w_in, v_w_in), (w_q_b, m_w_q_b, v_w_q_b), (w_kv_b, m_w_kv_b, v_w_kv_b), (conv_pad, pad8(m_conv_w), pad8(v_conv_w)),
              (w_o_mla, m_w_o_mla, v_w_o_mla), (w_o_conv, m_w_o_conv, v_w_o_conv), (w_out, m_w_out, v_w_out)]
    o_q, o_kv, o_cw, o_om, o_oc, o_out = [
        _reduce_adamw(o, g1, g2, slots, w, m, v, 128, "adamw_" + nm)
        for o, g1, g2, (w, m, v), nm in list(zip(own, got1, got2, params, gnames))[1:]]
    o_cw = [a[0:3] for a in o_cw]
    o_in = [a.T for a in _reduce_adamw_t(own[0], got1[0], got2[0], slots, w_in.T, m_w_in.T, v_w_in.T, 128, "adamw_w_in")]
    padv = lambda a: jnp.pad(a.reshape(1, -1), ((0, 0), (0, D - a.shape[0])))
    stack = lambda pre, post, qa, kva: jnp.concatenate([row2(pre), row2(post), padv(qa), padv(kva), jnp.zeros((4, D), F32)], axis=0)
    o_g = _sum_adamw(p_small, stack(pre_norm_g, post_norm_g, q_a_norm_g, kv_a_norm_g),
                     stack(m_pre_norm_g, m_post_norm_g, m_q_a_norm_g, m_kv_a_norm_g),
                     stack(v_pre_norm_g, v_post_norm_g, v_q_a_norm_g, v_kv_a_norm_g), 8, "adamw_gains")
    loss = o_g[0][4, 0]
    outs = {}
    for idx, kind in enumerate(("grad", "delta", "new_m", "new_v")):
        o = o_g[idx]
        outs[kind] = dict(pre_norm_g=o[0], w_in=o_in[idx], q_a_norm_g=o[2, 0:RQ], w_q_b=o_q[idx], kv_a_norm_g=o[3, 0:RKV],
                          w_kv_b=o_kv[idx], conv_w=o_cw[idx], w_o_mla=o_om[idx], w_o_conv=o_oc[idx], w_out=o_out[idx], post_norm_g=o[1])
    names = ["pre_norm_g", "w_in", "q_a_norm_g", "w_q_b", "kv_a_norm_g", "w_kv_b", "conv_w", "w_o_mla", "w_o_conv", "w_out", "post_norm_g"]
    return (loss, r["grad_x"][None], *[outs["grad"][n] for n in names], *[outs["delta"][n] for n in names],
            *[outs["new_m"][n] for n in names], *[outs["new_v"][n] for n in names])
```

```python
import functools
import math

import jax
import jax.numpy as jnp
from jax import lax
from jax.experimental import pallas as pl
from jax.experimental.pallas import tpu as pltpu

F32 = jnp.float32
BF16 = jnp.bfloat16

NDEV = 8
D = 2048
H = 16
DN = 128
DR = 64
DV = 128
RQ = 512
RKV = 512
HW = 256
ROPE_THETA = 10000.0
RMS_EPS = 1e-6
N_IN = 15424
SHARD_IN = N_IN // NDEV
SMALL = RQ + RKV + DR
NP = 7 * D + RQ + RKV + 128
SEG = dict(z_mla=0, c_in=1, b_gate=2, c_gate=3, z_conv=4, g_mla=5, g_conv=6)
OFF_QA = 7 * D
OFF_CKV = OFF_QA + RQ
OFF_KR = OFF_CKV + RKV
EXT = 2176
VMEM_CAP = 56 * 1024 * 1024

ADAM_LR = 0.001
ADAM_B1 = 0.9
ADAM_B2 = 0.999
ADAM_EPS = 1e-08
ADAM_WD = 0.01
ADAM_STEP = 10

LOG2E = math.log2(math.e)
SM_SCALE = 1.0 / math.sqrt(DN + DR)
QSCALE = SM_SCALE * LOG2E
NN = (((1,), (0,)), ((), ()))
NT = (((1,), (1,)), ((), ()))
TN = (((0,), (0,)), ((), ()))
MESH = pl.DeviceIdType.MESH


def _cparams(sem, vmem_bytes):
    return pltpu.CompilerParams(dimension_semantics=sem, vmem_limit_bytes=int(min(VMEM_CAP, max(vmem_bytes, 16 << 20))))


def _nbytes(shape, dtype):
    return math.prod(shape) * jnp.dtype(dtype).itemsize


class _Exchange:
    def __init__(self, arrays, out_shapes, sem_shapes, start, finish):
        self.arrays, self.out_shapes, self.sem_shapes, self.start, self.finish = arrays, out_shapes, sem_shapes, start, finish
        self.n_sems = len(sem_shapes)


def _matmul(a, b, *, mode, out_dtype, tm, tn, tk, name, m_outer=False, exchange=None):
    if mode == "nn":
        (M, K), (K2, N) = a.shape, b.shape
    elif mode == "nt":
        (M, K), (N, K2) = a.shape, b.shape
    else:
        (K, M), (K2, N) = a.shape, b.shape
    assert K == K2, (a.shape, b.shape, mode)
    tm, tn, tk = min(tm, M), min(tn, N), min(tk, K)
    assert M % tm == 0 and N % tn == 0 and K % tk == 0, (M, N, K, tm, tn, tk)
    ni, nj, nk = M // tm, N // tn, K // tk
    dims = dict(nn=NN, nt=NT, tn=TN)[mode]
    b_mode = pl.Buffered(1) if (nj == 1 and nk == 1) else None

    if m_outer:
        grid = (ni, nj, nk)
        ij = lambda g0, g1: (g0, g1)
    else:
        grid = (nj, ni, nk)
        ij = lambda g0, g1: (g1, g0)

    if mode == "tn":
        a_spec = pl.BlockSpec((tk, tm), lambda g0, g1, k: (k, ij(g0, g1)[0]))
        a_tile = (tk, tm)
    else:
        a_spec = pl.BlockSpec((tm, tk), lambda g0, g1, k: (ij(g0, g1)[0], k))
        a_tile = (tm, tk)
    if mode == "nt":
        b_spec = pl.BlockSpec((tn, tk), lambda g0, g1, k: (ij(g0, g1)[1], k), pipeline_mode=b_mode)
    else:
        b_spec = pl.BlockSpec((tk, tn), lambda g0, g1, k: (k, ij(g0, g1)[1]), pipeline_mode=b_mode)
    o_spec = pl.BlockSpec((tm, tn), lambda g0, g1, k: ij(g0, g1))

    n_in = len(exchange.arrays) if exchange else 0

    def body(a_ref, b_ref, *refs):
        x_in, o_ref, x_out, scratch = refs[:n_in], refs[n_in], refs[n_in + 1:2 * n_in + 1], refs[2 * n_in + 1:]
        if exchange:
            sems = scratch[len(scratch) - exchange.n_sems:]
            step = (pl.program_id(0) * grid[1] + pl.program_id(1)) * grid[2] + pl.program_id(2)

            @pl.when(step == 0)
            def _():
                exchange.start(x_in, x_out, sems)

        prod = lax.dot_general(a_ref[...], b_ref[...], dims, preferred_element_type=F32)
        if nk == 1:
            o_ref[...] = prod.astype(o_ref.dtype)
        else:
            acc_ref = scratch[0]
            k = pl.program_id(2)

            @pl.when(k == 0)
            def _():
                acc_ref[...] = prod

            @pl.when((k > 0) & (k < nk - 1))
            def _():
                acc_ref[...] += prod

            @pl.when(k == nk - 1)
            def _():
                o_ref[...] = (acc_ref[...] + prod).astype(o_ref.dtype)

        if exchange:
            @pl.when(step == grid[0] * grid[1] * grid[2] - 1)
            def _():
                exchange.finish(x_in, x_out, sems)

    vmem = 2 * (_nbytes(a_tile, a.dtype) + _nbytes((tk, tn), b.dtype) + _nbytes((tm, tn), out_dtype)) + 2 * _nbytes((tm, tn), F32)
    anyspec = pl.BlockSpec(memory_space=pl.ANY)
    outs = pl.pallas_call(
        body,
        grid=grid,
        in_specs=[a_spec, b_spec] + [anyspec] * n_in,
        out_specs=[o_spec] + [anyspec] * n_in,
        out_shape=[jax.ShapeDtypeStruct((M, N), out_dtype)] + (exchange.out_shapes if exchange else []),
        scratch_shapes=([] if nk == 1 else [pltpu.VMEM((tm, tn), F32)]) + (exchange.sem_shapes if exchange else []),
        compiler_params=_cparams(("arbitrary",) * 3 if exchange else ("parallel", "parallel", "arbitrary"), vmem + (8 << 20)),
        name=name,
    )(a, b, *(exchange.arrays if exchange else []))
    return (outs[0], outs[1:]) if exchange else outs[0]


def _in_dest(k):
    return SHARD_IN * k - SMALL


def _assemble_w_in(g):
    R = 128

    def body(g_ref, o_ref, ext, acc):
        ext[...] = jnp.zeros_like(ext)
        acc[...] = jnp.zeros_like(acc)
        lane = lax.broadcasted_iota(jnp.int32, (R, EXT), 1)
        ext[:, 0:SHARD_IN] = g_ref[0].astype(F32)
        v = ext[...]
        acc[:, OFF_QA:NP] = jnp.where(lane[:, 0:NP - OFF_QA] < SMALL, v[:, 0:NP - OFF_QA], 0.0)
        w = v[:, 1024:2048]
        w = pltpu.roll(w, 1024 - 64, 1)
        acc[:, 0:1024] = jnp.where(lane[:, 0:1024] < SHARD_IN - SMALL, w, 0.0)
        for k in range(1, NDEV):
            ext[:, 0:SHARD_IN] = g_ref[k].astype(F32)
            dest = _in_dest(k)
            t, o = dest // 128, dest % 128
            width = -(-(o + SHARD_IN) // 128) * 128
            v = pltpu.roll(ext[...], o, 1)[:, 0:width]
            acc[:, 128 * t:128 * t + width] += v
        o_ref[...] = acc[...].astype(BF16)

    return pl.pallas_call(
        body,
        grid=(D // R,),
        in_specs=[pl.BlockSpec((NDEV, R, SHARD_IN), lambda i: (0, i, 0))],
        out_specs=pl.BlockSpec((R, NP), lambda i: (i, 0)),
        out_shape=jax.ShapeDtypeStruct((D, NP), BF16),
        scratch_shapes=[pltpu.VMEM((R, EXT), F32), pltpu.VMEM((R, NP), F32)],
        compiler_params=_cparams(("parallel",), 40 << 20),
        name="assemble_w_in",
    )(g)


def _split_dw_in(dw, slots):
    R = 128

    def body(slots_ref, dw_ref, o_ref, ob_ref):
        def put(k, cols, v):
            ob_ref[k, :, cols] = v.astype(BF16)

            @pl.when(slots_ref[0] == k)
            def _():
                o_ref[:, cols] = v

        lane = lax.broadcasted_iota(jnp.int32, (R, 1024), 1)
        put(0, slice(0, 1024), dw_ref[:, OFF_QA:OFF_QA + 1024])
        tail = dw_ref[:, OFF_QA + 1024:NP]
        tail = jnp.concatenate([tail, jnp.zeros((R, 1024 - 128), F32)], axis=1)
        head = dw_ref[:, 0:1024]
        mixed = jnp.where(lane < 64, tail, pltpu.roll(head, 64, 1))
        put(0, slice(1024, SHARD_IN), mixed[:, 0:SHARD_IN - 1024])
        for k in range(1, NDEV):
            dest = _in_dest(k)
            t, o = dest // 128, dest % 128
            width = -(-(o + SHARD_IN) // 128) * 128
            v = dw_ref[:, 128 * t:128 * t + width]
            v = pltpu.roll(v, width - o, 1)
            put(k, slice(0, SHARD_IN), v[:, 0:SHARD_IN])

    return pl.pallas_call(
        body,
        grid_spec=pltpu.PrefetchScalarGridSpec(
            num_scalar_prefetch=1, grid=(D // R,),
            in_specs=[pl.BlockSpec((R, NP), lambda i, s: (i, 0))],
            out_specs=[pl.BlockSpec((R, SHARD_IN), lambda i, s: (i, 0)), pl.BlockSpec((NDEV, R, SHARD_IN), lambda i, s: (0, i, 0))]),
        out_shape=[jax.ShapeDtypeStruct((D, SHARD_IN), F32), jax.ShapeDtypeStruct((NDEV, D, SHARD_IN), BF16)],
        compiler_params=_cparams(("parallel",), 48 << 20),
        name="split_dw_in",
    )(slots, dw)


def _assemble_w_q(g):
    def body(g_ref, o_ref):
        lane = lax.broadcasted_iota(jnp.int32, (RQ, 128), 1)
        lo = lane < 64
        for k in range(NDEV):
            t0 = g_ref[k, :, 0:128].astype(F32)
            t1 = g_ref[k, :, 128:256].astype(F32)
            t2 = g_ref[k, :, 256:384].astype(F32)
            base = 2 * k * HW
            o_ref[:, base:base + 128] = t0.astype(BF16)
            o_ref[:, base + 128:base + 256] = jnp.where(lo, t1, 0.0).astype(BF16)
            o_ref[:, base + 256:base + 384] = pltpu.roll(jnp.where(lo, t2, t1), 64, 1).astype(BF16)
            o_ref[:, base + 384:base + 512] = jnp.where(lo, pltpu.roll(t2, 64, 1), 0.0).astype(BF16)

    return pl.pallas_call(
        body,
        out_shape=jax.ShapeDtypeStruct((RQ, H * HW), BF16),
        compiler_params=_cparams(None, 32 << 20),
        name="assemble_w_q",
    )(g)


def _split_dw_q(dw):
    def body(dw_ref, o_ref):
        lane = lax.broadcasted_iota(jnp.int32, (RQ, 128), 1)
        lo = lane < 64
        for k in range(NDEV):
            base = 2 * k * HW
            a = dw_ref[:, base:base + 128]
            b = dw_ref[:, base + 128:base + 256]
            c = pltpu.roll(dw_ref[:, base + 256:base + 384], 64, 1)
            d = pltpu.roll(dw_ref[:, base + 384:base + 512], 64, 1)
            o_ref[k, :, 0:128] = a
            o_ref[k, :, 128:256] = jnp.where(lo, b, c)
            o_ref[k, :, 256:384] = jnp.where(lo, c, d)

    return pl.pallas_call(
        body,
        out_shape=jax.ShapeDtypeStruct((NDEV, RQ, 384), F32),
        compiler_params=_cparams(None, 32 << 20),
        name="split_dw_q",
    )(dw)


def _concat_cols(g, dtype, name):
    n, R, C = g.shape

    def body(g_ref, o_ref):
        for k in range(n):
            o_ref[:, k * C:(k + 1) * C] = g_ref[k].astype(dtype)

    return pl.pallas_call(body, out_shape=jax.ShapeDtypeStruct((R, n * C), dtype), compiler_params=_cparams(None, 32 << 20), name=name)(g)


def _split_cols(x, n, name):
    R, NC = x.shape
    C = NC // n

    def body(x_ref, o_ref):
        for k in range(n):
            o_ref[k] = x_ref[:, k * C:(k + 1) * C]

    return pl.pallas_call(body, out_shape=jax.ShapeDtypeStruct((n, R, C), x.dtype), compiler_params=_cparams(None, 32 << 20), name=name)(x)


def _rms_scale(xf):
    return lax.rsqrt(jnp.mean(xf * xf, axis=-1, keepdims=True) + RMS_EPS)


def _prenorm(x, g, ts):
    S = x.shape[0]

    def body(x_ref, g_ref, h_ref, ht_ref):
        xf = x_ref[...]
        h = xf * _rms_scale(xf) * g_ref[...]
        h_ref[...] = h.astype(BF16)
        ht_ref[...] = jnp.transpose(h).astype(BF16)

    return pl.pallas_call(
        body,
        grid=(S // ts,),
        in_specs=[pl.BlockSpec((ts, D), lambda i: (i, 0)), pl.BlockSpec((1, D), lambda i: (0, 0))],
        out_specs=[pl.BlockSpec((ts, D), lambda i: (i, 0)), pl.BlockSpec((D, ts), lambda i: (0, i))],
        out_shape=[jax.ShapeDtypeStruct((S, D), BF16), jax.ShapeDtypeStruct((D, S), BF16)],
        compiler_params=_cparams(("parallel",), 32 << 20),
        name="prenorm",
    )(x, g)


def _inv_freq_tile():
    inv_freq = ROPE_THETA ** (-jnp.arange(0, DR, 2, dtype=F32) / DR)
    return jnp.tile(inv_freq, 4).reshape(1, 128)


def _rope_tables(pos_ref, freq_ref, ts):
    lane = lax.broadcasted_iota(jnp.int32, (ts, 128), 1)
    ang = pos_ref[...].astype(F32) * freq_ref[...]
    return jnp.cos(ang), jnp.sin(ang), lane


def _rope_swap(t, lane):
    return jnp.where(lane < 32, pltpu.roll(t, 96, 1), pltpu.roll(t, 32, 1))


def _qkv_prep(proj, pos, freq, g_qa, g_kva, w_q, w_kv, ts):
    S = proj.shape[0]

    def body(qa_ref, ckv_ref, kr_ref, pos_ref, freq_ref, gq_ref, gk_ref, wq_ref, wkv_ref, q_ref, k_ref, v_ref, vt_ref, qn_ref, kvn_ref):
        qa = _f32(qa_ref)
        qn = (qa * _rms_scale(qa) * gq_ref[...]).astype(BF16)
        ckv = _f32(ckv_ref)
        kvn = (ckv * _rms_scale(ckv) * gk_ref[...]).astype(BF16)
        qn_ref[...] = qn
        kvn_ref[...] = kvn
        cos, sin, lane = _rope_tables(pos_ref, freq_ref, ts)
        sgn_sin = jnp.where(lane < 32, -sin, sin)
        live = lane < DR
        kr = _f32(kr_ref)
        kr = jnp.where(live, kr * cos + _rope_swap(kr, lane) * sgn_sin, 0.0).astype(BF16)
        qf = jnp.dot(qn, wq_ref[...], preferred_element_type=F32)
        kvf = jnp.dot(kvn, wkv_ref[...], preferred_element_type=F32)
        for h in range(H):
            q_ref[h, :, 0:DN] = (qf[:, h * HW:h * HW + DN] * QSCALE).astype(BF16)
            t = qf[:, h * HW + DN:(h + 1) * HW]
            q_ref[h, :, DN:HW] = jnp.where(live, (t * cos + _rope_swap(t, lane) * sgn_sin) * QSCALE, 0.0).astype(BF16)
            k_ref[h, :, 0:DN] = kvf[:, h * HW:h * HW + DN].astype(BF16)
            k_ref[h, :, DN:HW] = kr
            vh = kvf[:, h * HW + DN:(h + 1) * HW]
            v_ref[h] = vh.astype(BF16)
            vt_ref[h] = jnp.transpose(vh).astype(BF16)

    row = lambda w, blk: pl.BlockSpec((ts, w), lambda i: (i, blk))
    full = lambda a: pl.BlockSpec(a.shape, lambda i: (0,) * a.ndim)
    return pl.pallas_call(
        body,
        grid=(S // ts,),
        in_specs=[row(RQ, OFF_QA // RQ), row(RKV, OFF_CKV // RKV), row(128, OFF_KR // 128),
                  pl.BlockSpec((ts, 1), lambda i: (i, 0)), full(freq), full(g_qa), full(g_kva), full(w_q), full(w_kv)],
        out_specs=[pl.BlockSpec((H, ts, HW), lambda i: (0, i, 0)), pl.BlockSpec((H, ts, HW), lambda i: (0, i, 0)),
                   pl.BlockSpec((H, ts, DV), lambda i: (0, i, 0)), pl.BlockSpec((H, DV, ts), lambda i: (0, 0, i)),
                   pl.BlockSpec((ts, RQ), lambda i: (i, 0)), pl.BlockSpec((ts, RKV), lambda i: (i, 0))],
        out_shape=[jax.ShapeDtypeStruct((H, S, HW), BF16), jax.ShapeDtypeStruct((H, S, HW), BF16),
                   jax.ShapeDtypeStruct((H, S, DV), BF16), jax.ShapeDtypeStruct((H, DV, S), BF16),
                   jax.ShapeDtypeStruct((S, RQ), BF16), jax.ShapeDtypeStruct((S, RKV), BF16)],
        compiler_params=_cparams(("parallel",), 48 << 20),
        name="qkv_prep",
    )(proj, proj, proj, pos, freq, g_qa, g_kva, w_q, w_kv)


def _col_to_row8(col, n):
    return jnp.transpose(jnp.broadcast_to(col, (n, 128)))[0:8, :]


def _causal_pairs(n, key_major):
    if key_major:
        pairs = [(i, j) for j in range(n) for i in range(j, n)]
    else:
        pairs = [(i, j) for i in range(n) for j in range(i + 1)]
    return jnp.array([p[0] for p in pairs], jnp.int32), jnp.array([p[1] for p in pairs], jnp.int32)


def _flash_fwd(q, k, vt, tq, hb=16):
    S = q.shape[1]
    nq = S // tq
    tsub = tq
    nsub = tq // tsub

    qi_tab, kj_tab = _causal_pairs(nq, key_major=False)

    def body(qi_ref, kj_ref, q_ref, k_ref, vt_ref, o_ref, lse_ref, m_sc, l_sc, acc_sc):
        p = pl.program_id(1)
        qi, kj = qi_ref[p], kj_ref[p]

        @pl.when(kj == 0)
        def _():
            m_sc[...] = jnp.full_like(m_sc, -jnp.inf)
            l_sc[...] = jnp.zeros_like(l_sc)
            acc_sc[...] = jnp.zeros_like(acc_sc)

        def step(diag):
            chains = [(h, u) for h in range(hb) for u in range(nsub)]

            def scores(h, u):
                return lax.dot_general(k_ref[h], q_ref[h, u * tsub:(u + 1) * tsub, :], NT, preferred_element_type=F32)

            st_next = scores(*chains[0])
            for ci, (h, u) in enumerate(chains):
                st = st_next
                if ci + 1 < len(chains):
                    st_next = scores(*chains[ci + 1])
                cols = slice(u * tsub, (u + 1) * tsub)
                if diag:
                    r = lax.broadcasted_iota(jnp.int32, (tq, tsub), 0)
                    c = lax.broadcasted_iota(jnp.int32, (tq, tsub), 1) + u * tsub
                    st = jnp.where(r <= c, st, -jnp.inf)
                m_prev = m_sc[h, 0:1, cols]
                m_new = jnp.maximum(m_prev, jnp.max(st, axis=0, keepdims=True))
                alpha = jnp.exp2(m_prev - m_new)
                pt = jnp.exp2(st - m_new)
                l_sc[h, :, cols] = jnp.broadcast_to(alpha * l_sc[h, 0:1, cols] + jnp.sum(pt, axis=0, keepdims=True), (8, tsub))
                m_sc[h, :, cols] = jnp.broadcast_to(m_new, (8, tsub))
                acc_sc[h, :, cols] = alpha * acc_sc[h, :, cols] + jnp.dot(vt_ref[h], pt.astype(BF16), preferred_element_type=F32)

        @pl.when(kj < qi)
        def _():
            step(False)

        @pl.when(kj == qi)
        def _():
            step(True)
            for h in range(hb):
                l = l_sc[h, 0:1, :]
                o_ref[:, h * DV:(h + 1) * DV] = jnp.transpose(acc_sc[h] / l).astype(BF16)
                lse_ref[h] = m_sc[h] + jnp.log2(l_sc[h])

    return pl.pallas_call(
        body,
        grid_spec=pltpu.PrefetchScalarGridSpec(
            num_scalar_prefetch=2,
            grid=(H // hb, len(qi_tab)),
            in_specs=[pl.BlockSpec((hb, tq, HW), lambda h, p, qi, kj: (h, qi[p], 0)),
                      pl.BlockSpec((hb, tq, HW), lambda h, p, qi, kj: (h, kj[p], 0)),
                      pl.BlockSpec((hb, DV, tq), lambda h, p, qi, kj: (h, 0, kj[p]))],
            out_specs=[pl.BlockSpec((tq, hb * DV), lambda h, p, qi, kj: (qi[p], h)),
                       pl.BlockSpec((hb, 8, tq), lambda h, p, qi, kj: (h, 0, qi[p]))],
            scratch_shapes=[pltpu.VMEM((hb, 8, tq), F32), pltpu.VMEM((hb, 8, tq), F32), pltpu.VMEM((hb, DV, tq), F32)],
        ),
        out_shape=[jax.ShapeDtypeStruct((S, H * DV), BF16), jax.ShapeDtypeStruct((H, 8, S), F32)],
        compiler_params=_cparams(("parallel", "arbitrary"), 40 << 20),
        name="flash_fwd",
    )(qi_tab, kj_tab, q, k, vt)


def _sigmoid(x):
    return 1.0 / (1.0 + jnp.exp(-x))


HALO = 16


def _f32(ref):
    return ref[...].astype(F32)


def _shift_rows(u, prev, n, first):
    ts = u.shape[0]
    row = lax.broadcasted_iota(jnp.int32, u.shape, 0)
    out = pltpu.roll(u, n, 0)
    for j in range(n):
        halo = jnp.where(first, 0.0, prev[HALO - n + j:HALO - n + j + 1, :])
        out = jnp.where(row == j, halo, out)
    return out


def _gates_fwd(proj, attn, conv_w, ts):
    S = proj.shape[0]

    def body(attn_ref, zm_ref, cin_ref, bg_ref, cg_ref, zc_ref, cin_p, cg_p, w_ref, am_ref, ac_ref):
        first = pl.program_id(0) == 0
        zm = _f32(zm_ref)
        am_ref[...] = (_f32(attn_ref) * (zm * _sigmoid(zm))).astype(BF16)
        u = _f32(cg_ref) * _f32(cin_ref)
        up = _f32(cg_p) * _f32(cin_p)
        w = w_ref[...]
        conv = w[0:1, :] * _shift_rows(u, up, 2, first) + w[1:2, :] * _shift_rows(u, up, 1, first) + w[2:3, :] * u
        zc = _f32(zc_ref)
        ac_ref[...] = (_f32(bg_ref) * conv * (zc * _sigmoid(zc))).astype(BF16)

    seg = lambda name: pl.BlockSpec((ts, D), lambda i: (i, SEG[name]))
    prev = lambda name: pl.BlockSpec((HALO, D), lambda i: (jnp.maximum(i * (ts // HALO) - 1, 0), SEG[name]))
    return pl.pallas_call(
        body,
        grid=(S // ts,),
        in_specs=[pl.BlockSpec((ts, D), lambda i: (i, 0)), seg("z_mla"), seg("c_in"), seg("b_gate"), seg("c_gate"), seg("z_conv"),
                  prev("c_in"), prev("c_gate"), pl.BlockSpec((8, D), lambda i: (0, 0))],
        out_specs=[pl.BlockSpec((ts, D), lambda i: (i, 0))] * 2,
        out_shape=[jax.ShapeDtypeStruct((S, D), BF16)] * 2,
        compiler_params=_cparams(("arbitrary",), 48 << 20),
        name="gates_fwd",
    )(attn, proj, proj, proj, proj, proj, proj, proj, conv_w)


def _merge_fwd(proj, y_mla, y_conv, ts):
    S = proj.shape[0]

    def body(gm_ref, gc_ref, ym_ref, yc_ref, o_ref):
        o_ref[...] = (_sigmoid(_f32(gm_ref)) * _f32(ym_ref) + _sigmoid(_f32(gc_ref)) * _f32(yc_ref)).astype(BF16)

    seg = lambda name: pl.BlockSpec((ts, D), lambda i: (i, SEG[name]))
    row = pl.BlockSpec((ts, D), lambda i: (i, 0))
    return pl.pallas_call(
        body, grid=(S // ts,), in_specs=[seg("g_mla"), seg("g_conv"), row, row], out_specs=row,
        out_shape=jax.ShapeDtypeStruct((S, D), BF16), compiler_params=_cparams(("parallel",), 32 << 20), name="merge_fwd",
    )(proj, proj, y_mla, y_conv)


def _post_loss(out, x, target, g_post, ts):
    S = out.shape[0]

    def body(o_ref, x_ref, t_ref, g_ref, dy_ref, do_ref, dg_ref, loss_ref):
        i = pl.program_id(0)
        o = o_ref[...]
        r = _rms_scale(o)
        n = o * r
        g = g_ref[...]
        err = x_ref[...] + n * g - t_ref[...]
        dy = err * (1.0 / D)
        dy_ref[...] = dy
        dn = dy * g
        do_ref[...] = (r * (dn - n * jnp.mean(dn * n, axis=-1, keepdims=True))).astype(BF16)
        dg = jnp.sum(dy * n, axis=0, keepdims=True)
        part = jnp.sum(jnp.sum(err * err, axis=0, keepdims=True), axis=1, keepdims=True) * (0.5 / D)

        @pl.when(i == 0)
        def _():
            dg_ref[...] = jnp.zeros_like(dg_ref)
            loss_ref[...] = jnp.zeros_like(loss_ref)

        dg_ref[0:1, :] += dg
        loss_ref[...] += jnp.broadcast_to(part, loss_ref.shape)

    row = pl.BlockSpec((ts, D), lambda i: (i, 0))
    return pl.pallas_call(
        body, grid=(S // ts,),
        in_specs=[row, row, row, pl.BlockSpec((1, D), lambda i: (0, 0))],
        out_specs=[row, row, pl.BlockSpec((8, D), lambda i: (0, 0)), pl.BlockSpec((8, 128), lambda i: (0, 0))],
        out_shape=[jax.ShapeDtypeStruct((S, D), F32), jax.ShapeDtypeStruct((S, D), BF16),
                   jax.ShapeDtypeStruct((8, D), F32), jax.ShapeDtypeStruct((8, 128), F32)],
        compiler_params=_cparams(("arbitrary",), 40 << 20), name="post_loss",
    )(out, x, target, g_post)


def _merge_bwd(proj, y_mla, y_conv, dmerged, ts):
    S = proj.shape[0]

    def body(gm_ref, gc_ref, ym_ref, yc_ref, dm_ref, dym_ref, dyc_ref, dgm_ref, dgc_ref):
        dm = _f32(dm_ref)
        sm = _sigmoid(_f32(gm_ref))
        sc = _sigmoid(_f32(gc_ref))
        dym_ref[...] = (dm * sm).astype(BF16)
        dyc_ref[...] = (dm * sc).astype(BF16)
        dgm_ref[...] = (dm * _f32(ym_ref) * (sm * (1.0 - sm))).astype(BF16)
        dgc_ref[...] = (dm * _f32(yc_ref) * (sc * (1.0 - sc))).astype(BF16)

    seg = lambda name: pl.BlockSpec((ts, D), lambda i: (i, SEG[name]))
    row = pl.BlockSpec((ts, D), lambda i: (i, 0))
    return pl.pallas_call(
        body, grid=(S // ts,), in_specs=[seg("g_mla"), seg("g_conv"), row, row, row], out_specs=[row] * 4,
        out_shape=[jax.ShapeDtypeStruct((S, D), BF16)] * 4, compiler_params=_cparams(("parallel",), 40 << 20), name="merge_bwd",
    )(proj, proj, y_mla, y_conv, dmerged)


def _gates_bwd(proj, attn, da_mla, da_conv, conv_w, ts):
    S = proj.shape[0]
    nblk = S // ts

    def body(attn_ref, zm_ref, cin_ref, bg_ref, cg_ref, zc_ref, dam_ref, dac_ref, cin_p, cg_p, bg_n, zc_n, dac_n, w_ref,
             dattn_ref, delta_ref, dzm_ref, dcin_ref, dbg_ref, dcg_ref, dzc_ref, dw_ref):
        i = pl.program_id(0)
        first = i == 0
        last = i == nblk - 1

        @pl.when(first)
        def _():
            dw_ref[...] = jnp.zeros_like(dw_ref)

        row = lax.broadcasted_iota(jnp.int32, (ts, DV), 0)
        for c in range(D // DV):
            cols = slice(c * DV, (c + 1) * DV)
            ld = lambda ref: ref[:, cols].astype(F32)
            zm = ld(zm_ref)
            sg = _sigmoid(zm)
            attn = ld(attn_ref)
            dam = ld(dam_ref)
            dattn = dam * (zm * sg)
            dattn_ref[:, cols] = dattn.astype(BF16)
            dzm_ref[:, cols] = (dam * attn * (sg * (1.0 + zm * (1.0 - sg)))).astype(BF16)
            delta_ref[c] = _col_to_row8(jnp.sum(dattn * attn, axis=1, keepdims=True), ts)
            w = w_ref[:, cols]
            cin, cg, bg, zc = ld(cin_ref), ld(cg_ref), ld(bg_ref), ld(zc_ref)
            u = cg * cin
            up = ld(cg_p) * ld(cin_p)
            u1 = _shift_rows(u, up, 1, first)
            u2 = _shift_rows(u, up, 2, first)
            conv = w[0:1, :] * u2 + w[1:2, :] * u1 + w[2:3, :] * u
            sgc = _sigmoid(zc)
            siluc = zc * sgc
            dac = ld(dac_ref)
            dbg_ref[:, cols] = (dac * conv * siluc).astype(BF16)
            dzc_ref[:, cols] = (dac * bg * conv * (sgc * (1.0 + zc * (1.0 - sgc)))).astype(BF16)
            dconv = dac * bg * siluc
            zn = ld(zc_n)
            dconv_n = jnp.where(last, 0.0, ld(dac_n) * ld(bg_n) * (zn * _sigmoid(zn)))
            d1 = jnp.where(row == ts - 1, dconv_n[0:1, :], pltpu.roll(dconv, ts - 1, 0))
            d2 = jnp.where(row == ts - 1, dconv_n[1:2, :], jnp.where(row == ts - 2, dconv_n[0:1, :], pltpu.roll(dconv, ts - 2, 0)))
            du = w[2:3, :] * dconv + w[1:2, :] * d1 + w[0:1, :] * d2
            dcg_ref[:, cols] = (du * cin).astype(BF16)
            dcin_ref[:, cols] = (du * cg).astype(BF16)
            dw_ref[0:1, cols] += jnp.sum(dconv * u2, axis=0, keepdims=True)
            dw_ref[1:2, cols] += jnp.sum(dconv * u1, axis=0, keepdims=True)
            dw_ref[2:3, cols] += jnp.sum(dconv * u, axis=0, keepdims=True)

    seg = lambda name: pl.BlockSpec((ts, D), lambda i: (i, SEG[name]))
    prev = lambda name: pl.BlockSpec((HALO, D), lambda i: (jnp.maximum(i * (ts // HALO) - 1, 0), SEG[name]))
    nxt = lambda blk: pl.BlockSpec((HALO, D), lambda i: (jnp.minimum((i + 1) * (ts // HALO), S // HALO - 1), blk))
    row = pl.BlockSpec((ts, D), lambda i: (i, 0))
    return pl.pallas_call(
        body, grid=(nblk,),
        in_specs=[row, seg("z_mla"), seg("c_in"), seg("b_gate"), seg("c_gate"), seg("z_conv"), row, row,
                  prev("c_in"), prev("c_gate"), nxt(SEG["b_gate"]), nxt(SEG["z_conv"]), nxt(0), pl.BlockSpec((8, D), lambda i: (0, 0))],
        out_specs=[row, pl.BlockSpec((H, 8, ts), lambda i: (0, 0, i)), row, row, row, row, row, pl.BlockSpec((8, D), lambda i: (0, 0))],
        out_shape=[jax.ShapeDtypeStruct((S, D), BF16), jax.ShapeDtypeStruct((H, 8, S), F32)] + [jax.ShapeDtypeStruct((S, D), BF16)] * 5
        + [jax.ShapeDtypeStruct((8, D), F32)],
        compiler_params=_cparams(("arbitrary",), 56 << 20), name="gates_bwd",
    )(attn, proj, proj, proj, proj, proj, da_mla, da_conv, proj, proj, proj, proj, da_conv, conv_w)


def _flash_bwd(q, k, v, do, lse, delta, tq, hb=2):
    S = q.shape[1]
    nq = S // tq
    scale = SM_SCALE
    tsub = min(256, tq)
    nsub = tq // tsub

    qi_tab, kj_tab = _causal_pairs(nq, key_major=True)
    npairs = nq * (nq + 1) // 2

    def body(qi_ref, kj_ref, q_ref, k_ref, v_ref, do_ref, lse_ref, dl_ref, dq_ref, dk_ref, dv_ref, dk_sc, dv_sc, dq_sc):
        p = pl.program_id(1)
        qi, kj = qi_ref[p], kj_ref[p]

        @pl.when(p == 0)
        def _():
            dq_sc[...] = jnp.zeros_like(dq_sc)

        @pl.when(qi == kj)
        def _():
            dk_sc[...] = jnp.zeros_like(dk_sc)
            dv_sc[...] = jnp.zeros_like(dv_sc)

        def step(diag):
            chains = [(h, u) for h in range(hb) for u in range(nsub)]

            def first_matmuls(h, u):
                sub = slice(u * tsub, (u + 1) * tsub)
                st = lax.dot_general(k_ref[h], q_ref[h, sub, :], NT, preferred_element_type=F32)
                dpt = lax.dot_general(v_ref[h], do_ref[sub, h * DV:(h + 1) * DV], NT, preferred_element_type=F32)
                return st, dpt

            nxt = first_matmuls(*chains[0])
            for ci, (h, u) in enumerate(chains):
                st, dpt = nxt
                if ci + 1 < len(chains):
                    nxt = first_matmuls(*chains[ci + 1])
                sub = slice(u * tsub, (u + 1) * tsub)
                pt = jnp.exp2(st - lse_ref[h, 0:1, sub])
                if diag:
                    r = lax.broadcasted_iota(jnp.int32, (tq, tsub), 0)
                    c = lax.broadcasted_iota(jnp.int32, (tq, tsub), 1) + u * tsub
                    pt = jnp.where(r <= c, pt, 0.0)
                dst = (pt * (dpt - dl_ref[h, 0:1, sub])).astype(BF16)
                dv_sc[h] += jnp.dot(pt.astype(BF16), do_ref[sub, h * DV:(h + 1) * DV], preferred_element_type=F32)
                dk_sc[h] += jnp.dot(dst, q_ref[h, sub, :], preferred_element_type=F32)
                rows = pl.ds(pl.multiple_of(qi * tq + u * tsub, tsub), tsub)
                dq_sc[h, rows, :] += lax.dot_general(dst, k_ref[h], TN, preferred_element_type=F32)

        @pl.when(qi > kj)
        def _():
            step(False)

        @pl.when(qi == kj)
        def _():
            step(True)

        @pl.when(qi == nq - 1)
        def _():
            dk_ref[...] = (dk_sc[...] * (1.0 / LOG2E)).astype(BF16)
            dv_ref[...] = dv_sc[...].astype(BF16)

        @pl.when(p == npairs - 1)
        def _():
            dq_ref[...] = (dq_sc[...] * scale).astype(BF16)

    return pl.pallas_call(
        body,
        grid_spec=pltpu.PrefetchScalarGridSpec(
            num_scalar_prefetch=2,
            grid=(H // hb, npairs),
            in_specs=[pl.BlockSpec((hb, tq, HW), lambda h, p, qi, kj: (h, qi[p], 0)),
                      pl.BlockSpec((hb, tq, HW), lambda h, p, qi, kj: (h, kj[p], 0)),
                      pl.BlockSpec((hb, tq, DV), lambda h, p, qi, kj: (h, kj[p], 0)),
                      pl.BlockSpec((tq, hb * DV), lambda h, p, qi, kj: (qi[p], h)),
                      pl.BlockSpec((hb, 8, tq), lambda h, p, qi, kj: (h, 0, qi[p])),
                      pl.BlockSpec((hb, 8, tq), lambda h, p, qi, kj: (h, 0, qi[p]))],
            out_specs=[pl.BlockSpec((hb, S, HW), lambda h, p, qi, kj: (h, 0, 0)),
                       pl.BlockSpec((hb, tq, HW), lambda h, p, qi, kj: (h, kj[p], 0)),
                       pl.BlockSpec((hb, tq, DV), lambda h, p, qi, kj: (h, kj[p], 0))],
            scratch_shapes=[pltpu.VMEM((hb, tq, HW), F32), pltpu.VMEM((hb, tq, DV), F32), pltpu.VMEM((hb, S, HW), F32)],
        ),
        out_shape=[jax.ShapeDtypeStruct((H, S, HW), BF16), jax.ShapeDtypeStruct((H, S, HW), BF16), jax.ShapeDtypeStruct((H, S, DV), BF16)],
        compiler_params=_cparams(("parallel", "arbitrary"), VMEM_CAP),
        name="flash_bwd",
    )(qi_tab, kj_tab, q, k, v, do, lse, delta)


def _rms_bwd(xf, g, dn_out):
    r = _rms_scale(xf)
    n = xf * r
    dn = dn_out * g
    dx = r * (dn - n * jnp.mean(dn * n, axis=-1, keepdims=True))
    return dx, jnp.sum(dn_out * n, axis=0, keepdims=True)


def _qkv_bwd(proj, pos, freq, g_qa, g_kva, w_q, w_kv, dq, dk, dv, segments, ts):
    S = proj.shape[0]
    nseg = len(segments)

    def body(qa_ref, ckv_ref, pos_ref, freq_ref, gq_ref, gk_ref, wq_ref, wkv_ref, dq_ref, dk_ref, dv_ref, *rest):
        seg_refs, (dqp_ref, dkvp_ref, dproj_ref, dgq_ref, dgk_ref) = rest[:nseg], rest[nseg:]
        for j, seg_ref in enumerate(seg_refs):
            dproj_ref[:, j * D:(j + 1) * D] = seg_ref[...]
        dqa_ref = dproj_ref.at[:, OFF_QA:OFF_QA + RQ]
        dckv_ref = dproj_ref.at[:, OFF_CKV:OFF_CKV + RKV]
        dkr_ref = dproj_ref.at[:, OFF_KR:OFF_KR + 128]
        i = pl.program_id(0)
        cos, sin, lane = _rope_tables(pos_ref, freq_ref, ts)
        sgn_sin = jnp.where(lane < 32, sin, -sin)
        live = lane < DR
        kr_sum = jnp.zeros((ts, 128), F32)
        for h in range(H):
            dqp_ref[:, h * HW:h * HW + DN] = dq_ref[h, :, 0:DN]
            t = dq_ref[h, :, DN:HW].astype(F32)
            dqp_ref[:, h * HW + DN:(h + 1) * HW] = jnp.where(live, t * cos + _rope_swap(t, lane) * sgn_sin, 0.0).astype(BF16)
            dkvp_ref[:, h * HW:h * HW + DN] = dk_ref[h, :, 0:DN]
            dkvp_ref[:, h * HW + DN:(h + 1) * HW] = dv_ref[h]
            kr_sum = kr_sum + dk_ref[h, :, DN:HW].astype(F32)
        dkr_ref[...] = jnp.where(live, kr_sum * cos + _rope_swap(kr_sum, lane) * sgn_sin, 0.0).astype(BF16)
        dqn = lax.dot_general(dqp_ref[...], wq_ref[...], NT, preferred_element_type=F32)
        dkvn = lax.dot_general(dkvp_ref[...], wkv_ref[...], NT, preferred_element_type=F32)
        dqa, dgq = _rms_bwd(_f32(qa_ref), gq_ref[...], dqn)
        dckv, dgk = _rms_bwd(_f32(ckv_ref), gk_ref[...], dkvn)
        dqa_ref[...] = dqa.astype(BF16)
        dckv_ref[...] = dckv.astype(BF16)

        @pl.when(i == 0)
        def _():
            dgq_ref[...] = jnp.zeros_like(dgq_ref)
            dgk_ref[...] = jnp.zeros_like(dgk_ref)

        dgq_ref[0:1, :] += dgq
        dgk_ref[0:1, :] += dgk

    rowb = lambda w, blk: pl.BlockSpec((ts, w), lambda i: (i, blk))
    full = lambda a: pl.BlockSpec(a.shape, lambda i: (0,) * a.ndim)
    heads = lambda w: pl.BlockSpec((H, ts, w), lambda i: (0, i, 0))
    return pl.pallas_call(
        body, grid=(S // ts,),
        in_specs=[rowb(RQ, OFF_QA // RQ), rowb(RKV, OFF_CKV // RKV), pl.BlockSpec((ts, 1), lambda i: (i, 0)), full(freq), full(g_qa), full(g_kva),
                  full(w_q), full(w_kv), heads(HW), heads(HW), heads(DV)] + [rowb(D, 0)] * nseg,
        out_specs=[rowb(H * HW, 0), rowb(H * HW, 0), rowb(NP, 0),
                   pl.BlockSpec((8, RQ), lambda i: (0, 0)), pl.BlockSpec((8, RKV), lambda i: (0, 0))],
        out_shape=[jax.ShapeDtypeStruct((S, H * HW), BF16), jax.ShapeDtypeStruct((S, H * HW), BF16), jax.ShapeDtypeStruct((S, NP), BF16),
                   jax.ShapeDtypeStruct((8, RQ), F32), jax.ShapeDtypeStruct((8, RKV), F32)],
        compiler_params=_cparams(("arbitrary",), 56 << 20), name="qkv_bwd",
    )(proj, proj, pos, freq, g_qa, g_kva, w_q, w_kv, dq, dk, dv, *segments)


def _prenorm_bwd(x, g, dh, dy, ts):
    S = x.shape[0]

    def body(x_ref, g_ref, dh_ref, dy_ref, gx_ref, dg_ref):
        dx, dg = _rms_bwd(x_ref[...], g_ref[...], dh_ref[...])
        gx_ref[...] = dy_ref[...] + dx

        @pl.when(pl.program_id(0) == 0)
        def _():
            dg_ref[...] = jnp.zeros_like(dg_ref)

        dg_ref[0:1, :] += dg

    row = pl.BlockSpec((ts, D), lambda i: (i, 0))
    return pl.pallas_call(
        body, grid=(S // ts,), in_specs=[row, pl.BlockSpec((1, D), lambda i: (0, 0)), row, row],
        out_specs=[row, pl.BlockSpec((8, D), lambda i: (0, 0))],
        out_shape=[jax.ShapeDtypeStruct((S, D), F32), jax.ShapeDtypeStruct((8, D), F32)],
        compiler_params=_cparams(("arbitrary",), 40 << 20), name="prenorm_bwd",
    )(x, g, dh, dy)


def _local_step(x, pos, target, g_pre, g_qa, g_kva, g_post, w_in, other_weights, reduce_grads=None):
    S = x.shape[0]
    ts = min(256, S)
    tq = min(512, S)
    tm = min(2048, S)
    mm = functools.partial(_matmul, tm=tm)
    freq = _inv_freq_tile()

    h, ht = _prenorm(x, g_pre, ts)
    if len(other_weights) == 2:
        gather, assemble = other_weights
        proj, gathered = _matmul(h, w_in, mode="nn", out_dtype=BF16, tm=2048, tn=1408, tk=D, name="mm_proj", exchange=gather)
        other_weights = assemble(*gathered)
    else:
        proj = _matmul(h, w_in, mode="nn", out_dtype=BF16, tm=2048, tn=1408, tk=D, name="mm_proj")
    w_q, w_kv, conv_w8, w_o_mla, w_o_conv, w_out = other_weights
    q, k, v, vt, qn, kvn = _qkv_prep(proj, pos, freq, g_qa, g_kva, w_q, w_kv, ts)
    attn, lse = _flash_fwd(q, k, vt, tq)
    a_mla, a_conv = _gates_fwd(proj, attn, conv_w8, ts)
    y_mla = mm(a_mla, w_o_mla, mode="nn", out_dtype=BF16, tn=D, tk=D, name="mm_y_mla")
    y_conv = mm(a_conv, w_o_conv, mode="nn", out_dtype=BF16, tn=D, tk=D, name="mm_y_conv")
    merged = _merge_fwd(proj, y_mla, y_conv, ts)
    out = _matmul(merged, w_out, mode="nn", out_dtype=F32, tm=min(1024, S), tn=D, tk=D, name="mm_out")
    dy, dout, dg_post, loss = _post_loss(out, x, target, g_post, ts)

    dmerged = mm(dout, w_out, mode="nt", out_dtype=BF16, tn=D, tk=D, name="mm_dmerged")
    dw_out = _matmul(merged, dout, mode="tn", out_dtype=F32, tm=1024, tn=1024, tk=2048, name="mm_dw_out")
    dy_mla, dy_conv, dg_mla, dg_conv = _merge_bwd(proj, y_mla, y_conv, dmerged, ts)
    da_mla = mm(dy_mla, w_o_mla, mode="nt", out_dtype=BF16, tn=D, tk=D, name="mm_da_mla")
    da_conv = mm(dy_conv, w_o_conv, mode="nt", out_dtype=BF16, tn=D, tk=D, name="mm_da_conv")
    dw_o_mla = _matmul(a_mla, dy_mla, mode="tn", out_dtype=F32, tm=1024, tn=1024, tk=2048, name="mm_dw_o_mla")
    dw_o_conv = _matmul(a_conv, dy_conv, mode="tn", out_dtype=F32, tm=1024, tn=1024, tk=2048, name="mm_dw_o_conv")
    dattn, delta, dz_mla, dc_in, db_gate, dc_gate, dz_conv, dconv_w = _gates_bwd(proj, attn, da_mla, da_conv, conv_w8, min(128, S))
    dq, dk, dv = _flash_bwd(q, k, v, dattn, lse, delta, tq)
    dqp, dkvp, dproj, dg_qa, dg_kva = _qkv_bwd(proj, pos, freq, g_qa, g_kva, w_q, w_kv, dq, dk, dv,
                                               [dz_mla, dc_in, db_gate, dc_gate, dz_conv, dg_mla, dg_conv], min(128, S))
    dw_q = _matmul(qn, dqp, mode="tn", out_dtype=F32, tm=RQ, tn=1024, tk=2048, name="mm_dw_q")
    dw_kv = _matmul(kvn, dkvp, mode="tn", out_dtype=F32, tm=RKV, tn=1024, tk=2048, name="mm_dw_kv")
    dw_in = _matmul(ht, dproj, mode="nn", out_dtype=F32, tm=1024, tn=1408, tk=2048, name="mm_dw_in")
    res = dict(dw_in=dw_in, dw_q=dw_q, dw_kv=dw_kv, dconv_w=dconv_w, dw_o_mla=dw_o_mla, dw_o_conv=dw_o_conv, dw_out=dw_out)
    if reduce_grads is None:
        dh = _matmul(dproj, w_in, mode="nt", out_dtype=F32, tm=1024, tn=D, tk=1408, name="mm_dh")
    else:
        exchange, finish = reduce_grads(res)
        dh, got = _matmul(dproj, w_in, mode="nt", out_dtype=F32, tm=1024, tn=D, tk=1408, name="mm_dh", exchange=exchange)
        res["reduced"] = finish(got)
    grad_x, dg_pre = _prenorm_bwd(x, g_pre, dh, dy, ts)
    res.update(loss=loss, grad_x=grad_x, dg_pre=dg_pre, dg_qa=dg_qa, dg_kva=dg_kva, dg_post=dg_post)
    return res


def _my_id():
    return lax.axis_index("x") * 4 + lax.axis_index("y") * 2 + lax.axis_index("c")


def _place():
    x, y, c = lax.axis_index("x"), lax.axis_index("y"), lax.axis_index("c")
    return (x, y, c), (x, y, 1 - c), [(1 - x, y), (x, 1 - y), (1 - x, 1 - y)]


def _slot(px, py, pc):
    return 4 * px + 2 * py + pc


def _all_gather(arrays, name):
    n = len(arrays)

    def body(*refs):
        ins, outs = refs[:n], refs[n:2 * n]
        send_sems, recv_sems, local_sems = refs[2 * n:]
        me, sib, chips = _place()
        c = me[2]

        def copy(a, k, block, to, src=None):
            rows = outs[a].at[_slot(*block)]
            return pltpu.make_async_remote_copy(src_ref=rows if src is None else src, dst_ref=rows, send_sem=send_sems.at[a, k],
                                                recv_sem=recv_sems.at[a, k], device_id=to, device_id_type=MESH)

        local = [pltpu.make_async_copy(ins[a], outs[a].at[_slot(*me)], local_sems.at[a]) for a in range(n)]
        for cp in local:
            cp.start()
        sends = []
        for a in range(n):
            sends.append(copy(a, 0, me, sib, src=ins[a]))
            sends += [copy(a, 1 + j, me, (*chip, c), src=ins[a]) for j, chip in enumerate(chips)]
        for cp in sends:
            cp.start()
        for a in range(n):
            for j, chip in enumerate(chips):
                copy(a, 1 + j, (*chip, c), me).wait_recv()
                fwd = copy(a, 4 + j, (*chip, c), sib)
                fwd.start()
                sends.append(fwd)
        for a in range(n):
            copy(a, 0, sib, me).wait_recv()
            for j, chip in enumerate(chips):
                copy(a, 4 + j, (*chip, 1 - c), me).wait_recv()
        for cp in sends:
            cp.wait_send()
        for cp in local:
            cp.wait()

    anyspec = pl.BlockSpec(memory_space=pl.ANY)
    return pl.pallas_call(
        body,
        in_specs=[anyspec] * n,
        out_specs=[anyspec] * n,
        out_shape=[jax.ShapeDtypeStruct((NDEV,) + a.shape, a.dtype) for a in arrays],
        scratch_shapes=[pltpu.SemaphoreType.DMA((n, NDEV - 1)), pltpu.SemaphoreType.DMA((n, NDEV - 1)), pltpu.SemaphoreType.DMA((n,))],
        name=name,
    )(*arrays)


def _all_gather_relayed(x, name):
    R, C = x.shape
    half = R // 2

    def body(x_ref, out_ref, send_sems, recv_sems, local_sem):
        (mx, my, c), sib, _ = _place()
        xn, yn, dg = (1 - mx, my, c), (mx, 1 - my, c), (1 - mx, 1 - my, c)

        def rows(block, h):
            return out_ref.at[_slot(*block), pl.ds(h * half, half), :]

        def copy(k, dst, to, src=None):
            return pltpu.make_async_remote_copy(src_ref=dst if src is None else src, dst_ref=dst, send_sem=send_sems.at[k],
                                                recv_sem=recv_sems.at[k], device_id=to, device_id_type=MESH)

        me = (mx, my, c)
        local = pltpu.make_async_copy(x_ref, out_ref.at[_slot(*me)], local_sem)
        local.start()
        sends = [copy(0, out_ref.at[_slot(*me)], sib, src=x_ref)]
        for h in range(2):
            sends.append(copy(1 + h, rows(me, h), xn, src=x_ref.at[pl.ds(h * half, half), :]))
            sends.append(copy(3 + h, rows(me, h), yn, src=x_ref.at[pl.ds(h * half, half), :]))
        for cp in sends:
            cp.start()
        copy(3, rows(yn, 0), me).wait_recv()
        relay_x = copy(5, rows(yn, 0), xn)
        relay_x.start()
        copy(1, rows(xn, 0), me).wait_recv()
        copy(2, rows(xn, 1), me).wait_recv()
        relay_y = copy(6, rows(xn, 1), yn)
        relay_y.start()
        sends += [relay_x, relay_y]

        def to_sibling(k, block):
            fwd = copy(k, out_ref.at[_slot(*block)], sib)
            fwd.start()
            sends.append(fwd)

        to_sibling(7, xn)
        copy(4, rows(yn, 1), me).wait_recv()
        to_sibling(8, yn)
        copy(5, rows(dg, 0), me).wait_recv()
        copy(6, rows(dg, 1), me).wait_recv()
        to_sibling(9, dg)
        copy(0, out_ref.at[_slot(*sib)], me).wait_recv()
        for k, block in ((7, xn), (8, yn), (9, dg)):
            copy(k, out_ref.at[_slot(block[0], block[1], 1 - c)], me).wait_recv()
        for cp in sends:
            cp.wait_send()
        local.wait()

    anyspec = pl.BlockSpec(memory_space=pl.ANY)
    return pl.pallas_call(
        body, in_specs=[anyspec], out_specs=anyspec, out_shape=jax.ShapeDtypeStruct((NDEV, R, C), x.dtype),
        scratch_shapes=[pltpu.SemaphoreType.DMA((10,)), pltpu.SemaphoreType.DMA((10,)), pltpu.SemaphoreType.DMA],
        name=name,
    )(x)


def _slab_exchange(arrays, nslots, pick):
    n = len(arrays)

    def copies(ins, outs, sems):
        send_sems, recv_sems = sems
        return [pltpu.make_async_remote_copy(src_ref=ins[a].at[pick(k)[0]], dst_ref=outs[a].at[k], send_sem=send_sems.at[a, k],
                                             recv_sem=recv_sems.at[a, k], device_id=pick(k)[1], device_id_type=MESH)
                for a in range(n) for k in range(nslots)]

    def start(ins, outs, sems):
        for cp in copies(ins, outs, sems):
            cp.start()

    def finish(ins, outs, sems):
        cps = copies(ins, outs, sems)
        for cp in cps:
            cp.wait_recv()
        for cp in cps:
            cp.wait_send()

    return _Exchange(arrays, [jax.ShapeDtypeStruct((nslots,) + a.shape[1:], a.dtype) for a in arrays],
                     [pltpu.SemaphoreType.DMA((n, nslots)), pltpu.SemaphoreType.DMA((n, nslots))], start, finish)


def _direct_gather(arrays):
    n = len(arrays)

    def peer(d):
        p = (_my_id() + d) % NDEV
        return (p // 4, (p // 2) % 2, p % 2), p

    def copies(ins, outs, sems, receiving):
        send_sems, recv_sems, _ = sems
        slot = lambda d: peer(NDEV - d)[1] if receiving else _my_id()
        return [pltpu.make_async_remote_copy(src_ref=ins[a], dst_ref=outs[a].at[slot(d)], send_sem=send_sems.at[a, d - 1],
                                             recv_sem=recv_sems.at[a, d - 1], device_id=peer(d)[0], device_id_type=MESH)
                for d in range(1, NDEV) for a in range(n)]

    def local(ins, outs, sems):
        return [pltpu.make_async_copy(ins[a], outs[a].at[_my_id()], sems[2].at[a]) for a in range(n)]

    def start(ins, outs, sems):
        for cp in local(ins, outs, sems) + copies(ins, outs, sems, False):
            cp.start()

    def finish(ins, outs, sems):
        for cp in copies(ins, outs, sems, True):
            cp.wait_recv()
        for cp in copies(ins, outs, sems, False):
            cp.wait_send()
        for cp in local(ins, outs, sems):
            cp.wait()

    return _Exchange(arrays, [jax.ShapeDtypeStruct((NDEV,) + a.shape, a.dtype) for a in arrays],
                     [pltpu.SemaphoreType.DMA((n, NDEV - 1)), pltpu.SemaphoreType.DMA((n, NDEV - 1)), pltpu.SemaphoreType.DMA((n,))],
                     start, finish)


def _run_exchange(exchange, name):
    n = len(exchange.arrays)

    def body(*refs):
        ins, outs, sems = refs[:n], refs[n:2 * n], refs[2 * n:]
        exchange.start(ins, outs, sems)
        exchange.finish(ins, outs, sems)

    anyspec = pl.BlockSpec(memory_space=pl.ANY)
    return pl.pallas_call(body, in_specs=[anyspec] * n, out_specs=[anyspec] * n, out_shape=exchange.out_shapes,
                          scratch_shapes=exchange.sem_shapes, name=name)(*exchange.arrays)


def _slabs_bf16(x, rows_per_block, name):
    n, R, C = x.shape
    tr = min(rows_per_block, R)

    def body(x_ref, o_ref):
        o_ref[...] = x_ref[...].astype(BF16)

    blk = pl.BlockSpec((1, tr, C), lambda k, i: (k, i, 0))
    return pl.pallas_call(body, grid=(n, R // tr), in_specs=[blk], out_specs=blk, out_shape=jax.ShapeDtypeStruct(x.shape, BF16),
                          compiler_params=_cparams(("parallel", "parallel"), 32 << 20), name=name)(x)


def _to_sibling(k):
    me, sib, chips = _place()
    dest = sib if k == 0 else (*chips[k - 1], sib[2])
    return _slot(*dest), sib


def _to_chips(k):
    me, sib, chips = _place()
    return k, (*chips[k], me[2])


def _chip_sums(own, got, slots, rows_per_block, name):
    _, R, C = own.shape
    tr = min(rows_per_block, R)

    def body(slots_ref, own_ref, got_ref, o_ref):
        o_ref[0] = (own_ref[0].astype(F32) + got_ref[0].astype(F32)).astype(BF16)

    return pl.pallas_call(
        body,
        grid_spec=pltpu.PrefetchScalarGridSpec(
            num_scalar_prefetch=1, grid=(3, R // tr),
            in_specs=[pl.BlockSpec((1, tr, C), lambda j, i, s: (s[1 + j], i, 0)), pl.BlockSpec((1, tr, C), lambda j, i, s: (1 + j, i, 0))],
            out_specs=pl.BlockSpec((1, tr, C), lambda j, i, s: (j, i, 0))),
        out_shape=jax.ShapeDtypeStruct((3, R, C), BF16),
        compiler_params=_cparams(("parallel", "parallel"), 32 << 20), name=name,
    )(slots, own, got)


def _adamw_math(w, g, m, v):
    m = ADAM_B1 * m + (1.0 - ADAM_B1) * g
    v = ADAM_B2 * v + (1.0 - ADAM_B2) * (g * g)
    m_hat = m / (1.0 - ADAM_B1 ** ADAM_STEP)
    v_hat = v / (1.0 - ADAM_B2 ** ADAM_STEP)
    delta = -ADAM_LR * (m_hat / (jnp.sqrt(v_hat) + ADAM_EPS) + ADAM_WD * w)
    return delta, m, v


def _reduce_adamw(own, got1, got2, slots, w, m, v, rows_per_block, name):
    _, R, C = own.shape
    tr = min(rows_per_block, R)
    assert R % tr == 0

    def body(slots_ref, own_ref, g1_ref, g2_ref, w_ref, m_ref, v_ref, g_ref, d_ref, nm_ref, nv_ref):
        g = own_ref[0] + g1_ref[0].astype(F32)
        for j in range(3):
            g = g + g2_ref[j].astype(F32)
        g_ref[...] = g
        d, nm, nv = _adamw_math(w_ref[...], g, m_ref[...], v_ref[...])
        d_ref[...] = d
        nm_ref[...] = nm
        nv_ref[...] = nv

    blk = pl.BlockSpec((tr, C), lambda i, s: (i, 0))
    return pl.pallas_call(
        body,
        grid_spec=pltpu.PrefetchScalarGridSpec(
            num_scalar_prefetch=1, grid=(R // tr,),
            in_specs=[pl.BlockSpec((1, tr, C), lambda i, s: (s[0], i, 0)), pl.BlockSpec((1, tr, C), lambda i, s: (0, i, 0)),
                      pl.BlockSpec((3, tr, C), lambda i, s: (0, i, 0)), blk, blk, blk],
            out_specs=[blk] * 4),
        out_shape=[jax.ShapeDtypeStruct((R, C), F32)] * 4,
        compiler_params=_cparams(("parallel",), 48 << 20), name=name,
    )(slots, own, got1, got2, w, m, v)


def _reduce_adamw_t(own, got1, got2, slots, wt, mt, vt, rows_per_block, name):
    R, C = own.shape
    tr = min(rows_per_block, R)
    assert R % tr == 0

    def body(slots_ref, own_ref, g1_ref, g2_ref, w_ref, m_ref, v_ref, g_ref, d_ref, nm_ref, nv_ref):
        g = own_ref[...] + g1_ref[0].astype(F32)
        for j in range(3):
            g = g + g2_ref[j].astype(F32)
        g = jnp.transpose(g)
        g_ref[...] = g
        d, nm, nv = _adamw_math(w_ref[...], g, m_ref[...], v_ref[...])
        d_ref[...] = d
        nm_ref[...] = nm
        nv_ref[...] = nv

    blk = pl.BlockSpec((C, tr), lambda i, s: (0, i))
    return pl.pallas_call(
        body,
        grid_spec=pltpu.PrefetchScalarGridSpec(
            num_scalar_prefetch=1, grid=(R // tr,),
            in_specs=[pl.BlockSpec((tr, C), lambda i, s: (i, 0)), pl.BlockSpec((1, tr, C), lambda i, s: (0, i, 0)),
                      pl.BlockSpec((3, tr, C), lambda i, s: (0, i, 0)), blk, blk, blk],
            out_specs=[blk] * 4),
        out_shape=[jax.ShapeDtypeStruct((C, R), F32)] * 4,
        compiler_params=_cparams(("parallel",), 48 << 20), name=name,
    )(slots, own, got1, got2, wt, mt, vt)


def _sum_adamw(parts, w, m, v, rows_per_block, name):
    n, R, C = parts.shape
    tr = min(rows_per_block, R)
    assert R % tr == 0

    def body(p_ref, w_ref, m_ref, v_ref, g_ref, d_ref, nm_ref, nv_ref):
        g = p_ref[0]
        for j in range(1, n):
            g = g + p_ref[j]
        g_ref[...] = g
        d, nm, nv = _adamw_math(w_ref[...], g, m_ref[...], v_ref[...])
        d_ref[...] = d
        nm_ref[...] = nm
        nv_ref[...] = nv

    blk = pl.BlockSpec((tr, C), lambda i: (i, 0))
    return pl.pallas_call(
        body, grid=(R // tr,),
        in_specs=[pl.BlockSpec((n, tr, C), lambda i: (0, i, 0)), blk, blk, blk],
        out_specs=[blk] * 4,
        out_shape=[jax.ShapeDtypeStruct((R, C), F32)] * 4,
        compiler_params=_cparams(("parallel",), 48 << 20), name=name,
    )(parts, w, m, v)


def kernel(x, positions, pre_norm_g, w_in, q_a_norm_g, w_q_b, kv_a_norm_g, w_kv_b, conv_w, w_o_mla, w_o_conv, w_out, post_norm_g, loss_target, m_pre_norm_g, m_w_in, m_q_a_norm_g, m_w_q_b, m_kv_a_norm_g, m_w_kv_b, m_conv_w, m_w_o_mla, m_w_o_conv, m_w_out, m_post_norm_g, v_pre_norm_g, v_w_in, v_q_a_norm_g, v_w_q_b, v_kv_a_norm_g, v_w_kv_b, v_conv_w, v_w_o_mla, v_w_o_conv, v_w_out, v_post_norm_g):
    S = x.shape[1]
    conv_pad = jnp.zeros((8, 256), F32).at[0:3, :].set(conv_w)
    g_in = _all_gather_relayed(w_in.astype(BF16), "all_gather_w_in")
    w_in_f = _assemble_w_in(g_in)
    gather_rest = _direct_gather([w_q_b.astype(BF16), w_kv_b.astype(BF16), conv_pad, w_o_mla.astype(BF16), w_o_conv.astype(BF16),
                                  w_out.astype(BF16)])

    def assemble_rest(g_q, g_kv, g_cw, g_om, g_oc, g_out):
        return (_assemble_w_q(g_q), _concat_cols(g_kv, BF16, "assemble_w_kv"), _concat_cols(g_cw, F32, "assemble_conv_w"),
                g_om.reshape(D, D), g_oc.reshape(D, D), g_out.reshape(D, D))

    gnames = ["w_in", "w_q", "w_kv", "conv_w", "w_o_mla", "w_o_conv", "w_out"]
    (mx, my, mc), _, chips = _place()
    slots = jnp.stack([_slot(mx, my, mc)] + [_slot(cx, cy, mc) for cx, cy in chips]).astype(jnp.int32)

    def reduce_grads(r):
        own_in, slabs_in = _split_dw_in(r["dw_in"], slots)
        own = [own_in, _split_dw_q(r["dw_q"]), _split_cols(r["dw_kv"], NDEV, "split_dw_kv"),
               _split_cols(r["dconv_w"], NDEV, "split_dconv_w"), r["dw_o_mla"].reshape(NDEV, D // NDEV, D),
               r["dw_o_conv"].reshape(NDEV, D // NDEV, D), r["dw_out"].reshape(NDEV, D // NDEV, D)]
        slabs = [slabs_in] + [_slabs_bf16(g, 128, "bf16_" + nm) for g, nm in zip(own[1:], gnames[1:])]
        to_sibling = _slab_exchange(slabs, 4, _to_sibling)
        got1 = _run_exchange(to_sibling, "grads_to_sibling")
        sums = [_chip_sums(o, g1, slots, 128, "chip_sum_" + nm) for o, g1, nm in zip([slabs_in] + own[1:], got1, gnames)]
        return _slab_exchange(sums, 3, _to_chips), lambda got2: (own, got1, got2)

    row2 = lambda a: a.reshape(1, -1)
    r = _local_step(x[0], positions.reshape(S, 1), loss_target[0], row2(pre_norm_g), row2(q_a_norm_g), row2(kv_a_norm_g),
                    row2(post_norm_g), w_in_f, (gather_rest, assemble_rest), reduce_grads)
    own, got1, got2 = r["reduced"]
    small = jnp.concatenate([r["dg_pre"][0:1], r["dg_post"][0:1], jnp.pad(r["dg_qa"][0:1], ((0, 0), (0, D - RQ))),
                             jnp.pad(r["dg_kva"][0:1], ((0, 0), (0, D - RKV))), jnp.pad(r["loss"][0:1], ((0, 0), (0, D - 128))),
                             jnp.zeros((3, D), F32)], axis=0)
    p_small, = _all_gather([small], "all_gather_small")
    pad8 = lambda a: jnp.zeros((8, 256), F32).at[0:3, :].set(a)
    params = [(w_in, m_w_in, v_w_in), (w_q_b, m_w_q_b, v_w_q_b), (w_kv_b, m_w_kv_b, v_w_kv_b), (conv_pad, pad8(m_conv_w), pad8(v_conv_w)),
              (w_o_mla, m_w_o_mla, v_w_o_mla), (w_o_conv, m_w_o_conv, v_w_o_conv), (w_out, m_w_out, v_w_out)]
    o_q, o_kv, o_cw, o_om, o_oc, o_out = [
        _reduce_adamw(o, g1, g2, slots, w, m, v, 128, "adamw_" + nm)
        for o, g1, g2, (w, m, v), nm in list(zip(own, got1, got2, params, gnames))[1:]]
    o_cw = [a[0:3] for a in o_cw]
    o_in = [a.T for a in _reduce_adamw_t(own[0], got1[0], got2[0], slots, w_in.T, m_w_in.T, v_w_in.T, 128, "adamw_w_in")]
    padv = lambda a: jnp.pad(a.reshape(1, -1), ((0, 0), (0, D - a.shape[0])))
    stack = lambda pre, post, qa, kva: jnp.concatenate([row2(pre), row2(post), padv(qa), padv(kva), jnp.zeros((4, D), F32)], axis=0)
    o_g = _sum_adamw(p_small, stack(pre_norm_g, post_norm_g, q_a_norm_g, kv_a_norm_g),
                     stack(m_pre_norm_g, m_post_norm_g, m_q_a_norm_g, m_kv_a_norm_g),
                     stack(v_pre_norm_g, v_post_norm_g, v_q_a_norm_g, v_kv_a_norm_g), 8, "adamw_gains")
    loss = o_g[0][4, 0]
    outs = {}
    for idx, kind in enumerate(("grad", "delta", "new_m", "new_v")):
        o = o_g[idx]
        outs[kind] = dict(pre_norm_g=o[0], w_in=o_in[idx], q_a_norm_g=o[2, 0:RQ], w_q_b=o_q[idx], kv_a_norm_g=o[3, 0:RKV],
                          w_kv_b=o_kv[idx], conv_w=o_cw[idx], w_o_mla=o_om[idx], w_o_conv=o_oc[idx], w_out=o_out[idx], post_norm_g=o[1])
    names = ["pre_norm_g", "w_in", "q_a_norm_g", "w_q_b", "kv_a_norm_g", "w_kv_b", "conv_w", "w_o_mla", "w_o_conv", "w_out", "post_norm_g"]
    return (loss, r["grad_x"][None], *[outs["grad"][n] for n in names], *[outs["delta"][n] for n in names],
            *[outs["new_m"][n] for n in names], *[outs["new_v"][n] for n in names])
```
